```python
import jax
import jax.numpy as jnp
from jax import lax
import numpy as np

D_MODEL = 1024
BATCH = 8
SEQ = 16384
DEPTH = 2

GRID_W = 64
CTX_LEN = 256
N_MOD = 9
D_FF = 2816
NA_HEADS = 8
NA_HEAD_DIM = 64
NA_WIDTH = NA_HEADS * NA_HEAD_DIM
NA_KH = 8
NA_KW = 16
POOL_WINDOWS = (2, 4, 8, 16)
POOL_GROUPS = len(POOL_WINDOWS)
POOL_WIDTH = D_MODEL - NA_WIDTH
POOL_GROUP_DIM = POOL_WIDTH // POOL_GROUPS
EVEN_IN_WIDTH = 3 * NA_WIDTH + POOL_WIDTH
EVEN_MIX_WIDTH = NA_WIDTH + POOL_WIDTH
CONV_WIDTH = D_MODEL
CONV_K = 3
N_EVEN = (DEPTH + 1) // 2
N_ODD = DEPTH // 2
RMS_EPS = 1e-6
NEG_INF = -1e30

kernel_name = "hybrid_natten_pool_shortconv_dit"


def rms_norm(x, g):
    x32 = x.astype(jnp.float32)
    y = x32 * lax.rsqrt(jnp.mean(x32 * x32, axis=-1, keepdims=True) + RMS_EPS)
    return (y * g.astype(jnp.float32)).astype(x.dtype)


def modulate(h, shift, scale):
    return h * (1 + scale) + shift


def adaln(cond, w, b):
    m = jax.nn.silu(cond) @ w + b
    return m.reshape(m.shape[:-1] + (N_MOD, D_MODEL))


def swiglu(h, w13, w2):
    a, b = jnp.split(h @ w13, 2, axis=-1)
    return (jax.nn.silu(a) * b) @ w2


def ffn_sublayer(h, m, g, w13, w2, base):
    hn = modulate(rms_norm(h, g), m[:, :, base], m[:, :, base + 1])
    return h + 0.5 * m[:, :, base + 2] * swiglu(hn, w13, w2)


def split_heads(t):
    return t.reshape(t.shape[:2] + (NA_HEADS, NA_HEAD_DIM))


def neighbourhood_attention(q, k, v, k_ctx, v_ctx, rpb):
    B, L, H, dh = q.shape
    R = L // GRID_W
    kh = min(NA_KH, R)
    kw = NA_KW
    scale = dh ** -0.5
    qg = q.reshape(B, R, GRID_W, H, dh)
    kg = k.reshape(B, R, GRID_W, H, dh)
    vg = v.reshape(B, R, GRID_W, H, dh)
    r = jnp.arange(R)
    row_start = jnp.clip(r - kh // 2, 0, R - kh)
    ridx = row_start[:, None] + jnp.arange(kh)[None, :]
    k_blk = kg[:, ridx]
    v_blk = vg[:, ridx]
    col = jnp.arange(GRID_W)
    col_start = jnp.clip(col - kw // 2, 0, GRID_W - kw)
    col_ok = (col[None, :] >= col_start[:, None]) & (col[None, :] < col_start[:, None] + kw)
    ri = ridx - r[:, None] + (NA_KH - 1)
    ci = jnp.clip(col[None, :] - col[:, None] + (NA_KW - 1), 0, 2 * NA_KW - 2)
    bias = rpb[:, ri[:, None, :, None], ci[None, :, None, :]].astype(jnp.float32)
    s_win = jnp.einsum('brqhd,brkwhd->bhrqkw', qg, k_blk).astype(jnp.float32) * scale + bias
    s_win = jnp.where(col_ok[:, None, :], s_win, NEG_INF)
    n_win = kh * GRID_W
    s_win = s_win.reshape(B, H, R, GRID_W, n_win)
    s_ctx = jnp.einsum('brqhd,bchd->bhrqc', qg, k_ctx).astype(jnp.float32) * scale
    p = jax.nn.softmax(jnp.concatenate([s_win, s_ctx], axis=-1), axis=-1).astype(v.dtype)
    p_win = p[..., :n_win].reshape(B, H, R, GRID_W, kh, GRID_W)
    o = (jnp.einsum('bhrqkw,brkwhd->brqhd', p_win, v_blk)
         + jnp.einsum('bhrqc,bchd->brqhd', p[..., n_win:], v_ctx))
    return o.reshape(B, L, H * dh)


def context_attention(q, k, v):
    B, C, H, dh = q.shape
    s = jnp.einsum('bqhd,bkhd->bhqk', q, k).astype(jnp.float32) * dh ** -0.5
    p = jax.nn.softmax(s, axis=-1).astype(v.dtype)
    return jnp.einsum('bhqk,bkhd->bqhd', p, v).reshape(B, C, H * dh)


def multiscale_pool(u, pool_w, pool_scale):
    B, L, _ = u.shape
    t = jnp.arange(L)
    ug = u.reshape(B, L, POOL_GROUPS, POOL_GROUP_DIM)
    outs = []
    for g, w in enumerate(POOL_WINDOWS):
        xg = ug[:, :, g].astype(jnp.float32)
        cs = jnp.pad(jnp.cumsum(xg, axis=1), ((0, 0), (1, 0), (0, 0)))
        lo = jnp.clip(t - w // 2, 0, L)
        hi = jnp.clip(t - w // 2 + w, 0, L)
        cnt = (hi - lo).astype(jnp.float32)[None, :, None]
        mean = (jnp.take(cs, hi, axis=1) - jnp.take(cs, lo, axis=1)) / cnt
        outs.append((mean - xg).astype(u.dtype) @ pool_w[g])
    return jnp.concatenate(outs, axis=-1) * pool_scale


def even_mixer(hl, hc, w_in, w_out, rpb, pool_w, pool_scale, ctx_out):
    q, k, v, u = jnp.split(hl @ w_in, [NA_WIDTH, 2 * NA_WIDTH, 3 * NA_WIDTH], axis=-1)
    k_c, v_c = jnp.split(hc @ w_in[:, NA_WIDTH:3 * NA_WIDTH], 2, axis=-1)
    k_c, v_c = split_heads(k_c), split_heads(v_c)
    att = neighbourhood_attention(split_heads(q), split_heads(k), split_heads(v), k_c, v_c, rpb)
    pool = multiscale_pool(u, pool_w, pool_scale)
    y_lat = jnp.concatenate([att, pool], axis=-1) @ w_out
    y_ctx = None
    if ctx_out:
        q_c = split_heads(hc @ w_in[:, :NA_WIDTH])
        u_c = hc @ w_in[:, 3 * NA_WIDTH:]
        att_c = context_attention(q_c, k_c, v_c)
        pool_c = multiscale_pool(u_c, pool_w, pool_scale)
        y_ctx = jnp.concatenate([att_c, pool_c], axis=-1) @ w_out
    return y_lat, y_ctx


def short_conv_mixer(h, w_in, conv_w, w_out):
    bg, cg, xin = jnp.split(h @ w_in, 3, axis=-1)
    z = cg * xin
    L = h.shape[1]
    zp = jnp.pad(z, ((0, 0), (1, 1), (0, 0)))
    y = zp[:, 0:L] * conv_w[0] + zp[:, 1:L + 1] * conv_w[1] + zp[:, 2:L + 2] * conv_w[2]
    return (bg * y) @ w_out


def _fwd_setup_inputs(seed: int = 0) -> dict:
    key = jax.random.key(seed)
    ks = jax.random.split(key, 18)
    D = D_MODEL

    def nrm(k, shape, s):
        return jax.random.normal(k, shape, jnp.float32) * s

    return {
        "x": nrm(ks[0], (BATCH, SEQ, D), 1.0),
        "c": nrm(ks[1], (BATCH, D), 1.0),
        "ctx": nrm(ks[2], (BATCH, CTX_LEN, D), 1.0),
        "c_ctx": nrm(ks[3], (D,), 1.0),
        "mod_w": nrm(ks[4], (DEPTH, D, N_MOD * D), 0.5 * D ** -0.5),
        "mod_b": nrm(ks[5], (DEPTH, N_MOD * D), 0.01),
        "norm_g": 1.0 + nrm(ks[6], (DEPTH, 3, D), 0.02),
        "ffn_w13": nrm(ks[7], (DEPTH, 2, D, 2 * D_FF), D ** -0.5),
        "ffn_w2": nrm(ks[8], (DEPTH, 2, D_FF, D), D_FF ** -0.5),
        "even_w_in": nrm(ks[9], (N_EVEN, D, EVEN_IN_WIDTH), D ** -0.5),
        "even_w_out": nrm(ks[10], (N_EVEN, EVEN_MIX_WIDTH, D), EVEN_MIX_WIDTH ** -0.5),
        "na_rpb": nrm(ks[11], (N_EVEN, NA_HEADS, 2 * NA_KH - 1, 2 * NA_KW - 1), 0.1),
        "pool_w": nrm(ks[12], (N_EVEN, POOL_GROUPS, POOL_GROUP_DIM, POOL_GROUP_DIM), POOL_GROUP_DIM ** -0.5),
        "pool_scale": 1.0 + nrm(ks[13], (N_EVEN, POOL_WIDTH), 0.1),
        "conv_w_in": nrm(ks[14], (N_ODD, D, 3 * CONV_WIDTH), D ** -0.5),
        "conv_w": nrm(ks[15], (N_ODD, CONV_K, CONV_WIDTH), CONV_K ** -0.5),
        "conv_w_out": nrm(ks[16], (N_ODD, CONV_WIDTH, D), CONV_WIDTH ** -0.5),
        "final_g": 1.0 + nrm(ks[17], (D,), 0.02),
    }


def _fwd_reference(x, c, ctx, c_ctx, mod_w, mod_b, norm_g, ffn_w13, ffn_w2, even_w_in, even_w_out,
              na_rpb, pool_w, pool_scale, conv_w_in, conv_w, conv_w_out, final_g):
    for i in range(DEPTH):
        even = (i % 2 == 0)
        ctx_out = any(j % 2 == 0 for j in range(i + 1, DEPTH))
        ctx_here = even or ctx_out
        m_l = adaln(c[:, None, :], mod_w[i], mod_b[i])
        m_c = adaln(c_ctx[None, None, :], mod_w[i], mod_b[i])

        x = ffn_sublayer(x, m_l, norm_g[i, 0], ffn_w13[i, 0], ffn_w2[i, 0], 0)
        if ctx_here:
            ctx = ffn_sublayer(ctx, m_c, norm_g[i, 0], ffn_w13[i, 0], ffn_w2[i, 0], 0)

        xn = modulate(rms_norm(x, norm_g[i, 1]), m_l[:, :, 3], m_l[:, :, 4])
        y_c = None
        if even:
            e = i // 2
            cn = modulate(rms_norm(ctx, norm_g[i, 1]), m_c[:, :, 3], m_c[:, :, 4])
            y_l, y_c = even_mixer(xn, cn, even_w_in[e], even_w_out[e], na_rpb[e],
                                  pool_w[e], pool_scale[e], ctx_out)
        else:
            o = i // 2
            y_l = short_conv_mixer(xn, conv_w_in[o], conv_w[o], conv_w_out[o])
            if ctx_out:
                cn = modulate(rms_norm(ctx, norm_g[i, 1]), m_c[:, :, 3], m_c[:, :, 4])
                y_c = short_conv_mixer(cn, conv_w_in[o], conv_w[o], conv_w_out[o])
        x = x + m_l[:, :, 5] * y_l

        x = ffn_sublayer(x, m_l, norm_g[i, 2], ffn_w13[i, 1], ffn_w2[i, 1], 6)
        if ctx_out:
            ctx = ctx + m_c[:, :, 5] * y_c
            ctx = ffn_sublayer(ctx, m_c, norm_g[i, 2], ffn_w13[i, 1], ffn_w2[i, 1], 6)

    return rms_norm(x, final_g)


import jax as _jax
import jax.numpy as _jnp

TWIN_FORMAT = 'train_step'
FWD_PARAMS = ['x', 'c', 'ctx', 'c_ctx', 'mod_w', 'mod_b', 'norm_g', 'ffn_w13', 'ffn_w2', 'even_w_in', 'even_w_out', 'na_rpb', 'pool_w', 'pool_scale', 'conv_w_in', 'conv_w', 'conv_w_out', 'final_g']
TWIN_WEIGHTS = ['c_ctx', 'mod_w', 'mod_b', 'norm_g', 'ffn_w13', 'ffn_w2', 'even_w_in', 'even_w_out', 'na_rpb', 'pool_w', 'pool_scale', 'conv_w_in', 'conv_w', 'conv_w_out', 'final_g']
TWIN_DIFF_INPUT = 'x'
TWIN_INPUTS = ['x', 'c', 'ctx', 'c_ctx', 'mod_w', 'mod_b', 'norm_g', 'ffn_w13', 'ffn_w2', 'even_w_in', 'even_w_out', 'na_rpb', 'pool_w', 'pool_scale', 'conv_w_in', 'conv_w', 'conv_w_out', 'final_g', 'loss_target', 'm_c_ctx', 'm_mod_w', 'm_mod_b', 'm_norm_g', 'm_ffn_w13', 'm_ffn_w2', 'm_even_w_in', 'm_even_w_out', 'm_na_rpb', 'm_pool_w', 'm_pool_scale', 'm_conv_w_in', 'm_conv_w', 'm_conv_w_out', 'm_final_g', 'v_c_ctx', 'v_mod_w', 'v_mod_b', 'v_norm_g', 'v_ffn_w13', 'v_ffn_w2', 'v_even_w_in', 'v_even_w_out', 'v_na_rpb', 'v_pool_w', 'v_pool_scale', 'v_conv_w_in', 'v_conv_w', 'v_conv_w_out', 'v_final_g']
TWIN_OUTPUTS = ['loss', 'grad_x', 'grad_c_ctx', 'grad_mod_w', 'grad_mod_b', 'grad_norm_g', 'grad_ffn_w13', 'grad_ffn_w2', 'grad_even_w_in', 'grad_even_w_out', 'grad_na_rpb', 'grad_pool_w', 'grad_pool_scale', 'grad_conv_w_in', 'grad_conv_w', 'grad_conv_w_out', 'grad_final_g', 'delta_c_ctx', 'delta_mod_w', 'delta_mod_b', 'delta_norm_g', 'delta_ffn_w13', 'delta_ffn_w2', 'delta_even_w_in', 'delta_even_w_out', 'delta_na_rpb', 'delta_pool_w', 'delta_pool_scale', 'delta_conv_w_in', 'delta_conv_w', 'delta_conv_w_out', 'delta_final_g', 'new_m_c_ctx', 'new_m_mod_w', 'new_m_mod_b', 'new_m_norm_g', 'new_m_ffn_w13', 'new_m_ffn_w2', 'new_m_even_w_in', 'new_m_even_w_out', 'new_m_na_rpb', 'new_m_pool_w', 'new_m_pool_scale', 'new_m_conv_w_in', 'new_m_conv_w', 'new_m_conv_w_out', 'new_m_final_g', 'new_v_c_ctx', 'new_v_mod_w', 'new_v_mod_b', 'new_v_norm_g', 'new_v_ffn_w13', 'new_v_ffn_w2', 'new_v_even_w_in', 'new_v_even_w_out', 'new_v_na_rpb', 'new_v_pool_w', 'new_v_pool_scale', 'new_v_conv_w_in', 'new_v_conv_w', 'new_v_conv_w_out', 'new_v_final_g']
TWIN_LEAF_KINDS = {'loss': 'loss', 'grad_x': 'grad_x', 'grad_c_ctx': 'grad_w', 'grad_mod_w': 'grad_w', 'grad_mod_b': 'grad_w', 'grad_norm_g': 'grad_w', 'grad_ffn_w13': 'grad_w', 'grad_ffn_w2': 'grad_w', 'grad_even_w_in': 'grad_w', 'grad_even_w_out': 'grad_w', 'grad_na_rpb': 'grad_w', 'grad_pool_w': 'grad_w', 'grad_pool_scale': 'grad_w', 'grad_conv_w_in': 'grad_w', 'grad_conv_w': 'grad_w', 'grad_conv_w_out': 'grad_w', 'grad_final_g': 'grad_w', 'delta_c_ctx': 'delta_w', 'delta_mod_w': 'delta_w', 'delta_mod_b': 'delta_w', 'delta_norm_g': 'delta_w', 'delta_ffn_w13': 'delta_w', 'delta_ffn_w2': 'delta_w', 'delta_even_w_in': 'delta_w', 'delta_even_w_out': 'delta_w', 'delta_na_rpb': 'delta_w', 'delta_pool_w': 'delta_w', 'delta_pool_scale': 'delta_w', 'delta_conv_w_in': 'delta_w', 'delta_conv_w': 'delta_w', 'delta_conv_w_out': 'delta_w', 'delta_final_g': 'delta_w', 'new_m_c_ctx': 'new_m', 'new_m_mod_w': 'new_m', 'new_m_mod_b': 'new_m', 'new_m_norm_g': 'new_m', 'new_m_ffn_w13': 'new_m', 'new_m_ffn_w2': 'new_m', 'new_m_even_w_in': 'new_m', 'new_m_even_w_out': 'new_m', 'new_m_na_rpb': 'new_m', 'new_m_pool_w': 'new_m', 'new_m_pool_scale': 'new_m', 'new_m_conv_w_in': 'new_m', 'new_m_conv_w': 'new_m', 'new_m_conv_w_out': 'new_m', 'new_m_final_g': 'new_m', 'new_v_c_ctx': 'new_v', 'new_v_mod_w': 'new_v', 'new_v_mod_b': 'new_v', 'new_v_norm_g': 'new_v', 'new_v_ffn_w13': 'new_v', 'new_v_ffn_w2': 'new_v', 'new_v_even_w_in': 'new_v', 'new_v_even_w_out': 'new_v', 'new_v_na_rpb': 'new_v', 'new_v_pool_w': 'new_v', 'new_v_pool_scale': 'new_v', 'new_v_conv_w_in': 'new_v', 'new_v_conv_w': 'new_v', 'new_v_conv_w_out': 'new_v', 'new_v_final_g': 'new_v'}


def _forward(args):
    return _fwd_reference(*[args[k] for k in FWD_PARAMS])


def _output_shape():
    def fwd():
        inp = _fwd_setup_inputs(0)
        return _fwd_reference(*[inp[k] for k in FWD_PARAMS])
    out = _jax.eval_shape(fwd)
    return out.shape, out.dtype

N_MICROBATCH = 1
ADAM_LR = 0.001
ADAM_B1 = 0.9
ADAM_B2 = 0.999
ADAM_EPS = 1e-08
ADAM_WD = 0.01
ADAM_STEP = 10
PER_EXAMPLE_BATCH_AXIS = {'x': 0, 'c': 0, 'ctx': 0, 'loss_target': 0}
SHARED_INPUTS = []
_WEIGHT_DTYPES = {'c_ctx': _jnp.float32, 'mod_w': _jnp.float32, 'mod_b': _jnp.float32, 'norm_g': _jnp.float32, 'ffn_w13': _jnp.float32, 'ffn_w2': _jnp.float32, 'even_w_in': _jnp.float32, 'even_w_out': _jnp.float32, 'na_rpb': _jnp.float32, 'pool_w': _jnp.float32, 'pool_scale': _jnp.float32, 'conv_w_in': _jnp.float32, 'conv_w': _jnp.float32, 'conv_w_out': _jnp.float32, 'final_g': _jnp.float32}
MOMENT_SCALE = {'c_ctx': 1.862481e-02, 'mod_w': 1.038329e-01, 'mod_b': 1.846279e-01, 'norm_g': 1.101147e-01, 'ffn_w13': 2.464504e-02, 'ffn_w2': 4.011548e-02, 'even_w_in': 5.638513e-02, 'even_w_out': 7.853291e-02, 'na_rpb': 3.411440e-03, 'pool_w': 1.060419e-01, 'pool_scale': 1.075788e-01, 'conv_w_in': 1.246121e-01, 'conv_w': 1.289207e-01, 'conv_w_out': 1.246318e-01, 'final_g': 1.284136e+02}


def _to_microbatches(a, axis):
    t = _jnp.moveaxis(a, axis, 0)
    t = t.reshape((N_MICROBATCH, t.shape[0] // N_MICROBATCH) + t.shape[1:])
    return _jnp.moveaxis(t, 1, axis + 1)


def setup_inputs(seed: int = 0) -> dict:
    inp = _fwd_setup_inputs(seed)
    key = _jax.random.fold_in(_jax.random.key(seed), 7919)
    shape, _ = _output_shape()
    out = dict(inp)
    out["loss_target"] = _jax.random.normal(_jax.random.fold_in(key, 0), shape, _jnp.float32)
    for i, name in enumerate(TWIN_WEIGHTS):
        w = inp[name].astype(_jnp.float32)
        if MOMENT_SCALE is None:
            s = _jnp.sqrt(_jnp.mean(_jnp.square(w)) + 1e-30)
        else:
            s = MOMENT_SCALE[name]
        km, kv = _jax.random.split(_jax.random.fold_in(key, i + 1))
        out[name] = w
        out["m_" + name] = s * _jax.random.normal(km, w.shape, _jnp.float32)
        out["v_" + name] = (s * s) * _jax.random.uniform(kv, w.shape, _jnp.float32, 0.5, 1.5)
    if N_MICROBATCH > 1:
        for name, axis in PER_EXAMPLE_BATCH_AXIS.items():
            out[name] = _to_microbatches(out[name], axis)
    return {'x': out['x'], 'c': out['c'], 'ctx': out['ctx'], 'c_ctx': out['c_ctx'], 'mod_w': out['mod_w'], 'mod_b': out['mod_b'], 'norm_g': out['norm_g'], 'ffn_w13': out['ffn_w13'], 'ffn_w2': out['ffn_w2'], 'even_w_in': out['even_w_in'], 'even_w_out': out['even_w_out'], 'na_rpb': out['na_rpb'], 'pool_w': out['pool_w'], 'pool_scale': out['pool_scale'], 'conv_w_in': out['conv_w_in'], 'conv_w': out['conv_w'], 'conv_w_out': out['conv_w_out'], 'final_g': out['final_g'], 'loss_target': out['loss_target'], 'm_c_ctx': out['m_c_ctx'], 'm_mod_w': out['m_mod_w'], 'm_mod_b': out['m_mod_b'], 'm_norm_g': out['m_norm_g'], 'm_ffn_w13': out['m_ffn_w13'], 'm_ffn_w2': out['m_ffn_w2'], 'm_even_w_in': out['m_even_w_in'], 'm_even_w_out': out['m_even_w_out'], 'm_na_rpb': out['m_na_rpb'], 'm_pool_w': out['m_pool_w'], 'm_pool_scale': out['m_pool_scale'], 'm_conv_w_in': out['m_conv_w_in'], 'm_conv_w': out['m_conv_w'], 'm_conv_w_out': out['m_conv_w_out'], 'm_final_g': out['m_final_g'], 'v_c_ctx': out['v_c_ctx'], 'v_mod_w': out['v_mod_w'], 'v_mod_b': out['v_mod_b'], 'v_norm_g': out['v_norm_g'], 'v_ffn_w13': out['v_ffn_w13'], 'v_ffn_w2': out['v_ffn_w2'], 'v_even_w_in': out['v_even_w_in'], 'v_even_w_out': out['v_even_w_out'], 'v_na_rpb': out['v_na_rpb'], 'v_pool_w': out['v_pool_w'], 'v_pool_scale': out['v_pool_scale'], 'v_conv_w_in': out['v_conv_w_in'], 'v_conv_w': out['v_conv_w'], 'v_conv_w_out': out['v_conv_w_out'], 'v_final_g': out['v_final_g']}


def _loss(weights, diff, rest, loss_target):
    with _jax.named_scope("forward"):
        args = {**rest, TWIN_DIFF_INPUT: diff, **{k: w.astype(_WEIGHT_DTYPES[k]) for k, w in weights.items()}}
        y = _forward(args)
    with _jax.named_scope("loss_head"):
        err = _jnp.square(y.astype(_jnp.float32) - loss_target)
        return 0.5 * _jnp.sum(_jnp.mean(err, axis=-1)) if err.ndim else 0.5 * err


def _adamw(w, g, m, v):
    m = ADAM_B1 * m + (1.0 - ADAM_B1) * g
    v = ADAM_B2 * v + (1.0 - ADAM_B2) * _jnp.square(g)
    m_hat = m / (1.0 - ADAM_B1 ** ADAM_STEP)
    v_hat = v / (1.0 - ADAM_B2 ** ADAM_STEP)
    delta = -ADAM_LR * (m_hat / (_jnp.sqrt(v_hat) + ADAM_EPS) + ADAM_WD * w)
    return delta, m, v


def reference(x, c, ctx, c_ctx, mod_w, mod_b, norm_g, ffn_w13, ffn_w2, even_w_in, even_w_out, na_rpb, pool_w, pool_scale, conv_w_in, conv_w, conv_w_out, final_g, loss_target, m_c_ctx, m_mod_w, m_mod_b, m_norm_g, m_ffn_w13, m_ffn_w2, m_even_w_in, m_even_w_out, m_na_rpb, m_pool_w, m_pool_scale, m_conv_w_in, m_conv_w, m_conv_w_out, m_final_g, v_c_ctx, v_mod_w, v_mod_b, v_norm_g, v_ffn_w13, v_ffn_w2, v_even_w_in, v_even_w_out, v_na_rpb, v_pool_w, v_pool_scale, v_conv_w_in, v_conv_w, v_conv_w_out, v_final_g):
    given = dict(x=x, c=c, ctx=ctx, c_ctx=c_ctx, mod_w=mod_w, mod_b=mod_b, norm_g=norm_g, ffn_w13=ffn_w13, ffn_w2=ffn_w2, even_w_in=even_w_in, even_w_out=even_w_out, na_rpb=na_rpb, pool_w=pool_w, pool_scale=pool_scale, conv_w_in=conv_w_in, conv_w=conv_w, conv_w_out=conv_w_out, final_g=final_g, loss_target=loss_target, m_c_ctx=m_c_ctx, m_mod_w=m_mod_w, m_mod_b=m_mod_b, m_norm_g=m_norm_g, m_ffn_w13=m_ffn_w13, m_ffn_w2=m_ffn_w2, m_even_w_in=m_even_w_in, m_even_w_out=m_even_w_out, m_na_rpb=m_na_rpb, m_pool_w=m_pool_w, m_pool_scale=m_pool_scale, m_conv_w_in=m_conv_w_in, m_conv_w=m_conv_w, m_conv_w_out=m_conv_w_out, m_final_g=m_final_g, v_c_ctx=v_c_ctx, v_mod_w=v_mod_w, v_mod_b=v_mod_b, v_norm_g=v_norm_g, v_ffn_w13=v_ffn_w13, v_ffn_w2=v_ffn_w2, v_even_w_in=v_even_w_in, v_even_w_out=v_even_w_out, v_na_rpb=v_na_rpb, v_pool_w=v_pool_w, v_pool_scale=v_pool_scale, v_conv_w_in=v_conv_w_in, v_conv_w=v_conv_w, v_conv_w_out=v_conv_w_out, v_final_g=v_final_g)
    weights = {n: given[n] for n in TWIN_WEIGHTS}
    shared = {n: given[n] for n in SHARED_INPUTS}
    per_example = {n: given[n] for n in ['x', 'c', 'ctx']}
    grad_fn = _jax.value_and_grad(_loss, argnums=(0, 1))

    def one_microbatch(ex, loss_target):
        ex = dict(ex)
        diff = ex.pop(TWIN_DIFF_INPUT)
        return grad_fn(weights, diff, {**shared, **ex}, loss_target)

    if N_MICROBATCH == 1:
        loss, (grad_w, grad_x) = one_microbatch(per_example, given["loss_target"])
    else:
        def body(carry, xs):
            loss_sum, grad_sum = carry
            l_k, (gw_k, gx_k) = one_microbatch(xs[0], xs[1])
            with _jax.named_scope("update"):
                return (loss_sum + l_k, _jax.tree.map(_jnp.add, grad_sum, gw_k)), gx_k

        init = (_jnp.zeros((), _jnp.float32), _jax.tree.map(_jnp.zeros_like, weights))
        (loss, grad_w), grad_x = _jax.lax.scan(body, init, (per_example, given["loss_target"]))
    with _jax.named_scope("update"):
        delta_w, new_m, new_v = {}, {}, {}
        for n in TWIN_WEIGHTS:
            delta_w[n], new_m[n], new_v[n] = _adamw(weights[n], grad_w[n], given["m_" + n], given["v_" + n])
    return (loss, grad_x, *[grad_w[n] for n in TWIN_WEIGHTS], *[delta_w[n] for n in TWIN_WEIGHTS],
            *[new_m[n] for n in TWIN_WEIGHTS], *[new_v[n] for n in TWIN_WEIGHTS])
```

```python
import functools

import numpy as np
import jax
import jax.numpy as jnp
from jax import lax
from jax.experimental import pallas as pl
from jax.experimental.pallas import tpu as pltpu

D = 1024
FF = 2816
SEQ = 16384
CTX = 256
GRID_W = 64
N_MOD = 9
HEADS = 8
HEAD_DIM = 64
NA_W = 512
POOL_W = 512
POOL_G = 128
POOL_WINDOWS = (2, 4, 8, 16)
KH = 8
KW = 16
RMS_EPS = 1e-6
NEG_INF = -1e30
N_DEV = 8

ADAM_LR = 0.001
ADAM_B1 = 0.9
ADAM_B2 = 0.999
ADAM_EPS = 1e-08
ADAM_WD = 0.01
ADAM_STEP = 10

VMEM_LIMIT = 52 * 1024 * 1024
HALO = 16
QROWS = 8
WROWS = 24

BF16 = jnp.bfloat16
F32 = jnp.float32
MESH_ID = pl.DeviceIdType.MESH
HI = lax.Precision.HIGHEST

NT_DIMS = (((1,), (1,)), ((), ()))
TN_DIMS = (((0,), (0,)), ((), ()))


def _tile(n, cands):
    for c in cands:
        if n % c == 0:
            return c
    return n


def _params(sem):
    return pltpu.CompilerParams(dimension_semantics=sem, vmem_limit_bytes=VMEM_LIMIT)


def _dot(a, b):
    return jnp.dot(a, b, preferred_element_type=F32)


def _dot_nt(a, b):
    return lax.dot_general(a, b, NT_DIMS, preferred_element_type=F32)


def _dot_tn(a, b):
    return lax.dot_general(a, b, TN_DIMS, preferred_element_type=F32)


def _sigmoid(x):
    return 1.0 / (1.0 + jnp.exp(-x))


def normmod(h, g, mods, i_shift, i_scale, name):
    n = h.shape[0]
    te = _tile(n, (512, 256))

    def body(h_ref, g_ref, m_ref, o_ref):
        x = h_ref[...]
        r = lax.rsqrt(jnp.mean(x * x, axis=-1, keepdims=True) + RMS_EPS)
        y = x * r * g_ref[...]
        o_ref[...] = (y * (1.0 + m_ref[i_scale:i_scale + 1, :]) + m_ref[i_shift:i_shift + 1, :]).astype(BF16)

    return pl.pallas_call(
        body, name=name, grid=(n // te,),
        in_specs=[pl.BlockSpec((te, D), lambda i: (i, 0)),
                  pl.BlockSpec((1, D), lambda i: (0, 0)),
                  pl.BlockSpec((N_MOD, D), lambda i: (0, 0))],
        out_specs=pl.BlockSpec((te, D), lambda i: (i, 0)),
        out_shape=jax.ShapeDtypeStruct((n, D), BF16),
        compiler_params=_params(("parallel",)),
    )(h, g, mods)


def normmod_bwd(dhn, h, g, mods, i_scale, dres, name):
    n = h.shape[0]
    te = _tile(n, (256,))

    def body(dhn_ref, h_ref, g_ref, m_ref, dres_ref, dh_ref, dshift_ref, dscale_ref, dg_ref):
        i = pl.program_id(0)

        @pl.when(i == 0)
        def _():
            dshift_ref[...] = jnp.zeros_like(dshift_ref)
            dscale_ref[...] = jnp.zeros_like(dscale_ref)
            dg_ref[...] = jnp.zeros_like(dg_ref)

        x = h_ref[...]
        d = dhn_ref[...]
        gv = g_ref[...]
        r = lax.rsqrt(jnp.mean(x * x, axis=-1, keepdims=True) + RMS_EPS)
        xhat = x * r
        dshift_ref[...] += jnp.sum(d, axis=0, keepdims=True)
        dscale_ref[...] += jnp.sum(d * (xhat * gv), axis=0, keepdims=True)
        dn = d * (1.0 + m_ref[i_scale:i_scale + 1, :])
        dg_ref[...] += jnp.sum(dn * xhat, axis=0, keepdims=True)
        dxhat = dn * gv
        dh_ref[...] = dres_ref[...] + r * (dxhat - xhat * jnp.mean(dxhat * xhat, axis=-1, keepdims=True))

    row = pl.BlockSpec((te, D), lambda i: (i, 0))
    vec = pl.BlockSpec((1, D), lambda i: (0, 0))
    return pl.pallas_call(
        body, name=name, grid=(n // te,),
        in_specs=[row, row, vec, pl.BlockSpec((N_MOD, D), lambda i: (0, 0)), row],
        out_specs=[row, vec, vec, vec],
        out_shape=[jax.ShapeDtypeStruct((n, D), F32)] + [jax.ShapeDtypeStruct((1, D), F32)] * 3,
        compiler_params=_params(("arbitrary",)),
    )(dhn, h, g, mods, dres)


def gate_bwd(dout, y, mods, i_gate, coef, name):
    n = dout.shape[0]
    te = _tile(n, (256,))

    def body(d_ref, y_ref, m_ref, dy_ref, dg_ref):
        @pl.when(pl.program_id(0) == 0)
        def _():
            dg_ref[...] = jnp.zeros_like(dg_ref)

        d = d_ref[...]
        dy_ref[...] = (d * (coef * m_ref[i_gate:i_gate + 1, :])).astype(BF16)
        dg_ref[...] += coef * jnp.sum(d * y_ref[...].astype(F32), axis=0, keepdims=True)

    row = pl.BlockSpec((te, D), lambda i: (i, 0))
    return pl.pallas_call(
        body, name=name, grid=(n // te,),
        in_specs=[row, row, pl.BlockSpec((N_MOD, D), lambda i: (0, 0))],
        out_specs=[row, pl.BlockSpec((1, D), lambda i: (0, 0))],
        out_shape=[jax.ShapeDtypeStruct((n, D), BF16), jax.ShapeDtypeStruct((1, D), F32)],
        compiler_params=_params(("arbitrary",)),
    )(dout, y, mods)


def loss_head(x, g, target, name):
    n = x.shape[0]
    te = _tile(n, (256,))

    def body(x_ref, g_ref, t_ref, dx_ref, loss_ref, dg_ref):
        @pl.when(pl.program_id(0) == 0)
        def _():
            loss_ref[...] = jnp.zeros_like(loss_ref)
            dg_ref[...] = jnp.zeros_like(dg_ref)

        xv = x_ref[...]
        gv = g_ref[...]
        r = lax.rsqrt(jnp.mean(xv * xv, axis=-1, keepdims=True) + RMS_EPS)
        xhat = xv * r
        e = xhat * gv - t_ref[...]
        per_tok = jnp.mean(e * e, axis=-1, keepdims=True)
        loss_ref[...] += 0.5 * jnp.sum(per_tok, axis=0, keepdims=True)
        dy = e * (1.0 / D)
        dg_ref[...] += jnp.sum(dy * xhat, axis=0, keepdims=True)
        dxhat = dy * gv
        dx_ref[...] = r * (dxhat - xhat * jnp.mean(dxhat * xhat, axis=-1, keepdims=True))

    row = pl.BlockSpec((te, D), lambda i: (i, 0))
    vec = pl.BlockSpec((1, D), lambda i: (0, 0))
    return pl.pallas_call(
        body, name=name, grid=(n // te,),
        in_specs=[row, vec, row],
        out_specs=[row, pl.BlockSpec((1, 128), lambda i: (0, 0)), vec],
        out_shape=[jax.ShapeDtypeStruct((n, D), F32), jax.ShapeDtypeStruct((1, 128), F32),
                   jax.ShapeDtypeStruct((1, D), F32)],
        compiler_params=_params(("arbitrary",)),
    )(x, g, target)


def ffn_up(hn, w13, name):
    n = hn.shape[0]
    ff = w13.shape[1] // 2
    tm = _tile(n, (512, 256))
    tn = _tile(ff, (1408, 512, 256, 128))
    nj = ff // tn

    def body(h_ref, wa_ref, wb_ref, a_ref, b_ref, s_ref):
        hv = h_ref[...]
        a = _dot(hv, wa_ref[...])
        b = _dot(hv, wb_ref[...])
        a_ref[...] = a.astype(BF16)
        b_ref[...] = b.astype(BF16)
        s_ref[...] = (a * _sigmoid(a) * b).astype(BF16)

    out = pl.BlockSpec((tm, tn), lambda j, i: (i, j))
    return pl.pallas_call(
        body, name=name, grid=(nj, n // tm),
        in_specs=[pl.BlockSpec((tm, D), lambda j, i: (i, 0)),
                  pl.BlockSpec((D, tn), lambda j, i: (0, j)),
                  pl.BlockSpec((D, tn), lambda j, i: (0, j + nj))],
        out_specs=[out, out, out],
        out_shape=[jax.ShapeDtypeStruct((n, ff), BF16)] * 3,
        compiler_params=_params(("parallel", "parallel")),
    )(hn, w13, w13)


def mm_nn(a_list, w, row_offs, name, out_dtype=BF16, res=None):
    n = a_list[0].shape[0]
    nout = w.shape[1]
    ks = [a.shape[1] for a in a_list]
    tm = _tile(n, (512, 256))
    tn = _tile(nout, (1024, 512, 256, 128))
    na = len(a_list)

    def body(*refs):
        a_refs = refs[:na]
        w_refs = refs[na:2 * na]
        acc = _dot(a_refs[0][...], w_refs[0][...])
        for k in range(1, na):
            acc += _dot(a_refs[k][...], w_refs[k][...])
        if res is None:
            refs[2 * na][...] = acc.astype(out_dtype)
        else:
            h_ref, m_ref, hn_ref, y_ref = refs[2 * na:]
            i_gate, coef = res[2], res[3]
            hn_ref[...] = h_ref[...] + (coef * m_ref[i_gate:i_gate + 1, :]) * acc
            y_ref[...] = acc.astype(BF16)

    in_specs = [pl.BlockSpec((tm, k), lambda j, i: (i, 0)) for k in ks]
    for k, off in zip(ks, row_offs):
        in_specs.append(pl.BlockSpec((k, tn), functools.partial(lambda j, i, ob: (ob, j), ob=off // k)))
    args = list(a_list) + [w] * na
    out = pl.BlockSpec((tm, tn), lambda j, i: (i, j))
    if res is None:
        out_specs = out
        out_shape = jax.ShapeDtypeStruct((n, nout), out_dtype)
    else:
        in_specs += [out, pl.BlockSpec((N_MOD, tn), lambda j, i: (0, j))]
        args += [res[0], res[1]]
        out_specs = [out, out]
        out_shape = [jax.ShapeDtypeStruct((n, nout), F32), jax.ShapeDtypeStruct((n, nout), BF16)]
    return pl.pallas_call(
        body, name=name, grid=(nout // tn, n // tm),
        in_specs=in_specs, out_specs=out_specs, out_shape=out_shape,
        compiler_params=_params(("parallel", "parallel")),
    )(*args)


def mm_nt(g_list, w, col_offs, name, out_dtype=F32, dswiglu=None):
    n = g_list[0].shape[0]
    nout = w.shape[0]
    kg = g_list[0].shape[1]
    tm = _tile(n, (512, 256))
    tn = _tile(nout, (1408, 1024, 512, 256, 128))
    tk = _tile(kg, (1408, 1024, 512, 256, 128))
    ng = len(g_list)
    nk = kg // tk

    def body(*refs):
        g_refs = refs[:ng]
        w_refs = refs[ng:2 * ng]
        rest = refs[2 * ng:]
        acc_ref = rest[-1]
        k = pl.program_id(2)

        @pl.when(k == 0)
        def _():
            acc_ref[...] = jnp.zeros_like(acc_ref)

        acc = _dot_nt(g_refs[0][...], w_refs[0][...])
        for q in range(1, ng):
            acc += _dot_nt(g_refs[q][...], w_refs[q][...])
        acc_ref[...] += acc

        @pl.when(k == nk - 1)
        def _():
            r = acc_ref[...]
            if dswiglu is None:
                rest[0][...] = r.astype(out_dtype)
            else:
                a_ref, b_ref, da_ref, db_ref = rest[:4]
                a = a_ref[...].astype(F32)
                sig = _sigmoid(a)
                da_ref[...] = (r * b_ref[...].astype(F32) * (sig * (1.0 + a * (1.0 - sig)))).astype(BF16)
                db_ref[...] = (r * (a * sig)).astype(BF16)

    in_specs = [pl.BlockSpec((tm, tk), lambda j, i, k: (i, k)) for _ in g_list]
    for off in col_offs:
        in_specs.append(pl.BlockSpec((tn, tk), functools.partial(lambda j, i, k, ob: (j, ob + k), ob=off // tk)))
    args = list(g_list) + [w] * ng
    out = pl.BlockSpec((tm, tn), lambda j, i, k: (i, j))
    if dswiglu is None:
        out_specs = out
        out_shape = jax.ShapeDtypeStruct((n, nout), out_dtype)
    else:
        in_specs += [out, out]
        args += list(dswiglu)
        out_specs = [out, out]
        out_shape = [jax.ShapeDtypeStruct((n, nout), BF16)] * 2
    return pl.pallas_call(
        body, name=name, grid=(nout // tn, n // tm, nk),
        in_specs=in_specs, out_specs=out_specs, out_shape=out_shape,
        scratch_shapes=[pltpu.VMEM((tm, tn), F32)],
        compiler_params=_params(("parallel", "parallel", "arbitrary")),
    )(*args)


def mm_tn(a, g, name, acc=None):
    n, ka = a.shape
    ngc = g.shape[1]
    tka = _tile(ka, (1408, 1024, 512, 256, 128))
    tng = _tile(ngc, (1408, 1024, 512, 256, 128))
    tr = _tile(n, (512, 256))
    has_acc = acc is not None

    def body(*refs):
        a_ref, g_ref = refs[0], refs[1]
        o_ref = refs[-1]
        r = pl.program_id(2)

        @pl.when(r == 0)
        def _():
            if has_acc:
                o_ref[...] = refs[2][...]
            else:
                o_ref[...] = jnp.zeros_like(o_ref)

        o_ref[...] += _dot_tn(a_ref[...], g_ref[...])

    out = pl.BlockSpec((tka, tng), lambda p, q, r: (p, q))
    in_specs = [pl.BlockSpec((tr, tka), lambda p, q, r: (r, p)),
                pl.BlockSpec((tr, tng), lambda p, q, r: (r, q))]
    args = [a, g]
    if has_acc:
        in_specs.append(out)
        args.append(acc)
    return pl.pallas_call(
        body, name=name, grid=(ka // tka, ngc // tng, n // tr),
        in_specs=in_specs, out_specs=out,
        out_shape=jax.ShapeDtypeStruct((ka, ngc), F32),
        compiler_params=_params(("parallel", "parallel", "arbitrary")),
    )(*args)


def mm_small(a, b, name, trans_b=False):
    m = a.shape[0]
    nout = b.shape[0] if trans_b else b.shape[1]

    def body(a_ref, b_ref, o_ref):
        if trans_b:
            o_ref[...] = lax.dot_general(a_ref[...], b_ref[...], NT_DIMS, precision=HI, preferred_element_type=F32)
        else:
            o_ref[...] = jnp.dot(a_ref[...], b_ref[...], precision=HI, preferred_element_type=F32)

    return pl.pallas_call(
        body, name=name,
        out_shape=jax.ShapeDtypeStruct((m, nout), F32),
        compiler_params=pltpu.CompilerParams(vmem_limit_bytes=VMEM_LIMIT),
    )(a, b)


def _col_tables():
    col = np.arange(GRID_W)
    start = np.clip(col - KW // 2, 0, GRID_W - KW)
    ok = (col[None, :] >= start[:, None]) & (col[None, :] < start[:, None] + KW)
    ci = np.clip(col[None, :] - col[:, None] + (KW - 1), 0, 2 * KW - 2)
    e = np.zeros((2 * KW - 1, GRID_W, GRID_W), np.float32)
    for c in range(2 * KW - 1):
        e[c] = (ci == c) & ok
    return e.reshape(2 * KW - 1, GRID_W * GRID_W), ok


def bias_table(rpb):
    e, ok = _col_tables()
    e_pad = np.zeros((32, GRID_W * GRID_W), np.float32)
    e_pad[:31] = e
    rp = jnp.pad(rpb.reshape(HEADS * 15, 31), ((0, 0), (0, 1)))
    t = mm_small(rp, jnp.asarray(e_pad), "rpb_expand").reshape(HEADS, 15, GRID_W, GRID_W)
    t = jnp.where(jnp.asarray(ok)[None, None], t, NEG_INF)
    tab = jnp.stack([t[:, v:v + KH] for v in range(8)], axis=0)
    return tab.transpose(0, 1, 3, 2, 4).reshape(8, HEADS, GRID_W, KH * GRID_W)


def bias_table_bwd(dtab):
    e, _ = _col_tables()
    e_pad = np.zeros((128, GRID_W * GRID_W), np.float32)
    e_pad[:31] = e
    d = dtab.reshape(8, HEADS, GRID_W, KH, GRID_W).transpose(0, 1, 3, 2, 4).reshape(8 * HEADS * KH, GRID_W * GRID_W)
    gv = mm_small(d, jnp.asarray(e_pad), "rpb_reduce", trans_b=True)[:, :31]
    gv = gv.reshape(8, HEADS, KH, 31).transpose(0, 2, 1, 3).reshape(8 * KH, HEADS * 31)
    sel = np.zeros((16, 8 * KH), np.float32)
    for v in range(8):
        for j in range(KH):
            sel[v + j, v * KH + j] = 1.0
    gpad = jnp.pad(gv, ((0, 0), (0, 256 - HEADS * 31)))
    out = mm_small(jnp.asarray(sel), gpad, "rpb_fold")[:15, :HEADS * 31]
    return out.reshape(15, HEADS, 31).transpose(1, 0, 2)


def _attn_geometry(seq):
    rows = seq // GRID_W
    nb = rows // QROWS
    return rows, nb


def _lane_half(e):
    return (lax.broadcasted_iota(jnp.int32, (1, 128), 1) // HEAD_DIM) == e


def attn_fwd(qkvu, qkvu_c, tab, name):
    seq = qkvu.shape[0]
    nctx = qkvu_c.shape[0]
    rows, nb = _attn_geometry(seq)
    qt = QROWS * GRID_W
    wt = WROWS * GRID_W
    scale = HEAD_DIM ** -0.5

    def wb0(i):
        return jnp.clip(i - 1, 0, nb - 3)

    def body(q_ref, k0, k1, k2, v0, v1, v2, kc_ref, vc_ref, tab_hbm, o_ref, kbuf, vbuf, tab_s, sem):
        i = pl.program_id(0)

        @pl.when(i == 0)
        def _():
            cp = pltpu.make_async_copy(tab_hbm, tab_s, sem)
            cp.start()
            cp.wait()

        for t, (kr, vr) in enumerate(((k0, v0), (k1, v1), (k2, v2))):
            kbuf[t * qt:(t + 1) * qt, :] = kr[...]
            vbuf[t * qt:(t + 1) * qt, :] = vr[...]
        base = wb0(i) * QROWS

        def row_body(rl, carry):
            r = i * QROWS + rl
            rs = jnp.clip(r - KH // 2, 0, rows - KH)
            vi = rs - r + (KH - 1)
            off = pl.multiple_of((rs - base) * GRID_W, GRID_W)
            qoff = pl.multiple_of(rl * GRID_W, GRID_W)
            for p in range(HEADS // 2):
                ls = slice(p * 128, (p + 1) * 128)
                q2 = q_ref[pl.ds(qoff, GRID_W), ls]
                k2v = kbuf[pl.ds(off, KH * GRID_W), ls]
                v2v = vbuf[pl.ds(off, KH * GRID_W), ls]
                kc2 = kc_ref[:, ls]
                vc2 = vc_ref[:, ls]
                acc = jnp.zeros((GRID_W, 128), F32)
                for e in range(2):
                    me = _lane_half(e)
                    qm = jnp.where(me, q2, jnp.zeros_like(q2))
                    s_w = _dot_nt(qm, k2v) * scale + tab_s[vi, 2 * p + e]
                    s_c = _dot_nt(qm, kc2) * scale
                    m = jnp.maximum(jnp.max(s_w, axis=-1, keepdims=True), jnp.max(s_c, axis=-1, keepdims=True))
                    pw = jnp.exp(s_w - m)
                    pc = jnp.exp(s_c - m)
                    l = jnp.sum(pw, axis=-1, keepdims=True) + jnp.sum(pc, axis=-1, keepdims=True)
                    o = _dot(pw.astype(BF16), v2v) + _dot(pc.astype(BF16), vc2)
                    acc = jnp.where(me, o * (1.0 / l), acc)
                o_ref[pl.ds(qoff, GRID_W), ls] = acc.astype(BF16)
            return carry

        lax.fori_loop(0, QROWS, row_body, 0)

    blk = lambda col: [pl.BlockSpec((qt, NA_W), functools.partial(lambda i, t, c: (wb0(i) + t, c), t=t, c=col))
                       for t in range(3)]
    return pl.pallas_call(
        body, name=name, grid=(nb,),
        in_specs=[pl.BlockSpec((qt, NA_W), lambda i: (i, 0))] + blk(1) + blk(2)
                 + [pl.BlockSpec((nctx, NA_W), lambda i: (0, 1)), pl.BlockSpec((nctx, NA_W), lambda i: (0, 2)),
                    pl.BlockSpec(memory_space=pl.ANY)],
        out_specs=pl.BlockSpec((qt, NA_W), lambda i: (i, 0)),
        out_shape=jax.ShapeDtypeStruct((seq, NA_W), BF16),
        scratch_shapes=[pltpu.VMEM((wt, NA_W), BF16), pltpu.VMEM((wt, NA_W), BF16),
                        pltpu.VMEM((8, HEADS, GRID_W, KH * GRID_W), F32), pltpu.SemaphoreType.DMA],
        compiler_params=_params(("arbitrary",)),
    )(qkvu, qkvu, qkvu, qkvu, qkvu, qkvu, qkvu, qkvu_c, qkvu_c, tab)


def attn_bwd(qkvu, qkvu_c, tab, dmix, name):
    seq = qkvu.shape[0]
    nctx = qkvu_c.shape[0]
    rows, nb = _attn_geometry(seq)
    qt = QROWS * GRID_W
    wt = WROWS * GRID_W
    scale = HEAD_DIM ** -0.5

    def wb0(i):
        return jnp.clip(i - 1, 0, nb - 3)

    def body(q_ref, k0, k1, k2, v0, v1, v2, kc_ref, vc_ref, do_ref, tab_hbm,
             dq_ref, dk_hbm, dv_hbm, dkc_ref, dvc_ref, dtab_hbm,
             kbuf, vbuf, dkacc, dvacc, tab_s, dtab_s, sem):
        i = pl.program_id(0)

        def flush(src, dst, src_row, dst_row, nrows):
            cp = pltpu.make_async_copy(src.at[pl.ds(src_row, nrows)], dst.at[pl.ds(dst_row, nrows)], sem)
            cp.start()
            cp.wait()

        @pl.when(i == 0)
        def _():
            cp = pltpu.make_async_copy(tab_hbm, tab_s, sem)
            cp.start()
            cp.wait()
            dtab_s[...] = jnp.zeros_like(dtab_s)
            dkacc[...] = jnp.zeros_like(dkacc)
            dvacc[...] = jnp.zeros_like(dvacc)
            dkc_ref[...] = jnp.zeros_like(dkc_ref)
            dvc_ref[...] = jnp.zeros_like(dvc_ref)

        @pl.when((i >= 2) & (i <= nb - 2))
        def _():
            dst_row = pl.multiple_of((i - 2) * qt, qt)
            for acc_ref, dst in ((dkacc, dk_hbm), (dvacc, dv_hbm)):
                flush(acc_ref, dst, 0, dst_row, qt)
                acc_ref[0:qt, :] = acc_ref[qt:2 * qt, :]
                acc_ref[qt:2 * qt, :] = acc_ref[2 * qt:3 * qt, :]
                acc_ref[2 * qt:3 * qt, :] = jnp.zeros((qt, NA_W), F32)

        for t, (kr, vr) in enumerate(((k0, v0), (k1, v1), (k2, v2))):
            kbuf[t * qt:(t + 1) * qt, :] = kr[...]
            vbuf[t * qt:(t + 1) * qt, :] = vr[...]
        base = wb0(i) * QROWS

        def row_body(rl, carry):
            r = i * QROWS + rl
            rs = jnp.clip(r - KH // 2, 0, rows - KH)
            vi = rs - r + (KH - 1)
            off = pl.multiple_of((rs - base) * GRID_W, GRID_W)
            qoff = pl.multiple_of(rl * GRID_W, GRID_W)
            for p in range(HEADS // 2):
                ls = slice(p * 128, (p + 1) * 128)
                q2 = q_ref[pl.ds(qoff, GRID_W), ls]
                do2 = do_ref[pl.ds(qoff, GRID_W), ls]
                k2v = kbuf[pl.ds(off, KH * GRID_W), ls]
                v2v = vbuf[pl.ds(off, KH * GRID_W), ls]
                kc2 = kc_ref[:, ls]
                vc2 = vc_ref[:, ls]
                dq_acc = jnp.zeros((GRID_W, 128), F32)
                dk_c = jnp.zeros((KH * GRID_W, 128), F32)
                dv_c = jnp.zeros((KH * GRID_W, 128), F32)
                dkc_c = jnp.zeros((nctx, 128), F32)
                dvc_c = jnp.zeros((nctx, 128), F32)
                for e in range(2):
                    me = _lane_half(e)
                    qm = jnp.where(me, q2, jnp.zeros_like(q2))
                    dom = jnp.where(me, do2, jnp.zeros_like(do2))
                    s_w = _dot_nt(qm, k2v) * scale + tab_s[vi, 2 * p + e]
                    s_c = _dot_nt(qm, kc2) * scale
                    m = jnp.maximum(jnp.max(s_w, axis=-1, keepdims=True), jnp.max(s_c, axis=-1, keepdims=True))
                    pw = jnp.exp(s_w - m)
                    pc = jnp.exp(s_c - m)
                    inv = 1.0 / (jnp.sum(pw, axis=-1, keepdims=True) + jnp.sum(pc, axis=-1, keepdims=True))
                    pw = pw * inv
                    pc = pc * inv
                    dpw = _dot_nt(dom, v2v)
                    dpc = _dot_nt(dom, vc2)
                    delta = jnp.sum(pw * dpw, axis=-1, keepdims=True) + jnp.sum(pc * dpc, axis=-1, keepdims=True)
                    ds_w = pw * (dpw - delta)
                    ds_c = pc * (dpc - delta)
                    dtab_s[vi, 2 * p + e] += ds_w
                    dsw16 = ds_w.astype(BF16)
                    dsc16 = ds_c.astype(BF16)
                    dq_e = (_dot(dsw16, k2v) + _dot(dsc16, kc2)) * scale
                    dq_acc = jnp.where(me, dq_e, dq_acc)
                    dk_c += _dot_tn(dsw16, qm)
                    dkc_c += _dot_tn(dsc16, qm)
                    dv_c += _dot_tn(pw.astype(BF16), dom)
                    dvc_c += _dot_tn(pc.astype(BF16), dom)
                dq_ref[pl.ds(qoff, GRID_W), ls] = dq_acc.astype(BF16)
                dkacc[pl.ds(off, KH * GRID_W), ls] += dk_c * scale
                dvacc[pl.ds(off, KH * GRID_W), ls] += dv_c
                dkc_ref[:, ls] += dkc_c * scale
                dvc_ref[:, ls] += dvc_c
            return carry

        lax.fori_loop(0, QROWS, row_body, 0)

        @pl.when(i == nb - 1)
        def _():
            dst_row = pl.multiple_of((nb - 3) * qt, qt)
            flush(dkacc, dk_hbm, 0, dst_row, wt)
            flush(dvacc, dv_hbm, 0, dst_row, wt)
            cp = pltpu.make_async_copy(dtab_s, dtab_hbm, sem)
            cp.start()
            cp.wait()

    blk = lambda col: [pl.BlockSpec((qt, NA_W), functools.partial(lambda i, t, c: (wb0(i) + t, c), t=t, c=col))
                       for t in range(3)]
    any_spec = pl.BlockSpec(memory_space=pl.ANY)
    tab_shape = (8, HEADS, GRID_W, KH * GRID_W)
    return pl.pallas_call(
        body, name=name, grid=(nb,),
        in_specs=[pl.BlockSpec((qt, NA_W), lambda i: (i, 0))] + blk(1) + blk(2)
                 + [pl.BlockSpec((nctx, NA_W), lambda i: (0, 1)), pl.BlockSpec((nctx, NA_W), lambda i: (0, 2)),
                    pl.BlockSpec((qt, NA_W), lambda i: (i, 0)), any_spec],
        out_specs=[pl.BlockSpec((qt, NA_W), lambda i: (i, 0)), any_spec, any_spec,
                   pl.BlockSpec((nctx, NA_W), lambda i: (0, 0)), pl.BlockSpec((nctx, NA_W), lambda i: (0, 0)),
                   any_spec],
        out_shape=[jax.ShapeDtypeStruct((seq, NA_W), BF16), jax.ShapeDtypeStruct((seq, NA_W), F32),
                   jax.ShapeDtypeStruct((seq, NA_W), F32), jax.ShapeDtypeStruct((nctx, NA_W), F32),
                   jax.ShapeDtypeStruct((nctx, NA_W), F32), jax.ShapeDtypeStruct(tab_shape, F32)],
        scratch_shapes=[pltpu.VMEM((wt, NA_W), BF16), pltpu.VMEM((wt, NA_W), BF16),
                        pltpu.VMEM((wt, NA_W), F32), pltpu.VMEM((wt, NA_W), F32),
                        pltpu.VMEM(tab_shape, F32), pltpu.VMEM(tab_shape, F32), pltpu.SemaphoreType.DMA],
        compiler_params=_params(("arbitrary",)),
    )(qkvu, qkvu, qkvu, qkvu, qkvu, qkvu, qkvu, qkvu_c, qkvu_c, dmix, tab)


def _halo_specs(te, seq, col, width):
    per = te // HALO
    last = seq // HALO - 1
    return [pl.BlockSpec((HALO, width), lambda i: (jnp.maximum(i * per - 1, 0), col)),
            pl.BlockSpec((te, width), lambda i: (i, col)),
            pl.BlockSpec((HALO, width), lambda i: (jnp.minimum((i + 1) * per, last), col))]


def _extended(prev_ref, cur_ref, next_ref, i, te, seq):
    xe = jnp.concatenate([prev_ref[...], cur_ref[...], next_ref[...]], axis=0).astype(F32)
    pos = i * te - HALO + lax.broadcasted_iota(jnp.int32, (te + 2 * HALO, 1), 0)
    return jnp.where((pos >= 0) & (pos < seq), xe, 0.0), pos


def _window_sum(x, levels, n, mirrored):
    first = (n - 1) if mirrored else 1
    acc = x + pltpu.roll(x, first, 0)
    step = 1
    for _ in range(levels - 1):
        acc = pltpu.roll(acc, step, 0) + pltpu.roll(acc, n - step, 0)
        step *= 2
    return acc


def _window_count(pos, w, seq):
    lo = jnp.clip(pos - w // 2, 0, seq)
    hi = jnp.clip(pos - w // 2 + w, 0, seq)
    return jnp.maximum(hi - lo, 1).astype(F32)


def pool_fwd(qkvu, pool_w, pool_scale, name):
    seq = qkvu.shape[0]
    te = _tile(seq, (512, 256))
    n = te + 2 * HALO

    def body(up_ref, uc_ref, un_ref, w_ref, sc_ref, o_ref):
        i = pl.program_id(0)
        xe, pos = _extended(up_ref, uc_ref, un_ref, i, te, seq)
        cnt = pos[HALO:HALO + te]
        for g, w in enumerate(POOL_WINDOWS):
            ls = slice(g * POOL_G, (g + 1) * POOL_G)
            xg = xe[:, ls]
            win = _window_sum(xg, g + 1, n, False)[HALO:HALO + te]
            dlt = win / _window_count(cnt, w, seq) - xg[HALO:HALO + te]
            z = _dot(dlt.astype(BF16), w_ref[g])
            o_ref[:, ls] = (z * sc_ref[:, ls]).astype(BF16)

    return pl.pallas_call(
        body, name=name, grid=(seq // te,),
        in_specs=_halo_specs(te, seq, 3, POOL_W)
                 + [pl.BlockSpec((4, POOL_G, POOL_G), lambda i: (0, 0, 0)), pl.BlockSpec((1, POOL_W), lambda i: (0, 0))],
        out_specs=pl.BlockSpec((te, POOL_W), lambda i: (i, 0)),
        out_shape=jax.ShapeDtypeStruct((seq, POOL_W), BF16),
        compiler_params=_params(("parallel",)),
    )(qkvu, qkvu, qkvu, pool_w, pool_scale)


def pool_bwd(qkvu, dmix, pool_w, pool_scale, name):
    seq = qkvu.shape[0]
    te = _tile(seq, (512, 256))
    n = te + 2 * HALO

    def body(up_ref, uc_ref, un_ref, dp_ref, dc_ref, dn_ref, w_ref, sc_ref, du_ref, dw_ref, dsc_ref):
        i = pl.program_id(0)

        @pl.when(i == 0)
        def _():
            dw_ref[...] = jnp.zeros_like(dw_ref)
            dsc_ref[...] = jnp.zeros_like(dsc_ref)

        xe, pos = _extended(up_ref, uc_ref, un_ref, i, te, seq)
        de, _ = _extended(dp_ref, dc_ref, dn_ref, i, te, seq)
        cpos = pos[HALO:HALO + te]
        for g, w in enumerate(POOL_WINDOWS):
            ls = slice(g * POOL_G, (g + 1) * POOL_G)
            xg = xe[:, ls]
            wg = w_ref[g]
            win = _window_sum(xg, g + 1, n, False)[HALO:HALO + te]
            dlt = (win / _window_count(cpos, w, seq) - xg[HALO:HALO + te]).astype(BF16)
            z = _dot(dlt, wg)
            dpg = de[:, ls]
            dsc_ref[:, ls] += jnp.sum(dpg[HALO:HALO + te] * z, axis=0, keepdims=True)
            dz = (dpg * sc_ref[:, ls]).astype(BF16)
            dw_ref[g] += _dot_tn(dlt, dz[HALO:HALO + te])
            dd = _dot_nt(dz, wg)
            back = _window_sum(dd / _window_count(pos, w, seq), g + 1, n, True)
            du_ref[:, ls] = (back[HALO:HALO + te] - dd[HALO:HALO + te]).astype(BF16)

    return pl.pallas_call(
        body, name=name, grid=(seq // te,),
        in_specs=_halo_specs(te, seq, 3, POOL_W) + _halo_specs(te, seq, 1, POOL_W)
                 + [pl.BlockSpec((4, POOL_G, POOL_G), lambda i: (0, 0, 0)), pl.BlockSpec((1, POOL_W), lambda i: (0, 0))],
        out_specs=[pl.BlockSpec((te, POOL_W), lambda i: (i, 0)),
                   pl.BlockSpec((4, POOL_G, POOL_G), lambda i: (0, 0, 0)), pl.BlockSpec((1, POOL_W), lambda i: (0, 0))],
        out_shape=[jax.ShapeDtypeStruct((seq, POOL_W), BF16), jax.ShapeDtypeStruct((4, POOL_G, POOL_G), F32),
                   jax.ShapeDtypeStruct((1, POOL_W), F32)],
        compiler_params=_params(("arbitrary",)),
    )(qkvu, qkvu, qkvu, dmix, dmix, dmix, pool_w, pool_scale)


def _shifted(z, zprev_row, znext_row, te):
    rows = lax.broadcasted_iota(jnp.int32, (te, 1), 0)
    zp = jnp.where(rows == 0, zprev_row, pltpu.roll(z, 1, 0))
    zn = jnp.where(rows == te - 1, znext_row, pltpu.roll(z, te - 1, 0))
    return zp, zn


def _edge_rows(prev_ref, next_ref, i, nt):
    p = prev_ref[HALO - 1:HALO, :].astype(F32)
    q = next_ref[0:1, :].astype(F32)
    return jnp.where(i == 0, 0.0, p), jnp.where(i == nt - 1, 0.0, q)


def conv_fwd(proj, conv_w, name):
    seq = proj.shape[0]
    te = _tile(seq, (512, 256))
    nt = seq // te

    def body(bg_ref, cp_ref, cc_ref, cn_ref, xp_ref, xc_ref, xn_ref, w_ref, o_ref):
        i = pl.program_id(0)
        z = cc_ref[...].astype(F32) * xc_ref[...].astype(F32)
        cpr, cnr = _edge_rows(cp_ref, cn_ref, i, nt)
        xpr, xnr = _edge_rows(xp_ref, xn_ref, i, nt)
        zp, zn = _shifted(z, cpr * xpr, cnr * xnr, te)
        y = zp * w_ref[0:1, :] + z * w_ref[1:2, :] + zn * w_ref[2:3, :]
        o_ref[...] = (bg_ref[...].astype(F32) * y).astype(BF16)

    return pl.pallas_call(
        body, name=name, grid=(nt,),
        in_specs=[pl.BlockSpec((te, D), lambda i: (i, 0))] + _halo_specs(te, seq, 1, D) + _halo_specs(te, seq, 2, D)
                 + [pl.BlockSpec((3, D), lambda i: (0, 0))],
        out_specs=pl.BlockSpec((te, D), lambda i: (i, 0)),
        out_shape=jax.ShapeDtypeStruct((seq, D), BF16),
        compiler_params=_params(("parallel",)),
    )(proj, proj, proj, proj, proj, proj, proj, conv_w)


def conv_bwd(proj, dgm, conv_w, name):
    seq = proj.shape[0]
    te = _tile(seq, (512, 256))
    nt = seq // te

    def body(bp_ref, bc_ref, bn_ref, cp_ref, cc_ref, cn_ref, xp_ref, xc_ref, xn_ref, gp_ref, gc_ref, gn_ref, w_ref,
             dbg_ref, dcg_ref, dxin_ref, dw_ref):
        i = pl.program_id(0)

        @pl.when(i == 0)
        def _():
            dw_ref[...] = jnp.zeros_like(dw_ref)

        bg = bc_ref[...].astype(F32)
        cg = cc_ref[...].astype(F32)
        xin = xc_ref[...].astype(F32)
        dg = gc_ref[...].astype(F32)
        z = cg * xin
        cpr, cnr = _edge_rows(cp_ref, cn_ref, i, nt)
        xpr, xnr = _edge_rows(xp_ref, xn_ref, i, nt)
        zp, zn = _shifted(z, cpr * xpr, cnr * xnr, te)
        w0, w1, w2 = w_ref[0:1, :], w_ref[1:2, :], w_ref[2:3, :]
        y = zp * w0 + z * w1 + zn * w2
        dbg_ref[...] = (dg * y).astype(BF16)
        dy = dg * bg
        dw_ref[0:1, :] += jnp.sum(dy * zp, axis=0, keepdims=True)
        dw_ref[1:2, :] += jnp.sum(dy * z, axis=0, keepdims=True)
        dw_ref[2:3, :] += jnp.sum(dy * zn, axis=0, keepdims=True)
        bpr, bnr = _edge_rows(bp_ref, bn_ref, i, nt)
        gpr, gnr = _edge_rows(gp_ref, gn_ref, i, nt)
        dyp, dyn = _shifted(dy, bpr * gpr, bnr * gnr, te)
        dz = dyn * w0 + dy * w1 + dyp * w2
        dcg_ref[...] = (dz * xin).astype(BF16)
        dxin_ref[...] = (dz * cg).astype(BF16)

    row = pl.BlockSpec((te, D), lambda i: (i, 0))
    return pl.pallas_call(
        body, name=name, grid=(nt,),
        in_specs=_halo_specs(te, seq, 0, D) + _halo_specs(te, seq, 1, D) + _halo_specs(te, seq, 2, D)
                 + _halo_specs(te, seq, 0, D) + [pl.BlockSpec((3, D), lambda i: (0, 0))],
        out_specs=[row, row, row, pl.BlockSpec((3, D), lambda i: (0, 0))],
        out_shape=[jax.ShapeDtypeStruct((seq, D), BF16)] * 3 + [jax.ShapeDtypeStruct((3, D), F32)],
        compiler_params=_params(("arbitrary",)),
    )(proj, proj, proj, proj, proj, proj, proj, proj, proj, dgm, dgm, dgm, conv_w)


def _position():
    x, y, c = lax.axis_index("x"), lax.axis_index("y"), lax.axis_index("c")
    return x, y, c, 4 * x + 2 * y + c


def _peer(x, y, c, j):
    px = 1 - x if j & 4 else x
    py = 1 - y if j & 2 else y
    pc = 1 - c if j & 1 else c
    return (px, py, pc), 4 * px + 2 * py + pc


def small_allgather(v, name):
    rows, cols = v.shape

    def body(v_ref, o_ref, send_sems, recv_sems, local_sem):
        x, y, c, me = _position()
        mine = pltpu.make_async_copy(v_ref, o_ref.at[me], local_sem)
        mine.start()
        sends = []
        for j in range(1, N_DEV):
            peer, _ = _peer(x, y, c, j)
            cp = pltpu.make_async_remote_copy(src_ref=v_ref, dst_ref=o_ref.at[me], send_sem=send_sems.at[j - 1],
                                              recv_sem=recv_sems.at[j - 1], device_id=peer, device_id_type=MESH_ID)
            cp.start()
            sends.append(cp)
        for j in range(1, N_DEV):
            peer, pid = _peer(x, y, c, j)
            pltpu.make_async_remote_copy(src_ref=v_ref, dst_ref=o_ref.at[pid], send_sem=send_sems.at[j - 1],
                                         recv_sem=recv_sems.at[j - 1], device_id=peer,
                                         device_id_type=MESH_ID).wait_recv()
        for cp in sends:
            cp.wait_send()
        mine.wait()

    return pl.pallas_call(
        body, name=name,
        out_shape=jax.ShapeDtypeStruct((N_DEV, rows, cols), v.dtype),
        in_specs=[pl.BlockSpec(memory_space=pltpu.VMEM)],
        out_specs=pl.BlockSpec(memory_space=pltpu.VMEM),
        scratch_shapes=[pltpu.SemaphoreType.DMA((N_DEV - 1,)), pltpu.SemaphoreType.DMA((N_DEV - 1,)),
                        pltpu.SemaphoreType.DMA],
        compiler_params=pltpu.CompilerParams(vmem_limit_bytes=VMEM_LIMIT),
    )(v)


def allgather_hbm(shards, name):
    n = len(shards)

    def body(*refs):
        ins, outs = refs[:n], refs[n:2 * n]
        send_sems, recv_sems, local_sems = refs[2 * n:]
        x, y, c, me = _position()
        sibling = (x, y, 1 - c)
        chips = [(1 - x, y), (x, 1 - y), (1 - x, 1 - y)]

        def blk(k, px, py, pc):
            return outs[k].at[4 * px + 2 * py + pc]

        def copy(k, slot, block, to, src=None):
            return pltpu.make_async_remote_copy(
                src_ref=blk(k, *block) if src is None else src, dst_ref=blk(k, *block),
                send_sem=send_sems.at[k * 7 + slot], recv_sem=recv_sems.at[k * 7 + slot],
                device_id=to, device_id_type=MESH_ID)

        mine = [pltpu.make_async_copy(ins[k], blk(k, x, y, c), local_sems.at[k]) for k in range(n)]
        for cp in mine:
            cp.start()
        first = []
        for k in range(n):
            first.append(copy(k, 0, (x, y, c), sibling, src=ins[k]))
            first += [copy(k, 1 + j, (x, y, c), (*chip, c), src=ins[k]) for j, chip in enumerate(chips)]
        for cp in first:
            cp.start()
        passed = []
        for j, chip in enumerate(chips):
            for k in range(n):
                copy(k, 1 + j, (*chip, c), (x, y, c)).wait_recv()
                cp = copy(k, 4 + j, (*chip, c), sibling)
                cp.start()
                passed.append(cp)
        for k in range(n):
            copy(k, 0, (x, y, 1 - c), (x, y, c)).wait_recv()
            for j, chip in enumerate(chips):
                copy(k, 4 + j, (*chip, 1 - c), (x, y, c)).wait_recv()
        for cp in first + passed:
            cp.wait_send()
        for cp in mine:
            cp.wait()

    hbm = pl.BlockSpec(memory_space=pltpu.HBM)
    return pl.pallas_call(
        body, name=name,
        out_shape=[jax.ShapeDtypeStruct((N_DEV,) + s.shape, s.dtype) for s in shards],
        in_specs=[hbm] * n, out_specs=[hbm] * n,
        scratch_shapes=[pltpu.SemaphoreType.DMA((7 * n,)), pltpu.SemaphoreType.DMA((7 * n,)),
                        pltpu.SemaphoreType.DMA((n,))],
    )(*shards)


def scatter_hbm(blocks, name):
    n = len(blocks)

    def body(*refs):
        ins, outs = refs[:n], refs[n:2 * n]
        send_sems, recv_sems, local_sems = refs[2 * n:]
        x, y, c, me = _position()
        mine = [pltpu.make_async_copy(ins[k].at[me], outs[k].at[me], local_sems.at[k]) for k in range(n)]
        for cp in mine:
            cp.start()
        sends = []
        for j in range(1, N_DEV):
            peer, pid = _peer(x, y, c, j)
            for k in range(n):
                cp = pltpu.make_async_remote_copy(
                    src_ref=ins[k].at[pid], dst_ref=outs[k].at[me], send_sem=send_sems.at[k * 7 + j - 1],
                    recv_sem=recv_sems.at[k * 7 + j - 1], device_id=peer, device_id_type=MESH_ID)
                cp.start()
                sends.append(cp)
        for j in range(1, N_DEV):
            peer, pid = _peer(x, y, c, j)
            for k in range(n):
                pltpu.make_async_remote_copy(
                    src_ref=ins[k].at[pid], dst_ref=outs[k].at[pid], send_sem=send_sems.at[k * 7 + j - 1],
                    recv_sem=recv_sems.at[k * 7 + j - 1], device_id=peer, device_id_type=MESH_ID).wait_recv()
        for cp in sends:
            cp.wait_send()
        for cp in mine:
            cp.wait()

    hbm = pl.BlockSpec(memory_space=pltpu.HBM)
    return pl.pallas_call(
        body, name=name,
        out_shape=[jax.ShapeDtypeStruct(b.shape, b.dtype) for b in blocks],
        in_specs=[hbm] * n, out_specs=[hbm] * n,
        scratch_shapes=[pltpu.SemaphoreType.DMA((7 * n,)), pltpu.SemaphoreType.DMA((7 * n,)),
                        pltpu.SemaphoreType.DMA((n,))],
    )(*blocks)


def sum_devices(v, name):
    _, rows, cols = v.shape

    def body(v_ref, o_ref):
        acc = v_ref[0]
        for p in range(1, N_DEV):
            acc = acc + v_ref[p]
        o_ref[...] = acc

    return pl.pallas_call(
        body, name=name, out_shape=jax.ShapeDtypeStruct((rows, cols), F32),
        compiler_params=pltpu.CompilerParams(vmem_limit_bytes=VMEM_LIMIT),
    )(v)


def _silu(x):
    return x * _sigmoid(x)


def adaln_fwd(cm, mod_w, mod_b_cols, name):
    cols = mod_w.shape[2]

    def body(c_ref, w_ref, b_ref, o_ref):
        o_ref[0] = jnp.dot(_silu(c_ref[...]), w_ref[0], precision=HI, preferred_element_type=F32) + b_ref[0]

    return pl.pallas_call(
        body, name=name, grid=(2,),
        in_specs=[pl.BlockSpec((16, D), lambda l: (0, 0)), pl.BlockSpec((1, D, cols), lambda l: (l, 0, 0)),
                  pl.BlockSpec((1, 1, cols), lambda l: (l, 0, 0))],
        out_specs=pl.BlockSpec((1, 16, cols), lambda l: (l, 0, 0)),
        out_shape=jax.ShapeDtypeStruct((2, 16, cols), F32),
        compiler_params=_params(("parallel",)),
    )(cm, mod_w, mod_b_cols)


def adaln_bwd(cm_t, mod_w, dm_t, name):
    cols = mod_w.shape[2]

    def body(c_ref, w_ref, lat_ref, ctx_ref, gw_ref, pc_ref):
        ctot = jnp.sum(ctx_ref[0], axis=0, keepdims=True)
        rows = lax.broadcasted_iota(jnp.int32, (8, 1), 0)
        g_hi = jnp.where(rows == 0, ctot, 0.0)
        g = jnp.concatenate([lat_ref[0], g_hi], axis=0)
        gw_ref[0] = jnp.dot(_silu(c_ref[...]), g, precision=HI, preferred_element_type=F32)
        pc_ref[0] = lax.dot_general(g_hi, w_ref[0], NT_DIMS, precision=HI, preferred_element_type=F32)

    return pl.pallas_call(
        body, name=name, grid=(2,),
        in_specs=[pl.BlockSpec((D, 16), lambda l: (0, 0)), pl.BlockSpec((1, D, cols), lambda l: (l, 0, 0)),
                  pl.BlockSpec((1, 8, cols), lambda l: (l, 0, 0)), pl.BlockSpec((1, 8, cols), lambda l: (l + 2, 0, 0))],
        out_specs=[pl.BlockSpec((1, D, cols), lambda l: (l, 0, 0)), pl.BlockSpec((1, 8, D), lambda l: (l, 0, 0))],
        out_shape=[jax.ShapeDtypeStruct((2, D, cols), F32), jax.ShapeDtypeStruct((2, 8, D), F32)],
        compiler_params=_params(("parallel",)),
    )(cm_t, mod_w, dm_t, dm_t)


def mod_b_grad(dm_t, name):
    width = dm_t.shape[2]
    tn = width // 8

    def body(d_ref, o_ref):
        s = jnp.concatenate([jnp.sum(d_ref[k], axis=0, keepdims=True) for k in range(4)]
                            + [jnp.zeros((4, tn), F32)], axis=0)
        o_ref[...] = s + pltpu.roll(s, 6, 0)

    return pl.pallas_call(
        body, name=name, grid=(8,),
        in_specs=[pl.BlockSpec((4, 8, tn), lambda j: (0, 0, j))],
        out_specs=pl.BlockSpec((8, tn), lambda j: (0, j)),
        out_shape=jax.ShapeDtypeStruct((8, width), F32),
        compiler_params=_params(("parallel",)),
    )(dm_t)


def adamw(w, m, v, name, g=None, recv=None):
    rows, cols = w.shape
    tr = _tile(rows, (256, 128, 64, 32, 16, 8))
    summed = recv is not None

    def body(w_ref, m_ref, v_ref, g_ref, go_ref, d_ref, mo_ref, vo_ref):
        if summed:
            gv = g_ref[0].astype(F32)
            for p in range(1, N_DEV):
                gv = gv + g_ref[p].astype(F32)
        else:
            gv = g_ref[...]
        mn = ADAM_B1 * m_ref[...] + (1.0 - ADAM_B1) * gv
        vn = ADAM_B2 * v_ref[...] + (1.0 - ADAM_B2) * (gv * gv)
        m_hat = mn / (1.0 - ADAM_B1 ** ADAM_STEP)
        v_hat = vn / (1.0 - ADAM_B2 ** ADAM_STEP)
        go_ref[...] = gv
        d_ref[...] = -ADAM_LR * (m_hat / (jnp.sqrt(v_hat) + ADAM_EPS) + ADAM_WD * w_ref[...])
        mo_ref[...] = mn
        vo_ref[...] = vn

    row = pl.BlockSpec((tr, cols), lambda i: (i, 0))
    gspec = pl.BlockSpec((N_DEV, tr, cols), lambda i: (0, i, 0)) if summed else row
    return pl.pallas_call(
        body, name=name, grid=(rows // tr,),
        in_specs=[row, row, row, gspec], out_specs=[row] * 4,
        out_shape=[jax.ShapeDtypeStruct((rows, cols), F32)] * 4,
        compiler_params=_params(("parallel",)),
    )(w, m, v, recv if summed else g)


def _ffn_fwd(h, mods, g, w13, w2, base, tag):
    hn = normmod(h, g, mods, base, base + 1, tag + "_norm")
    a, b, s = ffn_up(hn, w13, tag + "_up")
    h_new, y = mm_nn([s], w2, [0], tag + "_down", res=(h, mods, base + 2, 0.5))
    return h_new, (h, hn, a, b, s, y)


def _ffn_bwd(dout, saved, mods, g, w13, w2, base, tag, acc=None):
    h, hn, a, b, s, y = saved
    ff = w2.shape[0]
    acc = acc or (None, None, None)
    dy, dgate = gate_bwd(dout, y, mods, base + 2, 0.5, tag + "_gate_bwd")
    da, db = mm_nt([dy], w2, [0], tag + "_ds", dswiglu=(a, b))
    dw2 = mm_tn(s, dy, tag + "_dw2", acc=acc[2])
    dhn = mm_nt([da, db], w13, [0, ff], tag + "_dhn")
    dwa = mm_tn(hn, da, tag + "_dw13a", acc=acc[0])
    dwb = mm_tn(hn, db, tag + "_dw13b", acc=acc[1])
    dh, dshift, dscale, dg = normmod_bwd(dhn, h, g, mods, base + 1, dout, tag + "_norm_bwd")
    return dh, (dwa, dwb, dw2), dg, {base: dshift, base + 1: dscale, base + 2: dgate}


def _mod_rows(parts):
    zero = jnp.zeros((1, D), F32)
    return jnp.concatenate([parts.get(k, zero) for k in range(N_MOD)], axis=0)


def local_step(x, ctx, ml, mc, wts, target):
    w13, w2 = wts["w13"], wts["w2"]
    ng = wts["norm_g"]
    gvec = lambda l, k: ng[l, k][None, :]
    pool_w16 = wts["pool_w"].astype(BF16)

    x1, sv1 = _ffn_fwd(x, ml[0], gvec(0, 0), w13[0, 0], w2[0, 0], 0, "l0f1")
    c1, sv1c = _ffn_fwd(ctx, mc[0], gvec(0, 0), w13[0, 0], w2[0, 0], 0, "l0f1c")
    xn = normmod(x1, gvec(0, 1), ml[0], 3, 4, "l0mix_norm")
    cn = normmod(c1, gvec(0, 1), mc[0], 3, 4, "l0mix_norm_c")
    qkvu = mm_nn([xn], wts["ewi"], [0], "l0mix_in")
    qkvu_c = mm_nn([cn], wts["ewi"], [0], "l0mix_in_c")
    tab = bias_table(wts["rpb"])
    att = attn_fwd(qkvu, qkvu_c, tab, "l0_attn")
    pool = pool_fwd(qkvu, pool_w16, wts["pool_scale"], "l0_pool")
    x2, ymix = mm_nn([att, pool], wts["ewo"], [0, NA_W], "l0mix_out", res=(x1, ml[0], 5, 1.0))
    x3, sv2 = _ffn_fwd(x2, ml[0], gvec(0, 2), w13[0, 1], w2[0, 1], 6, "l0f2")

    x4, sv3 = _ffn_fwd(x3, ml[1], gvec(1, 0), w13[1, 0], w2[1, 0], 0, "l1f1")
    xn1 = normmod(x4, gvec(1, 1), ml[1], 3, 4, "l1mix_norm")
    proj = mm_nn([xn1], wts["cwi"], [0], "l1mix_in")
    gm = conv_fwd(proj, wts["conv_w"], "l1_conv")
    x5, ycv = mm_nn([gm], wts["cwo"], [0], "l1mix_out", res=(x4, ml[1], 5, 1.0))
    x6, sv4 = _ffn_fwd(x5, ml[1], gvec(1, 2), w13[1, 1], w2[1, 1], 6, "l1f2")
    dx6, loss, dgf = loss_head(x6, wts["final_g"][None, :], target, "loss_head")

    dx5, dwf4, dg12, dm_f4 = _ffn_bwd(dx6, sv4, ml[1], gvec(1, 2), w13[1, 1], w2[1, 1], 6, "l1f2")
    dy, dgate1 = gate_bwd(dx5, ycv, ml[1], 5, 1.0, "l1mix_gate_bwd")
    dgm = mm_nt([dy], wts["cwo"], [0], "l1mix_dgm", out_dtype=BF16)
    dcwo = mm_tn(gm, dy, "l1mix_dwo")
    dbg, dcg, dxin, dconv_w = conv_bwd(proj, dgm, wts["conv_w"], "l1_conv_bwd")
    dxn1 = mm_nt([dbg, dcg, dxin], wts["cwi"], [0, D, 2 * D], "l1mix_dxn")
    dcwi = jnp.concatenate([mm_tn(xn1, t, "l1mix_dwi%d" % k) for k, t in enumerate((dbg, dcg, dxin))], axis=1)
    dx4, dsh, dsc, dg11 = normmod_bwd(dxn1, x4, gvec(1, 1), ml[1], 4, dx5, "l1mix_norm_bwd")
    dm1 = {3: dsh, 4: dsc, 5: dgate1, **dm_f4}
    dx3, dwf3, dg10, dm_f3 = _ffn_bwd(dx4, sv3, ml[1], gvec(1, 0), w13[1, 0], w2[1, 0], 0, "l1f1")
    dm1.update(dm_f3)

    dx2, dwf2, dg02, dm_f2 = _ffn_bwd(dx3, sv2, ml[0], gvec(0, 2), w13[0, 1], w2[0, 1], 6, "l0f2")
    dy, dgate0 = gate_bwd(dx2, ymix, ml[0], 5, 1.0, "l0mix_gate_bwd")
    dmix = mm_nt([dy], wts["ewo"], [0], "l0mix_dmix", out_dtype=BF16)
    dewo = jnp.concatenate([mm_tn(att, dy, "l0mix_dwo_att"), mm_tn(pool, dy, "l0mix_dwo_pool")], axis=0)
    dq, dk, dv, dkc, dvc, dtab = attn_bwd(qkvu, qkvu_c, tab, dmix, "l0_attn_bwd")
    du, dpool_w, dpool_scale = pool_bwd(qkvu, dmix, pool_w16, wts["pool_scale"], "l0_pool_bwd")
    drpb = bias_table_bwd(dtab)
    dk16, dv16, dkc16, dvc16 = (t.astype(BF16) for t in (dk, dv, dkc, dvc))
    dxn = mm_nt([dq, dk16, dv16, du], wts["ewi"], [0, NA_W, 2 * NA_W, 3 * NA_W], "l0mix_dxn")
    dcn = mm_nt([dkc16, dvc16], wts["ewi"], [NA_W, 2 * NA_W], "l0mix_dxn_c")
    dewi = jnp.concatenate([
        mm_tn(xn, dq, "l0mix_dwi_q"),
        mm_tn(cn, dkc16, "l0mix_dwi_kc", acc=mm_tn(xn, dk16, "l0mix_dwi_k")),
        mm_tn(cn, dvc16, "l0mix_dwi_vc", acc=mm_tn(xn, dv16, "l0mix_dwi_v")),
        mm_tn(xn, du, "l0mix_dwi_u")], axis=1)
    dx1, dsh, dsc, dg01 = normmod_bwd(dxn, x1, gvec(0, 1), ml[0], 4, dx2, "l0mix_norm_bwd")
    dc1, dsh_c, dsc_c, dg01c = normmod_bwd(dcn, c1, gvec(0, 1), mc[0], 4, jnp.zeros_like(dcn), "l0mix_norm_bwd_c")
    dm0 = {3: dsh, 4: dsc, 5: dgate0, **dm_f2}
    dx0, dwf1, dg00, dm_f1 = _ffn_bwd(dx1, sv1, ml[0], gvec(0, 0), w13[0, 0], w2[0, 0], 0, "l0f1")
    dm0.update(dm_f1)
    _, dwf1, dg00c, dm_f1c = _ffn_bwd(dc1, sv1c, mc[0], gvec(0, 0), w13[0, 0], w2[0, 0], 0, "l0f1c", acc=dwf1)
    dmc0 = {3: dsh_c, 4: dsc_c, **dm_f1c}

    stack_w13 = lambda f: jnp.concatenate([f[0], f[1]], axis=1)
    return {
        "loss": loss, "grad_x": dx0,
        "dml": jnp.stack([_mod_rows(dm0), _mod_rows(dm1)]),
        "dmc": jnp.stack([_mod_rows(dmc0), jnp.zeros((N_MOD, D), F32)]),
        "norm_g": jnp.concatenate([dg00 + dg00c, dg01 + dg01c, dg02, dg10, dg11, dg12], axis=0),
        "w13": jnp.stack([jnp.stack([stack_w13(dwf1), stack_w13(dwf2)]),
                          jnp.stack([stack_w13(dwf3), stack_w13(dwf4)])]),
        "w2": jnp.stack([jnp.stack([dwf1[2], dwf2[2]]), jnp.stack([dwf3[2], dwf4[2]])]),
        "ewi": dewi, "ewo": dewo, "rpb": drpb, "pool_w": dpool_w, "pool_scale": dpool_scale,
        "cwi": dcwi, "conv_w": dconv_w, "cwo": dcwo, "final_g": dgf,
    }


SMALL_ROWS = 88


def _rows_of(v, nrows):
    flat = v.reshape(-1)
    return jnp.pad(flat, (0, nrows * D - flat.shape[0])).reshape(nrows, D)


def kernel(x, c, ctx, c_ctx, mod_w, mod_b, norm_g, ffn_w13, ffn_w2, even_w_in, even_w_out, na_rpb, pool_w, pool_scale, conv_w_in, conv_w, conv_w_out, final_g, loss_target, m_c_ctx, m_mod_w, m_mod_b, m_norm_g, m_ffn_w13, m_ffn_w2, m_even_w_in, m_even_w_out, m_na_rpb, m_pool_w, m_pool_scale, m_conv_w_in, m_conv_w, m_conv_w_out, m_final_g, v_c_ctx, v_mod_w, v_mod_b, v_norm_g, v_ffn_w13, v_ffn_w2, v_even_w_in, v_even_w_out, v_na_rpb, v_pool_w, v_pool_scale, v_conv_w_in, v_conv_w, v_conv_w_out, v_final_g):
    me = 4 * lax.axis_index("x") + 2 * lax.axis_index("y") + lax.axis_index("c")
    ff = ffn_w2.shape[2] * N_DEV
    w13c = ffn_w13.shape[3]
    w2r = ffn_w2.shape[2]
    mcols = mod_w.shape[2]
    gcols = norm_g.shape[2]

    big = {"w13": ffn_w13.reshape(4 * D, w13c), "w2": ffn_w2.reshape(4 * w2r, D), "ewi": even_w_in[0],
           "ewo": even_w_out[0], "cwi": conv_w_in[0], "cwo": conv_w_out[0]}
    names = list(big)
    gathered = dict(zip(names, allgather_hbm([big[k].astype(BF16) for k in names], "weights_allgather")))
    wts = {
        "w13": gathered["w13"].reshape(N_DEV, 2, 2, D, w13c).transpose(1, 2, 3, 0, 4).reshape(2, 2, D, 2 * ff),
        "w2": gathered["w2"].reshape(N_DEV, 2, 2, w2r, D).transpose(1, 2, 0, 3, 4).reshape(2, 2, ff, D),
        "ewi": gathered["ewi"].transpose(1, 0, 2).reshape(D, -1),
        "ewo": gathered["ewo"].reshape(-1, D),
        "cwi": gathered["cwi"].transpose(1, 0, 2).reshape(D, -1),
        "cwo": gathered["cwo"].reshape(-1, D),
    }

    c_all = small_allgather(jnp.pad(c, ((0, 7), (0, 0))), "cond_allgather")[:, 0, :]
    cm = jnp.concatenate([c_all, c_ctx[None, :], jnp.zeros((7, D), F32)], axis=0)
    mod_b_cols = lax.dynamic_slice(mod_b, (0, me * mcols), (2, mcols))[:, None, :]
    m_cols = adaln_fwd(cm, mod_w, mod_b_cols, "adaln_fwd")
    m_all = small_allgather(m_cols.reshape(32, mcols), "mod_allgather")
    m_full = m_all.reshape(N_DEV, 2, 16, mcols).transpose(1, 2, 0, 3).reshape(2, 16, N_MOD * D)
    ml = lax.dynamic_slice(m_full, (0, me, 0), (2, 1, N_MOD * D)).reshape(2, N_MOD, D)
    mc = m_full[:, 8].reshape(2, N_MOD, D)

    full_norm_g = small_allgather(_rows_of(norm_g, 8), "norm_g_allgather")[:, 0, :2 * 3 * gcols]
    full_norm_g = full_norm_g.reshape(N_DEV, 2, 3, gcols).transpose(1, 2, 0, 3).reshape(2, 3, D)
    full_conv_w = small_allgather(_rows_of(conv_w, 8), "conv_w_allgather")[:, 0, :3 * gcols]
    full_conv_w = full_conv_w.reshape(N_DEV, 3, gcols).transpose(1, 0, 2).reshape(3, D)
    wts.update(norm_g=full_norm_g, conv_w=full_conv_w, rpb=na_rpb[0], pool_w=pool_w[0], pool_scale=pool_scale,
               final_g=final_g)
    out = local_step(x[0], ctx[0], ml, mc, wts, loss_target[0])

    dm_pack = jnp.concatenate([out["dml"].reshape(2, N_MOD * D), out["dmc"].reshape(2, N_MOD * D),
                               jnp.zeros((4, N_MOD * D), F32)], axis=0)
    dm_t = small_allgather(dm_pack, "dmod_allgather").transpose(1, 0, 2)[:4]
    dm_cols = lax.dynamic_slice(dm_t, (0, 0, me * mcols), (4, N_DEV, mcols))
    g_mod_w, pc = adaln_bwd(cm.T, mod_w, dm_cols, "adaln_bwd")
    g_mod_b = mod_b_grad(dm_t, "mod_b_grad")[:2]

    pack = jnp.concatenate([
        out["norm_g"], out["conv_w"], out["final_g"], pc[0, :1] + pc[1, :1], _rows_of(out["pool_scale"], 1),
        _rows_of(out["loss"], 1), _rows_of(out["rpb"], 4), jnp.zeros((7, D), F32), _rows_of(out["pool_w"], 64)], axis=0)
    small = sum_devices(small_allgather(pack, "small_grads_allgather"), "small_grads_sum")
    g_norm_g = lax.dynamic_slice(small[0:6].reshape(2, 3, D), (0, 0, me * gcols), (2, 3, gcols))
    g_conv_w = lax.dynamic_slice(small[6:9], (0, me * gcols), (3, gcols))[None]
    g_final_g = small[9]
    sg = _sigmoid(c_ctx)
    g_c_ctx = small[10] * (sg * (1.0 + c_ctx * (1.0 - sg)))
    g_pool_scale = small[11:12, :POOL_W]
    loss = small[12, 0]
    g_rpb = small[13:17].reshape(-1)[:na_rpb.size].reshape(na_rpb.shape)
    g_pool_w = small[24:88].reshape(pool_w.shape)

    blocks = {
        "w13": out["w13"].reshape(2, 2, D, N_DEV, w13c).transpose(3, 0, 1, 2, 4).reshape(N_DEV, 4 * D, w13c),
        "w2": out["w2"].reshape(2, 2, N_DEV, w2r, D).transpose(2, 0, 1, 3, 4).reshape(N_DEV, 4 * w2r, D),
        "ewi": out["ewi"].reshape(D, N_DEV, -1).transpose(1, 0, 2),
        "ewo": out["ewo"].reshape(N_DEV, -1, D),
        "cwi": out["cwi"].reshape(D, N_DEV, -1).transpose(1, 0, 2),
        "cwo": out["cwo"].reshape(N_DEV, -1, D),
    }
    recv = dict(zip(names, scatter_hbm([blocks[k].astype(BF16) for k in names], "grads_scatter")))

    moments = {"w13": (m_ffn_w13, v_ffn_w13), "w2": (m_ffn_w2, v_ffn_w2), "ewi": (m_even_w_in, v_even_w_in),
               "ewo": (m_even_w_out, v_even_w_out), "cwi": (m_conv_w_in, v_conv_w_in),
               "cwo": (m_conv_w_out, v_conv_w_out)}
    orig = {"w13": ffn_w13, "w2": ffn_w2, "ewi": even_w_in, "ewo": even_w_out, "cwi": conv_w_in, "cwo": conv_w_out}
    upd = {}
    for k in names:
        shp2 = big[k].shape
        res = adamw(big[k], moments[k][0].reshape(shp2), moments[k][1].reshape(shp2), "adamw_" + k, recv=recv[k])
        upd[k] = [r.reshape(orig[k].shape) for r in res]
    shp2 = (2 * D, mcols)
    upd["mod_w"] = [r.reshape(mod_w.shape) for r in adamw(mod_w.reshape(shp2), m_mod_w.reshape(shp2),
                                                          v_mod_w.reshape(shp2), "adamw_mod_w",
                                                          g=g_mod_w.reshape(shp2))]

    smalls = [("c_ctx", c_ctx, m_c_ctx, v_c_ctx, g_c_ctx, 1), ("mod_b", mod_b, m_mod_b, v_mod_b, g_mod_b, 18),
              ("norm_g", norm_g, m_norm_g, v_norm_g, g_norm_g, 1), ("rpb", na_rpb, m_na_rpb, v_na_rpb, g_rpb, 4),
              ("pool_w", pool_w, m_pool_w, v_pool_w, g_pool_w, 64),
              ("pool_scale", pool_scale, m_pool_scale, v_pool_scale, g_pool_scale, 1),
              ("conv_w", conv_w, m_conv_w, v_conv_w, g_conv_w, 1), ("final_g", final_g, m_final_g, v_final_g, g_final_g, 1)]
    packed = [jnp.concatenate([_rows_of(s[col], s[5]) for s in smalls] + [jnp.zeros((5, D), F32)], axis=0)
              for col in (1, 2, 3, 4)]
    res = adamw(packed[0], packed[1], packed[2], "adamw_small", g=packed[3])
    row = 0
    for name, w, _, _, _, nrows in smalls:
        upd[name] = [r[row:row + nrows].reshape(-1)[:w.size].reshape(w.shape) for r in res]
        row += nrows

    order = ["c_ctx", "mod_w", "mod_b", "norm_g", "w13", "w2", "ewi", "ewo", "rpb", "pool_w", "pool_scale", "cwi",
             "conv_w", "cwo", "final_g"]
    grad_x = out["grad_x"][None]
    return (loss, grad_x, *[upd[k][0] for k in order], *[upd[k][1] for k in order], *[upd[k][2] for k in order],
            *[upd[k][3] for k in order])
```

```python
import functools

import numpy as np
import jax
import jax.numpy as jnp
from jax import lax
from jax.experimental import pallas as pl
from jax.experimental.pallas import tpu as pltpu

D = 1024
FF = 2816
SEQ = 16384
CTX = 256
GRID_W = 64
N_MOD = 9
HEADS = 8
HEAD_DIM = 64
NA_W = 512
POOL_W = 512
POOL_G = 128
POOL_WINDOWS = (2, 4, 8, 16)
KH = 8
KW = 16
RMS_EPS = 1e-6
NEG_INF = -1e30
N_DEV = 8

ADAM_LR = 0.001
ADAM_B1 = 0.9
ADAM_B2 = 0.999
ADAM_EPS = 1e-08
ADAM_WD = 0.01
ADAM_STEP = 10

VMEM_LIMIT = 52 * 1024 * 1024
HALO = 16
QROWS = 8
WROWS = 24

BF16 = jnp.bfloat16
F32 = jnp.float32
MESH_ID = pl.DeviceIdType.MESH
HI = lax.Precision.HIGHEST

NT_DIMS = (((1,), (1,)), ((), ()))
TN_DIMS = (((0,), (0,)), ((), ()))


def _tile(n, cands):
    for c in cands:
        if n % c == 0:
            return c
    return n


def _params(sem):
    return pltpu.CompilerParams(dimension_semantics=sem, vmem_limit_bytes=VMEM_LIMIT)


def _dot(a, b):
    return jnp.dot(a, b, preferred_element_type=F32)


def _dot_nt(a, b):
    return lax.dot_general(a, b, NT_DIMS, preferred_element_type=F32)


def _dot_tn(a, b):
    return lax.dot_general(a, b, TN_DIMS, preferred_element_type=F32)


def _sigmoid(x):
    return 1.0 / (1.0 + jnp.exp(-x))


def normmod(h, g, mods, i_shift, i_scale, name):
    n = h.shape[0]
    te = _tile(n, (512, 256))

    def body(h_ref, g_ref, m_ref, o_ref):
        x = h_ref[...]
        r = lax.rsqrt(jnp.mean(x * x, axis=-1, keepdims=True) + RMS_EPS)
        y = x * r * g_ref[...]
        o_ref[...] = (y * (1.0 + m_ref[i_scale:i_scale + 1, :]) + m_ref[i_shift:i_shift + 1, :]).astype(BF16)

    return pl.pallas_call(
        body, name=name, grid=(n // te,),
        in_specs=[pl.BlockSpec((te, D), lambda i: (i, 0)),
                  pl.BlockSpec((1, D), lambda i: (0, 0)),
                  pl.BlockSpec((N_MOD, D), lambda i: (0, 0))],
        out_specs=pl.BlockSpec((te, D), lambda i: (i, 0)),
        out_shape=jax.ShapeDtypeStruct((n, D), BF16),
        compiler_params=_params(("parallel",)),
    )(h, g, mods)


def normmod_bwd(dhn, h, g, mods, i_scale, dres, name):
    n = h.shape[0]
    te = _tile(n, (256,))

    def body(dhn_ref, h_ref, g_ref, m_ref, dres_ref, dh_ref, dshift_ref, dscale_ref, dg_ref):
        i = pl.program_id(0)

        @pl.when(i == 0)
        def _():
            dshift_ref[...] = jnp.zeros_like(dshift_ref)
            dscale_ref[...] = jnp.zeros_like(dscale_ref)
            dg_ref[...] = jnp.zeros_like(dg_ref)

        x = h_ref[...]
        d = dhn_ref[...]
        gv = g_ref[...]
        r = lax.rsqrt(jnp.mean(x * x, axis=-1, keepdims=True) + RMS_EPS)
        xhat = x * r
        dshift_ref[...] += jnp.sum(d, axis=0, keepdims=True)
        dscale_ref[...] += jnp.sum(d * (xhat * gv), axis=0, keepdims=True)
        dn = d * (1.0 + m_ref[i_scale:i_scale + 1, :])
        dg_ref[...] += jnp.sum(dn * xhat, axis=0, keepdims=True)
        dxhat = dn * gv
        dh_ref[...] = dres_ref[...] + r * (dxhat - xhat * jnp.mean(dxhat * xhat, axis=-1, keepdims=True))

    row = pl.BlockSpec((te, D), lambda i: (i, 0))
    vec = pl.BlockSpec((1, D), lambda i: (0, 0))
    return pl.pallas_call(
        body, name=name, grid=(n // te,),
        in_specs=[row, row, vec, pl.BlockSpec((N_MOD, D), lambda i: (0, 0)), row],
        out_specs=[row, vec, vec, vec],
        out_shape=[jax.ShapeDtypeStruct((n, D), F32)] + [jax.ShapeDtypeStruct((1, D), F32)] * 3,
        compiler_params=_params(("arbitrary",)),
    )(dhn, h, g, mods, dres)


def gate_bwd(dout, y, mods, i_gate, coef, name):
    n = dout.shape[0]
    te = _tile(n, (256,))

    def body(d_ref, y_ref, m_ref, dy_ref, dg_ref):
        @pl.when(pl.program_id(0) == 0)
        def _():
            dg_ref[...] = jnp.zeros_like(dg_ref)

        d = d_ref[...]
        dy_ref[...] = (d * (coef * m_ref[i_gate:i_gate + 1, :])).astype(BF16)
        dg_ref[...] += coef * jnp.sum(d * y_ref[...].astype(F32), axis=0, keepdims=True)

    row = pl.BlockSpec((te, D), lambda i: (i, 0))
    return pl.pallas_call(
        body, name=name, grid=(n // te,),
        in_specs=[row, row, pl.BlockSpec((N_MOD, D), lambda i: (0, 0))],
        out_specs=[row, pl.BlockSpec((1, D), lambda i: (0, 0))],
        out_shape=[jax.ShapeDtypeStruct((n, D), BF16), jax.ShapeDtypeStruct((1, D), F32)],
        compiler_params=_params(("arbitrary",)),
    )(dout, y, mods)


def loss_head(x, g, target, name):
    n = x.shape[0]
    te = _tile(n, (256,))

    def body(x_ref, g_ref, t_ref, dx_ref, loss_ref, dg_ref):
        @pl.when(pl.program_id(0) == 0)
        def _():
            loss_ref[...] = jnp.zeros_like(loss_ref)
            dg_ref[...] = jnp.zeros_like(dg_ref)

        xv = x_ref[...]
        gv = g_ref[...]
        r = lax.rsqrt(jnp.mean(xv * xv, axis=-1, keepdims=True) + RMS_EPS)
        xhat = xv * r
        e = xhat * gv - t_ref[...]
        per_tok = jnp.mean(e * e, axis=-1, keepdims=True)
        loss_ref[...] += 0.5 * jnp.sum(per_tok, axis=0, keepdims=True)
        dy = e * (1.0 / D)
        dg_ref[...] += jnp.sum(dy * xhat, axis=0, keepdims=True)
        dxhat = dy * gv
        dx_ref[...] = r * (dxhat - xhat * jnp.mean(dxhat * xhat, axis=-1, keepdims=True))

    row = pl.BlockSpec((te, D), lambda i: (i, 0))
    vec = pl.BlockSpec((1, D), lambda i: (0, 0))
    return pl.pallas_call(
        body, name=name, grid=(n // te,),
        in_specs=[row, vec, row],
        out_specs=[row, pl.BlockSpec((1, 128), lambda i: (0, 0)), vec],
        out_shape=[jax.ShapeDtypeStruct((n, D), F32), jax.ShapeDtypeStruct((1, 128), F32),
                   jax.ShapeDtypeStruct((1, D), F32)],
        compiler_params=_params(("arbitrary",)),
    )(x, g, target)


def ffn_up(hn, w13, name, exchange=None):
    n = hn.shape[0]
    ff = w13.shape[1] // 2
    tm = _tile(n, (512, 256))
    tn = _tile(ff, (1408, 512, 256, 128))
    nj = ff // tn
    ni = n // tm
    nx = exchange.n if exchange else 0

    def body(*refs):
        h_ref, wa_ref, wb_ref = refs[:3]
        x_in = refs[3:3 + nx]
        a_ref, b_ref, s_ref = refs[3 + nx:6 + nx]
        x_out = refs[6 + nx:6 + 2 * nx]
        x_sems = refs[6 + 2 * nx:]
        if exchange:
            @pl.when((pl.program_id(0) == 0) & (pl.program_id(1) == 0))
            def _():
                exchange.start(x_in, x_out, x_sems)

        hv = h_ref[...]
        a = _dot(hv, wa_ref[...])
        b = _dot(hv, wb_ref[...])
        a_ref[...] = a.astype(BF16)
        b_ref[...] = b.astype(BF16)
        s_ref[...] = (a * _sigmoid(a) * b).astype(BF16)

        if exchange:
            @pl.when((pl.program_id(0) == nj - 1) & (pl.program_id(1) == ni - 1))
            def _():
                exchange.finish(x_in, x_out, x_sems)

    out = pl.BlockSpec((tm, tn), lambda j, i: (i, j))
    hbm = pl.BlockSpec(memory_space=pltpu.HBM)
    sem = ("arbitrary", "arbitrary") if exchange else ("parallel", "parallel")
    res = pl.pallas_call(
        body, name=name, grid=(nj, ni),
        in_specs=[pl.BlockSpec((tm, D), lambda j, i: (i, 0)),
                  pl.BlockSpec((D, tn), lambda j, i: (0, j)),
                  pl.BlockSpec((D, tn), lambda j, i: (0, j + nj))] + [hbm] * nx,
        out_specs=[out, out, out] + [hbm] * nx,
        out_shape=[jax.ShapeDtypeStruct((n, ff), BF16)] * 3 + (exchange.out_shapes if exchange else []),
        scratch_shapes=exchange.scratch if exchange else [],
        compiler_params=_params(sem),
    )(hn, w13, w13, *(exchange.arrays if exchange else []))
    return res[:3], list(res[3:])


def mm_nn(a_list, w, row_offs, name, out_dtype=BF16, res=None):
    n = a_list[0].shape[0]
    nout = w.shape[1]
    ks = [a.shape[1] for a in a_list]
    tm = _tile(n, (512, 256))
    tn = _tile(nout, (1024, 512, 256, 128))
    na = len(a_list)

    def body(*refs):
        a_refs = refs[:na]
        w_refs = refs[na:2 * na]
        acc = _dot(a_refs[0][...], w_refs[0][...])
        for k in range(1, na):
            acc += _dot(a_refs[k][...], w_refs[k][...])
        if res is None:
            refs[2 * na][...] = acc.astype(out_dtype)
        else:
            h_ref, m_ref, hn_ref, y_ref = refs[2 * na:]
            i_gate, coef = res[2], res[3]
            hn_ref[...] = h_ref[...] + (coef * m_ref[i_gate:i_gate + 1, :]) * acc
            y_ref[...] = acc.astype(BF16)

    in_specs = [pl.BlockSpec((tm, k), lambda j, i: (i, 0)) for k in ks]
    for k, off in zip(ks, row_offs):
        in_specs.append(pl.BlockSpec((k, tn), functools.partial(lambda j, i, ob: (ob, j), ob=off // k)))
    args = list(a_list) + [w] * na
    out = pl.BlockSpec((tm, tn), lambda j, i: (i, j))
    if res is None:
        out_specs = out
        out_shape = jax.ShapeDtypeStruct((n, nout), out_dtype)
    else:
        in_specs += [out, pl.BlockSpec((N_MOD, tn), lambda j, i: (0, j))]
        args += [res[0], res[1]]
        out_specs = [out, out]
        out_shape = [jax.ShapeDtypeStruct((n, nout), F32), jax.ShapeDtypeStruct((n, nout), BF16)]
    return pl.pallas_call(
        body, name=name, grid=(nout // tn, n // tm),
        in_specs=in_specs, out_specs=out_specs, out_shape=out_shape,
        compiler_params=_params(("parallel", "parallel")),
    )(*args)


def mm_nt(g_list, w, col_offs, name, out_dtype=F32, dswiglu=None):
    n = g_list[0].shape[0]
    nout = w.shape[0]
    kg = g_list[0].shape[1]
    tm = _tile(n, (512, 256))
    tn = _tile(nout, (1408, 1024, 512, 256, 128))
    tk = _tile(kg, (1408, 1024, 512, 256, 128))
    ng = len(g_list)
    nk = kg // tk

    def body(*refs):
        g_refs = refs[:ng]
        w_refs = refs[ng:2 * ng]
        rest = refs[2 * ng:]
        acc_ref = rest[-1]
        k = pl.program_id(2)

        @pl.when(k == 0)
        def _():
            acc_ref[...] = jnp.zeros_like(acc_ref)

        acc = _dot_nt(g_refs[0][...], w_refs[0][...])
        for q in range(1, ng):
            acc += _dot_nt(g_refs[q][...], w_refs[q][...])
        acc_ref[...] += acc

        @pl.when(k == nk - 1)
        def _():
            r = acc_ref[...]
            if dswiglu is None:
                rest[0][...] = r.astype(out_dtype)
            else:
                a_ref, b_ref, da_ref, db_ref = rest[:4]
                a = a_ref[...].astype(F32)
                sig = _sigmoid(a)
                da_ref[...] = (r * b_ref[...].astype(F32) * (sig * (1.0 + a * (1.0 - sig)))).astype(BF16)
                db_ref[...] = (r * (a * sig)).astype(BF16)

    in_specs = [pl.BlockSpec((tm, tk), lambda j, i, k: (i, k)) for _ in g_list]
    for off in col_offs:
        in_specs.append(pl.BlockSpec((tn, tk), functools.partial(lambda j, i, k, ob: (j, ob + k), ob=off // tk)))
    args = list(g_list) + [w] * ng
    out = pl.BlockSpec((tm, tn), lambda j, i, k: (i, j))
    if dswiglu is None:
        out_specs = out
        out_shape = jax.ShapeDtypeStruct((n, nout), out_dtype)
    else:
        in_specs += [out, out]
        args += list(dswiglu)
        out_specs = [out, out]
        out_shape = [jax.ShapeDtypeStruct((n, nout), BF16)] * 2
    return pl.pallas_call(
        body, name=name, grid=(nout // tn, n // tm, nk),
        in_specs=in_specs, out_specs=out_specs, out_shape=out_shape,
        scratch_shapes=[pltpu.VMEM((tm, tn), F32)],
        compiler_params=_params(("parallel", "parallel", "arbitrary")),
    )(*args)


def mm_tn(a, g, name, acc=None):
    n, ka = a.shape
    ngc = g.shape[1]
    tka = _tile(ka, (1408, 1024, 512, 256, 128))
    tng = _tile(ngc, (1408, 1024, 512, 256, 128))
    tr = _tile(n, (512, 256))
    has_acc = acc is not None

    def body(*refs):
        a_ref, g_ref = refs[0], refs[1]
        o_ref = refs[-1]
        r = pl.program_id(2)

        @pl.when(r == 0)
        def _():
            if has_acc:
                o_ref[...] = refs[2][...]
            else:
                o_ref[...] = jnp.zeros_like(o_ref)

        o_ref[...] += _dot_tn(a_ref[...], g_ref[...])

    out = pl.BlockSpec((tka, tng), lambda p, q, r: (p, q))
    in_specs = [pl.BlockSpec((tr, tka), lambda p, q, r: (r, p)),
                pl.BlockSpec((tr, tng), lambda p, q, r: (r, q))]
    args = [a, g]
    if has_acc:
        in_specs.append(out)
        args.append(acc)
    return pl.pallas_call(
        body, name=name, grid=(ka // tka, ngc // tng, n // tr),
        in_specs=in_specs, out_specs=out,
        out_shape=jax.ShapeDtypeStruct((ka, ngc), F32),
        compiler_params=_params(("parallel", "parallel", "arbitrary")),
    )(*args)


def mm_small(a, b, name, trans_b=False):
    m = a.shape[0]
    nout = b.shape[0] if trans_b else b.shape[1]

    def body(a_ref, b_ref, o_ref):
        if trans_b:
            o_ref[...] = lax.dot_general(a_ref[...], b_ref[...], NT_DIMS, precision=HI, preferred_element_type=F32)
        else:
            o_ref[...] = jnp.dot(a_ref[...], b_ref[...], precision=HI, preferred_element_type=F32)

    return pl.pallas_call(
        body, name=name,
        out_shape=jax.ShapeDtypeStruct((m, nout), F32),
        compiler_params=pltpu.CompilerParams(vmem_limit_bytes=VMEM_LIMIT),
    )(a, b)


def _col_tables():
    col = np.arange(GRID_W)
    start = np.clip(col - KW // 2, 0, GRID_W - KW)
    ok = (col[None, :] >= start[:, None]) & (col[None, :] < start[:, None] + KW)
    ci = np.clip(col[None, :] - col[:, None] + (KW - 1), 0, 2 * KW - 2)
    e = np.zeros((2 * KW - 1, GRID_W, GRID_W), np.float32)
    for c in range(2 * KW - 1):
        e[c] = (ci == c) & ok
    return e.reshape(2 * KW - 1, GRID_W * GRID_W), ok


def bias_table(rpb):
    e, ok = _col_tables()
    e_pad = np.zeros((32, GRID_W * GRID_W), np.float32)
    e_pad[:31] = e
    rp = jnp.pad(rpb.reshape(HEADS * 15, 31), ((0, 0), (0, 1)))
    t = mm_small(rp, jnp.asarray(e_pad), "rpb_expand").reshape(HEADS, 15, GRID_W, GRID_W)
    t = jnp.where(jnp.asarray(ok)[None, None], t, NEG_INF)
    tab = jnp.stack([t[:, v:v + KH] for v in range(8)], axis=0)
    return tab.transpose(0, 1, 3, 2, 4).reshape(TAB_SHAPE)


def bias_table_bwd(dtab):
    e, _ = _col_tables()
    e_pad = np.zeros((128, GRID_W * GRID_W), np.float32)
    e_pad[:31] = e
    d = dtab.reshape(8, HEADS, GRID_W, KH, GRID_W).transpose(0, 1, 3, 2, 4).reshape(8 * HEADS * KH, GRID_W * GRID_W)
    gv = mm_small(d, jnp.asarray(e_pad), "rpb_reduce", trans_b=True)[:, :31]
    gv = gv.reshape(8, HEADS, KH, 31).transpose(0, 2, 1, 3).reshape(8 * KH, HEADS * 31)
    sel = np.zeros((16, 8 * KH), np.float32)
    for v in range(8):
        for j in range(KH):
            sel[v + j, v * KH + j] = 1.0
    gpad = jnp.pad(gv, ((0, 0), (0, 256 - HEADS * 31)))
    out = mm_small(jnp.asarray(sel), gpad, "rpb_fold")[:15, :HEADS * 31]
    return out.reshape(15, HEADS, 31).transpose(1, 0, 2)


def _attn_geometry(seq):
    rows = seq // GRID_W
    nb = rows // QROWS
    return rows, nb


def _stack_heads(t2):
    first = (lax.broadcasted_iota(jnp.int32, (1, 128), 1) // HEAD_DIM) == 0
    zero = jnp.zeros_like(t2)
    return jnp.concatenate([jnp.where(first, t2, zero), jnp.where(first, zero, t2)], axis=0)


def _unstack_heads(t):
    first = (lax.broadcasted_iota(jnp.int32, (1, 128), 1) // HEAD_DIM) == 0
    return jnp.where(first, t[0:GRID_W], t[GRID_W:2 * GRID_W])


TAB_SHAPE = (8, HEADS // 2, 2 * GRID_W, KH * GRID_W)


def attn_fwd(qkvu, qkvu_c, tab, name, exchange=None):
    seq = qkvu.shape[0]
    nctx = qkvu_c.shape[0]
    rows, nb = _attn_geometry(seq)
    qt = QROWS * GRID_W
    wt = WROWS * GRID_W
    scale = HEAD_DIM ** -0.5
    nx = exchange.n if exchange else 0

    def wb0(i):
        return jnp.clip(i - 1, 0, nb - 3)

    def body(*refs):
        q_ref, k0, k1, k2, v0, v1, v2, kc_ref, vc_ref, tab_hbm = refs[:10]
        x_in = refs[10:10 + nx]
        o_ref = refs[10 + nx]
        x_out = refs[11 + nx:11 + 2 * nx]
        kbuf, vbuf, tab_s, sem = refs[11 + 2 * nx:15 + 2 * nx]
        x_sems = refs[15 + 2 * nx:]
        i = pl.program_id(0)

        @pl.when(i == 0)
        def _():
            if exchange:
                exchange.start(x_in, x_out, x_sems)
            cp = pltpu.make_async_copy(tab_hbm, tab_s, sem)
            cp.start()
            cp.wait()

        for t, (kr, vr) in enumerate(((k0, v0), (k1, v1), (k2, v2))):
            kbuf[t * qt:(t + 1) * qt, :] = kr[...]
            vbuf[t * qt:(t + 1) * qt, :] = vr[...]
        base = wb0(i) * QROWS

        def row_body(rl, carry):
            r = i * QROWS + rl
            rs = jnp.clip(r - KH // 2, 0, rows - KH)
            vi = rs - r + (KH - 1)
            off = pl.multiple_of((rs - base) * GRID_W, GRID_W)
            qoff = pl.multiple_of(rl * GRID_W, GRID_W)
            for p in range(HEADS // 2):
                ls = slice(p * 128, (p + 1) * 128)
                qst = _stack_heads(q_ref[pl.ds(qoff, GRID_W), ls])
                k2v = kbuf[pl.ds(off, KH * GRID_W), ls]
                v2v = vbuf[pl.ds(off, KH * GRID_W), ls]
                s_w = _dot_nt(qst, k2v) * scale + tab_s[vi, p]
                s_c = _dot_nt(qst, kc_ref[:, ls]) * scale
                m = jnp.maximum(jnp.max(s_w, axis=-1, keepdims=True), jnp.max(s_c, axis=-1, keepdims=True))
                pw = jnp.exp(s_w - m)
                pc = jnp.exp(s_c - m)
                l = jnp.sum(pw, axis=-1, keepdims=True) + jnp.sum(pc, axis=-1, keepdims=True)
                o = _dot(pw.astype(BF16), v2v) + _dot(pc.astype(BF16), vc_ref[:, ls])
                o_ref[pl.ds(qoff, GRID_W), ls] = _unstack_heads(o * (1.0 / l)).astype(BF16)
            return carry

        lax.fori_loop(0, QROWS, row_body, 0)

        if exchange:
            @pl.when(i == nb - 1)
            def _():
                exchange.finish(x_in, x_out, x_sems)

    blk = lambda col: [pl.BlockSpec((qt, NA_W), functools.partial(lambda i, t, c: (wb0(i) + t, c), t=t, c=col))
                       for t in range(3)]
    hbm = pl.BlockSpec(memory_space=pltpu.HBM)
    res = pl.pallas_call(
        body, name=name, grid=(nb,),
        in_specs=[pl.BlockSpec((qt, NA_W), lambda i: (i, 0))] + blk(1) + blk(2)
                 + [pl.BlockSpec((nctx, NA_W), lambda i: (0, 1)), pl.BlockSpec((nctx, NA_W), lambda i: (0, 2)),
                    pl.BlockSpec(memory_space=pl.ANY)] + [hbm] * nx,
        out_specs=[pl.BlockSpec((qt, NA_W), lambda i: (i, 0))] + [hbm] * nx,
        out_shape=[jax.ShapeDtypeStruct((seq, NA_W), BF16)] + (exchange.out_shapes if exchange else []),
        scratch_shapes=[pltpu.VMEM((wt, NA_W), BF16), pltpu.VMEM((wt, NA_W), BF16),
                        pltpu.VMEM(TAB_SHAPE, F32), pltpu.SemaphoreType.DMA] + (exchange.scratch if exchange else []),
        compiler_params=_params(("arbitrary",)),
    )(qkvu, qkvu, qkvu, qkvu, qkvu, qkvu, qkvu, qkvu_c, qkvu_c, tab, *(exchange.arrays if exchange else []))
    return res[0], list(res[1:])


def attn_bwd(qkvu, qkvu_c, tab, dmix, name, exchange=None):
    seq = qkvu.shape[0]
    nctx = qkvu_c.shape[0]
    rows, nb = _attn_geometry(seq)
    qt = QROWS * GRID_W
    wt = WROWS * GRID_W
    scale = HEAD_DIM ** -0.5
    nx = exchange.n if exchange else 0

    def wb0(i):
        return jnp.clip(i - 1, 0, nb - 3)

    def body(*refs):
        q_ref, k0, k1, k2, v0, v1, v2, kc_ref, vc_ref, do_ref, tab_hbm = refs[:11]
        x_in = refs[11:11 + nx]
        dq_ref, dk_hbm, dv_hbm, dkc_ref, dvc_ref, dtab_hbm = refs[11 + nx:17 + nx]
        x_out = refs[17 + nx:17 + 2 * nx]
        kbuf, vbuf, dkacc, dvacc, tab_s, dtab_s, sem = refs[17 + 2 * nx:24 + 2 * nx]
        x_sems = refs[24 + 2 * nx:]
        i = pl.program_id(0)

        if exchange:
            @pl.when(i == 0)
            def _():
                exchange.start(x_in, x_out, x_sems)

        def flush(src, dst, src_row, dst_row, nrows):
            cp = pltpu.make_async_copy(src.at[pl.ds(src_row, nrows)], dst.at[pl.ds(dst_row, nrows)], sem)
            cp.start()
            cp.wait()

        @pl.when(i == 0)
        def _():
            cp = pltpu.make_async_copy(tab_hbm, tab_s, sem)
            cp.start()
            cp.wait()
            dtab_s[...] = jnp.zeros_like(dtab_s)
            dkacc[...] = jnp.zeros_like(dkacc)
            dvacc[...] = jnp.zeros_like(dvacc)
            dkc_ref[...] = jnp.zeros_like(dkc_ref)
            dvc_ref[...] = jnp.zeros_like(dvc_ref)

        @pl.when((i >= 2) & (i <= nb - 2))
        def _():
            dst_row = pl.multiple_of((i - 2) * qt, qt)
            for acc_ref, dst in ((dkacc, dk_hbm), (dvacc, dv_hbm)):
                flush(acc_ref, dst, 0, dst_row, qt)
                acc_ref[0:qt, :] = acc_ref[qt:2 * qt, :]
                acc_ref[qt:2 * qt, :] = acc_ref[2 * qt:3 * qt, :]
                acc_ref[2 * qt:3 * qt, :] = jnp.zeros((qt, NA_W), F32)

        for t, (kr, vr) in enumerate(((k0, v0), (k1, v1), (k2, v2))):
            kbuf[t * qt:(t + 1) * qt, :] = kr[...]
            vbuf[t * qt:(t + 1) * qt, :] = vr[...]
        base = wb0(i) * QROWS

        def row_body(rl, carry):
            r = i * QROWS + rl
            rs = jnp.clip(r - KH // 2, 0, rows - KH)
            vi = rs - r + (KH - 1)
            off = pl.multiple_of((rs - base) * GRID_W, GRID_W)
            qoff = pl.multiple_of(rl * GRID_W, GRID_W)
            for p in range(HEADS // 2):
                ls = slice(p * 128, (p + 1) * 128)
                qst = _stack_heads(q_ref[pl.ds(qoff, GRID_W), ls])
                dost = _stack_heads(do_ref[pl.ds(qoff, GRID_W), ls])
                k2v = kbuf[pl.ds(off, KH * GRID_W), ls]
                v2v = vbuf[pl.ds(off, KH * GRID_W), ls]
                kc2 = kc_ref[:, ls]
                vc2 = vc_ref[:, ls]
                s_w = _dot_nt(qst, k2v) * scale + tab_s[vi, p]
                s_c = _dot_nt(qst, kc2) * scale
                m = jnp.maximum(jnp.max(s_w, axis=-1, keepdims=True), jnp.max(s_c, axis=-1, keepdims=True))
                pw = jnp.exp(s_w - m)
                pc = jnp.exp(s_c - m)
                inv = 1.0 / (jnp.sum(pw, axis=-1, keepdims=True) + jnp.sum(pc, axis=-1, keepdims=True))
                pw = pw * inv
                pc = pc * inv
                dpw = _dot_nt(dost, v2v)
                dpc = _dot_nt(dost, vc2)
                delta = jnp.sum(pw * dpw, axis=-1, keepdims=True) + jnp.sum(pc * dpc, axis=-1, keepdims=True)
                ds_w = pw * (dpw - delta)
                ds_c = pc * (dpc - delta)
                dtab_s[vi, p] += ds_w
                dsw16 = ds_w.astype(BF16)
                dsc16 = ds_c.astype(BF16)
                dq = (_dot(dsw16, k2v) + _dot(dsc16, kc2)) * scale
                dq_ref[pl.ds(qoff, GRID_W), ls] = _unstack_heads(dq).astype(BF16)
                dkacc[pl.ds(off, KH * GRID_W), ls] += _dot_tn(dsw16, qst) * scale
                dvacc[pl.ds(off, KH * GRID_W), ls] += _dot_tn(pw.astype(BF16), dost)
                dkc_ref[:, ls] += _dot_tn(dsc16, qst) * scale
                dvc_ref[:, ls] += _dot_tn(pc.astype(BF16), dost)
            return carry

        lax.fori_loop(0, QROWS, row_body, 0)

        @pl.when(i == nb - 1)
        def _():
            dst_row = pl.multiple_of((nb - 3) * qt, qt)
            flush(dkacc, dk_hbm, 0, dst_row, wt)
            flush(dvacc, dv_hbm, 0, dst_row, wt)
            cp = pltpu.make_async_copy(dtab_s, dtab_hbm, sem)
            cp.start()
            cp.wait()
            if exchange:
                exchange.finish(x_in, x_out, x_sems)

    blk = lambda col: [pl.BlockSpec((qt, NA_W), functools.partial(lambda i, t, c: (wb0(i) + t, c), t=t, c=col))
                       for t in range(3)]
    any_spec = pl.BlockSpec(memory_space=pl.ANY)
    hbm = pl.BlockSpec(memory_space=pltpu.HBM)
    res = pl.pallas_call(
        body, name=name, grid=(nb,),
        in_specs=[pl.BlockSpec((qt, NA_W), lambda i: (i, 0))] + blk(1) + blk(2)
                 + [pl.BlockSpec((nctx, NA_W), lambda i: (0, 1)), pl.BlockSpec((nctx, NA_W), lambda i: (0, 2)),
                    pl.BlockSpec((qt, NA_W), lambda i: (i, 0)), any_spec] + [hbm] * nx,
        out_specs=[pl.BlockSpec((qt, NA_W), lambda i: (i, 0)), any_spec, any_spec,
                   pl.BlockSpec((nctx, NA_W), lambda i: (0, 0)), pl.BlockSpec((nctx, NA_W), lambda i: (0, 0)),
                   any_spec] + [hbm] * nx,
        out_shape=[jax.ShapeDtypeStruct((seq, NA_W), BF16), jax.ShapeDtypeStruct((seq, NA_W), F32),
                   jax.ShapeDtypeStruct((seq, NA_W), F32), jax.ShapeDtypeStruct((nctx, NA_W), F32),
                   jax.ShapeDtypeStruct((nctx, NA_W), F32), jax.ShapeDtypeStruct(TAB_SHAPE, F32)]
                  + (exchange.out_shapes if exchange else []),
        scratch_shapes=[pltpu.VMEM((wt, NA_W), BF16), pltpu.VMEM((wt, NA_W), BF16),
                        pltpu.VMEM((wt, NA_W), F32), pltpu.VMEM((wt, NA_W), F32),
                        pltpu.VMEM(TAB_SHAPE, F32), pltpu.VMEM(TAB_SHAPE, F32), pltpu.SemaphoreType.DMA]
                       + (exchange.scratch if exchange else []),
        compiler_params=_params(("arbitrary",)),
    )(qkvu, qkvu, qkvu, qkvu, qkvu, qkvu, qkvu, qkvu_c, qkvu_c, dmix, tab, *(exchange.arrays if exchange else []))
    return res[:6], list(res[6:])


def _halo_specs(te, seq, col, width):
    per = te // HALO
    last = seq // HALO - 1
    return [pl.BlockSpec((HALO, width), lambda i: (jnp.maximum(i * per - 1, 0), col)),
            pl.BlockSpec((te, width), lambda i: (i, col)),
            pl.BlockSpec((HALO, width), lambda i: (jnp.minimum((i + 1) * per, last), col))]


def _extended(prev_ref, cur_ref, next_ref, i, te, seq):
    xe = jnp.concatenate([prev_ref[...], cur_ref[...], next_ref[...]], axis=0).astype(F32)
    pos = i * te - HALO + lax.broadcasted_iota(jnp.int32, (te + 2 * HALO, 1), 0)
    return jnp.where((pos >= 0) & (pos < seq), xe, 0.0), pos


def _window_sum(x, levels, n, mirrored):
    first = (n - 1) if mirrored else 1
    acc = x + pltpu.roll(x, first, 0)
    step = 1
    for _ in range(levels - 1):
        acc = pltpu.roll(acc, step, 0) + pltpu.roll(acc, n - step, 0)
        step *= 2
    return acc


def _window_count(pos, w, seq):
    lo = jnp.clip(pos - w // 2, 0, seq)
    hi = jnp.clip(pos - w // 2 + w, 0, seq)
    return jnp.maximum(hi - lo, 1).astype(F32)


def pool_fwd(qkvu, pool_w, pool_scale, name):
    seq = qkvu.shape[0]
    te = _tile(seq, (512, 256))
    n = te + 2 * HALO

    def body(up_ref, uc_ref, un_ref, w_ref, sc_ref, o_ref):
        i = pl.program_id(0)
        xe, pos = _extended(up_ref, uc_ref, un_ref, i, te, seq)
        cnt = pos[HALO:HALO + te]
        for g, w in enumerate(POOL_WINDOWS):
            ls = slice(g * POOL_G, (g + 1) * POOL_G)
            xg = xe[:, ls]
            win = _window_sum(xg, g + 1, n, False)[HALO:HALO + te]
            dlt = win / _window_count(cnt, w, seq) - xg[HALO:HALO + te]
            z = _dot(dlt.astype(BF16), w_ref[g])
            o_ref[:, ls] = (z * sc_ref[:, ls]).astype(BF16)

    return pl.pallas_call(
        body, name=name, grid=(seq // te,),
        in_specs=_halo_specs(te, seq, 3, POOL_W)
                 + [pl.BlockSpec((4, POOL_G, POOL_G), lambda i: (0, 0, 0)), pl.BlockSpec((1, POOL_W), lambda i: (0, 0))],
        out_specs=pl.BlockSpec((te, POOL_W), lambda i: (i, 0)),
        out_shape=jax.ShapeDtypeStruct((seq, POOL_W), BF16),
        compiler_params=_params(("parallel",)),
    )(qkvu, qkvu, qkvu, pool_w, pool_scale)


def pool_bwd(qkvu, dmix, pool_w, pool_scale, name):
    seq = qkvu.shape[0]
    te = _tile(seq, (512, 256))
    n = te + 2 * HALO

    def body(up_ref, uc_ref, un_ref, dp_ref, dc_ref, dn_ref, w_ref, sc_ref, du_ref, dw_ref, dsc_ref):
        i = pl.program_id(0)

        @pl.when(i == 0)
        def _():
            dw_ref[...] = jnp.zeros_like(dw_ref)
            dsc_ref[...] = jnp.zeros_like(dsc_ref)

        xe, pos = _extended(up_ref, uc_ref, un_ref, i, te, seq)
        de, _ = _extended(dp_ref, dc_ref, dn_ref, i, te, seq)
        cpos = pos[HALO:HALO + te]
        for g, w in enumerate(POOL_WINDOWS):
            ls = slice(g * POOL_G, (g + 1) * POOL_G)
            xg = xe[:, ls]
            wg = w_ref[g]
            win = _window_sum(xg, g + 1, n, False)[HALO:HALO + te]
            dlt = (win / _window_count(cpos, w, seq) - xg[HALO:HALO + te]).astype(BF16)
            z = _dot(dlt, wg)
            dpg = de[:, ls]
            dsc_ref[:, ls] += jnp.sum(dpg[HALO:HALO + te] * z, axis=0, keepdims=True)
            dz = (dpg * sc_ref[:, ls]).astype(BF16)
            dw_ref[g] += _dot_tn(dlt, dz[HALO:HALO + te])
            dd = _dot_nt(dz, wg)
            back = _window_sum(dd / _window_count(pos, w, seq), g + 1, n, True)
            du_ref[:, ls] = (back[HALO:HALO + te] - dd[HALO:HALO + te]).astype(BF16)

    return pl.pallas_call(
        body, name=name, grid=(seq // te,),
        in_specs=_halo_specs(te, seq, 3, POOL_W) + _halo_specs(te, seq, 1, POOL_W)
                 + [pl.BlockSpec((4, POOL_G, POOL_G), lambda i: (0, 0, 0)), pl.BlockSpec((1, POOL_W), lambda i: (0, 0))],
        out_specs=[pl.BlockSpec((te, POOL_W), lambda i: (i, 0)),
                   pl.BlockSpec((4, POOL_G, POOL_G), lambda i: (0, 0, 0)), pl.BlockSpec((1, POOL_W), lambda i: (0, 0))],
        out_shape=[jax.ShapeDtypeStruct((seq, POOL_W), BF16), jax.ShapeDtypeStruct((4, POOL_G, POOL_G), F32),
                   jax.ShapeDtypeStruct((1, POOL_W), F32)],
        compiler_params=_params(("arbitrary",)),
    )(qkvu, qkvu, qkvu, dmix, dmix, dmix, pool_w, pool_scale)


def _shifted(z, zprev_row, znext_row, te):
    rows = lax.broadcasted_iota(jnp.int32, (te, 1), 0)
    zp = jnp.where(rows == 0, zprev_row, pltpu.roll(z, 1, 0))
    zn = jnp.where(rows == te - 1, znext_row, pltpu.roll(z, te - 1, 0))
    return zp, zn


def _edge_rows(prev_ref, next_ref, i, nt):
    p = prev_ref[HALO - 1:HALO, :].astype(F32)
    q = next_ref[0:1, :].astype(F32)
    return jnp.where(i == 0, 0.0, p), jnp.where(i == nt - 1, 0.0, q)


def conv_fwd(proj, conv_w, name):
    seq = proj.shape[0]
    te = _tile(seq, (512, 256))
    nt = seq // te

    def body(bg_ref, cp_ref, cc_ref, cn_ref, xp_ref, xc_ref, xn_ref, w_ref, o_ref):
        i = pl.program_id(0)
        z = cc_ref[...].astype(F32) * xc_ref[...].astype(F32)
        cpr, cnr = _edge_rows(cp_ref, cn_ref, i, nt)
        xpr, xnr = _edge_rows(xp_ref, xn_ref, i, nt)
        zp, zn = _shifted(z, cpr * xpr, cnr * xnr, te)
        y = zp * w_ref[0:1, :] + z * w_ref[1:2, :] + zn * w_ref[2:3, :]
        o_ref[...] = (bg_ref[...].astype(F32) * y).astype(BF16)

    return pl.pallas_call(
        body, name=name, grid=(nt,),
        in_specs=[pl.BlockSpec((te, D), lambda i: (i, 0))] + _halo_specs(te, seq, 1, D) + _halo_specs(te, seq, 2, D)
                 + [pl.BlockSpec((3, D), lambda i: (0, 0))],
        out_specs=pl.BlockSpec((te, D), lambda i: (i, 0)),
        out_shape=jax.ShapeDtypeStruct((seq, D), BF16),
        compiler_params=_params(("parallel",)),
    )(proj, proj, proj, proj, proj, proj, proj, conv_w)


def conv_bwd(proj, dgm, conv_w, name):
    seq = proj.shape[0]
    te = _tile(seq, (512, 256))
    nt = seq // te

    def body(bp_ref, bc_ref, bn_ref, cp_ref, cc_ref, cn_ref, xp_ref, xc_ref, xn_ref, gp_ref, gc_ref, gn_ref, w_ref,
             dbg_ref, dcg_ref, dxin_ref, dw_ref):
        i = pl.program_id(0)

        @pl.when(i == 0)
        def _():
            dw_ref[...] = jnp.zeros_like(dw_ref)

        bg = bc_ref[...].astype(F32)
        cg = cc_ref[...].astype(F32)
        xin = xc_ref[...].astype(F32)
        dg = gc_ref[...].astype(F32)
        z = cg * xin
        cpr, cnr = _edge_rows(cp_ref, cn_ref, i, nt)
        xpr, xnr = _edge_rows(xp_ref, xn_ref, i, nt)
        zp, zn = _shifted(z, cpr * xpr, cnr * xnr, te)
        w0, w1, w2 = w_ref[0:1, :], w_ref[1:2, :], w_ref[2:3, :]
        y = zp * w0 + z * w1 + zn * w2
        dbg_ref[...] = (dg * y).astype(BF16)
        dy = dg * bg
        dw_ref[0:1, :] += jnp.sum(dy * zp, axis=0, keepdims=True)
        dw_ref[1:2, :] += jnp.sum(dy * z, axis=0, keepdims=True)
        dw_ref[2:3, :] += jnp.sum(dy * zn, axis=0, keepdims=True)
        bpr, bnr = _edge_rows(bp_ref, bn_ref, i, nt)
        gpr, gnr = _edge_rows(gp_ref, gn_ref, i, nt)
        dyp, dyn = _shifted(dy, bpr * gpr, bnr * gnr, te)
        dz = dyn * w0 + dy * w1 + dyp * w2
        dcg_ref[...] = (dz * xin).astype(BF16)
        dxin_ref[...] = (dz * cg).astype(BF16)

    row = pl.BlockSpec((te, D), lambda i: (i, 0))
    return pl.pallas_call(
        body, name=name, grid=(nt,),
        in_specs=_halo_specs(te, seq, 0, D) + _halo_specs(te, seq, 1, D) + _halo_specs(te, seq, 2, D)
                 + _halo_specs(te, seq, 0, D) + [pl.BlockSpec((3, D), lambda i: (0, 0))],
        out_specs=[row, row, row, pl.BlockSpec((3, D), lambda i: (0, 0))],
        out_shape=[jax.ShapeDtypeStruct((seq, D), BF16)] * 3 + [jax.ShapeDtypeStruct((3, D), F32)],
        compiler_params=_params(("arbitrary",)),
    )(proj, proj, proj, proj, proj, proj, proj, proj, proj, dgm, dgm, dgm, conv_w)


def _position():
    x, y, c = lax.axis_index("x"), lax.axis_index("y"), lax.axis_index("c")
    return x, y, c, 4 * x + 2 * y + c


def _peer(x, y, c, j):
    px = 1 - x if j & 4 else x
    py = 1 - y if j & 2 else y
    pc = 1 - c if j & 1 else c
    return (px, py, pc), 4 * px + 2 * py + pc


def small_allgather(v, name):
    rows, cols = v.shape

    def body(v_ref, o_ref, send_sems, recv_sems, local_sem):
        x, y, c, me = _position()
        mine = pltpu.make_async_copy(v_ref, o_ref.at[me], local_sem)
        mine.start()
        sends = []
        for j in range(1, N_DEV):
            peer, _ = _peer(x, y, c, j)
            cp = pltpu.make_async_remote_copy(src_ref=v_ref, dst_ref=o_ref.at[me], send_sem=send_sems.at[j - 1],
                                              recv_sem=recv_sems.at[j - 1], device_id=peer, device_id_type=MESH_ID)
            cp.start()
            sends.append(cp)
        for j in range(1, N_DEV):
            peer, pid = _peer(x, y, c, j)
            pltpu.make_async_remote_copy(src_ref=v_ref, dst_ref=o_ref.at[pid], send_sem=send_sems.at[j - 1],
                                         recv_sem=recv_sems.at[j - 1], device_id=peer,
                                         device_id_type=MESH_ID).wait_recv()
        for cp in sends:
            cp.wait_send()
        mine.wait()

    return pl.pallas_call(
        body, name=name,
        out_shape=jax.ShapeDtypeStruct((N_DEV, rows, cols), v.dtype),
        in_specs=[pl.BlockSpec(memory_space=pltpu.VMEM)],
        out_specs=pl.BlockSpec(memory_space=pltpu.VMEM),
        scratch_shapes=[pltpu.SemaphoreType.DMA((N_DEV - 1,)), pltpu.SemaphoreType.DMA((N_DEV - 1,)),
                        pltpu.SemaphoreType.DMA],
        compiler_params=pltpu.CompilerParams(vmem_limit_bytes=VMEM_LIMIT),
    )(v)


class Exchange:
    def __init__(self, kind, arrays):
        self.kind, self.arrays, self.n = kind, list(arrays), len(arrays)
        n = self.n
        if kind == "gather":
            self.out_shapes = [jax.ShapeDtypeStruct((N_DEV,) + a.shape, a.dtype) for a in self.arrays]
        else:
            self.out_shapes = [jax.ShapeDtypeStruct(a.shape, a.dtype) for a in self.arrays]
        self.scratch = [pltpu.SemaphoreType.DMA((7 * n,)), pltpu.SemaphoreType.DMA((7 * n,)),
                        pltpu.SemaphoreType.DMA((n,))]

    def _gather_copies(self, ins, outs, sems):
        send_sems, recv_sems, local_sems = sems
        x, y, c, me = _position()
        chips = [(1 - x, y), (x, 1 - y), (1 - x, 1 - y)]

        def blk(k, px, py, pc):
            return outs[k].at[4 * px + 2 * py + pc]

        def copy(k, slot, block, to, src=None):
            return pltpu.make_async_remote_copy(
                src_ref=blk(k, *block) if src is None else src, dst_ref=blk(k, *block),
                send_sem=send_sems.at[k * 7 + slot], recv_sem=recv_sems.at[k * 7 + slot],
                device_id=to, device_id_type=MESH_ID)

        mine = [pltpu.make_async_copy(ins[k], blk(k, x, y, c), local_sems.at[k]) for k in range(self.n)]
        first = []
        for k in range(self.n):
            first.append(copy(k, 0, (x, y, c), (x, y, 1 - c), src=ins[k]))
            first += [copy(k, 1 + j, (x, y, c), (*chip, c), src=ins[k]) for j, chip in enumerate(chips)]
        return (x, y, c), chips, copy, mine, first

    def start(self, ins, outs, sems):
        if self.kind == "gather":
            _, _, _, mine, first = self._gather_copies(ins, outs, sems)
            for cp in mine + first:
                cp.start()
        else:
            for cp in self._scatter_copies(ins, outs, sems, False):
                cp.start()

    def finish(self, ins, outs, sems):
        if self.kind == "gather":
            (x, y, c), chips, copy, mine, first = self._gather_copies(ins, outs, sems)
            passed = []
            for j, chip in enumerate(chips):
                for k in range(self.n):
                    copy(k, 1 + j, (*chip, c), (x, y, c)).wait_recv()
                    cp = copy(k, 4 + j, (*chip, c), (x, y, 1 - c))
                    cp.start()
                    passed.append(cp)
            for k in range(self.n):
                copy(k, 0, (x, y, 1 - c), (x, y, c)).wait_recv()
                for j, chip in enumerate(chips):
                    copy(k, 4 + j, (*chip, 1 - c), (x, y, c)).wait_recv()
            for cp in first + passed:
                cp.wait_send()
            for cp in mine:
                cp.wait()
        else:
            for cp in self._scatter_copies(ins, outs, sems, True):
                cp.wait_recv()
            copies = self._scatter_copies(ins, outs, sems, False)
            for cp in copies[self.n:]:
                cp.wait_send()
            for cp in copies[:self.n]:
                cp.wait()

    def _scatter_copies(self, ins, outs, sems, arrivals):
        send_sems, recv_sems, local_sems = sems
        x, y, c, me = _position()
        out = []
        if not arrivals:
            out = [pltpu.make_async_copy(ins[k].at[me], outs[k].at[me], local_sems.at[k]) for k in range(self.n)]
        for j in range(1, N_DEV):
            peer, pid = _peer(x, y, c, j)
            for k in range(self.n):
                out.append(pltpu.make_async_remote_copy(
                    src_ref=ins[k].at[pid], dst_ref=outs[k].at[pid if arrivals else me],
                    send_sem=send_sems.at[k * 7 + j - 1], recv_sem=recv_sems.at[k * 7 + j - 1],
                    device_id=peer, device_id_type=MESH_ID))
        return out

    def run(self, name):
        n = self.n

        def body(*refs):
            ins, outs, sems = refs[:n], refs[n:2 * n], refs[2 * n:]
            self.start(ins, outs, sems)
            self.finish(ins, outs, sems)

        hbm = pl.BlockSpec(memory_space=pltpu.HBM)
        return list(pl.pallas_call(
            body, name=name, out_shape=self.out_shapes, in_specs=[hbm] * n, out_specs=[hbm] * n,
            scratch_shapes=self.scratch,
        )(*self.arrays))


def sum_devices(v, name):
    _, rows, cols = v.shape

    def body(v_ref, o_ref):
        acc = v_ref[0]
        for p in range(1, N_DEV):
            acc = acc + v_ref[p]
        o_ref[...] = acc

    return pl.pallas_call(
        body, name=name, out_shape=jax.ShapeDtypeStruct((rows, cols), F32),
        compiler_params=pltpu.CompilerParams(vmem_limit_bytes=VMEM_LIMIT),
    )(v)


def _silu(x):
    return x * _sigmoid(x)


def adaln_fwd(cm, mod_w, mod_b_cols, name):
    cols = mod_w.shape[2]

    def body(c_ref, w_ref, b_ref, o_ref):
        o_ref[0] = jnp.dot(_silu(c_ref[...]), w_ref[0], precision=HI, preferred_element_type=F32) + b_ref[0]

    return pl.pallas_call(
        body, name=name, grid=(2,),
        in_specs=[pl.BlockSpec((16, D), lambda l: (0, 0)), pl.BlockSpec((1, D, cols), lambda l: (l, 0, 0)),
                  pl.BlockSpec((1, 1, cols), lambda l: (l, 0, 0))],
        out_specs=pl.BlockSpec((1, 16, cols), lambda l: (l, 0, 0)),
        out_shape=jax.ShapeDtypeStruct((2, 16, cols), F32),
        compiler_params=_params(("parallel",)),
    )(cm, mod_w, mod_b_cols)


def adaln_bwd(cm_t, mod_w, dm_t, name):
    cols = mod_w.shape[2]

    def body(c_ref, w_ref, lat_ref, ctx_ref, gw_ref, pc_ref):
        ctot = jnp.sum(ctx_ref[0], axis=0, keepdims=True)
        rows = lax.broadcasted_iota(jnp.int32, (8, 1), 0)
        g_hi = jnp.where(rows == 0, ctot, 0.0)
        g = jnp.concatenate([lat_ref[0], g_hi], axis=0)
        gw_ref[0] = jnp.dot(_silu(c_ref[...]), g, precision=HI, preferred_element_type=F32)
        pc_ref[0] = lax.dot_general(g_hi, w_ref[0], NT_DIMS, precision=HI, preferred_element_type=F32)

    return pl.pallas_call(
        body, name=name, grid=(2,),
        in_specs=[pl.BlockSpec((D, 16), lambda l: (0, 0)), pl.BlockSpec((1, D, cols), lambda l: (l, 0, 0)),
                  pl.BlockSpec((1, 8, cols), lambda l: (l, 0, 0)), pl.BlockSpec((1, 8, cols), lambda l: (l + 2, 0, 0))],
        out_specs=[pl.BlockSpec((1, D, cols), lambda l: (l, 0, 0)), pl.BlockSpec((1, 8, D), lambda l: (l, 0, 0))],
        out_shape=[jax.ShapeDtypeStruct((2, D, cols), F32), jax.ShapeDtypeStruct((2, 8, D), F32)],
        compiler_params=_params(("parallel",)),
    )(cm_t, mod_w, dm_t, dm_t)


def mod_b_grad(dm_t, name):
    width = dm_t.shape[2]
    tn = width // 8

    def body(d_ref, o_ref):
        s = jnp.concatenate([jnp.sum(d_ref[k], axis=0, keepdims=True) for k in range(4)]
                            + [jnp.zeros((4, tn), F32)], axis=0)
        o_ref[...] = s + pltpu.roll(s, 6, 0)

    return pl.pallas_call(
        body, name=name, grid=(8,),
        in_specs=[pl.BlockSpec((4, 8, tn), lambda j: (0, 0, j))],
        out_specs=pl.BlockSpec((8, tn), lambda j: (0, j)),
        out_shape=jax.ShapeDtypeStruct((8, width), F32),
        compiler_params=_params(("parallel",)),
    )(dm_t)


def adamw(w, m, v, name, g=None, recv=None):
    rows, cols = w.shape
    tr = _tile(rows, (256, 128, 64, 32, 16, 8))
    summed = recv is not None

    def body(w_ref, m_ref, v_ref, g_ref, go_ref, d_ref, mo_ref, vo_ref):
        if summed:
            gv = g_ref[0].astype(F32)
            for p in range(1, N_DEV):
                gv = gv + g_ref[p].astype(F32)
        else:
            gv = g_ref[...]
        mn = ADAM_B1 * m_ref[...] + (1.0 - ADAM_B1) * gv
        vn = ADAM_B2 * v_ref[...] + (1.0 - ADAM_B2) * (gv * gv)
        m_hat = mn / (1.0 - ADAM_B1 ** ADAM_STEP)
        v_hat = vn / (1.0 - ADAM_B2 ** ADAM_STEP)
        go_ref[...] = gv
        d_ref[...] = -ADAM_LR * (m_hat / (jnp.sqrt(v_hat) + ADAM_EPS) + ADAM_WD * w_ref[...])
        mo_ref[...] = mn
        vo_ref[...] = vn

    row = pl.BlockSpec((tr, cols), lambda i: (i, 0))
    gspec = pl.BlockSpec((N_DEV, tr, cols), lambda i: (0, i, 0)) if summed else row
    return pl.pallas_call(
        body, name=name, grid=(rows // tr,),
        in_specs=[row, row, row, gspec], out_specs=[row] * 4,
        out_shape=[jax.ShapeDtypeStruct((rows, cols), F32)] * 4,
        compiler_params=_params(("parallel",)),
    )(w, m, v, recv if summed else g)


def _ffn_fwd(h, mods, g, w13, w2, base, tag, exchange=None):
    hn = normmod(h, g, mods, base, base + 1, tag + "_norm")
    (a, b, s), exchanged = ffn_up(hn, w13, tag + "_up", exchange)
    h_new, y = mm_nn([s], w2, [0], tag + "_down", res=(h, mods, base + 2, 0.5))
    saved = (h, hn, a, b, s, y)
    return (h_new, saved, exchanged) if exchange else (h_new, saved)


COLUMN_CUT = ("w13", "ewi", "cwi")
GATHER_FIRST = ("w13_00", "w2_00")
GATHER_IN_FFN = ("ewi", "ewo", "w13_01", "w2_01")
GATHER_IN_ATTN = ("w13_10", "w2_10", "cwi", "cwo", "w13_11", "w2_11")
SCATTER_IN_ATTN = ("w13_11", "w2_11", "cwi", "cwo", "w13_10", "w2_10", "w13_01", "w2_01")
SCATTER_LAST = ("w13_00", "w2_00", "ewi", "ewo")


def unpack_piece(p, g):
    if p.split("_")[0] in COLUMN_CUT:
        return g.transpose(1, 0, 2).reshape(g.shape[1], -1)
    return g.reshape(-1, g.shape[2])


def block_piece(p, full):
    if p.split("_")[0] in COLUMN_CUT:
        return full.reshape(full.shape[0], N_DEV, -1).transpose(1, 0, 2).astype(BF16)
    return full.reshape(N_DEV, -1, full.shape[1]).astype(BF16)


def _ffn_bwd(dout, saved, mods, g, w13, w2, base, tag, acc=None):
    h, hn, a, b, s, y = saved
    ff = w2.shape[0]
    acc = acc or (None, None, None)
    dy, dgate = gate_bwd(dout, y, mods, base + 2, 0.5, tag + "_gate_bwd")
    da, db = mm_nt([dy], w2, [0], tag + "_ds", dswiglu=(a, b))
    dw2 = mm_tn(s, dy, tag + "_dw2", acc=acc[2])
    dhn = mm_nt([da, db], w13, [0, ff], tag + "_dhn")
    dwa = mm_tn(hn, da, tag + "_dw13a", acc=acc[0])
    dwb = mm_tn(hn, db, tag + "_dw13b", acc=acc[1])
    dh, dshift, dscale, dg = normmod_bwd(dhn, h, g, mods, base + 1, dout, tag + "_norm_bwd")
    return dh, (dwa, dwb, dw2), dg, {base: dshift, base + 1: dscale, base + 2: dgate}


def _mod_rows(parts):
    zero = jnp.zeros((1, D), F32)
    return jnp.concatenate([parts.get(k, zero) for k in range(N_MOD)], axis=0)


def local_step(x, ctx, ml, mc, wts, target, shards=None):
    wts = dict(wts)
    ng = wts["norm_g"]
    gvec = lambda l, k: ng[l, k][None, :]
    pool_w16 = wts["pool_w"].astype(BF16)
    grads = {}

    def gather(pieces):
        return Exchange("gather", [shards[p] for p in pieces]) if shards else None

    def arrived(pieces, results):
        for p, g in zip(pieces, results):
            wts[p] = unpack_piece(p, g)

    def ffn_grads(lf, f):
        grads["w13_" + lf] = jnp.concatenate([f[0], f[1]], axis=1)
        grads["w2_" + lf] = f[2]

    if shards:
        x1, sv1, got = _ffn_fwd(x, ml[0], gvec(0, 0), wts["w13_00"], wts["w2_00"], 0, "l0f1", gather(GATHER_IN_FFN))
        arrived(GATHER_IN_FFN, got)
    else:
        x1, sv1 = _ffn_fwd(x, ml[0], gvec(0, 0), wts["w13_00"], wts["w2_00"], 0, "l0f1")
    c1, sv1c = _ffn_fwd(ctx, mc[0], gvec(0, 0), wts["w13_00"], wts["w2_00"], 0, "l0f1c")
    xn = normmod(x1, gvec(0, 1), ml[0], 3, 4, "l0mix_norm")
    cn = normmod(c1, gvec(0, 1), mc[0], 3, 4, "l0mix_norm_c")
    qkvu = mm_nn([xn], wts["ewi"], [0], "l0mix_in")
    qkvu_c = mm_nn([cn], wts["ewi"], [0], "l0mix_in_c")
    tab = bias_table(wts["rpb"])
    att, got = attn_fwd(qkvu, qkvu_c, tab, "l0_attn", gather(GATHER_IN_ATTN))
    arrived(GATHER_IN_ATTN, got)
    pool = pool_fwd(qkvu, pool_w16, wts["pool_scale"], "l0_pool")
    x2, ymix = mm_nn([att, pool], wts["ewo"], [0, NA_W], "l0mix_out", res=(x1, ml[0], 5, 1.0))
    x3, sv2 = _ffn_fwd(x2, ml[0], gvec(0, 2), wts["w13_01"], wts["w2_01"], 6, "l0f2")

    x4, sv3 = _ffn_fwd(x3, ml[1], gvec(1, 0), wts["w13_10"], wts["w2_10"], 0, "l1f1")
    xn1 = normmod(x4, gvec(1, 1), ml[1], 3, 4, "l1mix_norm")
    proj = mm_nn([xn1], wts["cwi"], [0], "l1mix_in")
    gm = conv_fwd(proj, wts["conv_w"], "l1_conv")
    x5, ycv = mm_nn([gm], wts["cwo"], [0], "l1mix_out", res=(x4, ml[1], 5, 1.0))
    x6, sv4 = _ffn_fwd(x5, ml[1], gvec(1, 2), wts["w13_11"], wts["w2_11"], 6, "l1f2")
    dx6, loss, dgf = loss_head(x6, wts["final_g"][None, :], target, "loss_head")

    dx5, dwf4, dg12, dm_f4 = _ffn_bwd(dx6, sv4, ml[1], gvec(1, 2), wts["w13_11"], wts["w2_11"], 6, "l1f2")
    ffn_grads("11", dwf4)
    dy, dgate1 = gate_bwd(dx5, ycv, ml[1], 5, 1.0, "l1mix_gate_bwd")
    dgm = mm_nt([dy], wts["cwo"], [0], "l1mix_dgm", out_dtype=BF16)
    grads["cwo"] = mm_tn(gm, dy, "l1mix_dwo")
    dbg, dcg, dxin, dconv_w = conv_bwd(proj, dgm, wts["conv_w"], "l1_conv_bwd")
    dxn1 = mm_nt([dbg, dcg, dxin], wts["cwi"], [0, D, 2 * D], "l1mix_dxn")
    grads["cwi"] = jnp.concatenate([mm_tn(xn1, t, "l1mix_dwi%d" % k) for k, t in enumerate((dbg, dcg, dxin))], axis=1)
    dx4, dsh, dsc, dg11 = normmod_bwd(dxn1, x4, gvec(1, 1), ml[1], 4, dx5, "l1mix_norm_bwd")
    dm1 = {3: dsh, 4: dsc, 5: dgate1, **dm_f4}
    dx3, dwf3, dg10, dm_f3 = _ffn_bwd(dx4, sv3, ml[1], gvec(1, 0), wts["w13_10"], wts["w2_10"], 0, "l1f1")
    ffn_grads("10", dwf3)
    dm1.update(dm_f3)

    dx2, dwf2, dg02, dm_f2 = _ffn_bwd(dx3, sv2, ml[0], gvec(0, 2), wts["w13_01"], wts["w2_01"], 6, "l0f2")
    ffn_grads("01", dwf2)
    dy, dgate0 = gate_bwd(dx2, ymix, ml[0], 5, 1.0, "l0mix_gate_bwd")
    dmix = mm_nt([dy], wts["ewo"], [0], "l0mix_dmix", out_dtype=BF16)
    dewo = jnp.concatenate([mm_tn(att, dy, "l0mix_dwo_att"), mm_tn(pool, dy, "l0mix_dwo_pool")], axis=0)
    scatter = Exchange("scatter", [block_piece(p, grads.pop(p)) for p in SCATTER_IN_ATTN]) if shards else None
    (dq, dk, dv, dkc, dvc, dtab), got = attn_bwd(qkvu, qkvu_c, tab, dmix, "l0_attn_bwd", scatter)
    recv = dict(zip(SCATTER_IN_ATTN, got))
    du, dpool_w, dpool_scale = pool_bwd(qkvu, dmix, pool_w16, wts["pool_scale"], "l0_pool_bwd")
    drpb = bias_table_bwd(dtab)
    dk16, dv16, dkc16, dvc16 = (t.astype(BF16) for t in (dk, dv, dkc, dvc))
    dxn = mm_nt([dq, dk16, dv16, du], wts["ewi"], [0, NA_W, 2 * NA_W, 3 * NA_W], "l0mix_dxn")
    dcn = mm_nt([dkc16, dvc16], wts["ewi"], [NA_W, 2 * NA_W], "l0mix_dxn_c")
    grads["ewo"] = dewo
    grads["ewi"] = jnp.concatenate([
        mm_tn(xn, dq, "l0mix_dwi_q"),
        mm_tn(cn, dkc16, "l0mix_dwi_kc", acc=mm_tn(xn, dk16, "l0mix_dwi_k")),
        mm_tn(cn, dvc16, "l0mix_dwi_vc", acc=mm_tn(xn, dv16, "l0mix_dwi_v")),
        mm_tn(xn, du, "l0mix_dwi_u")], axis=1)
    dx1, dsh, dsc, dg01 = normmod_bwd(dxn, x1, gvec(0, 1), ml[0], 4, dx2, "l0mix_norm_bwd")
    dc1, dsh_c, dsc_c, dg01c = normmod_bwd(dcn, c1, gvec(0, 1), mc[0], 4, jnp.zeros_like(dcn), "l0mix_norm_bwd_c")
    dm0 = {3: dsh, 4: dsc, 5: dgate0, **dm_f2}
    dx0, dwf1, dg00, dm_f1 = _ffn_bwd(dx1, sv1, ml[0], gvec(0, 0), wts["w13_00"], wts["w2_00"], 0, "l0f1")
    dm0.update(dm_f1)
    _, dwf1, dg00c, dm_f1c = _ffn_bwd(dc1, sv1c, mc[0], gvec(0, 0), wts["w13_00"], wts["w2_00"], 0, "l0f1c",
                                      acc=dwf1)
    ffn_grads("00", dwf1)
    dmc0 = {3: dsh_c, 4: dsc_c, **dm_f1c}

    return {
        "loss": loss, "grad_x": dx0,
        "dml": jnp.stack([_mod_rows(dm0), _mod_rows(dm1)]),
        "dmc": jnp.stack([_mod_rows(dmc0), jnp.zeros((N_MOD, D), F32)]),
        "norm_g": jnp.concatenate([dg00 + dg00c, dg01 + dg01c, dg02, dg10, dg11, dg12], axis=0),
        "grads": grads, "recv": recv,
        "rpb": drpb, "pool_w": dpool_w, "pool_scale": dpool_scale, "conv_w": dconv_w, "final_g": dgf,
    }


def _rows_of(v, nrows):
    flat = v.reshape(-1)
    return jnp.pad(flat, (0, nrows * D - flat.shape[0])).reshape(nrows, D)


def kernel(x, c, ctx, c_ctx, mod_w, mod_b, norm_g, ffn_w13, ffn_w2, even_w_in, even_w_out, na_rpb, pool_w, pool_scale, conv_w_in, conv_w, conv_w_out, final_g, loss_target, m_c_ctx, m_mod_w, m_mod_b, m_norm_g, m_ffn_w13, m_ffn_w2, m_even_w_in, m_even_w_out, m_na_rpb, m_pool_w, m_pool_scale, m_conv_w_in, m_conv_w, m_conv_w_out, m_final_g, v_c_ctx, v_mod_w, v_mod_b, v_norm_g, v_ffn_w13, v_ffn_w2, v_even_w_in, v_even_w_out, v_na_rpb, v_pool_w, v_pool_scale, v_conv_w_in, v_conv_w, v_conv_w_out, v_final_g):
    me = 4 * lax.axis_index("x") + 2 * lax.axis_index("y") + lax.axis_index("c")
    ff = ffn_w2.shape[2] * N_DEV
    w13c = ffn_w13.shape[3]
    w2r = ffn_w2.shape[2]
    mcols = mod_w.shape[2]
    gcols = norm_g.shape[2]

    big = {"w13": ffn_w13.reshape(4 * D, w13c), "w2": ffn_w2.reshape(4 * w2r, D), "ewi": even_w_in[0],
           "ewo": even_w_out[0], "cwi": conv_w_in[0], "cwo": conv_w_out[0]}
    names = list(big)
    shards = {"ewi": even_w_in[0].astype(BF16), "ewo": even_w_out[0].astype(BF16),
              "cwi": conv_w_in[0].astype(BF16), "cwo": conv_w_out[0].astype(BF16)}
    for l in range(2):
        for f in range(2):
            shards["w13_%d%d" % (l, f)] = ffn_w13[l, f].astype(BF16)
            shards["w2_%d%d" % (l, f)] = ffn_w2[l, f].astype(BF16)
    first = Exchange("gather", [shards[p] for p in GATHER_FIRST]).run("weights_allgather_first")
    wts = {p: unpack_piece(p, g) for p, g in zip(GATHER_FIRST, first)}

    c_all = small_allgather(jnp.pad(c, ((0, 7), (0, 0))), "cond_allgather")[:, 0, :]
    cm = jnp.concatenate([c_all, c_ctx[None, :], jnp.zeros((7, D), F32)], axis=0)
    mod_b_cols = lax.dynamic_slice(mod_b, (0, me * mcols), (2, mcols))[:, None, :]
    m_cols = adaln_fwd(cm, mod_w, mod_b_cols, "adaln_fwd")
    m_all = small_allgather(m_cols.reshape(32, mcols), "mod_allgather")
    m_full = m_all.reshape(N_DEV, 2, 16, mcols).transpose(1, 2, 0, 3).reshape(2, 16, N_MOD * D)
    ml = lax.dynamic_slice(m_full, (0, me, 0), (2, 1, N_MOD * D)).reshape(2, N_MOD, D)
    mc = m_full[:, 8].reshape(2, N_MOD, D)

    full_norm_g = small_allgather(_rows_of(norm_g, 8), "norm_g_allgather")[:, 0, :2 * 3 * gcols]
    full_norm_g = full_norm_g.reshape(N_DEV, 2, 3, gcols).transpose(1, 2, 0, 3).reshape(2, 3, D)
    full_conv_w = small_allgather(_rows_of(conv_w, 8), "conv_w_allgather")[:, 0, :3 * gcols]
    full_conv_w = full_conv_w.reshape(N_DEV, 3, gcols).transpose(1, 0, 2).reshape(3, D)
    wts.update(norm_g=full_norm_g, conv_w=full_conv_w, rpb=na_rpb[0], pool_w=pool_w[0], pool_scale=pool_scale,
               final_g=final_g)
    out = local_step(x[0], ctx[0], ml, mc, wts, loss_target[0], shards)

    dm_pack = jnp.concatenate([out["dml"].reshape(2, N_MOD * D), out["dmc"].reshape(2, N_MOD * D),
                               jnp.zeros((4, N_MOD * D), F32)], axis=0)
    dm_t = small_allgather(dm_pack, "dmod_allgather").transpose(1, 0, 2)[:4]
    dm_cols = lax.dynamic_slice(dm_t, (0, 0, me * mcols), (4, N_DEV, mcols))
    g_mod_w, pc = adaln_bwd(cm.T, mod_w, dm_cols, "adaln_bwd")
    g_mod_b = mod_b_grad(dm_t, "mod_b_grad")[:2]

    pack = jnp.concatenate([_rows_of(t, 8) for t in (
        out["norm_g"], out["conv_w"], out["final_g"], pc[0, :1] + pc[1, :1], out["pool_scale"], out["loss"],
        out["rpb"])] + [_rows_of(out["pool_w"], 64)], axis=0)
    small = sum_devices(small_allgather(pack, "small_grads_allgather"), "small_grads_sum")
    g_norm_g = lax.dynamic_slice(small[0:6].reshape(2, 3, D), (0, 0, me * gcols), (2, 3, gcols))
    g_conv_w = lax.dynamic_slice(small[8:11], (0, me * gcols), (3, gcols))[None]
    g_final_g = small[16]
    sg = _sigmoid(c_ctx)
    g_c_ctx = small[24] * (sg * (1.0 + c_ctx * (1.0 - sg)))
    g_pool_scale = small[32:33, :POOL_W]
    loss = small[40, 0]
    g_rpb = small[48:52].reshape(-1)[:na_rpb.size].reshape(na_rpb.shape)
    g_pool_w = small[56:120].reshape(pool_w.shape)

    last = Exchange("scatter", [block_piece(p, out["grads"][p]) for p in SCATTER_LAST]).run("grads_scatter_last")
    pieces = {**out["recv"], **dict(zip(SCATTER_LAST, last))}
    lf = ("00", "01", "10", "11")
    recv = {"w13": jnp.concatenate([pieces["w13_" + t] for t in lf], axis=1),
            "w2": jnp.concatenate([pieces["w2_" + t] for t in lf], axis=1),
            "ewi": pieces["ewi"], "ewo": pieces["ewo"], "cwi": pieces["cwi"], "cwo": pieces["cwo"]}

    moments = {"w13": (m_ffn_w13, v_ffn_w13), "w2": (m_ffn_w2, v_ffn_w2), "ewi": (m_even_w_in, v_even_w_in),
               "ewo": (m_even_w_out, v_even_w_out), "cwi": (m_conv_w_in, v_conv_w_in),
               "cwo": (m_conv_w_out, v_conv_w_out)}
    orig = {"w13": ffn_w13, "w2": ffn_w2, "ewi": even_w_in, "ewo": even_w_out, "cwi": conv_w_in, "cwo": conv_w_out}
    upd = {}
    for k in names:
        shp2 = big[k].shape
        res = adamw(big[k], moments[k][0].reshape(shp2), moments[k][1].reshape(shp2), "adamw_" + k, recv=recv[k])
        upd[k] = [r.reshape(orig[k].shape) for r in res]
    shp2 = (2 * D, mcols)
    upd["mod_w"] = [r.reshape(mod_w.shape) for r in adamw(mod_w.reshape(shp2), m_mod_w.reshape(shp2),
                                                          v_mod_w.reshape(shp2), "adamw_mod_w",
                                                          g=g_mod_w.reshape(shp2))]

    smalls = [("c_ctx", c_ctx, m_c_ctx, v_c_ctx, g_c_ctx, 8), ("mod_b", mod_b, m_mod_b, v_mod_b, g_mod_b, 24),
              ("norm_g", norm_g, m_norm_g, v_norm_g, g_norm_g, 8), ("rpb", na_rpb, m_na_rpb, v_na_rpb, g_rpb, 8),
              ("pool_w", pool_w, m_pool_w, v_pool_w, g_pool_w, 64),
              ("pool_scale", pool_scale, m_pool_scale, v_pool_scale, g_pool_scale, 8),
              ("conv_w", conv_w, m_conv_w, v_conv_w, g_conv_w, 8), ("final_g", final_g, m_final_g, v_final_g, g_final_g, 8)]
    packed = [jnp.concatenate([_rows_of(s[col], s[5]) for s in smalls], axis=0) for col in (1, 2, 3, 4)]
    res = adamw(packed[0], packed[1], packed[2], "adamw_small", g=packed[3])
    row = 0
    for name, w, _, _, _, nrows in smalls:
        upd[name] = [r[row:row + nrows].reshape(-1)[:w.size].reshape(w.shape) for r in res]
        row += nrows

    order = ["c_ctx", "mod_w", "mod_b", "norm_g", "w13", "w2", "ewi", "ewo", "rpb", "pool_w", "pool_scale", "cwi",
             "conv_w", "cwo", "final_g"]
    grad_x = out["grad_x"][None]
    return (loss, grad_x, *[upd[k][0] for k in order], *[upd[k][1] for k in order], *[upd[k][2] for k in order],
            *[upd[k][3] for k in order])
```

```python
import functools

import numpy as np
import jax
import jax.numpy as jnp
from jax import lax
from jax.experimental import pallas as pl
from jax.experimental.pallas import tpu as pltpu

D = 1024
FF = 2816
SEQ = 16384
CTX = 256
GRID_W = 64
N_MOD = 9
HEADS = 8
HEAD_DIM = 64
NA_W = 512
POOL_W = 512
POOL_G = 128
POOL_WINDOWS = (2, 4, 8, 16)
KH = 8
KW = 16
RMS_EPS = 1e-6
NEG_INF = -1e30
N_DEV = 8

ADAM_LR = 0.001
ADAM_B1 = 0.9
ADAM_B2 = 0.999
ADAM_EPS = 1e-08
ADAM_WD = 0.01
ADAM_STEP = 10

VMEM_LIMIT = 52 * 1024 * 1024
HALO = 16
QROWS = 8
WROWS = 24

BF16 = jnp.bfloat16
F32 = jnp.float32
MESH_ID = pl.DeviceIdType.MESH
HI = lax.Precision.HIGHEST

NT_DIMS = (((1,), (1,)), ((), ()))
TN_DIMS = (((0,), (0,)), ((), ()))


def _tile(n, cands):
    for c in cands:
        if n % c == 0:
            return c
    return n


def _params(sem):
    return pltpu.CompilerParams(dimension_semantics=sem, vmem_limit_bytes=VMEM_LIMIT)


def _dot(a, b):
    return jnp.dot(a, b, preferred_element_type=F32)


def _dot_nt(a, b):
    return lax.dot_general(a, b, NT_DIMS, preferred_element_type=F32)


def _dot_tn(a, b):
    return lax.dot_general(a, b, TN_DIMS, preferred_element_type=F32)


def _sigmoid(x):
    return 1.0 / (1.0 + jnp.exp(-x))


def normmod(h, g, mods, i_shift, i_scale, name):
    n = h.shape[0]
    te = _tile(n, (512, 256))

    def body(h_ref, g_ref, m_ref, o_ref):
        x = h_ref[...]
        r = lax.rsqrt(jnp.mean(x * x, axis=-1, keepdims=True) + RMS_EPS)
        y = x * r * g_ref[...]
        o_ref[...] = (y * (1.0 + m_ref[i_scale:i_scale + 1, :]) + m_ref[i_shift:i_shift + 1, :]).astype(BF16)

    return pl.pallas_call(
        body, name=name, grid=(n // te,),
        in_specs=[pl.BlockSpec((te, D), lambda i: (i, 0)),
                  pl.BlockSpec((1, D), lambda i: (0, 0)),
                  pl.BlockSpec((N_MOD, D), lambda i: (0, 0))],
        out_specs=pl.BlockSpec((te, D), lambda i: (i, 0)),
        out_shape=jax.ShapeDtypeStruct((n, D), BF16),
        compiler_params=_params(("parallel",)),
    )(h, g, mods)


def loss_head(x, g, target, prev, name):
    n = x.shape[0]
    te = _tile(n, (256,))
    i_gate, coef = prev[2], prev[3]

    def body(x_ref, g_ref, t_ref, y_ref, m_ref, dx_ref, loss_ref, dg_ref, dy_ref, dgate_ref):
        @pl.when(pl.program_id(0) == 0)
        def _():
            loss_ref[...] = jnp.zeros_like(loss_ref)
            dg_ref[...] = jnp.zeros_like(dg_ref)
            dgate_ref[...] = jnp.zeros_like(dgate_ref)

        xv = x_ref[...]
        gv = g_ref[...]
        r = lax.rsqrt(jnp.mean(xv * xv, axis=-1, keepdims=True) + RMS_EPS)
        xhat = xv * r
        e = xhat * gv - t_ref[...]
        per_tok = jnp.mean(e * e, axis=-1, keepdims=True)
        loss_ref[...] += 0.5 * jnp.sum(per_tok, axis=0, keepdims=True)
        dy = e * (1.0 / D)
        dg_ref[...] += jnp.sum(dy * xhat, axis=0, keepdims=True)
        dxhat = dy * gv
        dx = r * (dxhat - xhat * jnp.mean(dxhat * xhat, axis=-1, keepdims=True))
        dx_ref[...] = dx
        dy_ref[...] = (dx * (coef * m_ref[i_gate:i_gate + 1, :])).astype(BF16)
        dgate_ref[...] += coef * jnp.sum(dx * y_ref[...].astype(F32), axis=0, keepdims=True)

    row = pl.BlockSpec((te, D), lambda i: (i, 0))
    vec = pl.BlockSpec((1, D), lambda i: (0, 0))
    return pl.pallas_call(
        body, name=name, grid=(n // te,),
        in_specs=[row, vec, row, row, pl.BlockSpec((N_MOD, D), lambda i: (0, 0))],
        out_specs=[row, pl.BlockSpec((1, 128), lambda i: (0, 0)), vec, row, vec],
        out_shape=[jax.ShapeDtypeStruct((n, D), F32), jax.ShapeDtypeStruct((1, 128), F32),
                   jax.ShapeDtypeStruct((1, D), F32), jax.ShapeDtypeStruct((n, D), BF16),
                   jax.ShapeDtypeStruct((1, D), F32)],
        compiler_params=_params(("arbitrary",)),
    )(x, g, target, prev[0], prev[1])


def ffn_up(hn, w13, name, exchange=None):
    n = hn.shape[0]
    ff = w13.shape[1] // 2
    tm = _tile(n, (512, 256))
    tn = _tile(ff, (1408, 512, 256, 128))
    nj = ff // tn
    ni = n // tm
    nx = exchange.n if exchange else 0

    def body(*refs):
        h_ref, wa_ref, wb_ref = refs[:3]
        x_in = refs[3:3 + nx]
        a_ref, b_ref, s_ref = refs[3 + nx:6 + nx]
        x_out = refs[6 + nx:6 + 2 * nx]
        x_sems = refs[6 + 2 * nx:]
        if exchange:
            @pl.when((pl.program_id(0) == 0) & (pl.program_id(1) == 0))
            def _():
                exchange.start(x_in, x_out, x_sems)

        hv = h_ref[...]
        a = _dot(hv, wa_ref[...])
        b = _dot(hv, wb_ref[...])
        a_ref[...] = a.astype(BF16)
        b_ref[...] = b.astype(BF16)
        s_ref[...] = (a * _sigmoid(a) * b).astype(BF16)

        if exchange:
            @pl.when((pl.program_id(0) == nj - 1) & (pl.program_id(1) == ni - 1))
            def _():
                exchange.finish(x_in, x_out, x_sems)

    out = pl.BlockSpec((tm, tn), lambda j, i: (i, j))
    hbm = pl.BlockSpec(memory_space=pltpu.HBM)
    sem = ("arbitrary", "arbitrary") if exchange else ("parallel", "parallel")
    res = pl.pallas_call(
        body, name=name, grid=(nj, ni),
        in_specs=[pl.BlockSpec((tm, D), lambda j, i: (i, 0)),
                  pl.BlockSpec((D, tn), lambda j, i: (0, j)),
                  pl.BlockSpec((D, tn), lambda j, i: (0, j + nj))] + [hbm] * nx,
        out_specs=[out, out, out] + [hbm] * nx,
        out_shape=[jax.ShapeDtypeStruct((n, ff), BF16)] * 3 + (exchange.out_shapes if exchange else []),
        scratch_shapes=exchange.scratch if exchange else [],
        compiler_params=_params(sem),
    )(hn, w13, w13, *(exchange.arrays if exchange else []))
    return res[:3], list(res[3:])


def mm_nn(a_list, w, row_offs, name, out_dtype=BF16, res=None, nxt=None):
    n = a_list[0].shape[0]
    nout = w.shape[1]
    ks = [a.shape[1] for a in a_list]
    tm = _tile(n, (512, 256))
    tn = _tile(nout, (1024, 512, 256, 128))
    na = len(a_list)
    assert nxt is None or (res is not None and tn == D)

    def body(*refs):
        a_refs = refs[:na]
        w_refs = refs[na:2 * na]
        acc = _dot(a_refs[0][...], w_refs[0][...])
        for k in range(1, na):
            acc += _dot(a_refs[k][...], w_refs[k][...])
        if res is None:
            refs[2 * na][...] = acc.astype(out_dtype)
        else:
            h_ref, m_ref = refs[2 * na:2 * na + 2]
            i_gate, coef = res[2], res[3]
            h_new = h_ref[...] + (coef * m_ref[i_gate:i_gate + 1, :]) * acc
            if nxt is None:
                hn_ref, y_ref = refs[2 * na + 2:]
            else:
                g2_ref, m2_ref, hn_ref, y_ref, nx_ref = refs[2 * na + 2:]
                r = lax.rsqrt(jnp.mean(h_new * h_new, axis=-1, keepdims=True) + RMS_EPS)
                nx_ref[...] = ((h_new * r * g2_ref[...]) * (1.0 + m2_ref[nxt[3]:nxt[3] + 1, :])
                               + m2_ref[nxt[2]:nxt[2] + 1, :]).astype(BF16)
            hn_ref[...] = h_new
            y_ref[...] = acc.astype(BF16)

    in_specs = [pl.BlockSpec((tm, k), lambda j, i: (i, 0)) for k in ks]
    for k, off in zip(ks, row_offs):
        in_specs.append(pl.BlockSpec((k, tn), functools.partial(lambda j, i, ob: (ob, j), ob=off // k)))
    args = list(a_list) + [w] * na
    out = pl.BlockSpec((tm, tn), lambda j, i: (i, j))
    if res is None:
        out_specs = out
        out_shape = jax.ShapeDtypeStruct((n, nout), out_dtype)
    else:
        in_specs += [out, pl.BlockSpec((N_MOD, tn), lambda j, i: (0, j))]
        args += [res[0], res[1]]
        out_specs = [out, out]
        out_shape = [jax.ShapeDtypeStruct((n, nout), F32), jax.ShapeDtypeStruct((n, nout), BF16)]
        if nxt is not None:
            in_specs += [pl.BlockSpec((1, D), lambda j, i: (0, 0)), pl.BlockSpec((N_MOD, D), lambda j, i: (0, 0))]
            args += [nxt[0], nxt[1]]
            out_specs.append(out)
            out_shape.append(jax.ShapeDtypeStruct((n, nout), BF16))
    return pl.pallas_call(
        body, name=name, grid=(nout // tn, n // tm),
        in_specs=in_specs, out_specs=out_specs, out_shape=out_shape,
        compiler_params=_params(("parallel", "parallel")),
    )(*args)


def mm_nt(g_list, w, col_offs, name, out_dtype=F32, dswiglu=None):
    n = g_list[0].shape[0]
    nout = w.shape[0]
    kg = g_list[0].shape[1]
    tm = _tile(n, (512, 256))
    tn = _tile(nout, (1408, 1024, 512, 256, 128))
    tk = _tile(kg, (1408, 1024, 512, 256, 128))
    ng = len(g_list)
    nk = kg // tk

    def body(*refs):
        g_refs = refs[:ng]
        w_refs = refs[ng:2 * ng]
        rest = refs[2 * ng:]
        acc_ref = rest[-1]
        k = pl.program_id(2)

        @pl.when(k == 0)
        def _():
            acc_ref[...] = jnp.zeros_like(acc_ref)

        acc = _dot_nt(g_refs[0][...], w_refs[0][...])
        for q in range(1, ng):
            acc += _dot_nt(g_refs[q][...], w_refs[q][...])
        acc_ref[...] += acc

        @pl.when(k == nk - 1)
        def _():
            r = acc_ref[...]
            if dswiglu is None:
                rest[0][...] = r.astype(out_dtype)
            else:
                a_ref, b_ref, da_ref, db_ref = rest[:4]
                a = a_ref[...].astype(F32)
                sig = _sigmoid(a)
                da_ref[...] = (r * b_ref[...].astype(F32) * (sig * (1.0 + a * (1.0 - sig)))).astype(BF16)
                db_ref[...] = (r * (a * sig)).astype(BF16)

    in_specs = [pl.BlockSpec((tm, tk), lambda j, i, k: (i, k)) for _ in g_list]
    for off in col_offs:
        in_specs.append(pl.BlockSpec((tn, tk), functools.partial(lambda j, i, k, ob: (j, ob + k), ob=off // tk)))
    args = list(g_list) + [w] * ng
    out = pl.BlockSpec((tm, tn), lambda j, i, k: (i, j))
    if dswiglu is None:
        out_specs = out
        out_shape = jax.ShapeDtypeStruct((n, nout), out_dtype)
    else:
        in_specs += [out, out]
        args += list(dswiglu)
        out_specs = [out, out]
        out_shape = [jax.ShapeDtypeStruct((n, nout), BF16)] * 2
    return pl.pallas_call(
        body, name=name, grid=(nout // tn, n // tm, nk),
        in_specs=in_specs, out_specs=out_specs, out_shape=out_shape,
        scratch_shapes=[pltpu.VMEM((tm, tn), F32)],
        compiler_params=_params(("parallel", "parallel", "arbitrary")),
    )(*args)


def mm_nt_norm(g_list, w, col_offs, h, g, mods, i_scale, dres, name, prev=None):
    n = h.shape[0]
    kg = g_list[0].shape[1]
    tm = _tile(n, (512, 256))
    tk = _tile(kg, (1408, 1024, 512, 256, 128))
    ng = len(g_list)
    nk = kg // tk
    has_res = dres is not None

    def body(*refs):
        g_refs = refs[:ng]
        w_refs = refs[ng:2 * ng]
        pos = 2 * ng
        h_ref, gv_ref, m_ref = refs[pos:pos + 3]
        pos += 3
        if has_res:
            dres_ref = refs[pos]
            pos += 1
        if prev is not None:
            y_ref, mp_ref = refs[pos:pos + 2]
            pos += 2
        dh_ref, dshift_ref, dscale_ref, dg_ref = refs[pos:pos + 4]
        pos += 4
        if prev is not None:
            dy_ref, dgate_ref = refs[pos:pos + 2]
            pos += 2
        acc_ref = refs[pos]
        i = pl.program_id(0)
        k = pl.program_id(1)

        @pl.when((i == 0) & (k == 0))
        def _():
            dshift_ref[...] = jnp.zeros_like(dshift_ref)
            dscale_ref[...] = jnp.zeros_like(dscale_ref)
            dg_ref[...] = jnp.zeros_like(dg_ref)
            if prev is not None:
                dgate_ref[...] = jnp.zeros_like(dgate_ref)

        @pl.when(k == 0)
        def _():
            acc_ref[...] = jnp.zeros_like(acc_ref)

        acc = _dot_nt(g_refs[0][...], w_refs[0][...])
        for q in range(1, ng):
            acc += _dot_nt(g_refs[q][...], w_refs[q][...])
        acc_ref[...] += acc

        @pl.when(k == nk - 1)
        def _():
            d = acc_ref[...]
            x = h_ref[...]
            gv = gv_ref[...]
            r = lax.rsqrt(jnp.mean(x * x, axis=-1, keepdims=True) + RMS_EPS)
            xhat = x * r
            dshift_ref[...] += jnp.sum(d, axis=0, keepdims=True)
            dscale_ref[...] += jnp.sum(d * (xhat * gv), axis=0, keepdims=True)
            dn = d * (1.0 + m_ref[i_scale:i_scale + 1, :])
            dg_ref[...] += jnp.sum(dn * xhat, axis=0, keepdims=True)
            dxhat = dn * gv
            dh = r * (dxhat - xhat * jnp.mean(dxhat * xhat, axis=-1, keepdims=True))
            if has_res:
                dh = dh + dres_ref[...]
            dh_ref[...] = dh
            if prev is not None:
                i_gate, coef = prev[2], prev[3]
                dy_ref[...] = (dh * (coef * mp_ref[i_gate:i_gate + 1, :])).astype(BF16)
                dgate_ref[...] += coef * jnp.sum(dh * y_ref[...].astype(F32), axis=0, keepdims=True)

    row = pl.BlockSpec((tm, D), lambda i, k: (i, 0))
    vec = pl.BlockSpec((1, D), lambda i, k: (0, 0))
    modspec = pl.BlockSpec((N_MOD, D), lambda i, k: (0, 0))
    in_specs = [pl.BlockSpec((tm, tk), lambda i, k: (i, k)) for _ in g_list]
    for off in col_offs:
        in_specs.append(pl.BlockSpec((D, tk), functools.partial(lambda i, k, ob: (0, ob + k), ob=off // tk)))
    in_specs += [row, vec, modspec]
    args = list(g_list) + [w] * ng + [h, g, mods]
    out_specs = [row, vec, vec, vec]
    out_shape = [jax.ShapeDtypeStruct((n, D), F32)] + [jax.ShapeDtypeStruct((1, D), F32)] * 3
    if has_res:
        in_specs.append(row)
        args.append(dres)
    if prev is not None:
        in_specs += [row, modspec]
        args += [prev[0], prev[1]]
        out_specs += [row, vec]
        out_shape += [jax.ShapeDtypeStruct((n, D), BF16), jax.ShapeDtypeStruct((1, D), F32)]
    return pl.pallas_call(
        body, name=name, grid=(n // tm, nk),
        in_specs=in_specs, out_specs=out_specs, out_shape=out_shape,
        scratch_shapes=[pltpu.VMEM((tm, D), F32)],
        compiler_params=_params(("arbitrary", "arbitrary")),
    )(*args)


def mm_tn(a, g, name, acc=None):
    n, ka = a.shape
    ngc = g.shape[1]
    tka = _tile(ka, (1408, 1024, 512, 256, 128))
    tng = _tile(ngc, (1408, 1024, 512, 256, 128))
    tr = _tile(n, (512, 256))
    has_acc = acc is not None

    def body(*refs):
        a_ref, g_ref = refs[0], refs[1]
        o_ref = refs[-1]
        r = pl.program_id(2)

        @pl.when(r == 0)
        def _():
            if has_acc:
                o_ref[...] = refs[2][...]
            else:
                o_ref[...] = jnp.zeros_like(o_ref)

        o_ref[...] += _dot_tn(a_ref[...], g_ref[...])

    out = pl.BlockSpec((tka, tng), lambda p, q, r: (p, q))
    in_specs = [pl.BlockSpec((tr, tka), lambda p, q, r: (r, p)),
                pl.BlockSpec((tr, tng), lambda p, q, r: (r, q))]
    args = [a, g]
    if has_acc:
        in_specs.append(out)
        args.append(acc)
    return pl.pallas_call(
        body, name=name, grid=(ka // tka, ngc // tng, n // tr),
        in_specs=in_specs, out_specs=out,
        out_shape=jax.ShapeDtypeStruct((ka, ngc), F32),
        compiler_params=_params(("parallel", "parallel", "arbitrary")),
    )(*args)


def mm_small(a, b, name, trans_b=False):
    m = a.shape[0]
    nout = b.shape[0] if trans_b else b.shape[1]

    def body(a_ref, b_ref, o_ref):
        if trans_b:
            o_ref[...] = lax.dot_general(a_ref[...], b_ref[...], NT_DIMS, precision=HI, preferred_element_type=F32)
        else:
            o_ref[...] = jnp.dot(a_ref[...], b_ref[...], precision=HI, preferred_element_type=F32)

    return pl.pallas_call(
        body, name=name,
        out_shape=jax.ShapeDtypeStruct((m, nout), F32),
        compiler_params=pltpu.CompilerParams(vmem_limit_bytes=VMEM_LIMIT),
    )(a, b)


def _col_tables():
    col = np.arange(GRID_W)
    start = np.clip(col - KW // 2, 0, GRID_W - KW)
    ok = (col[None, :] >= start[:, None]) & (col[None, :] < start[:, None] + KW)
    ci = np.clip(col[None, :] - col[:, None] + (KW - 1), 0, 2 * KW - 2)
    e = np.zeros((2 * KW - 1, GRID_W, GRID_W), np.float32)
    for c in range(2 * KW - 1):
        e[c] = (ci == c) & ok
    return e.reshape(2 * KW - 1, GRID_W * GRID_W), ok


def bias_table(rpb):
    e, ok = _col_tables()
    e_pad = np.zeros((32, GRID_W * GRID_W), np.float32)
    e_pad[:31] = e
    rp = jnp.pad(rpb.reshape(HEADS * 15, 31), ((0, 0), (0, 1)))
    t = mm_small(rp, jnp.asarray(e_pad), "rpb_expand").reshape(HEADS, 15, GRID_W, GRID_W)
    t = jnp.where(jnp.asarray(ok)[None, None], t, NEG_INF)
    tab = jnp.stack([t[:, v:v + KH] for v in range(8)], axis=0)
    return tab.transpose(0, 1, 3, 2, 4).reshape(TAB_SHAPE)


def bias_table_bwd(dtab):
    e, _ = _col_tables()
    e_pad = np.zeros((128, GRID_W * GRID_W), np.float32)
    e_pad[:31] = e
    d = dtab.reshape(8, HEADS, GRID_W, KH, GRID_W).transpose(0, 1, 3, 2, 4).reshape(8 * HEADS * KH, GRID_W * GRID_W)
    gv = mm_small(d, jnp.asarray(e_pad), "rpb_reduce", trans_b=True)[:, :31]
    gv = gv.reshape(8, HEADS, KH, 31).transpose(0, 2, 1, 3).reshape(8 * KH, HEADS * 31)
    sel = np.zeros((16, 8 * KH), np.float32)
    for v in range(8):
        for j in range(KH):
            sel[v + j, v * KH + j] = 1.0
    gpad = jnp.pad(gv, ((0, 0), (0, 256 - HEADS * 31)))
    out = mm_small(jnp.asarray(sel), gpad, "rpb_fold")[:15, :HEADS * 31]
    return out.reshape(15, HEADS, 31).transpose(1, 0, 2)


def _attn_geometry(seq):
    rows = seq // GRID_W
    nb = rows // QROWS
    return rows, nb


def _stack_heads(t2):
    first = (lax.broadcasted_iota(jnp.int32, (1, 128), 1) // HEAD_DIM) == 0
    zero = jnp.zeros_like(t2)
    return jnp.concatenate([jnp.where(first, t2, zero), jnp.where(first, zero, t2)], axis=0)


def _unstack_heads(t):
    first = (lax.broadcasted_iota(jnp.int32, (1, 128), 1) // HEAD_DIM) == 0
    return jnp.where(first, t[0:GRID_W], t[GRID_W:2 * GRID_W])


TAB_SHAPE = (8, HEADS // 2, 2 * GRID_W, KH * GRID_W)


def attn_fwd(qkvu, qkvu_c, tab, name, exchange=None):
    seq = qkvu.shape[0]
    nctx = qkvu_c.shape[0]
    rows, nb = _attn_geometry(seq)
    qt = QROWS * GRID_W
    wt = WROWS * GRID_W
    scale = HEAD_DIM ** -0.5
    nx = exchange.n if exchange else 0

    def wb0(i):
        return jnp.clip(i - 1, 0, nb - 3)

    def body(*refs):
        q_ref, k0, k1, k2, v0, v1, v2, kc_ref, vc_ref, tab_hbm = refs[:10]
        x_in = refs[10:10 + nx]
        o_ref = refs[10 + nx]
        x_out = refs[11 + nx:11 + 2 * nx]
        kbuf, vbuf, tab_s, sem = refs[11 + 2 * nx:15 + 2 * nx]
        x_sems = refs[15 + 2 * nx:]
        i = pl.program_id(0)

        @pl.when(i == 0)
        def _():
            if exchange:
                exchange.start(x_in, x_out, x_sems)
            cp = pltpu.make_async_copy(tab_hbm, tab_s, sem)
            cp.start()
            cp.wait()

        for t, (kr, vr) in enumerate(((k0, v0), (k1, v1), (k2, v2))):
            kbuf[t * qt:(t + 1) * qt, :] = kr[...]
            vbuf[t * qt:(t + 1) * qt, :] = vr[...]
        base = wb0(i) * QROWS

        def row_body(rl, carry):
            r = i * QROWS + rl
            rs = jnp.clip(r - KH // 2, 0, rows - KH)
            vi = rs - r + (KH - 1)
            off = pl.multiple_of((rs - base) * GRID_W, GRID_W)
            qoff = pl.multiple_of(rl * GRID_W, GRID_W)
            for p in range(HEADS // 2):
                ls = slice(p * 128, (p + 1) * 128)
                qst = _stack_heads(q_ref[pl.ds(qoff, GRID_W), ls])
                k2v = kbuf[pl.ds(off, KH * GRID_W), ls]
                v2v = vbuf[pl.ds(off, KH * GRID_W), ls]
                s_w = _dot_nt(qst, k2v) * scale + tab_s[vi, p]
                s_c = _dot_nt(qst, kc_ref[:, ls]) * scale
                m = jnp.maximum(jnp.max(s_w, axis=-1, keepdims=True), jnp.max(s_c, axis=-1, keepdims=True))
                pw = jnp.exp(s_w - m)
                pc = jnp.exp(s_c - m)
                l = jnp.sum(pw, axis=-1, keepdims=True) + jnp.sum(pc, axis=-1, keepdims=True)
                o = _dot(pw.astype(BF16), v2v) + _dot(pc.astype(BF16), vc_ref[:, ls])
                o_ref[pl.ds(qoff, GRID_W), ls] = _unstack_heads(o * (1.0 / l)).astype(BF16)
            return carry

        lax.fori_loop(0, QROWS, row_body, 0)

        if exchange:
            @pl.when(i == nb - 1)
            def _():
                exchange.finish(x_in, x_out, x_sems)

    blk = lambda col: [pl.BlockSpec((qt, NA_W), functools.partial(lambda i, t, c: (wb0(i) + t, c), t=t, c=col))
                       for t in range(3)]
    hbm = pl.BlockSpec(memory_space=pltpu.HBM)
    res = pl.pallas_call(
        body, name=name, grid=(nb,),
        in_specs=[pl.BlockSpec((qt, NA_W), lambda i: (i, 0))] + blk(1) + blk(2)
                 + [pl.BlockSpec((nctx, NA_W), lambda i: (0, 1)), pl.BlockSpec((nctx, NA_W), lambda i: (0, 2)),
                    pl.BlockSpec(memory_space=pl.ANY)] + [hbm] * nx,
        out_specs=[pl.BlockSpec((qt, NA_W), lambda i: (i, 0))] + [hbm] * nx,
        out_shape=[jax.ShapeDtypeStruct((seq, NA_W), BF16)] + (exchange.out_shapes if exchange else []),
        scratch_shapes=[pltpu.VMEM((wt, NA_W), BF16), pltpu.VMEM((wt, NA_W), BF16),
                        pltpu.VMEM(TAB_SHAPE, F32), pltpu.SemaphoreType.DMA] + (exchange.scratch if exchange else []),
        compiler_params=_params(("arbitrary",)),
    )(qkvu, qkvu, qkvu, qkvu, qkvu, qkvu, qkvu, qkvu_c, qkvu_c, tab, *(exchange.arrays if exchange else []))
    return res[0], list(res[1:])


def attn_bwd(qkvu, qkvu_c, tab, dmix, name, exchange=None):
    seq = qkvu.shape[0]
    nctx = qkvu_c.shape[0]
    rows, nb = _attn_geometry(seq)
    qt = QROWS * GRID_W
    wt = WROWS * GRID_W
    scale = HEAD_DIM ** -0.5
    nx = exchange.n if exchange else 0

    def wb0(i):
        return jnp.clip(i - 1, 0, nb - 3)

    def body(*refs):
        q_ref, k0, k1, k2, v0, v1, v2, kc_ref, vc_ref, do_ref, tab_hbm = refs[:11]
        x_in = refs[11:11 + nx]
        dq_ref, dk_hbm, dv_hbm, dkc_ref, dvc_ref, dtab_hbm = refs[11 + nx:17 + nx]
        x_out = refs[17 + nx:17 + 2 * nx]
        kbuf, vbuf, dkacc, dvacc, tab_s, dtab_s, sem = refs[17 + 2 * nx:24 + 2 * nx]
        x_sems = refs[24 + 2 * nx:]
        i = pl.program_id(0)

        if exchange:
            @pl.when(i == 0)
            def _():
                exchange.start(x_in, x_out, x_sems)

        def flush(src, dst, src_row, dst_row, nrows):
            cp = pltpu.make_async_copy(src.at[pl.ds(src_row, nrows)], dst.at[pl.ds(dst_row, nrows)], sem)
            cp.start()
            cp.wait()

        @pl.when(i == 0)
        def _():
            cp = pltpu.make_async_copy(tab_hbm, tab_s, sem)
            cp.start()
            cp.wait()
            dtab_s[...] = jnp.zeros_like(dtab_s)
            dkacc[...] = jnp.zeros_like(dkacc)
            dvacc[...] = jnp.zeros_like(dvacc)
            dkc_ref[...] = jnp.zeros_like(dkc_ref)
            dvc_ref[...] = jnp.zeros_like(dvc_ref)

        @pl.when((i >= 2) & (i <= nb - 2))
        def _():
            dst_row = pl.multiple_of((i - 2) * qt, qt)
            for acc_ref, dst in ((dkacc, dk_hbm), (dvacc, dv_hbm)):
                flush(acc_ref, dst, 0, dst_row, qt)
                acc_ref[0:qt, :] = acc_ref[qt:2 * qt, :]
                acc_ref[qt:2 * qt, :] = acc_ref[2 * qt:3 * qt, :]
                acc_ref[2 * qt:3 * qt, :] = jnp.zeros((qt, NA_W), F32)

        for t, (kr, vr) in enumerate(((k0, v0), (k1, v1), (k2, v2))):
            kbuf[t * qt:(t + 1) * qt, :] = kr[...]
            vbuf[t * qt:(t + 1) * qt, :] = vr[...]
        base = wb0(i) * QROWS

        def row_body(rl, carry):
            r = i * QROWS + rl
            rs = jnp.clip(r - KH // 2, 0, rows - KH)
            vi = rs - r + (KH - 1)
            off = pl.multiple_of((rs - base) * GRID_W, GRID_W)
            qoff = pl.multiple_of(rl * GRID_W, GRID_W)
            for p in range(HEADS // 2):
                ls = slice(p * 128, (p + 1) * 128)
                qst = _stack_heads(q_ref[pl.ds(qoff, GRID_W), ls])
                dost = _stack_heads(do_ref[pl.ds(qoff, GRID_W), ls])
                k2v = kbuf[pl.ds(off, KH * GRID_W), ls]
                v2v = vbuf[pl.ds(off, KH * GRID_W), ls]
                kc2 = kc_ref[:, ls]
                vc2 = vc_ref[:, ls]
                s_w = _dot_nt(qst, k2v) * scale + tab_s[vi, p]
                s_c = _dot_nt(qst, kc2) * scale
                m = jnp.maximum(jnp.max(s_w, axis=-1, keepdims=True), jnp.max(s_c, axis=-1, keepdims=True))
                pw = jnp.exp(s_w - m)
                pc = jnp.exp(s_c - m)
                inv = 1.0 / (jnp.sum(pw, axis=-1, keepdims=True) + jnp.sum(pc, axis=-1, keepdims=True))
                pw = pw * inv
                pc = pc * inv
                dpw = _dot_nt(dost, v2v)
                dpc = _dot_nt(dost, vc2)
                delta = jnp.sum(pw * dpw, axis=-1, keepdims=True) + jnp.sum(pc * dpc, axis=-1, keepdims=True)
                ds_w = pw * (dpw - delta)
                ds_c = pc * (dpc - delta)
                dtab_s[vi, p] += ds_w
                dsw16 = ds_w.astype(BF16)
                dsc16 = ds_c.astype(BF16)
                dq = (_dot(dsw16, k2v) + _dot(dsc16, kc2)) * scale
                dq_ref[pl.ds(qoff, GRID_W), ls] = _unstack_heads(dq).astype(BF16)
                dkacc[pl.ds(off, KH * GRID_W), ls] += _dot_tn(dsw16, qst) * scale
                dvacc[pl.ds(off, KH * GRID_W), ls] += _dot_tn(pw.astype(BF16), dost)
                dkc_ref[:, ls] += _dot_tn(dsc16, qst) * scale
                dvc_ref[:, ls] += _dot_tn(pc.astype(BF16), dost)
            return carry

        lax.fori_loop(0, QROWS, row_body, 0)

        @pl.when(i == nb - 1)
        def _():
            dst_row = pl.multiple_of((nb - 3) * qt, qt)
            flush(dkacc, dk_hbm, 0, dst_row, wt)
            flush(dvacc, dv_hbm, 0, dst_row, wt)
            cp = pltpu.make_async_copy(dtab_s, dtab_hbm, sem)
            cp.start()
            cp.wait()
            if exchange:
                exchange.finish(x_in, x_out, x_sems)

    blk = lambda col: [pl.BlockSpec((qt, NA_W), functools.partial(lambda i, t, c: (wb0(i) + t, c), t=t, c=col))
                       for t in range(3)]
    any_spec = pl.BlockSpec(memory_space=pl.ANY)
    hbm = pl.BlockSpec(memory_space=pltpu.HBM)
    res = pl.pallas_call(
        body, name=name, grid=(nb,),
        in_specs=[pl.BlockSpec((qt, NA_W), lambda i: (i, 0))] + blk(1) + blk(2)
                 + [pl.BlockSpec((nctx, NA_W), lambda i: (0, 1)), pl.BlockSpec((nctx, NA_W), lambda i: (0, 2)),
                    pl.BlockSpec((qt, NA_W), lambda i: (i, 0)), any_spec] + [hbm] * nx,
        out_specs=[pl.BlockSpec((qt, NA_W), lambda i: (i, 0)), any_spec, any_spec,
                   pl.BlockSpec((nctx, NA_W), lambda i: (0, 0)), pl.BlockSpec((nctx, NA_W), lambda i: (0, 0)),
                   any_spec] + [hbm] * nx,
        out_shape=[jax.ShapeDtypeStruct((seq, NA_W), BF16), jax.ShapeDtypeStruct((seq, NA_W), F32),
                   jax.ShapeDtypeStruct((seq, NA_W), F32), jax.ShapeDtypeStruct((nctx, NA_W), F32),
                   jax.ShapeDtypeStruct((nctx, NA_W), F32), jax.ShapeDtypeStruct(TAB_SHAPE, F32)]
                  + (exchange.out_shapes if exchange else []),
        scratch_shapes=[pltpu.VMEM((wt, NA_W), BF16), pltpu.VMEM((wt, NA_W), BF16),
                        pltpu.VMEM((wt, NA_W), F32), pltpu.VMEM((wt, NA_W), F32),
                        pltpu.VMEM(TAB_SHAPE, F32), pltpu.VMEM(TAB_SHAPE, F32), pltpu.SemaphoreType.DMA]
                       + (exchange.scratch if exchange else []),
        compiler_params=_params(("arbitrary",)),
    )(qkvu, qkvu, qkvu, qkvu, qkvu, qkvu, qkvu, qkvu_c, qkvu_c, dmix, tab, *(exchange.arrays if exchange else []))
    return res[:6], list(res[6:])


def _halo_specs(te, seq, col, width):
    per = te // HALO
    last = seq // HALO - 1
    return [pl.BlockSpec((HALO, width), lambda i: (jnp.maximum(i * per - 1, 0), col)),
            pl.BlockSpec((te, width), lambda i: (i, col)),
            pl.BlockSpec((HALO, width), lambda i: (jnp.minimum((i + 1) * per, last), col))]


def _extended(prev_ref, cur_ref, next_ref, i, te, seq):
    xe = jnp.concatenate([prev_ref[...], cur_ref[...], next_ref[...]], axis=0).astype(F32)
    pos = i * te - HALO + lax.broadcasted_iota(jnp.int32, (te + 2 * HALO, 1), 0)
    return jnp.where((pos >= 0) & (pos < seq), xe, 0.0), pos


def _window_sum(x, levels, n, mirrored):
    first = (n - 1) if mirrored else 1
    acc = x + pltpu.roll(x, first, 0)
    step = 1
    for _ in range(levels - 1):
        acc = pltpu.roll(acc, step, 0) + pltpu.roll(acc, n - step, 0)
        step *= 2
    return acc


def _window_count(pos, w, seq):
    lo = jnp.clip(pos - w // 2, 0, seq)
    hi = jnp.clip(pos - w // 2 + w, 0, seq)
    return jnp.maximum(hi - lo, 1).astype(F32)


def pool_fwd(qkvu, pool_w, pool_scale, name):
    seq = qkvu.shape[0]
    te = _tile(seq, (512, 256))
    n = te + 2 * HALO

    def body(up_ref, uc_ref, un_ref, w_ref, sc_ref, o_ref):
        i = pl.program_id(0)
        xe, pos = _extended(up_ref, uc_ref, un_ref, i, te, seq)
        cnt = pos[HALO:HALO + te]
        for g, w in enumerate(POOL_WINDOWS):
            ls = slice(g * POOL_G, (g + 1) * POOL_G)
            xg = xe[:, ls]
            win = _window_sum(xg, g + 1, n, False)[HALO:HALO + te]
            dlt = win / _window_count(cnt, w, seq) - xg[HALO:HALO + te]
            z = _dot(dlt.astype(BF16), w_ref[g])
            o_ref[:, ls] = (z * sc_ref[:, ls]).astype(BF16)

    return pl.pallas_call(
        body, name=name, grid=(seq // te,),
        in_specs=_halo_specs(te, seq, 3, POOL_W)
                 + [pl.BlockSpec((4, POOL_G, POOL_G), lambda i: (0, 0, 0)), pl.BlockSpec((1, POOL_W), lambda i: (0, 0))],
        out_specs=pl.BlockSpec((te, POOL_W), lambda i: (i, 0)),
        out_shape=jax.ShapeDtypeStruct((seq, POOL_W), BF16),
        compiler_params=_params(("parallel",)),
    )(qkvu, qkvu, qkvu, pool_w, pool_scale)


def pool_bwd(qkvu, dmix, pool_w, pool_scale, name):
    seq = qkvu.shape[0]
    te = _tile(seq, (512, 256))
    n = te + 2 * HALO

    def body(up_ref, uc_ref, un_ref, dp_ref, dc_ref, dn_ref, w_ref, sc_ref, du_ref, dw_ref, dsc_ref):
        i = pl.program_id(0)

        @pl.when(i == 0)
        def _():
            dw_ref[...] = jnp.zeros_like(dw_ref)
            dsc_ref[...] = jnp.zeros_like(dsc_ref)

        xe, pos = _extended(up_ref, uc_ref, un_ref, i, te, seq)
        de, _ = _extended(dp_ref, dc_ref, dn_ref, i, te, seq)
        cpos = pos[HALO:HALO + te]
        for g, w in enumerate(POOL_WINDOWS):
            ls = slice(g * POOL_G, (g + 1) * POOL_G)
            xg = xe[:, ls]
            wg = w_ref[g]
            win = _window_sum(xg, g + 1, n, False)[HALO:HALO + te]
            dlt = (win / _window_count(cpos, w, seq) - xg[HALO:HALO + te]).astype(BF16)
            z = _dot(dlt, wg)
            dpg = de[:, ls]
            dsc_ref[:, ls] += jnp.sum(dpg[HALO:HALO + te] * z, axis=0, keepdims=True)
            dz = (dpg * sc_ref[:, ls]).astype(BF16)
            dw_ref[g] += _dot_tn(dlt, dz[HALO:HALO + te])
            dd = _dot_nt(dz, wg)
            back = _window_sum(dd / _window_count(pos, w, seq), g + 1, n, True)
            du_ref[:, ls] = (back[HALO:HALO + te] - dd[HALO:HALO + te]).astype(BF16)

    return pl.pallas_call(
        body, name=name, grid=(seq // te,),
        in_specs=_halo_specs(te, seq, 3, POOL_W) + _halo_specs(te, seq, 1, POOL_W)
                 + [pl.BlockSpec((4, POOL_G, POOL_G), lambda i: (0, 0, 0)), pl.BlockSpec((1, POOL_W), lambda i: (0, 0))],
        out_specs=[pl.BlockSpec((te, POOL_W), lambda i: (i, 0)),
                   pl.BlockSpec((4, POOL_G, POOL_G), lambda i: (0, 0, 0)), pl.BlockSpec((1, POOL_W), lambda i: (0, 0))],
        out_shape=[jax.ShapeDtypeStruct((seq, POOL_W), BF16), jax.ShapeDtypeStruct((4, POOL_G, POOL_G), F32),
                   jax.ShapeDtypeStruct((1, POOL_W), F32)],
        compiler_params=_params(("arbitrary",)),
    )(qkvu, qkvu, qkvu, dmix, dmix, dmix, pool_w, pool_scale)


def _shifted(z, zprev_row, znext_row, te):
    rows = lax.broadcasted_iota(jnp.int32, (te, 1), 0)
    zp = jnp.where(rows == 0, zprev_row, pltpu.roll(z, 1, 0))
    zn = jnp.where(rows == te - 1, znext_row, pltpu.roll(z, te - 1, 0))
    return zp, zn


def _edge_rows(prev_ref, next_ref, i, nt):
    p = prev_ref[HALO - 1:HALO, :].astype(F32)
    q = next_ref[0:1, :].astype(F32)
    return jnp.where(i == 0, 0.0, p), jnp.where(i == nt - 1, 0.0, q)


def conv_fwd(proj, conv_w, name):
    seq = proj.shape[0]
    te = _tile(seq, (512, 256))
    nt = seq // te

    def body(bg_ref, cp_ref, cc_ref, cn_ref, xp_ref, xc_ref, xn_ref, w_ref, o_ref):
        i = pl.program_id(0)
        z = cc_ref[...].astype(F32) * xc_ref[...].astype(F32)
        cpr, cnr = _edge_rows(cp_ref, cn_ref, i, nt)
        xpr, xnr = _edge_rows(xp_ref, xn_ref, i, nt)
        zp, zn = _shifted(z, cpr * xpr, cnr * xnr, te)
        y = zp * w_ref[0:1, :] + z * w_ref[1:2, :] + zn * w_ref[2:3, :]
        o_ref[...] = (bg_ref[...].astype(F32) * y).astype(BF16)

    return pl.pallas_call(
        body, name=name, grid=(nt,),
        in_specs=[pl.BlockSpec((te, D), lambda i: (i, 0))] + _halo_specs(te, seq, 1, D) + _halo_specs(te, seq, 2, D)
                 + [pl.BlockSpec((3, D), lambda i: (0, 0))],
        out_specs=pl.BlockSpec((te, D), lambda i: (i, 0)),
        out_shape=jax.ShapeDtypeStruct((seq, D), BF16),
        compiler_params=_params(("parallel",)),
    )(proj, proj, proj, proj, proj, proj, proj, conv_w)


def conv_bwd(proj, dgm, conv_w, name):
    seq = proj.shape[0]
    te = _tile(seq, (512, 256))
    nt = seq // te

    def body(bp_ref, bc_ref, bn_ref, cp_ref, cc_ref, cn_ref, xp_ref, xc_ref, xn_ref, gp_ref, gc_ref, gn_ref, w_ref,
             dbg_ref, dcg_ref, dxin_ref, dw_ref):
        i = pl.program_id(0)

        @pl.when(i == 0)
        def _():
            dw_ref[...] = jnp.zeros_like(dw_ref)

        bg = bc_ref[...].astype(F32)
        cg = cc_ref[...].astype(F32)
        xin = xc_ref[...].astype(F32)
        dg = gc_ref[...].astype(F32)
        z = cg * xin
        cpr, cnr = _edge_rows(cp_ref, cn_ref, i, nt)
        xpr, xnr = _edge_rows(xp_ref, xn_ref, i, nt)
        zp, zn = _shifted(z, cpr * xpr, cnr * xnr, te)
        w0, w1, w2 = w_ref[0:1, :], w_ref[1:2, :], w_ref[2:3, :]
        y = zp * w0 + z * w1 + zn * w2
        dbg_ref[...] = (dg * y).astype(BF16)
        dy = dg * bg
        dw_ref[0:1, :] += jnp.sum(dy * zp, axis=0, keepdims=True)
        dw_ref[1:2, :] += jnp.sum(dy * z, axis=0, keepdims=True)
        dw_ref[2:3, :] += jnp.sum(dy * zn, axis=0, keepdims=True)
        bpr, bnr = _edge_rows(bp_ref, bn_ref, i, nt)
        gpr, gnr = _edge_rows(gp_ref, gn_ref, i, nt)
        dyp, dyn = _shifted(dy, bpr * gpr, bnr * gnr, te)
        dz = dyn * w0 + dy * w1 + dyp * w2
        dcg_ref[...] = (dz * xin).astype(BF16)
        dxin_ref[...] = (dz * cg).astype(BF16)

    row = pl.BlockSpec((te, D), lambda i: (i, 0))
    return pl.pallas_call(
        body, name=name, grid=(nt,),
        in_specs=_halo_specs(te, seq, 0, D) + _halo_specs(te, seq, 1, D) + _halo_specs(te, seq, 2, D)
                 + _halo_specs(te, seq, 0, D) + [pl.BlockSpec((3, D), lambda i: (0, 0))],
        out_specs=[row, row, row, pl.BlockSpec((3, D), lambda i: (0, 0))],
        out_shape=[jax.ShapeDtypeStruct((seq, D), BF16)] * 3 + [jax.ShapeDtypeStruct((3, D), F32)],
        compiler_params=_params(("arbitrary",)),
    )(proj, proj, proj, proj, proj, proj, proj, proj, proj, dgm, dgm, dgm, conv_w)


def _position():
    x, y, c = lax.axis_index("x"), lax.axis_index("y"), lax.axis_index("c")
    return x, y, c, 4 * x + 2 * y + c


def _peer(x, y, c, j):
    px = 1 - x if j & 4 else x
    py = 1 - y if j & 2 else y
    pc = 1 - c if j & 1 else c
    return (px, py, pc), 4 * px + 2 * py + pc


def small_allgather(v, name):
    rows, cols = v.shape

    def body(v_ref, o_ref, send_sems, recv_sems, local_sem):
        x, y, c, me = _position()
        mine = pltpu.make_async_copy(v_ref, o_ref.at[me], local_sem)
        mine.start()
        sends = []
        for j in range(1, N_DEV):
            peer, _ = _peer(x, y, c, j)
            cp = pltpu.make_async_remote_copy(src_ref=v_ref, dst_ref=o_ref.at[me], send_sem=send_sems.at[j - 1],
                                              recv_sem=recv_sems.at[j - 1], device_id=peer, device_id_type=MESH_ID)
            cp.start()
            sends.append(cp)
        for j in range(1, N_DEV):
            peer, pid = _peer(x, y, c, j)
            pltpu.make_async_remote_copy(src_ref=v_ref, dst_ref=o_ref.at[pid], send_sem=send_sems.at[j - 1],
                                         recv_sem=recv_sems.at[j - 1], device_id=peer,
                                         device_id_type=MESH_ID).wait_recv()
        for cp in sends:
            cp.wait_send()
        mine.wait()

    return pl.pallas_call(
        body, name=name,
        out_shape=jax.ShapeDtypeStruct((N_DEV, rows, cols), v.dtype),
        in_specs=[pl.BlockSpec(memory_space=pltpu.VMEM)],
        out_specs=pl.BlockSpec(memory_space=pltpu.VMEM),
        scratch_shapes=[pltpu.SemaphoreType.DMA((N_DEV - 1,)), pltpu.SemaphoreType.DMA((N_DEV - 1,)),
                        pltpu.SemaphoreType.DMA],
        compiler_params=pltpu.CompilerParams(vmem_limit_bytes=VMEM_LIMIT),
    )(v)


class Exchange:
    def __init__(self, kind, arrays):
        self.kind, self.arrays, self.n = kind, list(arrays), len(arrays)
        n = self.n
        if kind == "gather":
            self.out_shapes = [jax.ShapeDtypeStruct((N_DEV,) + a.shape, a.dtype) for a in self.arrays]
        else:
            self.out_shapes = [jax.ShapeDtypeStruct(a.shape, a.dtype) for a in self.arrays]
        self.scratch = [pltpu.SemaphoreType.DMA((7 * n,)), pltpu.SemaphoreType.DMA((7 * n,)),
                        pltpu.SemaphoreType.DMA((n,))]

    def _gather_copies(self, ins, outs, sems):
        send_sems, recv_sems, local_sems = sems
        x, y, c, me = _position()
        chips = [(1 - x, y), (x, 1 - y), (1 - x, 1 - y)]

        def blk(k, px, py, pc):
            return outs[k].at[4 * px + 2 * py + pc]

        def copy(k, slot, block, to, src=None):
            return pltpu.make_async_remote_copy(
                src_ref=blk(k, *block) if src is None else src, dst_ref=blk(k, *block),
                send_sem=send_sems.at[k * 7 + slot], recv_sem=recv_sems.at[k * 7 + slot],
                device_id=to, device_id_type=MESH_ID)

        mine = [pltpu.make_async_copy(ins[k], blk(k, x, y, c), local_sems.at[k]) for k in range(self.n)]
        first = []
        for k in range(self.n):
            first.append(copy(k, 0, (x, y, c), (x, y, 1 - c), src=ins[k]))
            first += [copy(k, 1 + j, (x, y, c), (*chip, c), src=ins[k]) for j, chip in enumerate(chips)]
        return (x, y, c), chips, copy, mine, first

    def start(self, ins, outs, sems):
        if self.kind == "gather":
            _, _, _, mine, first = self._gather_copies(ins, outs, sems)
            for cp in mine + first:
                cp.start()
        else:
            for cp in self._scatter_copies(ins, outs, sems, False):
                cp.start()

    def finish(self, ins, outs, sems):
        if self.kind == "gather":
            (x, y, c), chips, copy, mine, first = self._gather_copies(ins, outs, sems)
            passed = []
            for j, chip in enumerate(chips):
                for k in range(self.n):
                    copy(k, 1 + j, (*chip, c), (x, y, c)).wait_recv()
                    cp = copy(k, 4 + j, (*chip, c), (x, y, 1 - c))
                    cp.start()
                    passed.append(cp)
            for k in range(self.n):
                copy(k, 0, (x, y, 1 - c), (x, y, c)).wait_recv()
                for j, chip in enumerate(chips):
                    copy(k, 4 + j, (*chip, 1 - c), (x, y, c)).wait_recv()
            for cp in first + passed:
                cp.wait_send()
            for cp in mine:
                cp.wait()
        else:
            for cp in self._scatter_copies(ins, outs, sems, True):
                cp.wait_recv()
            copies = self._scatter_copies(ins, outs, sems, False)
            for cp in copies[self.n:]:
                cp.wait_send()
            for cp in copies[:self.n]:
                cp.wait()

    def _scatter_copies(self, ins, outs, sems, arrivals):
        send_sems, recv_sems, local_sems = sems
        x, y, c, me = _position()
        out = []
        if not arrivals:
            out = [pltpu.make_async_copy(ins[k].at[me], outs[k].at[me], local_sems.at[k]) for k in range(self.n)]
        for j in range(1, N_DEV):
            peer, pid = _peer(x, y, c, j)
            for k in range(self.n):
                out.append(pltpu.make_async_remote_copy(
                    src_ref=ins[k].at[pid], dst_ref=outs[k].at[pid if arrivals else me],
                    send_sem=send_sems.at[k * 7 + j - 1], recv_sem=recv_sems.at[k * 7 + j - 1],
                    device_id=peer, device_id_type=MESH_ID))
        return out

    def run(self, name):
        n = self.n

        def body(*refs):
            ins, outs, sems = refs[:n], refs[n:2 * n], refs[2 * n:]
            self.start(ins, outs, sems)
            self.finish(ins, outs, sems)

        hbm = pl.BlockSpec(memory_space=pltpu.HBM)
        return list(pl.pallas_call(
            body, name=name, out_shape=self.out_shapes, in_specs=[hbm] * n, out_specs=[hbm] * n,
            scratch_shapes=self.scratch,
        )(*self.arrays))


def sum_devices(v, name):
    _, rows, cols = v.shape

    def body(v_ref, o_ref):
        acc = v_ref[0]
        for p in range(1, N_DEV):
            acc = acc + v_ref[p]
        o_ref[...] = acc

    return pl.pallas_call(
        body, name=name, out_shape=jax.ShapeDtypeStruct((rows, cols), F32),
        compiler_params=pltpu.CompilerParams(vmem_limit_bytes=VMEM_LIMIT),
    )(v)


def _silu(x):
    return x * _sigmoid(x)


def adaln_fwd(cm, mod_w, mod_b_cols, name):
    cols = mod_w.shape[2]

    def body(c_ref, w_ref, b_ref, o_ref):
        o_ref[0] = jnp.dot(_silu(c_ref[...]), w_ref[0], precision=HI, preferred_element_type=F32) + b_ref[0]

    return pl.pallas_call(
        body, name=name, grid=(2,),
        in_specs=[pl.BlockSpec((16, D), lambda l: (0, 0)), pl.BlockSpec((1, D, cols), lambda l: (l, 0, 0)),
                  pl.BlockSpec((1, 1, cols), lambda l: (l, 0, 0))],
        out_specs=pl.BlockSpec((1, 16, cols), lambda l: (l, 0, 0)),
        out_shape=jax.ShapeDtypeStruct((2, 16, cols), F32),
        compiler_params=_params(("parallel",)),
    )(cm, mod_w, mod_b_cols)


def adaln_bwd(cm_t, mod_w, dm_t, name):
    cols = mod_w.shape[2]

    def body(c_ref, w_ref, lat_ref, ctx_ref, gw_ref, pc_ref):
        ctot = jnp.sum(ctx_ref[0], axis=0, keepdims=True)
        rows = lax.broadcasted_iota(jnp.int32, (8, 1), 0)
        g_hi = jnp.where(rows == 0, ctot, 0.0)
        g = jnp.concatenate([lat_ref[0], g_hi], axis=0)
        gw_ref[0] = jnp.dot(_silu(c_ref[...]), g, precision=HI, preferred_element_type=F32)
        pc_ref[0] = lax.dot_general(g_hi, w_ref[0], NT_DIMS, precision=HI, preferred_element_type=F32)

    return pl.pallas_call(
        body, name=name, grid=(2,),
        in_specs=[pl.BlockSpec((D, 16), lambda l: (0, 0)), pl.BlockSpec((1, D, cols), lambda l: (l, 0, 0)),
                  pl.BlockSpec((1, 8, cols), lambda l: (l, 0, 0)), pl.BlockSpec((1, 8, cols), lambda l: (l + 2, 0, 0))],
        out_specs=[pl.BlockSpec((1, D, cols), lambda l: (l, 0, 0)), pl.BlockSpec((1, 8, D), lambda l: (l, 0, 0))],
        out_shape=[jax.ShapeDtypeStruct((2, D, cols), F32), jax.ShapeDtypeStruct((2, 8, D), F32)],
        compiler_params=_params(("parallel",)),
    )(cm_t, mod_w, dm_t, dm_t)


def mod_b_grad(dm_t, name):
    width = dm_t.shape[2]
    tn = width // 8

    def body(d_ref, o_ref):
        s = jnp.concatenate([jnp.sum(d_ref[k], axis=0, keepdims=True) for k in range(4)]
                            + [jnp.zeros((4, tn), F32)], axis=0)
        o_ref[...] = s + pltpu.roll(s, 6, 0)

    return pl.pallas_call(
        body, name=name, grid=(8,),
        in_specs=[pl.BlockSpec((4, 8, tn), lambda j: (0, 0, j))],
        out_specs=pl.BlockSpec((8, tn), lambda j: (0, j)),
        out_shape=jax.ShapeDtypeStruct((8, width), F32),
        compiler_params=_params(("parallel",)),
    )(dm_t)


def adamw(w, m, v, name, g=None, recv=None):
    rows, cols = w.shape
    tr = _tile(rows, (256, 128, 64, 32, 16, 8))
    summed = recv is not None

    def body(w_ref, m_ref, v_ref, g_ref, go_ref, d_ref, mo_ref, vo_ref):
        if summed:
            gv = g_ref[0].astype(F32)
            for p in range(1, N_DEV):
                gv = gv + g_ref[p].astype(F32)
        else:
            gv = g_ref[...]
        mn = ADAM_B1 * m_ref[...] + (1.0 - ADAM_B1) * gv
        vn = ADAM_B2 * v_ref[...] + (1.0 - ADAM_B2) * (gv * gv)
        m_hat = mn / (1.0 - ADAM_B1 ** ADAM_STEP)
        v_hat = vn / (1.0 - ADAM_B2 ** ADAM_STEP)
        go_ref[...] = gv
        d_ref[...] = -ADAM_LR * (m_hat / (jnp.sqrt(v_hat) + ADAM_EPS) + ADAM_WD * w_ref[...])
        mo_ref[...] = mn
        vo_ref[...] = vn

    row = pl.BlockSpec((tr, cols), lambda i: (i, 0))
    gspec = pl.BlockSpec((N_DEV, tr, cols), lambda i: (0, i, 0)) if summed else row
    return pl.pallas_call(
        body, name=name, grid=(rows // tr,),
        in_specs=[row, row, row, gspec], out_specs=[row] * 4,
        out_shape=[jax.ShapeDtypeStruct((rows, cols), F32)] * 4,
        compiler_params=_params(("parallel",)),
    )(w, m, v, recv if summed else g)


def _ffn_fwd(h, hn, mods, w13, w2, base, tag, nxt=None, exchange=None):
    (a, b, s), exchanged = ffn_up(hn, w13, tag + "_up", exchange)
    outs = mm_nn([s], w2, [0], tag + "_down", res=(h, mods, base + 2, 0.5), nxt=nxt)
    h_new, y = outs[0], outs[1]
    return h_new, (outs[2] if nxt else None), (h, hn, a, b, s, y), exchanged


COLUMN_CUT = ("w13", "ewi", "cwi")
GATHER_FIRST = ("w13_00", "w2_00")
GATHER_IN_FFN = ("ewi", "ewo", "w13_01", "w2_01")
GATHER_IN_ATTN = ("w13_10", "w2_10", "cwi", "cwo", "w13_11", "w2_11")
SCATTER_IN_ATTN = ("w13_11", "w2_11", "cwi", "cwo", "w13_10", "w2_10", "w13_01", "w2_01")
SCATTER_LAST = ("w13_00", "w2_00", "ewi", "ewo")


def unpack_piece(p, g):
    if p.split("_")[0] in COLUMN_CUT:
        return g.transpose(1, 0, 2).reshape(g.shape[1], -1)
    return g.reshape(-1, g.shape[2])


def block_piece(p, full):
    if p.split("_")[0] in COLUMN_CUT:
        return full.reshape(full.shape[0], N_DEV, -1).transpose(1, 0, 2).astype(BF16)
    return full.reshape(N_DEV, -1, full.shape[1]).astype(BF16)


def _ffn_bwd(dy, dres, saved, mods, g, w13, w2, base, tag, prev=None, acc=None):
    h, hn, a, b, s, _ = saved
    ff = w2.shape[0]
    acc = acc or (None, None, None)
    da, db = mm_nt([dy], w2, [0], tag + "_ds", dswiglu=(a, b))
    dw2 = mm_tn(s, dy, tag + "_dw2", acc=acc[2])
    dwa = mm_tn(hn, da, tag + "_dw13a", acc=acc[0])
    dwb = mm_tn(hn, db, tag + "_dw13b", acc=acc[1])
    outs = mm_nt_norm([da, db], w13, [0, ff], h, g, mods, base + 1, dres, tag + "_dhn", prev=prev)
    dh, dshift, dscale, dg = outs[:4]
    return dh, (dwa, dwb, dw2), dg, {base: dshift, base + 1: dscale}, tuple(outs[4:])


def _mod_rows(parts):
    zero = jnp.zeros((1, D), F32)
    return jnp.concatenate([parts.get(k, zero) for k in range(N_MOD)], axis=0)


def local_step(x, ctx, ml, mc, wts, target, shards=None):
    wts = dict(wts)
    ng = wts["norm_g"]
    gvec = lambda l, k: ng[l, k][None, :]
    pool_w16 = wts["pool_w"].astype(BF16)
    grads = {}

    def gather(pieces):
        return Exchange("gather", [shards[p] for p in pieces]) if shards else None

    def arrived(pieces, results):
        for p, g in zip(pieces, results):
            wts[p] = unpack_piece(p, g)

    def ffn_grads(lf, f):
        grads["w13_" + lf] = jnp.concatenate([f[0], f[1]], axis=1)
        grads["w2_" + lf] = f[2]

    xh = normmod(x, gvec(0, 0), ml[0], 0, 1, "l0f1_norm")
    ch = normmod(ctx, gvec(0, 0), mc[0], 0, 1, "l0f1c_norm")
    x1, xn, sv1, got = _ffn_fwd(x, xh, ml[0], wts["w13_00"], wts["w2_00"], 0, "l0f1",
                                nxt=(gvec(0, 1), ml[0], 3, 4), exchange=gather(GATHER_IN_FFN))
    arrived(GATHER_IN_FFN, got)
    c1, cn, sv1c, _ = _ffn_fwd(ctx, ch, mc[0], wts["w13_00"], wts["w2_00"], 0, "l0f1c", nxt=(gvec(0, 1), mc[0], 3, 4))
    qkvu = mm_nn([xn], wts["ewi"], [0], "l0mix_in")
    qkvu_c = mm_nn([cn], wts["ewi"], [0], "l0mix_in_c")
    tab = bias_table(wts["rpb"])
    att, got = attn_fwd(qkvu, qkvu_c, tab, "l0_attn", gather(GATHER_IN_ATTN))
    arrived(GATHER_IN_ATTN, got)
    pool = pool_fwd(qkvu, pool_w16, wts["pool_scale"], "l0_pool")
    x2, ymix, xh = mm_nn([att, pool], wts["ewo"], [0, NA_W], "l0mix_out", res=(x1, ml[0], 5, 1.0),
                         nxt=(gvec(0, 2), ml[0], 6, 7))
    x3, xh, sv2, _ = _ffn_fwd(x2, xh, ml[0], wts["w13_01"], wts["w2_01"], 6, "l0f2", nxt=(gvec(1, 0), ml[1], 0, 1))

    x4, xn1, sv3, _ = _ffn_fwd(x3, xh, ml[1], wts["w13_10"], wts["w2_10"], 0, "l1f1", nxt=(gvec(1, 1), ml[1], 3, 4))
    proj = mm_nn([xn1], wts["cwi"], [0], "l1mix_in")
    gm = conv_fwd(proj, wts["conv_w"], "l1_conv")
    x5, ycv, xh = mm_nn([gm], wts["cwo"], [0], "l1mix_out", res=(x4, ml[1], 5, 1.0), nxt=(gvec(1, 2), ml[1], 6, 7))
    x6, _, sv4, _ = _ffn_fwd(x5, xh, ml[1], wts["w13_11"], wts["w2_11"], 6, "l1f2")

    dx6, loss, dgf, dy, dgate = loss_head(x6, wts["final_g"][None, :], target, (sv4[5], ml[1], 8, 0.5), "loss_head")
    dm1 = {8: dgate}
    dx5, dwf4, dg12, dm_f4, (dy, dgate) = _ffn_bwd(dy, dx6, sv4, ml[1], gvec(1, 2), wts["w13_11"], wts["w2_11"], 6,
                                                   "l1f2", prev=(ycv, ml[1], 5, 1.0))
    ffn_grads("11", dwf4)
    dm1.update({5: dgate, **dm_f4})
    dgm = mm_nt([dy], wts["cwo"], [0], "l1mix_dgm", out_dtype=BF16)
    grads["cwo"] = mm_tn(gm, dy, "l1mix_dwo")
    dbg, dcg, dxin, dconv_w = conv_bwd(proj, dgm, wts["conv_w"], "l1_conv_bwd")
    grads["cwi"] = jnp.concatenate([mm_tn(xn1, t, "l1mix_dwi%d" % k) for k, t in enumerate((dbg, dcg, dxin))], axis=1)
    dx4, dsh, dsc, dg11, dy, dgate = mm_nt_norm([dbg, dcg, dxin], wts["cwi"], [0, D, 2 * D], x4, gvec(1, 1), ml[1], 4,
                                                dx5, "l1mix_dxn", prev=(sv3[5], ml[1], 2, 0.5))
    dm1.update({3: dsh, 4: dsc, 2: dgate})
    dx3, dwf3, dg10, dm_f3, (dy, dgate) = _ffn_bwd(dy, dx4, sv3, ml[1], gvec(1, 0), wts["w13_10"], wts["w2_10"], 0,
                                                   "l1f1", prev=(sv2[5], ml[0], 8, 0.5))
    ffn_grads("10", dwf3)
    dm1.update(dm_f3)
    dm0 = {8: dgate}

    dx2, dwf2, dg02, dm_f2, (dy, dgate) = _ffn_bwd(dy, dx3, sv2, ml[0], gvec(0, 2), wts["w13_01"], wts["w2_01"], 6,
                                                   "l0f2", prev=(ymix, ml[0], 5, 1.0))
    ffn_grads("01", dwf2)
    dm0.update({5: dgate, **dm_f2})
    dmix = mm_nt([dy], wts["ewo"], [0], "l0mix_dmix", out_dtype=BF16)
    dewo = jnp.concatenate([mm_tn(att, dy, "l0mix_dwo_att"), mm_tn(pool, dy, "l0mix_dwo_pool")], axis=0)
    scatter = Exchange("scatter", [block_piece(p, grads.pop(p)) for p in SCATTER_IN_ATTN]) if shards else None
    (dq, dk, dv, dkc, dvc, dtab), got = attn_bwd(qkvu, qkvu_c, tab, dmix, "l0_attn_bwd", scatter)
    recv = dict(zip(SCATTER_IN_ATTN, got))
    du, dpool_w, dpool_scale = pool_bwd(qkvu, dmix, pool_w16, wts["pool_scale"], "l0_pool_bwd")
    drpb = bias_table_bwd(dtab)
    dk16, dv16, dkc16, dvc16 = (t.astype(BF16) for t in (dk, dv, dkc, dvc))
    grads["ewo"] = dewo
    grads["ewi"] = jnp.concatenate([
        mm_tn(xn, dq, "l0mix_dwi_q"),
        mm_tn(cn, dkc16, "l0mix_dwi_kc", acc=mm_tn(xn, dk16, "l0mix_dwi_k")),
        mm_tn(cn, dvc16, "l0mix_dwi_vc", acc=mm_tn(xn, dv16, "l0mix_dwi_v")),
        mm_tn(xn, du, "l0mix_dwi_u")], axis=1)
    dx1, dsh, dsc, dg01, dy, dgate = mm_nt_norm([dq, dk16, dv16, du], wts["ewi"], [0, NA_W, 2 * NA_W, 3 * NA_W], x1,
                                                gvec(0, 1), ml[0], 4, dx2, "l0mix_dxn", prev=(sv1[5], ml[0], 2, 0.5))
    dm0.update({3: dsh, 4: dsc, 2: dgate})
    dc1, dsh_c, dsc_c, dg01c, dy_c, dgate_c = mm_nt_norm([dkc16, dvc16], wts["ewi"], [NA_W, 2 * NA_W], c1, gvec(0, 1),
                                                         mc[0], 4, None, "l0mix_dxn_c", prev=(sv1c[5], mc[0], 2, 0.5))
    dx0, dwf1, dg00, dm_f1, _ = _ffn_bwd(dy, dx1, sv1, ml[0], gvec(0, 0), wts["w13_00"], wts["w2_00"], 0, "l0f1")
    dm0.update(dm_f1)
    _, dwf1, dg00c, dm_f1c, _ = _ffn_bwd(dy_c, dc1, sv1c, mc[0], gvec(0, 0), wts["w13_00"], wts["w2_00"], 0, "l0f1c",
                                         acc=dwf1)
    ffn_grads("00", dwf1)
    dmc0 = {3: dsh_c, 4: dsc_c, 2: dgate_c, **dm_f1c}

    return {
        "loss": loss, "grad_x": dx0,
        "dml": jnp.stack([_mod_rows(dm0), _mod_rows(dm1)]),
        "dmc": jnp.stack([_mod_rows(dmc0), jnp.zeros((N_MOD, D), F32)]),
        "norm_g": jnp.concatenate([dg00 + dg00c, dg01 + dg01c, dg02, dg10, dg11, dg12], axis=0),
        "grads": grads, "recv": recv,
        "rpb": drpb, "pool_w": dpool_w, "pool_scale": dpool_scale, "conv_w": dconv_w, "final_g": dgf,
    }


def _rows_of(v, nrows):
    flat = v.reshape(-1)
    return jnp.pad(flat, (0, nrows * D - flat.shape[0])).reshape(nrows, D)


def kernel(x, c, ctx, c_ctx, mod_w, mod_b, norm_g, ffn_w13, ffn_w2, even_w_in, even_w_out, na_rpb, pool_w, pool_scale, conv_w_in, conv_w, conv_w_out, final_g, loss_target, m_c_ctx, m_mod_w, m_mod_b, m_norm_g, m_ffn_w13, m_ffn_w2, m_even_w_in, m_even_w_out, m_na_rpb, m_pool_w, m_pool_scale, m_conv_w_in, m_conv_w, m_conv_w_out, m_final_g, v_c_ctx, v_mod_w, v_mod_b, v_norm_g, v_ffn_w13, v_ffn_w2, v_even_w_in, v_even_w_out, v_na_rpb, v_pool_w, v_pool_scale, v_conv_w_in, v_conv_w, v_conv_w_out, v_final_g):
    me = 4 * lax.axis_index("x") + 2 * lax.axis_index("y") + lax.axis_index("c")
    ff = ffn_w2.shape[2] * N_DEV
    w13c = ffn_w13.shape[3]
    w2r = ffn_w2.shape[2]
    mcols = mod_w.shape[2]
    gcols = norm_g.shape[2]

    big = {"w13": ffn_w13.reshape(4 * D, w13c), "w2": ffn_w2.reshape(4 * w2r, D), "ewi": even_w_in[0],
           "ewo": even_w_out[0], "cwi": conv_w_in[0], "cwo": conv_w_out[0]}
    names = list(big)
    shards = {"ewi": even_w_in[0].astype(BF16), "ewo": even_w_out[0].astype(BF16),
              "cwi": conv_w_in[0].astype(BF16), "cwo": conv_w_out[0].astype(BF16)}
    for l in range(2):
        for f in range(2):
            shards["w13_%d%d" % (l, f)] = ffn_w13[l, f].astype(BF16)
            shards["w2_%d%d" % (l, f)] = ffn_w2[l, f].astype(BF16)
    first = Exchange("gather", [shards[p] for p in GATHER_FIRST]).run("weights_allgather_first")
    wts = {p: unpack_piece(p, g) for p, g in zip(GATHER_FIRST, first)}

    c_all = small_allgather(jnp.pad(c, ((0, 7), (0, 0))), "cond_allgather")[:, 0, :]
    cm = jnp.concatenate([c_all, c_ctx[None, :], jnp.zeros((7, D), F32)], axis=0)
    mod_b_cols = lax.dynamic_slice(mod_b, (0, me * mcols), (2, mcols))[:, None, :]
    m_cols = adaln_fwd(cm, mod_w, mod_b_cols, "adaln_fwd")
    m_all = small_allgather(m_cols.reshape(32, mcols), "mod_allgather")
    m_full = m_all.reshape(N_DEV, 2, 16, mcols).transpose(1, 2, 0, 3).reshape(2, 16, N_MOD * D)
    ml = lax.dynamic_slice(m_full, (0, me, 0), (2, 1, N_MOD * D)).reshape(2, N_MOD, D)
    mc = m_full[:, 8].reshape(2, N_MOD, D)

    full_norm_g = small_allgather(_rows_of(norm_g, 8), "norm_g_allgather")[:, 0, :2 * 3 * gcols]
    full_norm_g = full_norm_g.reshape(N_DEV, 2, 3, gcols).transpose(1, 2, 0, 3).reshape(2, 3, D)
    full_conv_w = small_allgather(_rows_of(conv_w, 8), "conv_w_allgather")[:, 0, :3 * gcols]
    full_conv_w = full_conv_w.reshape(N_DEV, 3, gcols).transpose(1, 0, 2).reshape(3, D)
    wts.update(norm_g=full_norm_g, conv_w=full_conv_w, rpb=na_rpb[0], pool_w=pool_w[0], pool_scale=pool_scale,
               final_g=final_g)
    out = local_step(x[0], ctx[0], ml, mc, wts, loss_target[0], shards)

    dm_pack = jnp.concatenate([out["dml"].reshape(2, N_MOD * D), out["dmc"].reshape(2, N_MOD * D),
                               jnp.zeros((4, N_MOD * D), F32)], axis=0)
    dm_t = small_allgather(dm_pack, "dmod_allgather").transpose(1, 0, 2)[:4]
    dm_cols = lax.dynamic_slice(dm_t, (0, 0, me * mcols), (4, N_DEV, mcols))
    g_mod_w, pc = adaln_bwd(cm.T, mod_w, dm_cols, "adaln_bwd")
    g_mod_b = mod_b_grad(dm_t, "mod_b_grad")[:2]

    pack = jnp.concatenate([_rows_of(t, 8) for t in (
        out["norm_g"], out["conv_w"], out["final_g"], pc[0, :1] + pc[1, :1], out["pool_scale"], out["loss"],
        out["rpb"])] + [_rows_of(out["pool_w"], 64)], axis=0)
    small = sum_devices(small_allgather(pack, "small_grads_allgather"), "small_grads_sum")
    g_norm_g = lax.dynamic_slice(small[0:6].reshape(2, 3, D), (0, 0, me * gcols), (2, 3, gcols))
    g_conv_w = lax.dynamic_slice(small[8:11], (0, me * gcols), (3, gcols))[None]
    g_final_g = small[16]
    sg = _sigmoid(c_ctx)
    g_c_ctx = small[24] * (sg * (1.0 + c_ctx * (1.0 - sg)))
    g_pool_scale = small[32:33, :POOL_W]
    loss = small[40, 0]
    g_rpb = small[48:52].reshape(-1)[:na_rpb.size].reshape(na_rpb.shape)
    g_pool_w = small[56:120].reshape(pool_w.shape)

    last = Exchange("scatter", [block_piece(p, out["grads"][p]) for p in SCATTER_LAST]).run("grads_scatter_last")
    pieces = {**out["recv"], **dict(zip(SCATTER_LAST, last))}
    lf = ("00", "01", "10", "11")
    recv = {"w13": jnp.concatenate([pieces["w13_" + t] for t in lf], axis=1),
            "w2": jnp.concatenate([pieces["w2_" + t] for t in lf], axis=1),
            "ewi": pieces["ewi"], "ewo": pieces["ewo"], "cwi": pieces["cwi"], "cwo": pieces["cwo"]}

    moments = {"w13": (m_ffn_w13, v_ffn_w13), "w2": (m_ffn_w2, v_ffn_w2), "ewi": (m_even_w_in, v_even_w_in),
               "ewo": (m_even_w_out, v_even_w_out), "cwi": (m_conv_w_in, v_conv_w_in),
               "cwo": (m_conv_w_out, v_conv_w_out)}
    orig = {"w13": ffn_w13, "w2": ffn_w2, "ewi": even_w_in, "ewo": even_w_out, "cwi": conv_w_in, "cwo": conv_w_out}
    upd = {}
    for k in names:
        shp2 = big[k].shape
        res = adamw(big[k], moments[k][0].reshape(shp2), moments[k][1].reshape(shp2), "adamw_" + k, recv=recv[k])
        upd[k] = [r.reshape(orig[k].shape) for r in res]
    shp2 = (2 * D, mcols)
    upd["mod_w"] = [r.reshape(mod_w.shape) for r in adamw(mod_w.reshape(shp2), m_mod_w.reshape(shp2),
                                                          v_mod_w.reshape(shp2), "adamw_mod_w",
                                                          g=g_mod_w.reshape(shp2))]

    smalls = [("c_ctx", c_ctx, m_c_ctx, v_c_ctx, g_c_ctx, 8), ("mod_b", mod_b, m_mod_b, v_mod_b, g_mod_b, 24),
              ("norm_g", norm_g, m_norm_g, v_norm_g, g_norm_g, 8), ("rpb", na_rpb, m_na_rpb, v_na_rpb, g_rpb, 8),
              ("pool_w", pool_w, m_pool_w, v_pool_w, g_pool_w, 64),
              ("pool_scale", pool_scale, m_pool_scale, v_pool_scale, g_pool_scale, 8),
              ("conv_w", conv_w, m_conv_w, v_conv_w, g_conv_w, 8), ("final_g", final_g, m_final_g, v_final_g, g_final_g, 8)]
    packed = [jnp.concatenate([_rows_of(s[col], s[5]) for s in smalls], axis=0) for col in (1, 2, 3, 4)]
    res = adamw(packed[0], packed[1], packed[2], "adamw_small", g=packed[3])
    row = 0
    for name, w, _, _, _, nrows in smalls:
        upd[name] = [r[row:row + nrows].reshape(-1)[:w.size].reshape(w.shape) for r in res]
        row += nrows

    order = ["c_ctx", "mod_w", "mod_b", "norm_g", "w13", "w2", "ewi", "ewo", "rpb", "pool_w", "pool_scale", "cwi",
             "conv_w", "cwo", "final_g"]
    grad_x = out["grad_x"][None]
    return (loss, grad_x, *[upd[k][0] for k in order], *[upd[k][1] for k in order], *[upd[k][2] for k in order],
            *[upd[k][3] for k in order])
```

```python
import functools

import numpy as np
import jax
import jax.numpy as jnp
from jax import lax
from jax.experimental import pallas as pl
from jax.experimental.pallas import tpu as pltpu

D = 1024
FF = 2816
SEQ = 16384
CTX = 256
GRID_W = 64
N_MOD = 9
HEADS = 8
HEAD_DIM = 64
NA_W = 512
POOL_W = 512
POOL_G = 128
POOL_WINDOWS = (2, 4, 8, 16)
KH = 8
KW = 16
RMS_EPS = 1e-6
NEG_INF = -1e30
N_DEV = 8

ADAM_LR = 0.001
ADAM_B1 = 0.9
ADAM_B2 = 0.999
ADAM_EPS = 1e-08
ADAM_WD = 0.01
ADAM_STEP = 10

VMEM_LIMIT = 52 * 1024 * 1024
HALO = 16
QROWS = 8
WROWS = 24

BF16 = jnp.bfloat16
F32 = jnp.float32
MESH_ID = pl.DeviceIdType.MESH
HI = lax.Precision.HIGHEST

NT_DIMS = (((1,), (1,)), ((), ()))
TN_DIMS = (((0,), (0,)), ((), ()))


def _tile(n, cands):
    for c in cands:
        if n % c == 0:
            return c
    return n


def _params(sem):
    return pltpu.CompilerParams(dimension_semantics=sem, vmem_limit_bytes=VMEM_LIMIT)


def _dot(a, b):
    return jnp.dot(a, b, preferred_element_type=F32)


def _dot_nt(a, b):
    return lax.dot_general(a, b, NT_DIMS, preferred_element_type=F32)


def _dot_tn(a, b):
    return lax.dot_general(a, b, TN_DIMS, preferred_element_type=F32)


def _sigmoid(x):
    return 1.0 / (1.0 + jnp.exp(-x))


def normmod(h, g, mods, i_shift, i_scale, name):
    n = h.shape[0]
    te = _tile(n, (512, 256))

    def body(h_ref, g_ref, m_ref, o_ref):
        x = h_ref[...]
        r = lax.rsqrt(jnp.mean(x * x, axis=-1, keepdims=True) + RMS_EPS)
        y = x * r * g_ref[...]
        o_ref[...] = (y * (1.0 + m_ref[i_scale:i_scale + 1, :]) + m_ref[i_shift:i_shift + 1, :]).astype(BF16)

    return pl.pallas_call(
        body, name=name, grid=(n // te,),
        in_specs=[pl.BlockSpec((te, D), lambda i: (i, 0)),
                  pl.BlockSpec((1, D), lambda i: (0, 0)),
                  pl.BlockSpec((N_MOD, D), lambda i: (0, 0))],
        out_specs=pl.BlockSpec((te, D), lambda i: (i, 0)),
        out_shape=jax.ShapeDtypeStruct((n, D), BF16),
        compiler_params=_params(("parallel",)),
    )(h, g, mods)


def loss_head(x, g, target, prev, name):
    n = x.shape[0]
    te = _tile(n, (256,))
    i_gate, coef = prev[2], prev[3]

    def body(x_ref, g_ref, t_ref, y_ref, m_ref, dx_ref, loss_ref, dg_ref, dy_ref, dgate_ref):
        @pl.when(pl.program_id(0) == 0)
        def _():
            loss_ref[...] = jnp.zeros_like(loss_ref)
            dg_ref[...] = jnp.zeros_like(dg_ref)
            dgate_ref[...] = jnp.zeros_like(dgate_ref)

        xv = x_ref[...]
        gv = g_ref[...]
        r = lax.rsqrt(jnp.mean(xv * xv, axis=-1, keepdims=True) + RMS_EPS)
        xhat = xv * r
        e = xhat * gv - t_ref[...]
        per_tok = jnp.mean(e * e, axis=-1, keepdims=True)
        loss_ref[...] += 0.5 * jnp.sum(per_tok, axis=0, keepdims=True)
        dy = e * (1.0 / D)
        dg_ref[...] += jnp.sum(dy * xhat, axis=0, keepdims=True)
        dxhat = dy * gv
        dx = r * (dxhat - xhat * jnp.mean(dxhat * xhat, axis=-1, keepdims=True))
        dx_ref[...] = dx
        dy_ref[...] = (dx * (coef * m_ref[i_gate:i_gate + 1, :])).astype(BF16)
        dgate_ref[...] += coef * jnp.sum(dx * y_ref[...].astype(F32), axis=0, keepdims=True)

    row = pl.BlockSpec((te, D), lambda i: (i, 0))
    vec = pl.BlockSpec((1, D), lambda i: (0, 0))
    return pl.pallas_call(
        body, name=name, grid=(n // te,),
        in_specs=[row, vec, row, row, pl.BlockSpec((N_MOD, D), lambda i: (0, 0))],
        out_specs=[row, pl.BlockSpec((1, 128), lambda i: (0, 0)), vec, row, vec],
        out_shape=[jax.ShapeDtypeStruct((n, D), F32), jax.ShapeDtypeStruct((1, 128), F32),
                   jax.ShapeDtypeStruct((1, D), F32), jax.ShapeDtypeStruct((n, D), BF16),
                   jax.ShapeDtypeStruct((1, D), F32)],
        compiler_params=_params(("arbitrary",)),
    )(x, g, target, prev[0], prev[1])


def ffn_up(hn, w13, name, exchange=None):
    n = hn.shape[0]
    ff = w13.shape[1] // 2
    tm = _tile(n, (512, 256))
    tn = _tile(ff, (1408, 512, 256, 128))
    nj = ff // tn
    ni = n // tm
    nx = exchange.n if exchange else 0

    def body(*refs):
        h_ref, wa_ref, wb_ref = refs[:3]
        x_in = refs[3:3 + nx]
        p_ref, u_ref, s_ref = refs[3 + nx:6 + nx]
        x_out = refs[6 + nx:6 + 2 * nx]
        x_sems = refs[6 + 2 * nx:]
        if exchange:
            @pl.when((pl.program_id(0) == 0) & (pl.program_id(1) == 0))
            def _():
                exchange.start(x_in, x_out, x_sems)

        hv = h_ref[...]
        a = _dot(hv, wa_ref[...])
        b = _dot(hv, wb_ref[...])
        sig = _sigmoid(a)
        p = a * sig
        p_ref[...] = p.astype(BF16)
        u_ref[...] = (b * (sig * (1.0 + a * (1.0 - sig)))).astype(BF16)
        s_ref[...] = (p * b).astype(BF16)

        if exchange:
            @pl.when((pl.program_id(0) == nj - 1) & (pl.program_id(1) == ni - 1))
            def _():
                exchange.finish(x_in, x_out, x_sems)

    out = pl.BlockSpec((tm, tn), lambda j, i: (i, j))
    hbm = pl.BlockSpec(memory_space=pltpu.HBM)
    sem = ("arbitrary", "arbitrary") if exchange else ("parallel", "parallel")
    res = pl.pallas_call(
        body, name=name, grid=(nj, ni),
        in_specs=[pl.BlockSpec((tm, D), lambda j, i: (i, 0)),
                  pl.BlockSpec((D, tn), lambda j, i: (0, j)),
                  pl.BlockSpec((D, tn), lambda j, i: (0, j + nj))] + [hbm] * nx,
        out_specs=[out, out, out] + [hbm] * nx,
        out_shape=[jax.ShapeDtypeStruct((n, ff), BF16)] * 3 + (exchange.out_shapes if exchange else []),
        scratch_shapes=exchange.scratch if exchange else [],
        compiler_params=_params(sem),
    )(hn, w13, w13, *(exchange.arrays if exchange else []))
    return res[:3], list(res[3:])


def mm_nn(a_list, w, row_offs, name, out_dtype=BF16, res=None, nxt=None):
    n = a_list[0].shape[0]
    nout = w.shape[1]
    ks = [a.shape[1] for a in a_list]
    tm = _tile(n, (512, 256))
    tn = _tile(nout, (1024, 512, 256, 128))
    na = len(a_list)
    assert nxt is None or (res is not None and tn == D)

    def body(*refs):
        a_refs = refs[:na]
        w_refs = refs[na:2 * na]
        acc = _dot(a_refs[0][...], w_refs[0][...])
        for k in range(1, na):
            acc += _dot(a_refs[k][...], w_refs[k][...])
        if res is None:
            refs[2 * na][...] = acc.astype(out_dtype)
        else:
            h_ref, m_ref = refs[2 * na:2 * na + 2]
            i_gate, coef = res[2], res[3]
            h_new = h_ref[...] + (coef * m_ref[i_gate:i_gate + 1, :]) * acc
            if nxt is None:
                hn_ref, y_ref = refs[2 * na + 2:]
            else:
                g2_ref, m2_ref, hn_ref, y_ref, nx_ref = refs[2 * na + 2:]
                r = lax.rsqrt(jnp.mean(h_new * h_new, axis=-1, keepdims=True) + RMS_EPS)
                nx_ref[...] = ((h_new * r * g2_ref[...]) * (1.0 + m2_ref[nxt[3]:nxt[3] + 1, :])
                               + m2_ref[nxt[2]:nxt[2] + 1, :]).astype(BF16)
            hn_ref[...] = h_new
            y_ref[...] = acc.astype(BF16)

    in_specs = [pl.BlockSpec((tm, k), lambda j, i: (i, 0)) for k in ks]
    for k, off in zip(ks, row_offs):
        in_specs.append(pl.BlockSpec((k, tn), functools.partial(lambda j, i, ob: (ob, j), ob=off // k)))
    args = list(a_list) + [w] * na
    out = pl.BlockSpec((tm, tn), lambda j, i: (i, j))
    if res is None:
        out_specs = out
        out_shape = jax.ShapeDtypeStruct((n, nout), out_dtype)
    else:
        in_specs += [out, pl.BlockSpec((N_MOD, tn), lambda j, i: (0, j))]
        args += [res[0], res[1]]
        out_specs = [out, out]
        out_shape = [jax.ShapeDtypeStruct((n, nout), F32), jax.ShapeDtypeStruct((n, nout), BF16)]
        if nxt is not None:
            in_specs += [pl.BlockSpec((1, D), lambda j, i: (0, 0)), pl.BlockSpec((N_MOD, D), lambda j, i: (0, 0))]
            args += [nxt[0], nxt[1]]
            out_specs.append(out)
            out_shape.append(jax.ShapeDtypeStruct((n, nout), BF16))
    return pl.pallas_call(
        body, name=name, grid=(nout // tn, n // tm),
        in_specs=in_specs, out_specs=out_specs, out_shape=out_shape,
        compiler_params=_params(("parallel", "parallel")),
    )(*args)


def mm_nt(g, w, name, dswiglu=None):
    n, kg = g.shape
    nout = w.shape[0]
    tm = _tile(n, (512, 256))
    tn = _tile(nout, (1408, 1024, 512, 256, 128))

    def body(*refs):
        r = _dot_nt(refs[0][...], refs[1][...])
        if dswiglu is None:
            refs[2][...] = r.astype(BF16)
        else:
            u_ref, p_ref, da_ref, db_ref = refs[2:]
            da_ref[...] = (r * u_ref[...].astype(F32)).astype(BF16)
            db_ref[...] = (r * p_ref[...].astype(F32)).astype(BF16)

    out = pl.BlockSpec((tm, tn), lambda j, i: (i, j))
    in_specs = [pl.BlockSpec((tm, kg), lambda j, i: (i, 0)), pl.BlockSpec((tn, kg), lambda j, i: (j, 0))]
    args = [g, w]
    if dswiglu is None:
        out_specs = out
        out_shape = jax.ShapeDtypeStruct((n, nout), BF16)
    else:
        in_specs += [out, out]
        args += list(dswiglu)
        out_specs = [out, out]
        out_shape = [jax.ShapeDtypeStruct((n, nout), BF16)] * 2
    return pl.pallas_call(
        body, name=name, grid=(nout // tn, n // tm),
        in_specs=in_specs, out_specs=out_specs, out_shape=out_shape,
        compiler_params=_params(("parallel", "parallel")),
    )(*args)


def mm_nt_norm(g_list, w, col_offs, h, g, mods, i_scale, dres, name, prev=None, exchange=None):
    n = h.shape[0]
    kg = g_list[0].shape[1]
    tm = _tile(n, (512, 256))
    tk = _tile(kg, (1408, 1024, 512, 256, 128))
    ng = len(g_list)
    nk = kg // tk
    ni = n // tm
    has_res = dres is not None
    nx = exchange.n if exchange else 0

    def body(*refs):
        g_refs = refs[:ng]
        w_refs = refs[ng:2 * ng]
        pos = 2 * ng
        h_ref, gv_ref, m_ref = refs[pos:pos + 3]
        pos += 3
        if has_res:
            dres_ref = refs[pos]
            pos += 1
        if prev is not None:
            y_ref, mp_ref = refs[pos:pos + 2]
            pos += 2
        x_in = refs[pos:pos + nx]
        pos += nx
        dh_ref, dshift_ref, dscale_ref, dg_ref = refs[pos:pos + 4]
        pos += 4
        if prev is not None:
            dy_ref, dgate_ref = refs[pos:pos + 2]
            pos += 2
        x_out = refs[pos:pos + nx]
        pos += nx
        acc_ref = refs[pos]
        x_sems = refs[pos + 1:]
        i = pl.program_id(0)
        k = pl.program_id(1)

        @pl.when((i == 0) & (k == 0))
        def _():
            if exchange:
                exchange.start(x_in, x_out, x_sems)
            dshift_ref[...] = jnp.zeros_like(dshift_ref)
            dscale_ref[...] = jnp.zeros_like(dscale_ref)
            dg_ref[...] = jnp.zeros_like(dg_ref)
            if prev is not None:
                dgate_ref[...] = jnp.zeros_like(dgate_ref)

        @pl.when(k == 0)
        def _():
            acc_ref[...] = jnp.zeros_like(acc_ref)

        acc = _dot_nt(g_refs[0][...], w_refs[0][...])
        for q in range(1, ng):
            acc += _dot_nt(g_refs[q][...], w_refs[q][...])
        acc_ref[...] += acc

        @pl.when(k == nk - 1)
        def _():
            d = acc_ref[...]
            x = h_ref[...]
            gv = gv_ref[...]
            r = lax.rsqrt(jnp.mean(x * x, axis=-1, keepdims=True) + RMS_EPS)
            xhat = x * r
            dshift_ref[...] += jnp.sum(d, axis=0, keepdims=True)
            dscale_ref[...] += jnp.sum(d * (xhat * gv), axis=0, keepdims=True)
            dn = d * (1.0 + m_ref[i_scale:i_scale + 1, :])
            dg_ref[...] += jnp.sum(dn * xhat, axis=0, keepdims=True)
            dxhat = dn * gv
            dh = r * (dxhat - xhat * jnp.mean(dxhat * xhat, axis=-1, keepdims=True))
            if has_res:
                dh = dh + dres_ref[...]
            dh_ref[...] = dh
            if prev is not None:
                i_gate, coef = prev[2], prev[3]
                dy_ref[...] = (dh * (coef * mp_ref[i_gate:i_gate + 1, :])).astype(BF16)
                dgate_ref[...] += coef * jnp.sum(dh * y_ref[...].astype(F32), axis=0, keepdims=True)

        if exchange:
            @pl.when((i == ni - 1) & (k == nk - 1))
            def _():
                exchange.finish(x_in, x_out, x_sems)

    row = pl.BlockSpec((tm, D), lambda i, k: (i, 0))
    vec = pl.BlockSpec((1, D), lambda i, k: (0, 0))
    modspec = pl.BlockSpec((N_MOD, D), lambda i, k: (0, 0))
    in_specs = [pl.BlockSpec((tm, tk), lambda i, k: (i, k)) for _ in g_list]
    for off in col_offs:
        in_specs.append(pl.BlockSpec((D, tk), functools.partial(lambda i, k, ob: (0, ob + k), ob=off // tk)))
    in_specs += [row, vec, modspec]
    args = list(g_list) + [w] * ng + [h, g, mods]
    out_specs = [row, vec, vec, vec]
    out_shape = [jax.ShapeDtypeStruct((n, D), F32)] + [jax.ShapeDtypeStruct((1, D), F32)] * 3
    if has_res:
        in_specs.append(row)
        args.append(dres)
    if prev is not None:
        in_specs += [row, modspec]
        args += [prev[0], prev[1]]
        out_specs += [row, vec]
        out_shape += [jax.ShapeDtypeStruct((n, D), BF16), jax.ShapeDtypeStruct((1, D), F32)]
    scratch = [pltpu.VMEM((tm, D), F32)]
    if exchange:
        hbm = pl.BlockSpec(memory_space=pltpu.HBM)
        in_specs += [hbm] * nx
        args += exchange.arrays
        out_specs += [hbm] * nx
        out_shape += exchange.out_shapes
        scratch += exchange.scratch
    return pl.pallas_call(
        body, name=name, grid=(ni, nk),
        in_specs=in_specs, out_specs=out_specs, out_shape=out_shape, scratch_shapes=scratch,
        compiler_params=_params(("arbitrary", "arbitrary")),
    )(*args)


def mm_tn(a, g, name, acc=None):
    n, ka = a.shape
    ngc = g.shape[1]
    tka = _tile(ka, (1408, 1024, 512, 256, 128))
    tng = _tile(ngc, (1408, 1024, 512, 256, 128))
    tr = _tile(n, (512, 256))
    has_acc = acc is not None

    def body(*refs):
        a_ref, g_ref = refs[0], refs[1]
        o_ref = refs[-1]
        r = pl.program_id(2)

        @pl.when(r == 0)
        def _():
            if has_acc:
                o_ref[...] = refs[2][...]
            else:
                o_ref[...] = jnp.zeros_like(o_ref)

        o_ref[...] += _dot_tn(a_ref[...], g_ref[...])

    out = pl.BlockSpec((tka, tng), lambda p, q, r: (p, q))
    in_specs = [pl.BlockSpec((tr, tka), lambda p, q, r: (r, p)),
                pl.BlockSpec((tr, tng), lambda p, q, r: (r, q))]
    args = [a, g]
    if has_acc:
        in_specs.append(out)
        args.append(acc)
    return pl.pallas_call(
        body, name=name, grid=(ka // tka, ngc // tng, n // tr),
        in_specs=in_specs, out_specs=out,
        out_shape=jax.ShapeDtypeStruct((ka, ngc), F32),
        compiler_params=_params(("parallel", "parallel", "arbitrary")),
    )(*args)


def mm_small(a, b, name, trans_b=False):
    m = a.shape[0]
    nout = b.shape[0] if trans_b else b.shape[1]

    def body(a_ref, b_ref, o_ref):
        if trans_b:
            o_ref[...] = lax.dot_general(a_ref[...], b_ref[...], NT_DIMS, precision=HI, preferred_element_type=F32)
        else:
            o_ref[...] = jnp.dot(a_ref[...], b_ref[...], precision=HI, preferred_element_type=F32)

    return pl.pallas_call(
        body, name=name,
        out_shape=jax.ShapeDtypeStruct((m, nout), F32),
        compiler_params=pltpu.CompilerParams(vmem_limit_bytes=VMEM_LIMIT),
    )(a, b)


def _col_tables():
    col = np.arange(GRID_W)
    start = np.clip(col - KW // 2, 0, GRID_W - KW)
    ok = (col[None, :] >= start[:, None]) & (col[None, :] < start[:, None] + KW)
    ci = np.clip(col[None, :] - col[:, None] + (KW - 1), 0, 2 * KW - 2)
    e = np.zeros((2 * KW - 1, GRID_W, GRID_W), np.float32)
    for c in range(2 * KW - 1):
        e[c] = (ci == c) & ok
    return e.reshape(2 * KW - 1, GRID_W * GRID_W), ok


def bias_table(rpb):
    e, ok = _col_tables()
    e_pad = np.zeros((32, GRID_W * GRID_W), np.float32)
    e_pad[:31] = e
    rp = jnp.pad(rpb.reshape(HEADS * 15, 31), ((0, 0), (0, 1)))
    t = mm_small(rp, jnp.asarray(e_pad), "rpb_expand").reshape(HEADS, 15, GRID_W, GRID_W)
    t = jnp.where(jnp.asarray(ok)[None, None], t, NEG_INF)
    tab = jnp.stack([t[:, v:v + KH] for v in range(8)], axis=0)
    return tab.transpose(0, 1, 3, 2, 4).reshape(TAB_SHAPE)


def bias_table_bwd(dtab):
    e, _ = _col_tables()
    e_pad = np.zeros((128, GRID_W * GRID_W), np.float32)
    e_pad[:31] = e
    d = dtab.reshape(8, HEADS, GRID_W, KH, GRID_W).transpose(0, 1, 3, 2, 4).reshape(8 * HEADS * KH, GRID_W * GRID_W)
    gv = mm_small(d, jnp.asarray(e_pad), "rpb_reduce", trans_b=True)[:, :31]
    gv = gv.reshape(8, HEADS, KH, 31).transpose(0, 2, 1, 3).reshape(8 * KH, HEADS * 31)
    sel = np.zeros((16, 8 * KH), np.float32)
    for v in range(8):
        for j in range(KH):
            sel[v + j, v * KH + j] = 1.0
    gpad = jnp.pad(gv, ((0, 0), (0, 256 - HEADS * 31)))
    out = mm_small(jnp.asarray(sel), gpad, "rpb_fold")[:15, :HEADS * 31]
    return out.reshape(15, HEADS, 31).transpose(1, 0, 2)


def _attn_geometry(seq):
    rows = seq // GRID_W
    nb = rows // QROWS
    return rows, nb


def _stack_heads(t2):
    first = (lax.broadcasted_iota(jnp.int32, (1, 128), 1) // HEAD_DIM) == 0
    zero = jnp.zeros_like(t2)
    return jnp.concatenate([jnp.where(first, t2, zero), jnp.where(first, zero, t2)], axis=0)


def _unstack_heads(t):
    first = (lax.broadcasted_iota(jnp.int32, (1, 128), 1) // HEAD_DIM) == 0
    return jnp.where(first, t[0:GRID_W], t[GRID_W:2 * GRID_W])


TAB_SHAPE = (8, HEADS // 2, 2 * GRID_W, KH * GRID_W)


def attn_fwd(qkvu, qkvu_c, tab, name, exchange=None):
    seq = qkvu.shape[0]
    nctx = qkvu_c.shape[0]
    rows, nb = _attn_geometry(seq)
    qt = QROWS * GRID_W
    wt = WROWS * GRID_W
    scale = HEAD_DIM ** -0.5
    nx = exchange.n if exchange else 0

    def wb0(i):
        return jnp.clip(i - 1, 0, nb - 3)

    def body(*refs):
        q_ref, k0, k1, k2, v0, v1, v2, kc_ref, vc_ref, tab_hbm = refs[:10]
        x_in = refs[10:10 + nx]
        o_ref = refs[10 + nx]
        x_out = refs[11 + nx:11 + 2 * nx]
        kbuf, vbuf, tab_s, sem = refs[11 + 2 * nx:15 + 2 * nx]
        x_sems = refs[15 + 2 * nx:]
        i = pl.program_id(0)

        @pl.when(i == 0)
        def _():
            if exchange:
                exchange.start(x_in, x_out, x_sems)
            cp = pltpu.make_async_copy(tab_hbm, tab_s, sem)
            cp.start()
            cp.wait()

        for t, (kr, vr) in enumerate(((k0, v0), (k1, v1), (k2, v2))):
            kbuf[t * qt:(t + 1) * qt, :] = kr[...]
            vbuf[t * qt:(t + 1) * qt, :] = vr[...]
        base = wb0(i) * QROWS

        def row_body(rl, carry):
            r = i * QROWS + rl
            rs = jnp.clip(r - KH // 2, 0, rows - KH)
            vi = rs - r + (KH - 1)
            off = pl.multiple_of((rs - base) * GRID_W, GRID_W)
            qoff = pl.multiple_of(rl * GRID_W, GRID_W)
            for p in range(HEADS // 2):
                ls = slice(p * 128, (p + 1) * 128)
                qst = _stack_heads(q_ref[pl.ds(qoff, GRID_W), ls])
                k2v = kbuf[pl.ds(off, KH * GRID_W), ls]
                v2v = vbuf[pl.ds(off, KH * GRID_W), ls]
                s_w = _dot_nt(qst, k2v) * scale + tab_s[vi, p]
                s_c = _dot_nt(qst, kc_ref[:, ls]) * scale
                m = jnp.maximum(jnp.max(s_w, axis=-1, keepdims=True), jnp.max(s_c, axis=-1, keepdims=True))
                pw = jnp.exp(s_w - m)
                pc = jnp.exp(s_c - m)
                l = jnp.sum(pw, axis=-1, keepdims=True) + jnp.sum(pc, axis=-1, keepdims=True)
                o = _dot(pw.astype(BF16), v2v) + _dot(pc.astype(BF16), vc_ref[:, ls])
                o_ref[pl.ds(qoff, GRID_W), ls] = _unstack_heads(o * (1.0 / l)).astype(BF16)
            return carry

        lax.fori_loop(0, QROWS, row_body, 0)

        if exchange:
            @pl.when(i == nb - 1)
            def _():
                exchange.finish(x_in, x_out, x_sems)

    blk = lambda col: [pl.BlockSpec((qt, NA_W), functools.partial(lambda i, t, c: (wb0(i) + t, c), t=t, c=col))
                       for t in range(3)]
    hbm = pl.BlockSpec(memory_space=pltpu.HBM)
    res = pl.pallas_call(
        body, name=name, grid=(nb,),
        in_specs=[pl.BlockSpec((qt, NA_W), lambda i: (i, 0))] + blk(1) + blk(2)
                 + [pl.BlockSpec((nctx, NA_W), lambda i: (0, 1)), pl.BlockSpec((nctx, NA_W), lambda i: (0, 2)),
                    pl.BlockSpec(memory_space=pl.ANY)] + [hbm] * nx,
        out_specs=[pl.BlockSpec((qt, NA_W), lambda i: (i, 0))] + [hbm] * nx,
        out_shape=[jax.ShapeDtypeStruct((seq, NA_W), BF16)] + (exchange.out_shapes if exchange else []),
        scratch_shapes=[pltpu.VMEM((wt, NA_W), BF16), pltpu.VMEM((wt, NA_W), BF16),
                        pltpu.VMEM(TAB_SHAPE, F32), pltpu.SemaphoreType.DMA] + (exchange.scratch if exchange else []),
        compiler_params=_params(("arbitrary",)),
    )(qkvu, qkvu, qkvu, qkvu, qkvu, qkvu, qkvu, qkvu_c, qkvu_c, tab, *(exchange.arrays if exchange else []))
    return res[0], list(res[1:])


def attn_bwd(qkvu, qkvu_c, tab, dmix, name, exchange=None):
    seq = qkvu.shape[0]
    nctx = qkvu_c.shape[0]
    rows, nb = _attn_geometry(seq)
    qt = QROWS * GRID_W
    wt = WROWS * GRID_W
    scale = HEAD_DIM ** -0.5
    nx = exchange.n if exchange else 0

    def wb0(i):
        return jnp.clip(i - 1, 0, nb - 3)

    def body(*refs):
        q_ref, k0, k1, k2, v0, v1, v2, kc_ref, vc_ref, do_ref, tab_hbm = refs[:11]
        x_in = refs[11:11 + nx]
        dq_ref, dk_hbm, dv_hbm, dkc_ref, dvc_ref, dtab_hbm = refs[11 + nx:17 + nx]
        x_out = refs[17 + nx:17 + 2 * nx]
        kbuf, vbuf, dkacc, dvacc, tab_s, dtab_s, sem = refs[17 + 2 * nx:24 + 2 * nx]
        x_sems = refs[24 + 2 * nx:]
        i = pl.program_id(0)

        if exchange:
            @pl.when(i == 0)
            def _():
                exchange.start(x_in, x_out, x_sems)

        def flush(src, dst, src_row, dst_row, nrows):
            cp = pltpu.make_async_copy(src.at[pl.ds(src_row, nrows)], dst.at[pl.ds(dst_row, nrows)], sem)
            cp.start()
            cp.wait()

        @pl.when(i == 0)
        def _():
            cp = pltpu.make_async_copy(tab_hbm, tab_s, sem)
            cp.start()
            cp.wait()
            dtab_s[...] = jnp.zeros_like(dtab_s)
            dkacc[...] = jnp.zeros_like(dkacc)
            dvacc[...] = jnp.zeros_like(dvacc)
            dkc_ref[...] = jnp.zeros_like(dkc_ref)
            dvc_ref[...] = jnp.zeros_like(dvc_ref)

        @pl.when((i >= 2) & (i <= nb - 2))
        def _():
            dst_row = pl.multiple_of((i - 2) * qt, qt)
            for acc_ref, dst in ((dkacc, dk_hbm), (dvacc, dv_hbm)):
                flush(acc_ref, dst, 0, dst_row, qt)
                acc_ref[0:qt, :] = acc_ref[qt:2 * qt, :]
                acc_ref[qt:2 * qt, :] = acc_ref[2 * qt:3 * qt, :]
                acc_ref[2 * qt:3 * qt, :] = jnp.zeros((qt, NA_W), F32)

        for t, (kr, vr) in enumerate(((k0, v0), (k1, v1), (k2, v2))):
            kbuf[t * qt:(t + 1) * qt, :] = kr[...]
            vbuf[t * qt:(t + 1) * qt, :] = vr[...]
        base = wb0(i) * QROWS

        def row_body(rl, carry):
            r = i * QROWS + rl
            rs = jnp.clip(r - KH // 2, 0, rows - KH)
            vi = rs - r + (KH - 1)
            off = pl.multiple_of((rs - base) * GRID_W, GRID_W)
            qoff = pl.multiple_of(rl * GRID_W, GRID_W)
            for p in range(HEADS // 2):
                ls = slice(p * 128, (p + 1) * 128)
                qst = _stack_heads(q_ref[pl.ds(qoff, GRID_W), ls])
                dost = _stack_heads(do_ref[pl.ds(qoff, GRID_W), ls])
                k2v = kbuf[pl.ds(off, KH * GRID_W), ls]
                v2v = vbuf[pl.ds(off, KH * GRID_W), ls]
                kc2 = kc_ref[:, ls]
                vc2 = vc_ref[:, ls]
                s_w = _dot_nt(qst, k2v) * scale + tab_s[vi, p]
                s_c = _dot_nt(qst, kc2) * scale
                m = jnp.maximum(jnp.max(s_w, axis=-1, keepdims=True), jnp.max(s_c, axis=-1, keepdims=True))
                pw = jnp.exp(s_w - m)
                pc = jnp.exp(s_c - m)
                inv = 1.0 / (jnp.sum(pw, axis=-1, keepdims=True) + jnp.sum(pc, axis=-1, keepdims=True))
                pw = pw * inv
                pc = pc * inv
                dpw = _dot_nt(dost, v2v)
                dpc = _dot_nt(dost, vc2)
                delta = jnp.sum(pw * dpw, axis=-1, keepdims=True) + jnp.sum(pc * dpc, axis=-1, keepdims=True)
                ds_w = pw * (dpw - delta)
                ds_c = pc * (dpc - delta)
                dtab_s[vi, p] += ds_w
                dsw16 = ds_w.astype(BF16)
                dsc16 = ds_c.astype(BF16)
                dq = (_dot(dsw16, k2v) + _dot(dsc16, kc2)) * scale
                dq_ref[pl.ds(qoff, GRID_W), ls] = _unstack_heads(dq).astype(BF16)
                dkacc[pl.ds(off, KH * GRID_W), ls] += _dot_tn(dsw16, qst) * scale
                dvacc[pl.ds(off, KH * GRID_W), ls] += _dot_tn(pw.astype(BF16), dost)
                dkc_ref[:, ls] += _dot_tn(dsc16, qst) * scale
                dvc_ref[:, ls] += _dot_tn(pc.astype(BF16), dost)
            return carry

        lax.fori_loop(0, QROWS, row_body, 0)

        @pl.when(i == nb - 1)
        def _():
            dst_row = pl.multiple_of((nb - 3) * qt, qt)
            flush(dkacc, dk_hbm, 0, dst_row, wt)
            flush(dvacc, dv_hbm, 0, dst_row, wt)
            cp = pltpu.make_async_copy(dtab_s, dtab_hbm, sem)
            cp.start()
            cp.wait()
            if exchange:
                exchange.finish(x_in, x_out, x_sems)

    blk = lambda col: [pl.BlockSpec((qt, NA_W), functools.partial(lambda i, t, c: (wb0(i) + t, c), t=t, c=col))
                       for t in range(3)]
    any_spec = pl.BlockSpec(memory_space=pl.ANY)
    hbm = pl.BlockSpec(memory_space=pltpu.HBM)
    res = pl.pallas_call(
        body, name=name, grid=(nb,),
        in_specs=[pl.BlockSpec((qt, NA_W), lambda i: (i, 0))] + blk(1) + blk(2)
                 + [pl.BlockSpec((nctx, NA_W), lambda i: (0, 1)), pl.BlockSpec((nctx, NA_W), lambda i: (0, 2)),
                    pl.BlockSpec((qt, NA_W), lambda i: (i, 0)), any_spec] + [hbm] * nx,
        out_specs=[pl.BlockSpec((qt, NA_W), lambda i: (i, 0)), any_spec, any_spec,
                   pl.BlockSpec((nctx, NA_W), lambda i: (0, 0)), pl.BlockSpec((nctx, NA_W), lambda i: (0, 0)),
                   any_spec] + [hbm] * nx,
        out_shape=[jax.ShapeDtypeStruct((seq, NA_W), BF16), jax.ShapeDtypeStruct((seq, NA_W), F32),
                   jax.ShapeDtypeStruct((seq, NA_W), F32), jax.ShapeDtypeStruct((nctx, NA_W), F32),
                   jax.ShapeDtypeStruct((nctx, NA_W), F32), jax.ShapeDtypeStruct(TAB_SHAPE, F32)]
                  + (exchange.out_shapes if exchange else []),
        scratch_shapes=[pltpu.VMEM((wt, NA_W), BF16), pltpu.VMEM((wt, NA_W), BF16),
                        pltpu.VMEM((wt, NA_W), F32), pltpu.VMEM((wt, NA_W), F32),
                        pltpu.VMEM(TAB_SHAPE, F32), pltpu.VMEM(TAB_SHAPE, F32), pltpu.SemaphoreType.DMA]
                       + (exchange.scratch if exchange else []),
        compiler_params=_params(("arbitrary",)),
    )(qkvu, qkvu, qkvu, qkvu, qkvu, qkvu, qkvu, qkvu_c, qkvu_c, dmix, tab, *(exchange.arrays if exchange else []))
    return res[:6], list(res[6:])


def _halo_specs(te, seq, col, width):
    per = te // HALO
    last = seq // HALO - 1
    return [pl.BlockSpec((HALO, width), lambda i: (jnp.maximum(i * per - 1, 0), col)),
            pl.BlockSpec((te, width), lambda i: (i, col)),
            pl.BlockSpec((HALO, width), lambda i: (jnp.minimum((i + 1) * per, last), col))]


def _extended(prev_ref, cur_ref, next_ref, i, te, seq):
    xe = jnp.concatenate([prev_ref[...], cur_ref[...], next_ref[...]], axis=0).astype(F32)
    pos = i * te - HALO + lax.broadcasted_iota(jnp.int32, (te + 2 * HALO, 1), 0)
    return jnp.where((pos >= 0) & (pos < seq), xe, 0.0), pos


def _window_sum(x, levels, n, mirrored):
    first = (n - 1) if mirrored else 1
    acc = x + pltpu.roll(x, first, 0)
    step = 1
    for _ in range(levels - 1):
        acc = pltpu.roll(acc, step, 0) + pltpu.roll(acc, n - step, 0)
        step *= 2
    return acc


def _window_count(pos, w, seq):
    lo = jnp.clip(pos - w // 2, 0, seq)
    hi = jnp.clip(pos - w // 2 + w, 0, seq)
    return jnp.maximum(hi - lo, 1).astype(F32)


def pool_fwd(qkvu, pool_w, pool_scale, name):
    seq = qkvu.shape[0]
    te = _tile(seq, (512, 256))
    n = te + 2 * HALO

    def body(up_ref, uc_ref, un_ref, w_ref, sc_ref, o_ref):
        i = pl.program_id(0)
        xe, pos = _extended(up_ref, uc_ref, un_ref, i, te, seq)
        cnt = pos[HALO:HALO + te]
        for g, w in enumerate(POOL_WINDOWS):
            ls = slice(g * POOL_G, (g + 1) * POOL_G)
            xg = xe[:, ls]
            win = _window_sum(xg, g + 1, n, False)[HALO:HALO + te]
            dlt = win / _window_count(cnt, w, seq) - xg[HALO:HALO + te]
            z = _dot(dlt.astype(BF16), w_ref[g])
            o_ref[:, ls] = (z * sc_ref[:, ls]).astype(BF16)

    return pl.pallas_call(
        body, name=name, grid=(seq // te,),
        in_specs=_halo_specs(te, seq, 3, POOL_W)
                 + [pl.BlockSpec((4, POOL_G, POOL_G), lambda i: (0, 0, 0)), pl.BlockSpec((1, POOL_W), lambda i: (0, 0))],
        out_specs=pl.BlockSpec((te, POOL_W), lambda i: (i, 0)),
        out_shape=jax.ShapeDtypeStruct((seq, POOL_W), BF16),
        compiler_params=_params(("parallel",)),
    )(qkvu, qkvu, qkvu, pool_w, pool_scale)


def pool_bwd(qkvu, dmix, pool_w, pool_scale, name):
    seq = qkvu.shape[0]
    te = _tile(seq, (512, 256))
    n = te + 2 * HALO

    def body(up_ref, uc_ref, un_ref, dp_ref, dc_ref, dn_ref, w_ref, sc_ref, du_ref, dw_ref, dsc_ref):
        i = pl.program_id(0)

        @pl.when(i == 0)
        def _():
            dw_ref[...] = jnp.zeros_like(dw_ref)
            dsc_ref[...] = jnp.zeros_like(dsc_ref)

        xe, pos = _extended(up_ref, uc_ref, un_ref, i, te, seq)
        de, _ = _extended(dp_ref, dc_ref, dn_ref, i, te, seq)
        cpos = pos[HALO:HALO + te]
        for g, w in enumerate(POOL_WINDOWS):
            ls = slice(g * POOL_G, (g + 1) * POOL_G)
            xg = xe[:, ls]
            wg = w_ref[g]
            win = _window_sum(xg, g + 1, n, False)[HALO:HALO + te]
            dlt = (win / _window_count(cpos, w, seq) - xg[HALO:HALO + te]).astype(BF16)
            z = _dot(dlt, wg)
            dpg = de[:, ls]
            dsc_ref[:, ls] += jnp.sum(dpg[HALO:HALO + te] * z, axis=0, keepdims=True)
            dz = (dpg * sc_ref[:, ls]).astype(BF16)
            dw_ref[g] += _dot_tn(dlt, dz[HALO:HALO + te])
            dd = _dot_nt(dz, wg)
            back = _window_sum(dd / _window_count(pos, w, seq), g + 1, n, True)
            du_ref[:, ls] = (back[HALO:HALO + te] - dd[HALO:HALO + te]).astype(BF16)

    return pl.pallas_call(
        body, name=name, grid=(seq // te,),
        in_specs=_halo_specs(te, seq, 3, POOL_W) + _halo_specs(te, seq, 1, POOL_W)
                 + [pl.BlockSpec((4, POOL_G, POOL_G), lambda i: (0, 0, 0)), pl.BlockSpec((1, POOL_W), lambda i: (0, 0))],
        out_specs=[pl.BlockSpec((te, POOL_W), lambda i: (i, 0)),
                   pl.BlockSpec((4, POOL_G, POOL_G), lambda i: (0, 0, 0)), pl.BlockSpec((1, POOL_W), lambda i: (0, 0))],
        out_shape=[jax.ShapeDtypeStruct((seq, POOL_W), BF16), jax.ShapeDtypeStruct((4, POOL_G, POOL_G), F32),
                   jax.ShapeDtypeStruct((1, POOL_W), F32)],
        compiler_params=_params(("arbitrary",)),
    )(qkvu, qkvu, qkvu, dmix, dmix, dmix, pool_w, pool_scale)


def _shifted(z, zprev_row, znext_row, te):
    rows = lax.broadcasted_iota(jnp.int32, (te, 1), 0)
    zp = jnp.where(rows == 0, zprev_row, pltpu.roll(z, 1, 0))
    zn = jnp.where(rows == te - 1, znext_row, pltpu.roll(z, te - 1, 0))
    return zp, zn


def _edge_rows(prev_ref, next_ref, i, nt):
    p = prev_ref[HALO - 1:HALO, :].astype(F32)
    q = next_ref[0:1, :].astype(F32)
    return jnp.where(i == 0, 0.0, p), jnp.where(i == nt - 1, 0.0, q)


def conv_fwd(proj, conv_w, name):
    seq = proj.shape[0]
    te = _tile(seq, (512, 256))
    nt = seq // te

    def body(bg_ref, cp_ref, cc_ref, cn_ref, xp_ref, xc_ref, xn_ref, w_ref, o_ref):
        i = pl.program_id(0)
        z = cc_ref[...].astype(F32) * xc_ref[...].astype(F32)
        cpr, cnr = _edge_rows(cp_ref, cn_ref, i, nt)
        xpr, xnr = _edge_rows(xp_ref, xn_ref, i, nt)
        zp, zn = _shifted(z, cpr * xpr, cnr * xnr, te)
        y = zp * w_ref[0:1, :] + z * w_ref[1:2, :] + zn * w_ref[2:3, :]
        o_ref[...] = (bg_ref[...].astype(F32) * y).astype(BF16)

    return pl.pallas_call(
        body, name=name, grid=(nt,),
        in_specs=[pl.BlockSpec((te, D), lambda i: (i, 0))] + _halo_specs(te, seq, 1, D) + _halo_specs(te, seq, 2, D)
                 + [pl.BlockSpec((3, D), lambda i: (0, 0))],
        out_specs=pl.BlockSpec((te, D), lambda i: (i, 0)),
        out_shape=jax.ShapeDtypeStruct((seq, D), BF16),
        compiler_params=_params(("parallel",)),
    )(proj, proj, proj, proj, proj, proj, proj, conv_w)


def conv_bwd(proj, dgm, conv_w, name):
    seq = proj.shape[0]
    te = _tile(seq, (512, 256))
    nt = seq // te

    def body(bp_ref, bc_ref, bn_ref, cp_ref, cc_ref, cn_ref, xp_ref, xc_ref, xn_ref, gp_ref, gc_ref, gn_ref, w_ref,
             dbg_ref, dcg_ref, dxin_ref, dw_ref):
        i = pl.program_id(0)

        @pl.when(i == 0)
        def _():
            dw_ref[...] = jnp.zeros_like(dw_ref)

        bg = bc_ref[...].astype(F32)
        cg = cc_ref[...].astype(F32)
        xin = xc_ref[...].astype(F32)
        dg = gc_ref[...].astype(F32)
        z = cg * xin
        cpr, cnr = _edge_rows(cp_ref, cn_ref, i, nt)
        xpr, xnr = _edge_rows(xp_ref, xn_ref, i, nt)
        zp, zn = _shifted(z, cpr * xpr, cnr * xnr, te)
        w0, w1, w2 = w_ref[0:1, :], w_ref[1:2, :], w_ref[2:3, :]
        y = zp * w0 + z * w1 + zn * w2
        dbg_ref[...] = (dg * y).astype(BF16)
        dy = dg * bg
        dw_ref[0:1, :] += jnp.sum(dy * zp, axis=0, keepdims=True)
        dw_ref[1:2, :] += jnp.sum(dy * z, axis=0, keepdims=True)
        dw_ref[2:3, :] += jnp.sum(dy * zn, axis=0, keepdims=True)
        bpr, bnr = _edge_rows(bp_ref, bn_ref, i, nt)
        gpr, gnr = _edge_rows(gp_ref, gn_ref, i, nt)
        dyp, dyn = _shifted(dy, bpr * gpr, bnr * gnr, te)
        dz = dyn * w0 + dy * w1 + dyp * w2
        dcg_ref[...] = (dz * xin).astype(BF16)
        dxin_ref[...] = (dz * cg).astype(BF16)

    row = pl.BlockSpec((te, D), lambda i: (i, 0))
    return pl.pallas_call(
        body, name=name, grid=(nt,),
        in_specs=_halo_specs(te, seq, 0, D) + _halo_specs(te, seq, 1, D) + _halo_specs(te, seq, 2, D)
                 + _halo_specs(te, seq, 0, D) + [pl.BlockSpec((3, D), lambda i: (0, 0))],
        out_specs=[row, row, row, pl.BlockSpec((3, D), lambda i: (0, 0))],
        out_shape=[jax.ShapeDtypeStruct((seq, D), BF16)] * 3 + [jax.ShapeDtypeStruct((3, D), F32)],
        compiler_params=_params(("arbitrary",)),
    )(proj, proj, proj, proj, proj, proj, proj, proj, proj, dgm, dgm, dgm, conv_w)


def _position():
    x, y, c = lax.axis_index("x"), lax.axis_index("y"), lax.axis_index("c")
    return x, y, c, 4 * x + 2 * y + c


def _peer(x, y, c, j):
    px = 1 - x if j & 4 else x
    py = 1 - y if j & 2 else y
    pc = 1 - c if j & 1 else c
    return (px, py, pc), 4 * px + 2 * py + pc


def small_allgather(v, name):
    rows, cols = v.shape

    def body(v_ref, o_ref, send_sems, recv_sems, local_sem):
        x, y, c, me = _position()
        mine = pltpu.make_async_copy(v_ref, o_ref.at[me], local_sem)
        mine.start()
        sends = []
        for j in range(1, N_DEV):
            peer, _ = _peer(x, y, c, j)
            cp = pltpu.make_async_remote_copy(src_ref=v_ref, dst_ref=o_ref.at[me], send_sem=send_sems.at[j - 1],
                                              recv_sem=recv_sems.at[j - 1], device_id=peer, device_id_type=MESH_ID)
            cp.start()
            sends.append(cp)
        for j in range(1, N_DEV):
            peer, pid = _peer(x, y, c, j)
            pltpu.make_async_remote_copy(src_ref=v_ref, dst_ref=o_ref.at[pid], send_sem=send_sems.at[j - 1],
                                         recv_sem=recv_sems.at[j - 1], device_id=peer,
                                         device_id_type=MESH_ID).wait_recv()
        for cp in sends:
            cp.wait_send()
        mine.wait()

    return pl.pallas_call(
        body, name=name,
        out_shape=jax.ShapeDtypeStruct((N_DEV, rows, cols), v.dtype),
        in_specs=[pl.BlockSpec(memory_space=pltpu.VMEM)],
        out_specs=pl.BlockSpec(memory_space=pltpu.VMEM),
        scratch_shapes=[pltpu.SemaphoreType.DMA((N_DEV - 1,)), pltpu.SemaphoreType.DMA((N_DEV - 1,)),
                        pltpu.SemaphoreType.DMA],
        compiler_params=pltpu.CompilerParams(vmem_limit_bytes=VMEM_LIMIT),
    )(v)


class Exchange:
    def __init__(self, kind, arrays):
        self.kind, self.arrays, self.n = kind, list(arrays), len(arrays)
        n = self.n
        if kind == "gather":
            self.out_shapes = [jax.ShapeDtypeStruct((N_DEV,) + a.shape, a.dtype) for a in self.arrays]
        else:
            self.out_shapes = [jax.ShapeDtypeStruct(a.shape, a.dtype) for a in self.arrays]
        self.scratch = [pltpu.SemaphoreType.DMA((7 * n,)), pltpu.SemaphoreType.DMA((7 * n,)),
                        pltpu.SemaphoreType.DMA((n,))]

    def _gather_copies(self, ins, outs, sems):
        send_sems, recv_sems, local_sems = sems
        x, y, c, me = _position()
        chips = [(1 - x, y), (x, 1 - y), (1 - x, 1 - y)]

        def blk(k, px, py, pc):
            return outs[k].at[4 * px + 2 * py + pc]

        def copy(k, slot, block, to, src=None):
            return pltpu.make_async_remote_copy(
                src_ref=blk(k, *block) if src is None else src, dst_ref=blk(k, *block),
                send_sem=send_sems.at[k * 7 + slot], recv_sem=recv_sems.at[k * 7 + slot],
                device_id=to, device_id_type=MESH_ID)

        mine = [pltpu.make_async_copy(ins[k], blk(k, x, y, c), local_sems.at[k]) for k in range(self.n)]
        first = []
        for k in range(self.n):
            first.append(copy(k, 0, (x, y, c), (x, y, 1 - c), src=ins[k]))
            first += [copy(k, 1 + j, (x, y, c), (*chip, c), src=ins[k]) for j, chip in enumerate(chips)]
        return (x, y, c), chips, copy, mine, first

    def start(self, ins, outs, sems):
        if self.kind == "gather":
            _, _, _, mine, first = self._gather_copies(ins, outs, sems)
            for cp in mine + first:
                cp.start()
        else:
            for cp in self._scatter_copies(ins, outs, sems, False):
                cp.start()

    def finish(self, ins, outs, sems):
        if self.kind == "gather":
            (x, y, c), chips, copy, mine, first = self._gather_copies(ins, outs, sems)
            passed = []
            for j, chip in enumerate(chips):
                for k in range(self.n):
                    copy(k, 1 + j, (*chip, c), (x, y, c)).wait_recv()
                    cp = copy(k, 4 + j, (*chip, c), (x, y, 1 - c))
                    cp.start()
                    passed.append(cp)
            for k in range(self.n):
                copy(k, 0, (x, y, 1 - c), (x, y, c)).wait_recv()
                for j, chip in enumerate(chips):
                    copy(k, 4 + j, (*chip, 1 - c), (x, y, c)).wait_recv()
            for cp in first + passed:
                cp.wait_send()
            for cp in mine:
                cp.wait()
        else:
            for cp in self._scatter_copies(ins, outs, sems, True):
                cp.wait_recv()
            copies = self._scatter_copies(ins, outs, sems, False)
            for cp in copies[self.n:]:
                cp.wait_send()
            for cp in copies[:self.n]:
                cp.wait()

    def _scatter_copies(self, ins, outs, sems, arrivals):
        send_sems, recv_sems, local_sems = sems
        x, y, c, me = _position()
        out = []
        if not arrivals:
            out = [pltpu.make_async_copy(ins[k].at[me], outs[k].at[me], local_sems.at[k]) for k in range(self.n)]
        for j in range(1, N_DEV):
            peer, pid = _peer(x, y, c, j)
            for k in range(self.n):
                out.append(pltpu.make_async_remote_copy(
                    src_ref=ins[k].at[pid], dst_ref=outs[k].at[pid if arrivals else me],
                    send_sem=send_sems.at[k * 7 + j - 1], recv_sem=recv_sems.at[k * 7 + j - 1],
                    device_id=peer, device_id_type=MESH_ID))
        return out

    def run(self, name):
        n = self.n

        def body(*refs):
            ins, outs, sems = refs[:n], refs[n:2 * n], refs[2 * n:]
            self.start(ins, outs, sems)
            self.finish(ins, outs, sems)

        hbm = pl.BlockSpec(memory_space=pltpu.HBM)
        return list(pl.pallas_call(
            body, name=name, out_shape=self.out_shapes, in_specs=[hbm] * n, out_specs=[hbm] * n,
            scratch_shapes=self.scratch,
        )(*self.arrays))


def sum_devices(v, name):
    _, rows, cols = v.shape

    def body(v_ref, o_ref):
        acc = v_ref[0]
        for p in range(1, N_DEV):
            acc = acc + v_ref[p]
        o_ref[...] = acc

    return pl.pallas_call(
        body, name=name, out_shape=jax.ShapeDtypeStruct((rows, cols), F32),
        compiler_params=pltpu.CompilerParams(vmem_limit_bytes=VMEM_LIMIT),
    )(v)


def _silu(x):
    return x * _sigmoid(x)


def adaln_fwd(cm, mod_w, mod_b_cols, name):
    cols = mod_w.shape[2]

    def body(c_ref, w_ref, b_ref, o_ref):
        o_ref[0] = jnp.dot(_silu(c_ref[...]), w_ref[0], precision=HI, preferred_element_type=F32) + b_ref[0]

    return pl.pallas_call(
        body, name=name, grid=(2,),
        in_specs=[pl.BlockSpec((16, D), lambda l: (0, 0)), pl.BlockSpec((1, D, cols), lambda l: (l, 0, 0)),
                  pl.BlockSpec((1, 1, cols), lambda l: (l, 0, 0))],
        out_specs=pl.BlockSpec((1, 16, cols), lambda l: (l, 0, 0)),
        out_shape=jax.ShapeDtypeStruct((2, 16, cols), F32),
        compiler_params=_params(("parallel",)),
    )(cm, mod_w, mod_b_cols)


def adaln_bwd(cm_t, mod_w, dm_t, name):
    cols = mod_w.shape[2]

    def body(c_ref, w_ref, lat_ref, ctx_ref, gw_ref, pc_ref):
        ctot = jnp.sum(ctx_ref[0], axis=0, keepdims=True)
        rows = lax.broadcasted_iota(jnp.int32, (8, 1), 0)
        g_hi = jnp.where(rows == 0, ctot, 0.0)
        g = jnp.concatenate([lat_ref[0], g_hi], axis=0)
        gw_ref[0] = jnp.dot(_silu(c_ref[...]), g, precision=HI, preferred_element_type=F32)
        pc_ref[0] = lax.dot_general(g_hi, w_ref[0], NT_DIMS, precision=HI, preferred_element_type=F32)

    return pl.pallas_call(
        body, name=name, grid=(2,),
        in_specs=[pl.BlockSpec((D, 16), lambda l: (0, 0)), pl.BlockSpec((1, D, cols), lambda l: (l, 0, 0)),
                  pl.BlockSpec((1, 8, cols), lambda l: (l, 0, 0)), pl.BlockSpec((1, 8, cols), lambda l: (l + 2, 0, 0))],
        out_specs=[pl.BlockSpec((1, D, cols), lambda l: (l, 0, 0)), pl.BlockSpec((1, 8, D), lambda l: (l, 0, 0))],
        out_shape=[jax.ShapeDtypeStruct((2, D, cols), F32), jax.ShapeDtypeStruct((2, 8, D), F32)],
        compiler_params=_params(("parallel",)),
    )(cm_t, mod_w, dm_t, dm_t)


def mod_b_grad(dm_t, name):
    width = dm_t.shape[2]
    tn = width // 8

    def body(d_ref, o_ref):
        s = jnp.concatenate([jnp.sum(d_ref[k], axis=0, keepdims=True) for k in range(4)]
                            + [jnp.zeros((4, tn), F32)], axis=0)
        o_ref[...] = s + pltpu.roll(s, 6, 0)

    return pl.pallas_call(
        body, name=name, grid=(8,),
        in_specs=[pl.BlockSpec((4, 8, tn), lambda j: (0, 0, j))],
        out_specs=pl.BlockSpec((8, tn), lambda j: (0, j)),
        out_shape=jax.ShapeDtypeStruct((8, width), F32),
        compiler_params=_params(("parallel",)),
    )(dm_t)


def adamw(w, m, v, name, g=None, recv=None):
    rows, cols = w.shape
    tr = _tile(rows, (256, 128, 64, 32, 16, 8))
    summed = recv is not None

    def body(w_ref, m_ref, v_ref, g_ref, go_ref, d_ref, mo_ref, vo_ref):
        if summed:
            gv = g_ref[0].astype(F32)
            for p in range(1, N_DEV):
                gv = gv + g_ref[p].astype(F32)
        else:
            gv = g_ref[...]
        mn = ADAM_B1 * m_ref[...] + (1.0 - ADAM_B1) * gv
        vn = ADAM_B2 * v_ref[...] + (1.0 - ADAM_B2) * (gv * gv)
        m_hat = mn / (1.0 - ADAM_B1 ** ADAM_STEP)
        v_hat = vn / (1.0 - ADAM_B2 ** ADAM_STEP)
        go_ref[...] = gv
        d_ref[...] = -ADAM_LR * (m_hat / (jnp.sqrt(v_hat) + ADAM_EPS) + ADAM_WD * w_ref[...])
        mo_ref[...] = mn
        vo_ref[...] = vn

    row = pl.BlockSpec((tr, cols), lambda i: (i, 0))
    gspec = pl.BlockSpec((N_DEV, tr, cols), lambda i: (0, i, 0)) if summed else row
    return pl.pallas_call(
        body, name=name, grid=(rows // tr,),
        in_specs=[row, row, row, gspec], out_specs=[row] * 4,
        out_shape=[jax.ShapeDtypeStruct((rows, cols), F32)] * 4,
        compiler_params=_params(("parallel",)),
    )(w, m, v, recv if summed else g)


def _ffn_fwd(h, hn, mods, w13, w2, base, tag, nxt=None, exchange=None):
    (p, u, s), exchanged = ffn_up(hn, w13, tag + "_up", exchange)
    if callable(w2):
        w2 = w2(exchanged)
    outs = mm_nn([s], w2, [0], tag + "_down", res=(h, mods, base + 2, 0.5), nxt=nxt)
    h_new, y = outs[0], outs[1]
    return h_new, (outs[2] if nxt else None), (h, hn, p, u, s, y), exchanged


COLUMN_CUT = ("w13", "ewi", "cwi")
GATHER_FIRST = ("w13_00",)
GATHER_IN_FFN = ("w2_00", "ewi", "ewo", "w13_01", "w2_01")
GATHER_IN_ATTN = ("w13_10", "w2_10", "cwi", "cwo", "w13_11", "w2_11")
SCATTER_IN_ATTN = ("w13_11", "w2_11", "cwi", "cwo", "w13_10", "w2_10", "w13_01", "w2_01", "ewo")
SCATTER_LAST = ("w13_00", "w2_00", "ewi")


def unpack_piece(p, g):
    if p.split("_")[0] in COLUMN_CUT:
        return g.transpose(1, 0, 2).reshape(g.shape[1], -1)
    return g.reshape(-1, g.shape[2])


def block_piece(p, full):
    if p.split("_")[0] in COLUMN_CUT:
        return full.reshape(full.shape[0], N_DEV, -1).transpose(1, 0, 2).astype(BF16)
    return full.reshape(N_DEV, -1, full.shape[1]).astype(BF16)


def _ffn_bwd(dy, dres, saved, mods, g, w13, w2, base, tag, prev=None, acc=None, exchange_of=None):
    h, hn, p, u, s, _ = saved
    ff = w2.shape[0]
    acc = acc or (None, None, None)
    da, db = mm_nt(dy, w2, tag + "_ds", dswiglu=(u, p))
    dw2 = mm_tn(s, dy, tag + "_dw2", acc=acc[2])
    dwa = mm_tn(hn, da, tag + "_dw13a", acc=acc[0])
    dwb = mm_tn(hn, db, tag + "_dw13b", acc=acc[1])
    exchange = exchange_of(dwa, dwb, dw2) if exchange_of else None
    outs = mm_nt_norm([da, db], w13, [0, ff], h, g, mods, base + 1, dres, tag + "_dhn", prev=prev, exchange=exchange)
    dh, dshift, dscale, dg = outs[:4]
    nprev = 2 if prev else 0
    return (dh, (dwa, dwb, dw2), dg, {base: dshift, base + 1: dscale}, tuple(outs[4:4 + nprev]),
            list(outs[4 + nprev:]))


def _mod_rows(parts):
    zero = jnp.zeros((1, D), F32)
    return jnp.concatenate([parts.get(k, zero) for k in range(N_MOD)], axis=0)


def local_step(x, ctx, ml, mc, wts, target, shards=None):
    wts = dict(wts)
    ng = wts["norm_g"]
    gvec = lambda l, k: ng[l, k][None, :]
    pool_w16 = wts["pool_w"].astype(BF16)
    grads = {}

    def gather(pieces):
        return Exchange("gather", [shards[p] for p in pieces]) if shards else None

    def arrived(pieces, results):
        for p, g in zip(pieces, results):
            wts[p] = unpack_piece(p, g)

    def ffn_grads(lf, f):
        grads["w13_" + lf] = jnp.concatenate([f[0], f[1]], axis=1)
        grads["w2_" + lf] = f[2]

    xh = normmod(x, gvec(0, 0), ml[0], 0, 1, "l0f1_norm")
    ch = normmod(ctx, gvec(0, 0), mc[0], 0, 1, "l0f1c_norm")
    def w2_after_up(got):
        arrived(GATHER_IN_FFN, got)
        return wts["w2_00"]

    x1, xn, sv1, _ = _ffn_fwd(x, xh, ml[0], wts["w13_00"], w2_after_up, 0, "l0f1",
                              nxt=(gvec(0, 1), ml[0], 3, 4), exchange=gather(GATHER_IN_FFN))
    c1, cn, sv1c, _ = _ffn_fwd(ctx, ch, mc[0], wts["w13_00"], wts["w2_00"], 0, "l0f1c", nxt=(gvec(0, 1), mc[0], 3, 4))
    qkvu = mm_nn([xn], wts["ewi"], [0], "l0mix_in")
    qkvu_c = mm_nn([cn], wts["ewi"], [0], "l0mix_in_c")
    tab = bias_table(wts["rpb"])
    att, got = attn_fwd(qkvu, qkvu_c, tab, "l0_attn", gather(GATHER_IN_ATTN))
    arrived(GATHER_IN_ATTN, got)
    pool = pool_fwd(qkvu, pool_w16, wts["pool_scale"], "l0_pool")
    x2, ymix, xh = mm_nn([att, pool], wts["ewo"], [0, NA_W], "l0mix_out", res=(x1, ml[0], 5, 1.0),
                         nxt=(gvec(0, 2), ml[0], 6, 7))
    x3, xh, sv2, _ = _ffn_fwd(x2, xh, ml[0], wts["w13_01"], wts["w2_01"], 6, "l0f2", nxt=(gvec(1, 0), ml[1], 0, 1))

    x4, xn1, sv3, _ = _ffn_fwd(x3, xh, ml[1], wts["w13_10"], wts["w2_10"], 0, "l1f1", nxt=(gvec(1, 1), ml[1], 3, 4))
    proj = mm_nn([xn1], wts["cwi"], [0], "l1mix_in")
    gm = conv_fwd(proj, wts["conv_w"], "l1_conv")
    x5, ycv, xh = mm_nn([gm], wts["cwo"], [0], "l1mix_out", res=(x4, ml[1], 5, 1.0), nxt=(gvec(1, 2), ml[1], 6, 7))
    x6, _, sv4, _ = _ffn_fwd(x5, xh, ml[1], wts["w13_11"], wts["w2_11"], 6, "l1f2")

    dx6, loss, dgf, dy, dgate = loss_head(x6, wts["final_g"][None, :], target, (sv4[5], ml[1], 8, 0.5), "loss_head")
    dm1 = {8: dgate}
    dx5, dwf4, dg12, dm_f4, (dy, dgate), _ = _ffn_bwd(dy, dx6, sv4, ml[1], gvec(1, 2), wts["w13_11"], wts["w2_11"], 6,
                                                      "l1f2", prev=(ycv, ml[1], 5, 1.0))
    ffn_grads("11", dwf4)
    dm1.update({5: dgate, **dm_f4})
    dgm = mm_nt(dy, wts["cwo"], "l1mix_dgm")
    grads["cwo"] = mm_tn(gm, dy, "l1mix_dwo")
    dbg, dcg, dxin, dconv_w = conv_bwd(proj, dgm, wts["conv_w"], "l1_conv_bwd")
    grads["cwi"] = jnp.concatenate([mm_tn(xn1, t, "l1mix_dwi%d" % k) for k, t in enumerate((dbg, dcg, dxin))], axis=1)
    dx4, dsh, dsc, dg11, dy, dgate = mm_nt_norm([dbg, dcg, dxin], wts["cwi"], [0, D, 2 * D], x4, gvec(1, 1), ml[1], 4,
                                                dx5, "l1mix_dxn", prev=(sv3[5], ml[1], 2, 0.5))
    dm1.update({3: dsh, 4: dsc, 2: dgate})
    dx3, dwf3, dg10, dm_f3, (dy, dgate), _ = _ffn_bwd(dy, dx4, sv3, ml[1], gvec(1, 0), wts["w13_10"], wts["w2_10"], 0,
                                                      "l1f1", prev=(sv2[5], ml[0], 8, 0.5))
    ffn_grads("10", dwf3)
    dm1.update(dm_f3)
    dm0 = {8: dgate}

    dx2, dwf2, dg02, dm_f2, (dy, dgate), _ = _ffn_bwd(dy, dx3, sv2, ml[0], gvec(0, 2), wts["w13_01"], wts["w2_01"], 6,
                                                      "l0f2", prev=(ymix, ml[0], 5, 1.0))
    ffn_grads("01", dwf2)
    dm0.update({5: dgate, **dm_f2})
    dmix = mm_nt(dy, wts["ewo"], "l0mix_dmix")
    grads["ewo"] = jnp.concatenate([mm_tn(att, dy, "l0mix_dwo_att"), mm_tn(pool, dy, "l0mix_dwo_pool")], axis=0)
    scatter = Exchange("scatter", [block_piece(p, grads.pop(p)) for p in SCATTER_IN_ATTN]) if shards else None
    (dq, dk, dv, dkc, dvc, dtab), got = attn_bwd(qkvu, qkvu_c, tab, dmix, "l0_attn_bwd", scatter)
    recv = dict(zip(SCATTER_IN_ATTN, got))
    du, dpool_w, dpool_scale = pool_bwd(qkvu, dmix, pool_w16, wts["pool_scale"], "l0_pool_bwd")
    drpb = bias_table_bwd(dtab)
    dk16, dv16, dkc16, dvc16 = (t.astype(BF16) for t in (dk, dv, dkc, dvc))
    grads["ewi"] = jnp.concatenate([
        mm_tn(xn, dq, "l0mix_dwi_q"),
        mm_tn(cn, dkc16, "l0mix_dwi_kc", acc=mm_tn(xn, dk16, "l0mix_dwi_k")),
        mm_tn(cn, dvc16, "l0mix_dwi_vc", acc=mm_tn(xn, dv16, "l0mix_dwi_v")),
        mm_tn(xn, du, "l0mix_dwi_u")], axis=1)
    dx1, dsh, dsc, dg01, dy, dgate = mm_nt_norm([dq, dk16, dv16, du], wts["ewi"], [0, NA_W, 2 * NA_W, 3 * NA_W], x1,
                                                gvec(0, 1), ml[0], 4, dx2, "l0mix_dxn", prev=(sv1[5], ml[0], 2, 0.5))
    dm0.update({3: dsh, 4: dsc, 2: dgate})
    dc1, dsh_c, dsc_c, dg01c, dy_c, dgate_c = mm_nt_norm([dkc16, dvc16], wts["ewi"], [NA_W, 2 * NA_W], c1, gvec(0, 1),
                                                         mc[0], 4, None, "l0mix_dxn_c", prev=(sv1c[5], mc[0], 2, 0.5))
    _, dwf1c, dg00c, dm_f1c, _, _ = _ffn_bwd(dy_c, dc1, sv1c, mc[0], gvec(0, 0), wts["w13_00"], wts["w2_00"], 0, "l0f1c")
    dmc0 = {3: dsh_c, 4: dsc_c, 2: dgate_c, **dm_f1c}

    def last_scatter(dwa, dwb, dw2):
        ffn_grads("00", (dwa, dwb, dw2))
        return Exchange("scatter", [block_piece(p, grads.pop(p)) for p in SCATTER_LAST]) if shards else None

    dx0, _, dg00, dm_f1, _, got = _ffn_bwd(dy, dx1, sv1, ml[0], gvec(0, 0), wts["w13_00"], wts["w2_00"], 0, "l0f1",
                                           acc=dwf1c, exchange_of=last_scatter)
    recv.update(zip(SCATTER_LAST, got))
    dm0.update(dm_f1)

    return {
        "loss": loss, "grad_x": dx0,
        "dml": jnp.stack([_mod_rows(dm0), _mod_rows(dm1)]),
        "dmc": jnp.stack([_mod_rows(dmc0), jnp.zeros((N_MOD, D), F32)]),
        "norm_g": jnp.concatenate([dg00 + dg00c, dg01 + dg01c, dg02, dg10, dg11, dg12], axis=0),
        "grads": grads, "recv": recv,
        "rpb": drpb, "pool_w": dpool_w, "pool_scale": dpool_scale, "conv_w": dconv_w, "final_g": dgf,
    }


def _rows_of(v, nrows):
    flat = v.reshape(-1)
    return jnp.pad(flat, (0, nrows * D - flat.shape[0])).reshape(nrows, D)


def kernel(x, c, ctx, c_ctx, mod_w, mod_b, norm_g, ffn_w13, ffn_w2, even_w_in, even_w_out, na_rpb, pool_w, pool_scale, conv_w_in, conv_w, conv_w_out, final_g, loss_target, m_c_ctx, m_mod_w, m_mod_b, m_norm_g, m_ffn_w13, m_ffn_w2, m_even_w_in, m_even_w_out, m_na_rpb, m_pool_w, m_pool_scale, m_conv_w_in, m_conv_w, m_conv_w_out, m_final_g, v_c_ctx, v_mod_w, v_mod_b, v_norm_g, v_ffn_w13, v_ffn_w2, v_even_w_in, v_even_w_out, v_na_rpb, v_pool_w, v_pool_scale, v_conv_w_in, v_conv_w, v_conv_w_out, v_final_g):
    me = 4 * lax.axis_index("x") + 2 * lax.axis_index("y") + lax.axis_index("c")
    ff = ffn_w2.shape[2] * N_DEV
    w13c = ffn_w13.shape[3]
    w2r = ffn_w2.shape[2]
    mcols = mod_w.shape[2]
    gcols = norm_g.shape[2]

    big = {"w13": ffn_w13.reshape(4 * D, w13c), "w2": ffn_w2.reshape(4 * w2r, D), "ewi": even_w_in[0],
           "ewo": even_w_out[0], "cwi": conv_w_in[0], "cwo": conv_w_out[0]}
    names = list(big)
    shards = {"ewi": even_w_in[0].astype(BF16), "ewo": even_w_out[0].astype(BF16),
              "cwi": conv_w_in[0].astype(BF16), "cwo": conv_w_out[0].astype(BF16)}
    for l in range(2):
        for f in range(2):
            shards["w13_%d%d" % (l, f)] = ffn_w13[l, f].astype(BF16)
            shards["w2_%d%d" % (l, f)] = ffn_w2[l, f].astype(BF16)
    first = Exchange("gather", [shards[p] for p in GATHER_FIRST]).run("weights_allgather_first")
    wts = {p: unpack_piece(p, g) for p, g in zip(GATHER_FIRST, first)}

    c_all = small_allgather(jnp.pad(c, ((0, 7), (0, 0))), "cond_allgather")[:, 0, :]
    cm = jnp.concatenate([c_all, c_ctx[None, :], jnp.zeros((7, D), F32)], axis=0)
    mod_b_cols = lax.dynamic_slice(mod_b, (0, me * mcols), (2, mcols))[:, None, :]
    m_cols = adaln_fwd(cm, mod_w, mod_b_cols, "adaln_fwd")
    m_all = small_allgather(m_cols.reshape(32, mcols), "mod_allgather")
    m_full = m_all.reshape(N_DEV, 2, 16, mcols).transpose(1, 2, 0, 3).reshape(2, 16, N_MOD * D)
    ml = lax.dynamic_slice(m_full, (0, me, 0), (2, 1, N_MOD * D)).reshape(2, N_MOD, D)
    mc = m_full[:, 8].reshape(2, N_MOD, D)

    full_norm_g = small_allgather(_rows_of(norm_g, 8), "norm_g_allgather")[:, 0, :2 * 3 * gcols]
    full_norm_g = full_norm_g.reshape(N_DEV, 2, 3, gcols).transpose(1, 2, 0, 3).reshape(2, 3, D)
    full_conv_w = small_allgather(_rows_of(conv_w, 8), "conv_w_allgather")[:, 0, :3 * gcols]
    full_conv_w = full_conv_w.reshape(N_DEV, 3, gcols).transpose(1, 0, 2).reshape(3, D)
    wts.update(norm_g=full_norm_g, conv_w=full_conv_w, rpb=na_rpb[0], pool_w=pool_w[0], pool_scale=pool_scale,
               final_g=final_g)
    out = local_step(x[0], ctx[0], ml, mc, wts, loss_target[0], shards)

    dm_pack = jnp.concatenate([out["dml"].reshape(2, N_MOD * D), out["dmc"].reshape(2, N_MOD * D),
                               jnp.zeros((4, N_MOD * D), F32)], axis=0)
    dm_t = small_allgather(dm_pack, "dmod_allgather").transpose(1, 0, 2)[:4]
    dm_cols = lax.dynamic_slice(dm_t, (0, 0, me * mcols), (4, N_DEV, mcols))
    g_mod_w, pc = adaln_bwd(cm.T, mod_w, dm_cols, "adaln_bwd")
    g_mod_b = mod_b_grad(dm_t, "mod_b_grad")[:2]

    pack = jnp.concatenate([_rows_of(t, 8) for t in (
        out["norm_g"], out["conv_w"], out["final_g"], pc[0, :1] + pc[1, :1], out["pool_scale"], out["loss"],
        out["rpb"])] + [_rows_of(out["pool_w"], 64)], axis=0)
    small = sum_devices(small_allgather(pack, "small_grads_allgather"), "small_grads_sum")
    g_norm_g = lax.dynamic_slice(small[0:6].reshape(2, 3, D), (0, 0, me * gcols), (2, 3, gcols))
    g_conv_w = lax.dynamic_slice(small[8:11], (0, me * gcols), (3, gcols))[None]
    g_final_g = small[16]
    sg = _sigmoid(c_ctx)
    g_c_ctx = small[24] * (sg * (1.0 + c_ctx * (1.0 - sg)))
    g_pool_scale = small[32:33, :POOL_W]
    loss = small[40, 0]
    g_rpb = small[48:52].reshape(-1)[:na_rpb.size].reshape(na_rpb.shape)
    g_pool_w = small[56:120].reshape(pool_w.shape)

    pieces = out["recv"]
    lf = ("00", "01", "10", "11")
    recv = {"w13": jnp.concatenate([pieces["w13_" + t] for t in lf], axis=1),
            "w2": jnp.concatenate([pieces["w2_" + t] for t in lf], axis=1),
            "ewi": pieces["ewi"], "ewo": pieces["ewo"], "cwi": pieces["cwi"], "cwo": pieces["cwo"]}

    moments = {"w13": (m_ffn_w13, v_ffn_w13), "w2": (m_ffn_w2, v_ffn_w2), "ewi": (m_even_w_in, v_even_w_in),
               "ewo": (m_even_w_out, v_even_w_out), "cwi": (m_conv_w_in, v_conv_w_in),
               "cwo": (m_conv_w_out, v_conv_w_out)}
    orig = {"w13": ffn_w13, "w2": ffn_w2, "ewi": even_w_in, "ewo": even_w_out, "cwi": conv_w_in, "cwo": conv_w_out}
    upd = {}
    for k in names:
        shp2 = big[k].shape
        res = adamw(big[k], moments[k][0].reshape(shp2), moments[k][1].reshape(shp2), "adamw_" + k, recv=recv[k])
        upd[k] = [r.reshape(orig[k].shape) for r in res]
    shp2 = (2 * D, mcols)
    upd["mod_w"] = [r.reshape(mod_w.shape) for r in adamw(mod_w.reshape(shp2), m_mod_w.reshape(shp2),
                                                          v_mod_w.reshape(shp2), "adamw_mod_w",
                                                          g=g_mod_w.reshape(shp2))]

    smalls = [("c_ctx", c_ctx, m_c_ctx, v_c_ctx, g_c_ctx, 8), ("mod_b", mod_b, m_mod_b, v_mod_b, g_mod_b, 24),
              ("norm_g", norm_g, m_norm_g, v_norm_g, g_norm_g, 8), ("rpb", na_rpb, m_na_rpb, v_na_rpb, g_rpb, 8),
              ("pool_w", pool_w, m_pool_w, v_pool_w, g_pool_w, 64),
              ("pool_scale", pool_scale, m_pool_scale, v_pool_scale, g_pool_scale, 8),
              ("conv_w", conv_w, m_conv_w, v_conv_w, g_conv_w, 8), ("final_g", final_g, m_final_g, v_final_g, g_final_g, 8)]
    packed = [jnp.concatenate([_rows_of(s[col], s[5]) for s in smalls], axis=0) for col in (1, 2, 3, 4)]
    res = adamw(packed[0], packed[1], packed[2], "adamw_small", g=packed[3])
    row = 0
    for name, w, _, _, _, nrows in smalls:
        upd[name] = [r[row:row + nrows].reshape(-1)[:w.size].reshape(w.shape) for r in res]
        row += nrows

    order = ["c_ctx", "mod_w", "mod_b", "norm_g", "w13", "w2", "ewi", "ewo", "rpb", "pool_w", "pool_scale", "cwi",
             "conv_w", "cwo", "final_g"]
    grad_x = out["grad_x"][None]
    return (loss, grad_x, *[upd[k][0] for k in order], *[upd[k][1] for k in order], *[upd[k][2] for k in order],
            *[upd[k][3] for k in order])
```

```python
import functools

import numpy as np
import jax
import jax.numpy as jnp
from jax import lax
from jax.experimental import pallas as pl
from jax.experimental.pallas import tpu as pltpu

D = 1024
FF = 2816
SEQ = 16384
CTX = 256
GRID_W = 64
N_MOD = 9
HEADS = 8
HEAD_DIM = 64
NA_W = 512
POOL_W = 512
POOL_G = 128
POOL_WINDOWS = (2, 4, 8, 16)
KH = 8
KW = 16
RMS_EPS = 1e-6
NEG_INF = -1e30
N_DEV = 8

ADAM_LR = 0.001
ADAM_B1 = 0.9
ADAM_B2 = 0.999
ADAM_EPS = 1e-08
ADAM_WD = 0.01
ADAM_STEP = 10

VMEM_LIMIT = 52 * 1024 * 1024
HALO = 16
QROWS = 8
WROWS = 24

BF16 = jnp.bfloat16
F32 = jnp.float32
MESH_ID = pl.DeviceIdType.MESH
HI = lax.Precision.HIGHEST

NT_DIMS = (((1,), (1,)), ((), ()))
TN_DIMS = (((0,), (0,)), ((), ()))


def _tile(n, cands):
    for c in cands:
        if n % c == 0:
            return c
    return n


def _params(sem):
    return pltpu.CompilerParams(dimension_semantics=sem, vmem_limit_bytes=VMEM_LIMIT)


def _dot(a, b):
    return jnp.dot(a, b, preferred_element_type=F32)


def _dot_nt(a, b):
    return lax.dot_general(a, b, NT_DIMS, preferred_element_type=F32)


def _dot_tn(a, b):
    return lax.dot_general(a, b, TN_DIMS, preferred_element_type=F32)


def _sigmoid(x):
    return 1.0 / (1.0 + jnp.exp(-x))


def normmod(h, g, mods, i_shift, i_scale, name):
    n = h.shape[0]
    te = _tile(n, (512, 256))

    def body(h_ref, g_ref, m_ref, o_ref):
        x = h_ref[...]
        r = lax.rsqrt(jnp.mean(x * x, axis=-1, keepdims=True) + RMS_EPS)
        y = x * r * g_ref[...]
        o_ref[...] = (y * (1.0 + m_ref[i_scale:i_scale + 1, :]) + m_ref[i_shift:i_shift + 1, :]).astype(BF16)

    return pl.pallas_call(
        body, name=name, grid=(n // te,),
        in_specs=[pl.BlockSpec((te, D), lambda i: (i, 0)),
                  pl.BlockSpec((1, D), lambda i: (0, 0)),
                  pl.BlockSpec((N_MOD, D), lambda i: (0, 0))],
        out_specs=pl.BlockSpec((te, D), lambda i: (i, 0)),
        out_shape=jax.ShapeDtypeStruct((n, D), BF16),
        compiler_params=_params(("parallel",)),
    )(h, g, mods)


def loss_head(x, g, target, prev, name):
    n = x.shape[0]
    te = _tile(n, (256,))
    i_gate, coef = prev[2], prev[3]

    def body(x_ref, g_ref, t_ref, y_ref, m_ref, dx_ref, loss_ref, dg_ref, dy_ref, dgate_ref):
        @pl.when(pl.program_id(0) == 0)
        def _():
            loss_ref[...] = jnp.zeros_like(loss_ref)
            dg_ref[...] = jnp.zeros_like(dg_ref)
            dgate_ref[...] = jnp.zeros_like(dgate_ref)

        xv = x_ref[...]
        gv = g_ref[...]
        r = lax.rsqrt(jnp.mean(xv * xv, axis=-1, keepdims=True) + RMS_EPS)
        xhat = xv * r
        e = xhat * gv - t_ref[...]
        per_tok = jnp.mean(e * e, axis=-1, keepdims=True)
        loss_ref[...] += 0.5 * jnp.sum(per_tok, axis=0, keepdims=True)
        dy = e * (1.0 / D)
        dg_ref[...] += jnp.sum(dy * xhat, axis=0, keepdims=True)
        dxhat = dy * gv
        dx = r * (dxhat - xhat * jnp.mean(dxhat * xhat, axis=-1, keepdims=True))
        dx_ref[...] = dx
        dy_ref[...] = (dx * (coef * m_ref[i_gate:i_gate + 1, :])).astype(BF16)
        dgate_ref[...] += coef * jnp.sum(dx * y_ref[...].astype(F32), axis=0, keepdims=True)

    row = pl.BlockSpec((te, D), lambda i: (i, 0))
    vec = pl.BlockSpec((1, D), lambda i: (0, 0))
    return pl.pallas_call(
        body, name=name, grid=(n // te,),
        in_specs=[row, vec, row, row, pl.BlockSpec((N_MOD, D), lambda i: (0, 0))],
        out_specs=[row, pl.BlockSpec((1, 128), lambda i: (0, 0)), vec, row, vec],
        out_shape=[jax.ShapeDtypeStruct((n, D), F32), jax.ShapeDtypeStruct((1, 128), F32),
                   jax.ShapeDtypeStruct((1, D), F32), jax.ShapeDtypeStruct((n, D), BF16),
                   jax.ShapeDtypeStruct((1, D), F32)],
        compiler_params=_params(("arbitrary",)),
    )(x, g, target, prev[0], prev[1])


def ffn_up(hn, w13, name, exchange=None):
    n = hn.shape[0]
    ff = w13.shape[1] // 2
    tm = _tile(n, (512, 256))
    tn = _tile(ff, (1408, 512, 256, 128))
    nj = ff // tn
    ni = n // tm
    nx = exchange.n if exchange else 0

    def body(*refs):
        h_ref, wa_ref, wb_ref = refs[:3]
        x_in = refs[3:3 + nx]
        p_ref, u_ref, s_ref = refs[3 + nx:6 + nx]
        x_out = refs[6 + nx:6 + 2 * nx]
        x_sems = refs[6 + 2 * nx:]
        if exchange:
            @pl.when((pl.program_id(0) == 0) & (pl.program_id(1) == 0))
            def _():
                exchange.start(x_in, x_out, x_sems)

        hv = h_ref[...]
        a = _dot(hv, wa_ref[...])
        b = _dot(hv, wb_ref[...])
        sig = _sigmoid(a)
        p = a * sig
        p_ref[...] = p.astype(BF16)
        u_ref[...] = (b * (sig * (1.0 + a * (1.0 - sig)))).astype(BF16)
        s_ref[...] = (p * b).astype(BF16)

        if exchange:
            @pl.when((pl.program_id(0) == nj - 1) & (pl.program_id(1) == ni - 1))
            def _():
                exchange.finish(x_in, x_out, x_sems)

    out = pl.BlockSpec((tm, tn), lambda j, i: (i, j))
    hbm = pl.BlockSpec(memory_space=pltpu.HBM)
    sem = ("arbitrary", "arbitrary") if exchange else ("parallel", "parallel")
    res = pl.pallas_call(
        body, name=name, grid=(nj, ni),
        in_specs=[pl.BlockSpec((tm, D), lambda j, i: (i, 0)),
                  pl.BlockSpec((D, tn), lambda j, i: (0, j)),
                  pl.BlockSpec((D, tn), lambda j, i: (0, j + nj))] + [hbm] * nx,
        out_specs=[out, out, out] + [hbm] * nx,
        out_shape=[jax.ShapeDtypeStruct((n, ff), BF16)] * 3 + (exchange.out_shapes if exchange else []),
        scratch_shapes=exchange.scratch if exchange else [],
        compiler_params=_params(sem),
    )(hn, w13, w13, *(exchange.arrays if exchange else []))
    return res[:3], list(res[3:])


def mm_nn(a_list, w, row_offs, name, out_dtype=BF16, res=None, nxt=None):
    n = a_list[0].shape[0]
    nout = w.shape[1]
    ks = [a.shape[1] for a in a_list]
    tm = _tile(n, (512, 256))
    tn = _tile(nout, (1024, 512, 256, 128))
    na = len(a_list)
    assert nxt is None or (res is not None and tn == D)

    def body(*refs):
        a_refs = refs[:na]
        w_refs = refs[na:2 * na]
        acc = _dot(a_refs[0][...], w_refs[0][...])
        for k in range(1, na):
            acc += _dot(a_refs[k][...], w_refs[k][...])
        if res is None:
            refs[2 * na][...] = acc.astype(out_dtype)
        else:
            h_ref, m_ref = refs[2 * na:2 * na + 2]
            i_gate, coef = res[2], res[3]
            h_new = h_ref[...] + (coef * m_ref[i_gate:i_gate + 1, :]) * acc
            if nxt is None:
                hn_ref, y_ref = refs[2 * na + 2:]
            else:
                g2_ref, m2_ref, hn_ref, y_ref, nx_ref = refs[2 * na + 2:]
                r = lax.rsqrt(jnp.mean(h_new * h_new, axis=-1, keepdims=True) + RMS_EPS)
                nx_ref[...] = ((h_new * r * g2_ref[...]) * (1.0 + m2_ref[nxt[3]:nxt[3] + 1, :])
                               + m2_ref[nxt[2]:nxt[2] + 1, :]).astype(BF16)
            hn_ref[...] = h_new
            y_ref[...] = acc.astype(BF16)

    in_specs = [pl.BlockSpec((tm, k), lambda j, i: (i, 0)) for k in ks]
    for k, off in zip(ks, row_offs):
        in_specs.append(pl.BlockSpec((k, tn), functools.partial(lambda j, i, ob: (ob, j), ob=off // k)))
    args = list(a_list) + [w] * na
    out = pl.BlockSpec((tm, tn), lambda j, i: (i, j))
    if res is None:
        out_specs = out
        out_shape = jax.ShapeDtypeStruct((n, nout), out_dtype)
    else:
        in_specs += [out, pl.BlockSpec((N_MOD, tn), lambda j, i: (0, j))]
        args += [res[0], res[1]]
        out_specs = [out, out]
        out_shape = [jax.ShapeDtypeStruct((n, nout), F32), jax.ShapeDtypeStruct((n, nout), BF16)]
        if nxt is not None:
            in_specs += [pl.BlockSpec((1, D), lambda j, i: (0, 0)), pl.BlockSpec((N_MOD, D), lambda j, i: (0, 0))]
            args += [nxt[0], nxt[1]]
            out_specs.append(out)
            out_shape.append(jax.ShapeDtypeStruct((n, nout), BF16))
    return pl.pallas_call(
        body, name=name, grid=(nout // tn, n // tm),
        in_specs=in_specs, out_specs=out_specs, out_shape=out_shape,
        compiler_params=_params(("parallel", "parallel")),
    )(*args)


def mm_nt(g, wt, name, dswiglu=None):
    n, kg = g.shape
    nout = wt.shape[1]
    tm = _tile(n, (512, 256))
    tn = _tile(nout, (1408, 1024, 512, 256, 128))

    def body(*refs):
        r = _dot(refs[0][...], refs[1][...])
        if dswiglu is None:
            refs[2][...] = r.astype(BF16)
        else:
            u_ref, p_ref, da_ref, db_ref = refs[2:]
            da_ref[...] = (r * u_ref[...].astype(F32)).astype(BF16)
            db_ref[...] = (r * p_ref[...].astype(F32)).astype(BF16)

    out = pl.BlockSpec((tm, tn), lambda j, i: (i, j))
    in_specs = [pl.BlockSpec((tm, kg), lambda j, i: (i, 0)), pl.BlockSpec((kg, tn), lambda j, i: (0, j))]
    args = [g, wt]
    if dswiglu is None:
        out_specs = out
        out_shape = jax.ShapeDtypeStruct((n, nout), BF16)
    else:
        in_specs += [out, out]
        args += list(dswiglu)
        out_specs = [out, out]
        out_shape = [jax.ShapeDtypeStruct((n, nout), BF16)] * 2
    return pl.pallas_call(
        body, name=name, grid=(nout // tn, n // tm),
        in_specs=in_specs, out_specs=out_specs, out_shape=out_shape,
        compiler_params=_params(("parallel", "parallel")),
    )(*args)


def mm_nt_norm(g_list, w, col_offs, h, g, mods, i_scale, dres, name, prev=None, exchange=None):
    n = h.shape[0]
    kg = g_list[0].shape[1]
    tm = _tile(n, (512, 256))
    tk = _tile(kg, (1408, 1024, 512, 256, 128))
    ng = len(g_list)
    nk = kg // tk
    ni = n // tm
    has_res = dres is not None
    nx = exchange.n if exchange else 0

    def body(*refs):
        g_refs = refs[:ng]
        w_refs = refs[ng:2 * ng]
        pos = 2 * ng
        h_ref, gv_ref, m_ref = refs[pos:pos + 3]
        pos += 3
        if has_res:
            dres_ref = refs[pos]
            pos += 1
        if prev is not None:
            y_ref, mp_ref = refs[pos:pos + 2]
            pos += 2
        x_in = refs[pos:pos + nx]
        pos += nx
        dh_ref, dshift_ref, dscale_ref, dg_ref = refs[pos:pos + 4]
        pos += 4
        if prev is not None:
            dy_ref, dgate_ref = refs[pos:pos + 2]
            pos += 2
        x_out = refs[pos:pos + nx]
        pos += nx
        acc_ref = refs[pos]
        x_sems = refs[pos + 1:]
        i = pl.program_id(0)
        k = pl.program_id(1)

        @pl.when((i == 0) & (k == 0))
        def _():
            if exchange:
                exchange.start(x_in, x_out, x_sems)
            acc_ref[...] = jnp.zeros_like(acc_ref)
            dshift_ref[...] = jnp.zeros_like(dshift_ref)
            dscale_ref[...] = jnp.zeros_like(dscale_ref)
            dg_ref[...] = jnp.zeros_like(dg_ref)
            if prev is not None:
                dgate_ref[...] = jnp.zeros_like(dgate_ref)

        def dots():
            acc = _dot_nt(g_refs[0][...], w_refs[0][...])
            for q in range(1, ng):
                acc += _dot_nt(g_refs[q][...], w_refs[q][...])
            return acc

        cur = i % 2

        @pl.when(k == 0)
        def _():
            acc_ref[cur] = dots()
            valid = i >= 1
            d = acc_ref[1 - cur]
            x = h_ref[...]
            gv = gv_ref[...]
            r = lax.rsqrt(jnp.mean(x * x, axis=-1, keepdims=True) + RMS_EPS)
            xhat = x * r
            one_scale = 1.0 + m_ref[i_scale:i_scale + 1, :]
            t = d * xhat
            tsum = jnp.sum(t, axis=0, keepdims=True)
            dshift_ref[...] += jnp.sum(d, axis=0, keepdims=True)
            dscale_ref[...] += gv * tsum
            dg_ref[...] += one_scale * tsum
            cvec = one_scale * gv
            dh = r * (d * cvec - xhat * jnp.mean(t * cvec, axis=-1, keepdims=True))
            if has_res:
                dh = dh + dres_ref[...]
            dh_ref[...] = dh
            if prev is not None:
                i_gate, coef = prev[2], prev[3]
                dy_ref[...] = (dh * (coef * mp_ref[i_gate:i_gate + 1, :])).astype(BF16)
                part = coef * jnp.sum(dh * y_ref[...].astype(F32), axis=0, keepdims=True)
                dgate_ref[...] += jnp.where(valid, part, 0.0)

        if nk > 1:
            @pl.when(k > 0)
            def _():
                acc_ref[cur] += dots()

        if exchange:
            @pl.when((i == ni) & (k == nk - 1))
            def _():
                exchange.finish(x_in, x_out, x_sems)

    row = pl.BlockSpec((tm, D), lambda i, k: (jnp.maximum(i - 1, 0), 0))
    vec = pl.BlockSpec((1, D), lambda i, k: (0, 0))
    modspec = pl.BlockSpec((N_MOD, D), lambda i, k: (0, 0))
    in_specs = [pl.BlockSpec((tm, tk), lambda i, k: (jnp.minimum(i, ni - 1), k)) for _ in g_list]
    for off in col_offs:
        in_specs.append(pl.BlockSpec((D, tk), functools.partial(lambda i, k, ob: (0, ob + k), ob=off // tk)))
    in_specs += [row, vec, modspec]
    args = list(g_list) + [w] * ng + [h, g, mods]
    out_specs = [row, vec, vec, vec]
    out_shape = [jax.ShapeDtypeStruct((n, D), F32)] + [jax.ShapeDtypeStruct((1, D), F32)] * 3
    if has_res:
        in_specs.append(row)
        args.append(dres)
    if prev is not None:
        in_specs += [row, modspec]
        args += [prev[0], prev[1]]
        out_specs += [row, vec]
        out_shape += [jax.ShapeDtypeStruct((n, D), BF16), jax.ShapeDtypeStruct((1, D), F32)]
    scratch = [pltpu.VMEM((2, tm, D), F32)]
    if exchange:
        hbm = pl.BlockSpec(memory_space=pltpu.HBM)
        in_specs += [hbm] * nx
        args += exchange.arrays
        out_specs += [hbm] * nx
        out_shape += exchange.out_shapes
        scratch += exchange.scratch
    return pl.pallas_call(
        body, name=name, grid=(ni + 1, nk),
        in_specs=in_specs, out_specs=out_specs, out_shape=out_shape, scratch_shapes=scratch,
        compiler_params=_params(("arbitrary", "arbitrary")),
    )(*args)


def mm_tn(a, g, name, acc=None):
    n, ka = a.shape
    ngc = g.shape[1]
    tka = _tile(ka, (1408, 1024, 512, 256, 128))
    tng = _tile(ngc, (1408, 1024, 512, 256, 128))
    tr = _tile(n, (512, 256))
    has_acc = acc is not None

    def body(*refs):
        a_ref, g_ref = refs[0], refs[1]
        o_ref = refs[-1]
        r = pl.program_id(2)

        @pl.when(r == 0)
        def _():
            if has_acc:
                o_ref[...] = refs[2][...]
            else:
                o_ref[...] = jnp.zeros_like(o_ref)

        o_ref[...] += _dot_tn(a_ref[...], g_ref[...])

    out = pl.BlockSpec((tka, tng), lambda p, q, r: (p, q))
    in_specs = [pl.BlockSpec((tr, tka), lambda p, q, r: (r, p)),
                pl.BlockSpec((tr, tng), lambda p, q, r: (r, q))]
    args = [a, g]
    if has_acc:
        in_specs.append(out)
        args.append(acc)
    return pl.pallas_call(
        body, name=name, grid=(ka // tka, ngc // tng, n // tr),
        in_specs=in_specs, out_specs=out,
        out_shape=jax.ShapeDtypeStruct((ka, ngc), F32),
        compiler_params=_params(("parallel", "parallel", "arbitrary")),
    )(*args)


def mm_small(a, b, name, trans_b=False):
    m = a.shape[0]
    nout = b.shape[0] if trans_b else b.shape[1]

    def body(a_ref, b_ref, o_ref):
        if trans_b:
            o_ref[...] = lax.dot_general(a_ref[...], b_ref[...], NT_DIMS, precision=HI, preferred_element_type=F32)
        else:
            o_ref[...] = jnp.dot(a_ref[...], b_ref[...], precision=HI, preferred_element_type=F32)

    return pl.pallas_call(
        body, name=name,
        out_shape=jax.ShapeDtypeStruct((m, nout), F32),
        compiler_params=pltpu.CompilerParams(vmem_limit_bytes=VMEM_LIMIT),
    )(a, b)


def _col_tables():
    col = np.arange(GRID_W)
    start = np.clip(col - KW // 2, 0, GRID_W - KW)
    ok = (col[None, :] >= start[:, None]) & (col[None, :] < start[:, None] + KW)
    ci = np.clip(col[None, :] - col[:, None] + (KW - 1), 0, 2 * KW - 2)
    e = np.zeros((2 * KW - 1, GRID_W, GRID_W), np.float32)
    for c in range(2 * KW - 1):
        e[c] = (ci == c) & ok
    return e.reshape(2 * KW - 1, GRID_W * GRID_W), ok


def bias_table(rpb):
    e, ok = _col_tables()
    e_pad = np.zeros((32, GRID_W * GRID_W), np.float32)
    e_pad[:31] = e
    rp = jnp.pad(rpb.reshape(HEADS * 15, 31), ((0, 0), (0, 1)))
    t = mm_small(rp, jnp.asarray(e_pad), "rpb_expand").reshape(HEADS, 15, GRID_W, GRID_W)
    t = jnp.where(jnp.asarray(ok)[None, None], t, NEG_INF)
    tab = jnp.stack([t[:, v:v + KH] for v in range(8)], axis=0)
    return tab.transpose(0, 1, 3, 2, 4).reshape(TAB_SHAPE)


def bias_table_bwd(dtab):
    e, _ = _col_tables()
    e_pad = np.zeros((128, GRID_W * GRID_W), np.float32)
    e_pad[:31] = e
    d = dtab.reshape(8, HEADS, GRID_W, KH, GRID_W).transpose(0, 1, 3, 2, 4).reshape(8 * HEADS * KH, GRID_W * GRID_W)
    gv = mm_small(d, jnp.asarray(e_pad), "rpb_reduce", trans_b=True)[:, :31]
    gv = gv.reshape(8, HEADS, KH, 31).transpose(0, 2, 1, 3).reshape(8 * KH, HEADS * 31)
    sel = np.zeros((16, 8 * KH), np.float32)
    for v in range(8):
        for j in range(KH):
            sel[v + j, v * KH + j] = 1.0
    gpad = jnp.pad(gv, ((0, 0), (0, 256 - HEADS * 31)))
    out = mm_small(jnp.asarray(sel), gpad, "rpb_fold")[:15, :HEADS * 31]
    return out.reshape(15, HEADS, 31).transpose(1, 0, 2)


def _attn_geometry(seq):
    rows = seq // GRID_W
    nb = rows // QROWS
    return rows, nb


def _stack_heads(t2):
    first = (lax.broadcasted_iota(jnp.int32, (1, 128), 1) // HEAD_DIM) == 0
    zero = jnp.zeros_like(t2)
    return jnp.concatenate([jnp.where(first, t2, zero), jnp.where(first, zero, t2)], axis=0)


def _unstack_heads(t):
    first = (lax.broadcasted_iota(jnp.int32, (1, 128), 1) // HEAD_DIM) == 0
    return jnp.where(first, t[0:GRID_W], t[GRID_W:2 * GRID_W])


TAB_SHAPE = (8, HEADS // 2, 2 * GRID_W, KH * GRID_W)


def attn_fwd(qkvu, qkvu_c, tab, name, exchange=None):
    seq = qkvu.shape[0]
    nctx = qkvu_c.shape[0]
    rows, nb = _attn_geometry(seq)
    qt = QROWS * GRID_W
    wt = WROWS * GRID_W
    scale = HEAD_DIM ** -0.5
    nx = exchange.n if exchange else 0

    def wb0(i):
        return jnp.clip(i - 1, 0, nb - 3)

    def body(*refs):
        q_ref, k0, k1, k2, v0, v1, v2, kc_ref, vc_ref, tab_hbm = refs[:10]
        x_in = refs[10:10 + nx]
        o_ref = refs[10 + nx]
        x_out = refs[11 + nx:11 + 2 * nx]
        kbuf, vbuf, tab_s, sem = refs[11 + 2 * nx:15 + 2 * nx]
        x_sems = refs[15 + 2 * nx:]
        i = pl.program_id(0)

        @pl.when(i == 0)
        def _():
            if exchange:
                exchange.start(x_in, x_out, x_sems)
            cp = pltpu.make_async_copy(tab_hbm, tab_s, sem)
            cp.start()
            cp.wait()

        for t, (kr, vr) in enumerate(((k0, v0), (k1, v1), (k2, v2))):
            kbuf[t * qt:(t + 1) * qt, :] = kr[...]
            vbuf[t * qt:(t + 1) * qt, :] = vr[...]
        base = wb0(i) * QROWS

        def row_body(rl, carry):
            r = i * QROWS + rl
            rs = jnp.clip(r - KH // 2, 0, rows - KH)
            vi = rs - r + (KH - 1)
            off = pl.multiple_of((rs - base) * GRID_W, GRID_W)
            qoff = pl.multiple_of(rl * GRID_W, GRID_W)
            for p in range(HEADS // 2):
                ls = slice(p * 128, (p + 1) * 128)
                qst = _stack_heads(q_ref[pl.ds(qoff, GRID_W), ls])
                k2v = kbuf[pl.ds(off, KH * GRID_W), ls]
                v2v = vbuf[pl.ds(off, KH * GRID_W), ls]
                s_w = _dot_nt(qst, k2v) * scale + tab_s[vi, p]
                s_c = _dot_nt(qst, kc_ref[:, ls]) * scale
                m = jnp.maximum(jnp.max(s_w, axis=-1, keepdims=True), jnp.max(s_c, axis=-1, keepdims=True))
                pw = jnp.exp(s_w - m)
                pc = jnp.exp(s_c - m)
                l = jnp.sum(pw, axis=-1, keepdims=True) + jnp.sum(pc, axis=-1, keepdims=True)
                o = _dot(pw.astype(BF16), v2v) + _dot(pc.astype(BF16), vc_ref[:, ls])
                o_ref[pl.ds(qoff, GRID_W), ls] = _unstack_heads(o * (1.0 / l)).astype(BF16)
            return carry

        lax.fori_loop(0, QROWS, row_body, 0)

        if exchange:
            @pl.when(i == nb - 1)
            def _():
                exchange.finish(x_in, x_out, x_sems)

    blk = lambda col: [pl.BlockSpec((qt, NA_W), functools.partial(lambda i, t, c: (wb0(i) + t, c), t=t, c=col))
                       for t in range(3)]
    hbm = pl.BlockSpec(memory_space=pltpu.HBM)
    res = pl.pallas_call(
        body, name=name, grid=(nb,),
        in_specs=[pl.BlockSpec((qt, NA_W), lambda i: (i, 0))] + blk(1) + blk(2)
                 + [pl.BlockSpec((nctx, NA_W), lambda i: (0, 1)), pl.BlockSpec((nctx, NA_W), lambda i: (0, 2)),
                    pl.BlockSpec(memory_space=pl.ANY)] + [hbm] * nx,
        out_specs=[pl.BlockSpec((qt, NA_W), lambda i: (i, 0))] + [hbm] * nx,
        out_shape=[jax.ShapeDtypeStruct((seq, NA_W), BF16)] + (exchange.out_shapes if exchange else []),
        scratch_shapes=[pltpu.VMEM((wt, NA_W), BF16), pltpu.VMEM((wt, NA_W), BF16),
                        pltpu.VMEM(TAB_SHAPE, F32), pltpu.SemaphoreType.DMA] + (exchange.scratch if exchange else []),
        compiler_params=_params(("arbitrary",)),
    )(qkvu, qkvu, qkvu, qkvu, qkvu, qkvu, qkvu, qkvu_c, qkvu_c, tab, *(exchange.arrays if exchange else []))
    return res[0], list(res[1:])


def attn_bwd(qkvu, qkvu_c, tab, dmix, name, exchange=None):
    seq = qkvu.shape[0]
    nctx = qkvu_c.shape[0]
    rows, nb = _attn_geometry(seq)
    qt = QROWS * GRID_W
    wt = WROWS * GRID_W
    scale = HEAD_DIM ** -0.5
    nx = exchange.n if exchange else 0

    def wb0(i):
        return jnp.clip(i - 1, 0, nb - 3)

    def body(*refs):
        q_ref, k0, k1, k2, v0, v1, v2, kc_ref, vc_ref, do_ref, tab_hbm = refs[:11]
        x_in = refs[11:11 + nx]
        dq_ref, dk_hbm, dv_hbm, dkc_ref, dvc_ref, dtab_hbm = refs[11 + nx:17 + nx]
        x_out = refs[17 + nx:17 + 2 * nx]
        kbuf, vbuf, dkacc, dvacc, tab_s, dtab_s, stage, sem = refs[17 + 2 * nx:25 + 2 * nx]
        x_sems = refs[25 + 2 * nx:]
        i = pl.program_id(0)

        if exchange:
            @pl.when(i == 0)
            def _():
                exchange.start(x_in, x_out, x_sems)

        def flush(src, dst, block, dst_row):
            stage[...] = src[block * qt:(block + 1) * qt, :].astype(BF16)
            cp = pltpu.make_async_copy(stage, dst.at[pl.ds(dst_row, qt)], sem)
            cp.start()
            cp.wait()

        @pl.when(i == 0)
        def _():
            cp = pltpu.make_async_copy(tab_hbm, tab_s, sem)
            cp.start()
            cp.wait()
            dtab_s[...] = jnp.zeros_like(dtab_s)
            dkacc[...] = jnp.zeros_like(dkacc)
            dvacc[...] = jnp.zeros_like(dvacc)
            dkc_ref[...] = jnp.zeros_like(dkc_ref)
            dvc_ref[...] = jnp.zeros_like(dvc_ref)

        @pl.when((i >= 2) & (i <= nb - 2))
        def _():
            dst_row = pl.multiple_of((i - 2) * qt, qt)
            for acc_ref, dst in ((dkacc, dk_hbm), (dvacc, dv_hbm)):
                flush(acc_ref, dst, 0, dst_row)
                acc_ref[0:qt, :] = acc_ref[qt:2 * qt, :]
                acc_ref[qt:2 * qt, :] = acc_ref[2 * qt:3 * qt, :]
                acc_ref[2 * qt:3 * qt, :] = jnp.zeros((qt, NA_W), F32)

        for t, (kr, vr) in enumerate(((k0, v0), (k1, v1), (k2, v2))):
            kbuf[t * qt:(t + 1) * qt, :] = kr[...]
            vbuf[t * qt:(t + 1) * qt, :] = vr[...]
        base = wb0(i) * QROWS

        def row_body(rl, carry):
            r = i * QROWS + rl
            rs = jnp.clip(r - KH // 2, 0, rows - KH)
            vi = rs - r + (KH - 1)
            off = pl.multiple_of((rs - base) * GRID_W, GRID_W)
            qoff = pl.multiple_of(rl * GRID_W, GRID_W)
            for p in range(HEADS // 2):
                ls = slice(p * 128, (p + 1) * 128)
                qst = _stack_heads(q_ref[pl.ds(qoff, GRID_W), ls])
                dost = _stack_heads(do_ref[pl.ds(qoff, GRID_W), ls])
                k2v = kbuf[pl.ds(off, KH * GRID_W), ls]
                v2v = vbuf[pl.ds(off, KH * GRID_W), ls]
                kc2 = kc_ref[:, ls]
                vc2 = vc_ref[:, ls]
                s_w = _dot_nt(qst, k2v) * scale + tab_s[vi, p]
                s_c = _dot_nt(qst, kc2) * scale
                m = jnp.maximum(jnp.max(s_w, axis=-1, keepdims=True), jnp.max(s_c, axis=-1, keepdims=True))
                pw = jnp.exp(s_w - m)
                pc = jnp.exp(s_c - m)
                inv = 1.0 / (jnp.sum(pw, axis=-1, keepdims=True) + jnp.sum(pc, axis=-1, keepdims=True))
                pw = pw * inv
                pc = pc * inv
                dpw = _dot_nt(dost, v2v)
                dpc = _dot_nt(dost, vc2)
                delta = jnp.sum(pw * dpw, axis=-1, keepdims=True) + jnp.sum(pc * dpc, axis=-1, keepdims=True)
                ds_w = pw * (dpw - delta)
                ds_c = pc * (dpc - delta)
                dtab_s[vi, p] += ds_w
                dsw16 = ds_w.astype(BF16)
                dsc16 = ds_c.astype(BF16)
                dq = (_dot(dsw16, k2v) + _dot(dsc16, kc2)) * scale
                dq_ref[pl.ds(qoff, GRID_W), ls] = _unstack_heads(dq).astype(BF16)
                dkacc[pl.ds(off, KH * GRID_W), ls] += _dot_tn(dsw16, qst) * scale
                dvacc[pl.ds(off, KH * GRID_W), ls] += _dot_tn(pw.astype(BF16), dost)
                dkc_ref[:, ls] += _dot_tn(dsc16, qst) * scale
                dvc_ref[:, ls] += _dot_tn(pc.astype(BF16), dost)
            return carry

        lax.fori_loop(0, QROWS, row_body, 0)

        @pl.when(i == nb - 1)
        def _():
            for t in range(3):
                dst_row = (nb - 3 + t) * qt
                flush(dkacc, dk_hbm, t, dst_row)
                flush(dvacc, dv_hbm, t, dst_row)
            cp = pltpu.make_async_copy(dtab_s, dtab_hbm, sem)
            cp.start()
            cp.wait()
            if exchange:
                exchange.finish(x_in, x_out, x_sems)

    blk = lambda col: [pl.BlockSpec((qt, NA_W), functools.partial(lambda i, t, c: (wb0(i) + t, c), t=t, c=col))
                       for t in range(3)]
    any_spec = pl.BlockSpec(memory_space=pl.ANY)
    hbm = pl.BlockSpec(memory_space=pltpu.HBM)
    res = pl.pallas_call(
        body, name=name, grid=(nb,),
        in_specs=[pl.BlockSpec((qt, NA_W), lambda i: (i, 0))] + blk(1) + blk(2)
                 + [pl.BlockSpec((nctx, NA_W), lambda i: (0, 1)), pl.BlockSpec((nctx, NA_W), lambda i: (0, 2)),
                    pl.BlockSpec((qt, NA_W), lambda i: (i, 0)), any_spec] + [hbm] * nx,
        out_specs=[pl.BlockSpec((qt, NA_W), lambda i: (i, 0)), any_spec, any_spec,
                   pl.BlockSpec((nctx, NA_W), lambda i: (0, 0)), pl.BlockSpec((nctx, NA_W), lambda i: (0, 0)),
                   any_spec] + [hbm] * nx,
        out_shape=[jax.ShapeDtypeStruct((seq, NA_W), BF16), jax.ShapeDtypeStruct((seq, NA_W), BF16),
                   jax.ShapeDtypeStruct((seq, NA_W), BF16), jax.ShapeDtypeStruct((nctx, NA_W), F32),
                   jax.ShapeDtypeStruct((nctx, NA_W), F32), jax.ShapeDtypeStruct(TAB_SHAPE, F32)]
                  + (exchange.out_shapes if exchange else []),
        scratch_shapes=[pltpu.VMEM((wt, NA_W), BF16), pltpu.VMEM((wt, NA_W), BF16),
                        pltpu.VMEM((wt, NA_W), F32), pltpu.VMEM((wt, NA_W), F32),
                        pltpu.VMEM(TAB_SHAPE, F32), pltpu.VMEM(TAB_SHAPE, F32), pltpu.VMEM((qt, NA_W), BF16),
                        pltpu.SemaphoreType.DMA]
                       + (exchange.scratch if exchange else []),
        compiler_params=_params(("arbitrary",)),
    )(qkvu, qkvu, qkvu, qkvu, qkvu, qkvu, qkvu, qkvu_c, qkvu_c, dmix, tab, *(exchange.arrays if exchange else []))
    return res[:6], list(res[6:])


def _halo_specs(te, seq, col, width):
    per = te // HALO
    last = seq // HALO - 1
    return [pl.BlockSpec((HALO, width), lambda i: (jnp.maximum(i * per - 1, 0), col)),
            pl.BlockSpec((te, width), lambda i: (i, col)),
            pl.BlockSpec((HALO, width), lambda i: (jnp.minimum((i + 1) * per, last), col))]


def _extended(prev_ref, cur_ref, next_ref, i, te, seq):
    xe = jnp.concatenate([prev_ref[...], cur_ref[...], next_ref[...]], axis=0).astype(F32)
    pos = i * te - HALO + lax.broadcasted_iota(jnp.int32, (te + 2 * HALO, 1), 0)
    return jnp.where((pos >= 0) & (pos < seq), xe, 0.0), pos


def _window_sum(x, levels, n, mirrored):
    first = (n - 1) if mirrored else 1
    acc = x + pltpu.roll(x, first, 0)
    step = 1
    for _ in range(levels - 1):
        acc = pltpu.roll(acc, step, 0) + pltpu.roll(acc, n - step, 0)
        step *= 2
    return acc


def _window_count(pos, w, seq):
    lo = jnp.clip(pos - w // 2, 0, seq)
    hi = jnp.clip(pos - w // 2 + w, 0, seq)
    return jnp.maximum(hi - lo, 1).astype(F32)


def pool_fwd(qkvu, pool_w, pool_scale, name):
    seq = qkvu.shape[0]
    te = _tile(seq, (512, 256))
    n = te + 2 * HALO

    def body(up_ref, uc_ref, un_ref, w_ref, sc_ref, o_ref):
        i = pl.program_id(0)
        xe, pos = _extended(up_ref, uc_ref, un_ref, i, te, seq)
        cnt = pos[HALO:HALO + te]
        for g, w in enumerate(POOL_WINDOWS):
            ls = slice(g * POOL_G, (g + 1) * POOL_G)
            xg = xe[:, ls]
            win = _window_sum(xg, g + 1, n, False)[HALO:HALO + te]
            dlt = win / _window_count(cnt, w, seq) - xg[HALO:HALO + te]
            z = _dot(dlt.astype(BF16), w_ref[g])
            o_ref[:, ls] = (z * sc_ref[:, ls]).astype(BF16)

    return pl.pallas_call(
        body, name=name, grid=(seq // te,),
        in_specs=_halo_specs(te, seq, 3, POOL_W)
                 + [pl.BlockSpec((4, POOL_G, POOL_G), lambda i: (0, 0, 0)), pl.BlockSpec((1, POOL_W), lambda i: (0, 0))],
        out_specs=pl.BlockSpec((te, POOL_W), lambda i: (i, 0)),
        out_shape=jax.ShapeDtypeStruct((seq, POOL_W), BF16),
        compiler_params=_params(("parallel",)),
    )(qkvu, qkvu, qkvu, pool_w, pool_scale)


def pool_bwd(qkvu, dmix, pool_w, pool_scale, name):
    seq = qkvu.shape[0]
    te = _tile(seq, (512, 256))
    n = te + 2 * HALO

    def body(up_ref, uc_ref, un_ref, dp_ref, dc_ref, dn_ref, w_ref, sc_ref, du_ref, dw_ref, dsc_ref):
        i = pl.program_id(0)

        @pl.when(i == 0)
        def _():
            dw_ref[...] = jnp.zeros_like(dw_ref)
            dsc_ref[...] = jnp.zeros_like(dsc_ref)

        xe, pos = _extended(up_ref, uc_ref, un_ref, i, te, seq)
        de, _ = _extended(dp_ref, dc_ref, dn_ref, i, te, seq)
        cpos = pos[HALO:HALO + te]
        for g, w in enumerate(POOL_WINDOWS):
            ls = slice(g * POOL_G, (g + 1) * POOL_G)
            xg = xe[:, ls]
            wg = w_ref[g]
            win = _window_sum(xg, g + 1, n, False)[HALO:HALO + te]
            dlt = (win / _window_count(cpos, w, seq) - xg[HALO:HALO + te]).astype(BF16)
            z = _dot(dlt, wg)
            dpg = de[:, ls]
            dsc_ref[:, ls] += jnp.sum(dpg[HALO:HALO + te] * z, axis=0, keepdims=True)
            dz = (dpg * sc_ref[:, ls]).astype(BF16)
            dw_ref[g] += _dot_tn(dlt, dz[HALO:HALO + te])
            dd = _dot_nt(dz, wg)
            back = _window_sum(dd / _window_count(pos, w, seq), g + 1, n, True)
            du_ref[:, ls] = (back[HALO:HALO + te] - dd[HALO:HALO + te]).astype(BF16)

    return pl.pallas_call(
        body, name=name, grid=(seq // te,),
        in_specs=_halo_specs(te, seq, 3, POOL_W) + _halo_specs(te, seq, 1, POOL_W)
                 + [pl.BlockSpec((4, POOL_G, POOL_G), lambda i: (0, 0, 0)), pl.BlockSpec((1, POOL_W), lambda i: (0, 0))],
        out_specs=[pl.BlockSpec((te, POOL_W), lambda i: (i, 0)),
                   pl.BlockSpec((4, POOL_G, POOL_G), lambda i: (0, 0, 0)), pl.BlockSpec((1, POOL_W), lambda i: (0, 0))],
        out_shape=[jax.ShapeDtypeStruct((seq, POOL_W), BF16), jax.ShapeDtypeStruct((4, POOL_G, POOL_G), F32),
                   jax.ShapeDtypeStruct((1, POOL_W), F32)],
        compiler_params=_params(("arbitrary",)),
    )(qkvu, qkvu, qkvu, dmix, dmix, dmix, pool_w, pool_scale)


def _shifted(z, zprev_row, znext_row, te):
    rows = lax.broadcasted_iota(jnp.int32, (te, 1), 0)
    zp = jnp.where(rows == 0, zprev_row, pltpu.roll(z, 1, 0))
    zn = jnp.where(rows == te - 1, znext_row, pltpu.roll(z, te - 1, 0))
    return zp, zn


def _edge_rows(prev_ref, next_ref, i, nt):
    p = prev_ref[HALO - 1:HALO, :].astype(F32)
    q = next_ref[0:1, :].astype(F32)
    return jnp.where(i == 0, 0.0, p), jnp.where(i == nt - 1, 0.0, q)


def conv_fwd(proj, conv_w, name):
    seq = proj.shape[0]
    te = _tile(seq, (512, 256))
    nt = seq // te

    def body(bg_ref, cp_ref, cc_ref, cn_ref, xp_ref, xc_ref, xn_ref, w_ref, o_ref):
        i = pl.program_id(0)
        z = cc_ref[...].astype(F32) * xc_ref[...].astype(F32)
        cpr, cnr = _edge_rows(cp_ref, cn_ref, i, nt)
        xpr, xnr = _edge_rows(xp_ref, xn_ref, i, nt)
        zp, zn = _shifted(z, cpr * xpr, cnr * xnr, te)
        y = zp * w_ref[0:1, :] + z * w_ref[1:2, :] + zn * w_ref[2:3, :]
        o_ref[...] = (bg_ref[...].astype(F32) * y).astype(BF16)

    return pl.pallas_call(
        body, name=name, grid=(nt,),
        in_specs=[pl.BlockSpec((te, D), lambda i: (i, 0))] + _halo_specs(te, seq, 1, D) + _halo_specs(te, seq, 2, D)
                 + [pl.BlockSpec((3, D), lambda i: (0, 0))],
        out_specs=pl.BlockSpec((te, D), lambda i: (i, 0)),
        out_shape=jax.ShapeDtypeStruct((seq, D), BF16),
        compiler_params=_params(("parallel",)),
    )(proj, proj, proj, proj, proj, proj, proj, conv_w)


def conv_bwd(proj, dgm, conv_w, name):
    seq = proj.shape[0]
    te = _tile(seq, (512, 256))
    nt = seq // te

    def body(bp_ref, bc_ref, bn_ref, cp_ref, cc_ref, cn_ref, xp_ref, xc_ref, xn_ref, gp_ref, gc_ref, gn_ref, w_ref,
             dbg_ref, dcg_ref, dxin_ref, dw_ref):
        i = pl.program_id(0)

        @pl.when(i == 0)
        def _():
            dw_ref[...] = jnp.zeros_like(dw_ref)

        bg = bc_ref[...].astype(F32)
        cg = cc_ref[...].astype(F32)
        xin = xc_ref[...].astype(F32)
        dg = gc_ref[...].astype(F32)
        z = cg * xin
        cpr, cnr = _edge_rows(cp_ref, cn_ref, i, nt)
        xpr, xnr = _edge_rows(xp_ref, xn_ref, i, nt)
        zp, zn = _shifted(z, cpr * xpr, cnr * xnr, te)
        w0, w1, w2 = w_ref[0:1, :], w_ref[1:2, :], w_ref[2:3, :]
        y = zp * w0 + z * w1 + zn * w2
        dbg_ref[...] = (dg * y).astype(BF16)
        dy = dg * bg
        dw_ref[0:1, :] += jnp.sum(dy * zp, axis=0, keepdims=True)
        dw_ref[1:2, :] += jnp.sum(dy * z, axis=0, keepdims=True)
        dw_ref[2:3, :] += jnp.sum(dy * zn, axis=0, keepdims=True)
        bpr, bnr = _edge_rows(bp_ref, bn_ref, i, nt)
        gpr, gnr = _edge_rows(gp_ref, gn_ref, i, nt)
        dyp, dyn = _shifted(dy, bpr * gpr, bnr * gnr, te)
        dz = dyn * w0 + dy * w1 + dyp * w2
        dcg_ref[...] = (dz * xin).astype(BF16)
        dxin_ref[...] = (dz * cg).astype(BF16)

    row = pl.BlockSpec((te, D), lambda i: (i, 0))
    return pl.pallas_call(
        body, name=name, grid=(nt,),
        in_specs=_halo_specs(te, seq, 0, D) + _halo_specs(te, seq, 1, D) + _halo_specs(te, seq, 2, D)
                 + _halo_specs(te, seq, 0, D) + [pl.BlockSpec((3, D), lambda i: (0, 0))],
        out_specs=[row, row, row, pl.BlockSpec((3, D), lambda i: (0, 0))],
        out_shape=[jax.ShapeDtypeStruct((seq, D), BF16)] * 3 + [jax.ShapeDtypeStruct((3, D), F32)],
        compiler_params=_params(("arbitrary",)),
    )(proj, proj, proj, proj, proj, proj, proj, proj, proj, dgm, dgm, dgm, conv_w)


def _position():
    x, y, c = lax.axis_index("x"), lax.axis_index("y"), lax.axis_index("c")
    return x, y, c, 4 * x + 2 * y + c


def _peer(x, y, c, j):
    px = 1 - x if j & 4 else x
    py = 1 - y if j & 2 else y
    pc = 1 - c if j & 1 else c
    return (px, py, pc), 4 * px + 2 * py + pc


def small_allgather(v, name):
    rows, cols = v.shape

    def body(v_ref, o_ref, send_sems, recv_sems, local_sem):
        x, y, c, me = _position()
        mine = pltpu.make_async_copy(v_ref, o_ref.at[me], local_sem)
        mine.start()
        sends = []
        for j in range(1, N_DEV):
            peer, _ = _peer(x, y, c, j)
            cp = pltpu.make_async_remote_copy(src_ref=v_ref, dst_ref=o_ref.at[me], send_sem=send_sems.at[j - 1],
                                              recv_sem=recv_sems.at[j - 1], device_id=peer, device_id_type=MESH_ID)
            cp.start()
            sends.append(cp)
        for j in range(1, N_DEV):
            peer, pid = _peer(x, y, c, j)
            pltpu.make_async_remote_copy(src_ref=v_ref, dst_ref=o_ref.at[pid], send_sem=send_sems.at[j - 1],
                                         recv_sem=recv_sems.at[j - 1], device_id=peer,
                                         device_id_type=MESH_ID).wait_recv()
        for cp in sends:
            cp.wait_send()
        mine.wait()

    return pl.pallas_call(
        body, name=name,
        out_shape=jax.ShapeDtypeStruct((N_DEV, rows, cols), v.dtype),
        in_specs=[pl.BlockSpec(memory_space=pltpu.VMEM)],
        out_specs=pl.BlockSpec(memory_space=pltpu.VMEM),
        scratch_shapes=[pltpu.SemaphoreType.DMA((N_DEV - 1,)), pltpu.SemaphoreType.DMA((N_DEV - 1,)),
                        pltpu.SemaphoreType.DMA],
        compiler_params=pltpu.CompilerParams(vmem_limit_bytes=VMEM_LIMIT),
    )(v)


class Exchange:
    def __init__(self, kind, arrays):
        self.kind, self.arrays, self.n = kind, list(arrays), len(arrays)
        n = self.n
        if kind == "gather":
            self.out_shapes = [jax.ShapeDtypeStruct((N_DEV,) + a.shape, a.dtype) for a in self.arrays]
        else:
            self.out_shapes = [jax.ShapeDtypeStruct(a.shape, a.dtype) for a in self.arrays]
        self.scratch = [pltpu.SemaphoreType.DMA((7 * n,)), pltpu.SemaphoreType.DMA((7 * n,)),
                        pltpu.SemaphoreType.DMA((n,))]

    def _gather_copies(self, ins, outs, sems):
        send_sems, recv_sems, local_sems = sems
        x, y, c, me = _position()
        chips = [(1 - x, y), (x, 1 - y), (1 - x, 1 - y)]

        def blk(k, px, py, pc):
            return outs[k].at[4 * px + 2 * py + pc]

        def copy(k, slot, block, to, src=None):
            return pltpu.make_async_remote_copy(
                src_ref=blk(k, *block) if src is None else src, dst_ref=blk(k, *block),
                send_sem=send_sems.at[k * 7 + slot], recv_sem=recv_sems.at[k * 7 + slot],
                device_id=to, device_id_type=MESH_ID)

        mine = [pltpu.make_async_copy(ins[k], blk(k, x, y, c), local_sems.at[k]) for k in range(self.n)]
        first = []
        for k in range(self.n):
            first.append(copy(k, 0, (x, y, c), (x, y, 1 - c), src=ins[k]))
            first += [copy(k, 1 + j, (x, y, c), (*chip, c), src=ins[k]) for j, chip in enumerate(chips)]
        return (x, y, c), chips, copy, mine, first

    def start(self, ins, outs, sems):
        if self.kind == "gather":
            _, _, _, mine, first = self._gather_copies(ins, outs, sems)
            for cp in mine + first:
                cp.start()
        else:
            for cp in self._scatter_copies(ins, outs, sems, False):
                cp.start()

    def finish(self, ins, outs, sems):
        if self.kind == "gather":
            (x, y, c), chips, copy, mine, first = self._gather_copies(ins, outs, sems)
            passed = []
            for j, chip in enumerate(chips):
                for k in range(self.n):
                    copy(k, 1 + j, (*chip, c), (x, y, c)).wait_recv()
                    cp = copy(k, 4 + j, (*chip, c), (x, y, 1 - c))
                    cp.start()
                    passed.append(cp)
            for k in range(self.n):
                copy(k, 0, (x, y, 1 - c), (x, y, c)).wait_recv()
                for j, chip in enumerate(chips):
                    copy(k, 4 + j, (*chip, 1 - c), (x, y, c)).wait_recv()
            for cp in first + passed:
                cp.wait_send()
            for cp in mine:
                cp.wait()
        else:
            for cp in self._scatter_copies(ins, outs, sems, True):
                cp.wait_recv()
            copies = self._scatter_copies(ins, outs, sems, False)
            for cp in copies[self.n:]:
                cp.wait_send()
            for cp in copies[:self.n]:
                cp.wait()

    def _scatter_copies(self, ins, outs, sems, arrivals):
        send_sems, recv_sems, local_sems = sems
        x, y, c, me = _position()
        out = []
        if not arrivals:
            out = [pltpu.make_async_copy(ins[k].at[me], outs[k].at[me], local_sems.at[k]) for k in range(self.n)]
        for j in range(1, N_DEV):
            peer, pid = _peer(x, y, c, j)
            for k in range(self.n):
                out.append(pltpu.make_async_remote_copy(
                    src_ref=ins[k].at[pid], dst_ref=outs[k].at[pid if arrivals else me],
                    send_sem=send_sems.at[k * 7 + j - 1], recv_sem=recv_sems.at[k * 7 + j - 1],
                    device_id=peer, device_id_type=MESH_ID))
        return out

    def run(self, name):
        n = self.n

        def body(*refs):
            ins, outs, sems = refs[:n], refs[n:2 * n], refs[2 * n:]
            self.start(ins, outs, sems)
            self.finish(ins, outs, sems)

        hbm = pl.BlockSpec(memory_space=pltpu.HBM)
        return list(pl.pallas_call(
            body, name=name, out_shape=self.out_shapes, in_specs=[hbm] * n, out_specs=[hbm] * n,
            scratch_shapes=self.scratch,
        )(*self.arrays))


def sum_devices(v, name):
    _, rows, cols = v.shape

    def body(v_ref, o_ref):
        acc = v_ref[0]
        for p in range(1, N_DEV):
            acc = acc + v_ref[p]
        o_ref[...] = acc

    return pl.pallas_call(
        body, name=name, out_shape=jax.ShapeDtypeStruct((rows, cols), F32),
        compiler_params=pltpu.CompilerParams(vmem_limit_bytes=VMEM_LIMIT),
    )(v)


def _silu(x):
    return x * _sigmoid(x)


def adaln_fwd(cm, mod_w, mod_b_cols, name):
    cols = mod_w.shape[2]

    def body(c_ref, w_ref, b_ref, o_ref):
        o_ref[0] = jnp.dot(_silu(c_ref[...]), w_ref[0], precision=HI, preferred_element_type=F32) + b_ref[0]

    return pl.pallas_call(
        body, name=name, grid=(2,),
        in_specs=[pl.BlockSpec((16, D), lambda l: (0, 0)), pl.BlockSpec((1, D, cols), lambda l: (l, 0, 0)),
                  pl.BlockSpec((1, 1, cols), lambda l: (l, 0, 0))],
        out_specs=pl.BlockSpec((1, 16, cols), lambda l: (l, 0, 0)),
        out_shape=jax.ShapeDtypeStruct((2, 16, cols), F32),
        compiler_params=_params(("parallel",)),
    )(cm, mod_w, mod_b_cols)


def adaln_bwd(cm_t, mod_w, dm_t, name):
    cols = mod_w.shape[2]

    def body(c_ref, w_ref, lat_ref, ctx_ref, gw_ref, pc_ref):
        ctot = jnp.sum(ctx_ref[0], axis=0, keepdims=True)
        rows = lax.broadcasted_iota(jnp.int32, (8, 1), 0)
        g_hi = jnp.where(rows == 0, ctot, 0.0)
        g = jnp.concatenate([lat_ref[0], g_hi], axis=0)
        gw_ref[0] = jnp.dot(_silu(c_ref[...]), g, precision=HI, preferred_element_type=F32)
        pc_ref[0] = lax.dot_general(g_hi, w_ref[0], NT_DIMS, precision=HI, preferred_element_type=F32)

    return pl.pallas_call(
        body, name=name, grid=(2,),
        in_specs=[pl.BlockSpec((D, 16), lambda l: (0, 0)), pl.BlockSpec((1, D, cols), lambda l: (l, 0, 0)),
                  pl.BlockSpec((1, 8, cols), lambda l: (l, 0, 0)), pl.BlockSpec((1, 8, cols), lambda l: (l + 2, 0, 0))],
        out_specs=[pl.BlockSpec((1, D, cols), lambda l: (l, 0, 0)), pl.BlockSpec((1, 8, D), lambda l: (l, 0, 0))],
        out_shape=[jax.ShapeDtypeStruct((2, D, cols), F32), jax.ShapeDtypeStruct((2, 8, D), F32)],
        compiler_params=_params(("parallel",)),
    )(cm_t, mod_w, dm_t, dm_t)


def mod_b_grad(dm_t, name):
    width = dm_t.shape[2]
    tn = width // 8

    def body(d_ref, o_ref):
        s = jnp.concatenate([jnp.sum(d_ref[k], axis=0, keepdims=True) for k in range(4)]
                            + [jnp.zeros((4, tn), F32)], axis=0)
        o_ref[...] = s + pltpu.roll(s, 6, 0)

    return pl.pallas_call(
        body, name=name, grid=(8,),
        in_specs=[pl.BlockSpec((4, 8, tn), lambda j: (0, 0, j))],
        out_specs=pl.BlockSpec((8, tn), lambda j: (0, j)),
        out_shape=jax.ShapeDtypeStruct((8, width), F32),
        compiler_params=_params(("parallel",)),
    )(dm_t)


def adamw(w, m, v, name, g=None, recv=None):
    rows, cols = w.shape
    tr = _tile(rows, (256, 128, 64, 32, 16, 8))
    summed = recv is not None

    def body(w_ref, m_ref, v_ref, g_ref, go_ref, d_ref, mo_ref, vo_ref):
        if summed:
            gv = g_ref[0].astype(F32)
            for p in range(1, N_DEV):
                gv = gv + g_ref[p].astype(F32)
        else:
            gv = g_ref[...]
        mn = ADAM_B1 * m_ref[...] + (1.0 - ADAM_B1) * gv
        vn = ADAM_B2 * v_ref[...] + (1.0 - ADAM_B2) * (gv * gv)
        m_hat = mn / (1.0 - ADAM_B1 ** ADAM_STEP)
        v_hat = vn / (1.0 - ADAM_B2 ** ADAM_STEP)
        go_ref[...] = gv
        d_ref[...] = -ADAM_LR * (m_hat / (jnp.sqrt(v_hat) + ADAM_EPS) + ADAM_WD * w_ref[...])
        mo_ref[...] = mn
        vo_ref[...] = vn

    row = pl.BlockSpec((tr, cols), lambda i: (i, 0))
    gspec = pl.BlockSpec((N_DEV, tr, cols), lambda i: (0, i, 0)) if summed else row
    return pl.pallas_call(
        body, name=name, grid=(rows // tr,),
        in_specs=[row, row, row, gspec], out_specs=[row] * 4,
        out_shape=[jax.ShapeDtypeStruct((rows, cols), F32)] * 4,
        compiler_params=_params(("parallel",)),
    )(w, m, v, recv if summed else g)


def _ffn_fwd(h, hn, mods, w13, w2, base, tag, nxt=None, exchange=None):
    (p, u, s), exchanged = ffn_up(hn, w13, tag + "_up", exchange)
    if callable(w2):
        w2 = w2(exchanged)
    outs = mm_nn([s], w2, [0], tag + "_down", res=(h, mods, base + 2, 0.5), nxt=nxt)
    h_new, y = outs[0], outs[1]
    return h_new, (outs[2] if nxt else None), (h, hn, p, u, s, y), exchanged


COLUMN_CUT = ("w13", "ewi", "cwi")
GATHER_FIRST = ("w13_00",)
GATHER_IN_FFN = ("w2_00", "ewi", "ewo", "w13_01", "w2_01")
GATHER_IN_ATTN = ("w13_10", "w2_10", "cwi", "cwo", "w13_11", "w2_11")
SCATTER_IN_ATTN = ("w13_11", "w2_11", "cwi", "cwo", "w13_10", "w2_10", "w13_01", "w2_01", "ewo")
SCATTER_LAST = ("w13_00", "w2_00", "ewi")


def unpack_piece(p, g):
    if p.split("_")[0] in COLUMN_CUT:
        return g.transpose(1, 0, 2).reshape(g.shape[1], -1)
    return g.reshape(-1, g.shape[2])


def block_piece(p, full):
    if p.split("_")[0] in COLUMN_CUT:
        return full.reshape(full.shape[0], N_DEV, -1).transpose(1, 0, 2).astype(BF16)
    return full.reshape(N_DEV, -1, full.shape[1]).astype(BF16)


def _ffn_bwd(dy, dres, saved, mods, g, w13, w2, base, tag, prev=None, acc=None, exchange_of=None):
    h, hn, p, u, s, _ = saved
    ff = w2.shape[0]
    acc = acc or (None, None, None)
    da, db = mm_nt(dy, w2.T, tag + "_ds", dswiglu=(u, p))
    dw2 = mm_tn(s, dy, tag + "_dw2", acc=acc[2])
    dwa = mm_tn(hn, da, tag + "_dw13a", acc=acc[0])
    dwb = mm_tn(hn, db, tag + "_dw13b", acc=acc[1])
    exchange = exchange_of(dwa, dwb, dw2) if exchange_of else None
    outs = mm_nt_norm([da, db], w13, [0, ff], h, g, mods, base + 1, dres, tag + "_dhn", prev=prev, exchange=exchange)
    dh, dshift, dscale, dg = outs[:4]
    nprev = 2 if prev else 0
    return (dh, (dwa, dwb, dw2), dg, {base: dshift, base + 1: dscale}, tuple(outs[4:4 + nprev]),
            list(outs[4 + nprev:]))


def _mod_rows(parts):
    zero = jnp.zeros((1, D), F32)
    return jnp.concatenate([parts.get(k, zero) for k in range(N_MOD)], axis=0)


def local_step(x, ctx, ml, mc, wts, target, shards=None):
    wts = dict(wts)
    ng = wts["norm_g"]
    gvec = lambda l, k: ng[l, k][None, :]
    pool_w16 = wts["pool_w"].astype(BF16)
    grads = {}

    def gather(pieces):
        return Exchange("gather", [shards[p] for p in pieces]) if shards else None

    def arrived(pieces, results):
        for p, g in zip(pieces, results):
            wts[p] = unpack_piece(p, g)

    def ffn_grads(lf, f):
        grads["w13_" + lf] = jnp.concatenate([f[0], f[1]], axis=1)
        grads["w2_" + lf] = f[2]

    xh = normmod(x, gvec(0, 0), ml[0], 0, 1, "l0f1_norm")
    ch = normmod(ctx, gvec(0, 0), mc[0], 0, 1, "l0f1c_norm")
    def w2_after_up(got):
        arrived(GATHER_IN_FFN, got)
        return wts["w2_00"]

    x1, xn, sv1, _ = _ffn_fwd(x, xh, ml[0], wts["w13_00"], w2_after_up, 0, "l0f1",
                              nxt=(gvec(0, 1), ml[0], 3, 4), exchange=gather(GATHER_IN_FFN))
    c1, cn, sv1c, _ = _ffn_fwd(ctx, ch, mc[0], wts["w13_00"], wts["w2_00"], 0, "l0f1c", nxt=(gvec(0, 1), mc[0], 3, 4))
    qkvu = mm_nn([xn], wts["ewi"], [0], "l0mix_in")
    qkvu_c = mm_nn([cn], wts["ewi"], [0], "l0mix_in_c")
    tab = bias_table(wts["rpb"])
    att, got = attn_fwd(qkvu, qkvu_c, tab, "l0_attn", gather(GATHER_IN_ATTN))
    arrived(GATHER_IN_ATTN, got)
    pool = pool_fwd(qkvu, pool_w16, wts["pool_scale"], "l0_pool")
    x2, ymix, xh = mm_nn([att, pool], wts["ewo"], [0, NA_W], "l0mix_out", res=(x1, ml[0], 5, 1.0),
                         nxt=(gvec(0, 2), ml[0], 6, 7))
    x3, xh, sv2, _ = _ffn_fwd(x2, xh, ml[0], wts["w13_01"], wts["w2_01"], 6, "l0f2", nxt=(gvec(1, 0), ml[1], 0, 1))

    x4, xn1, sv3, _ = _ffn_fwd(x3, xh, ml[1], wts["w13_10"], wts["w2_10"], 0, "l1f1", nxt=(gvec(1, 1), ml[1], 3, 4))
    proj = mm_nn([xn1], wts["cwi"], [0], "l1mix_in")
    gm = conv_fwd(proj, wts["conv_w"], "l1_conv")
    x5, ycv, xh = mm_nn([gm], wts["cwo"], [0], "l1mix_out", res=(x4, ml[1], 5, 1.0), nxt=(gvec(1, 2), ml[1], 6, 7))
    x6, _, sv4, _ = _ffn_fwd(x5, xh, ml[1], wts["w13_11"], wts["w2_11"], 6, "l1f2")

    dx6, loss, dgf, dy, dgate = loss_head(x6, wts["final_g"][None, :], target, (sv4[5], ml[1], 8, 0.5), "loss_head")
    dm1 = {8: dgate}
    dx5, dwf4, dg12, dm_f4, (dy, dgate), _ = _ffn_bwd(dy, dx6, sv4, ml[1], gvec(1, 2), wts["w13_11"], wts["w2_11"], 6,
                                                      "l1f2", prev=(ycv, ml[1], 5, 1.0))
    ffn_grads("11", dwf4)
    dm1.update({5: dgate, **dm_f4})
    dgm = mm_nt(dy, wts["cwo"].T, "l1mix_dgm")
    grads["cwo"] = mm_tn(gm, dy, "l1mix_dwo")
    dbg, dcg, dxin, dconv_w = conv_bwd(proj, dgm, wts["conv_w"], "l1_conv_bwd")
    grads["cwi"] = jnp.concatenate([mm_tn(xn1, t, "l1mix_dwi%d" % k) for k, t in enumerate((dbg, dcg, dxin))], axis=1)
    dx4, dsh, dsc, dg11, dy, dgate = mm_nt_norm([dbg, dcg, dxin], wts["cwi"], [0, D, 2 * D], x4, gvec(1, 1), ml[1], 4,
                                                dx5, "l1mix_dxn", prev=(sv3[5], ml[1], 2, 0.5))
    dm1.update({3: dsh, 4: dsc, 2: dgate})
    dx3, dwf3, dg10, dm_f3, (dy, dgate), _ = _ffn_bwd(dy, dx4, sv3, ml[1], gvec(1, 0), wts["w13_10"], wts["w2_10"], 0,
                                                      "l1f1", prev=(sv2[5], ml[0], 8, 0.5))
    ffn_grads("10", dwf3)
    dm1.update(dm_f3)
    dm0 = {8: dgate}

    dx2, dwf2, dg02, dm_f2, (dy, dgate), _ = _ffn_bwd(dy, dx3, sv2, ml[0], gvec(0, 2), wts["w13_01"], wts["w2_01"], 6,
                                                      "l0f2", prev=(ymix, ml[0], 5, 1.0))
    ffn_grads("01", dwf2)
    dm0.update({5: dgate, **dm_f2})
    dmix = mm_nt(dy, wts["ewo"].T, "l0mix_dmix")
    grads["ewo"] = jnp.concatenate([mm_tn(att, dy, "l0mix_dwo_att"), mm_tn(pool, dy, "l0mix_dwo_pool")], axis=0)
    scatter = Exchange("scatter", [block_piece(p, grads.pop(p)) for p in SCATTER_IN_ATTN]) if shards else None
    (dq, dk, dv, dkc, dvc, dtab), got = attn_bwd(qkvu, qkvu_c, tab, dmix, "l0_attn_bwd", scatter)
    recv = dict(zip(SCATTER_IN_ATTN, got))
    du, dpool_w, dpool_scale = pool_bwd(qkvu, dmix, pool_w16, wts["pool_scale"], "l0_pool_bwd")
    drpb = bias_table_bwd(dtab)
    dk16, dv16, dkc16, dvc16 = dk, dv, dkc.astype(BF16), dvc.astype(BF16)
    grads["ewi"] = jnp.concatenate([
        mm_tn(xn, dq, "l0mix_dwi_q"),
        mm_tn(cn, dkc16, "l0mix_dwi_kc", acc=mm_tn(xn, dk16, "l0mix_dwi_k")),
        mm_tn(cn, dvc16, "l0mix_dwi_vc", acc=mm_tn(xn, dv16, "l0mix_dwi_v")),
        mm_tn(xn, du, "l0mix_dwi_u")], axis=1)
    dx1, dsh, dsc, dg01, dy, dgate = mm_nt_norm([dq, dk16, dv16, du], wts["ewi"], [0, NA_W, 2 * NA_W, 3 * NA_W], x1,
                                                gvec(0, 1), ml[0], 4, dx2, "l0mix_dxn", prev=(sv1[5], ml[0], 2, 0.5))
    dm0.update({3: dsh, 4: dsc, 2: dgate})
    dc1, dsh_c, dsc_c, dg01c, dy_c, dgate_c = mm_nt_norm([dkc16, dvc16], wts["ewi"], [NA_W, 2 * NA_W], c1, gvec(0, 1),
                                                         mc[0], 4, None, "l0mix_dxn_c", prev=(sv1c[5], mc[0], 2, 0.5))
    _, dwf1c, dg00c, dm_f1c, _, _ = _ffn_bwd(dy_c, dc1, sv1c, mc[0], gvec(0, 0), wts["w13_00"], wts["w2_00"], 0, "l0f1c")
    dmc0 = {3: dsh_c, 4: dsc_c, 2: dgate_c, **dm_f1c}

    def last_scatter(dwa, dwb, dw2):
        ffn_grads("00", (dwa, dwb, dw2))
        return Exchange("scatter", [block_piece(p, grads.pop(p)) for p in SCATTER_LAST]) if shards else None

    dx0, _, dg00, dm_f1, _, got = _ffn_bwd(dy, dx1, sv1, ml[0], gvec(0, 0), wts["w13_00"], wts["w2_00"], 0, "l0f1",
                                           acc=dwf1c, exchange_of=last_scatter)
    recv.update(zip(SCATTER_LAST, got))
    dm0.update(dm_f1)

    return {
        "loss": loss, "grad_x": dx0,
        "dml": jnp.stack([_mod_rows(dm0), _mod_rows(dm1)]),
        "dmc": jnp.stack([_mod_rows(dmc0), jnp.zeros((N_MOD, D), F32)]),
        "norm_g": jnp.concatenate([dg00 + dg00c, dg01 + dg01c, dg02, dg10, dg11, dg12], axis=0),
        "grads": grads, "recv": recv,
        "rpb": drpb, "pool_w": dpool_w, "pool_scale": dpool_scale, "conv_w": dconv_w, "final_g": dgf,
    }


def _rows_of(v, nrows):
    flat = v.reshape(-1)
    return jnp.pad(flat, (0, nrows * D - flat.shape[0])).reshape(nrows, D)


def kernel(x, c, ctx, c_ctx, mod_w, mod_b, norm_g, ffn_w13, ffn_w2, even_w_in, even_w_out, na_rpb, pool_w, pool_scale, conv_w_in, conv_w, conv_w_out, final_g, loss_target, m_c_ctx, m_mod_w, m_mod_b, m_norm_g, m_ffn_w13, m_ffn_w2, m_even_w_in, m_even_w_out, m_na_rpb, m_pool_w, m_pool_scale, m_conv_w_in, m_conv_w, m_conv_w_out, m_final_g, v_c_ctx, v_mod_w, v_mod_b, v_norm_g, v_ffn_w13, v_ffn_w2, v_even_w_in, v_even_w_out, v_na_rpb, v_pool_w, v_pool_scale, v_conv_w_in, v_conv_w, v_conv_w_out, v_final_g):
    me = 4 * lax.axis_index("x") + 2 * lax.axis_index("y") + lax.axis_index("c")
    ff = ffn_w2.shape[2] * N_DEV
    w13c = ffn_w13.shape[3]
    w2r = ffn_w2.shape[2]
    mcols = mod_w.shape[2]
    gcols = norm_g.shape[2]

    big = {"w13": ffn_w13.reshape(4 * D, w13c), "w2": ffn_w2.reshape(4 * w2r, D), "ewi": even_w_in[0],
           "ewo": even_w_out[0], "cwi": conv_w_in[0], "cwo": conv_w_out[0]}
    names = list(big)
    shards = {"ewi": even_w_in[0].astype(BF16), "ewo": even_w_out[0].astype(BF16),
              "cwi": conv_w_in[0].astype(BF16), "cwo": conv_w_out[0].astype(BF16)}
    for l in range(2):
        for f in range(2):
            shards["w13_%d%d" % (l, f)] = ffn_w13[l, f].astype(BF16)
            shards["w2_%d%d" % (l, f)] = ffn_w2[l, f].astype(BF16)
    first = Exchange("gather", [shards[p] for p in GATHER_FIRST]).run("weights_allgather_first")
    wts = {p: unpack_piece(p, g) for p, g in zip(GATHER_FIRST, first)}

    c_all = small_allgather(jnp.pad(c, ((0, 7), (0, 0))), "cond_allgather")[:, 0, :]
    cm = jnp.concatenate([c_all, c_ctx[None, :], jnp.zeros((7, D), F32)], axis=0)
    mod_b_cols = lax.dynamic_slice(mod_b, (0, me * mcols), (2, mcols))[:, None, :]
    m_cols = adaln_fwd(cm, mod_w, mod_b_cols, "adaln_fwd")
    m_all = small_allgather(m_cols.reshape(32, mcols), "mod_allgather")
    m_full = m_all.reshape(N_DEV, 2, 16, mcols).transpose(1, 2, 0, 3).reshape(2, 16, N_MOD * D)
    ml = lax.dynamic_slice(m_full, (0, me, 0), (2, 1, N_MOD * D)).reshape(2, N_MOD, D)
    mc = m_full[:, 8].reshape(2, N_MOD, D)

    full_norm_g = small_allgather(_rows_of(norm_g, 8), "norm_g_allgather")[:, 0, :2 * 3 * gcols]
    full_norm_g = full_norm_g.reshape(N_DEV, 2, 3, gcols).transpose(1, 2, 0, 3).reshape(2, 3, D)
    full_conv_w = small_allgather(_rows_of(conv_w, 8), "conv_w_allgather")[:, 0, :3 * gcols]
    full_conv_w = full_conv_w.reshape(N_DEV, 3, gcols).transpose(1, 0, 2).reshape(3, D)
    wts.update(norm_g=full_norm_g, conv_w=full_conv_w, rpb=na_rpb[0], pool_w=pool_w[0], pool_scale=pool_scale,
               final_g=final_g)
    out = local_step(x[0], ctx[0], ml, mc, wts, loss_target[0], shards)

    dm_pack = jnp.concatenate([out["dml"].reshape(2, N_MOD * D), out["dmc"].reshape(2, N_MOD * D),
                               jnp.zeros((4, N_MOD * D), F32)], axis=0)
    dm_t = small_allgather(dm_pack, "dmod_allgather").transpose(1, 0, 2)[:4]
    dm_cols = lax.dynamic_slice(dm_t, (0, 0, me * mcols), (4, N_DEV, mcols))
    g_mod_w, pc = adaln_bwd(cm.T, mod_w, dm_cols, "adaln_bwd")
    g_mod_b = mod_b_grad(dm_t, "mod_b_grad")[:2]

    pack = jnp.concatenate([_rows_of(t, 8) for t in (
        out["norm_g"], out["conv_w"], out["final_g"], pc[0, :1] + pc[1, :1], out["pool_scale"], out["loss"],
        out["rpb"])] + [_rows_of(out["pool_w"], 64)], axis=0)
    small = sum_devices(small_allgather(pack, "small_grads_allgather"), "small_grads_sum")
    g_norm_g = lax.dynamic_slice(small[0:6].reshape(2, 3, D), (0, 0, me * gcols), (2, 3, gcols))
    g_conv_w = lax.dynamic_slice(small[8:11], (0, me * gcols), (3, gcols))[None]
    g_final_g = small[16]
    sg = _sigmoid(c_ctx)
    g_c_ctx = small[24] * (sg * (1.0 + c_ctx * (1.0 - sg)))
    g_pool_scale = small[32:33, :POOL_W]
    loss = small[40, 0]
    g_rpb = small[48:52].reshape(-1)[:na_rpb.size].reshape(na_rpb.shape)
    g_pool_w = small[56:120].reshape(pool_w.shape)

    pieces = out["recv"]
    lf = ("00", "01", "10", "11")
    recv = {"w13": jnp.concatenate([pieces["w13_" + t] for t in lf], axis=1),
            "w2": jnp.concatenate([pieces["w2_" + t] for t in lf], axis=1),
            "ewi": pieces["ewi"], "ewo": pieces["ewo"], "cwi": pieces["cwi"], "cwo": pieces["cwo"]}

    moments = {"w13": (m_ffn_w13, v_ffn_w13), "w2": (m_ffn_w2, v_ffn_w2), "ewi": (m_even_w_in, v_even_w_in),
               "ewo": (m_even_w_out, v_even_w_out), "cwi": (m_conv_w_in, v_conv_w_in),
               "cwo": (m_conv_w_out, v_conv_w_out)}
    orig = {"w13": ffn_w13, "w2": ffn_w2, "ewi": even_w_in, "ewo": even_w_out, "cwi": conv_w_in, "cwo": conv_w_out}
    upd = {}
    for k in names:
        shp2 = big[k].shape
        res = adamw(big[k], moments[k][0].reshape(shp2), moments[k][1].reshape(shp2), "adamw_" + k, recv=recv[k])
        upd[k] = [r.reshape(orig[k].shape) for r in res]
    shp2 = (2 * D, mcols)
    upd["mod_w"] = [r.reshape(mod_w.shape) for r in adamw(mod_w.reshape(shp2), m_mod_w.reshape(shp2),
                                                          v_mod_w.reshape(shp2), "adamw_mod_w",
                                                          g=g_mod_w.reshape(shp2))]

    smalls = [("c_ctx", c_ctx, m_c_ctx, v_c_ctx, g_c_ctx, 8), ("mod_b", mod_b, m_mod_b, v_mod_b, g_mod_b, 24),
              ("norm_g", norm_g, m_norm_g, v_norm_g, g_norm_g, 8), ("rpb", na_rpb, m_na_rpb, v_na_rpb, g_rpb, 8),
              ("pool_w", pool_w, m_pool_w, v_pool_w, g_pool_w, 64),
              ("pool_scale", pool_scale, m_pool_scale, v_pool_scale, g_pool_scale, 8),
              ("conv_w", conv_w, m_conv_w, v_conv_w, g_conv_w, 8), ("final_g", final_g, m_final_g, v_final_g, g_final_g, 8)]
    packed = [jnp.concatenate([_rows_of(s[col], s[5]) for s in smalls], axis=0) for col in (1, 2, 3, 4)]
    res = adamw(packed[0], packed[1], packed[2], "adamw_small", g=packed[3])
    row = 0
    for name, w, _, _, _, nrows in smalls:
        upd[name] = [r[row:row + nrows].reshape(-1)[:w.size].reshape(w.shape) for r in res]
        row += nrows

    order = ["c_ctx", "mod_w", "mod_b", "norm_g", "w13", "w2", "ewi", "ewo", "rpb", "pool_w", "pool_scale", "cwi",
             "conv_w", "cwo", "final_g"]
    grad_x = out["grad_x"][None]
    return (loss, grad_x, *[upd[k][0] for k in order], *[upd[k][1] for k in order], *[upd[k][2] for k in order],
            *[upd[k][3] for k in order])
```

```python
import functools

import numpy as np
import jax
import jax.numpy as jnp
from jax import lax
from jax.experimental import pallas as pl
from jax.experimental.pallas import tpu as pltpu

D = 1024
FF = 2816
SEQ = 16384
CTX = 256
GRID_W = 64
N_MOD = 9
HEADS = 8
HEAD_DIM = 64
NA_W = 512
POOL_W = 512
POOL_G = 128
POOL_WINDOWS = (2, 4, 8, 16)
KH = 8
KW = 16
RMS_EPS = 1e-6
NEG_INF = -1e30
N_DEV = 8

ADAM_LR = 0.001
ADAM_B1 = 0.9
ADAM_B2 = 0.999
ADAM_EPS = 1e-08
ADAM_WD = 0.01
ADAM_STEP = 10

VMEM_LIMIT = 52 * 1024 * 1024
HALO = 16
QROWS = 8
WROWS = 24

BF16 = jnp.bfloat16
F32 = jnp.float32
MESH_ID = pl.DeviceIdType.MESH
HI = lax.Precision.HIGHEST

NT_DIMS = (((1,), (1,)), ((), ()))
TN_DIMS = (((0,), (0,)), ((), ()))


def _tile(n, cands):
    for c in cands:
        if n % c == 0:
            return c
    return n


def _params(sem):
    return pltpu.CompilerParams(dimension_semantics=sem, vmem_limit_bytes=VMEM_LIMIT)


def _dot(a, b):
    return jnp.dot(a, b, preferred_element_type=F32)


def _dot_nt(a, b):
    return lax.dot_general(a, b, NT_DIMS, preferred_element_type=F32)


def _dot_tn(a, b):
    return lax.dot_general(a, b, TN_DIMS, preferred_element_type=F32)


def _sigmoid(x):
    return 1.0 / (1.0 + jnp.exp(-x))


def normmod(h, g, mods, i_shift, i_scale, name, exchange=None):
    n = h.shape[0]
    te = _tile(n, (512, 256))
    nt = n // te
    nx = exchange.n if exchange else 0

    def body(*refs):
        h_ref, g_ref, m_ref = refs[:3]
        x_in = refs[3:3 + nx]
        o_ref = refs[3 + nx]
        x_out = refs[4 + nx:4 + 2 * nx]
        x_sems = refs[4 + 2 * nx:]
        if exchange:
            @pl.when(pl.program_id(0) == 0)
            def _():
                exchange.start(x_in, x_out, x_sems)

        x = h_ref[...]
        r = lax.rsqrt(jnp.mean(x * x, axis=-1, keepdims=True) + RMS_EPS)
        y = x * r * g_ref[...]
        o_ref[...] = (y * (1.0 + m_ref[i_scale:i_scale + 1, :]) + m_ref[i_shift:i_shift + 1, :]).astype(BF16)

        if exchange:
            @pl.when(pl.program_id(0) == nt - 1)
            def _():
                exchange.finish(x_in, x_out, x_sems)

    hbm = pl.BlockSpec(memory_space=pltpu.HBM)
    res = pl.pallas_call(
        body, name=name, grid=(nt,),
        in_specs=[pl.BlockSpec((te, D), lambda i: (i, 0)),
                  pl.BlockSpec((1, D), lambda i: (0, 0)),
                  pl.BlockSpec((N_MOD, D), lambda i: (0, 0))] + [hbm] * nx,
        out_specs=[pl.BlockSpec((te, D), lambda i: (i, 0))] + [hbm] * nx,
        out_shape=[jax.ShapeDtypeStruct((n, D), BF16)] + (exchange.out_shapes if exchange else []),
        scratch_shapes=exchange.scratch if exchange else [],
        compiler_params=_params(("arbitrary",) if exchange else ("parallel",)),
    )(h, g, mods, *(exchange.arrays if exchange else []))
    return (res[0], list(res[1:])) if exchange else res[0]


def loss_head(x, g, target, prev, name):
    n = x.shape[0]
    te = _tile(n, (256,))
    i_gate, coef = prev[2], prev[3]

    def body(x_ref, g_ref, t_ref, y_ref, m_ref, dx_ref, loss_ref, dg_ref, dy_ref, dgate_ref):
        @pl.when(pl.program_id(0) == 0)
        def _():
            loss_ref[...] = jnp.zeros_like(loss_ref)
            dg_ref[...] = jnp.zeros_like(dg_ref)
            dgate_ref[...] = jnp.zeros_like(dgate_ref)

        xv = x_ref[...]
        gv = g_ref[...]
        r = lax.rsqrt(jnp.mean(xv * xv, axis=-1, keepdims=True) + RMS_EPS)
        xhat = xv * r
        e = xhat * gv - t_ref[...]
        per_tok = jnp.mean(e * e, axis=-1, keepdims=True)
        loss_ref[...] += 0.5 * jnp.sum(per_tok, axis=0, keepdims=True)
        dy = e * (1.0 / D)
        dg_ref[...] += jnp.sum(dy * xhat, axis=0, keepdims=True)
        dxhat = dy * gv
        dx = r * (dxhat - xhat * jnp.mean(dxhat * xhat, axis=-1, keepdims=True))
        dx_ref[...] = dx
        dy_ref[...] = (dx * (coef * m_ref[i_gate:i_gate + 1, :])).astype(BF16)
        dgate_ref[...] += coef * jnp.sum(dx * y_ref[...].astype(F32), axis=0, keepdims=True)

    row = pl.BlockSpec((te, D), lambda i: (i, 0))
    vec = pl.BlockSpec((1, D), lambda i: (0, 0))
    return pl.pallas_call(
        body, name=name, grid=(n // te,),
        in_specs=[row, vec, row, row, pl.BlockSpec((N_MOD, D), lambda i: (0, 0))],
        out_specs=[row, pl.BlockSpec((1, 128), lambda i: (0, 0)), vec, row, vec],
        out_shape=[jax.ShapeDtypeStruct((n, D), F32), jax.ShapeDtypeStruct((1, 128), F32),
                   jax.ShapeDtypeStruct((1, D), F32), jax.ShapeDtypeStruct((n, D), BF16),
                   jax.ShapeDtypeStruct((1, D), F32)],
        compiler_params=_params(("arbitrary",)),
    )(x, g, target, prev[0], prev[1])


def ffn_up(hn, w13, name, exchange=None):
    n = hn.shape[0]
    ff = w13.shape[1] // 2
    tm = _tile(n, (512, 256))
    tn = _tile(ff, (1408, 512, 256, 128))
    nj = ff // tn
    ni = n // tm
    nx = exchange.n if exchange else 0

    def body(*refs):
        h_ref, wa_ref, wb_ref = refs[:3]
        x_in = refs[3:3 + nx]
        p_ref, u_ref, s_ref = refs[3 + nx:6 + nx]
        x_out = refs[6 + nx:6 + 2 * nx]
        x_sems = refs[6 + 2 * nx:]
        if exchange:
            @pl.when((pl.program_id(0) == 0) & (pl.program_id(1) == 0))
            def _():
                exchange.start(x_in, x_out, x_sems)

        hv = h_ref[...]
        a = _dot(hv, wa_ref[...])
        b = _dot(hv, wb_ref[...])
        sig = _sigmoid(a)
        p = a * sig
        p_ref[...] = p.astype(BF16)
        u_ref[...] = (b * (sig * (1.0 + a * (1.0 - sig)))).astype(BF16)
        s_ref[...] = (p * b).astype(BF16)

        if exchange:
            @pl.when((pl.program_id(0) == nj - 1) & (pl.program_id(1) == ni - 1))
            def _():
                exchange.finish(x_in, x_out, x_sems)

    out = pl.BlockSpec((tm, tn), lambda j, i: (i, j))
    hbm = pl.BlockSpec(memory_space=pltpu.HBM)
    sem = ("arbitrary", "arbitrary") if exchange else ("parallel", "parallel")
    res = pl.pallas_call(
        body, name=name, grid=(nj, ni),
        in_specs=[pl.BlockSpec((tm, D), lambda j, i: (i, 0)),
                  pl.BlockSpec((D, tn), lambda j, i: (0, j)),
                  pl.BlockSpec((D, tn), lambda j, i: (0, j + nj))] + [hbm] * nx,
        out_specs=[out, out, out] + [hbm] * nx,
        out_shape=[jax.ShapeDtypeStruct((n, ff), BF16)] * 3 + (exchange.out_shapes if exchange else []),
        scratch_shapes=exchange.scratch if exchange else [],
        compiler_params=_params(sem),
    )(hn, w13, w13, *(exchange.arrays if exchange else []))
    return res[:3], list(res[3:])


def mm_nn(a_list, w, row_offs, name, out_dtype=BF16, res=None, nxt=None):
    n = a_list[0].shape[0]
    nout = w.shape[1]
    ks = [a.shape[1] for a in a_list]
    tm = _tile(n, (512, 256))
    tn = _tile(nout, (1024, 512, 256, 128))
    na = len(a_list)
    assert nxt is None or (res is not None and tn == D)

    def body(*refs):
        a_refs = refs[:na]
        w_refs = refs[na:2 * na]
        acc = _dot(a_refs[0][...], w_refs[0][...])
        for k in range(1, na):
            acc += _dot(a_refs[k][...], w_refs[k][...])
        if res is None:
            refs[2 * na][...] = acc.astype(out_dtype)
        else:
            h_ref, m_ref = refs[2 * na:2 * na + 2]
            i_gate, coef = res[2], res[3]
            h_new = h_ref[...] + (coef * m_ref[i_gate:i_gate + 1, :]) * acc
            if nxt is None:
                hn_ref, y_ref = refs[2 * na + 2:]
            else:
                g2_ref, m2_ref, hn_ref, y_ref, nx_ref = refs[2 * na + 2:]
                r = lax.rsqrt(jnp.mean(h_new * h_new, axis=-1, keepdims=True) + RMS_EPS)
                nx_ref[...] = ((h_new * r * g2_ref[...]) * (1.0 + m2_ref[nxt[3]:nxt[3] + 1, :])
                               + m2_ref[nxt[2]:nxt[2] + 1, :]).astype(BF16)
            hn_ref[...] = h_new
            y_ref[...] = acc.astype(BF16)

    in_specs = [pl.BlockSpec((tm, k), lambda j, i: (i, 0)) for k in ks]
    for k, off in zip(ks, row_offs):
        in_specs.append(pl.BlockSpec((k, tn), functools.partial(lambda j, i, ob: (ob, j), ob=off // k)))
    args = list(a_list) + [w] * na
    out = pl.BlockSpec((tm, tn), lambda j, i: (i, j))
    if res is None:
        out_specs = out
        out_shape = jax.ShapeDtypeStruct((n, nout), out_dtype)
    else:
        in_specs += [out, pl.BlockSpec((N_MOD, tn), lambda j, i: (0, j))]
        args += [res[0], res[1]]
        out_specs = [out, out]
        out_shape = [jax.ShapeDtypeStruct((n, nout), F32), jax.ShapeDtypeStruct((n, nout), BF16)]
        if nxt is not None:
            in_specs += [pl.BlockSpec((1, D), lambda j, i: (0, 0)), pl.BlockSpec((N_MOD, D), lambda j, i: (0, 0))]
            args += [nxt[0], nxt[1]]
            out_specs.append(out)
            out_shape.append(jax.ShapeDtypeStruct((n, nout), BF16))
    return pl.pallas_call(
        body, name=name, grid=(nout // tn, n // tm),
        in_specs=in_specs, out_specs=out_specs, out_shape=out_shape,
        compiler_params=_params(("parallel", "parallel")),
    )(*args)


def mm_nt(g, w, name, dswiglu=None):
    n, kg = g.shape
    nout = w.shape[0]
    tm = _tile(n, (512, 256))
    tn = _tile(nout, (1408, 1024, 512, 256, 128))

    def body(*refs):
        r = _dot_nt(refs[0][...], refs[1][...])
        if dswiglu is None:
            refs[2][...] = r.astype(BF16)
        else:
            u_ref, p_ref, da_ref, db_ref = refs[2:]
            da_ref[...] = (r * u_ref[...].astype(F32)).astype(BF16)
            db_ref[...] = (r * p_ref[...].astype(F32)).astype(BF16)

    out = pl.BlockSpec((tm, tn), lambda j, i: (i, j))
    in_specs = [pl.BlockSpec((tm, kg), lambda j, i: (i, 0)), pl.BlockSpec((tn, kg), lambda j, i: (j, 0))]
    args = [g, w]
    if dswiglu is None:
        out_specs = out
        out_shape = jax.ShapeDtypeStruct((n, nout), BF16)
    else:
        in_specs += [out, out]
        args += list(dswiglu)
        out_specs = [out, out]
        out_shape = [jax.ShapeDtypeStruct((n, nout), BF16)] * 2
    return pl.pallas_call(
        body, name=name, grid=(nout // tn, n // tm),
        in_specs=in_specs, out_specs=out_specs, out_shape=out_shape,
        compiler_params=_params(("parallel", "parallel")),
    )(*args)


def mm_nt_norm(g_list, w, col_offs, h, g, mods, i_scale, dres, name, prev=None, exchange=None):
    n = h.shape[0]
    kg = g_list[0].shape[1]
    tm = _tile(n, (512, 256))
    tk = _tile(kg, (1408, 1024, 512, 256, 128))
    ng = len(g_list)
    nk = kg // tk
    ni = n // tm
    has_res = dres is not None
    nx = exchange.n if exchange else 0

    def body(*refs):
        g_refs = refs[:ng]
        w_refs = refs[ng:2 * ng]
        pos = 2 * ng
        h_ref, gv_ref, m_ref = refs[pos:pos + 3]
        pos += 3
        if has_res:
            dres_ref = refs[pos]
            pos += 1
        if prev is not None:
            y_ref, mp_ref = refs[pos:pos + 2]
            pos += 2
        x_in = refs[pos:pos + nx]
        pos += nx
        dh_ref, dshift_ref, dscale_ref, dg_ref = refs[pos:pos + 4]
        pos += 4
        if prev is not None:
            dy_ref, dgate_ref = refs[pos:pos + 2]
            pos += 2
        x_out = refs[pos:pos + nx]
        pos += nx
        acc_ref = refs[pos]
        x_sems = refs[pos + 1:]
        i = pl.program_id(0)
        k = pl.program_id(1)

        @pl.when((i == 0) & (k == 0))
        def _():
            if exchange:
                exchange.start(x_in, x_out, x_sems)
            dshift_ref[...] = jnp.zeros_like(dshift_ref)
            dscale_ref[...] = jnp.zeros_like(dscale_ref)
            dg_ref[...] = jnp.zeros_like(dg_ref)
            if prev is not None:
                dgate_ref[...] = jnp.zeros_like(dgate_ref)

        @pl.when(k == 0)
        def _():
            acc_ref[...] = jnp.zeros_like(acc_ref)

        acc = _dot_nt(g_refs[0][...], w_refs[0][...])
        for q in range(1, ng):
            acc += _dot_nt(g_refs[q][...], w_refs[q][...])
        acc_ref[...] += acc

        @pl.when(k == nk - 1)
        def _():
            d = acc_ref[...]
            x = h_ref[...]
            gv = gv_ref[...]
            r = lax.rsqrt(jnp.mean(x * x, axis=-1, keepdims=True) + RMS_EPS)
            xhat = x * r
            one_scale = 1.0 + m_ref[i_scale:i_scale + 1, :]
            t = d * xhat
            tsum = jnp.sum(t, axis=0, keepdims=True)
            dshift_ref[...] += jnp.sum(d, axis=0, keepdims=True)
            dscale_ref[...] += gv * tsum
            dg_ref[...] += one_scale * tsum
            cvec = one_scale * gv
            dh = r * (d * cvec - xhat * jnp.mean(t * cvec, axis=-1, keepdims=True))
            if has_res:
                dh = dh + dres_ref[...]
            dh_ref[...] = dh
            if prev is not None:
                i_gate, coef = prev[2], prev[3]
                dy_ref[...] = (dh * (coef * mp_ref[i_gate:i_gate + 1, :])).astype(BF16)
                dgate_ref[...] += coef * jnp.sum(dh * y_ref[...].astype(F32), axis=0, keepdims=True)

        if exchange:
            @pl.when((i == ni - 1) & (k == nk - 1))
            def _():
                exchange.finish(x_in, x_out, x_sems)

    row = pl.BlockSpec((tm, D), lambda i, k: (i, 0))
    vec = pl.BlockSpec((1, D), lambda i, k: (0, 0))
    modspec = pl.BlockSpec((N_MOD, D), lambda i, k: (0, 0))
    in_specs = [pl.BlockSpec((tm, tk), lambda i, k: (i, k)) for _ in g_list]
    for off in col_offs:
        in_specs.append(pl.BlockSpec((D, tk), functools.partial(lambda i, k, ob: (0, ob + k), ob=off // tk)))
    in_specs += [row, vec, modspec]
    args = list(g_list) + [w] * ng + [h, g, mods]
    out_specs = [row, vec, vec, vec]
    out_shape = [jax.ShapeDtypeStruct((n, D), F32)] + [jax.ShapeDtypeStruct((1, D), F32)] * 3
    if has_res:
        in_specs.append(row)
        args.append(dres)
    if prev is not None:
        in_specs += [row, modspec]
        args += [prev[0], prev[1]]
        out_specs += [row, vec]
        out_shape += [jax.ShapeDtypeStruct((n, D), BF16), jax.ShapeDtypeStruct((1, D), F32)]
    scratch = [pltpu.VMEM((tm, D), F32)]
    if exchange:
        hbm = pl.BlockSpec(memory_space=pltpu.HBM)
        in_specs += [hbm] * nx
        args += exchange.arrays
        out_specs += [hbm] * nx
        out_shape += exchange.out_shapes
        scratch += exchange.scratch
    return pl.pallas_call(
        body, name=name, grid=(ni, nk),
        in_specs=in_specs, out_specs=out_specs, out_shape=out_shape, scratch_shapes=scratch,
        compiler_params=_params(("arbitrary", "arbitrary")),
    )(*args)


def mm_tn(a, g, name, acc=None):
    n, ka = a.shape
    ngc = g.shape[1]
    tka = _tile(ka, (1408, 1024, 512, 256, 128))
    tng = _tile(ngc, (1408, 1024, 512, 256, 128))
    tr = _tile(n, (512, 256))
    has_acc = acc is not None

    def body(*refs):
        a_ref, g_ref = refs[0], refs[1]
        o_ref = refs[-1]
        r = pl.program_id(2)

        @pl.when(r == 0)
        def _():
            if has_acc:
                o_ref[...] = refs[2][...]
            else:
                o_ref[...] = jnp.zeros_like(o_ref)

        o_ref[...] += _dot_tn(a_ref[...], g_ref[...])

    out = pl.BlockSpec((tka, tng), lambda p, q, r: (p, q))
    in_specs = [pl.BlockSpec((tr, tka), lambda p, q, r: (r, p)),
                pl.BlockSpec((tr, tng), lambda p, q, r: (r, q))]
    args = [a, g]
    if has_acc:
        in_specs.append(out)
        args.append(acc)
    return pl.pallas_call(
        body, name=name, grid=(ka // tka, ngc // tng, n // tr),
        in_specs=in_specs, out_specs=out,
        out_shape=jax.ShapeDtypeStruct((ka, ngc), F32),
        compiler_params=_params(("parallel", "parallel", "arbitrary")),
    )(*args)


def mm_small(a, b, name, trans_b=False):
    m = a.shape[0]
    nout = b.shape[0] if trans_b else b.shape[1]

    def body(a_ref, b_ref, o_ref):
        if trans_b:
            o_ref[...] = lax.dot_general(a_ref[...], b_ref[...], NT_DIMS, precision=HI, preferred_element_type=F32)
        else:
            o_ref[...] = jnp.dot(a_ref[...], b_ref[...], precision=HI, preferred_element_type=F32)

    return pl.pallas_call(
        body, name=name,
        out_shape=jax.ShapeDtypeStruct((m, nout), F32),
        compiler_params=pltpu.CompilerParams(vmem_limit_bytes=VMEM_LIMIT),
    )(a, b)


def _col_tables():
    col = np.arange(GRID_W)
    start = np.clip(col - KW // 2, 0, GRID_W - KW)
    ok = (col[None, :] >= start[:, None]) & (col[None, :] < start[:, None] + KW)
    ci = np.clip(col[None, :] - col[:, None] + (KW - 1), 0, 2 * KW - 2)
    e = np.zeros((2 * KW - 1, GRID_W, GRID_W), np.float32)
    for c in range(2 * KW - 1):
        e[c] = (ci == c) & ok
    return e.reshape(2 * KW - 1, GRID_W * GRID_W), ok


def bias_table(rpb):
    e, ok = _col_tables()
    e_pad = np.zeros((32, GRID_W * GRID_W), np.float32)
    e_pad[:31] = e
    rp = jnp.pad(rpb.reshape(HEADS * 15, 31), ((0, 0), (0, 1)))
    t = mm_small(rp, jnp.asarray(e_pad), "rpb_expand").reshape(HEADS, 15, GRID_W, GRID_W)
    t = jnp.where(jnp.asarray(ok)[None, None], t, NEG_INF)
    tab = jnp.stack([t[:, v:v + KH] for v in range(8)], axis=0)
    return tab.transpose(0, 1, 3, 2, 4).reshape(TAB_SHAPE)


def bias_table_bwd(dtab):
    e, _ = _col_tables()
    e_pad = np.zeros((128, GRID_W * GRID_W), np.float32)
    e_pad[:31] = e
    d = dtab.reshape(8, HEADS, GRID_W, KH, GRID_W).transpose(0, 1, 3, 2, 4).reshape(8 * HEADS * KH, GRID_W * GRID_W)
    gv = mm_small(d, jnp.asarray(e_pad), "rpb_reduce", trans_b=True)[:, :31]
    gv = gv.reshape(8, HEADS, KH, 31).transpose(0, 2, 1, 3).reshape(8 * KH, HEADS * 31)
    sel = np.zeros((16, 8 * KH), np.float32)
    for v in range(8):
        for j in range(KH):
            sel[v + j, v * KH + j] = 1.0
    gpad = jnp.pad(gv, ((0, 0), (0, 256 - HEADS * 31)))
    out = mm_small(jnp.asarray(sel), gpad, "rpb_fold")[:15, :HEADS * 31]
    return out.reshape(15, HEADS, 31).transpose(1, 0, 2)


def _attn_geometry(seq):
    rows = seq // GRID_W
    nb = rows // QROWS
    return rows, nb


def _stack_heads(t2):
    first = (lax.broadcasted_iota(jnp.int32, (1, 128), 1) // HEAD_DIM) == 0
    zero = jnp.zeros_like(t2)
    return jnp.concatenate([jnp.where(first, t2, zero), jnp.where(first, zero, t2)], axis=0)


def _unstack_heads(t):
    first = (lax.broadcasted_iota(jnp.int32, (1, 128), 1) // HEAD_DIM) == 0
    return jnp.where(first, t[0:GRID_W], t[GRID_W:2 * GRID_W])


TAB_SHAPE = (8, HEADS // 2, 2 * GRID_W, KH * GRID_W)


def attn_fwd(qkvu, qkvu_c, tab, name, exchange=None):
    seq = qkvu.shape[0]
    nctx = qkvu_c.shape[0]
    rows, nb = _attn_geometry(seq)
    qt = QROWS * GRID_W
    wt = WROWS * GRID_W
    scale = HEAD_DIM ** -0.5
    nx = exchange.n if exchange else 0

    def wb0(i):
        return jnp.clip(i - 1, 0, nb - 3)

    def body(*refs):
        q_ref, k0, k1, k2, v0, v1, v2, kc_ref, vc_ref, tab_hbm = refs[:10]
        x_in = refs[10:10 + nx]
        o_ref = refs[10 + nx]
        x_out = refs[11 + nx:11 + 2 * nx]
        kbuf, vbuf, tab_s, sem = refs[11 + 2 * nx:15 + 2 * nx]
        x_sems = refs[15 + 2 * nx:]
        i = pl.program_id(0)

        @pl.when(i == 0)
        def _():
            if exchange:
                exchange.start(x_in, x_out, x_sems)
            cp = pltpu.make_async_copy(tab_hbm, tab_s, sem)
            cp.start()
            cp.wait()

        for t, (kr, vr) in enumerate(((k0, v0), (k1, v1), (k2, v2))):
            kbuf[t * qt:(t + 1) * qt, :] = kr[...]
            vbuf[t * qt:(t + 1) * qt, :] = vr[...]
        base = wb0(i) * QROWS

        def row_body(rl, carry):
            r = i * QROWS + rl
            rs = jnp.clip(r - KH // 2, 0, rows - KH)
            vi = rs - r + (KH - 1)
            off = pl.multiple_of((rs - base) * GRID_W, GRID_W)
            qoff = pl.multiple_of(rl * GRID_W, GRID_W)
            for p in range(HEADS // 2):
                ls = slice(p * 128, (p + 1) * 128)
                qst = _stack_heads(q_ref[pl.ds(qoff, GRID_W), ls])
                k2v = kbuf[pl.ds(off, KH * GRID_W), ls]
                v2v = vbuf[pl.ds(off, KH * GRID_W), ls]
                s_w = _dot_nt(qst, k2v) * scale + tab_s[vi, p]
                s_c = _dot_nt(qst, kc_ref[:, ls]) * scale
                m = jnp.maximum(jnp.max(s_w, axis=-1, keepdims=True), jnp.max(s_c, axis=-1, keepdims=True))
                pw = jnp.exp(s_w - m)
                pc = jnp.exp(s_c - m)
                l = jnp.sum(pw, axis=-1, keepdims=True) + jnp.sum(pc, axis=-1, keepdims=True)
                o = _dot(pw.astype(BF16), v2v) + _dot(pc.astype(BF16), vc_ref[:, ls])
                o_ref[pl.ds(qoff, GRID_W), ls] = _unstack_heads(o * (1.0 / l)).astype(BF16)
            return carry

        lax.fori_loop(0, QROWS, row_body, 0)

        if exchange:
            @pl.when(i == nb - 1)
            def _():
                exchange.finish(x_in, x_out, x_sems)

    blk = lambda col: [pl.BlockSpec((qt, NA_W), functools.partial(lambda i, t, c: (wb0(i) + t, c), t=t, c=col))
                       for t in range(3)]
    hbm = pl.BlockSpec(memory_space=pltpu.HBM)
    res = pl.pallas_call(
        body, name=name, grid=(nb,),
        in_specs=[pl.BlockSpec((qt, NA_W), lambda i: (i, 0))] + blk(1) + blk(2)
                 + [pl.BlockSpec((nctx, NA_W), lambda i: (0, 1)), pl.BlockSpec((nctx, NA_W), lambda i: (0, 2)),
                    pl.BlockSpec(memory_space=pl.ANY)] + [hbm] * nx,
        out_specs=[pl.BlockSpec((qt, NA_W), lambda i: (i, 0))] + [hbm] * nx,
        out_shape=[jax.ShapeDtypeStruct((seq, NA_W), BF16)] + (exchange.out_shapes if exchange else []),
        scratch_shapes=[pltpu.VMEM((wt, NA_W), BF16), pltpu.VMEM((wt, NA_W), BF16),
                        pltpu.VMEM(TAB_SHAPE, F32), pltpu.SemaphoreType.DMA] + (exchange.scratch if exchange else []),
        compiler_params=_params(("arbitrary",)),
    )(qkvu, qkvu, qkvu, qkvu, qkvu, qkvu, qkvu, qkvu_c, qkvu_c, tab, *(exchange.arrays if exchange else []))
    return res[0], list(res[1:])


def attn_bwd(qkvu, qkvu_c, tab, dmix, name, exchange=None):
    seq = qkvu.shape[0]
    nctx = qkvu_c.shape[0]
    rows, nb = _attn_geometry(seq)
    qt = QROWS * GRID_W
    wt = WROWS * GRID_W
    scale = HEAD_DIM ** -0.5
    nx = exchange.n if exchange else 0

    def wb0(i):
        return jnp.clip(i - 1, 0, nb - 3)

    def body(*refs):
        q_ref, k0, k1, k2, v0, v1, v2, kc_ref, vc_ref, do_ref, tab_hbm = refs[:11]
        x_in = refs[11:11 + nx]
        dq_ref, dk_hbm, dv_hbm, dkc_ref, dvc_ref, dtab_hbm = refs[11 + nx:17 + nx]
        x_out = refs[17 + nx:17 + 2 * nx]
        kbuf, vbuf, dkacc, dvacc, tab_s, dtab_s, stage, sem = refs[17 + 2 * nx:25 + 2 * nx]
        x_sems = refs[25 + 2 * nx:]
        i = pl.program_id(0)

        if exchange:
            @pl.when(i == 0)
            def _():
                exchange.start(x_in, x_out, x_sems)

        def flush(src, dst, block, dst_row):
            stage[...] = src[block * qt:(block + 1) * qt, :].astype(BF16)
            cp = pltpu.make_async_copy(stage, dst.at[pl.ds(dst_row, qt)], sem)
            cp.start()
            cp.wait()

        @pl.when(i == 0)
        def _():
            cp = pltpu.make_async_copy(tab_hbm, tab_s, sem)
            cp.start()
            cp.wait()
            dtab_s[...] = jnp.zeros_like(dtab_s)
            dkacc[...] = jnp.zeros_like(dkacc)
            dvacc[...] = jnp.zeros_like(dvacc)
            dkc_ref[...] = jnp.zeros_like(dkc_ref)
            dvc_ref[...] = jnp.zeros_like(dvc_ref)

        @pl.when((i >= 2) & (i <= nb - 2))
        def _():
            dst_row = pl.multiple_of((i - 2) * qt, qt)
            for acc_ref, dst in ((dkacc, dk_hbm), (dvacc, dv_hbm)):
                flush(acc_ref, dst, 0, dst_row)
                acc_ref[0:qt, :] = acc_ref[qt:2 * qt, :]
                acc_ref[qt:2 * qt, :] = acc_ref[2 * qt:3 * qt, :]
                acc_ref[2 * qt:3 * qt, :] = jnp.zeros((qt, NA_W), F32)

        for t, (kr, vr) in enumerate(((k0, v0), (k1, v1), (k2, v2))):
            kbuf[t * qt:(t + 1) * qt, :] = kr[...]
            vbuf[t * qt:(t + 1) * qt, :] = vr[...]
        base = wb0(i) * QROWS

        def row_body(rl, carry):
            r = i * QROWS + rl
            rs = jnp.clip(r - KH // 2, 0, rows - KH)
            vi = rs - r + (KH - 1)
            off = pl.multiple_of((rs - base) * GRID_W, GRID_W)
            qoff = pl.multiple_of(rl * GRID_W, GRID_W)
            for p in range(HEADS // 2):
                ls = slice(p * 128, (p + 1) * 128)
                qst = _stack_heads(q_ref[pl.ds(qoff, GRID_W), ls])
                dost = _stack_heads(do_ref[pl.ds(qoff, GRID_W), ls])
                k2v = kbuf[pl.ds(off, KH * GRID_W), ls]
                v2v = vbuf[pl.ds(off, KH * GRID_W), ls]
                kc2 = kc_ref[:, ls]
                vc2 = vc_ref[:, ls]
                s_w = _dot_nt(qst, k2v) * scale + tab_s[vi, p]
                s_c = _dot_nt(qst, kc2) * scale
                m = jnp.maximum(jnp.max(s_w, axis=-1, keepdims=True), jnp.max(s_c, axis=-1, keepdims=True))
                pw = jnp.exp(s_w - m)
                pc = jnp.exp(s_c - m)
                inv = 1.0 / (jnp.sum(pw, axis=-1, keepdims=True) + jnp.sum(pc, axis=-1, keepdims=True))
                pw = pw * inv
                pc = pc * inv
                dpw = _dot_nt(dost, v2v)
                dpc = _dot_nt(dost, vc2)
                delta = jnp.sum(pw * dpw, axis=-1, keepdims=True) + jnp.sum(pc * dpc, axis=-1, keepdims=True)
                ds_w = pw * (dpw - delta)
                ds_c = pc * (dpc - delta)
                dtab_s[vi, p] += ds_w
                dsw16 = ds_w.astype(BF16)
                dsc16 = ds_c.astype(BF16)
                dq = (_dot(dsw16, k2v) + _dot(dsc16, kc2)) * scale
                dq_ref[pl.ds(qoff, GRID_W), ls] = _unstack_heads(dq).astype(BF16)
                dkacc[pl.ds(off, KH * GRID_W), ls] += _dot_tn(dsw16, qst) * scale
                dvacc[pl.ds(off, KH * GRID_W), ls] += _dot_tn(pw.astype(BF16), dost)
                dkc_ref[:, ls] += _dot_tn(dsc16, qst) * scale
                dvc_ref[:, ls] += _dot_tn(pc.astype(BF16), dost)
            return carry

        lax.fori_loop(0, QROWS, row_body, 0)

        @pl.when(i == nb - 1)
        def _():
            for t in range(3):
                dst_row = (nb - 3 + t) * qt
                flush(dkacc, dk_hbm, t, dst_row)
                flush(dvacc, dv_hbm, t, dst_row)
            cp = pltpu.make_async_copy(dtab_s, dtab_hbm, sem)
            cp.start()
            cp.wait()
            if exchange:
                exchange.finish(x_in, x_out, x_sems)

    blk = lambda col: [pl.BlockSpec((qt, NA_W), functools.partial(lambda i, t, c: (wb0(i) + t, c), t=t, c=col))
                       for t in range(3)]
    any_spec = pl.BlockSpec(memory_space=pl.ANY)
    hbm = pl.BlockSpec(memory_space=pltpu.HBM)
    res = pl.pallas_call(
        body, name=name, grid=(nb,),
        in_specs=[pl.BlockSpec((qt, NA_W), lambda i: (i, 0))] + blk(1) + blk(2)
                 + [pl.BlockSpec((nctx, NA_W), lambda i: (0, 1)), pl.BlockSpec((nctx, NA_W), lambda i: (0, 2)),
                    pl.BlockSpec((qt, NA_W), lambda i: (i, 0)), any_spec] + [hbm] * nx,
        out_specs=[pl.BlockSpec((qt, NA_W), lambda i: (i, 0)), any_spec, any_spec,
                   pl.BlockSpec((nctx, NA_W), lambda i: (0, 0)), pl.BlockSpec((nctx, NA_W), lambda i: (0, 0)),
                   any_spec] + [hbm] * nx,
        out_shape=[jax.ShapeDtypeStruct((seq, NA_W), BF16), jax.ShapeDtypeStruct((seq, NA_W), BF16),
                   jax.ShapeDtypeStruct((seq, NA_W), BF16), jax.ShapeDtypeStruct((nctx, NA_W), F32),
                   jax.ShapeDtypeStruct((nctx, NA_W), F32), jax.ShapeDtypeStruct(TAB_SHAPE, F32)]
                  + (exchange.out_shapes if exchange else []),
        scratch_shapes=[pltpu.VMEM((wt, NA_W), BF16), pltpu.VMEM((wt, NA_W), BF16),
                        pltpu.VMEM((wt, NA_W), F32), pltpu.VMEM((wt, NA_W), F32),
                        pltpu.VMEM(TAB_SHAPE, F32), pltpu.VMEM(TAB_SHAPE, F32), pltpu.VMEM((qt, NA_W), BF16),
                        pltpu.SemaphoreType.DMA]
                       + (exchange.scratch if exchange else []),
        compiler_params=_params(("arbitrary",)),
    )(qkvu, qkvu, qkvu, qkvu, qkvu, qkvu, qkvu, qkvu_c, qkvu_c, dmix, tab, *(exchange.arrays if exchange else []))
    return res[:6], list(res[6:])


def _halo_specs(te, seq, col, width):
    per = te // HALO
    last = seq // HALO - 1
    return [pl.BlockSpec((HALO, width), lambda i: (jnp.maximum(i * per - 1, 0), col)),
            pl.BlockSpec((te, width), lambda i: (i, col)),
            pl.BlockSpec((HALO, width), lambda i: (jnp.minimum((i + 1) * per, last), col))]


def _extended(prev_ref, cur_ref, next_ref, i, te, seq):
    xe = jnp.concatenate([prev_ref[...], cur_ref[...], next_ref[...]], axis=0).astype(F32)
    pos = i * te - HALO + lax.broadcasted_iota(jnp.int32, (te + 2 * HALO, 1), 0)
    return jnp.where((pos >= 0) & (pos < seq), xe, 0.0), pos


def _window_sum(x, levels, n, mirrored):
    first = (n - 1) if mirrored else 1
    acc = x + pltpu.roll(x, first, 0)
    step = 1
    for _ in range(levels - 1):
        acc = pltpu.roll(acc, step, 0) + pltpu.roll(acc, n - step, 0)
        step *= 2
    return acc


def _window_count(pos, w, seq):
    lo = jnp.clip(pos - w // 2, 0, seq)
    hi = jnp.clip(pos - w // 2 + w, 0, seq)
    return jnp.maximum(hi - lo, 1).astype(F32)


def pool_fwd(qkvu, pool_w, pool_scale, name):
    seq = qkvu.shape[0]
    te = _tile(seq, (512, 256))
    n = te + 2 * HALO

    def body(up_ref, uc_ref, un_ref, w_ref, sc_ref, o_ref):
        i = pl.program_id(0)
        xe, pos = _extended(up_ref, uc_ref, un_ref, i, te, seq)
        cnt = pos[HALO:HALO + te]
        for g, w in enumerate(POOL_WINDOWS):
            ls = slice(g * POOL_G, (g + 1) * POOL_G)
            xg = xe[:, ls]
            win = _window_sum(xg, g + 1, n, False)[HALO:HALO + te]
            dlt = win / _window_count(cnt, w, seq) - xg[HALO:HALO + te]
            z = _dot(dlt.astype(BF16), w_ref[g])
            o_ref[:, ls] = (z * sc_ref[:, ls]).astype(BF16)

    return pl.pallas_call(
        body, name=name, grid=(seq // te,),
        in_specs=_halo_specs(te, seq, 3, POOL_W)
                 + [pl.BlockSpec((4, POOL_G, POOL_G), lambda i: (0, 0, 0)), pl.BlockSpec((1, POOL_W), lambda i: (0, 0))],
        out_specs=pl.BlockSpec((te, POOL_W), lambda i: (i, 0)),
        out_shape=jax.ShapeDtypeStruct((seq, POOL_W), BF16),
        compiler_params=_params(("parallel",)),
    )(qkvu, qkvu, qkvu, pool_w, pool_scale)


def pool_bwd(qkvu, dmix, pool_w, pool_scale, name):
    seq = qkvu.shape[0]
    te = _tile(seq, (512, 256))
    n = te + 2 * HALO

    def body(up_ref, uc_ref, un_ref, dp_ref, dc_ref, dn_ref, w_ref, sc_ref, du_ref, dw_ref, dsc_ref):
        i = pl.program_id(0)

        @pl.when(i == 0)
        def _():
            dw_ref[...] = jnp.zeros_like(dw_ref)
            dsc_ref[...] = jnp.zeros_like(dsc_ref)

        xe, pos = _extended(up_ref, uc_ref, un_ref, i, te, seq)
        de, _ = _extended(dp_ref, dc_ref, dn_ref, i, te, seq)
        cpos = pos[HALO:HALO + te]
        for g, w in enumerate(POOL_WINDOWS):
            ls = slice(g * POOL_G, (g + 1) * POOL_G)
            xg = xe[:, ls]
            wg = w_ref[g]
            win = _window_sum(xg, g + 1, n, False)[HALO:HALO + te]
            dlt = (win / _window_count(cpos, w, seq) - xg[HALO:HALO + te]).astype(BF16)
            z = _dot(dlt, wg)
            dpg = de[:, ls]
            dsc_ref[:, ls] += jnp.sum(dpg[HALO:HALO + te] * z, axis=0, keepdims=True)
            dz = (dpg * sc_ref[:, ls]).astype(BF16)
            dw_ref[g] += _dot_tn(dlt, dz[HALO:HALO + te])
            dd = _dot_nt(dz, wg)
            back = _window_sum(dd / _window_count(pos, w, seq), g + 1, n, True)
            du_ref[:, ls] = (back[HALO:HALO + te] - dd[HALO:HALO + te]).astype(BF16)

    return pl.pallas_call(
        body, name=name, grid=(seq // te,),
        in_specs=_halo_specs(te, seq, 3, POOL_W) + _halo_specs(te, seq, 1, POOL_W)
                 + [pl.BlockSpec((4, POOL_G, POOL_G), lambda i: (0, 0, 0)), pl.BlockSpec((1, POOL_W), lambda i: (0, 0))],
        out_specs=[pl.BlockSpec((te, POOL_W), lambda i: (i, 0)),
                   pl.BlockSpec((4, POOL_G, POOL_G), lambda i: (0, 0, 0)), pl.BlockSpec((1, POOL_W), lambda i: (0, 0))],
        out_shape=[jax.ShapeDtypeStruct((seq, POOL_W), BF16), jax.ShapeDtypeStruct((4, POOL_G, POOL_G), F32),
                   jax.ShapeDtypeStruct((1, POOL_W), F32)],
        compiler_params=_params(("arbitrary",)),
    )(qkvu, qkvu, qkvu, dmix, dmix, dmix, pool_w, pool_scale)


def _shifted(z, zprev_row, znext_row, te):
    rows = lax.broadcasted_iota(jnp.int32, (te, 1), 0)
    zp = jnp.where(rows == 0, zprev_row, pltpu.roll(z, 1, 0))
    zn = jnp.where(rows == te - 1, znext_row, pltpu.roll(z, te - 1, 0))
    return zp, zn


def _edge_rows(prev_ref, next_ref, i, nt):
    p = prev_ref[HALO - 1:HALO, :].astype(F32)
    q = next_ref[0:1, :].astype(F32)
    return jnp.where(i == 0, 0.0, p), jnp.where(i == nt - 1, 0.0, q)


def conv_fwd(proj, conv_w, name):
    seq = proj.shape[0]
    te = _tile(seq, (512, 256))
    nt = seq // te

    def body(bg_ref, cp_ref, cc_ref, cn_ref, xp_ref, xc_ref, xn_ref, w_ref, o_ref):
        i = pl.program_id(0)
        z = cc_ref[...].astype(F32) * xc_ref[...].astype(F32)
        cpr, cnr = _edge_rows(cp_ref, cn_ref, i, nt)
        xpr, xnr = _edge_rows(xp_ref, xn_ref, i, nt)
        zp, zn = _shifted(z, cpr * xpr, cnr * xnr, te)
        y = zp * w_ref[0:1, :] + z * w_ref[1:2, :] + zn * w_ref[2:3, :]
        o_ref[...] = (bg_ref[...].astype(F32) * y).astype(BF16)

    return pl.pallas_call(
        body, name=name, grid=(nt,),
        in_specs=[pl.BlockSpec((te, D), lambda i: (i, 0))] + _halo_specs(te, seq, 1, D) + _halo_specs(te, seq, 2, D)
                 + [pl.BlockSpec((3, D), lambda i: (0, 0))],
        out_specs=pl.BlockSpec((te, D), lambda i: (i, 0)),
        out_shape=jax.ShapeDtypeStruct((seq, D), BF16),
        compiler_params=_params(("parallel",)),
    )(proj, proj, proj, proj, proj, proj, proj, conv_w)


def conv_bwd(proj, dgm, conv_w, name):
    seq = proj.shape[0]
    te = _tile(seq, (512, 256))
    nt = seq // te

    def body(bp_ref, bc_ref, bn_ref, cp_ref, cc_ref, cn_ref, xp_ref, xc_ref, xn_ref, gp_ref, gc_ref, gn_ref, w_ref,
             dbg_ref, dcg_ref, dxin_ref, dw_ref):
        i = pl.program_id(0)

        @pl.when(i == 0)
        def _():
            dw_ref[...] = jnp.zeros_like(dw_ref)

        bg = bc_ref[...].astype(F32)
        cg = cc_ref[...].astype(F32)
        xin = xc_ref[...].astype(F32)
        dg = gc_ref[...].astype(F32)
        z = cg * xin
        cpr, cnr = _edge_rows(cp_ref, cn_ref, i, nt)
        xpr, xnr = _edge_rows(xp_ref, xn_ref, i, nt)
        zp, zn = _shifted(z, cpr * xpr, cnr * xnr, te)
        w0, w1, w2 = w_ref[0:1, :], w_ref[1:2, :], w_ref[2:3, :]
        y = zp * w0 + z * w1 + zn * w2
        dbg_ref[...] = (dg * y).astype(BF16)
        dy = dg * bg
        dw_ref[0:1, :] += jnp.sum(dy * zp, axis=0, keepdims=True)
        dw_ref[1:2, :] += jnp.sum(dy * z, axis=0, keepdims=True)
        dw_ref[2:3, :] += jnp.sum(dy * zn, axis=0, keepdims=True)
        bpr, bnr = _edge_rows(bp_ref, bn_ref, i, nt)
        gpr, gnr = _edge_rows(gp_ref, gn_ref, i, nt)
        dyp, dyn = _shifted(dy, bpr * gpr, bnr * gnr, te)
        dz = dyn * w0 + dy * w1 + dyp * w2
        dcg_ref[...] = (dz * xin).astype(BF16)
        dxin_ref[...] = (dz * cg).astype(BF16)

    row = pl.BlockSpec((te, D), lambda i: (i, 0))
    return pl.pallas_call(
        body, name=name, grid=(nt,),
        in_specs=_halo_specs(te, seq, 0, D) + _halo_specs(te, seq, 1, D) + _halo_specs(te, seq, 2, D)
                 + _halo_specs(te, seq, 0, D) + [pl.BlockSpec((3, D), lambda i: (0, 0))],
        out_specs=[row, row, row, pl.BlockSpec((3, D), lambda i: (0, 0))],
        out_shape=[jax.ShapeDtypeStruct((seq, D), BF16)] * 3 + [jax.ShapeDtypeStruct((3, D), F32)],
        compiler_params=_params(("arbitrary",)),
    )(proj, proj, proj, proj, proj, proj, proj, proj, proj, dgm, dgm, dgm, conv_w)


def _position():
    x, y, c = lax.axis_index("x"), lax.axis_index("y"), lax.axis_index("c")
    return x, y, c, 4 * x + 2 * y + c


def _peer(x, y, c, j):
    px = 1 - x if j & 4 else x
    py = 1 - y if j & 2 else y
    pc = 1 - c if j & 1 else c
    return (px, py, pc), 4 * px + 2 * py + pc


def small_allgather(v, name):
    rows, cols = v.shape

    def body(v_ref, o_ref, send_sems, recv_sems, local_sem):
        x, y, c, me = _position()
        mine = pltpu.make_async_copy(v_ref, o_ref.at[me], local_sem)
        mine.start()
        sends = []
        for j in range(1, N_DEV):
            peer, _ = _peer(x, y, c, j)
            cp = pltpu.make_async_remote_copy(src_ref=v_ref, dst_ref=o_ref.at[me], send_sem=send_sems.at[j - 1],
                                              recv_sem=recv_sems.at[j - 1], device_id=peer, device_id_type=MESH_ID)
            cp.start()
            sends.append(cp)
        for j in range(1, N_DEV):
            peer, pid = _peer(x, y, c, j)
            pltpu.make_async_remote_copy(src_ref=v_ref, dst_ref=o_ref.at[pid], send_sem=send_sems.at[j - 1],
                                         recv_sem=recv_sems.at[j - 1], device_id=peer,
                                         device_id_type=MESH_ID).wait_recv()
        for cp in sends:
            cp.wait_send()
        mine.wait()

    return pl.pallas_call(
        body, name=name,
        out_shape=jax.ShapeDtypeStruct((N_DEV, rows, cols), v.dtype),
        in_specs=[pl.BlockSpec(memory_space=pltpu.VMEM)],
        out_specs=pl.BlockSpec(memory_space=pltpu.VMEM),
        scratch_shapes=[pltpu.SemaphoreType.DMA((N_DEV - 1,)), pltpu.SemaphoreType.DMA((N_DEV - 1,)),
                        pltpu.SemaphoreType.DMA],
        compiler_params=pltpu.CompilerParams(vmem_limit_bytes=VMEM_LIMIT),
    )(v)


class Exchange:
    def __init__(self, kind, arrays):
        self.kind, self.arrays, self.n = kind, list(arrays), len(arrays)
        n = self.n
        if kind == "gather":
            self.out_shapes = [jax.ShapeDtypeStruct((N_DEV,) + a.shape, a.dtype) for a in self.arrays]
        else:
            self.out_shapes = [jax.ShapeDtypeStruct(a.shape, a.dtype) for a in self.arrays]
        self.scratch = [pltpu.SemaphoreType.DMA((7 * n,)), pltpu.SemaphoreType.DMA((7 * n,)),
                        pltpu.SemaphoreType.DMA((n,))]

    def _gather_copies(self, ins, outs, sems):
        send_sems, recv_sems, local_sems = sems
        x, y, c, me = _position()
        chips = [(1 - x, y), (x, 1 - y), (1 - x, 1 - y)]

        def blk(k, px, py, pc):
            return outs[k].at[4 * px + 2 * py + pc]

        def copy(k, slot, block, to, src=None):
            return pltpu.make_async_remote_copy(
                src_ref=blk(k, *block) if src is None else src, dst_ref=blk(k, *block),
                send_sem=send_sems.at[k * 7 + slot], recv_sem=recv_sems.at[k * 7 + slot],
                device_id=to, device_id_type=MESH_ID)

        mine = [pltpu.make_async_copy(ins[k], blk(k, x, y, c), local_sems.at[k]) for k in range(self.n)]
        first = []
        for k in range(self.n):
            first.append(copy(k, 0, (x, y, c), (x, y, 1 - c), src=ins[k]))
            first += [copy(k, 1 + j, (x, y, c), (*chip, c), src=ins[k]) for j, chip in enumerate(chips)]
        return (x, y, c), chips, copy, mine, first

    def start(self, ins, outs, sems):
        if self.kind == "gather":
            _, _, _, mine, first = self._gather_copies(ins, outs, sems)
            for cp in mine + first:
                cp.start()
        else:
            for cp in self._scatter_copies(ins, outs, sems, False):
                cp.start()

    def finish(self, ins, outs, sems):
        if self.kind == "gather":
            (x, y, c), chips, copy, mine, first = self._gather_copies(ins, outs, sems)
            passed = []
            for j, chip in enumerate(chips):
                for k in range(self.n):
                    copy(k, 1 + j, (*chip, c), (x, y, c)).wait_recv()
                    cp = copy(k, 4 + j, (*chip, c), (x, y, 1 - c))
                    cp.start()
                    passed.append(cp)
            for k in range(self.n):
                copy(k, 0, (x, y, 1 - c), (x, y, c)).wait_recv()
                for j, chip in enumerate(chips):
                    copy(k, 4 + j, (*chip, 1 - c), (x, y, c)).wait_recv()
            for cp in first + passed:
                cp.wait_send()
            for cp in mine:
                cp.wait()
        else:
            for cp in self._scatter_copies(ins, outs, sems, True):
                cp.wait_recv()
            copies = self._scatter_copies(ins, outs, sems, False)
            for cp in copies[self.n:]:
                cp.wait_send()
            for cp in copies[:self.n]:
                cp.wait()

    def _scatter_copies(self, ins, outs, sems, arrivals):
        send_sems, recv_sems, local_sems = sems
        x, y, c, me = _position()
        out = []
        if not arrivals:
            out = [pltpu.make_async_copy(ins[k].at[me], outs[k].at[me], local_sems.at[k]) for k in range(self.n)]
        for j in range(1, N_DEV):
            peer, pid = _peer(x, y, c, j)
            for k in range(self.n):
                out.append(pltpu.make_async_remote_copy(
                    src_ref=ins[k].at[pid], dst_ref=outs[k].at[pid if arrivals else me],
                    send_sem=send_sems.at[k * 7 + j - 1], recv_sem=recv_sems.at[k * 7 + j - 1],
                    device_id=peer, device_id_type=MESH_ID))
        return out


def sum_devices(v, name):
    _, rows, cols = v.shape

    def body(v_ref, o_ref):
        acc = v_ref[0]
        for p in range(1, N_DEV):
            acc = acc + v_ref[p]
        o_ref[...] = acc

    return pl.pallas_call(
        body, name=name, out_shape=jax.ShapeDtypeStruct((rows, cols), F32),
        compiler_params=pltpu.CompilerParams(vmem_limit_bytes=VMEM_LIMIT),
    )(v)


def _silu(x):
    return x * _sigmoid(x)


def adaln_fwd(cm, mod_w, mod_b_cols, name):
    cols = mod_w.shape[2]

    def body(c_ref, w_ref, b_ref, o_ref):
        o_ref[0] = jnp.dot(_silu(c_ref[...]), w_ref[0], precision=HI, preferred_element_type=F32) + b_ref[0]

    return pl.pallas_call(
        body, name=name, grid=(2,),
        in_specs=[pl.BlockSpec((16, D), lambda l: (0, 0)), pl.BlockSpec((1, D, cols), lambda l: (l, 0, 0)),
                  pl.BlockSpec((1, 1, cols), lambda l: (l, 0, 0))],
        out_specs=pl.BlockSpec((1, 16, cols), lambda l: (l, 0, 0)),
        out_shape=jax.ShapeDtypeStruct((2, 16, cols), F32),
        compiler_params=_params(("parallel",)),
    )(cm, mod_w, mod_b_cols)


def adaln_bwd(cm_t, mod_w, dm_t, name):
    cols = mod_w.shape[2]

    def body(c_ref, w_ref, lat_ref, ctx_ref, gw_ref, pc_ref):
        ctot = jnp.sum(ctx_ref[0], axis=0, keepdims=True)
        rows = lax.broadcasted_iota(jnp.int32, (8, 1), 0)
        g_hi = jnp.where(rows == 0, ctot, 0.0)
        g = jnp.concatenate([lat_ref[0], g_hi], axis=0)
        gw_ref[0] = jnp.dot(_silu(c_ref[...]), g, precision=HI, preferred_element_type=F32)
        pc_ref[0] = lax.dot_general(g_hi, w_ref[0], NT_DIMS, precision=HI, preferred_element_type=F32)

    return pl.pallas_call(
        body, name=name, grid=(2,),
        in_specs=[pl.BlockSpec((D, 16), lambda l: (0, 0)), pl.BlockSpec((1, D, cols), lambda l: (l, 0, 0)),
                  pl.BlockSpec((1, 8, cols), lambda l: (l, 0, 0)), pl.BlockSpec((1, 8, cols), lambda l: (l + 2, 0, 0))],
        out_specs=[pl.BlockSpec((1, D, cols), lambda l: (l, 0, 0)), pl.BlockSpec((1, 8, D), lambda l: (l, 0, 0))],
        out_shape=[jax.ShapeDtypeStruct((2, D, cols), F32), jax.ShapeDtypeStruct((2, 8, D), F32)],
        compiler_params=_params(("parallel",)),
    )(cm_t, mod_w, dm_t, dm_t)


def mod_b_grad(dm_t, name):
    width = dm_t.shape[2]
    tn = width // 8

    def body(d_ref, o_ref):
        s = jnp.concatenate([jnp.sum(d_ref[k], axis=0, keepdims=True) for k in range(4)]
                            + [jnp.zeros((4, tn), F32)], axis=0)
        o_ref[...] = s + pltpu.roll(s, 6, 0)

    return pl.pallas_call(
        body, name=name, grid=(8,),
        in_specs=[pl.BlockSpec((4, 8, tn), lambda j: (0, 0, j))],
        out_specs=pl.BlockSpec((8, tn), lambda j: (0, j)),
        out_shape=jax.ShapeDtypeStruct((8, width), F32),
        compiler_params=_params(("parallel",)),
    )(dm_t)


def adamw(w, m, v, name, g=None, recv=None):
    rows, cols = w.shape
    tr = _tile(rows, (256, 128, 64, 32, 16, 8))
    summed = recv is not None

    def body(w_ref, m_ref, v_ref, g_ref, go_ref, d_ref, mo_ref, vo_ref):
        if summed:
            gv = g_ref[0].astype(F32)
            for p in range(1, N_DEV):
                gv = gv + g_ref[p].astype(F32)
        else:
            gv = g_ref[...]
        mn = ADAM_B1 * m_ref[...] + (1.0 - ADAM_B1) * gv
        vn = ADAM_B2 * v_ref[...] + (1.0 - ADAM_B2) * (gv * gv)
        m_hat = mn / (1.0 - ADAM_B1 ** ADAM_STEP)
        v_hat = vn / (1.0 - ADAM_B2 ** ADAM_STEP)
        go_ref[...] = gv
        d_ref[...] = -ADAM_LR * (m_hat / (jnp.sqrt(v_hat) + ADAM_EPS) + ADAM_WD * w_ref[...])
        mo_ref[...] = mn
        vo_ref[...] = vn

    row = pl.BlockSpec((tr, cols), lambda i: (i, 0))
    gspec = pl.BlockSpec((N_DEV, tr, cols), lambda i: (0, i, 0)) if summed else row
    return pl.pallas_call(
        body, name=name, grid=(rows // tr,),
        in_specs=[row, row, row, gspec], out_specs=[row] * 4,
        out_shape=[jax.ShapeDtypeStruct((rows, cols), F32)] * 4,
        compiler_params=_params(("parallel",)),
    )(w, m, v, recv if summed else g)


def _ffn_fwd(h, hn, mods, w13, w2, base, tag, nxt=None, exchange=None):
    (p, u, s), exchanged = ffn_up(hn, w13, tag + "_up", exchange)
    if callable(w2):
        w2 = w2(exchanged)
    outs = mm_nn([s], w2, [0], tag + "_down", res=(h, mods, base + 2, 0.5), nxt=nxt)
    h_new, y = outs[0], outs[1]
    return h_new, (outs[2] if nxt else None), (h, hn, p, u, s, y), exchanged


COLUMN_CUT = ("w13", "ewi", "cwi")
GATHER_FIRST = ("w13_00",)
GATHER_IN_FFN = ("w2_00", "ewi", "ewo", "w13_01", "w2_01")
GATHER_IN_ATTN = ("w13_10", "w2_10", "cwi", "cwo", "w13_11", "w2_11")
SCATTER_IN_ATTN = ("w13_11", "w2_11", "cwi", "cwo", "w13_10", "w2_10", "w13_01", "w2_01", "ewo")
SCATTER_LAST = ("w13_00", "w2_00", "ewi")


def unpack_piece(p, g):
    if p.split("_")[0] in COLUMN_CUT:
        return g.transpose(1, 0, 2).reshape(g.shape[1], -1)
    return g.reshape(-1, g.shape[2])


def block_piece(p, full):
    if p.split("_")[0] in COLUMN_CUT:
        return full.reshape(full.shape[0], N_DEV, -1).transpose(1, 0, 2).astype(BF16)
    return full.reshape(N_DEV, -1, full.shape[1]).astype(BF16)


def _ffn_bwd(dy, dres, saved, mods, g, w13, w2, base, tag, prev=None, acc=None, exchange_of=None):
    h, hn, p, u, s, _ = saved
    ff = w2.shape[0]
    acc = acc or (None, None, None)
    da, db = mm_nt(dy, w2, tag + "_ds", dswiglu=(u, p))
    dw2 = mm_tn(s, dy, tag + "_dw2", acc=acc[2])
    dwa = mm_tn(hn, da, tag + "_dw13a", acc=acc[0])
    dwb = mm_tn(hn, db, tag + "_dw13b", acc=acc[1])
    exchange = exchange_of(dwa, dwb, dw2) if exchange_of else None
    outs = mm_nt_norm([da, db], w13, [0, ff], h, g, mods, base + 1, dres, tag + "_dhn", prev=prev, exchange=exchange)
    dh, dshift, dscale, dg = outs[:4]
    nprev = 2 if prev else 0
    return (dh, (dwa, dwb, dw2), dg, {base: dshift, base + 1: dscale}, tuple(outs[4:4 + nprev]),
            list(outs[4 + nprev:]))


def _mod_rows(parts):
    zero = jnp.zeros((1, D), F32)
    return jnp.concatenate([parts.get(k, zero) for k in range(N_MOD)], axis=0)


def local_step(x, ctx, ml, mc, wts, target, shards=None):
    wts = dict(wts)
    ng = wts["norm_g"]
    gvec = lambda l, k: ng[l, k][None, :]
    pool_w16 = wts["pool_w"].astype(BF16)
    grads = {}

    def gather(pieces):
        return Exchange("gather", [shards[p] for p in pieces]) if shards else None

    def arrived(pieces, results):
        for p, g in zip(pieces, results):
            wts[p] = unpack_piece(p, g)

    def ffn_grads(lf, f):
        grads["w13_" + lf] = jnp.concatenate([f[0], f[1]], axis=1)
        grads["w2_" + lf] = f[2]

    if shards:
        xh, got = normmod(x, gvec(0, 0), ml[0], 0, 1, "l0f1_norm", gather(GATHER_FIRST))
        arrived(GATHER_FIRST, got)
    else:
        xh = normmod(x, gvec(0, 0), ml[0], 0, 1, "l0f1_norm")
    ch = normmod(ctx, gvec(0, 0), mc[0], 0, 1, "l0f1c_norm")
    def w2_after_up(got):
        arrived(GATHER_IN_FFN, got)
        return wts["w2_00"]

    x1, xn, sv1, _ = _ffn_fwd(x, xh, ml[0], wts["w13_00"], w2_after_up, 0, "l0f1",
                              nxt=(gvec(0, 1), ml[0], 3, 4), exchange=gather(GATHER_IN_FFN))
    c1, cn, sv1c, _ = _ffn_fwd(ctx, ch, mc[0], wts["w13_00"], wts["w2_00"], 0, "l0f1c", nxt=(gvec(0, 1), mc[0], 3, 4))
    qkvu = mm_nn([xn], wts["ewi"], [0], "l0mix_in")
    qkvu_c = mm_nn([cn], wts["ewi"], [0], "l0mix_in_c")
    tab = bias_table(wts["rpb"])
    att, got = attn_fwd(qkvu, qkvu_c, tab, "l0_attn", gather(GATHER_IN_ATTN))
    arrived(GATHER_IN_ATTN, got)
    pool = pool_fwd(qkvu, pool_w16, wts["pool_scale"], "l0_pool")
    x2, ymix, xh = mm_nn([att, pool], wts["ewo"], [0, NA_W], "l0mix_out", res=(x1, ml[0], 5, 1.0),
                         nxt=(gvec(0, 2), ml[0], 6, 7))
    x3, xh, sv2, _ = _ffn_fwd(x2, xh, ml[0], wts["w13_01"], wts["w2_01"], 6, "l0f2", nxt=(gvec(1, 0), ml[1], 0, 1))

    x4, xn1, sv3, _ = _ffn_fwd(x3, xh, ml[1], wts["w13_10"], wts["w2_10"], 0, "l1f1", nxt=(gvec(1, 1), ml[1], 3, 4))
    proj = mm_nn([xn1], wts["cwi"], [0], "l1mix_in")
    gm = conv_fwd(proj, wts["conv_w"], "l1_conv")
    x5, ycv, xh = mm_nn([gm], wts["cwo"], [0], "l1mix_out", res=(x4, ml[1], 5, 1.0), nxt=(gvec(1, 2), ml[1], 6, 7))
    x6, _, sv4, _ = _ffn_fwd(x5, xh, ml[1], wts["w13_11"], wts["w2_11"], 6, "l1f2")

    dx6, loss, dgf, dy, dgate = loss_head(x6, wts["final_g"][None, :], target, (sv4[5], ml[1], 8, 0.5), "loss_head")
    dm1 = {8: dgate}
    dx5, dwf4, dg12, dm_f4, (dy, dgate), _ = _ffn_bwd(dy, dx6, sv4, ml[1], gvec(1, 2), wts["w13_11"], wts["w2_11"], 6,
                                                      "l1f2", prev=(ycv, ml[1], 5, 1.0))
    ffn_grads("11", dwf4)
    dm1.update({5: dgate, **dm_f4})
    dgm = mm_nt(dy, wts["cwo"], "l1mix_dgm")
    grads["cwo"] = mm_tn(gm, dy, "l1mix_dwo")
    dbg, dcg, dxin, dconv_w = conv_bwd(proj, dgm, wts["conv_w"], "l1_conv_bwd")
    grads["cwi"] = jnp.concatenate([mm_tn(xn1, t, "l1mix_dwi%d" % k) for k, t in enumerate((dbg, dcg, dxin))], axis=1)
    dx4, dsh, dsc, dg11, dy, dgate = mm_nt_norm([dbg, dcg, dxin], wts["cwi"], [0, D, 2 * D], x4, gvec(1, 1), ml[1], 4,
                                                dx5, "l1mix_dxn", prev=(sv3[5], ml[1], 2, 0.5))
    dm1.update({3: dsh, 4: dsc, 2: dgate})
    dx3, dwf3, dg10, dm_f3, (dy, dgate), _ = _ffn_bwd(dy, dx4, sv3, ml[1], gvec(1, 0), wts["w13_10"], wts["w2_10"], 0,
                                                      "l1f1", prev=(sv2[5], ml[0], 8, 0.5))
    ffn_grads("10", dwf3)
    dm1.update(dm_f3)
    dm0 = {8: dgate}

    dx2, dwf2, dg02, dm_f2, (dy, dgate), _ = _ffn_bwd(dy, dx3, sv2, ml[0], gvec(0, 2), wts["w13_01"], wts["w2_01"], 6,
                                                      "l0f2", prev=(ymix, ml[0], 5, 1.0))
    ffn_grads("01", dwf2)
    dm0.update({5: dgate, **dm_f2})
    dmix = mm_nt(dy, wts["ewo"], "l0mix_dmix")
    grads["ewo"] = jnp.concatenate([mm_tn(att, dy, "l0mix_dwo_att"), mm_tn(pool, dy, "l0mix_dwo_pool")], axis=0)
    scatter = Exchange("scatter", [block_piece(p, grads.pop(p)) for p in SCATTER_IN_ATTN]) if shards else None
    (dq, dk, dv, dkc, dvc, dtab), got = attn_bwd(qkvu, qkvu_c, tab, dmix, "l0_attn_bwd", scatter)
    recv = dict(zip(SCATTER_IN_ATTN, got))
    du, dpool_w, dpool_scale = pool_bwd(qkvu, dmix, pool_w16, wts["pool_scale"], "l0_pool_bwd")
    drpb = bias_table_bwd(dtab)
    dk16, dv16, dkc16, dvc16 = dk, dv, dkc.astype(BF16), dvc.astype(BF16)
    grads["ewi"] = jnp.concatenate([
        mm_tn(xn, dq, "l0mix_dwi_q"),
        mm_tn(cn, dkc16, "l0mix_dwi_kc", acc=mm_tn(xn, dk16, "l0mix_dwi_k")),
        mm_tn(cn, dvc16, "l0mix_dwi_vc", acc=mm_tn(xn, dv16, "l0mix_dwi_v")),
        mm_tn(xn, du, "l0mix_dwi_u")], axis=1)
    dx1, dsh, dsc, dg01, dy, dgate = mm_nt_norm([dq, dk16, dv16, du], wts["ewi"], [0, NA_W, 2 * NA_W, 3 * NA_W], x1,
                                                gvec(0, 1), ml[0], 4, dx2, "l0mix_dxn", prev=(sv1[5], ml[0], 2, 0.5))
    dm0.update({3: dsh, 4: dsc, 2: dgate})
    dc1, dsh_c, dsc_c, dg01c, dy_c, dgate_c = mm_nt_norm([dkc16, dvc16], wts["ewi"], [NA_W, 2 * NA_W], c1, gvec(0, 1),
                                                         mc[0], 4, None, "l0mix_dxn_c", prev=(sv1c[5], mc[0], 2, 0.5))
    _, dwf1c, dg00c, dm_f1c, _, _ = _ffn_bwd(dy_c, dc1, sv1c, mc[0], gvec(0, 0), wts["w13_00"], wts["w2_00"], 0, "l0f1c")
    dmc0 = {3: dsh_c, 4: dsc_c, 2: dgate_c, **dm_f1c}

    def last_scatter(dwa, dwb, dw2):
        ffn_grads("00", (dwa, dwb, dw2))
        return Exchange("scatter", [block_piece(p, grads.pop(p)) for p in SCATTER_LAST]) if shards else None

    dx0, _, dg00, dm_f1, _, got = _ffn_bwd(dy, dx1, sv1, ml[0], gvec(0, 0), wts["w13_00"], wts["w2_00"], 0, "l0f1",
                                           acc=dwf1c, exchange_of=last_scatter)
    recv.update(zip(SCATTER_LAST, got))
    dm0.update(dm_f1)

    return {
        "loss": loss, "grad_x": dx0,
        "dml": jnp.stack([_mod_rows(dm0), _mod_rows(dm1)]),
        "dmc": jnp.stack([_mod_rows(dmc0), jnp.zeros((N_MOD, D), F32)]),
        "norm_g": jnp.concatenate([dg00 + dg00c, dg01 + dg01c, dg02, dg10, dg11, dg12], axis=0),
        "grads": grads, "recv": recv,
        "rpb": drpb, "pool_w": dpool_w, "pool_scale": dpool_scale, "conv_w": dconv_w, "final_g": dgf,
    }


def _rows_of(v, nrows):
    flat = v.reshape(-1)
    return jnp.pad(flat, (0, nrows * D - flat.shape[0])).reshape(nrows, D)


def kernel(x, c, ctx, c_ctx, mod_w, mod_b, norm_g, ffn_w13, ffn_w2, even_w_in, even_w_out, na_rpb, pool_w, pool_scale, conv_w_in, conv_w, conv_w_out, final_g, loss_target, m_c_ctx, m_mod_w, m_mod_b, m_norm_g, m_ffn_w13, m_ffn_w2, m_even_w_in, m_even_w_out, m_na_rpb, m_pool_w, m_pool_scale, m_conv_w_in, m_conv_w, m_conv_w_out, m_final_g, v_c_ctx, v_mod_w, v_mod_b, v_norm_g, v_ffn_w13, v_ffn_w2, v_even_w_in, v_even_w_out, v_na_rpb, v_pool_w, v_pool_scale, v_conv_w_in, v_conv_w, v_conv_w_out, v_final_g):
    me = 4 * lax.axis_index("x") + 2 * lax.axis_index("y") + lax.axis_index("c")
    ff = ffn_w2.shape[2] * N_DEV
    w13c = ffn_w13.shape[3]
    w2r = ffn_w2.shape[2]
    mcols = mod_w.shape[2]
    gcols = norm_g.shape[2]

    big = {"w13": ffn_w13.reshape(4 * D, w13c), "w2": ffn_w2.reshape(4 * w2r, D), "ewi": even_w_in[0],
           "ewo": even_w_out[0], "cwi": conv_w_in[0], "cwo": conv_w_out[0]}
    names = list(big)
    shards = {"ewi": even_w_in[0].astype(BF16), "ewo": even_w_out[0].astype(BF16),
              "cwi": conv_w_in[0].astype(BF16), "cwo": conv_w_out[0].astype(BF16)}
    for l in range(2):
        for f in range(2):
            shards["w13_%d%d" % (l, f)] = ffn_w13[l, f].astype(BF16)
            shards["w2_%d%d" % (l, f)] = ffn_w2[l, f].astype(BF16)
    wts = {}

    c_all = small_allgather(jnp.pad(c, ((0, 7), (0, 0))), "cond_allgather")[:, 0, :]
    cm = jnp.concatenate([c_all, c_ctx[None, :], jnp.zeros((7, D), F32)], axis=0)
    mod_b_cols = lax.dynamic_slice(mod_b, (0, me * mcols), (2, mcols))[:, None, :]
    m_cols = adaln_fwd(cm, mod_w, mod_b_cols, "adaln_fwd")
    m_all = small_allgather(m_cols.reshape(32, mcols), "mod_allgather")
    m_full = m_all.reshape(N_DEV, 2, 16, mcols).transpose(1, 2, 0, 3).reshape(2, 16, N_MOD * D)
    ml = lax.dynamic_slice(m_full, (0, me, 0), (2, 1, N_MOD * D)).reshape(2, N_MOD, D)
    mc = m_full[:, 8].reshape(2, N_MOD, D)

    full_norm_g = small_allgather(_rows_of(norm_g, 8), "norm_g_allgather")[:, 0, :2 * 3 * gcols]
    full_norm_g = full_norm_g.reshape(N_DEV, 2, 3, gcols).transpose(1, 2, 0, 3).reshape(2, 3, D)
    full_conv_w = small_allgather(_rows_of(conv_w, 8), "conv_w_allgather")[:, 0, :3 * gcols]
    full_conv_w = full_conv_w.reshape(N_DEV, 3, gcols).transpose(1, 0, 2).reshape(3, D)
    wts.update(norm_g=full_norm_g, conv_w=full_conv_w, rpb=na_rpb[0], pool_w=pool_w[0], pool_scale=pool_scale,
               final_g=final_g)
    out = local_step(x[0], ctx[0], ml, mc, wts, loss_target[0], shards)

    dm_pack = jnp.concatenate([out["dml"].reshape(2, N_MOD * D), out["dmc"].reshape(2, N_MOD * D),
                               jnp.zeros((4, N_MOD * D), F32)], axis=0)
    dm_t = small_allgather(dm_pack, "dmod_allgather").transpose(1, 0, 2)[:4]
    dm_cols = lax.dynamic_slice(dm_t, (0, 0, me * mcols), (4, N_DEV, mcols))
    g_mod_w, pc = adaln_bwd(cm.T, mod_w, dm_cols, "adaln_bwd")
    g_mod_b = mod_b_grad(dm_t, "mod_b_grad")[:2]

    pack = jnp.concatenate([_rows_of(t, 8) for t in (
        out["norm_g"], out["conv_w"], out["final_g"], pc[0, :1] + pc[1, :1], out["pool_scale"], out["loss"],
        out["rpb"])] + [_rows_of(out["pool_w"], 64)], axis=0)
    small = sum_devices(small_allgather(pack, "small_grads_allgather"), "small_grads_sum")
    g_norm_g = lax.dynamic_slice(small[0:6].reshape(2, 3, D), (0, 0, me * gcols), (2, 3, gcols))
    g_conv_w = lax.dynamic_slice(small[8:11], (0, me * gcols), (3, gcols))[None]
    g_final_g = small[16]
    sg = _sigmoid(c_ctx)
    g_c_ctx = small[24] * (sg * (1.0 + c_ctx * (1.0 - sg)))
    g_pool_scale = small[32:33, :POOL_W]
    loss = small[40, 0]
    g_rpb = small[48:52].reshape(-1)[:na_rpb.size].reshape(na_rpb.shape)
    g_pool_w = small[56:120].reshape(pool_w.shape)

    pieces = out["recv"]
    lf = ("00", "01", "10", "11")
    recv = {"w13": jnp.concatenate([pieces["w13_" + t] for t in lf], axis=1),
            "w2": jnp.concatenate([pieces["w2_" + t] for t in lf], axis=1),
            "ewi": pieces["ewi"], "ewo": pieces["ewo"], "cwi": pieces["cwi"], "cwo": pieces["cwo"]}

    moments = {"w13": (m_ffn_w13, v_ffn_w13), "w2": (m_ffn_w2, v_ffn_w2), "ewi": (m_even_w_in, v_even_w_in),
               "ewo": (m_even_w_out, v_even_w_out), "cwi": (m_conv_w_in, v_conv_w_in),
               "cwo": (m_conv_w_out, v_conv_w_out)}
    orig = {"w13": ffn_w13, "w2": ffn_w2, "ewi": even_w_in, "ewo": even_w_out, "cwi": conv_w_in, "cwo": conv_w_out}
    upd = {}
    for k in names:
        shp2 = big[k].shape
        res = adamw(big[k], moments[k][0].reshape(shp2), moments[k][1].reshape(shp2), "adamw_" + k, recv=recv[k])
        upd[k] = [r.reshape(orig[k].shape) for r in res]
    shp2 = (2 * D, mcols)
    upd["mod_w"] = [r.reshape(mod_w.shape) for r in adamw(mod_w.reshape(shp2), m_mod_w.reshape(shp2),
                                                          v_mod_w.reshape(shp2), "adamw_mod_w",
                                                          g=g_mod_w.reshape(shp2))]

    smalls = [("c_ctx", c_ctx, m_c_ctx, v_c_ctx, g_c_ctx, 8), ("mod_b", mod_b, m_mod_b, v_mod_b, g_mod_b, 24),
              ("norm_g", norm_g, m_norm_g, v_norm_g, g_norm_g, 8), ("rpb", na_rpb, m_na_rpb, v_na_rpb, g_rpb, 8),
              ("pool_w", pool_w, m_pool_w, v_pool_w, g_pool_w, 64),
              ("pool_scale", pool_scale, m_pool_scale, v_pool_scale, g_pool_scale, 8),
              ("conv_w", conv_w, m_conv_w, v_conv_w, g_conv_w, 8), ("final_g", final_g, m_final_g, v_final_g, g_final_g, 8)]
    packed = [jnp.concatenate([_rows_of(s[col], s[5]) for s in smalls], axis=0) for col in (1, 2, 3, 4)]
    res = adamw(packed[0], packed[1], packed[2], "adamw_small", g=packed[3])
    row = 0
    for name, w, _, _, _, nrows in smalls:
        upd[name] = [r[row:row + nrows].reshape(-1)[:w.size].reshape(w.shape) for r in res]
        row += nrows

    order = ["c_ctx", "mod_w", "mod_b", "norm_g", "w13", "w2", "ewi", "ewo", "rpb", "pool_w", "pool_scale", "cwi",
             "conv_w", "cwo", "final_g"]
    grad_x = out["grad_x"][None]
    return (loss, grad_x, *[upd[k][0] for k in order], *[upd[k][1] for k in order], *[upd[k][2] for k in order],
            *[upd[k][3] for k in order])
```

```python
import functools

import numpy as np
import jax
import jax.numpy as jnp
from jax import lax
from jax.experimental import pallas as pl
from jax.experimental.pallas import tpu as pltpu

D = 1024
FF = 2816
SEQ = 16384
CTX = 256
GRID_W = 64
N_MOD = 9
HEADS = 8
HEAD_DIM = 64
NA_W = 512
POOL_W = 512
POOL_G = 128
POOL_WINDOWS = (2, 4, 8, 16)
KH = 8
KW = 16
RMS_EPS = 1e-6
NEG_INF = -1e30
N_DEV = 8

ADAM_LR = 0.001
ADAM_B1 = 0.9
ADAM_B2 = 0.999
ADAM_EPS = 1e-08
ADAM_WD = 0.01
ADAM_STEP = 10

VMEM_LIMIT = 52 * 1024 * 1024
HALO = 16
QROWS = 8
WROWS = 24

BF16 = jnp.bfloat16
F32 = jnp.float32
MESH_ID = pl.DeviceIdType.MESH
HI = lax.Precision.HIGHEST

NT_DIMS = (((1,), (1,)), ((), ()))
TN_DIMS = (((0,), (0,)), ((), ()))


def _tile(n, cands):
    for c in cands:
        if n % c == 0:
            return c
    return n


def _params(sem):
    return pltpu.CompilerParams(dimension_semantics=sem, vmem_limit_bytes=VMEM_LIMIT)


def _dot(a, b):
    return jnp.dot(a, b, preferred_element_type=F32)


def _dot_nt(a, b):
    return lax.dot_general(a, b, NT_DIMS, preferred_element_type=F32)


def _dot_tn(a, b):
    return lax.dot_general(a, b, TN_DIMS, preferred_element_type=F32)


def _sigmoid(x):
    return 1.0 / (1.0 + jnp.exp(-x))


def normmod(h, g, mods, i_shift, i_scale, name, exchange=None):
    n = h.shape[0]
    te = _tile(n, (512, 256))
    nt = n // te
    nx = exchange.n if exchange else 0

    def body(*refs):
        h_ref, g_ref, m_ref = refs[:3]
        x_in = refs[3:3 + nx]
        o_ref = refs[3 + nx]
        x_out = refs[4 + nx:4 + 2 * nx]
        x_sems = refs[4 + 2 * nx:]
        if exchange:
            @pl.when(pl.program_id(0) == 0)
            def _():
                exchange.start(x_in, x_out, x_sems)

        x = h_ref[...]
        r = lax.rsqrt(jnp.mean(x * x, axis=-1, keepdims=True) + RMS_EPS)
        y = x * r * g_ref[...]
        o_ref[...] = (y * (1.0 + m_ref[i_scale:i_scale + 1, :]) + m_ref[i_shift:i_shift + 1, :]).astype(BF16)

        if exchange:
            @pl.when(pl.program_id(0) == nt - 1)
            def _():
                exchange.finish(x_in, x_out, x_sems)

    hbm = pl.BlockSpec(memory_space=pltpu.HBM)
    res = pl.pallas_call(
        body, name=name, grid=(nt,),
        in_specs=[pl.BlockSpec((te, D), lambda i: (i, 0)),
                  pl.BlockSpec((1, D), lambda i: (0, 0)),
                  pl.BlockSpec((N_MOD, D), lambda i: (0, 0))] + [hbm] * nx,
        out_specs=[pl.BlockSpec((te, D), lambda i: (i, 0))] + [hbm] * nx,
        out_shape=[jax.ShapeDtypeStruct((n, D), BF16)] + (exchange.out_shapes if exchange else []),
        scratch_shapes=exchange.scratch if exchange else [],
        compiler_params=_params(("arbitrary",) if exchange else ("parallel",)),
    )(h, g, mods, *(exchange.arrays if exchange else []))
    return (res[0], list(res[1:])) if exchange else res[0]


def loss_head(x, g, target, prev, name):
    n = x.shape[0]
    te = _tile(n, (256,))
    i_gate, coef = prev[2], prev[3]

    def body(x_ref, g_ref, t_ref, y_ref, m_ref, dx_ref, loss_ref, dg_ref, dy_ref, dgate_ref):
        @pl.when(pl.program_id(0) == 0)
        def _():
            loss_ref[...] = jnp.zeros_like(loss_ref)
            dg_ref[...] = jnp.zeros_like(dg_ref)
            dgate_ref[...] = jnp.zeros_like(dgate_ref)

        xv = x_ref[...]
        gv = g_ref[...]
        r = lax.rsqrt(jnp.mean(xv * xv, axis=-1, keepdims=True) + RMS_EPS)
        xhat = xv * r
        e = xhat * gv - t_ref[...]
        per_tok = jnp.mean(e * e, axis=-1, keepdims=True)
        loss_ref[...] += 0.5 * jnp.sum(per_tok, axis=0, keepdims=True)
        dy = e * (1.0 / D)
        dg_ref[...] += jnp.sum(dy * xhat, axis=0, keepdims=True)
        dxhat = dy * gv
        dx = r * (dxhat - xhat * jnp.mean(dxhat * xhat, axis=-1, keepdims=True))
        dx_ref[...] = dx
        dy_ref[...] = (dx * (coef * m_ref[i_gate:i_gate + 1, :])).astype(BF16)
        dgate_ref[...] += coef * jnp.sum(dx * y_ref[...].astype(F32), axis=0, keepdims=True)

    row = pl.BlockSpec((te, D), lambda i: (i, 0))
    vec = pl.BlockSpec((1, D), lambda i: (0, 0))
    return pl.pallas_call(
        body, name=name, grid=(n // te,),
        in_specs=[row, vec, row, row, pl.BlockSpec((N_MOD, D), lambda i: (0, 0))],
        out_specs=[row, pl.BlockSpec((1, 128), lambda i: (0, 0)), vec, row, vec],
        out_shape=[jax.ShapeDtypeStruct((n, D), F32), jax.ShapeDtypeStruct((1, 128), F32),
                   jax.ShapeDtypeStruct((1, D), F32), jax.ShapeDtypeStruct((n, D), BF16),
                   jax.ShapeDtypeStruct((1, D), F32)],
        compiler_params=_params(("arbitrary",)),
    )(x, g, target, prev[0], prev[1])


def ffn_up(hn, w13, name, exchange=None):
    n = hn.shape[0]
    ff = w13.shape[1] // 2
    tm = _tile(n, (512, 256))
    tn = _tile(ff, (1408, 512, 256, 128))
    nj = ff // tn
    ni = n // tm
    nx = exchange.n if exchange else 0

    def body(*refs):
        h_ref, wa_ref, wb_ref = refs[:3]
        x_in = refs[3:3 + nx]
        p_ref, u_ref, s_ref = refs[3 + nx:6 + nx]
        x_out = refs[6 + nx:6 + 2 * nx]
        x_sems = refs[6 + 2 * nx:]
        if exchange:
            @pl.when((pl.program_id(0) == 0) & (pl.program_id(1) == 0))
            def _():
                exchange.start(x_in, x_out, x_sems)

        hv = h_ref[...]
        a = _dot(hv, wa_ref[...])
        b = _dot(hv, wb_ref[...])
        sig = _sigmoid(a)
        p = a * sig
        p_ref[...] = p.astype(BF16)
        u_ref[...] = (b * (sig * (1.0 + a * (1.0 - sig)))).astype(BF16)
        s_ref[...] = (p * b).astype(BF16)

        if exchange:
            @pl.when((pl.program_id(0) == nj - 1) & (pl.program_id(1) == ni - 1))
            def _():
                exchange.finish(x_in, x_out, x_sems)

    out = pl.BlockSpec((tm, tn), lambda j, i: (i, j))
    hbm = pl.BlockSpec(memory_space=pltpu.HBM)
    sem = ("arbitrary", "arbitrary") if exchange else ("parallel", "parallel")
    res = pl.pallas_call(
        body, name=name, grid=(nj, ni),
        in_specs=[pl.BlockSpec((tm, D), lambda j, i: (i, 0)),
                  pl.BlockSpec((D, tn), lambda j, i: (0, j)),
                  pl.BlockSpec((D, tn), lambda j, i: (0, j + nj))] + [hbm] * nx,
        out_specs=[out, out, out] + [hbm] * nx,
        out_shape=[jax.ShapeDtypeStruct((n, ff), BF16)] * 3 + (exchange.out_shapes if exchange else []),
        scratch_shapes=exchange.scratch if exchange else [],
        compiler_params=_params(sem),
    )(hn, w13, w13, *(exchange.arrays if exchange else []))
    return res[:3], list(res[3:])


def mm_nn(a_list, w, row_offs, name, out_dtype=BF16, res=None, nxt=None):
    n = a_list[0].shape[0]
    nout = w.shape[1]
    ks = [a.shape[1] for a in a_list]
    tm = _tile(n, (512, 256) if res is not None else (1024, 512, 256))
    tn = _tile(nout, (1024, 512, 256, 128))
    na = len(a_list)
    assert nxt is None or (res is not None and tn == D)

    def body(*refs):
        a_refs = refs[:na]
        w_refs = refs[na:2 * na]
        acc = _dot(a_refs[0][...], w_refs[0][...])
        for k in range(1, na):
            acc += _dot(a_refs[k][...], w_refs[k][...])
        if res is None:
            refs[2 * na][...] = acc.astype(out_dtype)
        else:
            h_ref, m_ref = refs[2 * na:2 * na + 2]
            i_gate, coef = res[2], res[3]
            h_new = h_ref[...] + (coef * m_ref[i_gate:i_gate + 1, :]) * acc
            if nxt is None:
                hn_ref, y_ref = refs[2 * na + 2:]
            else:
                g2_ref, m2_ref, hn_ref, y_ref, nx_ref = refs[2 * na + 2:]
                r = lax.rsqrt(jnp.mean(h_new * h_new, axis=-1, keepdims=True) + RMS_EPS)
                nx_ref[...] = ((h_new * r * g2_ref[...]) * (1.0 + m2_ref[nxt[3]:nxt[3] + 1, :])
                               + m2_ref[nxt[2]:nxt[2] + 1, :]).astype(BF16)
            hn_ref[...] = h_new
            y_ref[...] = acc.astype(BF16)

    in_specs = [pl.BlockSpec((tm, k), lambda j, i: (i, 0)) for k in ks]
    for k, off in zip(ks, row_offs):
        in_specs.append(pl.BlockSpec((k, tn), functools.partial(lambda j, i, ob: (ob, j), ob=off // k)))
    args = list(a_list) + [w] * na
    out = pl.BlockSpec((tm, tn), lambda j, i: (i, j))
    if res is None:
        out_specs = out
        out_shape = jax.ShapeDtypeStruct((n, nout), out_dtype)
    else:
        in_specs += [out, pl.BlockSpec((N_MOD, tn), lambda j, i: (0, j))]
        args += [res[0], res[1]]
        out_specs = [out, out]
        out_shape = [jax.ShapeDtypeStruct((n, nout), F32), jax.ShapeDtypeStruct((n, nout), BF16)]
        if nxt is not None:
            in_specs += [pl.BlockSpec((1, D), lambda j, i: (0, 0)), pl.BlockSpec((N_MOD, D), lambda j, i: (0, 0))]
            args += [nxt[0], nxt[1]]
            out_specs.append(out)
            out_shape.append(jax.ShapeDtypeStruct((n, nout), BF16))
    return pl.pallas_call(
        body, name=name, grid=(nout // tn, n // tm),
        in_specs=in_specs, out_specs=out_specs, out_shape=out_shape,
        compiler_params=_params(("parallel", "parallel")),
    )(*args)


def mm_nt(g, w, name, dswiglu=None):
    n, kg = g.shape
    nout = w.shape[0]
    tm = _tile(n, (512, 256) if dswiglu is not None else (1024, 512, 256))
    tn = _tile(nout, (1408, 1024, 512, 256, 128))

    def body(*refs):
        r = _dot_nt(refs[0][...], refs[1][...])
        if dswiglu is None:
            refs[2][...] = r.astype(BF16)
        else:
            u_ref, p_ref, da_ref, db_ref = refs[2:]
            da_ref[...] = (r * u_ref[...].astype(F32)).astype(BF16)
            db_ref[...] = (r * p_ref[...].astype(F32)).astype(BF16)

    out = pl.BlockSpec((tm, tn), lambda j, i: (i, j))
    in_specs = [pl.BlockSpec((tm, kg), lambda j, i: (i, 0)), pl.BlockSpec((tn, kg), lambda j, i: (j, 0))]
    args = [g, w]
    if dswiglu is None:
        out_specs = out
        out_shape = jax.ShapeDtypeStruct((n, nout), BF16)
    else:
        in_specs += [out, out]
        args += list(dswiglu)
        out_specs = [out, out]
        out_shape = [jax.ShapeDtypeStruct((n, nout), BF16)] * 2
    return pl.pallas_call(
        body, name=name, grid=(nout // tn, n // tm),
        in_specs=in_specs, out_specs=out_specs, out_shape=out_shape,
        compiler_params=_params(("parallel", "parallel")),
    )(*args)


def mm_nt_norm(g_list, w, col_offs, h, g, mods, i_scale, dres, name, prev=None, exchange=None):
    n = h.shape[0]
    kg = g_list[0].shape[1]
    tm = _tile(n, (512, 256))
    tk = _tile(kg, (1408, 1024, 512, 256, 128))
    ng = len(g_list)
    nk = kg // tk
    ni = n // tm
    has_res = dres is not None
    nx = exchange.n if exchange else 0

    def body(*refs):
        g_refs = refs[:ng]
        w_refs = refs[ng:2 * ng]
        pos = 2 * ng
        h_ref, gv_ref, m_ref = refs[pos:pos + 3]
        pos += 3
        if has_res:
            dres_ref = refs[pos]
            pos += 1
        if prev is not None:
            y_ref, mp_ref = refs[pos:pos + 2]
            pos += 2
        x_in = refs[pos:pos + nx]
        pos += nx
        dh_ref, dshift_ref, dscale_ref, dg_ref = refs[pos:pos + 4]
        pos += 4
        if prev is not None:
            dy_ref, dgate_ref = refs[pos:pos + 2]
            pos += 2
        x_out = refs[pos:pos + nx]
        pos += nx
        acc_ref = refs[pos]
        x_sems = refs[pos + 1:]
        i = pl.program_id(0)
        k = pl.program_id(1)

        @pl.when((i == 0) & (k == 0))
        def _():
            if exchange:
                exchange.start(x_in, x_out, x_sems)
            dshift_ref[...] = jnp.zeros_like(dshift_ref)
            dscale_ref[...] = jnp.zeros_like(dscale_ref)
            dg_ref[...] = jnp.zeros_like(dg_ref)
            if prev is not None:
                dgate_ref[...] = jnp.zeros_like(dgate_ref)

        @pl.when(k == 0)
        def _():
            acc_ref[...] = jnp.zeros_like(acc_ref)

        acc = _dot_nt(g_refs[0][...], w_refs[0][...])
        for q in range(1, ng):
            acc += _dot_nt(g_refs[q][...], w_refs[q][...])
        acc_ref[...] += acc

        @pl.when(k == nk - 1)
        def _():
            d = acc_ref[...]
            x = h_ref[...]
            gv = gv_ref[...]
            r = lax.rsqrt(jnp.mean(x * x, axis=-1, keepdims=True) + RMS_EPS)
            xhat = x * r
            one_scale = 1.0 + m_ref[i_scale:i_scale + 1, :]
            t = d * xhat
            tsum = jnp.sum(t, axis=0, keepdims=True)
            dshift_ref[...] += jnp.sum(d, axis=0, keepdims=True)
            dscale_ref[...] += gv * tsum
            dg_ref[...] += one_scale * tsum
            cvec = one_scale * gv
            dh = r * (d * cvec - xhat * jnp.mean(t * cvec, axis=-1, keepdims=True))
            if has_res:
                dh = dh + dres_ref[...]
            dh_ref[...] = dh
            if prev is not None:
                i_gate, coef = prev[2], prev[3]
                dy_ref[...] = (dh * (coef * mp_ref[i_gate:i_gate + 1, :])).astype(BF16)
                dgate_ref[...] += coef * jnp.sum(dh * y_ref[...].astype(F32), axis=0, keepdims=True)

        if exchange:
            @pl.when((i == ni - 1) & (k == nk - 1))
            def _():
                exchange.finish(x_in, x_out, x_sems)

    row = pl.BlockSpec((tm, D), lambda i, k: (i, 0))
    vec = pl.BlockSpec((1, D), lambda i, k: (0, 0))
    modspec = pl.BlockSpec((N_MOD, D), lambda i, k: (0, 0))
    in_specs = [pl.BlockSpec((tm, tk), lambda i, k: (i, k)) for _ in g_list]
    for off in col_offs:
        in_specs.append(pl.BlockSpec((D, tk), functools.partial(lambda i, k, ob: (0, ob + k), ob=off // tk)))
    in_specs += [row, vec, modspec]
    args = list(g_list) + [w] * ng + [h, g, mods]
    out_specs = [row, vec, vec, vec]
    out_shape = [jax.ShapeDtypeStruct((n, D), F32)] + [jax.ShapeDtypeStruct((1, D), F32)] * 3
    if has_res:
        in_specs.append(row)
        args.append(dres)
    if prev is not None:
        in_specs += [row, modspec]
        args += [prev[0], prev[1]]
        out_specs += [row, vec]
        out_shape += [jax.ShapeDtypeStruct((n, D), BF16), jax.ShapeDtypeStruct((1, D), F32)]
    scratch = [pltpu.VMEM((tm, D), F32)]
    if exchange:
        hbm = pl.BlockSpec(memory_space=pltpu.HBM)
        in_specs += [hbm] * nx
        args += exchange.arrays
        out_specs += [hbm] * nx
        out_shape += exchange.out_shapes
        scratch += exchange.scratch
    return pl.pallas_call(
        body, name=name, grid=(ni, nk),
        in_specs=in_specs, out_specs=out_specs, out_shape=out_shape, scratch_shapes=scratch,
        compiler_params=_params(("arbitrary", "arbitrary")),
    )(*args)


def mm_tn(a, g, name, acc=None):
    n, ka = a.shape
    ngc = g.shape[1]
    tka = _tile(ka, (1408, 1024, 512, 256, 128))
    tng = _tile(ngc, (1408, 1024, 512, 256, 128))
    tr = _tile(n, (2048, 1024, 512, 256))
    has_acc = acc is not None

    def body(*refs):
        a_ref, g_ref = refs[0], refs[1]
        o_ref = refs[-1]
        r = pl.program_id(2)

        @pl.when(r == 0)
        def _():
            if has_acc:
                o_ref[...] = refs[2][...]
            else:
                o_ref[...] = jnp.zeros_like(o_ref)

        o_ref[...] += _dot_tn(a_ref[...], g_ref[...])

    out = pl.BlockSpec((tka, tng), lambda p, q, r: (p, q))
    in_specs = [pl.BlockSpec((tr, tka), lambda p, q, r: (r, p)),
                pl.BlockSpec((tr, tng), lambda p, q, r: (r, q))]
    args = [a, g]
    if has_acc:
        in_specs.append(out)
        args.append(acc)
    return pl.pallas_call(
        body, name=name, grid=(ka // tka, ngc // tng, n // tr),
        in_specs=in_specs, out_specs=out,
        out_shape=jax.ShapeDtypeStruct((ka, ngc), F32),
        compiler_params=_params(("parallel", "parallel", "arbitrary")),
    )(*args)


def mm_small(a, b, name, trans_b=False):
    m = a.shape[0]
    nout = b.shape[0] if trans_b else b.shape[1]

    def body(a_ref, b_ref, o_ref):
        if trans_b:
            o_ref[...] = lax.dot_general(a_ref[...], b_ref[...], NT_DIMS, precision=HI, preferred_element_type=F32)
        else:
            o_ref[...] = jnp.dot(a_ref[...], b_ref[...], precision=HI, preferred_element_type=F32)

    return pl.pallas_call(
        body, name=name,
        out_shape=jax.ShapeDtypeStruct((m, nout), F32),
        compiler_params=pltpu.CompilerParams(vmem_limit_bytes=VMEM_LIMIT),
    )(a, b)


def _col_tables():
    col = np.arange(GRID_W)
    start = np.clip(col - KW // 2, 0, GRID_W - KW)
    ok = (col[None, :] >= start[:, None]) & (col[None, :] < start[:, None] + KW)
    ci = np.clip(col[None, :] - col[:, None] + (KW - 1), 0, 2 * KW - 2)
    e = np.zeros((2 * KW - 1, GRID_W, GRID_W), np.float32)
    for c in range(2 * KW - 1):
        e[c] = (ci == c) & ok
    return e.reshape(2 * KW - 1, GRID_W * GRID_W), ok


def bias_table(rpb):
    e, ok = _col_tables()
    e_pad = np.zeros((32, GRID_W * GRID_W), np.float32)
    e_pad[:31] = e
    rp = jnp.pad(rpb.reshape(HEADS * 15, 31), ((0, 0), (0, 1)))
    t = mm_small(rp, jnp.asarray(e_pad), "rpb_expand").reshape(HEADS, 15, GRID_W, GRID_W)
    t = jnp.where(jnp.asarray(ok)[None, None], t, NEG_INF)
    tab = jnp.stack([t[:, v:v + KH] for v in range(8)], axis=0)
    return tab.transpose(0, 1, 3, 2, 4).reshape(TAB_SHAPE)


def bias_table_bwd(dtab):
    e, _ = _col_tables()
    e_pad = np.zeros((128, GRID_W * GRID_W), np.float32)
    e_pad[:31] = e
    d = dtab.reshape(8, HEADS, GRID_W, KH, GRID_W).transpose(0, 1, 3, 2, 4).reshape(8 * HEADS * KH, GRID_W * GRID_W)
    gv = mm_small(d, jnp.asarray(e_pad), "rpb_reduce", trans_b=True)[:, :31]
    gv = gv.reshape(8, HEADS, KH, 31).transpose(0, 2, 1, 3).reshape(8 * KH, HEADS * 31)
    sel = np.zeros((16, 8 * KH), np.float32)
    for v in range(8):
        for j in range(KH):
            sel[v + j, v * KH + j] = 1.0
    gpad = jnp.pad(gv, ((0, 0), (0, 256 - HEADS * 31)))
    out = mm_small(jnp.asarray(sel), gpad, "rpb_fold")[:15, :HEADS * 31]
    return out.reshape(15, HEADS, 31).transpose(1, 0, 2)


def _attn_geometry(seq):
    rows = seq // GRID_W
    nb = rows // QROWS
    return rows, nb


def _stack_heads(t2):
    first = (lax.broadcasted_iota(jnp.int32, (1, 128), 1) // HEAD_DIM) == 0
    zero = jnp.zeros_like(t2)
    return jnp.concatenate([jnp.where(first, t2, zero), jnp.where(first, zero, t2)], axis=0)


def _unstack_heads(t):
    first = (lax.broadcasted_iota(jnp.int32, (1, 128), 1) // HEAD_DIM) == 0
    return jnp.where(first, t[0:GRID_W], t[GRID_W:2 * GRID_W])


TAB_SHAPE = (8, HEADS // 2, 2 * GRID_W, KH * GRID_W)


def attn_fwd(qkvu, qkvu_c, tab, name, exchange=None):
    seq = qkvu.shape[0]
    nctx = qkvu_c.shape[0]
    rows, nb = _attn_geometry(seq)
    qt = QROWS * GRID_W
    wt = WROWS * GRID_W
    scale = HEAD_DIM ** -0.5
    nx = exchange.n if exchange else 0

    def wb0(i):
        return jnp.clip(i - 1, 0, nb - 3)

    def body(*refs):
        q_ref, k0, k1, k2, v0, v1, v2, kc_ref, vc_ref, tab_hbm = refs[:10]
        x_in = refs[10:10 + nx]
        o_ref = refs[10 + nx]
        x_out = refs[11 + nx:11 + 2 * nx]
        kbuf, vbuf, tab_s, sem = refs[11 + 2 * nx:15 + 2 * nx]
        x_sems = refs[15 + 2 * nx:]
        i = pl.program_id(0)

        @pl.when(i == 0)
        def _():
            if exchange:
                exchange.start(x_in, x_out, x_sems)
            cp = pltpu.make_async_copy(tab_hbm, tab_s, sem)
            cp.start()
            cp.wait()

        for t, (kr, vr) in enumerate(((k0, v0), (k1, v1), (k2, v2))):
            kbuf[t * qt:(t + 1) * qt, :] = kr[...]
            vbuf[t * qt:(t + 1) * qt, :] = vr[...]
        base = wb0(i) * QROWS

        def row_body(rl, carry):
            r = i * QROWS + rl
            rs = jnp.clip(r - KH // 2, 0, rows - KH)
            vi = rs - r + (KH - 1)
            off = pl.multiple_of((rs - base) * GRID_W, GRID_W)
            qoff = pl.multiple_of(rl * GRID_W, GRID_W)
            for p in range(HEADS // 2):
                ls = slice(p * 128, (p + 1) * 128)
                qst = _stack_heads(q_ref[pl.ds(qoff, GRID_W), ls])
                k2v = kbuf[pl.ds(off, KH * GRID_W), ls]
                v2v = vbuf[pl.ds(off, KH * GRID_W), ls]
                s_w = _dot_nt(qst, k2v) * scale + tab_s[vi, p]
                s_c = _dot_nt(qst, kc_ref[:, ls]) * scale
                m = jnp.maximum(jnp.max(s_w, axis=-1, keepdims=True), jnp.max(s_c, axis=-1, keepdims=True))
                pw = jnp.exp(s_w - m)
                pc = jnp.exp(s_c - m)
                l = jnp.sum(pw, axis=-1, keepdims=True) + jnp.sum(pc, axis=-1, keepdims=True)
                o = _dot(pw.astype(BF16), v2v) + _dot(pc.astype(BF16), vc_ref[:, ls])
                o_ref[pl.ds(qoff, GRID_W), ls] = _unstack_heads(o * (1.0 / l)).astype(BF16)
            return carry

        lax.fori_loop(0, QROWS, row_body, 0)

        if exchange:
            @pl.when(i == nb - 1)
            def _():
                exchange.finish(x_in, x_out, x_sems)

    blk = lambda col: [pl.BlockSpec((qt, NA_W), functools.partial(lambda i, t, c: (wb0(i) + t, c), t=t, c=col))
                       for t in range(3)]
    hbm = pl.BlockSpec(memory_space=pltpu.HBM)
    res = pl.pallas_call(
        body, name=name, grid=(nb,),
        in_specs=[pl.BlockSpec((qt, NA_W), lambda i: (i, 0))] + blk(1) + blk(2)
                 + [pl.BlockSpec((nctx, NA_W), lambda i: (0, 1)), pl.BlockSpec((nctx, NA_W), lambda i: (0, 2)),
                    pl.BlockSpec(memory_space=pl.ANY)] + [hbm] * nx,
        out_specs=[pl.BlockSpec((qt, NA_W), lambda i: (i, 0))] + [hbm] * nx,
        out_shape=[jax.ShapeDtypeStruct((seq, NA_W), BF16)] + (exchange.out_shapes if exchange else []),
        scratch_shapes=[pltpu.VMEM((wt, NA_W), BF16), pltpu.VMEM((wt, NA_W), BF16),
                        pltpu.VMEM(TAB_SHAPE, F32), pltpu.SemaphoreType.DMA] + (exchange.scratch if exchange else []),
        compiler_params=_params(("arbitrary",)),
    )(qkvu, qkvu, qkvu, qkvu, qkvu, qkvu, qkvu, qkvu_c, qkvu_c, tab, *(exchange.arrays if exchange else []))
    return res[0], list(res[1:])


def attn_bwd(qkvu, qkvu_c, tab, dmix, name, exchange=None):
    seq = qkvu.shape[0]
    nctx = qkvu_c.shape[0]
    rows, nb = _attn_geometry(seq)
    qt = QROWS * GRID_W
    wt = WROWS * GRID_W
    scale = HEAD_DIM ** -0.5
    nx = exchange.n if exchange else 0

    def wb0(i):
        return jnp.clip(i - 1, 0, nb - 3)

    def body(*refs):
        q_ref, k0, k1, k2, v0, v1, v2, kc_ref, vc_ref, do_ref, tab_hbm = refs[:11]
        x_in = refs[11:11 + nx]
        dq_ref, dk_hbm, dv_hbm, dkc_ref, dvc_ref, dtab_hbm = refs[11 + nx:17 + nx]
        x_out = refs[17 + nx:17 + 2 * nx]
        kbuf, vbuf, dkacc, dvacc, tab_s, dtab_s, stage, sem = refs[17 + 2 * nx:25 + 2 * nx]
        x_sems = refs[25 + 2 * nx:]
        i = pl.program_id(0)

        if exchange:
            @pl.when(i == 0)
            def _():
                exchange.start(x_in, x_out, x_sems)

        def flush(src, dst, block, dst_row):
            stage[...] = src[block * qt:(block + 1) * qt, :].astype(BF16)
            cp = pltpu.make_async_copy(stage, dst.at[pl.ds(dst_row, qt)], sem)
            cp.start()
            cp.wait()

        @pl.when(i == 0)
        def _():
            cp = pltpu.make_async_copy(tab_hbm, tab_s, sem)
            cp.start()
            cp.wait()
            dtab_s[...] = jnp.zeros_like(dtab_s)
            dkacc[...] = jnp.zeros_like(dkacc)
            dvacc[...] = jnp.zeros_like(dvacc)
            dkc_ref[...] = jnp.zeros_like(dkc_ref)
            dvc_ref[...] = jnp.zeros_like(dvc_ref)

        @pl.when((i >= 2) & (i <= nb - 2))
        def _():
            dst_row = pl.multiple_of((i - 2) * qt, qt)
            for acc_ref, dst in ((dkacc, dk_hbm), (dvacc, dv_hbm)):
                flush(acc_ref, dst, 0, dst_row)
                acc_ref[0:qt, :] = acc_ref[qt:2 * qt, :]
                acc_ref[qt:2 * qt, :] = acc_ref[2 * qt:3 * qt, :]
                acc_ref[2 * qt:3 * qt, :] = jnp.zeros((qt, NA_W), F32)

        for t, (kr, vr) in enumerate(((k0, v0), (k1, v1), (k2, v2))):
            kbuf[t * qt:(t + 1) * qt, :] = kr[...]
            vbuf[t * qt:(t + 1) * qt, :] = vr[...]
        base = wb0(i) * QROWS

        def row_body(rl, carry):
            r = i * QROWS + rl
            rs = jnp.clip(r - KH // 2, 0, rows - KH)
            vi = rs - r + (KH - 1)
            off = pl.multiple_of((rs - base) * GRID_W, GRID_W)
            qoff = pl.multiple_of(rl * GRID_W, GRID_W)
            for p in range(HEADS // 2):
                ls = slice(p * 128, (p + 1) * 128)
                qst = _stack_heads(q_ref[pl.ds(qoff, GRID_W), ls])
                dost = _stack_heads(do_ref[pl.ds(qoff, GRID_W), ls])
                k2v = kbuf[pl.ds(off, KH * GRID_W), ls]
                v2v = vbuf[pl.ds(off, KH * GRID_W), ls]
                kc2 = kc_ref[:, ls]
                vc2 = vc_ref[:, ls]
                s_w = _dot_nt(qst, k2v) * scale + tab_s[vi, p]
                s_c = _dot_nt(qst, kc2) * scale
                m = jnp.maximum(jnp.max(s_w, axis=-1, keepdims=True), jnp.max(s_c, axis=-1, keepdims=True))
                pw = jnp.exp(s_w - m)
                pc = jnp.exp(s_c - m)
                inv = 1.0 / (jnp.sum(pw, axis=-1, keepdims=True) + jnp.sum(pc, axis=-1, keepdims=True))
                pw = pw * inv
                pc = pc * inv
                dpw = _dot_nt(dost, v2v)
                dpc = _dot_nt(dost, vc2)
                delta = jnp.sum(pw * dpw, axis=-1, keepdims=True) + jnp.sum(pc * dpc, axis=-1, keepdims=True)
                ds_w = pw * (dpw - delta)
                ds_c = pc * (dpc - delta)
                dtab_s[vi, p] += ds_w
                dsw16 = ds_w.astype(BF16)
                dsc16 = ds_c.astype(BF16)
                dq = (_dot(dsw16, k2v) + _dot(dsc16, kc2)) * scale
                dq_ref[pl.ds(qoff, GRID_W), ls] = _unstack_heads(dq).astype(BF16)
                dkacc[pl.ds(off, KH * GRID_W), ls] += _dot_tn(dsw16, qst) * scale
                dvacc[pl.ds(off, KH * GRID_W), ls] += _dot_tn(pw.astype(BF16), dost)
                dkc_ref[:, ls] += _dot_tn(dsc16, qst) * scale
                dvc_ref[:, ls] += _dot_tn(pc.astype(BF16), dost)
            return carry

        lax.fori_loop(0, QROWS, row_body, 0)

        @pl.when(i == nb - 1)
        def _():
            for t in range(3):
                dst_row = (nb - 3 + t) * qt
                flush(dkacc, dk_hbm, t, dst_row)
                flush(dvacc, dv_hbm, t, dst_row)
            cp = pltpu.make_async_copy(dtab_s, dtab_hbm, sem)
            cp.start()
            cp.wait()
            if exchange:
                exchange.finish(x_in, x_out, x_sems)

    blk = lambda col: [pl.BlockSpec((qt, NA_W), functools.partial(lambda i, t, c: (wb0(i) + t, c), t=t, c=col))
                       for t in range(3)]
    any_spec = pl.BlockSpec(memory_space=pl.ANY)
    hbm = pl.BlockSpec(memory_space=pltpu.HBM)
    res = pl.pallas_call(
        body, name=name, grid=(nb,),
        in_specs=[pl.BlockSpec((qt, NA_W), lambda i: (i, 0))] + blk(1) + blk(2)
                 + [pl.BlockSpec((nctx, NA_W), lambda i: (0, 1)), pl.BlockSpec((nctx, NA_W), lambda i: (0, 2)),
                    pl.BlockSpec((qt, NA_W), lambda i: (i, 0)), any_spec] + [hbm] * nx,
        out_specs=[pl.BlockSpec((qt, NA_W), lambda i: (i, 0)), any_spec, any_spec,
                   pl.BlockSpec((nctx, NA_W), lambda i: (0, 0)), pl.BlockSpec((nctx, NA_W), lambda i: (0, 0)),
                   any_spec] + [hbm] * nx,
        out_shape=[jax.ShapeDtypeStruct((seq, NA_W), BF16), jax.ShapeDtypeStruct((seq, NA_W), BF16),
                   jax.ShapeDtypeStruct((seq, NA_W), BF16), jax.ShapeDtypeStruct((nctx, NA_W), F32),
                   jax.ShapeDtypeStruct((nctx, NA_W), F32), jax.ShapeDtypeStruct(TAB_SHAPE, F32)]
                  + (exchange.out_shapes if exchange else []),
        scratch_shapes=[pltpu.VMEM((wt, NA_W), BF16), pltpu.VMEM((wt, NA_W), BF16),
                        pltpu.VMEM((wt, NA_W), F32), pltpu.VMEM((wt, NA_W), F32),
                        pltpu.VMEM(TAB_SHAPE, F32), pltpu.VMEM(TAB_SHAPE, F32), pltpu.VMEM((qt, NA_W), BF16),
                        pltpu.SemaphoreType.DMA]
                       + (exchange.scratch if exchange else []),
        compiler_params=_params(("arbitrary",)),
    )(qkvu, qkvu, qkvu, qkvu, qkvu, qkvu, qkvu, qkvu_c, qkvu_c, dmix, tab, *(exchange.arrays if exchange else []))
    return res[:6], list(res[6:])


def _halo_specs(te, seq, col, width):
    per = te // HALO
    last = seq // HALO - 1
    return [pl.BlockSpec((HALO, width), lambda i: (jnp.maximum(i * per - 1, 0), col)),
            pl.BlockSpec((te, width), lambda i: (i, col)),
            pl.BlockSpec((HALO, width), lambda i: (jnp.minimum((i + 1) * per, last), col))]


def _extended(prev_ref, cur_ref, next_ref, i, te, seq):
    xe = jnp.concatenate([prev_ref[...], cur_ref[...], next_ref[...]], axis=0).astype(F32)
    pos = i * te - HALO + lax.broadcasted_iota(jnp.int32, (te + 2 * HALO, 1), 0)
    return jnp.where((pos >= 0) & (pos < seq), xe, 0.0), pos


def _window_sum(x, levels, n, mirrored):
    first = (n - 1) if mirrored else 1
    acc = x + pltpu.roll(x, first, 0)
    step = 1
    for _ in range(levels - 1):
        acc = pltpu.roll(acc, step, 0) + pltpu.roll(acc, n - step, 0)
        step *= 2
    return acc


def _window_count(pos, w, seq):
    lo = jnp.clip(pos - w // 2, 0, seq)
    hi = jnp.clip(pos - w // 2 + w, 0, seq)
    return jnp.maximum(hi - lo, 1).astype(F32)


def pool_fwd(qkvu, pool_w, pool_scale, name):
    seq = qkvu.shape[0]
    te = _tile(seq, (512, 256))
    n = te + 2 * HALO

    def body(up_ref, uc_ref, un_ref, w_ref, sc_ref, o_ref):
        i = pl.program_id(0)
        xe, pos = _extended(up_ref, uc_ref, un_ref, i, te, seq)
        cnt = pos[HALO:HALO + te]
        for g, w in enumerate(POOL_WINDOWS):
            ls = slice(g * POOL_G, (g + 1) * POOL_G)
            xg = xe[:, ls]
            win = _window_sum(xg, g + 1, n, False)[HALO:HALO + te]
            dlt = win / _window_count(cnt, w, seq) - xg[HALO:HALO + te]
            z = _dot(dlt.astype(BF16), w_ref[g])
            o_ref[:, ls] = (z * sc_ref[:, ls]).astype(BF16)

    return pl.pallas_call(
        body, name=name, grid=(seq // te,),
        in_specs=_halo_specs(te, seq, 3, POOL_W)
                 + [pl.BlockSpec((4, POOL_G, POOL_G), lambda i: (0, 0, 0)), pl.BlockSpec((1, POOL_W), lambda i: (0, 0))],
        out_specs=pl.BlockSpec((te, POOL_W), lambda i: (i, 0)),
        out_shape=jax.ShapeDtypeStruct((seq, POOL_W), BF16),
        compiler_params=_params(("parallel",)),
    )(qkvu, qkvu, qkvu, pool_w, pool_scale)


def pool_bwd(qkvu, dmix, pool_w, pool_scale, name):
    seq = qkvu.shape[0]
    te = _tile(seq, (512, 256))
    n = te + 2 * HALO

    def body(up_ref, uc_ref, un_ref, dp_ref, dc_ref, dn_ref, w_ref, sc_ref, du_ref, dw_ref, dsc_ref):
        i = pl.program_id(0)

        @pl.when(i == 0)
        def _():
            dw_ref[...] = jnp.zeros_like(dw_ref)
            dsc_ref[...] = jnp.zeros_like(dsc_ref)

        xe, pos = _extended(up_ref, uc_ref, un_ref, i, te, seq)
        de, _ = _extended(dp_ref, dc_ref, dn_ref, i, te, seq)
        cpos = pos[HALO:HALO + te]
        for g, w in enumerate(POOL_WINDOWS):
            ls = slice(g * POOL_G, (g + 1) * POOL_G)
            xg = xe[:, ls]
            wg = w_ref[g]
            win = _window_sum(xg, g + 1, n, False)[HALO:HALO + te]
            dlt = (win / _window_count(cpos, w, seq) - xg[HALO:HALO + te]).astype(BF16)
            z = _dot(dlt, wg)
            dpg = de[:, ls]
            dsc_ref[:, ls] += jnp.sum(dpg[HALO:HALO + te] * z, axis=0, keepdims=True)
            dz = (dpg * sc_ref[:, ls]).astype(BF16)
            dw_ref[g] += _dot_tn(dlt, dz[HALO:HALO + te])
            dd = _dot_nt(dz, wg)
            back = _window_sum(dd / _window_count(pos, w, seq), g + 1, n, True)
            du_ref[:, ls] = (back[HALO:HALO + te] - dd[HALO:HALO + te]).astype(BF16)

    return pl.pallas_call(
        body, name=name, grid=(seq // te,),
        in_specs=_halo_specs(te, seq, 3, POOL_W) + _halo_specs(te, seq, 1, POOL_W)
                 + [pl.BlockSpec((4, POOL_G, POOL_G), lambda i: (0, 0, 0)), pl.BlockSpec((1, POOL_W), lambda i: (0, 0))],
        out_specs=[pl.BlockSpec((te, POOL_W), lambda i: (i, 0)),
                   pl.BlockSpec((4, POOL_G, POOL_G), lambda i: (0, 0, 0)), pl.BlockSpec((1, POOL_W), lambda i: (0, 0))],
        out_shape=[jax.ShapeDtypeStruct((seq, POOL_W), BF16), jax.ShapeDtypeStruct((4, POOL_G, POOL_G), F32),
                   jax.ShapeDtypeStruct((1, POOL_W), F32)],
        compiler_params=_params(("arbitrary",)),
    )(qkvu, qkvu, qkvu, dmix, dmix, dmix, pool_w, pool_scale)


def _shifted(z, zprev_row, znext_row, te):
    rows = lax.broadcasted_iota(jnp.int32, (te, 1), 0)
    zp = jnp.where(rows == 0, zprev_row, pltpu.roll(z, 1, 0))
    zn = jnp.where(rows == te - 1, znext_row, pltpu.roll(z, te - 1, 0))
    return zp, zn


def _edge_rows(prev_ref, next_ref, i, nt):
    p = prev_ref[HALO - 1:HALO, :].astype(F32)
    q = next_ref[0:1, :].astype(F32)
    return jnp.where(i == 0, 0.0, p), jnp.where(i == nt - 1, 0.0, q)


def conv_fwd(proj, conv_w, name):
    seq = proj.shape[0]
    te = _tile(seq, (512, 256))
    nt = seq // te

    def body(bg_ref, cp_ref, cc_ref, cn_ref, xp_ref, xc_ref, xn_ref, w_ref, o_ref):
        i = pl.program_id(0)
        z = cc_ref[...].astype(F32) * xc_ref[...].astype(F32)
        cpr, cnr = _edge_rows(cp_ref, cn_ref, i, nt)
        xpr, xnr = _edge_rows(xp_ref, xn_ref, i, nt)
        zp, zn = _shifted(z, cpr * xpr, cnr * xnr, te)
        y = zp * w_ref[0:1, :] + z * w_ref[1:2, :] + zn * w_ref[2:3, :]
        o_ref[...] = (bg_ref[...].astype(F32) * y).astype(BF16)

    return pl.pallas_call(
        body, name=name, grid=(nt,),
        in_specs=[pl.BlockSpec((te, D), lambda i: (i, 0))] + _halo_specs(te, seq, 1, D) + _halo_specs(te, seq, 2, D)
                 + [pl.BlockSpec((3, D), lambda i: (0, 0))],
        out_specs=pl.BlockSpec((te, D), lambda i: (i, 0)),
        out_shape=jax.ShapeDtypeStruct((seq, D), BF16),
        compiler_params=_params(("parallel",)),
    )(proj, proj, proj, proj, proj, proj, proj, conv_w)


def conv_bwd(proj, dgm, conv_w, name):
    seq = proj.shape[0]
    te = _tile(seq, (512, 256))
    nt = seq // te

    def body(bp_ref, bc_ref, bn_ref, cp_ref, cc_ref, cn_ref, xp_ref, xc_ref, xn_ref, gp_ref, gc_ref, gn_ref, w_ref,
             dbg_ref, dcg_ref, dxin_ref, dw_ref):
        i = pl.program_id(0)

        @pl.when(i == 0)
        def _():
            dw_ref[...] = jnp.zeros_like(dw_ref)

        bg = bc_ref[...].astype(F32)
        cg = cc_ref[...].astype(F32)
        xin = xc_ref[...].astype(F32)
        dg = gc_ref[...].astype(F32)
        z = cg * xin
        cpr, cnr = _edge_rows(cp_ref, cn_ref, i, nt)
        xpr, xnr = _edge_rows(xp_ref, xn_ref, i, nt)
        zp, zn = _shifted(z, cpr * xpr, cnr * xnr, te)
        w0, w1, w2 = w_ref[0:1, :], w_ref[1:2, :], w_ref[2:3, :]
        y = zp * w0 + z * w1 + zn * w2
        dbg_ref[...] = (dg * y).astype(BF16)
        dy = dg * bg
        dw_ref[0:1, :] += jnp.sum(dy * zp, axis=0, keepdims=True)
        dw_ref[1:2, :] += jnp.sum(dy * z, axis=0, keepdims=True)
        dw_ref[2:3, :] += jnp.sum(dy * zn, axis=0, keepdims=True)
        bpr, bnr = _edge_rows(bp_ref, bn_ref, i, nt)
        gpr, gnr = _edge_rows(gp_ref, gn_ref, i, nt)
        dyp, dyn = _shifted(dy, bpr * gpr, bnr * gnr, te)
        dz = dyn * w0 + dy * w1 + dyp * w2
        dcg_ref[...] = (dz * xin).astype(BF16)
        dxin_ref[...] = (dz * cg).astype(BF16)

    row = pl.BlockSpec((te, D), lambda i: (i, 0))
    return pl.pallas_call(
        body, name=name, grid=(nt,),
        in_specs=_halo_specs(te, seq, 0, D) + _halo_specs(te, seq, 1, D) + _halo_specs(te, seq, 2, D)
                 + _halo_specs(te, seq, 0, D) + [pl.BlockSpec((3, D), lambda i: (0, 0))],
        out_specs=[row, row, row, pl.BlockSpec((3, D), lambda i: (0, 0))],
        out_shape=[jax.ShapeDtypeStruct((seq, D), BF16)] * 3 + [jax.ShapeDtypeStruct((3, D), F32)],
        compiler_params=_params(("arbitrary",)),
    )(proj, proj, proj, proj, proj, proj, proj, proj, proj, dgm, dgm, dgm, conv_w)


def _position():
    x, y, c = lax.axis_index("x"), lax.axis_index("y"), lax.axis_index("c")
    return x, y, c, 4 * x + 2 * y + c


def _peer(x, y, c, j):
    px = 1 - x if j & 4 else x
    py = 1 - y if j & 2 else y
    pc = 1 - c if j & 1 else c
    return (px, py, pc), 4 * px + 2 * py + pc


def small_allgather(v, name):
    rows, cols = v.shape

    def body(v_ref, o_ref, send_sems, recv_sems, local_sem):
        x, y, c, me = _position()
        mine = pltpu.make_async_copy(v_ref, o_ref.at[me], local_sem)
        mine.start()
        sends = []
        for j in range(1, N_DEV):
            peer, _ = _peer(x, y, c, j)
            cp = pltpu.make_async_remote_copy(src_ref=v_ref, dst_ref=o_ref.at[me], send_sem=send_sems.at[j - 1],
                                              recv_sem=recv_sems.at[j - 1], device_id=peer, device_id_type=MESH_ID)
            cp.start()
            sends.append(cp)
        for j in range(1, N_DEV):
            peer, pid = _peer(x, y, c, j)
            pltpu.make_async_remote_copy(src_ref=v_ref, dst_ref=o_ref.at[pid], send_sem=send_sems.at[j - 1],
                                         recv_sem=recv_sems.at[j - 1], device_id=peer,
                                         device_id_type=MESH_ID).wait_recv()
        for cp in sends:
            cp.wait_send()
        mine.wait()

    return pl.pallas_call(
        body, name=name,
        out_shape=jax.ShapeDtypeStruct((N_DEV, rows, cols), v.dtype),
        in_specs=[pl.BlockSpec(memory_space=pltpu.VMEM)],
        out_specs=pl.BlockSpec(memory_space=pltpu.VMEM),
        scratch_shapes=[pltpu.SemaphoreType.DMA((N_DEV - 1,)), pltpu.SemaphoreType.DMA((N_DEV - 1,)),
                        pltpu.SemaphoreType.DMA],
        compiler_params=pltpu.CompilerParams(vmem_limit_bytes=VMEM_LIMIT),
    )(v)


class Exchange:
    def __init__(self, kind, arrays):
        self.kind, self.arrays, self.n = kind, list(arrays), len(arrays)
        n = self.n
        if kind == "gather":
            self.out_shapes = [jax.ShapeDtypeStruct((N_DEV,) + a.shape, a.dtype) for a in self.arrays]
        else:
            self.out_shapes = [jax.ShapeDtypeStruct(a.shape, a.dtype) for a in self.arrays]
        self.scratch = [pltpu.SemaphoreType.DMA((7 * n,)), pltpu.SemaphoreType.DMA((7 * n,)),
                        pltpu.SemaphoreType.DMA((n,))]

    def _gather_copies(self, ins, outs, sems):
        send_sems, recv_sems, local_sems = sems
        x, y, c, me = _position()
        chips = [(1 - x, y), (x, 1 - y), (1 - x, 1 - y)]

        def blk(k, px, py, pc):
            return outs[k].at[4 * px + 2 * py + pc]

        def copy(k, slot, block, to, src=None):
            return pltpu.make_async_remote_copy(
                src_ref=blk(k, *block) if src is None else src, dst_ref=blk(k, *block),
                send_sem=send_sems.at[k * 7 + slot], recv_sem=recv_sems.at[k * 7 + slot],
                device_id=to, device_id_type=MESH_ID)

        mine = [pltpu.make_async_copy(ins[k], blk(k, x, y, c), local_sems.at[k]) for k in range(self.n)]
        first = []
        for k in range(self.n):
            first.append(copy(k, 0, (x, y, c), (x, y, 1 - c), src=ins[k]))
            first += [copy(k, 1 + j, (x, y, c), (*chip, c), src=ins[k]) for j, chip in enumerate(chips)]
        return (x, y, c), chips, copy, mine, first

    def start(self, ins, outs, sems):
        if self.kind == "gather":
            _, _, _, mine, first = self._gather_copies(ins, outs, sems)
            for cp in mine + first:
                cp.start()
        else:
            for cp in self._scatter_copies(ins, outs, sems, False):
                cp.start()

    def finish(self, ins, outs, sems):
        if self.kind == "gather":
            (x, y, c), chips, copy, mine, first = self._gather_copies(ins, outs, sems)
            passed = []
            for j, chip in enumerate(chips):
                for k in range(self.n):
                    copy(k, 1 + j, (*chip, c), (x, y, c)).wait_recv()
                    cp = copy(k, 4 + j, (*chip, c), (x, y, 1 - c))
                    cp.start()
                    passed.append(cp)
            for k in range(self.n):
                copy(k, 0, (x, y, 1 - c), (x, y, c)).wait_recv()
                for j, chip in enumerate(chips):
                    copy(k, 4 + j, (*chip, 1 - c), (x, y, c)).wait_recv()
            for cp in first + passed:
                cp.wait_send()
            for cp in mine:
                cp.wait()
        else:
            for cp in self._scatter_copies(ins, outs, sems, True):
                cp.wait_recv()
            copies = self._scatter_copies(ins, outs, sems, False)
            for cp in copies[self.n:]:
                cp.wait_send()
            for cp in copies[:self.n]:
                cp.wait()

    def _scatter_copies(self, ins, outs, sems, arrivals):
        send_sems, recv_sems, local_sems = sems
        x, y, c, me = _position()
        out = []
        if not arrivals:
            out = [pltpu.make_async_copy(ins[k].at[me], outs[k].at[me], local_sems.at[k]) for k in range(self.n)]
        for j in range(1, N_DEV):
            peer, pid = _peer(x, y, c, j)
            for k in range(self.n):
                out.append(pltpu.make_async_remote_copy(
                    src_ref=ins[k].at[pid], dst_ref=outs[k].at[pid if arrivals else me],
                    send_sem=send_sems.at[k * 7 + j - 1], recv_sem=recv_sems.at[k * 7 + j - 1],
                    device_id=peer, device_id_type=MESH_ID))
        return out


def sum_devices(v, name):
    _, rows, cols = v.shape

    def body(v_ref, o_ref):
        acc = v_ref[0]
        for p in range(1, N_DEV):
            acc = acc + v_ref[p]
        o_ref[...] = acc

    return pl.pallas_call(
        body, name=name, out_shape=jax.ShapeDtypeStruct((rows, cols), F32),
        compiler_params=pltpu.CompilerParams(vmem_limit_bytes=VMEM_LIMIT),
    )(v)


def _silu(x):
    return x * _sigmoid(x)


def adaln_fwd(cm, mod_w, mod_b_cols, name):
    cols = mod_w.shape[2]

    def body(c_ref, w_ref, b_ref, o_ref):
        o_ref[0] = jnp.dot(_silu(c_ref[...]), w_ref[0], precision=HI, preferred_element_type=F32) + b_ref[0]

    return pl.pallas_call(
        body, name=name, grid=(2,),
        in_specs=[pl.BlockSpec((16, D), lambda l: (0, 0)), pl.BlockSpec((1, D, cols), lambda l: (l, 0, 0)),
                  pl.BlockSpec((1, 1, cols), lambda l: (l, 0, 0))],
        out_specs=pl.BlockSpec((1, 16, cols), lambda l: (l, 0, 0)),
        out_shape=jax.ShapeDtypeStruct((2, 16, cols), F32),
        compiler_params=_params(("parallel",)),
    )(cm, mod_w, mod_b_cols)


def adaln_bwd(cm_t, mod_w, dm_t, name):
    cols = mod_w.shape[2]

    def body(c_ref, w_ref, lat_ref, ctx_ref, gw_ref, pc_ref):
        ctot = jnp.sum(ctx_ref[0], axis=0, keepdims=True)
        rows = lax.broadcasted_iota(jnp.int32, (8, 1), 0)
        g_hi = jnp.where(rows == 0, ctot, 0.0)
        g = jnp.concatenate([lat_ref[0], g_hi], axis=0)
        gw_ref[0] = jnp.dot(_silu(c_ref[...]), g, precision=HI, preferred_element_type=F32)
        pc_ref[0] = lax.dot_general(g_hi, w_ref[0], NT_DIMS, precision=HI, preferred_element_type=F32)

    return pl.pallas_call(
        body, name=name, grid=(2,),
        in_specs=[pl.BlockSpec((D, 16), lambda l: (0, 0)), pl.BlockSpec((1, D, cols), lambda l: (l, 0, 0)),
                  pl.BlockSpec((1, 8, cols), lambda l: (l, 0, 0)), pl.BlockSpec((1, 8, cols), lambda l: (l + 2, 0, 0))],
        out_specs=[pl.BlockSpec((1, D, cols), lambda l: (l, 0, 0)), pl.BlockSpec((1, 8, D), lambda l: (l, 0, 0))],
        out_shape=[jax.ShapeDtypeStruct((2, D, cols), F32), jax.ShapeDtypeStruct((2, 8, D), F32)],
        compiler_params=_params(("parallel",)),
    )(cm_t, mod_w, dm_t, dm_t)


def mod_b_grad(dm_t, name):
    width = dm_t.shape[2]
    tn = width // 8

    def body(d_ref, o_ref):
        s = jnp.concatenate([jnp.sum(d_ref[k], axis=0, keepdims=True) for k in range(4)]
                            + [jnp.zeros((4, tn), F32)], axis=0)
        o_ref[...] = s + pltpu.roll(s, 6, 0)

    return pl.pallas_call(
        body, name=name, grid=(8,),
        in_specs=[pl.BlockSpec((4, 8, tn), lambda j: (0, 0, j))],
        out_specs=pl.BlockSpec((8, tn), lambda j: (0, j)),
        out_shape=jax.ShapeDtypeStruct((8, width), F32),
        compiler_params=_params(("parallel",)),
    )(dm_t)


def adamw(w, m, v, name, g=None, recv=None):
    rows, cols = w.shape
    tr = _tile(rows, (256, 128, 64, 32, 16, 8))
    summed = recv is not None

    def body(w_ref, m_ref, v_ref, g_ref, go_ref, d_ref, mo_ref, vo_ref):
        if summed:
            gv = g_ref[0].astype(F32)
            for p in range(1, N_DEV):
                gv = gv + g_ref[p].astype(F32)
        else:
            gv = g_ref[...]
        mn = ADAM_B1 * m_ref[...] + (1.0 - ADAM_B1) * gv
        vn = ADAM_B2 * v_ref[...] + (1.0 - ADAM_B2) * (gv * gv)
        m_hat = mn / (1.0 - ADAM_B1 ** ADAM_STEP)
        v_hat = vn / (1.0 - ADAM_B2 ** ADAM_STEP)
        go_ref[...] = gv
        d_ref[...] = -ADAM_LR * (m_hat / (jnp.sqrt(v_hat) + ADAM_EPS) + ADAM_WD * w_ref[...])
        mo_ref[...] = mn
        vo_ref[...] = vn

    row = pl.BlockSpec((tr, cols), lambda i: (i, 0))
    gspec = pl.BlockSpec((N_DEV, tr, cols), lambda i: (0, i, 0)) if summed else row
    return pl.pallas_call(
        body, name=name, grid=(rows // tr,),
        in_specs=[row, row, row, gspec], out_specs=[row] * 4,
        out_shape=[jax.ShapeDtypeStruct((rows, cols), F32)] * 4,
        compiler_params=_params(("parallel",)),
    )(w, m, v, recv if summed else g)


def _ffn_fwd(h, hn, mods, w13, w2, base, tag, nxt=None, exchange=None):
    (p, u, s), exchanged = ffn_up(hn, w13, tag + "_up", exchange)
    if callable(w2):
        w2 = w2(exchanged)
    outs = mm_nn([s], w2, [0], tag + "_down", res=(h, mods, base + 2, 0.5), nxt=nxt)
    h_new, y = outs[0], outs[1]
    return h_new, (outs[2] if nxt else None), (h, hn, p, u, s, y), exchanged


COLUMN_CUT = ("w13", "ewi", "cwi")
GATHER_FIRST = ("w13_00",)
GATHER_IN_FFN = ("w2_00", "ewi", "ewo", "w13_01", "w2_01")
GATHER_IN_ATTN = ("w13_10", "w2_10", "cwi", "cwo", "w13_11", "w2_11")
SCATTER_IN_ATTN = ("w13_11", "w2_11", "cwi", "cwo", "w13_10", "w2_10", "w13_01", "w2_01", "ewo")
SCATTER_LAST = ("w13_00", "w2_00", "ewi")


def unpack_piece(p, g):
    if p.split("_")[0] in COLUMN_CUT:
        return g.transpose(1, 0, 2).reshape(g.shape[1], -1)
    return g.reshape(-1, g.shape[2])


def block_piece(p, full):
    if p.split("_")[0] in COLUMN_CUT:
        return full.reshape(full.shape[0], N_DEV, -1).transpose(1, 0, 2).astype(BF16)
    return full.reshape(N_DEV, -1, full.shape[1]).astype(BF16)


def _ffn_bwd(dy, dres, saved, mods, g, w13, w2, base, tag, prev=None, acc=None, exchange_of=None):
    h, hn, p, u, s, _ = saved
    ff = w2.shape[0]
    acc = acc or (None, None, None)
    da, db = mm_nt(dy, w2, tag + "_ds", dswiglu=(u, p))
    dw2 = mm_tn(s, dy, tag + "_dw2", acc=acc[2])
    dwa = mm_tn(hn, da, tag + "_dw13a", acc=acc[0])
    dwb = mm_tn(hn, db, tag + "_dw13b", acc=acc[1])
    exchange = exchange_of(dwa, dwb, dw2) if exchange_of else None
    outs = mm_nt_norm([da, db], w13, [0, ff], h, g, mods, base + 1, dres, tag + "_dhn", prev=prev, exchange=exchange)
    dh, dshift, dscale, dg = outs[:4]
    nprev = 2 if prev else 0
    return (dh, (dwa, dwb, dw2), dg, {base: dshift, base + 1: dscale}, tuple(outs[4:4 + nprev]),
            list(outs[4 + nprev:]))


def _mod_rows(parts):
    zero = jnp.zeros((1, D), F32)
    return jnp.concatenate([parts.get(k, zero) for k in range(N_MOD)], axis=0)


def local_step(x, ctx, ml, mc, wts, target, shards=None):
    wts = dict(wts)
    ng = wts["norm_g"]
    gvec = lambda l, k: ng[l, k][None, :]
    pool_w16 = wts["pool_w"].astype(BF16)
    grads = {}

    def gather(pieces):
        return Exchange("gather", [shards[p] for p in pieces]) if shards else None

    def arrived(pieces, results):
        for p, g in zip(pieces, results):
            wts[p] = unpack_piece(p, g)

    def ffn_grads(lf, f):
        grads["w13_" + lf] = jnp.concatenate([f[0], f[1]], axis=1)
        grads["w2_" + lf] = f[2]

    if shards:
        xh, got = normmod(x, gvec(0, 0), ml[0], 0, 1, "l0f1_norm", gather(GATHER_FIRST))
        arrived(GATHER_FIRST, got)
    else:
        xh = normmod(x, gvec(0, 0), ml[0], 0, 1, "l0f1_norm")
    ch = normmod(ctx, gvec(0, 0), mc[0], 0, 1, "l0f1c_norm")
    def w2_after_up(got):
        arrived(GATHER_IN_FFN, got)
        return wts["w2_00"]

    x1, xn, sv1, _ = _ffn_fwd(x, xh, ml[0], wts["w13_00"], w2_after_up, 0, "l0f1",
                              nxt=(gvec(0, 1), ml[0], 3, 4), exchange=gather(GATHER_IN_FFN))
    c1, cn, sv1c, _ = _ffn_fwd(ctx, ch, mc[0], wts["w13_00"], wts["w2_00"], 0, "l0f1c", nxt=(gvec(0, 1), mc[0], 3, 4))
    qkvu = mm_nn([xn], wts["ewi"], [0], "l0mix_in")
    qkvu_c = mm_nn([cn], wts["ewi"], [0], "l0mix_in_c")
    tab = bias_table(wts["rpb"])
    att, got = attn_fwd(qkvu, qkvu_c, tab, "l0_attn", gather(GATHER_IN_ATTN))
    arrived(GATHER_IN_ATTN, got)
    pool = pool_fwd(qkvu, pool_w16, wts["pool_scale"], "l0_pool")
    x2, ymix, xh = mm_nn([att, pool], wts["ewo"], [0, NA_W], "l0mix_out", res=(x1, ml[0], 5, 1.0),
                         nxt=(gvec(0, 2), ml[0], 6, 7))
    x3, xh, sv2, _ = _ffn_fwd(x2, xh, ml[0], wts["w13_01"], wts["w2_01"], 6, "l0f2", nxt=(gvec(1, 0), ml[1], 0, 1))

    x4, xn1, sv3, _ = _ffn_fwd(x3, xh, ml[1], wts["w13_10"], wts["w2_10"], 0, "l1f1", nxt=(gvec(1, 1), ml[1], 3, 4))
    proj = mm_nn([xn1], wts["cwi"], [0], "l1mix_in")
    gm = conv_fwd(proj, wts["conv_w"], "l1_conv")
    x5, ycv, xh = mm_nn([gm], wts["cwo"], [0], "l1mix_out", res=(x4, ml[1], 5, 1.0), nxt=(gvec(1, 2), ml[1], 6, 7))
    x6, _, sv4, _ = _ffn_fwd(x5, xh, ml[1], wts["w13_11"], wts["w2_11"], 6, "l1f2")

    dx6, loss, dgf, dy, dgate = loss_head(x6, wts["final_g"][None, :], target, (sv4[5], ml[1], 8, 0.5), "loss_head")
    dm1 = {8: dgate}
    dx5, dwf4, dg12, dm_f4, (dy, dgate), _ = _ffn_bwd(dy, dx6, sv4, ml[1], gvec(1, 2), wts["w13_11"], wts["w2_11"], 6,
                                                      "l1f2", prev=(ycv, ml[1], 5, 1.0))
    ffn_grads("11", dwf4)
    dm1.update({5: dgate, **dm_f4})
    dgm = mm_nt(dy, wts["cwo"], "l1mix_dgm")
    grads["cwo"] = mm_tn(gm, dy, "l1mix_dwo")
    dbg, dcg, dxin, dconv_w = conv_bwd(proj, dgm, wts["conv_w"], "l1_conv_bwd")
    grads["cwi"] = jnp.concatenate([mm_tn(xn1, t, "l1mix_dwi%d" % k) for k, t in enumerate((dbg, dcg, dxin))], axis=1)
    dx4, dsh, dsc, dg11, dy, dgate = mm_nt_norm([dbg, dcg, dxin], wts["cwi"], [0, D, 2 * D], x4, gvec(1, 1), ml[1], 4,
                                                dx5, "l1mix_dxn", prev=(sv3[5], ml[1], 2, 0.5))
    dm1.update({3: dsh, 4: dsc, 2: dgate})
    dx3, dwf3, dg10, dm_f3, (dy, dgate), _ = _ffn_bwd(dy, dx4, sv3, ml[1], gvec(1, 0), wts["w13_10"], wts["w2_10"], 0,
                                                      "l1f1", prev=(sv2[5], ml[0], 8, 0.5))
    ffn_grads("10", dwf3)
    dm1.update(dm_f3)
    dm0 = {8: dgate}

    dx2, dwf2, dg02, dm_f2, (dy, dgate), _ = _ffn_bwd(dy, dx3, sv2, ml[0], gvec(0, 2), wts["w13_01"], wts["w2_01"], 6,
                                                      "l0f2", prev=(ymix, ml[0], 5, 1.0))
    ffn_grads("01", dwf2)
    dm0.update({5: dgate, **dm_f2})
    dmix = mm_nt(dy, wts["ewo"], "l0mix_dmix")
    grads["ewo"] = jnp.concatenate([mm_tn(att, dy, "l0mix_dwo_att"), mm_tn(pool, dy, "l0mix_dwo_pool")], axis=0)
    scatter = Exchange("scatter", [block_piece(p, grads.pop(p)) for p in SCATTER_IN_ATTN]) if shards else None
    (dq, dk, dv, dkc, dvc, dtab), got = attn_bwd(qkvu, qkvu_c, tab, dmix, "l0_attn_bwd", scatter)
    recv = dict(zip(SCATTER_IN_ATTN, got))
    du, dpool_w, dpool_scale = pool_bwd(qkvu, dmix, pool_w16, wts["pool_scale"], "l0_pool_bwd")
    drpb = bias_table_bwd(dtab)
    dk16, dv16, dkc16, dvc16 = dk, dv, dkc.astype(BF16), dvc.astype(BF16)
    grads["ewi"] = jnp.concatenate([
        mm_tn(xn, dq, "l0mix_dwi_q"),
        mm_tn(cn, dkc16, "l0mix_dwi_kc", acc=mm_tn(xn, dk16, "l0mix_dwi_k")),
        mm_tn(cn, dvc16, "l0mix_dwi_vc", acc=mm_tn(xn, dv16, "l0mix_dwi_v")),
        mm_tn(xn, du, "l0mix_dwi_u")], axis=1)
    dx1, dsh, dsc, dg01, dy, dgate = mm_nt_norm([dq, dk16, dv16, du], wts["ewi"], [0, NA_W, 2 * NA_W, 3 * NA_W], x1,
                                                gvec(0, 1), ml[0], 4, dx2, "l0mix_dxn", prev=(sv1[5], ml[0], 2, 0.5))
    dm0.update({3: dsh, 4: dsc, 2: dgate})
    dc1, dsh_c, dsc_c, dg01c, dy_c, dgate_c = mm_nt_norm([dkc16, dvc16], wts["ewi"], [NA_W, 2 * NA_W], c1, gvec(0, 1),
                                                         mc[0], 4, None, "l0mix_dxn_c", prev=(sv1c[5], mc[0], 2, 0.5))
    _, dwf1c, dg00c, dm_f1c, _, _ = _ffn_bwd(dy_c, dc1, sv1c, mc[0], gvec(0, 0), wts["w13_00"], wts["w2_00"], 0, "l0f1c")
    dmc0 = {3: dsh_c, 4: dsc_c, 2: dgate_c, **dm_f1c}

    def last_scatter(dwa, dwb, dw2):
        ffn_grads("00", (dwa, dwb, dw2))
        return Exchange("scatter", [block_piece(p, grads.pop(p)) for p in SCATTER_LAST]) if shards else None

    dx0, _, dg00, dm_f1, _, got = _ffn_bwd(dy, dx1, sv1, ml[0], gvec(0, 0), wts["w13_00"], wts["w2_00"], 0, "l0f1",
                                           acc=dwf1c, exchange_of=last_scatter)
    recv.update(zip(SCATTER_LAST, got))
    dm0.update(dm_f1)

    return {
        "loss": loss, "grad_x": dx0,
        "dml": jnp.stack([_mod_rows(dm0), _mod_rows(dm1)]),
        "dmc": jnp.stack([_mod_rows(dmc0), jnp.zeros((N_MOD, D), F32)]),
        "norm_g": jnp.concatenate([dg00 + dg00c, dg01 + dg01c, dg02, dg10, dg11, dg12], axis=0),
        "grads": grads, "recv": recv,
        "rpb": drpb, "pool_w": dpool_w, "pool_scale": dpool_scale, "conv_w": dconv_w, "final_g": dgf,
    }


def _rows_of(v, nrows):
    flat = v.reshape(-1)
    return jnp.pad(flat, (0, nrows * D - flat.shape[0])).reshape(nrows, D)


def kernel(x, c, ctx, c_ctx, mod_w, mod_b, norm_g, ffn_w13, ffn_w2, even_w_in, even_w_out, na_rpb, pool_w, pool_scale, conv_w_in, conv_w, conv_w_out, final_g, loss_target, m_c_ctx, m_mod_w, m_mod_b, m_norm_g, m_ffn_w13, m_ffn_w2, m_even_w_in, m_even_w_out, m_na_rpb, m_pool_w, m_pool_scale, m_conv_w_in, m_conv_w, m_conv_w_out, m_final_g, v_c_ctx, v_mod_w, v_mod_b, v_norm_g, v_ffn_w13, v_ffn_w2, v_even_w_in, v_even_w_out, v_na_rpb, v_pool_w, v_pool_scale, v_conv_w_in, v_conv_w, v_conv_w_out, v_final_g):
    me = 4 * lax.axis_index("x") + 2 * lax.axis_index("y") + lax.axis_index("c")
    ff = ffn_w2.shape[2] * N_DEV
    w13c = ffn_w13.shape[3]
    w2r = ffn_w2.shape[2]
    mcols = mod_w.shape[2]
    gcols = norm_g.shape[2]

    big = {"w13": ffn_w13.reshape(4 * D, w13c), "w2": ffn_w2.reshape(4 * w2r, D), "ewi": even_w_in[0],
           "ewo": even_w_out[0], "cwi": conv_w_in[0], "cwo": conv_w_out[0]}
    names = list(big)
    shards = {"ewi": even_w_in[0].astype(BF16), "ewo": even_w_out[0].astype(BF16),
              "cwi": conv_w_in[0].astype(BF16), "cwo": conv_w_out[0].astype(BF16)}
    for l in range(2):
        for f in range(2):
            shards["w13_%d%d" % (l, f)] = ffn_w13[l, f].astype(BF16)
            shards["w2_%d%d" % (l, f)] = ffn_w2[l, f].astype(BF16)
    wts = {}

    c_all = small_allgather(jnp.pad(c, ((0, 7), (0, 0))), "cond_allgather")[:, 0, :]
    cm = jnp.concatenate([c_all, c_ctx[None, :], jnp.zeros((7, D), F32)], axis=0)
    mod_b_cols = lax.dynamic_slice(mod_b, (0, me * mcols), (2, mcols))[:, None, :]
    m_cols = adaln_fwd(cm, mod_w, mod_b_cols, "adaln_fwd")
    m_all = small_allgather(m_cols.reshape(32, mcols), "mod_allgather")
    m_full = m_all.reshape(N_DEV, 2, 16, mcols).transpose(1, 2, 0, 3).reshape(2, 16, N_MOD * D)
    ml = lax.dynamic_slice(m_full, (0, me, 0), (2, 1, N_MOD * D)).reshape(2, N_MOD, D)
    mc = m_full[:, 8].reshape(2, N_MOD, D)

    full_norm_g = small_allgather(_rows_of(norm_g, 8), "norm_g_allgather")[:, 0, :2 * 3 * gcols]
    full_norm_g = full_norm_g.reshape(N_DEV, 2, 3, gcols).transpose(1, 2, 0, 3).reshape(2, 3, D)
    full_conv_w = small_allgather(_rows_of(conv_w, 8), "conv_w_allgather")[:, 0, :3 * gcols]
    full_conv_w = full_conv_w.reshape(N_DEV, 3, gcols).transpose(1, 0, 2).reshape(3, D)
    wts.update(norm_g=full_norm_g, conv_w=full_conv_w, rpb=na_rpb[0], pool_w=pool_w[0], pool_scale=pool_scale,
               final_g=final_g)
    out = local_step(x[0], ctx[0], ml, mc, wts, loss_target[0], shards)

    dm_pack = jnp.concatenate([out["dml"].reshape(2, N_MOD * D), out["dmc"].reshape(2, N_MOD * D),
                               jnp.zeros((4, N_MOD * D), F32)], axis=0)
    dm_t = small_allgather(dm_pack, "dmod_allgather").transpose(1, 0, 2)[:4]
    dm_cols = lax.dynamic_slice(dm_t, (0, 0, me * mcols), (4, N_DEV, mcols))
    g_mod_w, pc = adaln_bwd(cm.T, mod_w, dm_cols, "adaln_bwd")
    g_mod_b = mod_b_grad(dm_t, "mod_b_grad")[:2]

    pack = jnp.concatenate([_rows_of(t, 8) for t in (
        out["norm_g"], out["conv_w"], out["final_g"], pc[0, :1] + pc[1, :1], out["pool_scale"], out["loss"],
        out["rpb"])] + [_rows_of(out["pool_w"], 64)], axis=0)
    small = sum_devices(small_allgather(pack, "small_grads_allgather"), "small_grads_sum")
    g_norm_g = lax.dynamic_slice(small[0:6].reshape(2, 3, D), (0, 0, me * gcols), (2, 3, gcols))
    g_conv_w = lax.dynamic_slice(small[8:11], (0, me * gcols), (3, gcols))[None]
    g_final_g = small[16]
    sg = _sigmoid(c_ctx)
    g_c_ctx = small[24] * (sg * (1.0 + c_ctx * (1.0 - sg)))
    g_pool_scale = small[32:33, :POOL_W]
    loss = small[40, 0]
    g_rpb = small[48:52].reshape(-1)[:na_rpb.size].reshape(na_rpb.shape)
    g_pool_w = small[56:120].reshape(pool_w.shape)

    pieces = out["recv"]
    lf = ("00", "01", "10", "11")
    recv = {"w13": jnp.concatenate([pieces["w13_" + t] for t in lf], axis=1),
            "w2": jnp.concatenate([pieces["w2_" + t] for t in lf], axis=1),
            "ewi": pieces["ewi"], "ewo": pieces["ewo"], "cwi": pieces["cwi"], "cwo": pieces["cwo"]}

    moments = {"w13": (m_ffn_w13, v_ffn_w13), "w2": (m_ffn_w2, v_ffn_w2), "ewi": (m_even_w_in, v_even_w_in),
               "ewo": (m_even_w_out, v_even_w_out), "cwi": (m_conv_w_in, v_conv_w_in),
               "cwo": (m_conv_w_out, v_conv_w_out)}
    orig = {"w13": ffn_w13, "w2": ffn_w2, "ewi": even_w_in, "ewo": even_w_out, "cwi": conv_w_in, "cwo": conv_w_out}
    upd = {}
    for k in names:
        shp2 = big[k].shape
        res = adamw(big[k], moments[k][0].reshape(shp2), moments[k][1].reshape(shp2), "adamw_" + k, recv=recv[k])
        upd[k] = [r.reshape(orig[k].shape) for r in res]
    shp2 = (2 * D, mcols)
    upd["mod_w"] = [r.reshape(mod_w.shape) for r in adamw(mod_w.reshape(shp2), m_mod_w.reshape(shp2),
                                                          v_mod_w.reshape(shp2), "adamw_mod_w",
                                                          g=g_mod_w.reshape(shp2))]

    smalls = [("c_ctx", c_ctx, m_c_ctx, v_c_ctx, g_c_ctx, 8), ("mod_b", mod_b, m_mod_b, v_mod_b, g_mod_b, 24),
              ("norm_g", norm_g, m_norm_g, v_norm_g, g_norm_g, 8), ("rpb", na_rpb, m_na_rpb, v_na_rpb, g_rpb, 8),
              ("pool_w", pool_w, m_pool_w, v_pool_w, g_pool_w, 64),
              ("pool_scale", pool_scale, m_pool_scale, v_pool_scale, g_pool_scale, 8),
              ("conv_w", conv_w, m_conv_w, v_conv_w, g_conv_w, 8), ("final_g", final_g, m_final_g, v_final_g, g_final_g, 8)]
    packed = [jnp.concatenate([_rows_of(s[col], s[5]) for s in smalls], axis=0) for col in (1, 2, 3, 4)]
    res = adamw(packed[0], packed[1], packed[2], "adamw_small", g=packed[3])
    row = 0
    for name, w, _, _, _, nrows in smalls:
        upd[name] = [r[row:row + nrows].reshape(-1)[:w.size].reshape(w.shape) for r in res]
        row += nrows

    order = ["c_ctx", "mod_w", "mod_b", "norm_g", "w13", "w2", "ewi", "ewo", "rpb", "pool_w", "pool_scale", "cwi",
             "conv_w", "cwo", "final_g"]
    grad_x = out["grad_x"][None]
    return (loss, grad_x, *[upd[k][0] for k in order], *[upd[k][1] for k in order], *[upd[k][2] for k in order],
            *[upd[k][3] for k in order])
```

```python
import functools

import numpy as np
import jax
import jax.numpy as jnp
from jax import lax
from jax.experimental import pallas as pl
from jax.experimental.pallas import tpu as pltpu

D = 1024
FF = 2816
SEQ = 16384
CTX = 256
GRID_W = 64
N_MOD = 9
HEADS = 8
HEAD_DIM = 64
NA_W = 512
POOL_W = 512
POOL_G = 128
POOL_WINDOWS = (2, 4, 8, 16)
KH = 8
KW = 16
RMS_EPS = 1e-6
NEG_INF = -1e30
N_DEV = 8

ADAM_LR = 0.001
ADAM_B1 = 0.9
ADAM_B2 = 0.999
ADAM_EPS = 1e-08
ADAM_WD = 0.01
ADAM_STEP = 10

VMEM_LIMIT = 52 * 1024 * 1024
HALO = 16
QROWS = 8
WROWS = 24

BF16 = jnp.bfloat16
F32 = jnp.float32
MESH_ID = pl.DeviceIdType.MESH
HI = lax.Precision.HIGHEST

NT_DIMS = (((1,), (1,)), ((), ()))
TN_DIMS = (((0,), (0,)), ((), ()))


def _tile(n, cands):
    for c in cands:
        if n % c == 0:
            return c
    return n


def _params(sem):
    return pltpu.CompilerParams(dimension_semantics=sem, vmem_limit_bytes=VMEM_LIMIT)


def _dot(a, b):
    return jnp.dot(a, b, preferred_element_type=F32)


def _dot_nt(a, b):
    return lax.dot_general(a, b, NT_DIMS, preferred_element_type=F32)


def _dot_tn(a, b):
    return lax.dot_general(a, b, TN_DIMS, preferred_element_type=F32)


def _sigmoid(x):
    return 1.0 / (1.0 + jnp.exp(-x))


def normmod(h, g, mods, i_shift, i_scale, name, exchange=None):
    n = h.shape[0]
    te = _tile(n, (512, 256))
    nt = n // te
    nx = exchange.n if exchange else 0

    def body(*refs):
        h_ref, g_ref, m_ref = refs[:3]
        x_in = refs[3:3 + nx]
        o_ref = refs[3 + nx]
        x_out = refs[4 + nx:4 + 2 * nx]
        x_sems = refs[4 + 2 * nx:]
        if exchange:
            @pl.when(pl.program_id(0) == 0)
            def _():
                exchange.start(x_in, x_out, x_sems)

        x = h_ref[...]
        r = lax.rsqrt(jnp.mean(x * x, axis=-1, keepdims=True) + RMS_EPS)
        y = x * r * g_ref[...]
        o_ref[...] = (y * (1.0 + m_ref[i_scale:i_scale + 1, :]) + m_ref[i_shift:i_shift + 1, :]).astype(BF16)

        if exchange:
            @pl.when(pl.program_id(0) == nt - 1)
            def _():
                exchange.finish(x_in, x_out, x_sems)

    hbm = pl.BlockSpec(memory_space=pltpu.HBM)
    res = pl.pallas_call(
        body, name=name, grid=(nt,),
        in_specs=[pl.BlockSpec((te, D), lambda i: (i, 0)),
                  pl.BlockSpec((1, D), lambda i: (0, 0)),
                  pl.BlockSpec((N_MOD, D), lambda i: (0, 0))] + [hbm] * nx,
        out_specs=[pl.BlockSpec((te, D), lambda i: (i, 0))] + [hbm] * nx,
        out_shape=[jax.ShapeDtypeStruct((n, D), BF16)] + (exchange.out_shapes if exchange else []),
        scratch_shapes=exchange.scratch if exchange else [],
        compiler_params=_params(("arbitrary",) if exchange else ("parallel",)),
    )(h, g, mods, *(exchange.arrays if exchange else []))
    return (res[0], list(res[1:])) if exchange else res[0]


def loss_head(x, g, target, prev, name):
    n = x.shape[0]
    te = _tile(n, (256,))
    i_gate, coef = prev[2], prev[3]

    def body(x_ref, g_ref, t_ref, y_ref, m_ref, dx_ref, loss_ref, dg_ref, dy_ref, dgate_ref):
        @pl.when(pl.program_id(0) == 0)
        def _():
            loss_ref[...] = jnp.zeros_like(loss_ref)
            dg_ref[...] = jnp.zeros_like(dg_ref)
            dgate_ref[...] = jnp.zeros_like(dgate_ref)

        xv = x_ref[...]
        gv = g_ref[...]
        r = lax.rsqrt(jnp.mean(xv * xv, axis=-1, keepdims=True) + RMS_EPS)
        xhat = xv * r
        e = xhat * gv - t_ref[...]
        per_tok = jnp.mean(e * e, axis=-1, keepdims=True)
        loss_ref[...] += 0.5 * jnp.sum(per_tok, axis=0, keepdims=True)
        dy = e * (1.0 / D)
        dg_ref[...] += jnp.sum(dy * xhat, axis=0, keepdims=True)
        dxhat = dy * gv
        dx = r * (dxhat - xhat * jnp.mean(dxhat * xhat, axis=-1, keepdims=True))
        dx_ref[...] = dx
        dy_ref[...] = (dx * (coef * m_ref[i_gate:i_gate + 1, :])).astype(BF16)
        dgate_ref[...] += coef * jnp.sum(dx * y_ref[...].astype(F32), axis=0, keepdims=True)

    row = pl.BlockSpec((te, D), lambda i: (i, 0))
    vec = pl.BlockSpec((1, D), lambda i: (0, 0))
    return pl.pallas_call(
        body, name=name, grid=(n // te,),
        in_specs=[row, vec, row, row, pl.BlockSpec((N_MOD, D), lambda i: (0, 0))],
        out_specs=[row, pl.BlockSpec((1, 128), lambda i: (0, 0)), vec, row, vec],
        out_shape=[jax.ShapeDtypeStruct((n, D), F32), jax.ShapeDtypeStruct((1, 128), F32),
                   jax.ShapeDtypeStruct((1, D), F32), jax.ShapeDtypeStruct((n, D), BF16),
                   jax.ShapeDtypeStruct((1, D), F32)],
        compiler_params=_params(("arbitrary",)),
    )(x, g, target, prev[0], prev[1])


def ffn_up(hn, w13, name, exchange=None):
    n = hn.shape[0]
    ff = w13.shape[1] // 2
    tm = _tile(n, (512, 256))
    tn = _tile(ff, (1408, 512, 256, 128))
    nj = ff // tn
    ni = n // tm
    nx = exchange.n if exchange else 0

    def body(*refs):
        h_ref, wa_ref, wb_ref = refs[:3]
        x_in = refs[3:3 + nx]
        p_ref, u_ref, s_ref = refs[3 + nx:6 + nx]
        x_out = refs[6 + nx:6 + 2 * nx]
        x_sems = refs[6 + 2 * nx:]
        if exchange:
            @pl.when((pl.program_id(0) == 0) & (pl.program_id(1) == 0))
            def _():
                exchange.start(x_in, x_out, x_sems)

        hv = h_ref[...]
        a = _dot(hv, wa_ref[...])
        b = _dot(hv, wb_ref[...])
        sig = _sigmoid(a)
        p = a * sig
        p_ref[...] = p.astype(BF16)
        u_ref[...] = (b * (sig * (1.0 + a * (1.0 - sig)))).astype(BF16)
        s_ref[...] = (p * b).astype(BF16)

        if exchange:
            @pl.when((pl.program_id(0) == nj - 1) & (pl.program_id(1) == ni - 1))
            def _():
                exchange.finish(x_in, x_out, x_sems)

    out = pl.BlockSpec((tm, tn), lambda j, i: (i, j))
    hbm = pl.BlockSpec(memory_space=pltpu.HBM)
    sem = ("arbitrary", "arbitrary") if exchange else ("parallel", "parallel")
    res = pl.pallas_call(
        body, name=name, grid=(nj, ni),
        in_specs=[pl.BlockSpec((tm, D), lambda j, i: (i, 0)),
                  pl.BlockSpec((D, tn), lambda j, i: (0, j)),
                  pl.BlockSpec((D, tn), lambda j, i: (0, j + nj))] + [hbm] * nx,
        out_specs=[out, out, out] + [hbm] * nx,
        out_shape=[jax.ShapeDtypeStruct((n, ff), BF16)] * 3 + (exchange.out_shapes if exchange else []),
        scratch_shapes=exchange.scratch if exchange else [],
        compiler_params=_params(sem),
    )(hn, w13, w13, *(exchange.arrays if exchange else []))
    return res[:3], list(res[3:])


def mm_nn(a_list, w, row_offs, name, out_dtype=BF16, res=None, nxt=None):
    n = a_list[0].shape[0]
    nout = w.shape[1]
    ks = [a.shape[1] for a in a_list]
    tm = _tile(n, (512, 256) if res is not None else (1024, 512, 256))
    tn = _tile(nout, (1024, 512, 256, 128))
    na = len(a_list)
    assert nxt is None or (res is not None and tn == D)

    def body(*refs):
        a_refs = refs[:na]
        w_refs = refs[na:2 * na]
        acc = _dot(a_refs[0][...], w_refs[0][...])
        for k in range(1, na):
            acc += _dot(a_refs[k][...], w_refs[k][...])
        if res is None:
            refs[2 * na][...] = acc.astype(out_dtype)
        else:
            h_ref, m_ref = refs[2 * na:2 * na + 2]
            i_gate, coef = res[2], res[3]
            h_new = h_ref[...] + (coef * m_ref[i_gate:i_gate + 1, :]) * acc
            if nxt is None:
                hn_ref, y_ref = refs[2 * na + 2:]
            else:
                g2_ref, m2_ref, hn_ref, y_ref, nx_ref = refs[2 * na + 2:]
                r = lax.rsqrt(jnp.mean(h_new * h_new, axis=-1, keepdims=True) + RMS_EPS)
                nx_ref[...] = ((h_new * r * g2_ref[...]) * (1.0 + m2_ref[nxt[3]:nxt[3] + 1, :])
                               + m2_ref[nxt[2]:nxt[2] + 1, :]).astype(BF16)
            hn_ref[...] = h_new
            y_ref[...] = acc.astype(BF16)

    in_specs = [pl.BlockSpec((tm, k), lambda j, i: (i, 0)) for k in ks]
    for k, off in zip(ks, row_offs):
        in_specs.append(pl.BlockSpec((k, tn), functools.partial(lambda j, i, ob: (ob, j), ob=off // k)))
    args = list(a_list) + [w] * na
    out = pl.BlockSpec((tm, tn), lambda j, i: (i, j))
    if res is None:
        out_specs = out
        out_shape = jax.ShapeDtypeStruct((n, nout), out_dtype)
    else:
        in_specs += [out, pl.BlockSpec((N_MOD, tn), lambda j, i: (0, j))]
        args += [res[0], res[1]]
        out_specs = [out, out]
        out_shape = [jax.ShapeDtypeStruct((n, nout), F32), jax.ShapeDtypeStruct((n, nout), BF16)]
        if nxt is not None:
            in_specs += [pl.BlockSpec((1, D), lambda j, i: (0, 0)), pl.BlockSpec((N_MOD, D), lambda j, i: (0, 0))]
            args += [nxt[0], nxt[1]]
            out_specs.append(out)
            out_shape.append(jax.ShapeDtypeStruct((n, nout), BF16))
    return pl.pallas_call(
        body, name=name, grid=(nout // tn, n // tm),
        in_specs=in_specs, out_specs=out_specs, out_shape=out_shape,
        compiler_params=_params(("parallel", "parallel")),
    )(*args)


def mm_nt(g, w, name, dswiglu=None):
    n, kg = g.shape
    nout = w.shape[0]
    tm = _tile(n, (512, 256) if dswiglu is not None else (1024, 512, 256))
    tn = _tile(nout, (1408, 1024, 512, 256, 128))

    def body(*refs):
        r = _dot_nt(refs[0][...], refs[1][...])
        if dswiglu is None:
            refs[2][...] = r.astype(BF16)
        else:
            u_ref, p_ref, da_ref, db_ref = refs[2:]
            da_ref[...] = (r * u_ref[...].astype(F32)).astype(BF16)
            db_ref[...] = (r * p_ref[...].astype(F32)).astype(BF16)

    out = pl.BlockSpec((tm, tn), lambda j, i: (i, j))
    in_specs = [pl.BlockSpec((tm, kg), lambda j, i: (i, 0)), pl.BlockSpec((tn, kg), lambda j, i: (j, 0))]
    args = [g, w]
    if dswiglu is None:
        out_specs = out
        out_shape = jax.ShapeDtypeStruct((n, nout), BF16)
    else:
        in_specs += [out, out]
        args += list(dswiglu)
        out_specs = [out, out]
        out_shape = [jax.ShapeDtypeStruct((n, nout), BF16)] * 2
    return pl.pallas_call(
        body, name=name, grid=(nout // tn, n // tm),
        in_specs=in_specs, out_specs=out_specs, out_shape=out_shape,
        compiler_params=_params(("parallel", "parallel")),
    )(*args)


def mm_nt_norm(g_list, w, col_offs, h, g, mods, i_scale, dres, name, prev=None, exchange=None, swiglu=None):
    n = h.shape[0]
    fused = swiglu is not None
    kg = swiglu[2].shape[1] if fused else g_list[0].shape[1]
    tm = _tile(n, (256,)) if fused else _tile(n, (512, 256))
    tk = _tile(kg, (1408, 1024, 512, 256, 128))
    ng = 2 if fused else len(g_list)
    nk = kg // tk
    ni = n // tm
    has_res = dres is not None
    nx = exchange.n if exchange else 0

    def body(*refs):
        if fused:
            dyin_ref, w2_ref, u_ref, p_ref = refs[:4]
            w_refs = refs[4:6]
            pos = 6
        else:
            g_refs = refs[:ng]
            w_refs = refs[ng:2 * ng]
            pos = 2 * ng
        h_ref, gv_ref, m_ref = refs[pos:pos + 3]
        pos += 3
        if has_res:
            dres_ref = refs[pos]
            pos += 1
        if prev is not None:
            y_ref, mp_ref = refs[pos:pos + 2]
            pos += 2
        x_in = refs[pos:pos + nx]
        pos += nx
        dh_ref, dshift_ref, dscale_ref, dg_ref = refs[pos:pos + 4]
        pos += 4
        if prev is not None:
            dy_ref, dgate_ref = refs[pos:pos + 2]
            pos += 2
        if fused:
            da_ref, db_ref = refs[pos:pos + 2]
            pos += 2
        x_out = refs[pos:pos + nx]
        pos += nx
        acc_ref = refs[pos]
        x_sems = refs[pos + 1:]
        i = pl.program_id(0)
        k = pl.program_id(1)

        @pl.when((i == 0) & (k == 0))
        def _():
            if exchange:
                exchange.start(x_in, x_out, x_sems)
            dshift_ref[...] = jnp.zeros_like(dshift_ref)
            dscale_ref[...] = jnp.zeros_like(dscale_ref)
            dg_ref[...] = jnp.zeros_like(dg_ref)
            if prev is not None:
                dgate_ref[...] = jnp.zeros_like(dgate_ref)

        @pl.when(k == 0)
        def _():
            acc_ref[...] = jnp.zeros_like(acc_ref)

        if fused:
            ds = _dot_nt(dyin_ref[...], w2_ref[...])
            da = (ds * u_ref[...].astype(F32)).astype(BF16)
            db = (ds * p_ref[...].astype(F32)).astype(BF16)
            da_ref[...] = da
            db_ref[...] = db
            operands = (da, db)
        else:
            operands = [r[...] for r in g_refs]
        acc = _dot_nt(operands[0], w_refs[0][...])
        for q in range(1, ng):
            acc += _dot_nt(operands[q], w_refs[q][...])
        acc_ref[...] += acc

        @pl.when(k == nk - 1)
        def _():
            d = acc_ref[...]
            x = h_ref[...]
            gv = gv_ref[...]
            r = lax.rsqrt(jnp.mean(x * x, axis=-1, keepdims=True) + RMS_EPS)
            xhat = x * r
            one_scale = 1.0 + m_ref[i_scale:i_scale + 1, :]
            t = d * xhat
            tsum = jnp.sum(t, axis=0, keepdims=True)
            dshift_ref[...] += jnp.sum(d, axis=0, keepdims=True)
            dscale_ref[...] += gv * tsum
            dg_ref[...] += one_scale * tsum
            cvec = one_scale * gv
            dh = r * (d * cvec - xhat * jnp.mean(t * cvec, axis=-1, keepdims=True))
            if has_res:
                dh = dh + dres_ref[...]
            dh_ref[...] = dh
            if prev is not None:
                i_gate, coef = prev[2], prev[3]
                dy_ref[...] = (dh * (coef * mp_ref[i_gate:i_gate + 1, :])).astype(BF16)
                dgate_ref[...] += coef * jnp.sum(dh * y_ref[...].astype(F32), axis=0, keepdims=True)

        if exchange:
            @pl.when((i == ni - 1) & (k == nk - 1))
            def _():
                exchange.finish(x_in, x_out, x_sems)

    row = pl.BlockSpec((tm, D), lambda i, k: (i, 0))
    vec = pl.BlockSpec((1, D), lambda i, k: (0, 0))
    modspec = pl.BlockSpec((N_MOD, D), lambda i, k: (0, 0))
    kblock = pl.BlockSpec((tm, tk), lambda i, k: (i, k))
    if fused:
        in_specs = [row, pl.BlockSpec((tk, D), lambda i, k: (k, 0)), kblock, kblock]
        args = list(swiglu)
    else:
        in_specs = [kblock for _ in g_list]
        args = list(g_list)
    for off in col_offs:
        in_specs.append(pl.BlockSpec((D, tk), functools.partial(lambda i, k, ob: (0, ob + k), ob=off // tk)))
    in_specs += [row, vec, modspec]
    args += [w] * ng + [h, g, mods]
    out_specs = [row, vec, vec, vec]
    out_shape = [jax.ShapeDtypeStruct((n, D), F32)] + [jax.ShapeDtypeStruct((1, D), F32)] * 3
    if has_res:
        in_specs.append(row)
        args.append(dres)
    if prev is not None:
        in_specs += [row, modspec]
        args += [prev[0], prev[1]]
        out_specs += [row, vec]
        out_shape += [jax.ShapeDtypeStruct((n, D), BF16), jax.ShapeDtypeStruct((1, D), F32)]
    if fused:
        out_specs += [kblock, kblock]
        out_shape += [jax.ShapeDtypeStruct((n, kg), BF16)] * 2
    scratch = [pltpu.VMEM((tm, D), F32)]
    if exchange:
        hbm = pl.BlockSpec(memory_space=pltpu.HBM)
        in_specs += [hbm] * nx
        args += exchange.arrays
        out_specs += [hbm] * nx
        out_shape += exchange.out_shapes
        scratch += exchange.scratch
    return pl.pallas_call(
        body, name=name, grid=(ni, nk),
        in_specs=in_specs, out_specs=out_specs, out_shape=out_shape, scratch_shapes=scratch,
        compiler_params=_params(("arbitrary", "arbitrary")),
    )(*args)


def mm_tn(a, g, name, acc=None):
    n, ka = a.shape
    ngc = g.shape[1]
    tka = _tile(ka, (1408, 1024, 512, 256, 128))
    tng = _tile(ngc, (1408, 1024, 512, 256, 128))
    tr = _tile(n, (2048, 1024, 512, 256))
    has_acc = acc is not None

    def body(*refs):
        a_ref, g_ref = refs[0], refs[1]
        o_ref = refs[-1]
        r = pl.program_id(2)

        @pl.when(r == 0)
        def _():
            if has_acc:
                o_ref[...] = refs[2][...]
            else:
                o_ref[...] = jnp.zeros_like(o_ref)

        o_ref[...] += _dot_tn(a_ref[...], g_ref[...])

    out = pl.BlockSpec((tka, tng), lambda p, q, r: (p, q))
    in_specs = [pl.BlockSpec((tr, tka), lambda p, q, r: (r, p)),
                pl.BlockSpec((tr, tng), lambda p, q, r: (r, q))]
    args = [a, g]
    if has_acc:
        in_specs.append(out)
        args.append(acc)
    return pl.pallas_call(
        body, name=name, grid=(ka // tka, ngc // tng, n // tr),
        in_specs=in_specs, out_specs=out,
        out_shape=jax.ShapeDtypeStruct((ka, ngc), F32),
        compiler_params=_params(("parallel", "parallel", "arbitrary")),
    )(*args)


def mm_small(a, b, name, trans_b=False):
    m = a.shape[0]
    nout = b.shape[0] if trans_b else b.shape[1]

    def body(a_ref, b_ref, o_ref):
        if trans_b:
            o_ref[...] = lax.dot_general(a_ref[...], b_ref[...], NT_DIMS, precision=HI, preferred_element_type=F32)
        else:
            o_ref[...] = jnp.dot(a_ref[...], b_ref[...], precision=HI, preferred_element_type=F32)

    return pl.pallas_call(
        body, name=name,
        out_shape=jax.ShapeDtypeStruct((m, nout), F32),
        compiler_params=pltpu.CompilerParams(vmem_limit_bytes=VMEM_LIMIT),
    )(a, b)


def _col_tables():
    col = np.arange(GRID_W)
    start = np.clip(col - KW // 2, 0, GRID_W - KW)
    ok = (col[None, :] >= start[:, None]) & (col[None, :] < start[:, None] + KW)
    ci = np.clip(col[None, :] - col[:, None] + (KW - 1), 0, 2 * KW - 2)
    e = np.zeros((2 * KW - 1, GRID_W, GRID_W), np.float32)
    for c in range(2 * KW - 1):
        e[c] = (ci == c) & ok
    return e.reshape(2 * KW - 1, GRID_W * GRID_W), ok


def bias_table(rpb):
    e, ok = _col_tables()
    e_pad = np.zeros((32, GRID_W * GRID_W), np.float32)
    e_pad[:31] = e
    rp = jnp.pad(rpb.reshape(HEADS * 15, 31), ((0, 0), (0, 1)))
    t = mm_small(rp, jnp.asarray(e_pad), "rpb_expand").reshape(HEADS, 15, GRID_W, GRID_W)
    t = jnp.where(jnp.asarray(ok)[None, None], t, NEG_INF)
    tab = jnp.stack([t[:, v:v + KH] for v in range(8)], axis=0)
    return tab.transpose(0, 1, 3, 2, 4).reshape(TAB_SHAPE)


def bias_table_bwd(dtab):
    e, _ = _col_tables()
    e_pad = np.zeros((128, GRID_W * GRID_W), np.float32)
    e_pad[:31] = e
    d = dtab.reshape(8, HEADS, GRID_W, KH, GRID_W).transpose(0, 1, 3, 2, 4).reshape(8 * HEADS * KH, GRID_W * GRID_W)
    gv = mm_small(d, jnp.asarray(e_pad), "rpb_reduce", trans_b=True)[:, :31]
    gv = gv.reshape(8, HEADS, KH, 31).transpose(0, 2, 1, 3).reshape(8 * KH, HEADS * 31)
    sel = np.zeros((16, 8 * KH), np.float32)
    for v in range(8):
        for j in range(KH):
            sel[v + j, v * KH + j] = 1.0
    gpad = jnp.pad(gv, ((0, 0), (0, 256 - HEADS * 31)))
    out = mm_small(jnp.asarray(sel), gpad, "rpb_fold")[:15, :HEADS * 31]
    return out.reshape(15, HEADS, 31).transpose(1, 0, 2)


def _attn_geometry(seq):
    rows = seq // GRID_W
    nb = rows // QROWS
    return rows, nb


def _stack_heads(t2):
    first = (lax.broadcasted_iota(jnp.int32, (1, 128), 1) // HEAD_DIM) == 0
    zero = jnp.zeros_like(t2)
    return jnp.concatenate([jnp.where(first, t2, zero), jnp.where(first, zero, t2)], axis=0)


def _unstack_heads(t):
    first = (lax.broadcasted_iota(jnp.int32, (1, 128), 1) // HEAD_DIM) == 0
    return jnp.where(first, t[0:GRID_W], t[GRID_W:2 * GRID_W])


TAB_SHAPE = (8, HEADS // 2, 2 * GRID_W, KH * GRID_W)


def attn_fwd(qkvu, qkvu_c, tab, name, exchange=None):
    seq = qkvu.shape[0]
    nctx = qkvu_c.shape[0]
    rows, nb = _attn_geometry(seq)
    qt = QROWS * GRID_W
    wt = WROWS * GRID_W
    scale = HEAD_DIM ** -0.5
    nx = exchange.n if exchange else 0

    def wb0(i):
        return jnp.clip(i - 1, 0, nb - 3)

    def body(*refs):
        q_ref, k0, k1, k2, v0, v1, v2, kc_ref, vc_ref, tab_hbm = refs[:10]
        x_in = refs[10:10 + nx]
        o_ref = refs[10 + nx]
        x_out = refs[11 + nx:11 + 2 * nx]
        kbuf, vbuf, tab_s, sem = refs[11 + 2 * nx:15 + 2 * nx]
        x_sems = refs[15 + 2 * nx:]
        i = pl.program_id(0)

        @pl.when(i == 0)
        def _():
            if exchange:
                exchange.start(x_in, x_out, x_sems)
            cp = pltpu.make_async_copy(tab_hbm, tab_s, sem)
            cp.start()
            cp.wait()

        for t, (kr, vr) in enumerate(((k0, v0), (k1, v1), (k2, v2))):
            kbuf[t * qt:(t + 1) * qt, :] = kr[...]
            vbuf[t * qt:(t + 1) * qt, :] = vr[...]
        base = wb0(i) * QROWS

        def row_body(rl, carry):
            r = i * QROWS + rl
            rs = jnp.clip(r - KH // 2, 0, rows - KH)
            vi = rs - r + (KH - 1)
            off = pl.multiple_of((rs - base) * GRID_W, GRID_W)
            qoff = pl.multiple_of(rl * GRID_W, GRID_W)
            for p in range(HEADS // 2):
                ls = slice(p * 128, (p + 1) * 128)
                qst = _stack_heads(q_ref[pl.ds(qoff, GRID_W), ls])
                k2v = kbuf[pl.ds(off, KH * GRID_W), ls]
                v2v = vbuf[pl.ds(off, KH * GRID_W), ls]
                s_w = _dot_nt(qst, k2v) * scale + tab_s[vi, p]
                s_c = _dot_nt(qst, kc_ref[:, ls]) * scale
                m = jnp.maximum(jnp.max(s_w, axis=-1, keepdims=True), jnp.max(s_c, axis=-1, keepdims=True))
                pw = jnp.exp(s_w - m)
                pc = jnp.exp(s_c - m)
                l = jnp.sum(pw, axis=-1, keepdims=True) + jnp.sum(pc, axis=-1, keepdims=True)
                o = _dot(pw.astype(BF16), v2v) + _dot(pc.astype(BF16), vc_ref[:, ls])
                o_ref[pl.ds(qoff, GRID_W), ls] = _unstack_heads(o * (1.0 / l)).astype(BF16)
            return carry

        lax.fori_loop(0, QROWS, row_body, 0)

        if exchange:
            @pl.when(i == nb - 1)
            def _():
                exchange.finish(x_in, x_out, x_sems)

    blk = lambda col: [pl.BlockSpec((qt, NA_W), functools.partial(lambda i, t, c: (wb0(i) + t, c), t=t, c=col))
                       for t in range(3)]
    hbm = pl.BlockSpec(memory_space=pltpu.HBM)
    res = pl.pallas_call(
        body, name=name, grid=(nb,),
        in_specs=[pl.BlockSpec((qt, NA_W), lambda i: (i, 0))] + blk(1) + blk(2)
                 + [pl.BlockSpec((nctx, NA_W), lambda i: (0, 1)), pl.BlockSpec((nctx, NA_W), lambda i: (0, 2)),
                    pl.BlockSpec(memory_space=pl.ANY)] + [hbm] * nx,
        out_specs=[pl.BlockSpec((qt, NA_W), lambda i: (i, 0))] + [hbm] * nx,
        out_shape=[jax.ShapeDtypeStruct((seq, NA_W), BF16)] + (exchange.out_shapes if exchange else []),
        scratch_shapes=[pltpu.VMEM((wt, NA_W), BF16), pltpu.VMEM((wt, NA_W), BF16),
                        pltpu.VMEM(TAB_SHAPE, F32), pltpu.SemaphoreType.DMA] + (exchange.scratch if exchange else []),
        compiler_params=_params(("arbitrary",)),
    )(qkvu, qkvu, qkvu, qkvu, qkvu, qkvu, qkvu, qkvu_c, qkvu_c, tab, *(exchange.arrays if exchange else []))
    return res[0], list(res[1:])


def attn_bwd(qkvu, qkvu_c, tab, dmix, name, exchange=None):
    seq = qkvu.shape[0]
    nctx = qkvu_c.shape[0]
    rows, nb = _attn_geometry(seq)
    qt = QROWS * GRID_W
    wt = WROWS * GRID_W
    scale = HEAD_DIM ** -0.5
    nx = exchange.n if exchange else 0

    def wb0(i):
        return jnp.clip(i - 1, 0, nb - 3)

    def body(*refs):
        q_ref, k0, k1, k2, v0, v1, v2, kc_ref, vc_ref, do_ref, tab_hbm = refs[:11]
        x_in = refs[11:11 + nx]
        dq_ref, dk_hbm, dv_hbm, dkc_ref, dvc_ref, dtab_hbm = refs[11 + nx:17 + nx]
        x_out = refs[17 + nx:17 + 2 * nx]
        kbuf, vbuf, dkacc, dvacc, tab_s, dtab_s, stage, sem = refs[17 + 2 * nx:25 + 2 * nx]
        x_sems = refs[25 + 2 * nx:]
        i = pl.program_id(0)

        if exchange:
            @pl.when(i == 0)
            def _():
                exchange.start(x_in, x_out, x_sems)

        def flush(src, dst, block, dst_row):
            stage[...] = src[block * qt:(block + 1) * qt, :].astype(BF16)
            cp = pltpu.make_async_copy(stage, dst.at[pl.ds(dst_row, qt)], sem)
            cp.start()
            cp.wait()

        @pl.when(i == 0)
        def _():
            cp = pltpu.make_async_copy(tab_hbm, tab_s, sem)
            cp.start()
            cp.wait()
            dtab_s[...] = jnp.zeros_like(dtab_s)
            dkacc[...] = jnp.zeros_like(dkacc)
            dvacc[...] = jnp.zeros_like(dvacc)
            dkc_ref[...] = jnp.zeros_like(dkc_ref)
            dvc_ref[...] = jnp.zeros_like(dvc_ref)

        @pl.when((i >= 2) & (i <= nb - 2))
        def _():
            dst_row = pl.multiple_of((i - 2) * qt, qt)
            for acc_ref, dst in ((dkacc, dk_hbm), (dvacc, dv_hbm)):
                flush(acc_ref, dst, 0, dst_row)
                acc_ref[0:qt, :] = acc_ref[qt:2 * qt, :]
                acc_ref[qt:2 * qt, :] = acc_ref[2 * qt:3 * qt, :]
                acc_ref[2 * qt:3 * qt, :] = jnp.zeros((qt, NA_W), F32)

        for t, (kr, vr) in enumerate(((k0, v0), (k1, v1), (k2, v2))):
            kbuf[t * qt:(t + 1) * qt, :] = kr[...]
            vbuf[t * qt:(t + 1) * qt, :] = vr[...]
        base = wb0(i) * QROWS

        def row_body(rl, carry):
            r = i * QROWS + rl
            rs = jnp.clip(r - KH // 2, 0, rows - KH)
            vi = rs - r + (KH - 1)
            off = pl.multiple_of((rs - base) * GRID_W, GRID_W)
            qoff = pl.multiple_of(rl * GRID_W, GRID_W)
            for p in range(HEADS // 2):
                ls = slice(p * 128, (p + 1) * 128)
                qst = _stack_heads(q_ref[pl.ds(qoff, GRID_W), ls])
                dost = _stack_heads(do_ref[pl.ds(qoff, GRID_W), ls])
                k2v = kbuf[pl.ds(off, KH * GRID_W), ls]
                v2v = vbuf[pl.ds(off, KH * GRID_W), ls]
                kc2 = kc_ref[:, ls]
                vc2 = vc_ref[:, ls]
                s_w = _dot_nt(qst, k2v) * scale + tab_s[vi, p]
                s_c = _dot_nt(qst, kc2) * scale
                m = jnp.maximum(jnp.max(s_w, axis=-1, keepdims=True), jnp.max(s_c, axis=-1, keepdims=True))
                pw = jnp.exp(s_w - m)
                pc = jnp.exp(s_c - m)
                inv = 1.0 / (jnp.sum(pw, axis=-1, keepdims=True) + jnp.sum(pc, axis=-1, keepdims=True))
                pw = pw * inv
                pc = pc * inv
                dpw = _dot_nt(dost, v2v)
                dpc = _dot_nt(dost, vc2)
                delta = jnp.sum(pw * dpw, axis=-1, keepdims=True) + jnp.sum(pc * dpc, axis=-1, keepdims=True)
                ds_w = pw * (dpw - delta)
                ds_c = pc * (dpc - delta)
                dtab_s[vi, p] += ds_w
                dsw16 = ds_w.astype(BF16)
                dsc16 = ds_c.astype(BF16)
                dq = (_dot(dsw16, k2v) + _dot(dsc16, kc2)) * scale
                dq_ref[pl.ds(qoff, GRID_W), ls] = _unstack_heads(dq).astype(BF16)
                dkacc[pl.ds(off, KH * GRID_W), ls] += _dot_tn(dsw16, qst) * scale
                dvacc[pl.ds(off, KH * GRID_W), ls] += _dot_tn(pw.astype(BF16), dost)
                dkc_ref[:, ls] += _dot_tn(dsc16, qst) * scale
                dvc_ref[:, ls] += _dot_tn(pc.astype(BF16), dost)
            return carry

        lax.fori_loop(0, QROWS, row_body, 0)

        @pl.when(i == nb - 1)
        def _():
            for t in range(3):
                dst_row = (nb - 3 + t) * qt
                flush(dkacc, dk_hbm, t, dst_row)
                flush(dvacc, dv_hbm, t, dst_row)
            cp = pltpu.make_async_copy(dtab_s, dtab_hbm, sem)
            cp.start()
            cp.wait()
            if exchange:
                exchange.finish(x_in, x_out, x_sems)

    blk = lambda col: [pl.BlockSpec((qt, NA_W), functools.partial(lambda i, t, c: (wb0(i) + t, c), t=t, c=col))
                       for t in range(3)]
    any_spec = pl.BlockSpec(memory_space=pl.ANY)
    hbm = pl.BlockSpec(memory_space=pltpu.HBM)
    res = pl.pallas_call(
        body, name=name, grid=(nb,),
        in_specs=[pl.BlockSpec((qt, NA_W), lambda i: (i, 0))] + blk(1) + blk(2)
                 + [pl.BlockSpec((nctx, NA_W), lambda i: (0, 1)), pl.BlockSpec((nctx, NA_W), lambda i: (0, 2)),
                    pl.BlockSpec((qt, NA_W), lambda i: (i, 0)), any_spec] + [hbm] * nx,
        out_specs=[pl.BlockSpec((qt, NA_W), lambda i: (i, 0)), any_spec, any_spec,
                   pl.BlockSpec((nctx, NA_W), lambda i: (0, 0)), pl.BlockSpec((nctx, NA_W), lambda i: (0, 0)),
                   any_spec] + [hbm] * nx,
        out_shape=[jax.ShapeDtypeStruct((seq, NA_W), BF16), jax.ShapeDtypeStruct((seq, NA_W), BF16),
                   jax.ShapeDtypeStruct((seq, NA_W), BF16), jax.ShapeDtypeStruct((nctx, NA_W), F32),
                   jax.ShapeDtypeStruct((nctx, NA_W), F32), jax.ShapeDtypeStruct(TAB_SHAPE, F32)]
                  + (exchange.out_shapes if exchange else []),
        scratch_shapes=[pltpu.VMEM((wt, NA_W), BF16), pltpu.VMEM((wt, NA_W), BF16),
                        pltpu.VMEM((wt, NA_W), F32), pltpu.VMEM((wt, NA_W), F32),
                        pltpu.VMEM(TAB_SHAPE, F32), pltpu.VMEM(TAB_SHAPE, F32), pltpu.VMEM((qt, NA_W), BF16),
                        pltpu.SemaphoreType.DMA]
                       + (exchange.scratch if exchange else []),
        compiler_params=_params(("arbitrary",)),
    )(qkvu, qkvu, qkvu, qkvu, qkvu, qkvu, qkvu, qkvu_c, qkvu_c, dmix, tab, *(exchange.arrays if exchange else []))
    return res[:6], list(res[6:])


def _halo_specs(te, seq, col, width):
    per = te // HALO
    last = seq // HALO - 1
    return [pl.BlockSpec((HALO, width), lambda i: (jnp.maximum(i * per - 1, 0), col)),
            pl.BlockSpec((te, width), lambda i: (i, col)),
            pl.BlockSpec((HALO, width), lambda i: (jnp.minimum((i + 1) * per, last), col))]


def _extended(prev_ref, cur_ref, next_ref, i, te, seq):
    xe = jnp.concatenate([prev_ref[...], cur_ref[...], next_ref[...]], axis=0).astype(F32)
    pos = i * te - HALO + lax.broadcasted_iota(jnp.int32, (te + 2 * HALO, 1), 0)
    return jnp.where((pos >= 0) & (pos < seq), xe, 0.0), pos


def _window_sum(x, levels, n, mirrored):
    first = (n - 1) if mirrored else 1
    acc = x + pltpu.roll(x, first, 0)
    step = 1
    for _ in range(levels - 1):
        acc = pltpu.roll(acc, step, 0) + pltpu.roll(acc, n - step, 0)
        step *= 2
    return acc


def _window_count(pos, w, seq):
    lo = jnp.clip(pos - w // 2, 0, seq)
    hi = jnp.clip(pos - w // 2 + w, 0, seq)
    return jnp.maximum(hi - lo, 1).astype(F32)


def pool_fwd(qkvu, pool_w, pool_scale, name):
    seq = qkvu.shape[0]
    te = _tile(seq, (512, 256))
    n = te + 2 * HALO

    def body(up_ref, uc_ref, un_ref, w_ref, sc_ref, o_ref):
        i = pl.program_id(0)
        xe, pos = _extended(up_ref, uc_ref, un_ref, i, te, seq)
        cnt = pos[HALO:HALO + te]
        for g, w in enumerate(POOL_WINDOWS):
            ls = slice(g * POOL_G, (g + 1) * POOL_G)
            xg = xe[:, ls]
            win = _window_sum(xg, g + 1, n, False)[HALO:HALO + te]
            dlt = win / _window_count(cnt, w, seq) - xg[HALO:HALO + te]
            z = _dot(dlt.astype(BF16), w_ref[g])
            o_ref[:, ls] = (z * sc_ref[:, ls]).astype(BF16)

    return pl.pallas_call(
        body, name=name, grid=(seq // te,),
        in_specs=_halo_specs(te, seq, 3, POOL_W)
                 + [pl.BlockSpec((4, POOL_G, POOL_G), lambda i: (0, 0, 0)), pl.BlockSpec((1, POOL_W), lambda i: (0, 0))],
        out_specs=pl.BlockSpec((te, POOL_W), lambda i: (i, 0)),
        out_shape=jax.ShapeDtypeStruct((seq, POOL_W), BF16),
        compiler_params=_params(("parallel",)),
    )(qkvu, qkvu, qkvu, pool_w, pool_scale)


def pool_bwd(qkvu, dmix, pool_w, pool_scale, name):
    seq = qkvu.shape[0]
    te = _tile(seq, (512, 256))
    n = te + 2 * HALO

    def body(up_ref, uc_ref, un_ref, dp_ref, dc_ref, dn_ref, w_ref, sc_ref, du_ref, dw_ref, dsc_ref):
        i = pl.program_id(0)

        @pl.when(i == 0)
        def _():
            dw_ref[...] = jnp.zeros_like(dw_ref)
            dsc_ref[...] = jnp.zeros_like(dsc_ref)

        xe, pos = _extended(up_ref, uc_ref, un_ref, i, te, seq)
        de, _ = _extended(dp_ref, dc_ref, dn_ref, i, te, seq)
        cpos = pos[HALO:HALO + te]
        for g, w in enumerate(POOL_WINDOWS):
            ls = slice(g * POOL_G, (g + 1) * POOL_G)
            xg = xe[:, ls]
            wg = w_ref[g]
            win = _window_sum(xg, g + 1, n, False)[HALO:HALO + te]
            dlt = (win / _window_count(cpos, w, seq) - xg[HALO:HALO + te]).astype(BF16)
            z = _dot(dlt, wg)
            dpg = de[:, ls]
            dsc_ref[:, ls] += jnp.sum(dpg[HALO:HALO + te] * z, axis=0, keepdims=True)
            dz = (dpg * sc_ref[:, ls]).astype(BF16)
            dw_ref[g] += _dot_tn(dlt, dz[HALO:HALO + te])
            dd = _dot_nt(dz, wg)
            back = _window_sum(dd / _window_count(pos, w, seq), g + 1, n, True)
            du_ref[:, ls] = (back[HALO:HALO + te] - dd[HALO:HALO + te]).astype(BF16)

    return pl.pallas_call(
        body, name=name, grid=(seq // te,),
        in_specs=_halo_specs(te, seq, 3, POOL_W) + _halo_specs(te, seq, 1, POOL_W)
                 + [pl.BlockSpec((4, POOL_G, POOL_G), lambda i: (0, 0, 0)), pl.BlockSpec((1, POOL_W), lambda i: (0, 0))],
        out_specs=[pl.BlockSpec((te, POOL_W), lambda i: (i, 0)),
                   pl.BlockSpec((4, POOL_G, POOL_G), lambda i: (0, 0, 0)), pl.BlockSpec((1, POOL_W), lambda i: (0, 0))],
        out_shape=[jax.ShapeDtypeStruct((seq, POOL_W), BF16), jax.ShapeDtypeStruct((4, POOL_G, POOL_G), F32),
                   jax.ShapeDtypeStruct((1, POOL_W), F32)],
        compiler_params=_params(("arbitrary",)),
    )(qkvu, qkvu, qkvu, dmix, dmix, dmix, pool_w, pool_scale)


def _shifted(z, zprev_row, znext_row, te):
    rows = lax.broadcasted_iota(jnp.int32, (te, 1), 0)
    zp = jnp.where(rows == 0, zprev_row, pltpu.roll(z, 1, 0))
    zn = jnp.where(rows == te - 1, znext_row, pltpu.roll(z, te - 1, 0))
    return zp, zn


def _edge_rows(prev_ref, next_ref, i, nt):
    p = prev_ref[HALO - 1:HALO, :].astype(F32)
    q = next_ref[0:1, :].astype(F32)
    return jnp.where(i == 0, 0.0, p), jnp.where(i == nt - 1, 0.0, q)


def conv_fwd(proj, conv_w, name):
    seq = proj.shape[0]
    te = _tile(seq, (512, 256))
    nt = seq // te

    def body(bg_ref, cp_ref, cc_ref, cn_ref, xp_ref, xc_ref, xn_ref, w_ref, o_ref):
        i = pl.program_id(0)
        z = cc_ref[...].astype(F32) * xc_ref[...].astype(F32)
        cpr, cnr = _edge_rows(cp_ref, cn_ref, i, nt)
        xpr, xnr = _edge_rows(xp_ref, xn_ref, i, nt)
        zp, zn = _shifted(z, cpr * xpr, cnr * xnr, te)
        y = zp * w_ref[0:1, :] + z * w_ref[1:2, :] + zn * w_ref[2:3, :]
        o_ref[...] = (bg_ref[...].astype(F32) * y).astype(BF16)

    return pl.pallas_call(
        body, name=name, grid=(nt,),
        in_specs=[pl.BlockSpec((te, D), lambda i: (i, 0))] + _halo_specs(te, seq, 1, D) + _halo_specs(te, seq, 2, D)
                 + [pl.BlockSpec((3, D), lambda i: (0, 0))],
        out_specs=pl.BlockSpec((te, D), lambda i: (i, 0)),
        out_shape=jax.ShapeDtypeStruct((seq, D), BF16),
        compiler_params=_params(("parallel",)),
    )(proj, proj, proj, proj, proj, proj, proj, conv_w)


def conv_bwd(proj, dgm, conv_w, name):
    seq = proj.shape[0]
    te = _tile(seq, (512, 256))
    nt = seq // te

    def body(bp_ref, bc_ref, bn_ref, cp_ref, cc_ref, cn_ref, xp_ref, xc_ref, xn_ref, gp_ref, gc_ref, gn_ref, w_ref,
             dbg_ref, dcg_ref, dxin_ref, dw_ref):
        i = pl.program_id(0)

        @pl.when(i == 0)
        def _():
            dw_ref[...] = jnp.zeros_like(dw_ref)

        bg = bc_ref[...].astype(F32)
        cg = cc_ref[...].astype(F32)
        xin = xc_ref[...].astype(F32)
        dg = gc_ref[...].astype(F32)
        z = cg * xin
        cpr, cnr = _edge_rows(cp_ref, cn_ref, i, nt)
        xpr, xnr = _edge_rows(xp_ref, xn_ref, i, nt)
        zp, zn = _shifted(z, cpr * xpr, cnr * xnr, te)
        w0, w1, w2 = w_ref[0:1, :], w_ref[1:2, :], w_ref[2:3, :]
        y = zp * w0 + z * w1 + zn * w2
        dbg_ref[...] = (dg * y).astype(BF16)
        dy = dg * bg
        dw_ref[0:1, :] += jnp.sum(dy * zp, axis=0, keepdims=True)
        dw_ref[1:2, :] += jnp.sum(dy * z, axis=0, keepdims=True)
        dw_ref[2:3, :] += jnp.sum(dy * zn, axis=0, keepdims=True)
        bpr, bnr = _edge_rows(bp_ref, bn_ref, i, nt)
        gpr, gnr = _edge_rows(gp_ref, gn_ref, i, nt)
        dyp, dyn = _shifted(dy, bpr * gpr, bnr * gnr, te)
        dz = dyn * w0 + dy * w1 + dyp * w2
        dcg_ref[...] = (dz * xin).astype(BF16)
        dxin_ref[...] = (dz * cg).astype(BF16)

    row = pl.BlockSpec((te, D), lambda i: (i, 0))
    return pl.pallas_call(
        body, name=name, grid=(nt,),
        in_specs=_halo_specs(te, seq, 0, D) + _halo_specs(te, seq, 1, D) + _halo_specs(te, seq, 2, D)
                 + _halo_specs(te, seq, 0, D) + [pl.BlockSpec((3, D), lambda i: (0, 0))],
        out_specs=[row, row, row, pl.BlockSpec((3, D), lambda i: (0, 0))],
        out_shape=[jax.ShapeDtypeStruct((seq, D), BF16)] * 3 + [jax.ShapeDtypeStruct((3, D), F32)],
        compiler_params=_params(("arbitrary",)),
    )(proj, proj, proj, proj, proj, proj, proj, proj, proj, dgm, dgm, dgm, conv_w)


def _position():
    x, y, c = lax.axis_index("x"), lax.axis_index("y"), lax.axis_index("c")
    return x, y, c, 4 * x + 2 * y + c


def _peer(x, y, c, j):
    px = 1 - x if j & 4 else x
    py = 1 - y if j & 2 else y
    pc = 1 - c if j & 1 else c
    return (px, py, pc), 4 * px + 2 * py + pc


def small_allgather(v, name):
    rows, cols = v.shape

    def body(v_ref, o_ref, send_sems, recv_sems, local_sem):
        x, y, c, me = _position()
        mine = pltpu.make_async_copy(v_ref, o_ref.at[me], local_sem)
        mine.start()
        sends = []
        for j in range(1, N_DEV):
            peer, _ = _peer(x, y, c, j)
            cp = pltpu.make_async_remote_copy(src_ref=v_ref, dst_ref=o_ref.at[me], send_sem=send_sems.at[j - 1],
                                              recv_sem=recv_sems.at[j - 1], device_id=peer, device_id_type=MESH_ID)
            cp.start()
            sends.append(cp)
        for j in range(1, N_DEV):
            peer, pid = _peer(x, y, c, j)
            pltpu.make_async_remote_copy(src_ref=v_ref, dst_ref=o_ref.at[pid], send_sem=send_sems.at[j - 1],
                                         recv_sem=recv_sems.at[j - 1], device_id=peer,
                                         device_id_type=MESH_ID).wait_recv()
        for cp in sends:
            cp.wait_send()
        mine.wait()

    return pl.pallas_call(
        body, name=name,
        out_shape=jax.ShapeDtypeStruct((N_DEV, rows, cols), v.dtype),
        in_specs=[pl.BlockSpec(memory_space=pltpu.VMEM)],
        out_specs=pl.BlockSpec(memory_space=pltpu.VMEM),
        scratch_shapes=[pltpu.SemaphoreType.DMA((N_DEV - 1,)), pltpu.SemaphoreType.DMA((N_DEV - 1,)),
                        pltpu.SemaphoreType.DMA],
        compiler_params=pltpu.CompilerParams(vmem_limit_bytes=VMEM_LIMIT),
    )(v)


class Exchange:
    def __init__(self, kind, arrays):
        self.kind, self.arrays, self.n = kind, list(arrays), len(arrays)
        n = self.n
        if kind == "gather":
            self.out_shapes = [jax.ShapeDtypeStruct((N_DEV,) + a.shape, a.dtype) for a in self.arrays]
        else:
            self.out_shapes = [jax.ShapeDtypeStruct(a.shape, a.dtype) for a in self.arrays]
        self.scratch = [pltpu.SemaphoreType.DMA((7 * n,)), pltpu.SemaphoreType.DMA((7 * n,)),
                        pltpu.SemaphoreType.DMA((n,))]

    def _gather_copies(self, ins, outs, sems):
        send_sems, recv_sems, local_sems = sems
        x, y, c, me = _position()
        chips = [(1 - x, y), (x, 1 - y), (1 - x, 1 - y)]

        def blk(k, px, py, pc):
            return outs[k].at[4 * px + 2 * py + pc]

        def copy(k, slot, block, to, src=None):
            return pltpu.make_async_remote_copy(
                src_ref=blk(k, *block) if src is None else src, dst_ref=blk(k, *block),
                send_sem=send_sems.at[k * 7 + slot], recv_sem=recv_sems.at[k * 7 + slot],
                device_id=to, device_id_type=MESH_ID)

        mine = [pltpu.make_async_copy(ins[k], blk(k, x, y, c), local_sems.at[k]) for k in range(self.n)]
        first = []
        for k in range(self.n):
            first.append(copy(k, 0, (x, y, c), (x, y, 1 - c), src=ins[k]))
            first += [copy(k, 1 + j, (x, y, c), (*chip, c), src=ins[k]) for j, chip in enumerate(chips)]
        return (x, y, c), chips, copy, mine, first

    def start(self, ins, outs, sems):
        if self.kind == "gather":
            _, _, _, mine, first = self._gather_copies(ins, outs, sems)
            for cp in mine + first:
                cp.start()
        else:
            for cp in self._scatter_copies(ins, outs, sems, False):
                cp.start()

    def finish(self, ins, outs, sems):
        if self.kind == "gather":
            (x, y, c), chips, copy, mine, first = self._gather_copies(ins, outs, sems)
            passed = []
            for j, chip in enumerate(chips):
                for k in range(self.n):
                    copy(k, 1 + j, (*chip, c), (x, y, c)).wait_recv()
                    cp = copy(k, 4 + j, (*chip, c), (x, y, 1 - c))
                    cp.start()
                    passed.append(cp)
            for k in range(self.n):
                copy(k, 0, (x, y, 1 - c), (x, y, c)).wait_recv()
                for j, chip in enumerate(chips):
                    copy(k, 4 + j, (*chip, 1 - c), (x, y, c)).wait_recv()
            for cp in first + passed:
                cp.wait_send()
            for cp in mine:
                cp.wait()
        else:
            for cp in self._scatter_copies(ins, outs, sems, True):
                cp.wait_recv()
            copies = self._scatter_copies(ins, outs, sems, False)
            for cp in copies[self.n:]:
                cp.wait_send()
            for cp in copies[:self.n]:
                cp.wait()

    def _scatter_copies(self, ins, outs, sems, arrivals):
        send_sems, recv_sems, local_sems = sems
        x, y, c, me = _position()
        out = []
        if not arrivals:
            out = [pltpu.make_async_copy(ins[k].at[me], outs[k].at[me], local_sems.at[k]) for k in range(self.n)]
        for j in range(1, N_DEV):
            peer, pid = _peer(x, y, c, j)
            for k in range(self.n):
                out.append(pltpu.make_async_remote_copy(
                    src_ref=ins[k].at[pid], dst_ref=outs[k].at[pid if arrivals else me],
                    send_sem=send_sems.at[k * 7 + j - 1], recv_sem=recv_sems.at[k * 7 + j - 1],
                    device_id=peer, device_id_type=MESH_ID))
        return out


def sum_devices(v, name):
    _, rows, cols = v.shape

    def body(v_ref, o_ref):
        acc = v_ref[0]
        for p in range(1, N_DEV):
            acc = acc + v_ref[p]
        o_ref[...] = acc

    return pl.pallas_call(
        body, name=name, out_shape=jax.ShapeDtypeStruct((rows, cols), F32),
        compiler_params=pltpu.CompilerParams(vmem_limit_bytes=VMEM_LIMIT),
    )(v)


def _silu(x):
    return x * _sigmoid(x)


def adaln_fwd(cm, mod_w, mod_b_cols, name):
    cols = mod_w.shape[2]

    def body(c_ref, w_ref, b_ref, o_ref):
        o_ref[0] = jnp.dot(_silu(c_ref[...]), w_ref[0], precision=HI, preferred_element_type=F32) + b_ref[0]

    return pl.pallas_call(
        body, name=name, grid=(2,),
        in_specs=[pl.BlockSpec((16, D), lambda l: (0, 0)), pl.BlockSpec((1, D, cols), lambda l: (l, 0, 0)),
                  pl.BlockSpec((1, 1, cols), lambda l: (l, 0, 0))],
        out_specs=pl.BlockSpec((1, 16, cols), lambda l: (l, 0, 0)),
        out_shape=jax.ShapeDtypeStruct((2, 16, cols), F32),
        compiler_params=_params(("parallel",)),
    )(cm, mod_w, mod_b_cols)


def adaln_bwd(cm_t, mod_w, dm_t, name):
    cols = mod_w.shape[2]

    def body(c_ref, w_ref, lat_ref, ctx_ref, gw_ref, pc_ref):
        ctot = jnp.sum(ctx_ref[0], axis=0, keepdims=True)
        rows = lax.broadcasted_iota(jnp.int32, (8, 1), 0)
        g_hi = jnp.where(rows == 0, ctot, 0.0)
        g = jnp.concatenate([lat_ref[0], g_hi], axis=0)
        gw_ref[0] = jnp.dot(_silu(c_ref[...]), g, precision=HI, preferred_element_type=F32)
        pc_ref[0] = lax.dot_general(g_hi, w_ref[0], NT_DIMS, precision=HI, preferred_element_type=F32)

    return pl.pallas_call(
        body, name=name, grid=(2,),
        in_specs=[pl.BlockSpec((D, 16), lambda l: (0, 0)), pl.BlockSpec((1, D, cols), lambda l: (l, 0, 0)),
                  pl.BlockSpec((1, 8, cols), lambda l: (l, 0, 0)), pl.BlockSpec((1, 8, cols), lambda l: (l + 2, 0, 0))],
        out_specs=[pl.BlockSpec((1, D, cols), lambda l: (l, 0, 0)), pl.BlockSpec((1, 8, D), lambda l: (l, 0, 0))],
        out_shape=[jax.ShapeDtypeStruct((2, D, cols), F32), jax.ShapeDtypeStruct((2, 8, D), F32)],
        compiler_params=_params(("parallel",)),
    )(cm_t, mod_w, dm_t, dm_t)


def mod_b_grad(dm_t, name):
    width = dm_t.shape[2]
    tn = width // 8

    def body(d_ref, o_ref):
        s = jnp.concatenate([jnp.sum(d_ref[k], axis=0, keepdims=True) for k in range(4)]
                            + [jnp.zeros((4, tn), F32)], axis=0)
        o_ref[...] = s + pltpu.roll(s, 6, 0)

    return pl.pallas_call(
        body, name=name, grid=(8,),
        in_specs=[pl.BlockSpec((4, 8, tn), lambda j: (0, 0, j))],
        out_specs=pl.BlockSpec((8, tn), lambda j: (0, j)),
        out_shape=jax.ShapeDtypeStruct((8, width), F32),
        compiler_params=_params(("parallel",)),
    )(dm_t)


def adamw(w, m, v, name, g=None, recv=None):
    rows, cols = w.shape
    tr = _tile(rows, (256, 128, 64, 32, 16, 8))
    summed = recv is not None

    def body(w_ref, m_ref, v_ref, g_ref, go_ref, d_ref, mo_ref, vo_ref):
        if summed:
            gv = g_ref[0].astype(F32)
            for p in range(1, N_DEV):
                gv = gv + g_ref[p].astype(F32)
        else:
            gv = g_ref[...]
        mn = ADAM_B1 * m_ref[...] + (1.0 - ADAM_B1) * gv
        vn = ADAM_B2 * v_ref[...] + (1.0 - ADAM_B2) * (gv * gv)
        m_hat = mn / (1.0 - ADAM_B1 ** ADAM_STEP)
        v_hat = vn / (1.0 - ADAM_B2 ** ADAM_STEP)
        go_ref[...] = gv
        d_ref[...] = -ADAM_LR * (m_hat / (jnp.sqrt(v_hat) + ADAM_EPS) + ADAM_WD * w_ref[...])
        mo_ref[...] = mn
        vo_ref[...] = vn

    row = pl.BlockSpec((tr, cols), lambda i: (i, 0))
    gspec = pl.BlockSpec((N_DEV, tr, cols), lambda i: (0, i, 0)) if summed else row
    return pl.pallas_call(
        body, name=name, grid=(rows // tr,),
        in_specs=[row, row, row, gspec], out_specs=[row] * 4,
        out_shape=[jax.ShapeDtypeStruct((rows, cols), F32)] * 4,
        compiler_params=_params(("parallel",)),
    )(w, m, v, recv if summed else g)


def _ffn_fwd(h, hn, mods, w13, w2, base, tag, nxt=None, exchange=None):
    (p, u, s), exchanged = ffn_up(hn, w13, tag + "_up", exchange)
    if callable(w2):
        w2 = w2(exchanged)
    outs = mm_nn([s], w2, [0], tag + "_down", res=(h, mods, base + 2, 0.5), nxt=nxt)
    h_new, y = outs[0], outs[1]
    return h_new, (outs[2] if nxt else None), (h, hn, p, u, s, y), exchanged


COLUMN_CUT = ("w13", "ewi", "cwi")
GATHER_FIRST = ("w13_00",)
GATHER_IN_FFN = ("w2_00", "ewi", "ewo", "w13_01", "w2_01")
GATHER_IN_ATTN = ("w13_10", "w2_10", "cwi", "cwo", "w13_11", "w2_11")
SCATTER_IN_ATTN = ("w13_11", "w2_11", "cwi", "cwo", "w13_10", "w2_10", "w13_01", "w2_01", "ewo")
SCATTER_LAST = ("w13_00", "w2_00", "ewi")


def unpack_piece(p, g):
    if p.split("_")[0] in COLUMN_CUT:
        return g.transpose(1, 0, 2).reshape(g.shape[1], -1)
    return g.reshape(-1, g.shape[2])


def block_piece(p, full):
    if p.split("_")[0] in COLUMN_CUT:
        return full.reshape(full.shape[0], N_DEV, -1).transpose(1, 0, 2).astype(BF16)
    return full.reshape(N_DEV, -1, full.shape[1]).astype(BF16)


def _ffn_bwd(dy, dres, saved, mods, g, w13, w2, base, tag, prev=None, acc=None, exchange_of=None):
    h, hn, p, u, s, _ = saved
    ff = w2.shape[0]
    acc = acc or (None, None, None)
    nprev = 2 if prev else 0
    if exchange_of is None:
        outs = mm_nt_norm(None, w13, [0, ff], h, g, mods, base + 1, dres, tag + "_dhn", prev=prev,
                          swiglu=(dy, w2, u, p))
        da, db = outs[4 + nprev:6 + nprev]
        dw2 = mm_tn(s, dy, tag + "_dw2", acc=acc[2])
        dwa = mm_tn(hn, da, tag + "_dw13a", acc=acc[0])
        dwb = mm_tn(hn, db, tag + "_dw13b", acc=acc[1])
        exchanged = []
    else:
        da, db = mm_nt(dy, w2, tag + "_ds", dswiglu=(u, p))
        dw2 = mm_tn(s, dy, tag + "_dw2", acc=acc[2])
        dwa = mm_tn(hn, da, tag + "_dw13a", acc=acc[0])
        dwb = mm_tn(hn, db, tag + "_dw13b", acc=acc[1])
        outs = mm_nt_norm([da, db], w13, [0, ff], h, g, mods, base + 1, dres, tag + "_dhn", prev=prev,
                          exchange=exchange_of(dwa, dwb, dw2))
        exchanged = list(outs[4 + nprev:])
    dh, dshift, dscale, dg = outs[:4]
    return (dh, (dwa, dwb, dw2), dg, {base: dshift, base + 1: dscale}, tuple(outs[4:4 + nprev]), exchanged)


def _mod_rows(parts):
    zero = jnp.zeros((1, D), F32)
    return jnp.concatenate([parts.get(k, zero) for k in range(N_MOD)], axis=0)


def local_step(x, ctx, ml, mc, wts, target, shards=None):
    wts = dict(wts)
    ng = wts["norm_g"]
    gvec = lambda l, k: ng[l, k][None, :]
    pool_w16 = wts["pool_w"].astype(BF16)
    grads = {}

    def gather(pieces):
        return Exchange("gather", [shards[p] for p in pieces]) if shards else None

    def arrived(pieces, results):
        for p, g in zip(pieces, results):
            wts[p] = unpack_piece(p, g)

    def ffn_grads(lf, f):
        grads["w13_" + lf] = jnp.concatenate([f[0], f[1]], axis=1)
        grads["w2_" + lf] = f[2]

    if shards:
        xh, got = normmod(x, gvec(0, 0), ml[0], 0, 1, "l0f1_norm", gather(GATHER_FIRST))
        arrived(GATHER_FIRST, got)
    else:
        xh = normmod(x, gvec(0, 0), ml[0], 0, 1, "l0f1_norm")
    ch = normmod(ctx, gvec(0, 0), mc[0], 0, 1, "l0f1c_norm")
    def w2_after_up(got):
        arrived(GATHER_IN_FFN, got)
        return wts["w2_00"]

    x1, xn, sv1, _ = _ffn_fwd(x, xh, ml[0], wts["w13_00"], w2_after_up, 0, "l0f1",
                              nxt=(gvec(0, 1), ml[0], 3, 4), exchange=gather(GATHER_IN_FFN))
    c1, cn, sv1c, _ = _ffn_fwd(ctx, ch, mc[0], wts["w13_00"], wts["w2_00"], 0, "l0f1c", nxt=(gvec(0, 1), mc[0], 3, 4))
    qkvu = mm_nn([xn], wts["ewi"], [0], "l0mix_in")
    qkvu_c = mm_nn([cn], wts["ewi"], [0], "l0mix_in_c")
    tab = bias_table(wts["rpb"])
    att, got = attn_fwd(qkvu, qkvu_c, tab, "l0_attn", gather(GATHER_IN_ATTN))
    arrived(GATHER_IN_ATTN, got)
    pool = pool_fwd(qkvu, pool_w16, wts["pool_scale"], "l0_pool")
    x2, ymix, xh = mm_nn([att, pool], wts["ewo"], [0, NA_W], "l0mix_out", res=(x1, ml[0], 5, 1.0),
                         nxt=(gvec(0, 2), ml[0], 6, 7))
    x3, xh, sv2, _ = _ffn_fwd(x2, xh, ml[0], wts["w13_01"], wts["w2_01"], 6, "l0f2", nxt=(gvec(1, 0), ml[1], 0, 1))

    x4, xn1, sv3, _ = _ffn_fwd(x3, xh, ml[1], wts["w13_10"], wts["w2_10"], 0, "l1f1", nxt=(gvec(1, 1), ml[1], 3, 4))
    proj = mm_nn([xn1], wts["cwi"], [0], "l1mix_in")
    gm = conv_fwd(proj, wts["conv_w"], "l1_conv")
    x5, ycv, xh = mm_nn([gm], wts["cwo"], [0], "l1mix_out", res=(x4, ml[1], 5, 1.0), nxt=(gvec(1, 2), ml[1], 6, 7))
    x6, _, sv4, _ = _ffn_fwd(x5, xh, ml[1], wts["w13_11"], wts["w2_11"], 6, "l1f2")

    dx6, loss, dgf, dy, dgate = loss_head(x6, wts["final_g"][None, :], target, (sv4[5], ml[1], 8, 0.5), "loss_head")
    dm1 = {8: dgate}
    dx5, dwf4, dg12, dm_f4, (dy, dgate), _ = _ffn_bwd(dy, dx6, sv4, ml[1], gvec(1, 2), wts["w13_11"], wts["w2_11"], 6,
                                                      "l1f2", prev=(ycv, ml[1], 5, 1.0))
    ffn_grads("11", dwf4)
    dm1.update({5: dgate, **dm_f4})
    dgm = mm_nt(dy, wts["cwo"], "l1mix_dgm")
    grads["cwo"] = mm_tn(gm, dy, "l1mix_dwo")
    dbg, dcg, dxin, dconv_w = conv_bwd(proj, dgm, wts["conv_w"], "l1_conv_bwd")
    grads["cwi"] = jnp.concatenate([mm_tn(xn1, t, "l1mix_dwi%d" % k) for k, t in enumerate((dbg, dcg, dxin))], axis=1)
    dx4, dsh, dsc, dg11, dy, dgate = mm_nt_norm([dbg, dcg, dxin], wts["cwi"], [0, D, 2 * D], x4, gvec(1, 1), ml[1], 4,
                                                dx5, "l1mix_dxn", prev=(sv3[5], ml[1], 2, 0.5))
    dm1.update({3: dsh, 4: dsc, 2: dgate})
    dx3, dwf3, dg10, dm_f3, (dy, dgate), _ = _ffn_bwd(dy, dx4, sv3, ml[1], gvec(1, 0), wts["w13_10"], wts["w2_10"], 0,
                                                      "l1f1", prev=(sv2[5], ml[0], 8, 0.5))
    ffn_grads("10", dwf3)
    dm1.update(dm_f3)
    dm0 = {8: dgate}

    dx2, dwf2, dg02, dm_f2, (dy, dgate), _ = _ffn_bwd(dy, dx3, sv2, ml[0], gvec(0, 2), wts["w13_01"], wts["w2_01"], 6,
                                                      "l0f2", prev=(ymix, ml[0], 5, 1.0))
    ffn_grads("01", dwf2)
    dm0.update({5: dgate, **dm_f2})
    dmix = mm_nt(dy, wts["ewo"], "l0mix_dmix")
    grads["ewo"] = jnp.concatenate([mm_tn(att, dy, "l0mix_dwo_att"), mm_tn(pool, dy, "l0mix_dwo_pool")], axis=0)
    scatter = Exchange("scatter", [block_piece(p, grads.pop(p)) for p in SCATTER_IN_ATTN]) if shards else None
    (dq, dk, dv, dkc, dvc, dtab), got = attn_bwd(qkvu, qkvu_c, tab, dmix, "l0_attn_bwd", scatter)
    recv = dict(zip(SCATTER_IN_ATTN, got))
    du, dpool_w, dpool_scale = pool_bwd(qkvu, dmix, pool_w16, wts["pool_scale"], "l0_pool_bwd")
    drpb = bias_table_bwd(dtab)
    dk16, dv16, dkc16, dvc16 = dk, dv, dkc.astype(BF16), dvc.astype(BF16)
    grads["ewi"] = jnp.concatenate([
        mm_tn(xn, dq, "l0mix_dwi_q"),
        mm_tn(cn, dkc16, "l0mix_dwi_kc", acc=mm_tn(xn, dk16, "l0mix_dwi_k")),
        mm_tn(cn, dvc16, "l0mix_dwi_vc", acc=mm_tn(xn, dv16, "l0mix_dwi_v")),
        mm_tn(xn, du, "l0mix_dwi_u")], axis=1)
    dx1, dsh, dsc, dg01, dy, dgate = mm_nt_norm([dq, dk16, dv16, du], wts["ewi"], [0, NA_W, 2 * NA_W, 3 * NA_W], x1,
                                                gvec(0, 1), ml[0], 4, dx2, "l0mix_dxn", prev=(sv1[5], ml[0], 2, 0.5))
    dm0.update({3: dsh, 4: dsc, 2: dgate})
    dc1, dsh_c, dsc_c, dg01c, dy_c, dgate_c = mm_nt_norm([dkc16, dvc16], wts["ewi"], [NA_W, 2 * NA_W], c1, gvec(0, 1),
                                                         mc[0], 4, None, "l0mix_dxn_c", prev=(sv1c[5], mc[0], 2, 0.5))
    _, dwf1c, dg00c, dm_f1c, _, _ = _ffn_bwd(dy_c, dc1, sv1c, mc[0], gvec(0, 0), wts["w13_00"], wts["w2_00"], 0, "l0f1c")
    dmc0 = {3: dsh_c, 4: dsc_c, 2: dgate_c, **dm_f1c}

    def last_scatter(dwa, dwb, dw2):
        ffn_grads("00", (dwa, dwb, dw2))
        return Exchange("scatter", [block_piece(p, grads.pop(p)) for p in SCATTER_LAST]) if shards else None

    dx0, _, dg00, dm_f1, _, got = _ffn_bwd(dy, dx1, sv1, ml[0], gvec(0, 0), wts["w13_00"], wts["w2_00"], 0, "l0f1",
                                           acc=dwf1c, exchange_of=last_scatter)
    recv.update(zip(SCATTER_LAST, got))
    dm0.update(dm_f1)

    return {
        "loss": loss, "grad_x": dx0,
        "dml": jnp.stack([_mod_rows(dm0), _mod_rows(dm1)]),
        "dmc": jnp.stack([_mod_rows(dmc0), jnp.zeros((N_MOD, D), F32)]),
        "norm_g": jnp.concatenate([dg00 + dg00c, dg01 + dg01c, dg02, dg10, dg11, dg12], axis=0),
        "grads": grads, "recv": recv,
        "rpb": drpb, "pool_w": dpool_w, "pool_scale": dpool_scale, "conv_w": dconv_w, "final_g": dgf,
    }


def _rows_of(v, nrows):
    flat = v.reshape(-1)
    return jnp.pad(flat, (0, nrows * D - flat.shape[0])).reshape(nrows, D)


def kernel(x, c, ctx, c_ctx, mod_w, mod_b, norm_g, ffn_w13, ffn_w2, even_w_in, even_w_out, na_rpb, pool_w, pool_scale, conv_w_in, conv_w, conv_w_out, final_g, loss_target, m_c_ctx, m_mod_w, m_mod_b, m_norm_g, m_ffn_w13, m_ffn_w2, m_even_w_in, m_even_w_out, m_na_rpb, m_pool_w, m_pool_scale, m_conv_w_in, m_conv_w, m_conv_w_out, m_final_g, v_c_ctx, v_mod_w, v_mod_b, v_norm_g, v_ffn_w13, v_ffn_w2, v_even_w_in, v_even_w_out, v_na_rpb, v_pool_w, v_pool_scale, v_conv_w_in, v_conv_w, v_conv_w_out, v_final_g):
    me = 4 * lax.axis_index("x") + 2 * lax.axis_index("y") + lax.axis_index("c")
    ff = ffn_w2.shape[2] * N_DEV
    w13c = ffn_w13.shape[3]
    w2r = ffn_w2.shape[2]
    mcols = mod_w.shape[2]
    gcols = norm_g.shape[2]

    big = {"w13": ffn_w13.reshape(4 * D, w13c), "w2": ffn_w2.reshape(4 * w2r, D), "ewi": even_w_in[0],
           "ewo": even_w_out[0], "cwi": conv_w_in[0], "cwo": conv_w_out[0]}
    names = list(big)
    shards = {"ewi": even_w_in[0].astype(BF16), "ewo": even_w_out[0].astype(BF16),
              "cwi": conv_w_in[0].astype(BF16), "cwo": conv_w_out[0].astype(BF16)}
    for l in range(2):
        for f in range(2):
            shards["w13_%d%d" % (l, f)] = ffn_w13[l, f].astype(BF16)
            shards["w2_%d%d" % (l, f)] = ffn_w2[l, f].astype(BF16)
    wts = {}

    c_all = small_allgather(jnp.pad(c, ((0, 7), (0, 0))), "cond_allgather")[:, 0, :]
    cm = jnp.concatenate([c_all, c_ctx[None, :], jnp.zeros((7, D), F32)], axis=0)
    mod_b_cols = lax.dynamic_slice(mod_b, (0, me * mcols), (2, mcols))[:, None, :]
    m_cols = adaln_fwd(cm, mod_w, mod_b_cols, "adaln_fwd")
    m_all = small_allgather(m_cols.reshape(32, mcols), "mod_allgather")
    m_full = m_all.reshape(N_DEV, 2, 16, mcols).transpose(1, 2, 0, 3).reshape(2, 16, N_MOD * D)
    ml = lax.dynamic_slice(m_full, (0, me, 0), (2, 1, N_MOD * D)).reshape(2, N_MOD, D)
    mc = m_full[:, 8].reshape(2, N_MOD, D)

    full_norm_g = small_allgather(_rows_of(norm_g, 8), "norm_g_allgather")[:, 0, :2 * 3 * gcols]
    full_norm_g = full_norm_g.reshape(N_DEV, 2, 3, gcols).transpose(1, 2, 0, 3).reshape(2, 3, D)
    full_conv_w = small_allgather(_rows_of(conv_w, 8), "conv_w_allgather")[:, 0, :3 * gcols]
    full_conv_w = full_conv_w.reshape(N_DEV, 3, gcols).transpose(1, 0, 2).reshape(3, D)
    wts.update(norm_g=full_norm_g, conv_w=full_conv_w, rpb=na_rpb[0], pool_w=pool_w[0], pool_scale=pool_scale,
               final_g=final_g)
    out = local_step(x[0], ctx[0], ml, mc, wts, loss_target[0], shards)

    dm_pack = jnp.concatenate([out["dml"].reshape(2, N_MOD * D), out["dmc"].reshape(2, N_MOD * D),
                               jnp.zeros((4, N_MOD * D), F32)], axis=0)
    dm_t = small_allgather(dm_pack, "dmod_allgather").transpose(1, 0, 2)[:4]
    dm_cols = lax.dynamic_slice(dm_t, (0, 0, me * mcols), (4, N_DEV, mcols))
    g_mod_w, pc = adaln_bwd(cm.T, mod_w, dm_cols, "adaln_bwd")
    g_mod_b = mod_b_grad(dm_t, "mod_b_grad")[:2]

    pack = jnp.concatenate([_rows_of(t, 8) for t in (
        out["norm_g"], out["conv_w"], out["final_g"], pc[0, :1] + pc[1, :1], out["pool_scale"], out["loss"],
        out["rpb"])] + [_rows_of(out["pool_w"], 64)], axis=0)
    small = sum_devices(small_allgather(pack, "small_grads_allgather"), "small_grads_sum")
    g_norm_g = lax.dynamic_slice(small[0:6].reshape(2, 3, D), (0, 0, me * gcols), (2, 3, gcols))
    g_conv_w = lax.dynamic_slice(small[8:11], (0, me * gcols), (3, gcols))[None]
    g_final_g = small[16]
    sg = _sigmoid(c_ctx)
    g_c_ctx = small[24] * (sg * (1.0 + c_ctx * (1.0 - sg)))
    g_pool_scale = small[32:33, :POOL_W]
    loss = small[40, 0]
    g_rpb = small[48:52].reshape(-1)[:na_rpb.size].reshape(na_rpb.shape)
    g_pool_w = small[56:120].reshape(pool_w.shape)

    pieces = out["recv"]
    lf = ("00", "01", "10", "11")
    recv = {"w13": jnp.concatenate([pieces["w13_" + t] for t in lf], axis=1),
            "w2": jnp.concatenate([pieces["w2_" + t] for t in lf], axis=1),
            "ewi": pieces["ewi"], "ewo": pieces["ewo"], "cwi": pieces["cwi"], "cwo": pieces["cwo"]}

    moments = {"w13": (m_ffn_w13, v_ffn_w13), "w2": (m_ffn_w2, v_ffn_w2), "ewi": (m_even_w_in, v_even_w_in),
               "ewo": (m_even_w_out, v_even_w_out), "cwi": (m_conv_w_in, v_conv_w_in),
               "cwo": (m_conv_w_out, v_conv_w_out)}
    orig = {"w13": ffn_w13, "w2": ffn_w2, "ewi": even_w_in, "ewo": even_w_out, "cwi": conv_w_in, "cwo": conv_w_out}
    upd = {}
    for k in names:
        shp2 = big[k].shape
        res = adamw(big[k], moments[k][0].reshape(shp2), moments[k][1].reshape(shp2), "adamw_" + k, recv=recv[k])
        upd[k] = [r.reshape(orig[k].shape) for r in res]
    shp2 = (2 * D, mcols)
    upd["mod_w"] = [r.reshape(mod_w.shape) for r in adamw(mod_w.reshape(shp2), m_mod_w.reshape(shp2),
                                                          v_mod_w.reshape(shp2), "adamw_mod_w",
                                                          g=g_mod_w.reshape(shp2))]

    smalls = [("c_ctx", c_ctx, m_c_ctx, v_c_ctx, g_c_ctx, 8), ("mod_b", mod_b, m_mod_b, v_mod_b, g_mod_b, 24),
              ("norm_g", norm_g, m_norm_g, v_norm_g, g_norm_g, 8), ("rpb", na_rpb, m_na_rpb, v_na_rpb, g_rpb, 8),
              ("pool_w", pool_w, m_pool_w, v_pool_w, g_pool_w, 64),
              ("pool_scale", pool_scale, m_pool_scale, v_pool_scale, g_pool_scale, 8),
              ("conv_w", conv_w, m_conv_w, v_conv_w, g_conv_w, 8), ("final_g", final_g, m_final_g, v_final_g, g_final_g, 8)]
    packed = [jnp.concatenate([_rows_of(s[col], s[5]) for s in smalls], axis=0) for col in (1, 2, 3, 4)]
    res = adamw(packed[0], packed[1], packed[2], "adamw_small", g=packed[3])
    row = 0
    for name, w, _, _, _, nrows in smalls:
        upd[name] = [r[row:row + nrows].reshape(-1)[:w.size].reshape(w.shape) for r in res]
        row += nrows

    order = ["c_ctx", "mod_w", "mod_b", "norm_g", "w13", "w2", "ewi", "ewo", "rpb", "pool_w", "pool_scale", "cwi",
             "conv_w", "cwo", "final_g"]
    grad_x = out["grad_x"][None]
    return (loss, grad_x, *[upd[k][0] for k in order], *[upd[k][1] for k in order], *[upd[k][2] for k in order],
            *[upd[k][3] for k in order])
```

```python
import functools

import numpy as np
import jax
import jax.numpy as jnp
from jax import lax
from jax.experimental import pallas as pl
from jax.experimental.pallas import tpu as pltpu

D = 1024
FF = 2816
SEQ = 16384
CTX = 256
GRID_W = 64
N_MOD = 9
HEADS = 8
HEAD_DIM = 64
NA_W = 512
POOL_W = 512
POOL_G = 128
POOL_WINDOWS = (2, 4, 8, 16)
KH = 8
KW = 16
RMS_EPS = 1e-6
NEG_INF = -1e30
N_DEV = 8

ADAM_LR = 0.001
ADAM_B1 = 0.9
ADAM_B2 = 0.999
ADAM_EPS = 1e-08
ADAM_WD = 0.01
ADAM_STEP = 10

VMEM_LIMIT = 52 * 1024 * 1024
HALO = 16
QROWS = 8
WROWS = 24

BF16 = jnp.bfloat16
F32 = jnp.float32
MESH_ID = pl.DeviceIdType.MESH
HI = lax.Precision.HIGHEST

NT_DIMS = (((1,), (1,)), ((), ()))
TN_DIMS = (((0,), (0,)), ((), ()))


def _tile(n, cands):
    for c in cands:
        if n % c == 0:
            return c
    return n


def _params(sem):
    return pltpu.CompilerParams(dimension_semantics=sem, vmem_limit_bytes=VMEM_LIMIT)


def _dot(a, b):
    return jnp.dot(a, b, preferred_element_type=F32)


def _dot_nt(a, b):
    return lax.dot_general(a, b, NT_DIMS, preferred_element_type=F32)


def _dot_tn(a, b):
    return lax.dot_general(a, b, TN_DIMS, preferred_element_type=F32)


def _sigmoid(x):
    return 1.0 / (1.0 + jnp.exp(-x))


def normmod(h, g, mods, i_shift, i_scale, name, exchange=None):
    n = h.shape[0]
    te = _tile(n, (512, 256))
    nt = n // te
    nx = exchange.n if exchange else 0

    def body(*refs):
        h_ref, g_ref, m_ref = refs[:3]
        x_in = refs[3:3 + nx]
        o_ref = refs[3 + nx]
        x_out = refs[4 + nx:4 + 2 * nx]
        x_sems = refs[4 + 2 * nx:]
        if exchange:
            @pl.when(pl.program_id(0) == 0)
            def _():
                exchange.start(x_in, x_out, x_sems)

        x = h_ref[...]
        r = lax.rsqrt(jnp.mean(x * x, axis=-1, keepdims=True) + RMS_EPS)
        y = x * r * g_ref[...]
        o_ref[...] = (y * (1.0 + m_ref[i_scale:i_scale + 1, :]) + m_ref[i_shift:i_shift + 1, :]).astype(BF16)

        if exchange:
            @pl.when(pl.program_id(0) == nt - 1)
            def _():
                exchange.finish(x_in, x_out, x_sems)

    hbm = pl.BlockSpec(memory_space=pltpu.HBM)
    res = pl.pallas_call(
        body, name=name, grid=(nt,),
        in_specs=[pl.BlockSpec((te, D), lambda i: (i, 0)),
                  pl.BlockSpec((1, D), lambda i: (0, 0)),
                  pl.BlockSpec((N_MOD, D), lambda i: (0, 0))] + [hbm] * nx,
        out_specs=[pl.BlockSpec((te, D), lambda i: (i, 0))] + [hbm] * nx,
        out_shape=[jax.ShapeDtypeStruct((n, D), BF16)] + (exchange.out_shapes if exchange else []),
        scratch_shapes=exchange.scratch if exchange else [],
        compiler_params=_params(("arbitrary",) if exchange else ("parallel",)),
    )(h, g, mods, *(exchange.arrays if exchange else []))
    return (res[0], list(res[1:])) if exchange else res[0]


def loss_head(x, g, target, prev, name):
    n = x.shape[0]
    te = _tile(n, (256,))
    i_gate, coef = prev[2], prev[3]

    def body(x_ref, g_ref, t_ref, y_ref, m_ref, dx_ref, loss_ref, dg_ref, dy_ref, dgate_ref):
        @pl.when(pl.program_id(0) == 0)
        def _():
            loss_ref[...] = jnp.zeros_like(loss_ref)
            dg_ref[...] = jnp.zeros_like(dg_ref)
            dgate_ref[...] = jnp.zeros_like(dgate_ref)

        xv = x_ref[...]
        gv = g_ref[...]
        r = lax.rsqrt(jnp.mean(xv * xv, axis=-1, keepdims=True) + RMS_EPS)
        xhat = xv * r
        e = xhat * gv - t_ref[...]
        per_tok = jnp.mean(e * e, axis=-1, keepdims=True)
        loss_ref[...] += 0.5 * jnp.sum(per_tok, axis=0, keepdims=True)
        dy = e * (1.0 / D)
        dg_ref[...] += jnp.sum(dy * xhat, axis=0, keepdims=True)
        dxhat = dy * gv
        dx = r * (dxhat - xhat * jnp.mean(dxhat * xhat, axis=-1, keepdims=True))
        dx_ref[...] = dx
        dy_ref[...] = (dx * (coef * m_ref[i_gate:i_gate + 1, :])).astype(BF16)
        dgate_ref[...] += coef * jnp.sum(dx * y_ref[...].astype(F32), axis=0, keepdims=True)

    row = pl.BlockSpec((te, D), lambda i: (i, 0))
    vec = pl.BlockSpec((1, D), lambda i: (0, 0))
    return pl.pallas_call(
        body, name=name, grid=(n // te,),
        in_specs=[row, vec, row, row, pl.BlockSpec((N_MOD, D), lambda i: (0, 0))],
        out_specs=[row, pl.BlockSpec((1, 128), lambda i: (0, 0)), vec, row, vec],
        out_shape=[jax.ShapeDtypeStruct((n, D), F32), jax.ShapeDtypeStruct((1, 128), F32),
                   jax.ShapeDtypeStruct((1, D), F32), jax.ShapeDtypeStruct((n, D), BF16),
                   jax.ShapeDtypeStruct((1, D), F32)],
        compiler_params=_params(("arbitrary",)),
    )(x, g, target, prev[0], prev[1])


def ffn_up(hn, w13, name, exchange=None):
    n = hn.shape[0]
    ff = w13.shape[1] // 2
    tm = _tile(n, (512, 256))
    tn = _tile(ff, (1408, 512, 256, 128))
    nj = ff // tn
    ni = n // tm
    nx = exchange.n if exchange else 0

    def body(*refs):
        h_ref, wa_ref, wb_ref = refs[:3]
        x_in = refs[3:3 + nx]
        p_ref, u_ref, s_ref = refs[3 + nx:6 + nx]
        x_out = refs[6 + nx:6 + 2 * nx]
        x_sems = refs[6 + 2 * nx:]
        if exchange:
            @pl.when((pl.program_id(0) == 0) & (pl.program_id(1) == 0))
            def _():
                exchange.start(x_in, x_out, x_sems)

        hv = h_ref[...]
        a = _dot(hv, wa_ref[...])
        b = _dot(hv, wb_ref[...])
        sig = _sigmoid(a)
        p = a * sig
        p_ref[...] = p.astype(BF16)
        u_ref[...] = (b * (sig * (1.0 + a * (1.0 - sig)))).astype(BF16)
        s_ref[...] = (p * b).astype(BF16)

        if exchange:
            @pl.when((pl.program_id(0) == nj - 1) & (pl.program_id(1) == ni - 1))
            def _():
                exchange.finish(x_in, x_out, x_sems)

    out = pl.BlockSpec((tm, tn), lambda j, i: (i, j))
    hbm = pl.BlockSpec(memory_space=pltpu.HBM)
    sem = ("arbitrary", "arbitrary") if exchange else ("parallel", "parallel")
    res = pl.pallas_call(
        body, name=name, grid=(nj, ni),
        in_specs=[pl.BlockSpec((tm, D), lambda j, i: (i, 0)),
                  pl.BlockSpec((D, tn), lambda j, i: (0, j)),
                  pl.BlockSpec((D, tn), lambda j, i: (0, j + nj))] + [hbm] * nx,
        out_specs=[out, out, out] + [hbm] * nx,
        out_shape=[jax.ShapeDtypeStruct((n, ff), BF16)] * 3 + (exchange.out_shapes if exchange else []),
        scratch_shapes=exchange.scratch if exchange else [],
        compiler_params=_params(sem),
    )(hn, w13, w13, *(exchange.arrays if exchange else []))
    return res[:3], list(res[3:])


def mm_nn(a_list, w, row_offs, name, out_dtype=BF16, res=None, nxt=None):
    n = a_list[0].shape[0]
    nout = w.shape[1]
    ks = [a.shape[1] for a in a_list]
    tm = _tile(n, (512, 256) if res is not None else (1024, 512, 256))
    tn = _tile(nout, (1024, 512, 256, 128))
    na = len(a_list)
    assert nxt is None or (res is not None and tn == D)

    def body(*refs):
        a_refs = refs[:na]
        w_refs = refs[na:2 * na]
        acc = _dot(a_refs[0][...], w_refs[0][...])
        for k in range(1, na):
            acc += _dot(a_refs[k][...], w_refs[k][...])
        if res is None:
            refs[2 * na][...] = acc.astype(out_dtype)
        else:
            h_ref, m_ref = refs[2 * na:2 * na + 2]
            i_gate, coef = res[2], res[3]
            h_new = h_ref[...] + (coef * m_ref[i_gate:i_gate + 1, :]) * acc
            if nxt is None:
                hn_ref, y_ref = refs[2 * na + 2:]
            else:
                g2_ref, m2_ref, hn_ref, y_ref, nx_ref = refs[2 * na + 2:]
                r = lax.rsqrt(jnp.mean(h_new * h_new, axis=-1, keepdims=True) + RMS_EPS)
                nx_ref[...] = ((h_new * r * g2_ref[...]) * (1.0 + m2_ref[nxt[3]:nxt[3] + 1, :])
                               + m2_ref[nxt[2]:nxt[2] + 1, :]).astype(BF16)
            hn_ref[...] = h_new
            y_ref[...] = acc.astype(BF16)

    in_specs = [pl.BlockSpec((tm, k), lambda j, i: (i, 0)) for k in ks]
    for k, off in zip(ks, row_offs):
        in_specs.append(pl.BlockSpec((k, tn), functools.partial(lambda j, i, ob: (ob, j), ob=off // k)))
    args = list(a_list) + [w] * na
    out = pl.BlockSpec((tm, tn), lambda j, i: (i, j))
    if res is None:
        out_specs = out
        out_shape = jax.ShapeDtypeStruct((n, nout), out_dtype)
    else:
        in_specs += [out, pl.BlockSpec((N_MOD, tn), lambda j, i: (0, j))]
        args += [res[0], res[1]]
        out_specs = [out, out]
        out_shape = [jax.ShapeDtypeStruct((n, nout), F32), jax.ShapeDtypeStruct((n, nout), BF16)]
        if nxt is not None:
            in_specs += [pl.BlockSpec((1, D), lambda j, i: (0, 0)), pl.BlockSpec((N_MOD, D), lambda j, i: (0, 0))]
            args += [nxt[0], nxt[1]]
            out_specs.append(out)
            out_shape.append(jax.ShapeDtypeStruct((n, nout), BF16))
    return pl.pallas_call(
        body, name=name, grid=(nout // tn, n // tm),
        in_specs=in_specs, out_specs=out_specs, out_shape=out_shape,
        compiler_params=_params(("parallel", "parallel")),
    )(*args)


def mm_nt(g, w, name, dswiglu=None):
    n, kg = g.shape
    nout = w.shape[0]
    tm = _tile(n, (512, 256) if dswiglu is not None else (1024, 512, 256))
    tn = _tile(nout, (1408, 1024, 512, 256, 128))

    def body(*refs):
        r = _dot_nt(refs[0][...], refs[1][...])
        if dswiglu is None:
            refs[2][...] = r.astype(BF16)
        else:
            u_ref, p_ref, da_ref, db_ref = refs[2:]
            da_ref[...] = (r * u_ref[...].astype(F32)).astype(BF16)
            db_ref[...] = (r * p_ref[...].astype(F32)).astype(BF16)

    out = pl.BlockSpec((tm, tn), lambda j, i: (i, j))
    in_specs = [pl.BlockSpec((tm, kg), lambda j, i: (i, 0)), pl.BlockSpec((tn, kg), lambda j, i: (j, 0))]
    args = [g, w]
    if dswiglu is None:
        out_specs = out
        out_shape = jax.ShapeDtypeStruct((n, nout), BF16)
    else:
        in_specs += [out, out]
        args += list(dswiglu)
        out_specs = [out, out]
        out_shape = [jax.ShapeDtypeStruct((n, nout), BF16)] * 2
    return pl.pallas_call(
        body, name=name, grid=(nout // tn, n // tm),
        in_specs=in_specs, out_specs=out_specs, out_shape=out_shape,
        compiler_params=_params(("parallel", "parallel")),
    )(*args)


def mm_nt_norm(g_list, w, col_offs, h, g, mods, i_scale, dres, name, prev=None, exchange=None):
    n = h.shape[0]
    kg = g_list[0].shape[1]
    tm = _tile(n, (512, 256))
    tk = _tile(kg, (1408, 1024, 512, 256, 128))
    ng = len(g_list)
    nk = kg // tk
    ni = n // tm
    has_res = dres is not None
    nx = exchange.n if exchange else 0

    def body(*refs):
        g_refs = refs[:ng]
        w_refs = refs[ng:2 * ng]
        pos = 2 * ng
        h_ref, gv_ref, m_ref = refs[pos:pos + 3]
        pos += 3
        if has_res:
            dres_ref = refs[pos]
            pos += 1
        if prev is not None:
            y_ref, mp_ref = refs[pos:pos + 2]
            pos += 2
        x_in = refs[pos:pos + nx]
        pos += nx
        dh_ref, dshift_ref, dscale_ref, dg_ref = refs[pos:pos + 4]
        pos += 4
        if prev is not None:
            dy_ref, dgate_ref = refs[pos:pos + 2]
            pos += 2
        x_out = refs[pos:pos + nx]
        pos += nx
        acc_ref = refs[pos]
        x_sems = refs[pos + 1:]
        i = pl.program_id(0)
        k = pl.program_id(1)

        @pl.when((i == 0) & (k == 0))
        def _():
            if exchange:
                exchange.start(x_in, x_out, x_sems)
            dshift_ref[...] = jnp.zeros_like(dshift_ref)
            dscale_ref[...] = jnp.zeros_like(dscale_ref)
            dg_ref[...] = jnp.zeros_like(dg_ref)
            if prev is not None:
                dgate_ref[...] = jnp.zeros_like(dgate_ref)

        def dots():
            acc = _dot_nt(g_refs[0][...], w_refs[0][...])
            for q in range(1, ng):
                acc += _dot_nt(g_refs[q][...], w_refs[q][...])
            return acc

        if nk > 1:
            @pl.when(k == 0)
            def _():
                acc_ref[...] = dots()

        if nk > 2:
            @pl.when((k > 0) & (k < nk - 1))
            def _():
                acc_ref[...] += dots()

        @pl.when(k == nk - 1)
        def _():
            d = dots() + acc_ref[...] if nk > 1 else dots()
            x = h_ref[...]
            gv = gv_ref[...]
            r = lax.rsqrt(jnp.mean(x * x, axis=-1, keepdims=True) + RMS_EPS)
            xhat = x * r
            one_scale = 1.0 + m_ref[i_scale:i_scale + 1, :]
            t = d * xhat
            tsum = jnp.sum(t, axis=0, keepdims=True)
            dshift_ref[...] += jnp.sum(d, axis=0, keepdims=True)
            dscale_ref[...] += gv * tsum
            dg_ref[...] += one_scale * tsum
            cvec = one_scale * gv
            dh = r * (d * cvec - xhat * jnp.mean(t * cvec, axis=-1, keepdims=True))
            if has_res:
                dh = dh + dres_ref[...]
            dh_ref[...] = dh
            if prev is not None:
                i_gate, coef = prev[2], prev[3]
                dy_ref[...] = (dh * (coef * mp_ref[i_gate:i_gate + 1, :])).astype(BF16)
                dgate_ref[...] += coef * jnp.sum(dh * y_ref[...].astype(F32), axis=0, keepdims=True)

        if exchange:
            @pl.when((i == ni - 1) & (k == nk - 1))
            def _():
                exchange.finish(x_in, x_out, x_sems)

    row = pl.BlockSpec((tm, D), lambda i, k: (i, 0))
    vec = pl.BlockSpec((1, D), lambda i, k: (0, 0))
    modspec = pl.BlockSpec((N_MOD, D), lambda i, k: (0, 0))
    in_specs = [pl.BlockSpec((tm, tk), lambda i, k: (i, k)) for _ in g_list]
    for off in col_offs:
        in_specs.append(pl.BlockSpec((D, tk), functools.partial(lambda i, k, ob: (0, ob + k), ob=off // tk)))
    in_specs += [row, vec, modspec]
    args = list(g_list) + [w] * ng + [h, g, mods]
    out_specs = [row, vec, vec, vec]
    out_shape = [jax.ShapeDtypeStruct((n, D), F32)] + [jax.ShapeDtypeStruct((1, D), F32)] * 3
    if has_res:
        in_specs.append(row)
        args.append(dres)
    if prev is not None:
        in_specs += [row, modspec]
        args += [prev[0], prev[1]]
        out_specs += [row, vec]
        out_shape += [jax.ShapeDtypeStruct((n, D), BF16), jax.ShapeDtypeStruct((1, D), F32)]
    scratch = [pltpu.VMEM((tm, D), F32)]
    if exchange:
        hbm = pl.BlockSpec(memory_space=pltpu.HBM)
        in_specs += [hbm] * nx
        args += exchange.arrays
        out_specs += [hbm] * nx
        out_shape += exchange.out_shapes
        scratch += exchange.scratch
    return pl.pallas_call(
        body, name=name, grid=(ni, nk),
        in_specs=in_specs, out_specs=out_specs, out_shape=out_shape, scratch_shapes=scratch,
        compiler_params=_params(("arbitrary", "arbitrary")),
    )(*args)


def mm_tn(a, g, name, acc=None):
    n, ka = a.shape
    ngc = g.shape[1]
    tka = _tile(ka, (1408, 1024, 512, 256, 128))
    tng = _tile(ngc, (1408, 1024, 512, 256, 128))
    tr = _tile(n, (2048, 1024, 512, 256))
    has_acc = acc is not None

    def body(*refs):
        a_ref, g_ref = refs[0], refs[1]
        o_ref = refs[-1]
        r = pl.program_id(2)

        @pl.when(r == 0)
        def _():
            d = _dot_tn(a_ref[...], g_ref[...])
            o_ref[...] = d + refs[2][...] if has_acc else d

        @pl.when(r > 0)
        def _():
            o_ref[...] += _dot_tn(a_ref[...], g_ref[...])

    out = pl.BlockSpec((tka, tng), lambda p, q, r: (p, q))
    in_specs = [pl.BlockSpec((tr, tka), lambda p, q, r: (r, p)),
                pl.BlockSpec((tr, tng), lambda p, q, r: (r, q))]
    args = [a, g]
    if has_acc:
        in_specs.append(out)
        args.append(acc)
    return pl.pallas_call(
        body, name=name, grid=(ka // tka, ngc // tng, n // tr),
        in_specs=in_specs, out_specs=out,
        out_shape=jax.ShapeDtypeStruct((ka, ngc), F32),
        compiler_params=_params(("parallel", "parallel", "arbitrary")),
    )(*args)


def mm_small(a, b, name, trans_b=False):
    m = a.shape[0]
    nout = b.shape[0] if trans_b else b.shape[1]

    def body(a_ref, b_ref, o_ref):
        if trans_b:
            o_ref[...] = lax.dot_general(a_ref[...], b_ref[...], NT_DIMS, precision=HI, preferred_element_type=F32)
        else:
            o_ref[...] = jnp.dot(a_ref[...], b_ref[...], precision=HI, preferred_element_type=F32)

    return pl.pallas_call(
        body, name=name,
        out_shape=jax.ShapeDtypeStruct((m, nout), F32),
        compiler_params=pltpu.CompilerParams(vmem_limit_bytes=VMEM_LIMIT),
    )(a, b)


def _col_tables():
    col = np.arange(GRID_W)
    start = np.clip(col - KW // 2, 0, GRID_W - KW)
    ok = (col[None, :] >= start[:, None]) & (col[None, :] < start[:, None] + KW)
    ci = np.clip(col[None, :] - col[:, None] + (KW - 1), 0, 2 * KW - 2)
    e = np.zeros((2 * KW - 1, GRID_W, GRID_W), np.float32)
    for c in range(2 * KW - 1):
        e[c] = (ci == c) & ok
    return e.reshape(2 * KW - 1, GRID_W * GRID_W), ok


def bias_table(rpb):
    e, ok = _col_tables()
    e_pad = np.zeros((32, GRID_W * GRID_W), np.float32)
    e_pad[:31] = e
    rp = jnp.pad(rpb.reshape(HEADS * 15, 31), ((0, 0), (0, 1)))
    t = mm_small(rp, jnp.asarray(e_pad), "rpb_expand").reshape(HEADS, 15, GRID_W, GRID_W)
    t = jnp.where(jnp.asarray(ok)[None, None], t, NEG_INF)
    tab = jnp.stack([t[:, v:v + KH] for v in range(8)], axis=0)
    return tab.transpose(0, 1, 3, 2, 4).reshape(TAB_SHAPE)


def bias_table_bwd(dtab):
    e, _ = _col_tables()
    e_pad = np.zeros((128, GRID_W * GRID_W), np.float32)
    e_pad[:31] = e
    d = dtab.reshape(8, HEADS, GRID_W, KH, GRID_W).transpose(0, 1, 3, 2, 4).reshape(8 * HEADS * KH, GRID_W * GRID_W)
    gv = mm_small(d, jnp.asarray(e_pad), "rpb_reduce", trans_b=True)[:, :31]
    gv = gv.reshape(8, HEADS, KH, 31).transpose(0, 2, 1, 3).reshape(8 * KH, HEADS * 31)
    sel = np.zeros((16, 8 * KH), np.float32)
    for v in range(8):
        for j in range(KH):
            sel[v + j, v * KH + j] = 1.0
    gpad = jnp.pad(gv, ((0, 0), (0, 256 - HEADS * 31)))
    out = mm_small(jnp.asarray(sel), gpad, "rpb_fold")[:15, :HEADS * 31]
    return out.reshape(15, HEADS, 31).transpose(1, 0, 2)


def _attn_geometry(seq):
    rows = seq // GRID_W
    nb = rows // QROWS
    return rows, nb


def _stack_heads(t2):
    first = (lax.broadcasted_iota(jnp.int32, (1, 128), 1) // HEAD_DIM) == 0
    zero = jnp.zeros_like(t2)
    return jnp.concatenate([jnp.where(first, t2, zero), jnp.where(first, zero, t2)], axis=0)


def _unstack_heads(t):
    first = (lax.broadcasted_iota(jnp.int32, (1, 128), 1) // HEAD_DIM) == 0
    return jnp.where(first, t[0:GRID_W], t[GRID_W:2 * GRID_W])


TAB_SHAPE = (8, HEADS // 2, 2 * GRID_W, KH * GRID_W)


def attn_fwd(qkvu, qkvu_c, tab, name, exchange=None):
    seq = qkvu.shape[0]
    nctx = qkvu_c.shape[0]
    rows, nb = _attn_geometry(seq)
    qt = QROWS * GRID_W
    wt = WROWS * GRID_W
    scale = HEAD_DIM ** -0.5
    nx = exchange.n if exchange else 0

    def wb0(i):
        return jnp.clip(i - 1, 0, nb - 3)

    def body(*refs):
        q_ref, k0, k1, k2, v0, v1, v2, kc_ref, vc_ref, tab_hbm = refs[:10]
        x_in = refs[10:10 + nx]
        o_ref = refs[10 + nx]
        x_out = refs[11 + nx:11 + 2 * nx]
        kbuf, vbuf, tab_s, sem = refs[11 + 2 * nx:15 + 2 * nx]
        x_sems = refs[15 + 2 * nx:]
        i = pl.program_id(0)

        @pl.when(i == 0)
        def _():
            if exchange:
                exchange.start(x_in, x_out, x_sems)
            cp = pltpu.make_async_copy(tab_hbm, tab_s, sem)
            cp.start()
            cp.wait()

        for t, (kr, vr) in enumerate(((k0, v0), (k1, v1), (k2, v2))):
            kbuf[t * qt:(t + 1) * qt, :] = kr[...]
            vbuf[t * qt:(t + 1) * qt, :] = vr[...]
        base = wb0(i) * QROWS

        def row_body(rl, carry):
            r = i * QROWS + rl
            rs = jnp.clip(r - KH // 2, 0, rows - KH)
            vi = rs - r + (KH - 1)
            off = pl.multiple_of((rs - base) * GRID_W, GRID_W)
            qoff = pl.multiple_of(rl * GRID_W, GRID_W)
            for p in range(HEADS // 2):
                ls = slice(p * 128, (p + 1) * 128)
                qst = _stack_heads(q_ref[pl.ds(qoff, GRID_W), ls])
                k2v = kbuf[pl.ds(off, KH * GRID_W), ls]
                v2v = vbuf[pl.ds(off, KH * GRID_W), ls]
                s_w = _dot_nt(qst, k2v) * scale + tab_s[vi, p]
                s_c = _dot_nt(qst, kc_ref[:, ls]) * scale
                m = jnp.maximum(jnp.max(s_w, axis=-1, keepdims=True), jnp.max(s_c, axis=-1, keepdims=True))
                pw = jnp.exp(s_w - m)
                pc = jnp.exp(s_c - m)
                l = jnp.sum(pw, axis=-1, keepdims=True) + jnp.sum(pc, axis=-1, keepdims=True)
                o = _dot(pw.astype(BF16), v2v) + _dot(pc.astype(BF16), vc_ref[:, ls])
                o_ref[pl.ds(qoff, GRID_W), ls] = _unstack_heads(o * (1.0 / l)).astype(BF16)
            return carry

        lax.fori_loop(0, QROWS, row_body, 0)

        if exchange:
            @pl.when(i == nb - 1)
            def _():
                exchange.finish(x_in, x_out, x_sems)

    blk = lambda col: [pl.BlockSpec((qt, NA_W), functools.partial(lambda i, t, c: (wb0(i) + t, c), t=t, c=col))
                       for t in range(3)]
    hbm = pl.BlockSpec(memory_space=pltpu.HBM)
    res = pl.pallas_call(
        body, name=name, grid=(nb,),
        in_specs=[pl.BlockSpec((qt, NA_W), lambda i: (i, 0))] + blk(1) + blk(2)
                 + [pl.BlockSpec((nctx, NA_W), lambda i: (0, 1)), pl.BlockSpec((nctx, NA_W), lambda i: (0, 2)),
                    pl.BlockSpec(memory_space=pl.ANY)] + [hbm] * nx,
        out_specs=[pl.BlockSpec((qt, NA_W), lambda i: (i, 0))] + [hbm] * nx,
        out_shape=[jax.ShapeDtypeStruct((seq, NA_W), BF16)] + (exchange.out_shapes if exchange else []),
        scratch_shapes=[pltpu.VMEM((wt, NA_W), BF16), pltpu.VMEM((wt, NA_W), BF16),
                        pltpu.VMEM(TAB_SHAPE, F32), pltpu.SemaphoreType.DMA] + (exchange.scratch if exchange else []),
        compiler_params=_params(("arbitrary",)),
    )(qkvu, qkvu, qkvu, qkvu, qkvu, qkvu, qkvu, qkvu_c, qkvu_c, tab, *(exchange.arrays if exchange else []))
    return res[0], list(res[1:])


def attn_bwd(qkvu, qkvu_c, tab, dmix, name, exchange=None):
    seq = qkvu.shape[0]
    nctx = qkvu_c.shape[0]
    rows, nb = _attn_geometry(seq)
    qt = QROWS * GRID_W
    wt = WROWS * GRID_W
    scale = HEAD_DIM ** -0.5
    nx = exchange.n if exchange else 0

    def wb0(i):
        return jnp.clip(i - 1, 0, nb - 3)

    def body(*refs):
        q_ref, k0, k1, k2, v0, v1, v2, kc_ref, vc_ref, do_ref, tab_hbm = refs[:11]
        x_in = refs[11:11 + nx]
        dq_ref, dk_hbm, dv_hbm, dkc_ref, dvc_ref, dtab_hbm = refs[11 + nx:17 + nx]
        x_out = refs[17 + nx:17 + 2 * nx]
        kbuf, vbuf, dkacc, dvacc, tab_s, dtab_s, stage, sem = refs[17 + 2 * nx:25 + 2 * nx]
        x_sems = refs[25 + 2 * nx:]
        i = pl.program_id(0)

        if exchange:
            @pl.when(i == 0)
            def _():
                exchange.start(x_in, x_out, x_sems)

        def flush(src, dst, block, dst_row):
            stage[...] = src[block * qt:(block + 1) * qt, :].astype(BF16)
            cp = pltpu.make_async_copy(stage, dst.at[pl.ds(dst_row, qt)], sem)
            cp.start()
            cp.wait()

        @pl.when(i == 0)
        def _():
            cp = pltpu.make_async_copy(tab_hbm, tab_s, sem)
            cp.start()
            cp.wait()
            dtab_s[...] = jnp.zeros_like(dtab_s)
            dkacc[...] = jnp.zeros_like(dkacc)
            dvacc[...] = jnp.zeros_like(dvacc)
            dkc_ref[...] = jnp.zeros_like(dkc_ref)
            dvc_ref[...] = jnp.zeros_like(dvc_ref)

        @pl.when((i >= 2) & (i <= nb - 2))
        def _():
            dst_row = pl.multiple_of((i - 2) * qt, qt)
            for acc_ref, dst in ((dkacc, dk_hbm), (dvacc, dv_hbm)):
                flush(acc_ref, dst, 0, dst_row)
                acc_ref[0:qt, :] = acc_ref[qt:2 * qt, :]
                acc_ref[qt:2 * qt, :] = acc_ref[2 * qt:3 * qt, :]
                acc_ref[2 * qt:3 * qt, :] = jnp.zeros((qt, NA_W), F32)

        for t, (kr, vr) in enumerate(((k0, v0), (k1, v1), (k2, v2))):
            kbuf[t * qt:(t + 1) * qt, :] = kr[...]
            vbuf[t * qt:(t + 1) * qt, :] = vr[...]
        base = wb0(i) * QROWS

        def row_body(rl, carry):
            r = i * QROWS + rl
            rs = jnp.clip(r - KH // 2, 0, rows - KH)
            vi = rs - r + (KH - 1)
            off = pl.multiple_of((rs - base) * GRID_W, GRID_W)
            qoff = pl.multiple_of(rl * GRID_W, GRID_W)
            for p in range(HEADS // 2):
                ls = slice(p * 128, (p + 1) * 128)
                qst = _stack_heads(q_ref[pl.ds(qoff, GRID_W), ls])
                dost = _stack_heads(do_ref[pl.ds(qoff, GRID_W), ls])
                k2v = kbuf[pl.ds(off, KH * GRID_W), ls]
                v2v = vbuf[pl.ds(off, KH * GRID_W), ls]
                kc2 = kc_ref[:, ls]
                vc2 = vc_ref[:, ls]
                s_w = _dot_nt(qst, k2v) * scale + tab_s[vi, p]
                s_c = _dot_nt(qst, kc2) * scale
                m = jnp.maximum(jnp.max(s_w, axis=-1, keepdims=True), jnp.max(s_c, axis=-1, keepdims=True))
                pw = jnp.exp(s_w - m)
                pc = jnp.exp(s_c - m)
                inv = 1.0 / (jnp.sum(pw, axis=-1, keepdims=True) + jnp.sum(pc, axis=-1, keepdims=True))
                pw = pw * inv
                pc = pc * inv
                dpw = _dot_nt(dost, v2v)
                dpc = _dot_nt(dost, vc2)
                delta = jnp.sum(pw * dpw, axis=-1, keepdims=True) + jnp.sum(pc * dpc, axis=-1, keepdims=True)
                ds_w = pw * (dpw - delta)
                ds_c = pc * (dpc - delta)
                dtab_s[vi, p] += ds_w
                dsw16 = ds_w.astype(BF16)
                dsc16 = ds_c.astype(BF16)
                dq = (_dot(dsw16, k2v) + _dot(dsc16, kc2)) * scale
                dq_ref[pl.ds(qoff, GRID_W), ls] = _unstack_heads(dq).astype(BF16)
                dkacc[pl.ds(off, KH * GRID_W), ls] += _dot_tn(dsw16, qst) * scale
                dvacc[pl.ds(off, KH * GRID_W), ls] += _dot_tn(pw.astype(BF16), dost)
                dkc_ref[:, ls] += _dot_tn(dsc16, qst) * scale
                dvc_ref[:, ls] += _dot_tn(pc.astype(BF16), dost)
            return carry

        lax.fori_loop(0, QROWS, row_body, 0)

        @pl.when(i == nb - 1)
        def _():
            for t in range(3):
                dst_row = (nb - 3 + t) * qt
                flush(dkacc, dk_hbm, t, dst_row)
                flush(dvacc, dv_hbm, t, dst_row)
            cp = pltpu.make_async_copy(dtab_s, dtab_hbm, sem)
            cp.start()
            cp.wait()
            if exchange:
                exchange.finish(x_in, x_out, x_sems)

    blk = lambda col: [pl.BlockSpec((qt, NA_W), functools.partial(lambda i, t, c: (wb0(i) + t, c), t=t, c=col))
                       for t in range(3)]
    any_spec = pl.BlockSpec(memory_space=pl.ANY)
    hbm = pl.BlockSpec(memory_space=pltpu.HBM)
    res = pl.pallas_call(
        body, name=name, grid=(nb,),
        in_specs=[pl.BlockSpec((qt, NA_W), lambda i: (i, 0))] + blk(1) + blk(2)
                 + [pl.BlockSpec((nctx, NA_W), lambda i: (0, 1)), pl.BlockSpec((nctx, NA_W), lambda i: (0, 2)),
                    pl.BlockSpec((qt, NA_W), lambda i: (i, 0)), any_spec] + [hbm] * nx,
        out_specs=[pl.BlockSpec((qt, NA_W), lambda i: (i, 0)), any_spec, any_spec,
                   pl.BlockSpec((nctx, NA_W), lambda i: (0, 0)), pl.BlockSpec((nctx, NA_W), lambda i: (0, 0)),
                   any_spec] + [hbm] * nx,
        out_shape=[jax.ShapeDtypeStruct((seq, NA_W), BF16), jax.ShapeDtypeStruct((seq, NA_W), BF16),
                   jax.ShapeDtypeStruct((seq, NA_W), BF16), jax.ShapeDtypeStruct((nctx, NA_W), F32),
                   jax.ShapeDtypeStruct((nctx, NA_W), F32), jax.ShapeDtypeStruct(TAB_SHAPE, F32)]
                  + (exchange.out_shapes if exchange else []),
        scratch_shapes=[pltpu.VMEM((wt, NA_W), BF16), pltpu.VMEM((wt, NA_W), BF16),
                        pltpu.VMEM((wt, NA_W), F32), pltpu.VMEM((wt, NA_W), F32),
                        pltpu.VMEM(TAB_SHAPE, F32), pltpu.VMEM(TAB_SHAPE, F32), pltpu.VMEM((qt, NA_W), BF16),
                        pltpu.SemaphoreType.DMA]
                       + (exchange.scratch if exchange else []),
        compiler_params=_params(("arbitrary",)),
    )(qkvu, qkvu, qkvu, qkvu, qkvu, qkvu, qkvu, qkvu_c, qkvu_c, dmix, tab, *(exchange.arrays if exchange else []))
    return res[:6], list(res[6:])


def _halo_specs(te, seq, col, width):
    per = te // HALO
    last = seq // HALO - 1
    return [pl.BlockSpec((HALO, width), lambda i: (jnp.maximum(i * per - 1, 0), col)),
            pl.BlockSpec((te, width), lambda i: (i, col)),
            pl.BlockSpec((HALO, width), lambda i: (jnp.minimum((i + 1) * per, last), col))]


def _extended(prev_ref, cur_ref, next_ref, i, te, seq):
    xe = jnp.concatenate([prev_ref[...], cur_ref[...], next_ref[...]], axis=0).astype(F32)
    pos = i * te - HALO + lax.broadcasted_iota(jnp.int32, (te + 2 * HALO, 1), 0)
    return jnp.where((pos >= 0) & (pos < seq), xe, 0.0), pos


def _window_sum(x, levels, n, mirrored):
    first = (n - 1) if mirrored else 1
    acc = x + pltpu.roll(x, first, 0)
    step = 1
    for _ in range(levels - 1):
        acc = pltpu.roll(acc, step, 0) + pltpu.roll(acc, n - step, 0)
        step *= 2
    return acc


def _window_count(pos, w, seq):
    lo = jnp.clip(pos - w // 2, 0, seq)
    hi = jnp.clip(pos - w // 2 + w, 0, seq)
    return jnp.maximum(hi - lo, 1).astype(F32)


def pool_fwd(qkvu, pool_w, pool_scale, name):
    seq = qkvu.shape[0]
    te = _tile(seq, (512, 256))
    n = te + 2 * HALO

    def body(up_ref, uc_ref, un_ref, w_ref, sc_ref, o_ref):
        i = pl.program_id(0)
        xe, pos = _extended(up_ref, uc_ref, un_ref, i, te, seq)
        cnt = pos[HALO:HALO + te]
        for g, w in enumerate(POOL_WINDOWS):
            ls = slice(g * POOL_G, (g + 1) * POOL_G)
            xg = xe[:, ls]
            win = _window_sum(xg, g + 1, n, False)[HALO:HALO + te]
            dlt = win / _window_count(cnt, w, seq) - xg[HALO:HALO + te]
            z = _dot(dlt.astype(BF16), w_ref[g])
            o_ref[:, ls] = (z * sc_ref[:, ls]).astype(BF16)

    return pl.pallas_call(
        body, name=name, grid=(seq // te,),
        in_specs=_halo_specs(te, seq, 3, POOL_W)
                 + [pl.BlockSpec((4, POOL_G, POOL_G), lambda i: (0, 0, 0)), pl.BlockSpec((1, POOL_W), lambda i: (0, 0))],
        out_specs=pl.BlockSpec((te, POOL_W), lambda i: (i, 0)),
        out_shape=jax.ShapeDtypeStruct((seq, POOL_W), BF16),
        compiler_params=_params(("parallel",)),
    )(qkvu, qkvu, qkvu, pool_w, pool_scale)


def pool_bwd(qkvu, dmix, pool_w, pool_scale, name):
    seq = qkvu.shape[0]
    te = _tile(seq, (512, 256))
    n = te + 2 * HALO

    def body(up_ref, uc_ref, un_ref, dp_ref, dc_ref, dn_ref, w_ref, sc_ref, du_ref, dw_ref, dsc_ref):
        i = pl.program_id(0)

        @pl.when(i == 0)
        def _():
            dw_ref[...] = jnp.zeros_like(dw_ref)
            dsc_ref[...] = jnp.zeros_like(dsc_ref)

        xe, pos = _extended(up_ref, uc_ref, un_ref, i, te, seq)
        de, _ = _extended(dp_ref, dc_ref, dn_ref, i, te, seq)
        cpos = pos[HALO:HALO + te]
        for g, w in enumerate(POOL_WINDOWS):
            ls = slice(g * POOL_G, (g + 1) * POOL_G)
            xg = xe[:, ls]
            wg = w_ref[g]
            win = _window_sum(xg, g + 1, n, False)[HALO:HALO + te]
            dlt = (win / _window_count(cpos, w, seq) - xg[HALO:HALO + te]).astype(BF16)
            z = _dot(dlt, wg)
            dpg = de[:, ls]
            dsc_ref[:, ls] += jnp.sum(dpg[HALO:HALO + te] * z, axis=0, keepdims=True)
            dz = (dpg * sc_ref[:, ls]).astype(BF16)
            dw_ref[g] += _dot_tn(dlt, dz[HALO:HALO + te])
            dd = _dot_nt(dz, wg)
            back = _window_sum(dd / _window_count(pos, w, seq), g + 1, n, True)
            du_ref[:, ls] = (back[HALO:HALO + te] - dd[HALO:HALO + te]).astype(BF16)

    return pl.pallas_call(
        body, name=name, grid=(seq // te,),
        in_specs=_halo_specs(te, seq, 3, POOL_W) + _halo_specs(te, seq, 1, POOL_W)
                 + [pl.BlockSpec((4, POOL_G, POOL_G), lambda i: (0, 0, 0)), pl.BlockSpec((1, POOL_W), lambda i: (0, 0))],
        out_specs=[pl.BlockSpec((te, POOL_W), lambda i: (i, 0)),
                   pl.BlockSpec((4, POOL_G, POOL_G), lambda i: (0, 0, 0)), pl.BlockSpec((1, POOL_W), lambda i: (0, 0))],
        out_shape=[jax.ShapeDtypeStruct((seq, POOL_W), BF16), jax.ShapeDtypeStruct((4, POOL_G, POOL_G), F32),
                   jax.ShapeDtypeStruct((1, POOL_W), F32)],
        compiler_params=_params(("arbitrary",)),
    )(qkvu, qkvu, qkvu, dmix, dmix, dmix, pool_w, pool_scale)


def _shifted(z, zprev_row, znext_row, te):
    rows = lax.broadcasted_iota(jnp.int32, (te, 1), 0)
    zp = jnp.where(rows == 0, zprev_row, pltpu.roll(z, 1, 0))
    zn = jnp.where(rows == te - 1, znext_row, pltpu.roll(z, te - 1, 0))
    return zp, zn


def _edge_rows(prev_ref, next_ref, i, nt):
    p = prev_ref[HALO - 1:HALO, :].astype(F32)
    q = next_ref[0:1, :].astype(F32)
    return jnp.where(i == 0, 0.0, p), jnp.where(i == nt - 1, 0.0, q)


def conv_fwd(proj, conv_w, name):
    seq = proj.shape[0]
    te = _tile(seq, (512, 256))
    nt = seq // te

    def body(bg_ref, cp_ref, cc_ref, cn_ref, xp_ref, xc_ref, xn_ref, w_ref, o_ref):
        i = pl.program_id(0)
        z = cc_ref[...].astype(F32) * xc_ref[...].astype(F32)
        cpr, cnr = _edge_rows(cp_ref, cn_ref, i, nt)
        xpr, xnr = _edge_rows(xp_ref, xn_ref, i, nt)
        zp, zn = _shifted(z, cpr * xpr, cnr * xnr, te)
        y = zp * w_ref[0:1, :] + z * w_ref[1:2, :] + zn * w_ref[2:3, :]
        o_ref[...] = (bg_ref[...].astype(F32) * y).astype(BF16)

    return pl.pallas_call(
        body, name=name, grid=(nt,),
        in_specs=[pl.BlockSpec((te, D), lambda i: (i, 0))] + _halo_specs(te, seq, 1, D) + _halo_specs(te, seq, 2, D)
                 + [pl.BlockSpec((3, D), lambda i: (0, 0))],
        out_specs=pl.BlockSpec((te, D), lambda i: (i, 0)),
        out_shape=jax.ShapeDtypeStruct((seq, D), BF16),
        compiler_params=_params(("parallel",)),
    )(proj, proj, proj, proj, proj, proj, proj, conv_w)


def conv_bwd(proj, dgm, conv_w, name):
    seq = proj.shape[0]
    te = _tile(seq, (512, 256))
    nt = seq // te

    def body(bp_ref, bc_ref, bn_ref, cp_ref, cc_ref, cn_ref, xp_ref, xc_ref, xn_ref, gp_ref, gc_ref, gn_ref, w_ref,
             dbg_ref, dcg_ref, dxin_ref, dw_ref):
        i = pl.program_id(0)

        @pl.when(i == 0)
        def _():
            dw_ref[...] = jnp.zeros_like(dw_ref)

        bg = bc_ref[...].astype(F32)
        cg = cc_ref[...].astype(F32)
        xin = xc_ref[...].astype(F32)
        dg = gc_ref[...].astype(F32)
        z = cg * xin
        cpr, cnr = _edge_rows(cp_ref, cn_ref, i, nt)
        xpr, xnr = _edge_rows(xp_ref, xn_ref, i, nt)
        zp, zn = _shifted(z, cpr * xpr, cnr * xnr, te)
        w0, w1, w2 = w_ref[0:1, :], w_ref[1:2, :], w_ref[2:3, :]
        y = zp * w0 + z * w1 + zn * w2
        dbg_ref[...] = (dg * y).astype(BF16)
        dy = dg * bg
        dw_ref[0:1, :] += jnp.sum(dy * zp, axis=0, keepdims=True)
        dw_ref[1:2, :] += jnp.sum(dy * z, axis=0, keepdims=True)
        dw_ref[2:3, :] += jnp.sum(dy * zn, axis=0, keepdims=True)
        bpr, bnr = _edge_rows(bp_ref, bn_ref, i, nt)
        gpr, gnr = _edge_rows(gp_ref, gn_ref, i, nt)
        dyp, dyn = _shifted(dy, bpr * gpr, bnr * gnr, te)
        dz = dyn * w0 + dy * w1 + dyp * w2
        dcg_ref[...] = (dz * xin).astype(BF16)
        dxin_ref[...] = (dz * cg).astype(BF16)

    row = pl.BlockSpec((te, D), lambda i: (i, 0))
    return pl.pallas_call(
        body, name=name, grid=(nt,),
        in_specs=_halo_specs(te, seq, 0, D) + _halo_specs(te, seq, 1, D) + _halo_specs(te, seq, 2, D)
                 + _halo_specs(te, seq, 0, D) + [pl.BlockSpec((3, D), lambda i: (0, 0))],
        out_specs=[row, row, row, pl.BlockSpec((3, D), lambda i: (0, 0))],
        out_shape=[jax.ShapeDtypeStruct((seq, D), BF16)] * 3 + [jax.ShapeDtypeStruct((3, D), F32)],
        compiler_params=_params(("arbitrary",)),
    )(proj, proj, proj, proj, proj, proj, proj, proj, proj, dgm, dgm, dgm, conv_w)


def _position():
    x, y, c = lax.axis_index("x"), lax.axis_index("y"), lax.axis_index("c")
    return x, y, c, 4 * x + 2 * y + c


def _peer(x, y, c, j):
    px = 1 - x if j & 4 else x
    py = 1 - y if j & 2 else y
    pc = 1 - c if j & 1 else c
    return (px, py, pc), 4 * px + 2 * py + pc


def small_allgather(v, name):
    rows, cols = v.shape

    def body(v_ref, o_ref, send_sems, recv_sems, local_sem):
        x, y, c, me = _position()
        mine = pltpu.make_async_copy(v_ref, o_ref.at[me], local_sem)
        mine.start()
        sends = []
        for j in range(1, N_DEV):
            peer, _ = _peer(x, y, c, j)
            cp = pltpu.make_async_remote_copy(src_ref=v_ref, dst_ref=o_ref.at[me], send_sem=send_sems.at[j - 1],
                                              recv_sem=recv_sems.at[j - 1], device_id=peer, device_id_type=MESH_ID)
            cp.start()
            sends.append(cp)
        for j in range(1, N_DEV):
            peer, pid = _peer(x, y, c, j)
            pltpu.make_async_remote_copy(src_ref=v_ref, dst_ref=o_ref.at[pid], send_sem=send_sems.at[j - 1],
                                         recv_sem=recv_sems.at[j - 1], device_id=peer,
                                         device_id_type=MESH_ID).wait_recv()
        for cp in sends:
            cp.wait_send()
        mine.wait()

    return pl.pallas_call(
        body, name=name,
        out_shape=jax.ShapeDtypeStruct((N_DEV, rows, cols), v.dtype),
        in_specs=[pl.BlockSpec(memory_space=pltpu.VMEM)],
        out_specs=pl.BlockSpec(memory_space=pltpu.VMEM),
        scratch_shapes=[pltpu.SemaphoreType.DMA((N_DEV - 1,)), pltpu.SemaphoreType.DMA((N_DEV - 1,)),
                        pltpu.SemaphoreType.DMA],
        compiler_params=pltpu.CompilerParams(vmem_limit_bytes=VMEM_LIMIT),
    )(v)


class Exchange:
    def __init__(self, kind, arrays):
        self.kind, self.arrays, self.n = kind, list(arrays), len(arrays)
        n = self.n
        if kind == "gather":
            self.out_shapes = [jax.ShapeDtypeStruct((N_DEV,) + a.shape, a.dtype) for a in self.arrays]
        else:
            self.out_shapes = [jax.ShapeDtypeStruct(a.shape, a.dtype) for a in self.arrays]
        self.scratch = [pltpu.SemaphoreType.DMA((7 * n,)), pltpu.SemaphoreType.DMA((7 * n,)),
                        pltpu.SemaphoreType.DMA((n,))]

    def _gather_copies(self, ins, outs, sems):
        send_sems, recv_sems, local_sems = sems
        x, y, c, me = _position()
        chips = [(1 - x, y), (x, 1 - y), (1 - x, 1 - y)]

        def blk(k, px, py, pc):
            return outs[k].at[4 * px + 2 * py + pc]

        def copy(k, slot, block, to, src=None):
            return pltpu.make_async_remote_copy(
                src_ref=blk(k, *block) if src is None else src, dst_ref=blk(k, *block),
                send_sem=send_sems.at[k * 7 + slot], recv_sem=recv_sems.at[k * 7 + slot],
                device_id=to, device_id_type=MESH_ID)

        mine = [pltpu.make_async_copy(ins[k], blk(k, x, y, c), local_sems.at[k]) for k in range(self.n)]
        first = []
        for k in range(self.n):
            first.append(copy(k, 0, (x, y, c), (x, y, 1 - c), src=ins[k]))
            first += [copy(k, 1 + j, (x, y, c), (*chip, c), src=ins[k]) for j, chip in enumerate(chips)]
        return (x, y, c), chips, copy, mine, first

    def start(self, ins, outs, sems):
        if self.kind == "gather":
            _, _, _, mine, first = self._gather_copies(ins, outs, sems)
            for cp in mine + first:
                cp.start()
        else:
            for cp in self._scatter_copies(ins, outs, sems, False):
                cp.start()

    def finish(self, ins, outs, sems):
        if self.kind == "gather":
            (x, y, c), chips, copy, mine, first = self._gather_copies(ins, outs, sems)
            passed = []
            for j, chip in enumerate(chips):
                for k in range(self.n):
                    copy(k, 1 + j, (*chip, c), (x, y, c)).wait_recv()
                    cp = copy(k, 4 + j, (*chip, c), (x, y, 1 - c))
                    cp.start()
                    passed.append(cp)
            for k in range(self.n):
                copy(k, 0, (x, y, 1 - c), (x, y, c)).wait_recv()
                for j, chip in enumerate(chips):
                    copy(k, 4 + j, (*chip, 1 - c), (x, y, c)).wait_recv()
            for cp in first + passed:
                cp.wait_send()
            for cp in mine:
                cp.wait()
        else:
            for cp in self._scatter_copies(ins, outs, sems, True):
                cp.wait_recv()
            copies = self._scatter_copies(ins, outs, sems, False)
            for cp in copies[self.n:]:
                cp.wait_send()
            for cp in copies[:self.n]:
                cp.wait()

    def _scatter_copies(self, ins, outs, sems, arrivals):
        send_sems, recv_sems, local_sems = sems
        x, y, c, me = _position()
        out = []
        if not arrivals:
            out = [pltpu.make_async_copy(ins[k].at[me], outs[k].at[me], local_sems.at[k]) for k in range(self.n)]
        for j in range(1, N_DEV):
            peer, pid = _peer(x, y, c, j)
            for k in range(self.n):
                out.append(pltpu.make_async_remote_copy(
                    src_ref=ins[k].at[pid], dst_ref=outs[k].at[pid if arrivals else me],
                    send_sem=send_sems.at[k * 7 + j - 1], recv_sem=recv_sems.at[k * 7 + j - 1],
                    device_id=peer, device_id_type=MESH_ID))
        return out


def sum_devices(v, name):
    _, rows, cols = v.shape

    def body(v_ref, o_ref):
        acc = v_ref[0]
        for p in range(1, N_DEV):
            acc = acc + v_ref[p]
        o_ref[...] = acc

    return pl.pallas_call(
        body, name=name, out_shape=jax.ShapeDtypeStruct((rows, cols), F32),
        compiler_params=pltpu.CompilerParams(vmem_limit_bytes=VMEM_LIMIT),
    )(v)


def _silu(x):
    return x * _sigmoid(x)


def adaln_fwd(cm, mod_w, mod_b_cols, name):
    cols = mod_w.shape[2]

    def body(c_ref, w_ref, b_ref, o_ref):
        o_ref[0] = jnp.dot(_silu(c_ref[...]), w_ref[0], precision=HI, preferred_element_type=F32) + b_ref[0]

    return pl.pallas_call(
        body, name=name, grid=(2,),
        in_specs=[pl.BlockSpec((16, D), lambda l: (0, 0)), pl.BlockSpec((1, D, cols), lambda l: (l, 0, 0)),
                  pl.BlockSpec((1, 1, cols), lambda l: (l, 0, 0))],
        out_specs=pl.BlockSpec((1, 16, cols), lambda l: (l, 0, 0)),
        out_shape=jax.ShapeDtypeStruct((2, 16, cols), F32),
        compiler_params=_params(("parallel",)),
    )(cm, mod_w, mod_b_cols)


def adaln_bwd(cm_t, mod_w, dm_t, name):
    cols = mod_w.shape[2]

    def body(c_ref, w_ref, lat_ref, ctx_ref, gw_ref, pc_ref):
        ctot = jnp.sum(ctx_ref[0], axis=0, keepdims=True)
        rows = lax.broadcasted_iota(jnp.int32, (8, 1), 0)
        g_hi = jnp.where(rows == 0, ctot, 0.0)
        g = jnp.concatenate([lat_ref[0], g_hi], axis=0)
        gw_ref[0] = jnp.dot(_silu(c_ref[...]), g, precision=HI, preferred_element_type=F32)
        pc_ref[0] = lax.dot_general(g_hi, w_ref[0], NT_DIMS, precision=HI, preferred_element_type=F32)

    return pl.pallas_call(
        body, name=name, grid=(2,),
        in_specs=[pl.BlockSpec((D, 16), lambda l: (0, 0)), pl.BlockSpec((1, D, cols), lambda l: (l, 0, 0)),
                  pl.BlockSpec((1, 8, cols), lambda l: (l, 0, 0)), pl.BlockSpec((1, 8, cols), lambda l: (l + 2, 0, 0))],
        out_specs=[pl.BlockSpec((1, D, cols), lambda l: (l, 0, 0)), pl.BlockSpec((1, 8, D), lambda l: (l, 0, 0))],
        out_shape=[jax.ShapeDtypeStruct((2, D, cols), F32), jax.ShapeDtypeStruct((2, 8, D), F32)],
        compiler_params=_params(("parallel",)),
    )(cm_t, mod_w, dm_t, dm_t)


def mod_b_grad(dm_t, name):
    width = dm_t.shape[2]
    tn = width // 8

    def body(d_ref, o_ref):
        s = jnp.concatenate([jnp.sum(d_ref[k], axis=0, keepdims=True) for k in range(4)]
                            + [jnp.zeros((4, tn), F32)], axis=0)
        o_ref[...] = s + pltpu.roll(s, 6, 0)

    return pl.pallas_call(
        body, name=name, grid=(8,),
        in_specs=[pl.BlockSpec((4, 8, tn), lambda j: (0, 0, j))],
        out_specs=pl.BlockSpec((8, tn), lambda j: (0, j)),
        out_shape=jax.ShapeDtypeStruct((8, width), F32),
        compiler_params=_params(("parallel",)),
    )(dm_t)


def adamw(w, m, v, name, g=None, recv=None):
    rows, cols = w.shape
    tr = _tile(rows, (256, 128, 64, 32, 16, 8))
    summed = recv is not None

    def body(w_ref, m_ref, v_ref, g_ref, go_ref, d_ref, mo_ref, vo_ref):
        if summed:
            gv = g_ref[0].astype(F32)
            for p in range(1, N_DEV):
                gv = gv + g_ref[p].astype(F32)
        else:
            gv = g_ref[...]
        mn = ADAM_B1 * m_ref[...] + (1.0 - ADAM_B1) * gv
        vn = ADAM_B2 * v_ref[...] + (1.0 - ADAM_B2) * (gv * gv)
        m_hat = mn / (1.0 - ADAM_B1 ** ADAM_STEP)
        v_hat = vn / (1.0 - ADAM_B2 ** ADAM_STEP)
        go_ref[...] = gv
        d_ref[...] = -ADAM_LR * (m_hat / (jnp.sqrt(v_hat) + ADAM_EPS) + ADAM_WD * w_ref[...])
        mo_ref[...] = mn
        vo_ref[...] = vn

    row = pl.BlockSpec((tr, cols), lambda i: (i, 0))
    gspec = pl.BlockSpec((N_DEV, tr, cols), lambda i: (0, i, 0)) if summed else row
    return pl.pallas_call(
        body, name=name, grid=(rows // tr,),
        in_specs=[row, row, row, gspec], out_specs=[row] * 4,
        out_shape=[jax.ShapeDtypeStruct((rows, cols), F32)] * 4,
        compiler_params=_params(("parallel",)),
    )(w, m, v, recv if summed else g)


def _ffn_fwd(h, hn, mods, w13, w2, base, tag, nxt=None, exchange=None):
    (p, u, s), exchanged = ffn_up(hn, w13, tag + "_up", exchange)
    if callable(w2):
        w2 = w2(exchanged)
    outs = mm_nn([s], w2, [0], tag + "_down", res=(h, mods, base + 2, 0.5), nxt=nxt)
    h_new, y = outs[0], outs[1]
    return h_new, (outs[2] if nxt else None), (h, hn, p, u, s, y), exchanged


COLUMN_CUT = ("w13", "ewi", "cwi")
GATHER_FIRST = ("w13_00",)
GATHER_IN_FFN = ("w2_00", "ewi", "ewo", "w13_01", "w2_01")
GATHER_IN_ATTN = ("w13_10", "w2_10", "cwi", "cwo", "w13_11", "w2_11")
SCATTER_IN_ATTN = ("w13_11", "w2_11", "cwi", "cwo", "w13_10", "w2_10", "w13_01", "w2_01", "ewo")
SCATTER_LAST = ("w13_00", "w2_00", "ewi")


def unpack_piece(p, g):
    if p.split("_")[0] in COLUMN_CUT:
        return g.transpose(1, 0, 2).reshape(g.shape[1], -1)
    return g.reshape(-1, g.shape[2])


def block_piece(p, full):
    if p.split("_")[0] in COLUMN_CUT:
        return full.reshape(full.shape[0], N_DEV, -1).transpose(1, 0, 2).astype(BF16)
    return full.reshape(N_DEV, -1, full.shape[1]).astype(BF16)


def _ffn_bwd(dy, dres, saved, mods, g, w13, w2, base, tag, prev=None, acc=None, exchange_of=None):
    h, hn, p, u, s, _ = saved
    ff = w2.shape[0]
    acc = acc or (None, None, None)
    da, db = mm_nt(dy, w2, tag + "_ds", dswiglu=(u, p))
    dw2 = mm_tn(s, dy, tag + "_dw2", acc=acc[2])
    dwa = mm_tn(hn, da, tag + "_dw13a", acc=acc[0])
    dwb = mm_tn(hn, db, tag + "_dw13b", acc=acc[1])
    exchange = exchange_of(dwa, dwb, dw2) if exchange_of else None
    outs = mm_nt_norm([da, db], w13, [0, ff], h, g, mods, base + 1, dres, tag + "_dhn", prev=prev, exchange=exchange)
    dh, dshift, dscale, dg = outs[:4]
    nprev = 2 if prev else 0
    return (dh, (dwa, dwb, dw2), dg, {base: dshift, base + 1: dscale}, tuple(outs[4:4 + nprev]),
            list(outs[4 + nprev:]))


def _mod_rows(parts):
    zero = jnp.zeros((1, D), F32)
    return jnp.concatenate([parts.get(k, zero) for k in range(N_MOD)], axis=0)


def local_step(x, ctx, ml, mc, wts, target, shards=None):
    wts = dict(wts)
    ng = wts["norm_g"]
    gvec = lambda l, k: ng[l, k][None, :]
    pool_w16 = wts["pool_w"].astype(BF16)
    grads = {}

    def gather(pieces):
        return Exchange("gather", [shards[p] for p in pieces]) if shards else None

    def arrived(pieces, results):
        for p, g in zip(pieces, results):
            wts[p] = unpack_piece(p, g)

    def ffn_grads(lf, f):
        grads["w13_" + lf] = jnp.concatenate([f[0], f[1]], axis=1)
        grads["w2_" + lf] = f[2]

    if shards:
        xh, got = normmod(x, gvec(0, 0), ml[0], 0, 1, "l0f1_norm", gather(GATHER_FIRST))
        arrived(GATHER_FIRST, got)
    else:
        xh = normmod(x, gvec(0, 0), ml[0], 0, 1, "l0f1_norm")
    ch = normmod(ctx, gvec(0, 0), mc[0], 0, 1, "l0f1c_norm")
    def w2_after_up(got):
        arrived(GATHER_IN_FFN, got)
        return wts["w2_00"]

    x1, xn, sv1, _ = _ffn_fwd(x, xh, ml[0], wts["w13_00"], w2_after_up, 0, "l0f1",
                              nxt=(gvec(0, 1), ml[0], 3, 4), exchange=gather(GATHER_IN_FFN))
    c1, cn, sv1c, _ = _ffn_fwd(ctx, ch, mc[0], wts["w13_00"], wts["w2_00"], 0, "l0f1c", nxt=(gvec(0, 1), mc[0], 3, 4))
    qkvu = mm_nn([xn], wts["ewi"], [0], "l0mix_in")
    qkvu_c = mm_nn([cn], wts["ewi"], [0], "l0mix_in_c")
    tab = bias_table(wts["rpb"])
    att, got = attn_fwd(qkvu, qkvu_c, tab, "l0_attn", gather(GATHER_IN_ATTN))
    arrived(GATHER_IN_ATTN, got)
    pool = pool_fwd(qkvu, pool_w16, wts["pool_scale"], "l0_pool")
    x2, ymix, xh = mm_nn([att, pool], wts["ewo"], [0, NA_W], "l0mix_out", res=(x1, ml[0], 5, 1.0),
                         nxt=(gvec(0, 2), ml[0], 6, 7))
    x3, xh, sv2, _ = _ffn_fwd(x2, xh, ml[0], wts["w13_01"], wts["w2_01"], 6, "l0f2", nxt=(gvec(1, 0), ml[1], 0, 1))

    x4, xn1, sv3, _ = _ffn_fwd(x3, xh, ml[1], wts["w13_10"], wts["w2_10"], 0, "l1f1", nxt=(gvec(1, 1), ml[1], 3, 4))
    proj = mm_nn([xn1], wts["cwi"], [0], "l1mix_in")
    gm = conv_fwd(proj, wts["conv_w"], "l1_conv")
    x5, ycv, xh = mm_nn([gm], wts["cwo"], [0], "l1mix_out", res=(x4, ml[1], 5, 1.0), nxt=(gvec(1, 2), ml[1], 6, 7))
    x6, _, sv4, _ = _ffn_fwd(x5, xh, ml[1], wts["w13_11"], wts["w2_11"], 6, "l1f2")

    dx6, loss, dgf, dy, dgate = loss_head(x6, wts["final_g"][None, :], target, (sv4[5], ml[1], 8, 0.5), "loss_head")
    dm1 = {8: dgate}
    dx5, dwf4, dg12, dm_f4, (dy, dgate), _ = _ffn_bwd(dy, dx6, sv4, ml[1], gvec(1, 2), wts["w13_11"], wts["w2_11"], 6,
                                                      "l1f2", prev=(ycv, ml[1], 5, 1.0))
    ffn_grads("11", dwf4)
    dm1.update({5: dgate, **dm_f4})
    dgm = mm_nt(dy, wts["cwo"], "l1mix_dgm")
    grads["cwo"] = mm_tn(gm, dy, "l1mix_dwo")
    dbg, dcg, dxin, dconv_w = conv_bwd(proj, dgm, wts["conv_w"], "l1_conv_bwd")
    grads["cwi"] = jnp.concatenate([mm_tn(xn1, t, "l1mix_dwi%d" % k) for k, t in enumerate((dbg, dcg, dxin))], axis=1)
    dx4, dsh, dsc, dg11, dy, dgate = mm_nt_norm([dbg, dcg, dxin], wts["cwi"], [0, D, 2 * D], x4, gvec(1, 1), ml[1], 4,
                                                dx5, "l1mix_dxn", prev=(sv3[5], ml[1], 2, 0.5))
    dm1.update({3: dsh, 4: dsc, 2: dgate})
    dx3, dwf3, dg10, dm_f3, (dy, dgate), _ = _ffn_bwd(dy, dx4, sv3, ml[1], gvec(1, 0), wts["w13_10"], wts["w2_10"], 0,
                                                      "l1f1", prev=(sv2[5], ml[0], 8, 0.5))
    ffn_grads("10", dwf3)
    dm1.update(dm_f3)
    dm0 = {8: dgate}

    dx2, dwf2, dg02, dm_f2, (dy, dgate), _ = _ffn_bwd(dy, dx3, sv2, ml[0], gvec(0, 2), wts["w13_01"], wts["w2_01"], 6,
                                                      "l0f2", prev=(ymix, ml[0], 5, 1.0))
    ffn_grads("01", dwf2)
    dm0.update({5: dgate, **dm_f2})
    dmix = mm_nt(dy, wts["ewo"], "l0mix_dmix")
    grads["ewo"] = jnp.concatenate([mm_tn(att, dy, "l0mix_dwo_att"), mm_tn(pool, dy, "l0mix_dwo_pool")], axis=0)
    scatter = Exchange("scatter", [block_piece(p, grads.pop(p)) for p in SCATTER_IN_ATTN]) if shards else None
    (dq, dk, dv, dkc, dvc, dtab), got = attn_bwd(qkvu, qkvu_c, tab, dmix, "l0_attn_bwd", scatter)
    recv = dict(zip(SCATTER_IN_ATTN, got))
    du, dpool_w, dpool_scale = pool_bwd(qkvu, dmix, pool_w16, wts["pool_scale"], "l0_pool_bwd")
    drpb = bias_table_bwd(dtab)
    dk16, dv16, dkc16, dvc16 = dk, dv, dkc.astype(BF16), dvc.astype(BF16)
    grads["ewi"] = jnp.concatenate([
        mm_tn(xn, dq, "l0mix_dwi_q"),
        mm_tn(cn, dkc16, "l0mix_dwi_kc", acc=mm_tn(xn, dk16, "l0mix_dwi_k")),
        mm_tn(cn, dvc16, "l0mix_dwi_vc", acc=mm_tn(xn, dv16, "l0mix_dwi_v")),
        mm_tn(xn, du, "l0mix_dwi_u")], axis=1)
    dx1, dsh, dsc, dg01, dy, dgate = mm_nt_norm([dq, dk16, dv16, du], wts["ewi"], [0, NA_W, 2 * NA_W, 3 * NA_W], x1,
                                                gvec(0, 1), ml[0], 4, dx2, "l0mix_dxn", prev=(sv1[5], ml[0], 2, 0.5))
    dm0.update({3: dsh, 4: dsc, 2: dgate})
    dc1, dsh_c, dsc_c, dg01c, dy_c, dgate_c = mm_nt_norm([dkc16, dvc16], wts["ewi"], [NA_W, 2 * NA_W], c1, gvec(0, 1),
                                                         mc[0], 4, None, "l0mix_dxn_c", prev=(sv1c[5], mc[0], 2, 0.5))
    _, dwf1c, dg00c, dm_f1c, _, _ = _ffn_bwd(dy_c, dc1, sv1c, mc[0], gvec(0, 0), wts["w13_00"], wts["w2_00"], 0, "l0f1c")
    dmc0 = {3: dsh_c, 4: dsc_c, 2: dgate_c, **dm_f1c}

    def last_scatter(dwa, dwb, dw2):
        ffn_grads("00", (dwa, dwb, dw2))
        return Exchange("scatter", [block_piece(p, grads.pop(p)) for p in SCATTER_LAST]) if shards else None

    dx0, _, dg00, dm_f1, _, got = _ffn_bwd(dy, dx1, sv1, ml[0], gvec(0, 0), wts["w13_00"], wts["w2_00"], 0, "l0f1",
                                           acc=dwf1c, exchange_of=last_scatter)
    recv.update(zip(SCATTER_LAST, got))
    dm0.update(dm_f1)

    return {
        "loss": loss, "grad_x": dx0,
        "dml": jnp.stack([_mod_rows(dm0), _mod_rows(dm1)]),
        "dmc": jnp.stack([_mod_rows(dmc0), jnp.zeros((N_MOD, D), F32)]),
        "norm_g": jnp.concatenate([dg00 + dg00c, dg01 + dg01c, dg02, dg10, dg11, dg12], axis=0),
        "grads": grads, "recv": recv,
        "rpb": drpb, "pool_w": dpool_w, "pool_scale": dpool_scale, "conv_w": dconv_w, "final_g": dgf,
    }


def _rows_of(v, nrows):
    flat = v.reshape(-1)
    return jnp.pad(flat, (0, nrows * D - flat.shape[0])).reshape(nrows, D)


def kernel(x, c, ctx, c_ctx, mod_w, mod_b, norm_g, ffn_w13, ffn_w2, even_w_in, even_w_out, na_rpb, pool_w, pool_scale, conv_w_in, conv_w, conv_w_out, final_g, loss_target, m_c_ctx, m_mod_w, m_mod_b, m_norm_g, m_ffn_w13, m_ffn_w2, m_even_w_in, m_even_w_out, m_na_rpb, m_pool_w, m_pool_scale, m_conv_w_in, m_conv_w, m_conv_w_out, m_final_g, v_c_ctx, v_mod_w, v_mod_b, v_norm_g, v_ffn_w13, v_ffn_w2, v_even_w_in, v_even_w_out, v_na_rpb, v_pool_w, v_pool_scale, v_conv_w_in, v_conv_w, v_conv_w_out, v_final_g):
    me = 4 * lax.axis_index("x") + 2 * lax.axis_index("y") + lax.axis_index("c")
    ff = ffn_w2.shape[2] * N_DEV
    w13c = ffn_w13.shape[3]
    w2r = ffn_w2.shape[2]
    mcols = mod_w.shape[2]
    gcols = norm_g.shape[2]

    big = {"w13": ffn_w13.reshape(4 * D, w13c), "w2": ffn_w2.reshape(4 * w2r, D), "ewi": even_w_in[0],
           "ewo": even_w_out[0], "cwi": conv_w_in[0], "cwo": conv_w_out[0]}
    names = list(big)
    shards = {"ewi": even_w_in[0].astype(BF16), "ewo": even_w_out[0].astype(BF16),
              "cwi": conv_w_in[0].astype(BF16), "cwo": conv_w_out[0].astype(BF16)}
    for l in range(2):
        for f in range(2):
            shards["w13_%d%d" % (l, f)] = ffn_w13[l, f].astype(BF16)
            shards["w2_%d%d" % (l, f)] = ffn_w2[l, f].astype(BF16)
    wts = {}

    c_all = small_allgather(jnp.pad(c, ((0, 7), (0, 0))), "cond_allgather")[:, 0, :]
    cm = jnp.concatenate([c_all, c_ctx[None, :], jnp.zeros((7, D), F32)], axis=0)
    mod_b_cols = lax.dynamic_slice(mod_b, (0, me * mcols), (2, mcols))[:, None, :]
    m_cols = adaln_fwd(cm, mod_w, mod_b_cols, "adaln_fwd")
    m_all = small_allgather(m_cols.reshape(32, mcols), "mod_allgather")
    m_full = m_all.reshape(N_DEV, 2, 16, mcols).transpose(1, 2, 0, 3).reshape(2, 16, N_MOD * D)
    ml = lax.dynamic_slice(m_full, (0, me, 0), (2, 1, N_MOD * D)).reshape(2, N_MOD, D)
    mc = m_full[:, 8].reshape(2, N_MOD, D)

    full_norm_g = small_allgather(_rows_of(norm_g, 8), "norm_g_allgather")[:, 0, :2 * 3 * gcols]
    full_norm_g = full_norm_g.reshape(N_DEV, 2, 3, gcols).transpose(1, 2, 0, 3).reshape(2, 3, D)
    full_conv_w = small_allgather(_rows_of(conv_w, 8), "conv_w_allgather")[:, 0, :3 * gcols]
    full_conv_w = full_conv_w.reshape(N_DEV, 3, gcols).transpose(1, 0, 2).reshape(3, D)
    wts.update(norm_g=full_norm_g, conv_w=full_conv_w, rpb=na_rpb[0], pool_w=pool_w[0], pool_scale=pool_scale,
               final_g=final_g)
    out = local_step(x[0], ctx[0], ml, mc, wts, loss_target[0], shards)

    dm_pack = jnp.concatenate([out["dml"].reshape(2, N_MOD * D), out["dmc"].reshape(2, N_MOD * D),
                               jnp.zeros((4, N_MOD * D), F32)], axis=0)
    dm_t = small_allgather(dm_pack, "dmod_allgather").transpose(1, 0, 2)[:4]
    dm_cols = lax.dynamic_slice(dm_t, (0, 0, me * mcols), (4, N_DEV, mcols))
    g_mod_w, pc = adaln_bwd(cm.T, mod_w, dm_cols, "adaln_bwd")
    g_mod_b = mod_b_grad(dm_t, "mod_b_grad")[:2]

    pack = jnp.concatenate([_rows_of(t, 8) for t in (
        out["norm_g"], out["conv_w"], out["final_g"], pc[0, :1] + pc[1, :1], out["pool_scale"], out["loss"],
        out["rpb"])] + [_rows_of(out["pool_w"], 64)], axis=0)
    small = sum_devices(small_allgather(pack, "small_grads_allgather"), "small_grads_sum")
    g_norm_g = lax.dynamic_slice(small[0:6].reshape(2, 3, D), (0, 0, me * gcols), (2, 3, gcols))
    g_conv_w = lax.dynamic_slice(small[8:11], (0, me * gcols), (3, gcols))[None]
    g_final_g = small[16]
    sg = _sigmoid(c_ctx)
    g_c_ctx = small[24] * (sg * (1.0 + c_ctx * (1.0 - sg)))
    g_pool_scale = small[32:33, :POOL_W]
    loss = small[40, 0]
    g_rpb = small[48:52].reshape(-1)[:na_rpb.size].reshape(na_rpb.shape)
    g_pool_w = small[56:120].reshape(pool_w.shape)

    pieces = out["recv"]
    lf = ("00", "01", "10", "11")
    recv = {"w13": jnp.concatenate([pieces["w13_" + t] for t in lf], axis=1),
            "w2": jnp.concatenate([pieces["w2_" + t] for t in lf], axis=1),
            "ewi": pieces["ewi"], "ewo": pieces["ewo"], "cwi": pieces["cwi"], "cwo": pieces["cwo"]}

    moments = {"w13": (m_ffn_w13, v_ffn_w13), "w2": (m_ffn_w2, v_ffn_w2), "ewi": (m_even_w_in, v_even_w_in),
               "ewo": (m_even_w_out, v_even_w_out), "cwi": (m_conv_w_in, v_conv_w_in),
               "cwo": (m_conv_w_out, v_conv_w_out)}
    orig = {"w13": ffn_w13, "w2": ffn_w2, "ewi": even_w_in, "ewo": even_w_out, "cwi": conv_w_in, "cwo": conv_w_out}
    upd = {}
    for k in names:
        shp2 = big[k].shape
        res = adamw(big[k], moments[k][0].reshape(shp2), moments[k][1].reshape(shp2), "adamw_" + k, recv=recv[k])
        upd[k] = [r.reshape(orig[k].shape) for r in res]
    shp2 = (2 * D, mcols)
    upd["mod_w"] = [r.reshape(mod_w.shape) for r in adamw(mod_w.reshape(shp2), m_mod_w.reshape(shp2),
                                                          v_mod_w.reshape(shp2), "adamw_mod_w",
                                                          g=g_mod_w.reshape(shp2))]

    smalls = [("c_ctx", c_ctx, m_c_ctx, v_c_ctx, g_c_ctx, 8), ("mod_b", mod_b, m_mod_b, v_mod_b, g_mod_b, 24),
              ("norm_g", norm_g, m_norm_g, v_norm_g, g_norm_g, 8), ("rpb", na_rpb, m_na_rpb, v_na_rpb, g_rpb, 8),
              ("pool_w", pool_w, m_pool_w, v_pool_w, g_pool_w, 64),
              ("pool_scale", pool_scale, m_pool_scale, v_pool_scale, g_pool_scale, 8),
              ("conv_w", conv_w, m_conv_w, v_conv_w, g_conv_w, 8), ("final_g", final_g, m_final_g, v_final_g, g_final_g, 8)]
    packed = [jnp.concatenate([_rows_of(s[col], s[5]) for s in smalls], axis=0) for col in (1, 2, 3, 4)]
    res = adamw(packed[0], packed[1], packed[2], "adamw_small", g=packed[3])
    row = 0
    for name, w, _, _, _, nrows in smalls:
        upd[name] = [r[row:row + nrows].reshape(-1)[:w.size].reshape(w.shape) for r in res]
        row += nrows

    order = ["c_ctx", "mod_w", "mod_b", "norm_g", "w13", "w2", "ewi", "ewo", "rpb", "pool_w", "pool_scale", "cwi",
             "conv_w", "cwo", "final_g"]
    grad_x = out["grad_x"][None]
    return (loss, grad_x, *[upd[k][0] for k in order], *[upd[k][1] for k in order], *[upd[k][2] for k in order],
            *[upd[k][3] for k in order])
```

```python
import functools

import numpy as np
import jax
import jax.numpy as jnp
from jax import lax
from jax.experimental import pallas as pl
from jax.experimental.pallas import tpu as pltpu

D = 1024
FF = 2816
SEQ = 16384
CTX = 256
GRID_W = 64
N_MOD = 9
HEADS = 8
HEAD_DIM = 64
NA_W = 512
POOL_W = 512
POOL_G = 128
POOL_WINDOWS = (2, 4, 8, 16)
KH = 8
KW = 16
RMS_EPS = 1e-6
NEG_INF = -1e30
N_DEV = 8

ADAM_LR = 0.001
ADAM_B1 = 0.9
ADAM_B2 = 0.999
ADAM_EPS = 1e-08
ADAM_WD = 0.01
ADAM_STEP = 10

VMEM_LIMIT = 52 * 1024 * 1024
HALO = 16
QROWS = 8
WROWS = 24

BF16 = jnp.bfloat16
F32 = jnp.float32
MESH_ID = pl.DeviceIdType.MESH
HI = lax.Precision.HIGHEST

NT_DIMS = (((1,), (1,)), ((), ()))
TN_DIMS = (((0,), (0,)), ((), ()))


def _tile(n, cands):
    for c in cands:
        if n % c == 0:
            return c
    return n


def _params(sem):
    return pltpu.CompilerParams(dimension_semantics=sem, vmem_limit_bytes=VMEM_LIMIT)


def _dot(a, b):
    return jnp.dot(a, b, preferred_element_type=F32)


def _dot_nt(a, b):
    return lax.dot_general(a, b, NT_DIMS, preferred_element_type=F32)


def _dot_tn(a, b):
    return lax.dot_general(a, b, TN_DIMS, preferred_element_type=F32)


def _sigmoid(x):
    return 1.0 / (1.0 + jnp.exp(-x))


def normmod(h, g, mods, i_shift, i_scale, name, exchange=None):
    n = h.shape[0]
    te = _tile(n, (512, 256))
    nt = n // te
    nx = exchange.n if exchange else 0

    def body(*refs):
        h_ref, g_ref, m_ref = refs[:3]
        x_in = refs[3:3 + nx]
        o_ref = refs[3 + nx]
        x_out = refs[4 + nx:4 + 2 * nx]
        x_sems = refs[4 + 2 * nx:]
        if exchange:
            @pl.when(pl.program_id(0) == 0)
            def _():
                exchange.start(x_in, x_out, x_sems)

        x = h_ref[...]
        r = lax.rsqrt(jnp.mean(x * x, axis=-1, keepdims=True) + RMS_EPS)
        y = x * r * g_ref[...]
        o_ref[...] = (y * (1.0 + m_ref[i_scale:i_scale + 1, :]) + m_ref[i_shift:i_shift + 1, :]).astype(BF16)

        if exchange:
            @pl.when(pl.program_id(0) == nt - 1)
            def _():
                exchange.finish(x_in, x_out, x_sems)

    hbm = pl.BlockSpec(memory_space=pltpu.HBM)
    res = pl.pallas_call(
        body, name=name, grid=(nt,),
        in_specs=[pl.BlockSpec((te, D), lambda i: (i, 0)),
                  pl.BlockSpec((1, D), lambda i: (0, 0)),
                  pl.BlockSpec((N_MOD, D), lambda i: (0, 0))] + [hbm] * nx,
        out_specs=[pl.BlockSpec((te, D), lambda i: (i, 0))] + [hbm] * nx,
        out_shape=[jax.ShapeDtypeStruct((n, D), BF16)] + (exchange.out_shapes if exchange else []),
        scratch_shapes=exchange.scratch if exchange else [],
        compiler_params=_params(("arbitrary",) if exchange else ("parallel",)),
    )(h, g, mods, *(exchange.arrays if exchange else []))
    return (res[0], list(res[1:])) if exchange else res[0]


def loss_head(x, g, target, prev, name):
    n = x.shape[0]
    te = _tile(n, (256,))
    i_gate, coef = prev[2], prev[3]

    def body(x_ref, g_ref, t_ref, y_ref, m_ref, dx_ref, loss_ref, dg_ref, dy_ref, dgate_ref):
        @pl.when(pl.program_id(0) == 0)
        def _():
            loss_ref[...] = jnp.zeros_like(loss_ref)
            dg_ref[...] = jnp.zeros_like(dg_ref)
            dgate_ref[...] = jnp.zeros_like(dgate_ref)

        xv = x_ref[...]
        gv = g_ref[...]
        r = lax.rsqrt(jnp.mean(xv * xv, axis=-1, keepdims=True) + RMS_EPS)
        xhat = xv * r
        e = xhat * gv - t_ref[...]
        per_tok = jnp.mean(e * e, axis=-1, keepdims=True)
        loss_ref[...] += 0.5 * jnp.sum(per_tok, axis=0, keepdims=True)
        dy = e * (1.0 / D)
        dg_ref[...] += jnp.sum(dy * xhat, axis=0, keepdims=True)
        dxhat = dy * gv
        dx = r * (dxhat - xhat * jnp.mean(dxhat * xhat, axis=-1, keepdims=True))
        dx_ref[...] = dx
        dy_ref[...] = (dx * (coef * m_ref[i_gate:i_gate + 1, :])).astype(BF16)
        dgate_ref[...] += coef * jnp.sum(dx * y_ref[...].astype(F32), axis=0, keepdims=True)

    row = pl.BlockSpec((te, D), lambda i: (i, 0))
    vec = pl.BlockSpec((1, D), lambda i: (0, 0))
    return pl.pallas_call(
        body, name=name, grid=(n // te,),
        in_specs=[row, vec, row, row, pl.BlockSpec((N_MOD, D), lambda i: (0, 0))],
        out_specs=[row, pl.BlockSpec((1, 128), lambda i: (0, 0)), vec, row, vec],
        out_shape=[jax.ShapeDtypeStruct((n, D), F32), jax.ShapeDtypeStruct((1, 128), F32),
                   jax.ShapeDtypeStruct((1, D), F32), jax.ShapeDtypeStruct((n, D), BF16),
                   jax.ShapeDtypeStruct((1, D), F32)],
        compiler_params=_params(("arbitrary",)),
    )(x, g, target, prev[0], prev[1])


def ffn_up(hn, w13, name, exchange=None):
    n = hn.shape[0]
    ff = w13.shape[1] // 2
    tm = _tile(n, (512, 256))
    tn = _tile(ff, (1408, 512, 256, 128))
    nj = ff // tn
    ni = n // tm
    nx = exchange.n if exchange else 0

    def body(*refs):
        h_ref, wa_ref, wb_ref = refs[:3]
        x_in = refs[3:3 + nx]
        p_ref, u_ref, s_ref = refs[3 + nx:6 + nx]
        x_out = refs[6 + nx:6 + 2 * nx]
        x_sems = refs[6 + 2 * nx:]
        if exchange:
            @pl.when((pl.program_id(0) == 0) & (pl.program_id(1) == 0))
            def _():
                exchange.start(x_in, x_out, x_sems)

        hv = h_ref[...]
        a = _dot(hv, wa_ref[...])
        b = _dot(hv, wb_ref[...])
        sig = _sigmoid(a)
        p = a * sig
        p_ref[...] = p.astype(BF16)
        u_ref[...] = (b * (sig * (1.0 + a * (1.0 - sig)))).astype(BF16)
        s_ref[...] = (p * b).astype(BF16)

        if exchange:
            @pl.when((pl.program_id(0) == nj - 1) & (pl.program_id(1) == ni - 1))
            def _():
                exchange.finish(x_in, x_out, x_sems)

    out = pl.BlockSpec((tm, tn), lambda j, i: (i, j))
    hbm = pl.BlockSpec(memory_space=pltpu.HBM)
    sem = ("arbitrary", "arbitrary") if exchange else ("parallel", "parallel")
    res = pl.pallas_call(
        body, name=name, grid=(nj, ni),
        in_specs=[pl.BlockSpec((tm, D), lambda j, i: (i, 0)),
                  pl.BlockSpec((D, tn), lambda j, i: (0, j)),
                  pl.BlockSpec((D, tn), lambda j, i: (0, j + nj))] + [hbm] * nx,
        out_specs=[out, out, out] + [hbm] * nx,
        out_shape=[jax.ShapeDtypeStruct((n, ff), BF16)] * 3 + (exchange.out_shapes if exchange else []),
        scratch_shapes=exchange.scratch if exchange else [],
        compiler_params=_params(sem),
    )(hn, w13, w13, *(exchange.arrays if exchange else []))
    return res[:3], list(res[3:])


def mm_nn(a_list, w, row_offs, name, out_dtype=BF16, res=None, nxt=None):
    n = a_list[0].shape[0]
    nout = w.shape[1]
    ks = [a.shape[1] for a in a_list]
    tm = _tile(n, (512, 256) if res is not None else (1024, 512, 256))
    tn = _tile(nout, (1024, 512, 256, 128))
    na = len(a_list)
    assert nxt is None or (res is not None and tn == D)

    def body(*refs):
        a_refs = refs[:na]
        w_refs = refs[na:2 * na]
        acc = _dot(a_refs[0][...], w_refs[0][...])
        for k in range(1, na):
            acc += _dot(a_refs[k][...], w_refs[k][...])
        if res is None:
            refs[2 * na][...] = acc.astype(out_dtype)
        else:
            h_ref, m_ref = refs[2 * na:2 * na + 2]
            i_gate, coef = res[2], res[3]
            h_new = h_ref[...] + (coef * m_ref[i_gate:i_gate + 1, :]) * acc
            if nxt is None:
                hn_ref, y_ref = refs[2 * na + 2:]
            else:
                g2_ref, m2_ref, hn_ref, y_ref, nx_ref = refs[2 * na + 2:]
                r = lax.rsqrt(jnp.mean(h_new * h_new, axis=-1, keepdims=True) + RMS_EPS)
                nx_ref[...] = ((h_new * r * g2_ref[...]) * (1.0 + m2_ref[nxt[3]:nxt[3] + 1, :])
                               + m2_ref[nxt[2]:nxt[2] + 1, :]).astype(BF16)
            hn_ref[...] = h_new
            y_ref[...] = acc.astype(BF16)

    in_specs = [pl.BlockSpec((tm, k), lambda j, i: (i, 0)) for k in ks]
    for k, off in zip(ks, row_offs):
        in_specs.append(pl.BlockSpec((k, tn), functools.partial(lambda j, i, ob: (ob, j), ob=off // k)))
    args = list(a_list) + [w] * na
    out = pl.BlockSpec((tm, tn), lambda j, i: (i, j))
    if res is None:
        out_specs = out
        out_shape = jax.ShapeDtypeStruct((n, nout), out_dtype)
    else:
        in_specs += [out, pl.BlockSpec((N_MOD, tn), lambda j, i: (0, j))]
        args += [res[0], res[1]]
        out_specs = [out, out]
        out_shape = [jax.ShapeDtypeStruct((n, nout), F32), jax.ShapeDtypeStruct((n, nout), BF16)]
        if nxt is not None:
            in_specs += [pl.BlockSpec((1, D), lambda j, i: (0, 0)), pl.BlockSpec((N_MOD, D), lambda j, i: (0, 0))]
            args += [nxt[0], nxt[1]]
            out_specs.append(out)
            out_shape.append(jax.ShapeDtypeStruct((n, nout), BF16))
    return pl.pallas_call(
        body, name=name, grid=(nout // tn, n // tm),
        in_specs=in_specs, out_specs=out_specs, out_shape=out_shape,
        compiler_params=_params(("parallel", "parallel")),
    )(*args)


def mm_nt(g, w, name, dswiglu=None):
    n, kg = g.shape
    nout = w.shape[0]
    tm = _tile(n, (512, 256) if dswiglu is not None else (1024, 512, 256))
    tn = _tile(nout, (1408, 1024, 512, 256, 128))

    def body(*refs):
        r = _dot_nt(refs[0][...], refs[1][...])
        if dswiglu is None:
            refs[2][...] = r.astype(BF16)
        else:
            u_ref, p_ref, da_ref, db_ref = refs[2:]
            da_ref[...] = (r * u_ref[...].astype(F32)).astype(BF16)
            db_ref[...] = (r * p_ref[...].astype(F32)).astype(BF16)

    out = pl.BlockSpec((tm, tn), lambda j, i: (i, j))
    in_specs = [pl.BlockSpec((tm, kg), lambda j, i: (i, 0)), pl.BlockSpec((tn, kg), lambda j, i: (j, 0))]
    args = [g, w]
    if dswiglu is None:
        out_specs = out
        out_shape = jax.ShapeDtypeStruct((n, nout), BF16)
    else:
        in_specs += [out, out]
        args += list(dswiglu)
        out_specs = [out, out]
        out_shape = [jax.ShapeDtypeStruct((n, nout), BF16)] * 2
    return pl.pallas_call(
        body, name=name, grid=(nout // tn, n // tm),
        in_specs=in_specs, out_specs=out_specs, out_shape=out_shape,
        compiler_params=_params(("parallel", "parallel")),
    )(*args)


def mm_nt_norm(g_list, w, col_offs, h, g, mods, i_scale, dres, name, prev=None, exchange=None):
    n = h.shape[0]
    kg = g_list[0].shape[1]
    tm = _tile(n, (512, 256))
    tk = _tile(kg, (1408, 1024, 512, 256, 128))
    ng = len(g_list)
    nk = kg // tk
    ni = n // tm
    has_res = dres is not None
    nx = exchange.n if exchange else 0

    def body(*refs):
        g_refs = refs[:ng]
        w_refs = refs[ng:2 * ng]
        pos = 2 * ng
        h_ref, gv_ref, m_ref = refs[pos:pos + 3]
        pos += 3
        if has_res:
            dres_ref = refs[pos]
            pos += 1
        if prev is not None:
            y_ref, mp_ref = refs[pos:pos + 2]
            pos += 2
        x_in = refs[pos:pos + nx]
        pos += nx
        dh_ref, dshift_ref, dscale_ref, dg_ref = refs[pos:pos + 4]
        pos += 4
        if prev is not None:
            dy_ref, dgate_ref = refs[pos:pos + 2]
            pos += 2
        x_out = refs[pos:pos + nx]
        pos += nx
        acc_ref = refs[pos]
        x_sems = refs[pos + 1:]
        i = pl.program_id(0)
        k = pl.program_id(1)

        @pl.when((i == 0) & (k == 0))
        def _():
            if exchange:
                exchange.start(x_in, x_out, x_sems)
            dshift_ref[...] = jnp.zeros_like(dshift_ref)
            dscale_ref[...] = jnp.zeros_like(dscale_ref)
            dg_ref[...] = jnp.zeros_like(dg_ref)
            if prev is not None:
                dgate_ref[...] = jnp.zeros_like(dgate_ref)

        def dots():
            acc = _dot_nt(g_refs[0][...], w_refs[0][...])
            for q in range(1, ng):
                acc += _dot_nt(g_refs[q][...], w_refs[q][...])
            return acc

        if nk > 1:
            @pl.when(k == 0)
            def _():
                acc_ref[...] = dots()

        if nk > 2:
            @pl.when((k > 0) & (k < nk - 1))
            def _():
                acc_ref[...] += dots()

        @pl.when(k == nk - 1)
        def _():
            d = dots() + acc_ref[...] if nk > 1 else dots()
            x = h_ref[...]
            gv = gv_ref[...]
            r = lax.rsqrt(jnp.mean(x * x, axis=-1, keepdims=True) + RMS_EPS)
            xhat = x * r
            one_scale = 1.0 + m_ref[i_scale:i_scale + 1, :]
            t = d * xhat
            tsum = jnp.sum(t, axis=0, keepdims=True)
            dshift_ref[...] += jnp.sum(d, axis=0, keepdims=True)
            dscale_ref[...] += gv * tsum
            dg_ref[...] += one_scale * tsum
            cvec = one_scale * gv
            dh = r * (d * cvec - xhat * jnp.mean(t * cvec, axis=-1, keepdims=True))
            if has_res:
                dh = dh + dres_ref[...]
            dh_ref[...] = dh
            if prev is not None:
                i_gate, coef = prev[2], prev[3]
                dy_ref[...] = (dh * (coef * mp_ref[i_gate:i_gate + 1, :])).astype(BF16)
                dgate_ref[...] += coef * jnp.sum(dh * y_ref[...].astype(F32), axis=0, keepdims=True)

        if exchange:
            @pl.when((i == ni - 1) & (k == nk - 1))
            def _():
                exchange.finish(x_in, x_out, x_sems)

    row = pl.BlockSpec((tm, D), lambda i, k: (i, 0))
    vec = pl.BlockSpec((1, D), lambda i, k: (0, 0))
    modspec = pl.BlockSpec((N_MOD, D), lambda i, k: (0, 0))
    in_specs = [pl.BlockSpec((tm, tk), lambda i, k: (i, k)) for _ in g_list]
    for off in col_offs:
        in_specs.append(pl.BlockSpec((D, tk), functools.partial(lambda i, k, ob: (0, ob + k), ob=off // tk)))
    in_specs += [row, vec, modspec]
    args = list(g_list) + [w] * ng + [h, g, mods]
    out_specs = [row, vec, vec, vec]
    out_shape = [jax.ShapeDtypeStruct((n, D), F32)] + [jax.ShapeDtypeStruct((1, D), F32)] * 3
    if has_res:
        in_specs.append(row)
        args.append(dres)
    if prev is not None:
        in_specs += [row, modspec]
        args += [prev[0], prev[1]]
        out_specs += [row, vec]
        out_shape += [jax.ShapeDtypeStruct((n, D), BF16), jax.ShapeDtypeStruct((1, D), F32)]
    scratch = [pltpu.VMEM((tm, D), F32)]
    if exchange:
        hbm = pl.BlockSpec(memory_space=pltpu.HBM)
        in_specs += [hbm] * nx
        args += exchange.arrays
        out_specs += [hbm] * nx
        out_shape += exchange.out_shapes
        scratch += exchange.scratch
    return pl.pallas_call(
        body, name=name, grid=(ni, nk),
        in_specs=in_specs, out_specs=out_specs, out_shape=out_shape, scratch_shapes=scratch,
        compiler_params=_params(("arbitrary", "arbitrary")),
    )(*args)


def mm_tn(a, g, name, acc=None):
    n, ka = a.shape
    ngc = g.shape[1]
    tka = _tile(ka, (1408, 1024, 512, 256, 128))
    tng = _tile(ngc, (1408, 1024, 512, 256, 128))
    tr = _tile(n, (2048, 1024, 512, 256))
    has_acc = acc is not None

    def body(*refs):
        a_ref, g_ref = refs[0], refs[1]
        o_ref = refs[-1]
        r = pl.program_id(2)

        @pl.when(r == 0)
        def _():
            d = _dot_tn(a_ref[...], g_ref[...])
            o_ref[...] = d + refs[2][...] if has_acc else d

        @pl.when(r > 0)
        def _():
            o_ref[...] += _dot_tn(a_ref[...], g_ref[...])

    out = pl.BlockSpec((tka, tng), lambda p, q, r: (p, q))
    in_specs = [pl.BlockSpec((tr, tka), lambda p, q, r: (r, p)),
                pl.BlockSpec((tr, tng), lambda p, q, r: (r, q))]
    args = [a, g]
    if has_acc:
        in_specs.append(out)
        args.append(acc)
    return pl.pallas_call(
        body, name=name, grid=(ka // tka, ngc // tng, n // tr),
        in_specs=in_specs, out_specs=out,
        out_shape=jax.ShapeDtypeStruct((ka, ngc), F32),
        compiler_params=_params(("parallel", "parallel", "arbitrary")),
    )(*args)


def mm_tn_multi(a, g_list, name, acc=None):
    n, ka = a.shape
    ngc = g_list[0].shape[1]
    ng = len(g_list)
    tka = _tile(ka, (512, 256, 128))
    tr = _tile(n, (1024, 512, 256))
    has_acc = acc is not None

    def body(*refs):
        a_ref = refs[0]
        g_refs = refs[1:1 + ng]
        o_ref = refs[-1]
        r = pl.program_id(1)

        @pl.when(r == 0)
        def _():
            av = a_ref[...]
            for q in range(ng):
                cs = slice(q * ngc, (q + 1) * ngc)
                d = _dot_tn(av, g_refs[q][...])
                o_ref[:, cs] = d + refs[1 + ng][:, cs] if has_acc else d

        @pl.when(r > 0)
        def _():
            av = a_ref[...]
            for q in range(ng):
                cs = slice(q * ngc, (q + 1) * ngc)
                o_ref[:, cs] += _dot_tn(av, g_refs[q][...])

    out = pl.BlockSpec((tka, ng * ngc), lambda p, r: (p, 0))
    in_specs = [pl.BlockSpec((tr, tka), lambda p, r: (r, p))]
    in_specs += [pl.BlockSpec((tr, ngc), lambda p, r: (r, 0)) for _ in g_list]
    args = [a] + list(g_list)
    if has_acc:
        in_specs.append(out)
        args.append(acc)
    return pl.pallas_call(
        body, name=name, grid=(ka // tka, n // tr),
        in_specs=in_specs, out_specs=out,
        out_shape=jax.ShapeDtypeStruct((ka, ng * ngc), F32),
        compiler_params=_params(("parallel", "arbitrary")),
    )(*args)


def mm_small(a, b, name, trans_b=False):
    m = a.shape[0]
    nout = b.shape[0] if trans_b else b.shape[1]

    def body(a_ref, b_ref, o_ref):
        if trans_b:
            o_ref[...] = lax.dot_general(a_ref[...], b_ref[...], NT_DIMS, precision=HI, preferred_element_type=F32)
        else:
            o_ref[...] = jnp.dot(a_ref[...], b_ref[...], precision=HI, preferred_element_type=F32)

    return pl.pallas_call(
        body, name=name,
        out_shape=jax.ShapeDtypeStruct((m, nout), F32),
        compiler_params=pltpu.CompilerParams(vmem_limit_bytes=VMEM_LIMIT),
    )(a, b)


def _col_tables():
    col = np.arange(GRID_W)
    start = np.clip(col - KW // 2, 0, GRID_W - KW)
    ok = (col[None, :] >= start[:, None]) & (col[None, :] < start[:, None] + KW)
    ci = np.clip(col[None, :] - col[:, None] + (KW - 1), 0, 2 * KW - 2)
    e = np.zeros((2 * KW - 1, GRID_W, GRID_W), np.float32)
    for c in range(2 * KW - 1):
        e[c] = (ci == c) & ok
    return e.reshape(2 * KW - 1, GRID_W * GRID_W), ok


def bias_table(rpb):
    e, ok = _col_tables()
    e_pad = np.zeros((32, GRID_W * GRID_W), np.float32)
    e_pad[:31] = e
    rp = jnp.pad(rpb.reshape(HEADS * 15, 31), ((0, 0), (0, 1)))
    t = mm_small(rp, jnp.asarray(e_pad), "rpb_expand").reshape(HEADS, 15, GRID_W, GRID_W)
    t = jnp.where(jnp.asarray(ok)[None, None], t, NEG_INF)
    tab = jnp.stack([t[:, v:v + KH] for v in range(8)], axis=0)
    return tab.transpose(0, 1, 3, 2, 4).reshape(TAB_SHAPE)


def bias_table_bwd(dtab):
    e, _ = _col_tables()
    e_pad = np.zeros((128, GRID_W * GRID_W), np.float32)
    e_pad[:31] = e
    d = dtab.reshape(8, HEADS, GRID_W, KH, GRID_W).transpose(0, 1, 3, 2, 4).reshape(8 * HEADS * KH, GRID_W * GRID_W)
    gv = mm_small(d, jnp.asarray(e_pad), "rpb_reduce", trans_b=True)[:, :31]
    gv = gv.reshape(8, HEADS, KH, 31).transpose(0, 2, 1, 3).reshape(8 * KH, HEADS * 31)
    sel = np.zeros((16, 8 * KH), np.float32)
    for v in range(8):
        for j in range(KH):
            sel[v + j, v * KH + j] = 1.0
    gpad = jnp.pad(gv, ((0, 0), (0, 256 - HEADS * 31)))
    out = mm_small(jnp.asarray(sel), gpad, "rpb_fold")[:15, :HEADS * 31]
    return out.reshape(15, HEADS, 31).transpose(1, 0, 2)


def _attn_geometry(seq):
    rows = seq // GRID_W
    nb = rows // QROWS
    return rows, nb


def _stack_heads(t2):
    first = (lax.broadcasted_iota(jnp.int32, (1, 128), 1) // HEAD_DIM) == 0
    zero = jnp.zeros_like(t2)
    return jnp.concatenate([jnp.where(first, t2, zero), jnp.where(first, zero, t2)], axis=0)


def _unstack_heads(t):
    first = (lax.broadcasted_iota(jnp.int32, (1, 128), 1) // HEAD_DIM) == 0
    return jnp.where(first, t[0:GRID_W], t[GRID_W:2 * GRID_W])


TAB_SHAPE = (8, HEADS // 2, 2 * GRID_W, KH * GRID_W)


def attn_fwd(qkvu, qkvu_c, tab, name, exchange=None):
    seq = qkvu.shape[0]
    nctx = qkvu_c.shape[0]
    rows, nb = _attn_geometry(seq)
    qt = QROWS * GRID_W
    wt = WROWS * GRID_W
    scale = HEAD_DIM ** -0.5
    nx = exchange.n if exchange else 0

    def wb0(i):
        return jnp.clip(i - 1, 0, nb - 3)

    def body(*refs):
        q_ref, k0, k1, k2, v0, v1, v2, kc_ref, vc_ref, tab_hbm = refs[:10]
        x_in = refs[10:10 + nx]
        o_ref = refs[10 + nx]
        x_out = refs[11 + nx:11 + 2 * nx]
        kbuf, vbuf, tab_s, sem = refs[11 + 2 * nx:15 + 2 * nx]
        x_sems = refs[15 + 2 * nx:]
        i = pl.program_id(0)

        @pl.when(i == 0)
        def _():
            if exchange:
                exchange.start(x_in, x_out, x_sems)
            cp = pltpu.make_async_copy(tab_hbm, tab_s, sem)
            cp.start()
            cp.wait()

        for t, (kr, vr) in enumerate(((k0, v0), (k1, v1), (k2, v2))):
            kbuf[t * qt:(t + 1) * qt, :] = kr[...]
            vbuf[t * qt:(t + 1) * qt, :] = vr[...]
        base = wb0(i) * QROWS

        def row_body(rl, carry):
            r = i * QROWS + rl
            rs = jnp.clip(r - KH // 2, 0, rows - KH)
            vi = rs - r + (KH - 1)
            off = pl.multiple_of((rs - base) * GRID_W, GRID_W)
            qoff = pl.multiple_of(rl * GRID_W, GRID_W)
            for p in range(HEADS // 2):
                ls = slice(p * 128, (p + 1) * 128)
                qst = _stack_heads(q_ref[pl.ds(qoff, GRID_W), ls])
                k2v = kbuf[pl.ds(off, KH * GRID_W), ls]
                v2v = vbuf[pl.ds(off, KH * GRID_W), ls]
                s_w = _dot_nt(qst, k2v) * scale + tab_s[vi, p]
                s_c = _dot_nt(qst, kc_ref[:, ls]) * scale
                m = jnp.maximum(jnp.max(s_w, axis=-1, keepdims=True), jnp.max(s_c, axis=-1, keepdims=True))
                pw = jnp.exp(s_w - m)
                pc = jnp.exp(s_c - m)
                l = jnp.sum(pw, axis=-1, keepdims=True) + jnp.sum(pc, axis=-1, keepdims=True)
                o = _dot(pw.astype(BF16), v2v) + _dot(pc.astype(BF16), vc_ref[:, ls])
                o_ref[pl.ds(qoff, GRID_W), ls] = _unstack_heads(o * (1.0 / l)).astype(BF16)
            return carry

        lax.fori_loop(0, QROWS, row_body, 0)

        if exchange:
            @pl.when(i == nb - 1)
            def _():
                exchange.finish(x_in, x_out, x_sems)

    blk = lambda col: [pl.BlockSpec((qt, NA_W), functools.partial(lambda i, t, c: (wb0(i) + t, c), t=t, c=col))
                       for t in range(3)]
    hbm = pl.BlockSpec(memory_space=pltpu.HBM)
    res = pl.pallas_call(
        body, name=name, grid=(nb,),
        in_specs=[pl.BlockSpec((qt, NA_W), lambda i: (i, 0))] + blk(1) + blk(2)
                 + [pl.BlockSpec((nctx, NA_W), lambda i: (0, 1)), pl.BlockSpec((nctx, NA_W), lambda i: (0, 2)),
                    pl.BlockSpec(memory_space=pl.ANY)] + [hbm] * nx,
        out_specs=[pl.BlockSpec((qt, NA_W), lambda i: (i, 0))] + [hbm] * nx,
        out_shape=[jax.ShapeDtypeStruct((seq, NA_W), BF16)] + (exchange.out_shapes if exchange else []),
        scratch_shapes=[pltpu.VMEM((wt, NA_W), BF16), pltpu.VMEM((wt, NA_W), BF16),
                        pltpu.VMEM(TAB_SHAPE, F32), pltpu.SemaphoreType.DMA] + (exchange.scratch if exchange else []),
        compiler_params=_params(("arbitrary",)),
    )(qkvu, qkvu, qkvu, qkvu, qkvu, qkvu, qkvu, qkvu_c, qkvu_c, tab, *(exchange.arrays if exchange else []))
    return res[0], list(res[1:])


def attn_bwd(qkvu, qkvu_c, tab, dmix, name, exchange=None):
    seq = qkvu.shape[0]
    nctx = qkvu_c.shape[0]
    rows, nb = _attn_geometry(seq)
    qt = QROWS * GRID_W
    wt = WROWS * GRID_W
    scale = HEAD_DIM ** -0.5
    nx = exchange.n if exchange else 0

    def wb0(i):
        return jnp.clip(i - 1, 0, nb - 3)

    def body(*refs):
        q_ref, k0, k1, k2, v0, v1, v2, kc_ref, vc_ref, do_ref, tab_hbm = refs[:11]
        x_in = refs[11:11 + nx]
        dq_ref, dk_hbm, dv_hbm, dkc_ref, dvc_ref, dtab_hbm = refs[11 + nx:17 + nx]
        x_out = refs[17 + nx:17 + 2 * nx]
        kbuf, vbuf, dkacc, dvacc, tab_s, dtab_s, stage, sem = refs[17 + 2 * nx:25 + 2 * nx]
        x_sems = refs[25 + 2 * nx:]
        i = pl.program_id(0)

        if exchange:
            @pl.when(i == 0)
            def _():
                exchange.start(x_in, x_out, x_sems)

        def flush(src, dst, block, dst_row):
            stage[...] = src[block * qt:(block + 1) * qt, :].astype(BF16)
            cp = pltpu.make_async_copy(stage, dst.at[pl.ds(dst_row, qt)], sem)
            cp.start()
            cp.wait()

        @pl.when(i == 0)
        def _():
            cp = pltpu.make_async_copy(tab_hbm, tab_s, sem)
            cp.start()
            cp.wait()
            dtab_s[...] = jnp.zeros_like(dtab_s)
            dkacc[...] = jnp.zeros_like(dkacc)
            dvacc[...] = jnp.zeros_like(dvacc)
            dkc_ref[...] = jnp.zeros_like(dkc_ref)
            dvc_ref[...] = jnp.zeros_like(dvc_ref)

        @pl.when((i >= 2) & (i <= nb - 2))
        def _():
            dst_row = pl.multiple_of((i - 2) * qt, qt)
            for acc_ref, dst in ((dkacc, dk_hbm), (dvacc, dv_hbm)):
                flush(acc_ref, dst, 0, dst_row)
                acc_ref[0:qt, :] = acc_ref[qt:2 * qt, :]
                acc_ref[qt:2 * qt, :] = acc_ref[2 * qt:3 * qt, :]
                acc_ref[2 * qt:3 * qt, :] = jnp.zeros((qt, NA_W), F32)

        for t, (kr, vr) in enumerate(((k0, v0), (k1, v1), (k2, v2))):
            kbuf[t * qt:(t + 1) * qt, :] = kr[...]
            vbuf[t * qt:(t + 1) * qt, :] = vr[...]
        base = wb0(i) * QROWS

        def row_body(rl, carry):
            r = i * QROWS + rl
            rs = jnp.clip(r - KH // 2, 0, rows - KH)
            vi = rs - r + (KH - 1)
            off = pl.multiple_of((rs - base) * GRID_W, GRID_W)
            qoff = pl.multiple_of(rl * GRID_W, GRID_W)
            for p in range(HEADS // 2):
                ls = slice(p * 128, (p + 1) * 128)
                qst = _stack_heads(q_ref[pl.ds(qoff, GRID_W), ls])
                dost = _stack_heads(do_ref[pl.ds(qoff, GRID_W), ls])
                k2v = kbuf[pl.ds(off, KH * GRID_W), ls]
                v2v = vbuf[pl.ds(off, KH * GRID_W), ls]
                kc2 = kc_ref[:, ls]
                vc2 = vc_ref[:, ls]
                s_w = _dot_nt(qst, k2v) * scale + tab_s[vi, p]
                s_c = _dot_nt(qst, kc2) * scale
                m = jnp.maximum(jnp.max(s_w, axis=-1, keepdims=True), jnp.max(s_c, axis=-1, keepdims=True))
                pw = jnp.exp(s_w - m)
                pc = jnp.exp(s_c - m)
                inv = 1.0 / (jnp.sum(pw, axis=-1, keepdims=True) + jnp.sum(pc, axis=-1, keepdims=True))
                pw = pw * inv
                pc = pc * inv
                dpw = _dot_nt(dost, v2v)
                dpc = _dot_nt(dost, vc2)
                delta = jnp.sum(pw * dpw, axis=-1, keepdims=True) + jnp.sum(pc * dpc, axis=-1, keepdims=True)
                ds_w = pw * (dpw - delta)
                ds_c = pc * (dpc - delta)
                dtab_s[vi, p] += ds_w
                dsw16 = ds_w.astype(BF16)
                dsc16 = ds_c.astype(BF16)
                dq = (_dot(dsw16, k2v) + _dot(dsc16, kc2)) * scale
                dq_ref[pl.ds(qoff, GRID_W), ls] = _unstack_heads(dq).astype(BF16)
                dkacc[pl.ds(off, KH * GRID_W), ls] += _dot_tn(dsw16, qst) * scale
                dvacc[pl.ds(off, KH * GRID_W), ls] += _dot_tn(pw.astype(BF16), dost)
                dkc_ref[:, ls] += _dot_tn(dsc16, qst) * scale
                dvc_ref[:, ls] += _dot_tn(pc.astype(BF16), dost)
            return carry

        lax.fori_loop(0, QROWS, row_body, 0)

        @pl.when(i == nb - 1)
        def _():
            for t in range(3):
                dst_row = (nb - 3 + t) * qt
                flush(dkacc, dk_hbm, t, dst_row)
                flush(dvacc, dv_hbm, t, dst_row)
            cp = pltpu.make_async_copy(dtab_s, dtab_hbm, sem)
            cp.start()
            cp.wait()
            if exchange:
                exchange.finish(x_in, x_out, x_sems)

    blk = lambda col: [pl.BlockSpec((qt, NA_W), functools.partial(lambda i, t, c: (wb0(i) + t, c), t=t, c=col))
                       for t in range(3)]
    any_spec = pl.BlockSpec(memory_space=pl.ANY)
    hbm = pl.BlockSpec(memory_space=pltpu.HBM)
    res = pl.pallas_call(
        body, name=name, grid=(nb,),
        in_specs=[pl.BlockSpec((qt, NA_W), lambda i: (i, 0))] + blk(1) + blk(2)
                 + [pl.BlockSpec((nctx, NA_W), lambda i: (0, 1)), pl.BlockSpec((nctx, NA_W), lambda i: (0, 2)),
                    pl.BlockSpec((qt, NA_W), lambda i: (i, 0)), any_spec] + [hbm] * nx,
        out_specs=[pl.BlockSpec((qt, NA_W), lambda i: (i, 0)), any_spec, any_spec,
                   pl.BlockSpec((nctx, NA_W), lambda i: (0, 0)), pl.BlockSpec((nctx, NA_W), lambda i: (0, 0)),
                   any_spec] + [hbm] * nx,
        out_shape=[jax.ShapeDtypeStruct((seq, NA_W), BF16), jax.ShapeDtypeStruct((seq, NA_W), BF16),
                   jax.ShapeDtypeStruct((seq, NA_W), BF16), jax.ShapeDtypeStruct((nctx, NA_W), F32),
                   jax.ShapeDtypeStruct((nctx, NA_W), F32), jax.ShapeDtypeStruct(TAB_SHAPE, F32)]
                  + (exchange.out_shapes if exchange else []),
        scratch_shapes=[pltpu.VMEM((wt, NA_W), BF16), pltpu.VMEM((wt, NA_W), BF16),
                        pltpu.VMEM((wt, NA_W), F32), pltpu.VMEM((wt, NA_W), F32),
                        pltpu.VMEM(TAB_SHAPE, F32), pltpu.VMEM(TAB_SHAPE, F32), pltpu.VMEM((qt, NA_W), BF16),
                        pltpu.SemaphoreType.DMA]
                       + (exchange.scratch if exchange else []),
        compiler_params=_params(("arbitrary",)),
    )(qkvu, qkvu, qkvu, qkvu, qkvu, qkvu, qkvu, qkvu_c, qkvu_c, dmix, tab, *(exchange.arrays if exchange else []))
    return res[:6], list(res[6:])


def _halo_specs(te, seq, col, width):
    per = te // HALO
    last = seq // HALO - 1
    return [pl.BlockSpec((HALO, width), lambda i: (jnp.maximum(i * per - 1, 0), col)),
            pl.BlockSpec((te, width), lambda i: (i, col)),
            pl.BlockSpec((HALO, width), lambda i: (jnp.minimum((i + 1) * per, last), col))]


def _extended(prev_ref, cur_ref, next_ref, i, te, seq):
    xe = jnp.concatenate([prev_ref[...], cur_ref[...], next_ref[...]], axis=0).astype(F32)
    pos = i * te - HALO + lax.broadcasted_iota(jnp.int32, (te + 2 * HALO, 1), 0)
    return jnp.where((pos >= 0) & (pos < seq), xe, 0.0), pos


def _window_sum(x, levels, n, mirrored):
    first = (n - 1) if mirrored else 1
    acc = x + pltpu.roll(x, first, 0)
    step = 1
    for _ in range(levels - 1):
        acc = pltpu.roll(acc, step, 0) + pltpu.roll(acc, n - step, 0)
        step *= 2
    return acc


def _window_count(pos, w, seq):
    lo = jnp.clip(pos - w // 2, 0, seq)
    hi = jnp.clip(pos - w // 2 + w, 0, seq)
    return jnp.maximum(hi - lo, 1).astype(F32)


def pool_fwd(qkvu, pool_w, pool_scale, name):
    seq = qkvu.shape[0]
    te = _tile(seq, (512, 256))
    n = te + 2 * HALO

    def body(up_ref, uc_ref, un_ref, w_ref, sc_ref, o_ref):
        i = pl.program_id(0)
        xe, pos = _extended(up_ref, uc_ref, un_ref, i, te, seq)
        cnt = pos[HALO:HALO + te]
        for g, w in enumerate(POOL_WINDOWS):
            ls = slice(g * POOL_G, (g + 1) * POOL_G)
            xg = xe[:, ls]
            win = _window_sum(xg, g + 1, n, False)[HALO:HALO + te]
            dlt = win / _window_count(cnt, w, seq) - xg[HALO:HALO + te]
            z = _dot(dlt.astype(BF16), w_ref[g])
            o_ref[:, ls] = (z * sc_ref[:, ls]).astype(BF16)

    return pl.pallas_call(
        body, name=name, grid=(seq // te,),
        in_specs=_halo_specs(te, seq, 3, POOL_W)
                 + [pl.BlockSpec((4, POOL_G, POOL_G), lambda i: (0, 0, 0)), pl.BlockSpec((1, POOL_W), lambda i: (0, 0))],
        out_specs=pl.BlockSpec((te, POOL_W), lambda i: (i, 0)),
        out_shape=jax.ShapeDtypeStruct((seq, POOL_W), BF16),
        compiler_params=_params(("parallel",)),
    )(qkvu, qkvu, qkvu, pool_w, pool_scale)


def pool_bwd(qkvu, dmix, pool_w, pool_scale, name):
    seq = qkvu.shape[0]
    te = _tile(seq, (512, 256))
    n = te + 2 * HALO

    def body(up_ref, uc_ref, un_ref, dp_ref, dc_ref, dn_ref, w_ref, sc_ref, du_ref, dw_ref, dsc_ref):
        i = pl.program_id(0)

        @pl.when(i == 0)
        def _():
            dw_ref[...] = jnp.zeros_like(dw_ref)
            dsc_ref[...] = jnp.zeros_like(dsc_ref)

        xe, pos = _extended(up_ref, uc_ref, un_ref, i, te, seq)
        de, _ = _extended(dp_ref, dc_ref, dn_ref, i, te, seq)
        cpos = pos[HALO:HALO + te]
        for g, w in enumerate(POOL_WINDOWS):
            ls = slice(g * POOL_G, (g + 1) * POOL_G)
            xg = xe[:, ls]
            wg = w_ref[g]
            win = _window_sum(xg, g + 1, n, False)[HALO:HALO + te]
            dlt = (win / _window_count(cpos, w, seq) - xg[HALO:HALO + te]).astype(BF16)
            z = _dot(dlt, wg)
            dpg = de[:, ls]
            dsc_ref[:, ls] += jnp.sum(dpg[HALO:HALO + te] * z, axis=0, keepdims=True)
            dz = (dpg * sc_ref[:, ls]).astype(BF16)
            dw_ref[g] += _dot_tn(dlt, dz[HALO:HALO + te])
            dd = _dot_nt(dz, wg)
            back = _window_sum(dd / _window_count(pos, w, seq), g + 1, n, True)
            du_ref[:, ls] = (back[HALO:HALO + te] - dd[HALO:HALO + te]).astype(BF16)

    return pl.pallas_call(
        body, name=name, grid=(seq // te,),
        in_specs=_halo_specs(te, seq, 3, POOL_W) + _halo_specs(te, seq, 1, POOL_W)
                 + [pl.BlockSpec((4, POOL_G, POOL_G), lambda i: (0, 0, 0)), pl.BlockSpec((1, POOL_W), lambda i: (0, 0))],
        out_specs=[pl.BlockSpec((te, POOL_W), lambda i: (i, 0)),
                   pl.BlockSpec((4, POOL_G, POOL_G), lambda i: (0, 0, 0)), pl.BlockSpec((1, POOL_W), lambda i: (0, 0))],
        out_shape=[jax.ShapeDtypeStruct((seq, POOL_W), BF16), jax.ShapeDtypeStruct((4, POOL_G, POOL_G), F32),
                   jax.ShapeDtypeStruct((1, POOL_W), F32)],
        compiler_params=_params(("arbitrary",)),
    )(qkvu, qkvu, qkvu, dmix, dmix, dmix, pool_w, pool_scale)


def _shifted(z, zprev_row, znext_row, te):
    rows = lax.broadcasted_iota(jnp.int32, (te, 1), 0)
    zp = jnp.where(rows == 0, zprev_row, pltpu.roll(z, 1, 0))
    zn = jnp.where(rows == te - 1, znext_row, pltpu.roll(z, te - 1, 0))
    return zp, zn


def _edge_rows(prev_ref, next_ref, i, nt):
    p = prev_ref[HALO - 1:HALO, :].astype(F32)
    q = next_ref[0:1, :].astype(F32)
    return jnp.where(i == 0, 0.0, p), jnp.where(i == nt - 1, 0.0, q)


def conv_fwd(proj, conv_w, name):
    seq = proj.shape[0]
    te = _tile(seq, (512, 256))
    nt = seq // te

    def body(bg_ref, cp_ref, cc_ref, cn_ref, xp_ref, xc_ref, xn_ref, w_ref, o_ref):
        i = pl.program_id(0)
        z = cc_ref[...].astype(F32) * xc_ref[...].astype(F32)
        cpr, cnr = _edge_rows(cp_ref, cn_ref, i, nt)
        xpr, xnr = _edge_rows(xp_ref, xn_ref, i, nt)
        zp, zn = _shifted(z, cpr * xpr, cnr * xnr, te)
        y = zp * w_ref[0:1, :] + z * w_ref[1:2, :] + zn * w_ref[2:3, :]
        o_ref[...] = (bg_ref[...].astype(F32) * y).astype(BF16)

    return pl.pallas_call(
        body, name=name, grid=(nt,),
        in_specs=[pl.BlockSpec((te, D), lambda i: (i, 0))] + _halo_specs(te, seq, 1, D) + _halo_specs(te, seq, 2, D)
                 + [pl.BlockSpec((3, D), lambda i: (0, 0))],
        out_specs=pl.BlockSpec((te, D), lambda i: (i, 0)),
        out_shape=jax.ShapeDtypeStruct((seq, D), BF16),
        compiler_params=_params(("parallel",)),
    )(proj, proj, proj, proj, proj, proj, proj, conv_w)


def conv_bwd(proj, dgm, conv_w, name):
    seq = proj.shape[0]
    te = _tile(seq, (512, 256))
    nt = seq // te

    def body(bp_ref, bc_ref, bn_ref, cp_ref, cc_ref, cn_ref, xp_ref, xc_ref, xn_ref, gp_ref, gc_ref, gn_ref, w_ref,
             dbg_ref, dcg_ref, dxin_ref, dw_ref):
        i = pl.program_id(0)

        @pl.when(i == 0)
        def _():
            dw_ref[...] = jnp.zeros_like(dw_ref)

        bg = bc_ref[...].astype(F32)
        cg = cc_ref[...].astype(F32)
        xin = xc_ref[...].astype(F32)
        dg = gc_ref[...].astype(F32)
        z = cg * xin
        cpr, cnr = _edge_rows(cp_ref, cn_ref, i, nt)
        xpr, xnr = _edge_rows(xp_ref, xn_ref, i, nt)
        zp, zn = _shifted(z, cpr * xpr, cnr * xnr, te)
        w0, w1, w2 = w_ref[0:1, :], w_ref[1:2, :], w_ref[2:3, :]
        y = zp * w0 + z * w1 + zn * w2
        dbg_ref[...] = (dg * y).astype(BF16)
        dy = dg * bg
        dw_ref[0:1, :] += jnp.sum(dy * zp, axis=0, keepdims=True)
        dw_ref[1:2, :] += jnp.sum(dy * z, axis=0, keepdims=True)
        dw_ref[2:3, :] += jnp.sum(dy * zn, axis=0, keepdims=True)
        bpr, bnr = _edge_rows(bp_ref, bn_ref, i, nt)
        gpr, gnr = _edge_rows(gp_ref, gn_ref, i, nt)
        dyp, dyn = _shifted(dy, bpr * gpr, bnr * gnr, te)
        dz = dyn * w0 + dy * w1 + dyp * w2
        dcg_ref[...] = (dz * xin).astype(BF16)
        dxin_ref[...] = (dz * cg).astype(BF16)

    row = pl.BlockSpec((te, D), lambda i: (i, 0))
    return pl.pallas_call(
        body, name=name, grid=(nt,),
        in_specs=_halo_specs(te, seq, 0, D) + _halo_specs(te, seq, 1, D) + _halo_specs(te, seq, 2, D)
                 + _halo_specs(te, seq, 0, D) + [pl.BlockSpec((3, D), lambda i: (0, 0))],
        out_specs=[row, row, row, pl.BlockSpec((3, D), lambda i: (0, 0))],
        out_shape=[jax.ShapeDtypeStruct((seq, D), BF16)] * 3 + [jax.ShapeDtypeStruct((3, D), F32)],
        compiler_params=_params(("arbitrary",)),
    )(proj, proj, proj, proj, proj, proj, proj, proj, proj, dgm, dgm, dgm, conv_w)


def _position():
    x, y, c = lax.axis_index("x"), lax.axis_index("y"), lax.axis_index("c")
    return x, y, c, 4 * x + 2 * y + c


def _peer(x, y, c, j):
    px = 1 - x if j & 4 else x
    py = 1 - y if j & 2 else y
    pc = 1 - c if j & 1 else c
    return (px, py, pc), 4 * px + 2 * py + pc


def small_allgather(v, name):
    rows, cols = v.shape

    def body(v_ref, o_ref, send_sems, recv_sems, local_sem):
        x, y, c, me = _position()
        mine = pltpu.make_async_copy(v_ref, o_ref.at[me], local_sem)
        mine.start()
        sends = []
        for j in range(1, N_DEV):
            peer, _ = _peer(x, y, c, j)
            cp = pltpu.make_async_remote_copy(src_ref=v_ref, dst_ref=o_ref.at[me], send_sem=send_sems.at[j - 1],
                                              recv_sem=recv_sems.at[j - 1], device_id=peer, device_id_type=MESH_ID)
            cp.start()
            sends.append(cp)
        for j in range(1, N_DEV):
            peer, pid = _peer(x, y, c, j)
            pltpu.make_async_remote_copy(src_ref=v_ref, dst_ref=o_ref.at[pid], send_sem=send_sems.at[j - 1],
                                         recv_sem=recv_sems.at[j - 1], device_id=peer,
                                         device_id_type=MESH_ID).wait_recv()
        for cp in sends:
            cp.wait_send()
        mine.wait()

    return pl.pallas_call(
        body, name=name,
        out_shape=jax.ShapeDtypeStruct((N_DEV, rows, cols), v.dtype),
        in_specs=[pl.BlockSpec(memory_space=pltpu.VMEM)],
        out_specs=pl.BlockSpec(memory_space=pltpu.VMEM),
        scratch_shapes=[pltpu.SemaphoreType.DMA((N_DEV - 1,)), pltpu.SemaphoreType.DMA((N_DEV - 1,)),
                        pltpu.SemaphoreType.DMA],
        compiler_params=pltpu.CompilerParams(vmem_limit_bytes=VMEM_LIMIT),
    )(v)


class Exchange:
    def __init__(self, kind, arrays):
        self.kind, self.arrays, self.n = kind, list(arrays), len(arrays)
        n = self.n
        if kind == "gather":
            self.out_shapes = [jax.ShapeDtypeStruct((N_DEV,) + a.shape, a.dtype) for a in self.arrays]
        else:
            self.out_shapes = [jax.ShapeDtypeStruct(a.shape, a.dtype) for a in self.arrays]
        self.scratch = [pltpu.SemaphoreType.DMA((7 * n,)), pltpu.SemaphoreType.DMA((7 * n,)),
                        pltpu.SemaphoreType.DMA((n,))]

    def _gather_copies(self, ins, outs, sems):
        send_sems, recv_sems, local_sems = sems
        x, y, c, me = _position()
        chips = [(1 - x, y), (x, 1 - y), (1 - x, 1 - y)]

        def blk(k, px, py, pc):
            return outs[k].at[4 * px + 2 * py + pc]

        def copy(k, slot, block, to, src=None):
            return pltpu.make_async_remote_copy(
                src_ref=blk(k, *block) if src is None else src, dst_ref=blk(k, *block),
                send_sem=send_sems.at[k * 7 + slot], recv_sem=recv_sems.at[k * 7 + slot],
                device_id=to, device_id_type=MESH_ID)

        mine = [pltpu.make_async_copy(ins[k], blk(k, x, y, c), local_sems.at[k]) for k in range(self.n)]
        first = []
        for k in range(self.n):
            first.append(copy(k, 0, (x, y, c), (x, y, 1 - c), src=ins[k]))
            first += [copy(k, 1 + j, (x, y, c), (*chip, c), src=ins[k]) for j, chip in enumerate(chips)]
        return (x, y, c), chips, copy, mine, first

    def start(self, ins, outs, sems):
        if self.kind == "gather":
            _, _, _, mine, first = self._gather_copies(ins, outs, sems)
            for cp in mine + first:
                cp.start()
        else:
            for cp in self._scatter_copies(ins, outs, sems, False):
                cp.start()

    def finish(self, ins, outs, sems):
        if self.kind == "gather":
            (x, y, c), chips, copy, mine, first = self._gather_copies(ins, outs, sems)
            passed = []
            for j, chip in enumerate(chips):
                for k in range(self.n):
                    copy(k, 1 + j, (*chip, c), (x, y, c)).wait_recv()
                    cp = copy(k, 4 + j, (*chip, c), (x, y, 1 - c))
                    cp.start()
                    passed.append(cp)
            for k in range(self.n):
                copy(k, 0, (x, y, 1 - c), (x, y, c)).wait_recv()
                for j, chip in enumerate(chips):
                    copy(k, 4 + j, (*chip, 1 - c), (x, y, c)).wait_recv()
            for cp in first + passed:
                cp.wait_send()
            for cp in mine:
                cp.wait()
        else:
            for cp in self._scatter_copies(ins, outs, sems, True):
                cp.wait_recv()
            copies = self._scatter_copies(ins, outs, sems, False)
            for cp in copies[self.n:]:
                cp.wait_send()
            for cp in copies[:self.n]:
                cp.wait()

    def _scatter_copies(self, ins, outs, sems, arrivals):
        send_sems, recv_sems, local_sems = sems
        x, y, c, me = _position()
        out = []
        if not arrivals:
            out = [pltpu.make_async_copy(ins[k].at[me], outs[k].at[me], local_sems.at[k]) for k in range(self.n)]
        for j in range(1, N_DEV):
            peer, pid = _peer(x, y, c, j)
            for k in range(self.n):
                out.append(pltpu.make_async_remote_copy(
                    src_ref=ins[k].at[pid], dst_ref=outs[k].at[pid if arrivals else me],
                    send_sem=send_sems.at[k * 7 + j - 1], recv_sem=recv_sems.at[k * 7 + j - 1],
                    device_id=peer, device_id_type=MESH_ID))
        return out


def sum_devices(v, name):
    _, rows, cols = v.shape

    def body(v_ref, o_ref):
        acc = v_ref[0]
        for p in range(1, N_DEV):
            acc = acc + v_ref[p]
        o_ref[...] = acc

    return pl.pallas_call(
        body, name=name, out_shape=jax.ShapeDtypeStruct((rows, cols), F32),
        compiler_params=pltpu.CompilerParams(vmem_limit_bytes=VMEM_LIMIT),
    )(v)


def _silu(x):
    return x * _sigmoid(x)


def adaln_fwd(cm, mod_w, mod_b_cols, name):
    cols = mod_w.shape[2]

    def body(c_ref, w_ref, b_ref, o_ref):
        o_ref[0] = jnp.dot(_silu(c_ref[...]), w_ref[0], precision=HI, preferred_element_type=F32) + b_ref[0]

    return pl.pallas_call(
        body, name=name, grid=(2,),
        in_specs=[pl.BlockSpec((16, D), lambda l: (0, 0)), pl.BlockSpec((1, D, cols), lambda l: (l, 0, 0)),
                  pl.BlockSpec((1, 1, cols), lambda l: (l, 0, 0))],
        out_specs=pl.BlockSpec((1, 16, cols), lambda l: (l, 0, 0)),
        out_shape=jax.ShapeDtypeStruct((2, 16, cols), F32),
        compiler_params=_params(("parallel",)),
    )(cm, mod_w, mod_b_cols)


def adaln_bwd(cm_t, mod_w, dm_t, name):
    cols = mod_w.shape[2]

    def body(c_ref, w_ref, lat_ref, ctx_ref, gw_ref, pc_ref):
        ctot = jnp.sum(ctx_ref[0], axis=0, keepdims=True)
        rows = lax.broadcasted_iota(jnp.int32, (8, 1), 0)
        g_hi = jnp.where(rows == 0, ctot, 0.0)
        g = jnp.concatenate([lat_ref[0], g_hi], axis=0)
        gw_ref[0] = jnp.dot(_silu(c_ref[...]), g, precision=HI, preferred_element_type=F32)
        pc_ref[0] = lax.dot_general(g_hi, w_ref[0], NT_DIMS, precision=HI, preferred_element_type=F32)

    return pl.pallas_call(
        body, name=name, grid=(2,),
        in_specs=[pl.BlockSpec((D, 16), lambda l: (0, 0)), pl.BlockSpec((1, D, cols), lambda l: (l, 0, 0)),
                  pl.BlockSpec((1, 8, cols), lambda l: (l, 0, 0)), pl.BlockSpec((1, 8, cols), lambda l: (l + 2, 0, 0))],
        out_specs=[pl.BlockSpec((1, D, cols), lambda l: (l, 0, 0)), pl.BlockSpec((1, 8, D), lambda l: (l, 0, 0))],
        out_shape=[jax.ShapeDtypeStruct((2, D, cols), F32), jax.ShapeDtypeStruct((2, 8, D), F32)],
        compiler_params=_params(("parallel",)),
    )(cm_t, mod_w, dm_t, dm_t)


def mod_b_grad(dm_t, name):
    width = dm_t.shape[2]
    tn = width // 8

    def body(d_ref, o_ref):
        s = jnp.concatenate([jnp.sum(d_ref[k], axis=0, keepdims=True) for k in range(4)]
                            + [jnp.zeros((4, tn), F32)], axis=0)
        o_ref[...] = s + pltpu.roll(s, 6, 0)

    return pl.pallas_call(
        body, name=name, grid=(8,),
        in_specs=[pl.BlockSpec((4, 8, tn), lambda j: (0, 0, j))],
        out_specs=pl.BlockSpec((8, tn), lambda j: (0, j)),
        out_shape=jax.ShapeDtypeStruct((8, width), F32),
        compiler_params=_params(("parallel",)),
    )(dm_t)


def adamw(w, m, v, name, g=None, recv=None):
    rows, cols = w.shape
    tr = _tile(rows, (256, 128, 64, 32, 16, 8))
    summed = recv is not None

    def body(w_ref, m_ref, v_ref, g_ref, go_ref, d_ref, mo_ref, vo_ref):
        if summed:
            gv = g_ref[0].astype(F32)
            for p in range(1, N_DEV):
                gv = gv + g_ref[p].astype(F32)
        else:
            gv = g_ref[...]
        mn = ADAM_B1 * m_ref[...] + (1.0 - ADAM_B1) * gv
        vn = ADAM_B2 * v_ref[...] + (1.0 - ADAM_B2) * (gv * gv)
        m_hat = mn / (1.0 - ADAM_B1 ** ADAM_STEP)
        v_hat = vn / (1.0 - ADAM_B2 ** ADAM_STEP)
        go_ref[...] = gv
        d_ref[...] = -ADAM_LR * (m_hat / (jnp.sqrt(v_hat) + ADAM_EPS) + ADAM_WD * w_ref[...])
        mo_ref[...] = mn
        vo_ref[...] = vn

    row = pl.BlockSpec((tr, cols), lambda i: (i, 0))
    gspec = pl.BlockSpec((N_DEV, tr, cols), lambda i: (0, i, 0)) if summed else row
    return pl.pallas_call(
        body, name=name, grid=(rows // tr,),
        in_specs=[row, row, row, gspec], out_specs=[row] * 4,
        out_shape=[jax.ShapeDtypeStruct((rows, cols), F32)] * 4,
        compiler_params=_params(("parallel",)),
    )(w, m, v, recv if summed else g)


def _ffn_fwd(h, hn, mods, w13, w2, base, tag, nxt=None, exchange=None):
    (p, u, s), exchanged = ffn_up(hn, w13, tag + "_up", exchange)
    if callable(w2):
        w2 = w2(exchanged)
    outs = mm_nn([s], w2, [0], tag + "_down", res=(h, mods, base + 2, 0.5), nxt=nxt)
    h_new, y = outs[0], outs[1]
    return h_new, (outs[2] if nxt else None), (h, hn, p, u, s, y), exchanged


COLUMN_CUT = ("w13", "ewi", "cwi")
GATHER_FIRST = ("w13_00",)
GATHER_IN_FFN = ("w2_00", "ewi", "ewo", "w13_01", "w2_01")
GATHER_IN_ATTN = ("w13_10", "w2_10", "cwi", "cwo", "w13_11", "w2_11")
SCATTER_IN_ATTN = ("w13_11", "w2_11", "cwi", "cwo", "w13_10", "w2_10", "w13_01", "w2_01", "ewo")
SCATTER_LAST = ("w13_00", "w2_00", "ewi")


def unpack_piece(p, g):
    if p.split("_")[0] in COLUMN_CUT:
        return g.transpose(1, 0, 2).reshape(g.shape[1], -1)
    return g.reshape(-1, g.shape[2])


def block_piece(p, full):
    if p.split("_")[0] in COLUMN_CUT:
        return full.reshape(full.shape[0], N_DEV, -1).transpose(1, 0, 2).astype(BF16)
    return full.reshape(N_DEV, -1, full.shape[1]).astype(BF16)


def _ffn_bwd(dy, dres, saved, mods, g, w13, w2, base, tag, prev=None, acc=None, exchange_of=None):
    h, hn, p, u, s, _ = saved
    ff = w2.shape[0]
    acc = acc or (None, None, None)
    da, db = mm_nt(dy, w2, tag + "_ds", dswiglu=(u, p))
    dw2 = mm_tn(s, dy, tag + "_dw2", acc=acc[2])
    dwa = mm_tn(hn, da, tag + "_dw13a", acc=acc[0])
    dwb = mm_tn(hn, db, tag + "_dw13b", acc=acc[1])
    exchange = exchange_of(dwa, dwb, dw2) if exchange_of else None
    outs = mm_nt_norm([da, db], w13, [0, ff], h, g, mods, base + 1, dres, tag + "_dhn", prev=prev, exchange=exchange)
    dh, dshift, dscale, dg = outs[:4]
    nprev = 2 if prev else 0
    return (dh, (dwa, dwb, dw2), dg, {base: dshift, base + 1: dscale}, tuple(outs[4:4 + nprev]),
            list(outs[4 + nprev:]))


def _mod_rows(parts):
    zero = jnp.zeros((1, D), F32)
    return jnp.concatenate([parts.get(k, zero) for k in range(N_MOD)], axis=0)


def local_step(x, ctx, ml, mc, wts, target, shards=None):
    wts = dict(wts)
    ng = wts["norm_g"]
    gvec = lambda l, k: ng[l, k][None, :]
    pool_w16 = wts["pool_w"].astype(BF16)
    grads = {}

    def gather(pieces):
        return Exchange("gather", [shards[p] for p in pieces]) if shards else None

    def arrived(pieces, results):
        for p, g in zip(pieces, results):
            wts[p] = unpack_piece(p, g)

    def ffn_grads(lf, f):
        grads["w13_" + lf] = jnp.concatenate([f[0], f[1]], axis=1)
        grads["w2_" + lf] = f[2]

    if shards:
        xh, got = normmod(x, gvec(0, 0), ml[0], 0, 1, "l0f1_norm", gather(GATHER_FIRST))
        arrived(GATHER_FIRST, got)
    else:
        xh = normmod(x, gvec(0, 0), ml[0], 0, 1, "l0f1_norm")
    ch = normmod(ctx, gvec(0, 0), mc[0], 0, 1, "l0f1c_norm")
    def w2_after_up(got):
        arrived(GATHER_IN_FFN, got)
        return wts["w2_00"]

    x1, xn, sv1, _ = _ffn_fwd(x, xh, ml[0], wts["w13_00"], w2_after_up, 0, "l0f1",
                              nxt=(gvec(0, 1), ml[0], 3, 4), exchange=gather(GATHER_IN_FFN))
    c1, cn, sv1c, _ = _ffn_fwd(ctx, ch, mc[0], wts["w13_00"], wts["w2_00"], 0, "l0f1c", nxt=(gvec(0, 1), mc[0], 3, 4))
    qkvu = mm_nn([xn], wts["ewi"], [0], "l0mix_in")
    qkvu_c = mm_nn([cn], wts["ewi"], [0], "l0mix_in_c")
    tab = bias_table(wts["rpb"])
    att, got = attn_fwd(qkvu, qkvu_c, tab, "l0_attn", gather(GATHER_IN_ATTN))
    arrived(GATHER_IN_ATTN, got)
    pool = pool_fwd(qkvu, pool_w16, wts["pool_scale"], "l0_pool")
    x2, ymix, xh = mm_nn([att, pool], wts["ewo"], [0, NA_W], "l0mix_out", res=(x1, ml[0], 5, 1.0),
                         nxt=(gvec(0, 2), ml[0], 6, 7))
    x3, xh, sv2, _ = _ffn_fwd(x2, xh, ml[0], wts["w13_01"], wts["w2_01"], 6, "l0f2", nxt=(gvec(1, 0), ml[1], 0, 1))

    x4, xn1, sv3, _ = _ffn_fwd(x3, xh, ml[1], wts["w13_10"], wts["w2_10"], 0, "l1f1", nxt=(gvec(1, 1), ml[1], 3, 4))
    proj = mm_nn([xn1], wts["cwi"], [0], "l1mix_in")
    gm = conv_fwd(proj, wts["conv_w"], "l1_conv")
    x5, ycv, xh = mm_nn([gm], wts["cwo"], [0], "l1mix_out", res=(x4, ml[1], 5, 1.0), nxt=(gvec(1, 2), ml[1], 6, 7))
    x6, _, sv4, _ = _ffn_fwd(x5, xh, ml[1], wts["w13_11"], wts["w2_11"], 6, "l1f2")

    dx6, loss, dgf, dy, dgate = loss_head(x6, wts["final_g"][None, :], target, (sv4[5], ml[1], 8, 0.5), "loss_head")
    dm1 = {8: dgate}
    dx5, dwf4, dg12, dm_f4, (dy, dgate), _ = _ffn_bwd(dy, dx6, sv4, ml[1], gvec(1, 2), wts["w13_11"], wts["w2_11"], 6,
                                                      "l1f2", prev=(ycv, ml[1], 5, 1.0))
    ffn_grads("11", dwf4)
    dm1.update({5: dgate, **dm_f4})
    dgm = mm_nt(dy, wts["cwo"], "l1mix_dgm")
    grads["cwo"] = mm_tn(gm, dy, "l1mix_dwo")
    dbg, dcg, dxin, dconv_w = conv_bwd(proj, dgm, wts["conv_w"], "l1_conv_bwd")
    grads["cwi"] = mm_tn_multi(xn1, [dbg, dcg, dxin], "l1mix_dwi")
    dx4, dsh, dsc, dg11, dy, dgate = mm_nt_norm([dbg, dcg, dxin], wts["cwi"], [0, D, 2 * D], x4, gvec(1, 1), ml[1], 4,
                                                dx5, "l1mix_dxn", prev=(sv3[5], ml[1], 2, 0.5))
    dm1.update({3: dsh, 4: dsc, 2: dgate})
    dx3, dwf3, dg10, dm_f3, (dy, dgate), _ = _ffn_bwd(dy, dx4, sv3, ml[1], gvec(1, 0), wts["w13_10"], wts["w2_10"], 0,
                                                      "l1f1", prev=(sv2[5], ml[0], 8, 0.5))
    ffn_grads("10", dwf3)
    dm1.update(dm_f3)
    dm0 = {8: dgate}

    dx2, dwf2, dg02, dm_f2, (dy, dgate), _ = _ffn_bwd(dy, dx3, sv2, ml[0], gvec(0, 2), wts["w13_01"], wts["w2_01"], 6,
                                                      "l0f2", prev=(ymix, ml[0], 5, 1.0))
    ffn_grads("01", dwf2)
    dm0.update({5: dgate, **dm_f2})
    dmix = mm_nt(dy, wts["ewo"], "l0mix_dmix")
    grads["ewo"] = jnp.concatenate([mm_tn(att, dy, "l0mix_dwo_att"), mm_tn(pool, dy, "l0mix_dwo_pool")], axis=0)
    scatter = Exchange("scatter", [block_piece(p, grads.pop(p)) for p in SCATTER_IN_ATTN]) if shards else None
    (dq, dk, dv, dkc, dvc, dtab), got = attn_bwd(qkvu, qkvu_c, tab, dmix, "l0_attn_bwd", scatter)
    recv = dict(zip(SCATTER_IN_ATTN, got))
    du, dpool_w, dpool_scale = pool_bwd(qkvu, dmix, pool_w16, wts["pool_scale"], "l0_pool_bwd")
    drpb = bias_table_bwd(dtab)
    dk16, dv16, dkc16, dvc16 = dk, dv, dkc.astype(BF16), dvc.astype(BF16)
    no_grad = jnp.zeros_like(dkc16)
    dewi_c = mm_tn_multi(cn, [no_grad, dkc16, dvc16, no_grad], "l0mix_dwi_c")
    grads["ewi"] = mm_tn_multi(xn, [dq, dk16, dv16, du], "l0mix_dwi", acc=dewi_c)
    dx1, dsh, dsc, dg01, dy, dgate = mm_nt_norm([dq, dk16, dv16, du], wts["ewi"], [0, NA_W, 2 * NA_W, 3 * NA_W], x1,
                                                gvec(0, 1), ml[0], 4, dx2, "l0mix_dxn", prev=(sv1[5], ml[0], 2, 0.5))
    dm0.update({3: dsh, 4: dsc, 2: dgate})
    dc1, dsh_c, dsc_c, dg01c, dy_c, dgate_c = mm_nt_norm([dkc16, dvc16], wts["ewi"], [NA_W, 2 * NA_W], c1, gvec(0, 1),
                                                         mc[0], 4, None, "l0mix_dxn_c", prev=(sv1c[5], mc[0], 2, 0.5))
    _, dwf1c, dg00c, dm_f1c, _, _ = _ffn_bwd(dy_c, dc1, sv1c, mc[0], gvec(0, 0), wts["w13_00"], wts["w2_00"], 0, "l0f1c")
    dmc0 = {3: dsh_c, 4: dsc_c, 2: dgate_c, **dm_f1c}

    def last_scatter(dwa, dwb, dw2):
        ffn_grads("00", (dwa, dwb, dw2))
        return Exchange("scatter", [block_piece(p, grads.pop(p)) for p in SCATTER_LAST]) if shards else None

    dx0, _, dg00, dm_f1, _, got = _ffn_bwd(dy, dx1, sv1, ml[0], gvec(0, 0), wts["w13_00"], wts["w2_00"], 0, "l0f1",
                                           acc=dwf1c, exchange_of=last_scatter)
    recv.update(zip(SCATTER_LAST, got))
    dm0.update(dm_f1)

    return {
        "loss": loss, "grad_x": dx0,
        "dml": jnp.stack([_mod_rows(dm0), _mod_rows(dm1)]),
        "dmc": jnp.stack([_mod_rows(dmc0), jnp.zeros((N_MOD, D), F32)]),
        "norm_g": jnp.concatenate([dg00 + dg00c, dg01 + dg01c, dg02, dg10, dg11, dg12], axis=0),
        "grads": grads, "recv": recv,
        "rpb": drpb, "pool_w": dpool_w, "pool_scale": dpool_scale, "conv_w": dconv_w, "final_g": dgf,
    }


def _rows_of(v, nrows):
    flat = v.reshape(-1)
    return jnp.pad(flat, (0, nrows * D - flat.shape[0])).reshape(nrows, D)


def kernel(x, c, ctx, c_ctx, mod_w, mod_b, norm_g, ffn_w13, ffn_w2, even_w_in, even_w_out, na_rpb, pool_w, pool_scale, conv_w_in, conv_w, conv_w_out, final_g, loss_target, m_c_ctx, m_mod_w, m_mod_b, m_norm_g, m_ffn_w13, m_ffn_w2, m_even_w_in, m_even_w_out, m_na_rpb, m_pool_w, m_pool_scale, m_conv_w_in, m_conv_w, m_conv_w_out, m_final_g, v_c_ctx, v_mod_w, v_mod_b, v_norm_g, v_ffn_w13, v_ffn_w2, v_even_w_in, v_even_w_out, v_na_rpb, v_pool_w, v_pool_scale, v_conv_w_in, v_conv_w, v_conv_w_out, v_final_g):
    me = 4 * lax.axis_index("x") + 2 * lax.axis_index("y") + lax.axis_index("c")
    ff = ffn_w2.shape[2] * N_DEV
    w13c = ffn_w13.shape[3]
    w2r = ffn_w2.shape[2]
    mcols = mod_w.shape[2]
    gcols = norm_g.shape[2]

    big = {"w13": ffn_w13.reshape(4 * D, w13c), "w2": ffn_w2.reshape(4 * w2r, D), "ewi": even_w_in[0],
           "ewo": even_w_out[0], "cwi": conv_w_in[0], "cwo": conv_w_out[0]}
    names = list(big)
    shards = {"ewi": even_w_in[0].astype(BF16), "ewo": even_w_out[0].astype(BF16),
              "cwi": conv_w_in[0].astype(BF16), "cwo": conv_w_out[0].astype(BF16)}
    for l in range(2):
        for f in range(2):
            shards["w13_%d%d" % (l, f)] = ffn_w13[l, f].astype(BF16)
            shards["w2_%d%d" % (l, f)] = ffn_w2[l, f].astype(BF16)
    wts = {}

    c_all = small_allgather(jnp.pad(c, ((0, 7), (0, 0))), "cond_allgather")[:, 0, :]
    cm = jnp.concatenate([c_all, c_ctx[None, :], jnp.zeros((7, D), F32)], axis=0)
    mod_b_cols = lax.dynamic_slice(mod_b, (0, me * mcols), (2, mcols))[:, None, :]
    m_cols = adaln_fwd(cm, mod_w, mod_b_cols, "adaln_fwd")
    m_all = small_allgather(m_cols.reshape(32, mcols), "mod_allgather")
    m_full = m_all.reshape(N_DEV, 2, 16, mcols).transpose(1, 2, 0, 3).reshape(2, 16, N_MOD * D)
    ml = lax.dynamic_slice(m_full, (0, me, 0), (2, 1, N_MOD * D)).reshape(2, N_MOD, D)
    mc = m_full[:, 8].reshape(2, N_MOD, D)

    full_norm_g = small_allgather(_rows_of(norm_g, 8), "norm_g_allgather")[:, 0, :2 * 3 * gcols]
    full_norm_g = full_norm_g.reshape(N_DEV, 2, 3, gcols).transpose(1, 2, 0, 3).reshape(2, 3, D)
    full_conv_w = small_allgather(_rows_of(conv_w, 8), "conv_w_allgather")[:, 0, :3 * gcols]
    full_conv_w = full_conv_w.reshape(N_DEV, 3, gcols).transpose(1, 0, 2).reshape(3, D)
    wts.update(norm_g=full_norm_g, conv_w=full_conv_w, rpb=na_rpb[0], pool_w=pool_w[0], pool_scale=pool_scale,
               final_g=final_g)
    out = local_step(x[0], ctx[0], ml, mc, wts, loss_target[0], shards)

    dm_pack = jnp.concatenate([out["dml"].reshape(2, N_MOD * D), out["dmc"].reshape(2, N_MOD * D),
                               jnp.zeros((4, N_MOD * D), F32)], axis=0)
    dm_t = small_allgather(dm_pack, "dmod_allgather").transpose(1, 0, 2)[:4]
    dm_cols = lax.dynamic_slice(dm_t, (0, 0, me * mcols), (4, N_DEV, mcols))
    g_mod_w, pc = adaln_bwd(cm.T, mod_w, dm_cols, "adaln_bwd")
    g_mod_b = mod_b_grad(dm_t, "mod_b_grad")[:2]

    pack = jnp.concatenate([_rows_of(t, 8) for t in (
        out["norm_g"], out["conv_w"], out["final_g"], pc[0, :1] + pc[1, :1], out["pool_scale"], out["loss"],
        out["rpb"])] + [_rows_of(out["pool_w"], 64)], axis=0)
    small = sum_devices(small_allgather(pack, "small_grads_allgather"), "small_grads_sum")
    g_norm_g = lax.dynamic_slice(small[0:6].reshape(2, 3, D), (0, 0, me * gcols), (2, 3, gcols))
    g_conv_w = lax.dynamic_slice(small[8:11], (0, me * gcols), (3, gcols))[None]
    g_final_g = small[16]
    sg = _sigmoid(c_ctx)
    g_c_ctx = small[24] * (sg * (1.0 + c_ctx * (1.0 - sg)))
    g_pool_scale = small[32:33, :POOL_W]
    loss = small[40, 0]
    g_rpb = small[48:52].reshape(-1)[:na_rpb.size].reshape(na_rpb.shape)
    g_pool_w = small[56:120].reshape(pool_w.shape)

    pieces = out["recv"]
    lf = ("00", "01", "10", "11")
    recv = {"w13": jnp.concatenate([pieces["w13_" + t] for t in lf], axis=1),
            "w2": jnp.concatenate([pieces["w2_" + t] for t in lf], axis=1),
            "ewi": pieces["ewi"], "ewo": pieces["ewo"], "cwi": pieces["cwi"], "cwo": pieces["cwo"]}

    moments = {"w13": (m_ffn_w13, v_ffn_w13), "w2": (m_ffn_w2, v_ffn_w2), "ewi": (m_even_w_in, v_even_w_in),
               "ewo": (m_even_w_out, v_even_w_out), "cwi": (m_conv_w_in, v_conv_w_in),
               "cwo": (m_conv_w_out, v_conv_w_out)}
    orig = {"w13": ffn_w13, "w2": ffn_w2, "ewi": even_w_in, "ewo": even_w_out, "cwi": conv_w_in, "cwo": conv_w_out}
    upd = {}
    for k in names:
        shp2 = big[k].shape
        res = adamw(big[k], moments[k][0].reshape(shp2), moments[k][1].reshape(shp2), "adamw_" + k, recv=recv[k])
        upd[k] = [r.reshape(orig[k].shape) for r in res]
    shp2 = (2 * D, mcols)
    upd["mod_w"] = [r.reshape(mod_w.shape) for r in adamw(mod_w.reshape(shp2), m_mod_w.reshape(shp2),
                                                          v_mod_w.reshape(shp2), "adamw_mod_w",
                                                          g=g_mod_w.reshape(shp2))]

    smalls = [("c_ctx", c_ctx, m_c_ctx, v_c_ctx, g_c_ctx, 8), ("mod_b", mod_b, m_mod_b, v_mod_b, g_mod_b, 24),
              ("norm_g", norm_g, m_norm_g, v_norm_g, g_norm_g, 8), ("rpb", na_rpb, m_na_rpb, v_na_rpb, g_rpb, 8),
              ("pool_w", pool_w, m_pool_w, v_pool_w, g_pool_w, 64),
              ("pool_scale", pool_scale, m_pool_scale, v_pool_scale, g_pool_scale, 8),
              ("conv_w", conv_w, m_conv_w, v_conv_w, g_conv_w, 8), ("final_g", final_g, m_final_g, v_final_g, g_final_g, 8)]
    packed = [jnp.concatenate([_rows_of(s[col], s[5]) for s in smalls], axis=0) for col in (1, 2, 3, 4)]
    res = adamw(packed[0], packed[1], packed[2], "adamw_small", g=packed[3])
    row = 0
    for name, w, _, _, _, nrows in smalls:
        upd[name] = [r[row:row + nrows].reshape(-1)[:w.size].reshape(w.shape) for r in res]
        row += nrows

    order = ["c_ctx", "mod_w", "mod_b", "norm_g", "w13", "w2", "ewi", "ewo", "rpb", "pool_w", "pool_scale", "cwi",
             "conv_w", "cwo", "final_g"]
    grad_x = out["grad_x"][None]
    return (loss, grad_x, *[upd[k][0] for k in order], *[upd[k][1] for k in order], *[upd[k][2] for k in order],
            *[upd[k][3] for k in order])
```

```python
import functools

import numpy as np
import jax
import jax.numpy as jnp
from jax import lax
from jax.experimental import pallas as pl
from jax.experimental.pallas import tpu as pltpu

D = 1024
FF = 2816
SEQ = 16384
CTX = 256
GRID_W = 64
N_MOD = 9
HEADS = 8
HEAD_DIM = 64
NA_W = 512
POOL_W = 512
POOL_G = 128
POOL_WINDOWS = (2, 4, 8, 16)
KH = 8
KW = 16
RMS_EPS = 1e-6
NEG_INF = -1e30
N_DEV = 8

ADAM_LR = 0.001
ADAM_B1 = 0.9
ADAM_B2 = 0.999
ADAM_EPS = 1e-08
ADAM_WD = 0.01
ADAM_STEP = 10

VMEM_LIMIT = 52 * 1024 * 1024
HALO = 16
QROWS = 8
WROWS = 24

BF16 = jnp.bfloat16
F32 = jnp.float32
MESH_ID = pl.DeviceIdType.MESH
HI = lax.Precision.HIGHEST

NT_DIMS = (((1,), (1,)), ((), ()))
TN_DIMS = (((0,), (0,)), ((), ()))


def _tile(n, cands):
    for c in cands:
        if n % c == 0:
            return c
    return n


def _params(sem):
    return pltpu.CompilerParams(dimension_semantics=sem, vmem_limit_bytes=VMEM_LIMIT)


def _dot(a, b):
    return jnp.dot(a, b, preferred_element_type=F32)


def _dot_nt(a, b):
    return lax.dot_general(a, b, NT_DIMS, preferred_element_type=F32)


def _dot_tn(a, b):
    return lax.dot_general(a, b, TN_DIMS, preferred_element_type=F32)


def _sigmoid(x):
    return 1.0 / (1.0 + jnp.exp(-x))


def normmod(h, g, mods, i_shift, i_scale, name, exchange=None):
    n = h.shape[0]
    te = _tile(n, (512, 256))
    nt = n // te
    nx = exchange.n if exchange else 0

    def body(*refs):
        h_ref, g_ref, m_ref = refs[:3]
        x_in = refs[3:3 + nx]
        o_ref = refs[3 + nx]
        x_out = refs[4 + nx:4 + 2 * nx]
        x_sems = refs[4 + 2 * nx:]
        if exchange:
            @pl.when(pl.program_id(0) == 0)
            def _():
                exchange.start(x_in, x_out, x_sems)

        x = h_ref[...]
        r = lax.rsqrt(jnp.mean(x * x, axis=-1, keepdims=True) + RMS_EPS)
        y = x * r * g_ref[...]
        o_ref[...] = (y * (1.0 + m_ref[i_scale:i_scale + 1, :]) + m_ref[i_shift:i_shift + 1, :]).astype(BF16)

        if exchange:
            @pl.when(pl.program_id(0) == nt - 1)
            def _():
                exchange.finish(x_in, x_out, x_sems)

    hbm = pl.BlockSpec(memory_space=pltpu.HBM)
    res = pl.pallas_call(
        body, name=name, grid=(nt,),
        in_specs=[pl.BlockSpec((te, D), lambda i: (i, 0)),
                  pl.BlockSpec((1, D), lambda i: (0, 0)),
                  pl.BlockSpec((N_MOD, D), lambda i: (0, 0))] + [hbm] * nx,
        out_specs=[pl.BlockSpec((te, D), lambda i: (i, 0))] + [hbm] * nx,
        out_shape=[jax.ShapeDtypeStruct((n, D), BF16)] + (exchange.out_shapes if exchange else []),
        scratch_shapes=exchange.scratch if exchange else [],
        compiler_params=_params(("arbitrary",) if exchange else ("parallel",)),
    )(h, g, mods, *(exchange.arrays if exchange else []))
    return (res[0], list(res[1:])) if exchange else res[0]


def loss_head(x, g, target, prev, name):
    n = x.shape[0]
    te = _tile(n, (256,))
    i_gate, coef = prev[2], prev[3]

    def body(x_ref, g_ref, t_ref, y_ref, m_ref, dx_ref, loss_ref, dg_ref, dy_ref, dgate_ref):
        @pl.when(pl.program_id(0) == 0)
        def _():
            loss_ref[...] = jnp.zeros_like(loss_ref)
            dg_ref[...] = jnp.zeros_like(dg_ref)
            dgate_ref[...] = jnp.zeros_like(dgate_ref)

        xv = x_ref[...]
        gv = g_ref[...]
        r = lax.rsqrt(jnp.mean(xv * xv, axis=-1, keepdims=True) + RMS_EPS)
        xhat = xv * r
        e = xhat * gv - t_ref[...]
        per_tok = jnp.mean(e * e, axis=-1, keepdims=True)
        loss_ref[...] += 0.5 * jnp.sum(per_tok, axis=0, keepdims=True)
        dy = e * (1.0 / D)
        dg_ref[...] += jnp.sum(dy * xhat, axis=0, keepdims=True)
        dxhat = dy * gv
        dx = r * (dxhat - xhat * jnp.mean(dxhat * xhat, axis=-1, keepdims=True))
        dx_ref[...] = dx
        dy_ref[...] = (dx * (coef * m_ref[i_gate:i_gate + 1, :])).astype(BF16)
        dgate_ref[...] += coef * jnp.sum(dx * y_ref[...].astype(F32), axis=0, keepdims=True)

    row = pl.BlockSpec((te, D), lambda i: (i, 0))
    vec = pl.BlockSpec((1, D), lambda i: (0, 0))
    return pl.pallas_call(
        body, name=name, grid=(n // te,),
        in_specs=[row, vec, row, row, pl.BlockSpec((N_MOD, D), lambda i: (0, 0))],
        out_specs=[row, pl.BlockSpec((1, 128), lambda i: (0, 0)), vec, row, vec],
        out_shape=[jax.ShapeDtypeStruct((n, D), F32), jax.ShapeDtypeStruct((1, 128), F32),
                   jax.ShapeDtypeStruct((1, D), F32), jax.ShapeDtypeStruct((n, D), BF16),
                   jax.ShapeDtypeStruct((1, D), F32)],
        compiler_params=_params(("arbitrary",)),
    )(x, g, target, prev[0], prev[1])


def ffn_up(hn, w13, name, exchange=None):
    n = hn.shape[0]
    ff = w13.shape[1] // 2
    tm = _tile(n, (512, 256))
    tn = _tile(ff, (1408, 512, 256, 128))
    nj = ff // tn
    ni = n // tm
    nx = exchange.n if exchange else 0

    def body(*refs):
        h_ref, wa_ref, wb_ref = refs[:3]
        x_in = refs[3:3 + nx]
        p_ref, u_ref, s_ref = refs[3 + nx:6 + nx]
        x_out = refs[6 + nx:6 + 2 * nx]
        x_sems = refs[6 + 2 * nx:]
        if exchange:
            @pl.when((pl.program_id(0) == 0) & (pl.program_id(1) == 0))
            def _():
                exchange.start(x_in, x_out, x_sems)

        hv = h_ref[...]
        a = _dot(hv, wa_ref[...])
        b = _dot(hv, wb_ref[...])
        sig = _sigmoid(a)
        p = a * sig
        p_ref[...] = p.astype(BF16)
        u_ref[...] = (b * (sig * (1.0 + a * (1.0 - sig)))).astype(BF16)
        s_ref[...] = (p * b).astype(BF16)

        if exchange:
            @pl.when((pl.program_id(0) == nj - 1) & (pl.program_id(1) == ni - 1))
            def _():
                exchange.finish(x_in, x_out, x_sems)

    out = pl.BlockSpec((tm, tn), lambda j, i: (i, j))
    hbm = pl.BlockSpec(memory_space=pltpu.HBM)
    sem = ("arbitrary", "arbitrary") if exchange else ("parallel", "parallel")
    res = pl.pallas_call(
        body, name=name, grid=(nj, ni),
        in_specs=[pl.BlockSpec((tm, D), lambda j, i: (i, 0)),
                  pl.BlockSpec((D, tn), lambda j, i: (0, j)),
                  pl.BlockSpec((D, tn), lambda j, i: (0, j + nj))] + [hbm] * nx,
        out_specs=[out, out, out] + [hbm] * nx,
        out_shape=[jax.ShapeDtypeStruct((n, ff), BF16)] * 3 + (exchange.out_shapes if exchange else []),
        scratch_shapes=exchange.scratch if exchange else [],
        compiler_params=_params(sem),
    )(hn, w13, w13, *(exchange.arrays if exchange else []))
    return res[:3], list(res[3:])


def mm_nn(a_list, w, row_offs, name, out_dtype=BF16, res=None, nxt=None):
    n = a_list[0].shape[0]
    nout = w.shape[1]
    ks = [a.shape[1] for a in a_list]
    tm = _tile(n, (512, 256) if res is not None else (1024, 512, 256))
    tn = _tile(nout, (1024, 512, 256, 128))
    na = len(a_list)
    assert nxt is None or (res is not None and tn == D)

    def body(*refs):
        a_refs = refs[:na]
        w_refs = refs[na:2 * na]
        acc = _dot(a_refs[0][...], w_refs[0][...])
        for k in range(1, na):
            acc += _dot(a_refs[k][...], w_refs[k][...])
        if res is None:
            refs[2 * na][...] = acc.astype(out_dtype)
        else:
            h_ref, m_ref = refs[2 * na:2 * na + 2]
            i_gate, coef = res[2], res[3]
            h_new = h_ref[...] + (coef * m_ref[i_gate:i_gate + 1, :]) * acc
            if nxt is None:
                hn_ref, y_ref = refs[2 * na + 2:]
            else:
                g2_ref, m2_ref, hn_ref, y_ref, nx_ref = refs[2 * na + 2:]
                r = lax.rsqrt(jnp.mean(h_new * h_new, axis=-1, keepdims=True) + RMS_EPS)
                nx_ref[...] = ((h_new * r * g2_ref[...]) * (1.0 + m2_ref[nxt[3]:nxt[3] + 1, :])
                               + m2_ref[nxt[2]:nxt[2] + 1, :]).astype(BF16)
            hn_ref[...] = h_new
            y_ref[...] = acc.astype(BF16)

    in_specs = [pl.BlockSpec((tm, k), lambda j, i: (i, 0)) for k in ks]
    for k, off in zip(ks, row_offs):
        in_specs.append(pl.BlockSpec((k, tn), functools.partial(lambda j, i, ob: (ob, j), ob=off // k)))
    args = list(a_list) + [w] * na
    out = pl.BlockSpec((tm, tn), lambda j, i: (i, j))
    if res is None:
        out_specs = out
        out_shape = jax.ShapeDtypeStruct((n, nout), out_dtype)
    else:
        in_specs += [out, pl.BlockSpec((N_MOD, tn), lambda j, i: (0, j))]
        args += [res[0], res[1]]
        out_specs = [out, out]
        out_shape = [jax.ShapeDtypeStruct((n, nout), F32), jax.ShapeDtypeStruct((n, nout), BF16)]
        if nxt is not None:
            in_specs += [pl.BlockSpec((1, D), lambda j, i: (0, 0)), pl.BlockSpec((N_MOD, D), lambda j, i: (0, 0))]
            args += [nxt[0], nxt[1]]
            out_specs.append(out)
            out_shape.append(jax.ShapeDtypeStruct((n, nout), BF16))
    return pl.pallas_call(
        body, name=name, grid=(nout // tn, n // tm),
        in_specs=in_specs, out_specs=out_specs, out_shape=out_shape,
        compiler_params=_params(("parallel", "parallel")),
    )(*args)


def mm_nt(g, w, name, dswiglu=None):
    n, kg = g.shape
    nout = w.shape[0]
    tm = _tile(n, (512, 256) if dswiglu is not None else (1024, 512, 256))
    tn = _tile(nout, (1408, 1024, 512, 256, 128))

    def body(*refs):
        r = _dot_nt(refs[0][...], refs[1][...])
        if dswiglu is None:
            refs[2][...] = r.astype(BF16)
        else:
            u_ref, p_ref, da_ref, db_ref = refs[2:]
            da_ref[...] = (r * u_ref[...].astype(F32)).astype(BF16)
            db_ref[...] = (r * p_ref[...].astype(F32)).astype(BF16)

    out = pl.BlockSpec((tm, tn), lambda j, i: (i, j))
    in_specs = [pl.BlockSpec((tm, kg), lambda j, i: (i, 0)), pl.BlockSpec((tn, kg), lambda j, i: (j, 0))]
    args = [g, w]
    if dswiglu is None:
        out_specs = out
        out_shape = jax.ShapeDtypeStruct((n, nout), BF16)
    else:
        in_specs += [out, out]
        args += list(dswiglu)
        out_specs = [out, out]
        out_shape = [jax.ShapeDtypeStruct((n, nout), BF16)] * 2
    return pl.pallas_call(
        body, name=name, grid=(nout // tn, n // tm),
        in_specs=in_specs, out_specs=out_specs, out_shape=out_shape,
        compiler_params=_params(("parallel", "parallel")),
    )(*args)


def mm_nt_norm(g_list, w, col_offs, h, g, mods, i_scale, dres, name, prev=None, exchange=None):
    n = h.shape[0]
    kg = g_list[0].shape[1]
    tm = _tile(n, (512, 256))
    tk = _tile(kg, (1408, 1024, 512, 256, 128))
    ng = len(g_list)
    nk = kg // tk
    ni = n // tm
    has_res = dres is not None
    nx = exchange.n if exchange else 0

    def body(*refs):
        g_refs = refs[:ng]
        w_refs = refs[ng:2 * ng]
        pos = 2 * ng
        h_ref, gv_ref, m_ref = refs[pos:pos + 3]
        pos += 3
        if has_res:
            dres_ref = refs[pos]
            pos += 1
        if prev is not None:
            y_ref, mp_ref = refs[pos:pos + 2]
            pos += 2
        x_in = refs[pos:pos + nx]
        pos += nx
        dh_ref, dshift_ref, dscale_ref, dg_ref = refs[pos:pos + 4]
        pos += 4
        if prev is not None:
            dy_ref, dgate_ref = refs[pos:pos + 2]
            pos += 2
        x_out = refs[pos:pos + nx]
        pos += nx
        acc_ref = refs[pos]
        x_sems = refs[pos + 1:]
        i = pl.program_id(0)
        k = pl.program_id(1)

        @pl.when((i == 0) & (k == 0))
        def _():
            if exchange:
                exchange.start(x_in, x_out, x_sems)
            dshift_ref[...] = jnp.zeros_like(dshift_ref)
            dscale_ref[...] = jnp.zeros_like(dscale_ref)
            dg_ref[...] = jnp.zeros_like(dg_ref)
            if prev is not None:
                dgate_ref[...] = jnp.zeros_like(dgate_ref)

        def dots():
            acc = _dot_nt(g_refs[0][...], w_refs[0][...])
            for q in range(1, ng):
                acc += _dot_nt(g_refs[q][...], w_refs[q][...])
            return acc

        if nk > 1:
            @pl.when(k == 0)
            def _():
                acc_ref[...] = dots()

        if nk > 2:
            @pl.when((k > 0) & (k < nk - 1))
            def _():
                acc_ref[...] += dots()

        @pl.when(k == nk - 1)
        def _():
            d = dots() + acc_ref[...] if nk > 1 else dots()
            x = h_ref[...]
            gv = gv_ref[...]
            r = lax.rsqrt(jnp.mean(x * x, axis=-1, keepdims=True) + RMS_EPS)
            xhat = x * r
            one_scale = 1.0 + m_ref[i_scale:i_scale + 1, :]
            t = d * xhat
            tsum = jnp.sum(t, axis=0, keepdims=True)
            dshift_ref[...] += jnp.sum(d, axis=0, keepdims=True)
            dscale_ref[...] += gv * tsum
            dg_ref[...] += one_scale * tsum
            cvec = one_scale * gv
            dh = r * (d * cvec - xhat * jnp.mean(t * cvec, axis=-1, keepdims=True))
            if has_res:
                dh = dh + dres_ref[...]
            dh_ref[...] = dh
            if prev is not None:
                i_gate, coef = prev[2], prev[3]
                dy_ref[...] = (dh * (coef * mp_ref[i_gate:i_gate + 1, :])).astype(BF16)
                dgate_ref[...] += coef * jnp.sum(dh * y_ref[...].astype(F32), axis=0, keepdims=True)

        if exchange:
            @pl.when((i == ni - 1) & (k == nk - 1))
            def _():
                exchange.finish(x_in, x_out, x_sems)

    row = pl.BlockSpec((tm, D), lambda i, k: (i, 0))
    vec = pl.BlockSpec((1, D), lambda i, k: (0, 0))
    modspec = pl.BlockSpec((N_MOD, D), lambda i, k: (0, 0))
    in_specs = [pl.BlockSpec((tm, tk), lambda i, k: (i, k)) for _ in g_list]
    for off in col_offs:
        in_specs.append(pl.BlockSpec((D, tk), functools.partial(lambda i, k, ob: (0, ob + k), ob=off // tk)))
    in_specs += [row, vec, modspec]
    args = list(g_list) + [w] * ng + [h, g, mods]
    out_specs = [row, vec, vec, vec]
    out_shape = [jax.ShapeDtypeStruct((n, D), F32)] + [jax.ShapeDtypeStruct((1, D), F32)] * 3
    if has_res:
        in_specs.append(row)
        args.append(dres)
    if prev is not None:
        in_specs += [row, modspec]
        args += [prev[0], prev[1]]
        out_specs += [row, vec]
        out_shape += [jax.ShapeDtypeStruct((n, D), BF16), jax.ShapeDtypeStruct((1, D), F32)]
    scratch = [pltpu.VMEM((tm, D), F32)]
    if exchange:
        hbm = pl.BlockSpec(memory_space=pltpu.HBM)
        in_specs += [hbm] * nx
        args += exchange.arrays
        out_specs += [hbm] * nx
        out_shape += exchange.out_shapes
        scratch += exchange.scratch
    return pl.pallas_call(
        body, name=name, grid=(ni, nk),
        in_specs=in_specs, out_specs=out_specs, out_shape=out_shape, scratch_shapes=scratch,
        compiler_params=_params(("arbitrary", "arbitrary")),
    )(*args)


def mm_tn(a, g, name, acc=None):
    n, ka = a.shape
    ngc = g.shape[1]
    tka = _tile(ka, (1408, 1024, 512, 256, 128))
    tng = _tile(ngc, (1408, 1024, 512, 256, 128))
    tr = _tile(n, (2048, 1024, 512, 256))
    has_acc = acc is not None

    def body(*refs):
        a_ref, g_ref = refs[0], refs[1]
        o_ref = refs[-1]
        r = pl.program_id(2)

        @pl.when(r == 0)
        def _():
            d = _dot_tn(a_ref[...], g_ref[...])
            o_ref[...] = d + refs[2][...] if has_acc else d

        @pl.when(r > 0)
        def _():
            o_ref[...] += _dot_tn(a_ref[...], g_ref[...])

    out = pl.BlockSpec((tka, tng), lambda p, q, r: (p, q))
    in_specs = [pl.BlockSpec((tr, tka), lambda p, q, r: (r, p)),
                pl.BlockSpec((tr, tng), lambda p, q, r: (r, q))]
    args = [a, g]
    if has_acc:
        in_specs.append(out)
        args.append(acc)
    return pl.pallas_call(
        body, name=name, grid=(ka // tka, ngc // tng, n // tr),
        in_specs=in_specs, out_specs=out,
        out_shape=jax.ShapeDtypeStruct((ka, ngc), F32),
        compiler_params=_params(("parallel", "parallel", "arbitrary")),
    )(*args)


def mm_tn_multi(a, g_list, name, acc=None):
    n, ka = a.shape
    ngc = g_list[0].shape[1]
    ng = len(g_list)
    tka = _tile(ka, (1024, 512, 256, 128))
    tr = _tile(n, (1024, 512, 256))
    has_acc = acc is not None

    def body(*refs):
        a_ref = refs[0]
        g_refs = refs[1:1 + ng]
        o_ref = refs[-1]
        r = pl.program_id(1)

        @pl.when(r == 0)
        def _():
            av = a_ref[...]
            for q in range(ng):
                cs = slice(q * ngc, (q + 1) * ngc)
                d = _dot_tn(av, g_refs[q][...])
                o_ref[:, cs] = d + refs[1 + ng][:, cs] if has_acc else d

        @pl.when(r > 0)
        def _():
            av = a_ref[...]
            for q in range(ng):
                cs = slice(q * ngc, (q + 1) * ngc)
                o_ref[:, cs] += _dot_tn(av, g_refs[q][...])

    out = pl.BlockSpec((tka, ng * ngc), lambda p, r: (p, 0))
    in_specs = [pl.BlockSpec((tr, tka), lambda p, r: (r, p))]
    in_specs += [pl.BlockSpec((tr, ngc), lambda p, r: (r, 0)) for _ in g_list]
    args = [a] + list(g_list)
    if has_acc:
        in_specs.append(out)
        args.append(acc)
    return pl.pallas_call(
        body, name=name, grid=(ka // tka, n // tr),
        in_specs=in_specs, out_specs=out,
        out_shape=jax.ShapeDtypeStruct((ka, ng * ngc), F32),
        compiler_params=_params(("parallel", "arbitrary")),
    )(*args)


def mm_small(a, b, name, trans_b=False):
    m = a.shape[0]
    nout = b.shape[0] if trans_b else b.shape[1]

    def body(a_ref, b_ref, o_ref):
        if trans_b:
            o_ref[...] = lax.dot_general(a_ref[...], b_ref[...], NT_DIMS, precision=HI, preferred_element_type=F32)
        else:
            o_ref[...] = jnp.dot(a_ref[...], b_ref[...], precision=HI, preferred_element_type=F32)

    return pl.pallas_call(
        body, name=name,
        out_shape=jax.ShapeDtypeStruct((m, nout), F32),
        compiler_params=pltpu.CompilerParams(vmem_limit_bytes=VMEM_LIMIT),
    )(a, b)


def _col_tables():
    col = np.arange(GRID_W)
    start = np.clip(col - KW // 2, 0, GRID_W - KW)
    ok = (col[None, :] >= start[:, None]) & (col[None, :] < start[:, None] + KW)
    ci = np.clip(col[None, :] - col[:, None] + (KW - 1), 0, 2 * KW - 2)
    e = np.zeros((2 * KW - 1, GRID_W, GRID_W), np.float32)
    for c in range(2 * KW - 1):
        e[c] = (ci == c) & ok
    return e.reshape(2 * KW - 1, GRID_W * GRID_W), ok


def bias_table(rpb):
    e, ok = _col_tables()
    e_pad = np.zeros((32, GRID_W * GRID_W), np.float32)
    e_pad[:31] = e
    rp = jnp.pad(rpb.reshape(HEADS * 15, 31), ((0, 0), (0, 1)))
    t = mm_small(rp, jnp.asarray(e_pad), "rpb_expand").reshape(HEADS, 15, GRID_W, GRID_W)
    t = jnp.where(jnp.asarray(ok)[None, None], t, NEG_INF)
    tab = jnp.stack([t[:, v:v + KH] for v in range(8)], axis=0)
    return tab.transpose(0, 1, 3, 2, 4).reshape(TAB_SHAPE)


def bias_table_bwd(dtab):
    e, _ = _col_tables()
    e_pad = np.zeros((128, GRID_W * GRID_W), np.float32)
    e_pad[:31] = e
    d = dtab.reshape(8, HEADS, GRID_W, KH, GRID_W).transpose(0, 1, 3, 2, 4).reshape(8 * HEADS * KH, GRID_W * GRID_W)
    gv = mm_small(d, jnp.asarray(e_pad), "rpb_reduce", trans_b=True)[:, :31]
    gv = gv.reshape(8, HEADS, KH, 31).transpose(0, 2, 1, 3).reshape(8 * KH, HEADS * 31)
    sel = np.zeros((16, 8 * KH), np.float32)
    for v in range(8):
        for j in range(KH):
            sel[v + j, v * KH + j] = 1.0
    gpad = jnp.pad(gv, ((0, 0), (0, 256 - HEADS * 31)))
    out = mm_small(jnp.asarray(sel), gpad, "rpb_fold")[:15, :HEADS * 31]
    return out.reshape(15, HEADS, 31).transpose(1, 0, 2)


def _attn_geometry(seq):
    rows = seq // GRID_W
    nb = rows // QROWS
    return rows, nb


def _stack_heads(t2):
    first = (lax.broadcasted_iota(jnp.int32, (1, 128), 1) // HEAD_DIM) == 0
    zero = jnp.zeros_like(t2)
    return jnp.concatenate([jnp.where(first, t2, zero), jnp.where(first, zero, t2)], axis=0)


def _unstack_heads(t):
    first = (lax.broadcasted_iota(jnp.int32, (1, 128), 1) // HEAD_DIM) == 0
    return jnp.where(first, t[0:GRID_W], t[GRID_W:2 * GRID_W])


TAB_SHAPE = (8, HEADS // 2, 2 * GRID_W, KH * GRID_W)


def attn_fwd(qkvu, qkvu_c, tab, name, exchange=None):
    seq = qkvu.shape[0]
    nctx = qkvu_c.shape[0]
    rows, nb = _attn_geometry(seq)
    qt = QROWS * GRID_W
    wt = WROWS * GRID_W
    scale = HEAD_DIM ** -0.5
    nx = exchange.n if exchange else 0

    def wb0(i):
        return jnp.clip(i - 1, 0, nb - 3)

    def body(*refs):
        q_ref, k0, k1, k2, v0, v1, v2, kc_ref, vc_ref, tab_hbm = refs[:10]
        x_in = refs[10:10 + nx]
        o_ref = refs[10 + nx]
        x_out = refs[11 + nx:11 + 2 * nx]
        kbuf, vbuf, tab_s, sem = refs[11 + 2 * nx:15 + 2 * nx]
        x_sems = refs[15 + 2 * nx:]
        i = pl.program_id(0)

        @pl.when(i == 0)
        def _():
            if exchange:
                exchange.start(x_in, x_out, x_sems)
            cp = pltpu.make_async_copy(tab_hbm, tab_s, sem)
            cp.start()
            cp.wait()

        for t, (kr, vr) in enumerate(((k0, v0), (k1, v1), (k2, v2))):
            kbuf[t * qt:(t + 1) * qt, :] = kr[...]
            vbuf[t * qt:(t + 1) * qt, :] = vr[...]
        base = wb0(i) * QROWS

        def row_body(rl, carry):
            r = i * QROWS + rl
            rs = jnp.clip(r - KH // 2, 0, rows - KH)
            vi = rs - r + (KH - 1)
            off = pl.multiple_of((rs - base) * GRID_W, GRID_W)
            qoff = pl.multiple_of(rl * GRID_W, GRID_W)
            for p in range(HEADS // 2):
                ls = slice(p * 128, (p + 1) * 128)
                qst = _stack_heads(q_ref[pl.ds(qoff, GRID_W), ls])
                k2v = kbuf[pl.ds(off, KH * GRID_W), ls]
                v2v = vbuf[pl.ds(off, KH * GRID_W), ls]
                s_w = _dot_nt(qst, k2v) * scale + tab_s[vi, p]
                s_c = _dot_nt(qst, kc_ref[:, ls]) * scale
                m = jnp.maximum(jnp.max(s_w, axis=-1, keepdims=True), jnp.max(s_c, axis=-1, keepdims=True))
                pw = jnp.exp(s_w - m)
                pc = jnp.exp(s_c - m)
                l = jnp.sum(pw, axis=-1, keepdims=True) + jnp.sum(pc, axis=-1, keepdims=True)
                o = _dot(pw.astype(BF16), v2v) + _dot(pc.astype(BF16), vc_ref[:, ls])
                o_ref[pl.ds(qoff, GRID_W), ls] = _unstack_heads(o * (1.0 / l)).astype(BF16)
            return carry

        lax.fori_loop(0, QROWS, row_body, 0)

        if exchange:
            @pl.when(i == nb - 1)
            def _():
                exchange.finish(x_in, x_out, x_sems)

    blk = lambda col: [pl.BlockSpec((qt, NA_W), functools.partial(lambda i, t, c: (wb0(i) + t, c), t=t, c=col))
                       for t in range(3)]
    hbm = pl.BlockSpec(memory_space=pltpu.HBM)
    res = pl.pallas_call(
        body, name=name, grid=(nb,),
        in_specs=[pl.BlockSpec((qt, NA_W), lambda i: (i, 0))] + blk(1) + blk(2)
                 + [pl.BlockSpec((nctx, NA_W), lambda i: (0, 1)), pl.BlockSpec((nctx, NA_W), lambda i: (0, 2)),
                    pl.BlockSpec(memory_space=pl.ANY)] + [hbm] * nx,
        out_specs=[pl.BlockSpec((qt, NA_W), lambda i: (i, 0))] + [hbm] * nx,
        out_shape=[jax.ShapeDtypeStruct((seq, NA_W), BF16)] + (exchange.out_shapes if exchange else []),
        scratch_shapes=[pltpu.VMEM((wt, NA_W), BF16), pltpu.VMEM((wt, NA_W), BF16),
                        pltpu.VMEM(TAB_SHAPE, F32), pltpu.SemaphoreType.DMA] + (exchange.scratch if exchange else []),
        compiler_params=_params(("arbitrary",)),
    )(qkvu, qkvu, qkvu, qkvu, qkvu, qkvu, qkvu, qkvu_c, qkvu_c, tab, *(exchange.arrays if exchange else []))
    return res[0], list(res[1:])


def attn_bwd(qkvu, qkvu_c, tab, dmix, name, exchange=None):
    seq = qkvu.shape[0]
    nctx = qkvu_c.shape[0]
    rows, nb = _attn_geometry(seq)
    qt = QROWS * GRID_W
    wt = WROWS * GRID_W
    scale = HEAD_DIM ** -0.5
    nx = exchange.n if exchange else 0

    def wb0(i):
        return jnp.clip(i - 1, 0, nb - 3)

    def body(*refs):
        q_ref, k0, k1, k2, v0, v1, v2, kc_ref, vc_ref, do_ref, tab_hbm = refs[:11]
        x_in = refs[11:11 + nx]
        dq_ref, dk_hbm, dv_hbm, dkc_ref, dvc_ref, dtab_hbm = refs[11 + nx:17 + nx]
        x_out = refs[17 + nx:17 + 2 * nx]
        kbuf, vbuf, dkacc, dvacc, tab_s, dtab_s, stage, sem = refs[17 + 2 * nx:25 + 2 * nx]
        x_sems = refs[25 + 2 * nx:]
        i = pl.program_id(0)

        if exchange:
            @pl.when(i == 0)
            def _():
                exchange.start(x_in, x_out, x_sems)

        def flush(src, dst, block, dst_row):
            stage[...] = src[block * qt:(block + 1) * qt, :].astype(BF16)
            cp = pltpu.make_async_copy(stage, dst.at[pl.ds(dst_row, qt)], sem)
            cp.start()
            cp.wait()

        @pl.when(i == 0)
        def _():
            cp = pltpu.make_async_copy(tab_hbm, tab_s, sem)
            cp.start()
            cp.wait()
            dtab_s[...] = jnp.zeros_like(dtab_s)
            dkacc[...] = jnp.zeros_like(dkacc)
            dvacc[...] = jnp.zeros_like(dvacc)
            dkc_ref[...] = jnp.zeros_like(dkc_ref)
            dvc_ref[...] = jnp.zeros_like(dvc_ref)

        @pl.when((i >= 2) & (i <= nb - 2))
        def _():
            dst_row = pl.multiple_of((i - 2) * qt, qt)
            for acc_ref, dst in ((dkacc, dk_hbm), (dvacc, dv_hbm)):
                flush(acc_ref, dst, 0, dst_row)
                acc_ref[0:qt, :] = acc_ref[qt:2 * qt, :]
                acc_ref[qt:2 * qt, :] = acc_ref[2 * qt:3 * qt, :]
                acc_ref[2 * qt:3 * qt, :] = jnp.zeros((qt, NA_W), F32)

        for t, (kr, vr) in enumerate(((k0, v0), (k1, v1), (k2, v2))):
            kbuf[t * qt:(t + 1) * qt, :] = kr[...]
            vbuf[t * qt:(t + 1) * qt, :] = vr[...]
        base = wb0(i) * QROWS

        def row_body(rl, carry):
            r = i * QROWS + rl
            rs = jnp.clip(r - KH // 2, 0, rows - KH)
            vi = rs - r + (KH - 1)
            off = pl.multiple_of((rs - base) * GRID_W, GRID_W)
            qoff = pl.multiple_of(rl * GRID_W, GRID_W)
            for p in range(HEADS // 2):
                ls = slice(p * 128, (p + 1) * 128)
                qst = _stack_heads(q_ref[pl.ds(qoff, GRID_W), ls])
                dost = _stack_heads(do_ref[pl.ds(qoff, GRID_W), ls])
                k2v = kbuf[pl.ds(off, KH * GRID_W), ls]
                v2v = vbuf[pl.ds(off, KH * GRID_W), ls]
                kc2 = kc_ref[:, ls]
                vc2 = vc_ref[:, ls]
                s_w = _dot_nt(qst, k2v) * scale + tab_s[vi, p]
                s_c = _dot_nt(qst, kc2) * scale
                m = jnp.maximum(jnp.max(s_w, axis=-1, keepdims=True), jnp.max(s_c, axis=-1, keepdims=True))
                pw = jnp.exp(s_w - m)
                pc = jnp.exp(s_c - m)
                inv = 1.0 / (jnp.sum(pw, axis=-1, keepdims=True) + jnp.sum(pc, axis=-1, keepdims=True))
                pw = pw * inv
                pc = pc * inv
                dpw = _dot_nt(dost, v2v)
                dpc = _dot_nt(dost, vc2)
                delta = jnp.sum(pw * dpw, axis=-1, keepdims=True) + jnp.sum(pc * dpc, axis=-1, keepdims=True)
                ds_w = pw * (dpw - delta)
                ds_c = pc * (dpc - delta)
                dtab_s[vi, p] += ds_w
                dsw16 = ds_w.astype(BF16)
                dsc16 = ds_c.astype(BF16)
                dq = (_dot(dsw16, k2v) + _dot(dsc16, kc2)) * scale
                dq_ref[pl.ds(qoff, GRID_W), ls] = _unstack_heads(dq).astype(BF16)
                dkacc[pl.ds(off, KH * GRID_W), ls] += _dot_tn(dsw16, qst) * scale
                dvacc[pl.ds(off, KH * GRID_W), ls] += _dot_tn(pw.astype(BF16), dost)
                dkc_ref[:, ls] += _dot_tn(dsc16, qst) * scale
                dvc_ref[:, ls] += _dot_tn(pc.astype(BF16), dost)
            return carry

        lax.fori_loop(0, QROWS, row_body, 0)

        @pl.when(i == nb - 1)
        def _():
            for t in range(3):
                dst_row = (nb - 3 + t) * qt
                flush(dkacc, dk_hbm, t, dst_row)
                flush(dvacc, dv_hbm, t, dst_row)
            cp = pltpu.make_async_copy(dtab_s, dtab_hbm, sem)
            cp.start()
            cp.wait()
            if exchange:
                exchange.finish(x_in, x_out, x_sems)

    blk = lambda col: [pl.BlockSpec((qt, NA_W), functools.partial(lambda i, t, c: (wb0(i) + t, c), t=t, c=col))
                       for t in range(3)]
    any_spec = pl.BlockSpec(memory_space=pl.ANY)
    hbm = pl.BlockSpec(memory_space=pltpu.HBM)
    res = pl.pallas_call(
        body, name=name, grid=(nb,),
        in_specs=[pl.BlockSpec((qt, NA_W), lambda i: (i, 0))] + blk(1) + blk(2)
                 + [pl.BlockSpec((nctx, NA_W), lambda i: (0, 1)), pl.BlockSpec((nctx, NA_W), lambda i: (0, 2)),
                    pl.BlockSpec((qt, NA_W), lambda i: (i, 0)), any_spec] + [hbm] * nx,
        out_specs=[pl.BlockSpec((qt, NA_W), lambda i: (i, 0)), any_spec, any_spec,
                   pl.BlockSpec((nctx, NA_W), lambda i: (0, 0)), pl.BlockSpec((nctx, NA_W), lambda i: (0, 0)),
                   any_spec] + [hbm] * nx,
        out_shape=[jax.ShapeDtypeStruct((seq, NA_W), BF16), jax.ShapeDtypeStruct((seq, NA_W), BF16),
                   jax.ShapeDtypeStruct((seq, NA_W), BF16), jax.ShapeDtypeStruct((nctx, NA_W), F32),
                   jax.ShapeDtypeStruct((nctx, NA_W), F32), jax.ShapeDtypeStruct(TAB_SHAPE, F32)]
                  + (exchange.out_shapes if exchange else []),
        scratch_shapes=[pltpu.VMEM((wt, NA_W), BF16), pltpu.VMEM((wt, NA_W), BF16),
                        pltpu.VMEM((wt, NA_W), F32), pltpu.VMEM((wt, NA_W), F32),
                        pltpu.VMEM(TAB_SHAPE, F32), pltpu.VMEM(TAB_SHAPE, F32), pltpu.VMEM((qt, NA_W), BF16),
                        pltpu.SemaphoreType.DMA]
                       + (exchange.scratch if exchange else []),
        compiler_params=_params(("arbitrary",)),
    )(qkvu, qkvu, qkvu, qkvu, qkvu, qkvu, qkvu, qkvu_c, qkvu_c, dmix, tab, *(exchange.arrays if exchange else []))
    return res[:6], list(res[6:])


def _halo_specs(te, seq, col, width):
    per = te // HALO
    last = seq // HALO - 1
    return [pl.BlockSpec((HALO, width), lambda i: (jnp.maximum(i * per - 1, 0), col)),
            pl.BlockSpec((te, width), lambda i: (i, col)),
            pl.BlockSpec((HALO, width), lambda i: (jnp.minimum((i + 1) * per, last), col))]


def _extended(prev_ref, cur_ref, next_ref, i, te, seq):
    xe = jnp.concatenate([prev_ref[...], cur_ref[...], next_ref[...]], axis=0).astype(F32)
    pos = i * te - HALO + lax.broadcasted_iota(jnp.int32, (te + 2 * HALO, 1), 0)
    return jnp.where((pos >= 0) & (pos < seq), xe, 0.0), pos


def _window_sum(x, levels, n, mirrored):
    first = (n - 1) if mirrored else 1
    acc = x + pltpu.roll(x, first, 0)
    step = 1
    for _ in range(levels - 1):
        acc = pltpu.roll(acc, step, 0) + pltpu.roll(acc, n - step, 0)
        step *= 2
    return acc


def _window_count(pos, w, seq):
    lo = jnp.clip(pos - w // 2, 0, seq)
    hi = jnp.clip(pos - w // 2 + w, 0, seq)
    return jnp.maximum(hi - lo, 1).astype(F32)


def pool_fwd(qkvu, pool_w, pool_scale, name):
    seq = qkvu.shape[0]
    te = _tile(seq, (512, 256))
    n = te + 2 * HALO

    def body(up_ref, uc_ref, un_ref, w_ref, sc_ref, o_ref):
        i = pl.program_id(0)
        xe, pos = _extended(up_ref, uc_ref, un_ref, i, te, seq)
        cnt = pos[HALO:HALO + te]
        for g, w in enumerate(POOL_WINDOWS):
            ls = slice(g * POOL_G, (g + 1) * POOL_G)
            xg = xe[:, ls]
            win = _window_sum(xg, g + 1, n, False)[HALO:HALO + te]
            dlt = win / _window_count(cnt, w, seq) - xg[HALO:HALO + te]
            z = _dot(dlt.astype(BF16), w_ref[g])
            o_ref[:, ls] = (z * sc_ref[:, ls]).astype(BF16)

    return pl.pallas_call(
        body, name=name, grid=(seq // te,),
        in_specs=_halo_specs(te, seq, 3, POOL_W)
                 + [pl.BlockSpec((4, POOL_G, POOL_G), lambda i: (0, 0, 0)), pl.BlockSpec((1, POOL_W), lambda i: (0, 0))],
        out_specs=pl.BlockSpec((te, POOL_W), lambda i: (i, 0)),
        out_shape=jax.ShapeDtypeStruct((seq, POOL_W), BF16),
        compiler_params=_params(("parallel",)),
    )(qkvu, qkvu, qkvu, pool_w, pool_scale)


def pool_bwd(qkvu, dmix, pool_w, pool_scale, name):
    seq = qkvu.shape[0]
    te = _tile(seq, (512, 256))
    n = te + 2 * HALO

    def body(up_ref, uc_ref, un_ref, dp_ref, dc_ref, dn_ref, w_ref, sc_ref, du_ref, dw_ref, dsc_ref):
        i = pl.program_id(0)

        @pl.when(i == 0)
        def _():
            dw_ref[...] = jnp.zeros_like(dw_ref)
            dsc_ref[...] = jnp.zeros_like(dsc_ref)

        xe, pos = _extended(up_ref, uc_ref, un_ref, i, te, seq)
        de, _ = _extended(dp_ref, dc_ref, dn_ref, i, te, seq)
        cpos = pos[HALO:HALO + te]
        for g, w in enumerate(POOL_WINDOWS):
            ls = slice(g * POOL_G, (g + 1) * POOL_G)
            xg = xe[:, ls]
            wg = w_ref[g]
            win = _window_sum(xg, g + 1, n, False)[HALO:HALO + te]
            dlt = (win / _window_count(cpos, w, seq) - xg[HALO:HALO + te]).astype(BF16)
            z = _dot(dlt, wg)
            dpg = de[:, ls]
            dsc_ref[:, ls] += jnp.sum(dpg[HALO:HALO + te] * z, axis=0, keepdims=True)
            dz = (dpg * sc_ref[:, ls]).astype(BF16)
            dw_ref[g] += _dot_tn(dlt, dz[HALO:HALO + te])
            dd = _dot_nt(dz, wg)
            back = _window_sum(dd / _window_count(pos, w, seq), g + 1, n, True)
            du_ref[:, ls] = (back[HALO:HALO + te] - dd[HALO:HALO + te]).astype(BF16)

    return pl.pallas_call(
        body, name=name, grid=(seq // te,),
        in_specs=_halo_specs(te, seq, 3, POOL_W) + _halo_specs(te, seq, 1, POOL_W)
                 + [pl.BlockSpec((4, POOL_G, POOL_G), lambda i: (0, 0, 0)), pl.BlockSpec((1, POOL_W), lambda i: (0, 0))],
        out_specs=[pl.BlockSpec((te, POOL_W), lambda i: (i, 0)),
                   pl.BlockSpec((4, POOL_G, POOL_G), lambda i: (0, 0, 0)), pl.BlockSpec((1, POOL_W), lambda i: (0, 0))],
        out_shape=[jax.ShapeDtypeStruct((seq, POOL_W), BF16), jax.ShapeDtypeStruct((4, POOL_G, POOL_G), F32),
                   jax.ShapeDtypeStruct((1, POOL_W), F32)],
        compiler_params=_params(("arbitrary",)),
    )(qkvu, qkvu, qkvu, dmix, dmix, dmix, pool_w, pool_scale)


def _shifted(z, zprev_row, znext_row, te):
    rows = lax.broadcasted_iota(jnp.int32, (te, 1), 0)
    zp = jnp.where(rows == 0, zprev_row, pltpu.roll(z, 1, 0))
    zn = jnp.where(rows == te - 1, znext_row, pltpu.roll(z, te - 1, 0))
    return zp, zn


def _edge_rows(prev_ref, next_ref, i, nt):
    p = prev_ref[HALO - 1:HALO, :].astype(F32)
    q = next_ref[0:1, :].astype(F32)
    return jnp.where(i == 0, 0.0, p), jnp.where(i == nt - 1, 0.0, q)


def conv_fwd(proj, conv_w, name):
    seq = proj.shape[0]
    te = _tile(seq, (512, 256))
    nt = seq // te

    def body(bg_ref, cp_ref, cc_ref, cn_ref, xp_ref, xc_ref, xn_ref, w_ref, o_ref):
        i = pl.program_id(0)
        z = cc_ref[...].astype(F32) * xc_ref[...].astype(F32)
        cpr, cnr = _edge_rows(cp_ref, cn_ref, i, nt)
        xpr, xnr = _edge_rows(xp_ref, xn_ref, i, nt)
        zp, zn = _shifted(z, cpr * xpr, cnr * xnr, te)
        y = zp * w_ref[0:1, :] + z * w_ref[1:2, :] + zn * w_ref[2:3, :]
        o_ref[...] = (bg_ref[...].astype(F32) * y).astype(BF16)

    return pl.pallas_call(
        body, name=name, grid=(nt,),
        in_specs=[pl.BlockSpec((te, D), lambda i: (i, 0))] + _halo_specs(te, seq, 1, D) + _halo_specs(te, seq, 2, D)
                 + [pl.BlockSpec((3, D), lambda i: (0, 0))],
        out_specs=pl.BlockSpec((te, D), lambda i: (i, 0)),
        out_shape=jax.ShapeDtypeStruct((seq, D), BF16),
        compiler_params=_params(("parallel",)),
    )(proj, proj, proj, proj, proj, proj, proj, conv_w)


def conv_bwd(proj, dgm, conv_w, name):
    seq = proj.shape[0]
    te = _tile(seq, (512, 256))
    nt = seq // te

    def body(bp_ref, bc_ref, bn_ref, cp_ref, cc_ref, cn_ref, xp_ref, xc_ref, xn_ref, gp_ref, gc_ref, gn_ref, w_ref,
             dbg_ref, dcg_ref, dxin_ref, dw_ref):
        i = pl.program_id(0)

        @pl.when(i == 0)
        def _():
            dw_ref[...] = jnp.zeros_like(dw_ref)

        bg = bc_ref[...].astype(F32)
        cg = cc_ref[...].astype(F32)
        xin = xc_ref[...].astype(F32)
        dg = gc_ref[...].astype(F32)
        z = cg * xin
        cpr, cnr = _edge_rows(cp_ref, cn_ref, i, nt)
        xpr, xnr = _edge_rows(xp_ref, xn_ref, i, nt)
        zp, zn = _shifted(z, cpr * xpr, cnr * xnr, te)
        w0, w1, w2 = w_ref[0:1, :], w_ref[1:2, :], w_ref[2:3, :]
        y = zp * w0 + z * w1 + zn * w2
        dbg_ref[...] = (dg * y).astype(BF16)
        dy = dg * bg
        dw_ref[0:1, :] += jnp.sum(dy * zp, axis=0, keepdims=True)
        dw_ref[1:2, :] += jnp.sum(dy * z, axis=0, keepdims=True)
        dw_ref[2:3, :] += jnp.sum(dy * zn, axis=0, keepdims=True)
        bpr, bnr = _edge_rows(bp_ref, bn_ref, i, nt)
        gpr, gnr = _edge_rows(gp_ref, gn_ref, i, nt)
        dyp, dyn = _shifted(dy, bpr * gpr, bnr * gnr, te)
        dz = dyn * w0 + dy * w1 + dyp * w2
        dcg_ref[...] = (dz * xin).astype(BF16)
        dxin_ref[...] = (dz * cg).astype(BF16)

    row = pl.BlockSpec((te, D), lambda i: (i, 0))
    return pl.pallas_call(
        body, name=name, grid=(nt,),
        in_specs=_halo_specs(te, seq, 0, D) + _halo_specs(te, seq, 1, D) + _halo_specs(te, seq, 2, D)
                 + _halo_specs(te, seq, 0, D) + [pl.BlockSpec((3, D), lambda i: (0, 0))],
        out_specs=[row, row, row, pl.BlockSpec((3, D), lambda i: (0, 0))],
        out_shape=[jax.ShapeDtypeStruct((seq, D), BF16)] * 3 + [jax.ShapeDtypeStruct((3, D), F32)],
        compiler_params=_params(("arbitrary",)),
    )(proj, proj, proj, proj, proj, proj, proj, proj, proj, dgm, dgm, dgm, conv_w)


def _position():
    x, y, c = lax.axis_index("x"), lax.axis_index("y"), lax.axis_index("c")
    return x, y, c, 4 * x + 2 * y + c


def _peer(x, y, c, j):
    px = 1 - x if j & 4 else x
    py = 1 - y if j & 2 else y
    pc = 1 - c if j & 1 else c
    return (px, py, pc), 4 * px + 2 * py + pc


def small_allgather(v, name):
    rows, cols = v.shape

    def body(v_ref, o_ref, send_sems, recv_sems, local_sem):
        x, y, c, me = _position()
        mine = pltpu.make_async_copy(v_ref, o_ref.at[me], local_sem)
        mine.start()
        sends = []
        for j in range(1, N_DEV):
            peer, _ = _peer(x, y, c, j)
            cp = pltpu.make_async_remote_copy(src_ref=v_ref, dst_ref=o_ref.at[me], send_sem=send_sems.at[j - 1],
                                              recv_sem=recv_sems.at[j - 1], device_id=peer, device_id_type=MESH_ID)
            cp.start()
            sends.append(cp)
        for j in range(1, N_DEV):
            peer, pid = _peer(x, y, c, j)
            pltpu.make_async_remote_copy(src_ref=v_ref, dst_ref=o_ref.at[pid], send_sem=send_sems.at[j - 1],
                                         recv_sem=recv_sems.at[j - 1], device_id=peer,
                                         device_id_type=MESH_ID).wait_recv()
        for cp in sends:
            cp.wait_send()
        mine.wait()

    return pl.pallas_call(
        body, name=name,
        out_shape=jax.ShapeDtypeStruct((N_DEV, rows, cols), v.dtype),
        in_specs=[pl.BlockSpec(memory_space=pltpu.VMEM)],
        out_specs=pl.BlockSpec(memory_space=pltpu.VMEM),
        scratch_shapes=[pltpu.SemaphoreType.DMA((N_DEV - 1,)), pltpu.SemaphoreType.DMA((N_DEV - 1,)),
                        pltpu.SemaphoreType.DMA],
        compiler_params=pltpu.CompilerParams(vmem_limit_bytes=VMEM_LIMIT),
    )(v)


class Exchange:
    def __init__(self, kind, arrays):
        self.kind, self.arrays, self.n = kind, list(arrays), len(arrays)
        n = self.n
        if kind == "gather":
            self.out_shapes = [jax.ShapeDtypeStruct((N_DEV,) + a.shape, a.dtype) for a in self.arrays]
        else:
            self.out_shapes = [jax.ShapeDtypeStruct(a.shape, a.dtype) for a in self.arrays]
        self.scratch = [pltpu.SemaphoreType.DMA((7 * n,)), pltpu.SemaphoreType.DMA((7 * n,)),
                        pltpu.SemaphoreType.DMA((n,))]

    def _gather_copies(self, ins, outs, sems):
        send_sems, recv_sems, local_sems = sems
        x, y, c, me = _position()
        chips = [(1 - x, y), (x, 1 - y), (1 - x, 1 - y)]

        def blk(k, px, py, pc):
            return outs[k].at[4 * px + 2 * py + pc]

        def copy(k, slot, block, to, src=None):
            return pltpu.make_async_remote_copy(
                src_ref=blk(k, *block) if src is None else src, dst_ref=blk(k, *block),
                send_sem=send_sems.at[k * 7 + slot], recv_sem=recv_sems.at[k * 7 + slot],
                device_id=to, device_id_type=MESH_ID)

        mine = [pltpu.make_async_copy(ins[k], blk(k, x, y, c), local_sems.at[k]) for k in range(self.n)]
        first = []
        for k in range(self.n):
            first.append(copy(k, 0, (x, y, c), (x, y, 1 - c), src=ins[k]))
            first += [copy(k, 1 + j, (x, y, c), (*chip, c), src=ins[k]) for j, chip in enumerate(chips)]
        return (x, y, c), chips, copy, mine, first

    def start(self, ins, outs, sems):
        if self.kind == "gather":
            _, _, _, mine, first = self._gather_copies(ins, outs, sems)
            for cp in mine + first:
                cp.start()
        else:
            for cp in self._scatter_copies(ins, outs, sems, False):
                cp.start()

    def finish(self, ins, outs, sems):
        if self.kind == "gather":
            (x, y, c), chips, copy, mine, first = self._gather_copies(ins, outs, sems)
            passed = []
            for j, chip in enumerate(chips):
                for k in range(self.n):
                    copy(k, 1 + j, (*chip, c), (x, y, c)).wait_recv()
                    cp = copy(k, 4 + j, (*chip, c), (x, y, 1 - c))
                    cp.start()
                    passed.append(cp)
            for k in range(self.n):
                copy(k, 0, (x, y, 1 - c), (x, y, c)).wait_recv()
                for j, chip in enumerate(chips):
                    copy(k, 4 + j, (*chip, 1 - c), (x, y, c)).wait_recv()
            for cp in first + passed:
                cp.wait_send()
            for cp in mine:
                cp.wait()
        else:
            for cp in self._scatter_copies(ins, outs, sems, True):
                cp.wait_recv()
            copies = self._scatter_copies(ins, outs, sems, False)
            for cp in copies[self.n:]:
                cp.wait_send()
            for cp in copies[:self.n]:
                cp.wait()

    def _scatter_copies(self, ins, outs, sems, arrivals):
        send_sems, recv_sems, local_sems = sems
        x, y, c, me = _position()
        out = []
        if not arrivals:
            out = [pltpu.make_async_copy(ins[k].at[me], outs[k].at[me], local_sems.at[k]) for k in range(self.n)]
        for j in range(1, N_DEV):
            peer, pid = _peer(x, y, c, j)
            for k in range(self.n):
                out.append(pltpu.make_async_remote_copy(
                    src_ref=ins[k].at[pid], dst_ref=outs[k].at[pid if arrivals else me],
                    send_sem=send_sems.at[k * 7 + j - 1], recv_sem=recv_sems.at[k * 7 + j - 1],
                    device_id=peer, device_id_type=MESH_ID))
        return out


def sum_devices(v, name):
    _, rows, cols = v.shape

    def body(v_ref, o_ref):
        acc = v_ref[0]
        for p in range(1, N_DEV):
            acc = acc + v_ref[p]
        o_ref[...] = acc

    return pl.pallas_call(
        body, name=name, out_shape=jax.ShapeDtypeStruct((rows, cols), F32),
        compiler_params=pltpu.CompilerParams(vmem_limit_bytes=VMEM_LIMIT),
    )(v)


def _silu(x):
    return x * _sigmoid(x)


def adaln_fwd(cm, mod_w, mod_b_cols, name):
    cols = mod_w.shape[2]

    def body(c_ref, w_ref, b_ref, o_ref):
        o_ref[0] = jnp.dot(_silu(c_ref[...]), w_ref[0], precision=HI, preferred_element_type=F32) + b_ref[0]

    return pl.pallas_call(
        body, name=name, grid=(2,),
        in_specs=[pl.BlockSpec((16, D), lambda l: (0, 0)), pl.BlockSpec((1, D, cols), lambda l: (l, 0, 0)),
                  pl.BlockSpec((1, 1, cols), lambda l: (l, 0, 0))],
        out_specs=pl.BlockSpec((1, 16, cols), lambda l: (l, 0, 0)),
        out_shape=jax.ShapeDtypeStruct((2, 16, cols), F32),
        compiler_params=_params(("parallel",)),
    )(cm, mod_w, mod_b_cols)


def adaln_bwd(cm_t, mod_w, dm_t, name):
    cols = mod_w.shape[2]

    def body(c_ref, w_ref, lat_ref, ctx_ref, gw_ref, pc_ref):
        ctot = jnp.sum(ctx_ref[0], axis=0, keepdims=True)
        rows = lax.broadcasted_iota(jnp.int32, (8, 1), 0)
        g_hi = jnp.where(rows == 0, ctot, 0.0)
        g = jnp.concatenate([lat_ref[0], g_hi], axis=0)
        gw_ref[0] = jnp.dot(_silu(c_ref[...]), g, precision=HI, preferred_element_type=F32)
        pc_ref[0] = lax.dot_general(g_hi, w_ref[0], NT_DIMS, precision=HI, preferred_element_type=F32)

    return pl.pallas_call(
        body, name=name, grid=(2,),
        in_specs=[pl.BlockSpec((D, 16), lambda l: (0, 0)), pl.BlockSpec((1, D, cols), lambda l: (l, 0, 0)),
                  pl.BlockSpec((1, 8, cols), lambda l: (l, 0, 0)), pl.BlockSpec((1, 8, cols), lambda l: (l + 2, 0, 0))],
        out_specs=[pl.BlockSpec((1, D, cols), lambda l: (l, 0, 0)), pl.BlockSpec((1, 8, D), lambda l: (l, 0, 0))],
        out_shape=[jax.ShapeDtypeStruct((2, D, cols), F32), jax.ShapeDtypeStruct((2, 8, D), F32)],
        compiler_params=_params(("parallel",)),
    )(cm_t, mod_w, dm_t, dm_t)


def mod_b_grad(dm_t, name):
    width = dm_t.shape[2]
    tn = width // 8

    def body(d_ref, o_ref):
        s = jnp.concatenate([jnp.sum(d_ref[k], axis=0, keepdims=True) for k in range(4)]
                            + [jnp.zeros((4, tn), F32)], axis=0)
        o_ref[...] = s + pltpu.roll(s, 6, 0)

    return pl.pallas_call(
        body, name=name, grid=(8,),
        in_specs=[pl.BlockSpec((4, 8, tn), lambda j: (0, 0, j))],
        out_specs=pl.BlockSpec((8, tn), lambda j: (0, j)),
        out_shape=jax.ShapeDtypeStruct((8, width), F32),
        compiler_params=_params(("parallel",)),
    )(dm_t)


def adamw(w, m, v, name, g=None, recv=None):
    rows, cols = w.shape
    tr = _tile(rows, (256, 128, 64, 32, 16, 8))
    summed = recv is not None

    def body(w_ref, m_ref, v_ref, g_ref, go_ref, d_ref, mo_ref, vo_ref):
        if summed:
            gv = g_ref[0].astype(F32)
            for p in range(1, N_DEV):
                gv = gv + g_ref[p].astype(F32)
        else:
            gv = g_ref[...]
        mn = ADAM_B1 * m_ref[...] + (1.0 - ADAM_B1) * gv
        vn = ADAM_B2 * v_ref[...] + (1.0 - ADAM_B2) * (gv * gv)
        m_hat = mn / (1.0 - ADAM_B1 ** ADAM_STEP)
        v_hat = vn / (1.0 - ADAM_B2 ** ADAM_STEP)
        go_ref[...] = gv
        d_ref[...] = -ADAM_LR * (m_hat / (jnp.sqrt(v_hat) + ADAM_EPS) + ADAM_WD * w_ref[...])
        mo_ref[...] = mn
        vo_ref[...] = vn

    row = pl.BlockSpec((tr, cols), lambda i: (i, 0))
    gspec = pl.BlockSpec((N_DEV, tr, cols), lambda i: (0, i, 0)) if summed else row
    return pl.pallas_call(
        body, name=name, grid=(rows // tr,),
        in_specs=[row, row, row, gspec], out_specs=[row] * 4,
        out_shape=[jax.ShapeDtypeStruct((rows, cols), F32)] * 4,
        compiler_params=_params(("parallel",)),
    )(w, m, v, recv if summed else g)


def _ffn_fwd(h, hn, mods, w13, w2, base, tag, nxt=None, exchange=None):
    (p, u, s), exchanged = ffn_up(hn, w13, tag + "_up", exchange)
    if callable(w2):
        w2 = w2(exchanged)
    outs = mm_nn([s], w2, [0], tag + "_down", res=(h, mods, base + 2, 0.5), nxt=nxt)
    h_new, y = outs[0], outs[1]
    return h_new, (outs[2] if nxt else None), (h, hn, p, u, s, y), exchanged


COLUMN_CUT = ("w13", "ewi", "cwi")
GATHER_FIRST = ("w13_00",)
GATHER_IN_FFN = ("w2_00", "ewi", "ewo", "w13_01", "w2_01")
GATHER_IN_ATTN = ("w13_10", "w2_10", "cwi", "cwo", "w13_11", "w2_11")
SCATTER_IN_ATTN = ("w13_11", "w2_11", "cwi", "cwo", "w13_10", "w2_10", "w13_01", "w2_01", "ewo")
SCATTER_LAST = ("w13_00", "w2_00", "ewi")


def unpack_piece(p, g):
    if p.split("_")[0] in COLUMN_CUT:
        return g.transpose(1, 0, 2).reshape(g.shape[1], -1)
    return g.reshape(-1, g.shape[2])


def block_piece(p, full):
    if p.split("_")[0] in COLUMN_CUT:
        return full.reshape(full.shape[0], N_DEV, -1).transpose(1, 0, 2).astype(BF16)
    return full.reshape(N_DEV, -1, full.shape[1]).astype(BF16)


def _ffn_bwd(dy, dres, saved, mods, g, w13, w2, base, tag, prev=None, acc=None, exchange_of=None):
    h, hn, p, u, s, _ = saved
    ff = w2.shape[0]
    acc = acc or (None, None, None)
    da, db = mm_nt(dy, w2, tag + "_ds", dswiglu=(u, p))
    dw2 = mm_tn(s, dy, tag + "_dw2", acc=acc[2])
    dwa = mm_tn(hn, da, tag + "_dw13a", acc=acc[0])
    dwb = mm_tn(hn, db, tag + "_dw13b", acc=acc[1])
    exchange = exchange_of(dwa, dwb, dw2) if exchange_of else None
    outs = mm_nt_norm([da, db], w13, [0, ff], h, g, mods, base + 1, dres, tag + "_dhn", prev=prev, exchange=exchange)
    dh, dshift, dscale, dg = outs[:4]
    nprev = 2 if prev else 0
    return (dh, (dwa, dwb, dw2), dg, {base: dshift, base + 1: dscale}, tuple(outs[4:4 + nprev]),
            list(outs[4 + nprev:]))


def _mod_rows(parts):
    zero = jnp.zeros((1, D), F32)
    return jnp.concatenate([parts.get(k, zero) for k in range(N_MOD)], axis=0)


def local_step(x, ctx, ml, mc, wts, target, shards=None):
    wts = dict(wts)
    ng = wts["norm_g"]
    gvec = lambda l, k: ng[l, k][None, :]
    pool_w16 = wts["pool_w"].astype(BF16)
    grads = {}

    def gather(pieces):
        return Exchange("gather", [shards[p] for p in pieces]) if shards else None

    def arrived(pieces, results):
        for p, g in zip(pieces, results):
            wts[p] = unpack_piece(p, g)

    def ffn_grads(lf, f):
        grads["w13_" + lf] = jnp.concatenate([f[0], f[1]], axis=1)
        grads["w2_" + lf] = f[2]

    if shards:
        xh, got = normmod(x, gvec(0, 0), ml[0], 0, 1, "l0f1_norm", gather(GATHER_FIRST))
        arrived(GATHER_FIRST, got)
    else:
        xh = normmod(x, gvec(0, 0), ml[0], 0, 1, "l0f1_norm")
    ch = normmod(ctx, gvec(0, 0), mc[0], 0, 1, "l0f1c_norm")
    def w2_after_up(got):
        arrived(GATHER_IN_FFN, got)
        return wts["w2_00"]

    x1, xn, sv1, _ = _ffn_fwd(x, xh, ml[0], wts["w13_00"], w2_after_up, 0, "l0f1",
                              nxt=(gvec(0, 1), ml[0], 3, 4), exchange=gather(GATHER_IN_FFN))
    c1, cn, sv1c, _ = _ffn_fwd(ctx, ch, mc[0], wts["w13_00"], wts["w2_00"], 0, "l0f1c", nxt=(gvec(0, 1), mc[0], 3, 4))
    qkvu = mm_nn([xn], wts["ewi"], [0], "l0mix_in")
    qkvu_c = mm_nn([cn], wts["ewi"], [0], "l0mix_in_c")
    tab = bias_table(wts["rpb"])
    att, got = attn_fwd(qkvu, qkvu_c, tab, "l0_attn", gather(GATHER_IN_ATTN))
    arrived(GATHER_IN_ATTN, got)
    pool = pool_fwd(qkvu, pool_w16, wts["pool_scale"], "l0_pool")
    x2, ymix, xh = mm_nn([att, pool], wts["ewo"], [0, NA_W], "l0mix_out", res=(x1, ml[0], 5, 1.0),
                         nxt=(gvec(0, 2), ml[0], 6, 7))
    x3, xh, sv2, _ = _ffn_fwd(x2, xh, ml[0], wts["w13_01"], wts["w2_01"], 6, "l0f2", nxt=(gvec(1, 0), ml[1], 0, 1))

    x4, xn1, sv3, _ = _ffn_fwd(x3, xh, ml[1], wts["w13_10"], wts["w2_10"], 0, "l1f1", nxt=(gvec(1, 1), ml[1], 3, 4))
    proj = mm_nn([xn1], wts["cwi"], [0], "l1mix_in")
    gm = conv_fwd(proj, wts["conv_w"], "l1_conv")
    x5, ycv, xh = mm_nn([gm], wts["cwo"], [0], "l1mix_out", res=(x4, ml[1], 5, 1.0), nxt=(gvec(1, 2), ml[1], 6, 7))
    x6, _, sv4, _ = _ffn_fwd(x5, xh, ml[1], wts["w13_11"], wts["w2_11"], 6, "l1f2")

    dx6, loss, dgf, dy, dgate = loss_head(x6, wts["final_g"][None, :], target, (sv4[5], ml[1], 8, 0.5), "loss_head")
    dm1 = {8: dgate}
    dx5, dwf4, dg12, dm_f4, (dy, dgate), _ = _ffn_bwd(dy, dx6, sv4, ml[1], gvec(1, 2), wts["w13_11"], wts["w2_11"], 6,
                                                      "l1f2", prev=(ycv, ml[1], 5, 1.0))
    ffn_grads("11", dwf4)
    dm1.update({5: dgate, **dm_f4})
    dgm = mm_nt(dy, wts["cwo"], "l1mix_dgm")
    grads["cwo"] = mm_tn(gm, dy, "l1mix_dwo")
    dbg, dcg, dxin, dconv_w = conv_bwd(proj, dgm, wts["conv_w"], "l1_conv_bwd")
    grads["cwi"] = mm_tn_multi(xn1, [dbg, dcg, dxin], "l1mix_dwi")
    dx4, dsh, dsc, dg11, dy, dgate = mm_nt_norm([dbg, dcg, dxin], wts["cwi"], [0, D, 2 * D], x4, gvec(1, 1), ml[1], 4,
                                                dx5, "l1mix_dxn", prev=(sv3[5], ml[1], 2, 0.5))
    dm1.update({3: dsh, 4: dsc, 2: dgate})
    dx3, dwf3, dg10, dm_f3, (dy, dgate), _ = _ffn_bwd(dy, dx4, sv3, ml[1], gvec(1, 0), wts["w13_10"], wts["w2_10"], 0,
                                                      "l1f1", prev=(sv2[5], ml[0], 8, 0.5))
    ffn_grads("10", dwf3)
    dm1.update(dm_f3)
    dm0 = {8: dgate}

    dx2, dwf2, dg02, dm_f2, (dy, dgate), _ = _ffn_bwd(dy, dx3, sv2, ml[0], gvec(0, 2), wts["w13_01"], wts["w2_01"], 6,
                                                      "l0f2", prev=(ymix, ml[0], 5, 1.0))
    ffn_grads("01", dwf2)
    dm0.update({5: dgate, **dm_f2})
    dmix = mm_nt(dy, wts["ewo"], "l0mix_dmix")
    grads["ewo"] = jnp.concatenate([mm_tn(att, dy, "l0mix_dwo_att"), mm_tn(pool, dy, "l0mix_dwo_pool")], axis=0)
    scatter = Exchange("scatter", [block_piece(p, grads.pop(p)) for p in SCATTER_IN_ATTN]) if shards else None
    (dq, dk, dv, dkc, dvc, dtab), got = attn_bwd(qkvu, qkvu_c, tab, dmix, "l0_attn_bwd", scatter)
    recv = dict(zip(SCATTER_IN_ATTN, got))
    du, dpool_w, dpool_scale = pool_bwd(qkvu, dmix, pool_w16, wts["pool_scale"], "l0_pool_bwd")
    drpb = bias_table_bwd(dtab)
    dk16, dv16, dkc16, dvc16 = dk, dv, dkc.astype(BF16), dvc.astype(BF16)
    no_grad = jnp.zeros_like(dkc16)
    dewi_c = mm_tn_multi(cn, [no_grad, dkc16, dvc16, no_grad], "l0mix_dwi_c")
    grads["ewi"] = mm_tn_multi(xn, [dq, dk16, dv16, du], "l0mix_dwi", acc=dewi_c)
    dx1, dsh, dsc, dg01, dy, dgate = mm_nt_norm([dq, dk16, dv16, du], wts["ewi"], [0, NA_W, 2 * NA_W, 3 * NA_W], x1,
                                                gvec(0, 1), ml[0], 4, dx2, "l0mix_dxn", prev=(sv1[5], ml[0], 2, 0.5))
    dm0.update({3: dsh, 4: dsc, 2: dgate})
    dc1, dsh_c, dsc_c, dg01c, dy_c, dgate_c = mm_nt_norm([dkc16, dvc16], wts["ewi"], [NA_W, 2 * NA_W], c1, gvec(0, 1),
                                                         mc[0], 4, None, "l0mix_dxn_c", prev=(sv1c[5], mc[0], 2, 0.5))
    _, dwf1c, dg00c, dm_f1c, _, _ = _ffn_bwd(dy_c, dc1, sv1c, mc[0], gvec(0, 0), wts["w13_00"], wts["w2_00"], 0, "l0f1c")
    dmc0 = {3: dsh_c, 4: dsc_c, 2: dgate_c, **dm_f1c}

    def last_scatter(dwa, dwb, dw2):
        ffn_grads("00", (dwa, dwb, dw2))
        return Exchange("scatter", [block_piece(p, grads.pop(p)) for p in SCATTER_LAST]) if shards else None

    dx0, _, dg00, dm_f1, _, got = _ffn_bwd(dy, dx1, sv1, ml[0], gvec(0, 0), wts["w13_00"], wts["w2_00"], 0, "l0f1",
                                           acc=dwf1c, exchange_of=last_scatter)
    recv.update(zip(SCATTER_LAST, got))
    dm0.update(dm_f1)

    return {
        "loss": loss, "grad_x": dx0,
        "dml": jnp.stack([_mod_rows(dm0), _mod_rows(dm1)]),
        "dmc": jnp.stack([_mod_rows(dmc0), jnp.zeros((N_MOD, D), F32)]),
        "norm_g": jnp.concatenate([dg00 + dg00c, dg01 + dg01c, dg02, dg10, dg11, dg12], axis=0),
        "grads": grads, "recv": recv,
        "rpb": drpb, "pool_w": dpool_w, "pool_scale": dpool_scale, "conv_w": dconv_w, "final_g": dgf,
    }


def _rows_of(v, nrows):
    flat = v.reshape(-1)
    return jnp.pad(flat, (0, nrows * D - flat.shape[0])).reshape(nrows, D)


def kernel(x, c, ctx, c_ctx, mod_w, mod_b, norm_g, ffn_w13, ffn_w2, even_w_in, even_w_out, na_rpb, pool_w, pool_scale, conv_w_in, conv_w, conv_w_out, final_g, loss_target, m_c_ctx, m_mod_w, m_mod_b, m_norm_g, m_ffn_w13, m_ffn_w2, m_even_w_in, m_even_w_out, m_na_rpb, m_pool_w, m_pool_scale, m_conv_w_in, m_conv_w, m_conv_w_out, m_final_g, v_c_ctx, v_mod_w, v_mod_b, v_norm_g, v_ffn_w13, v_ffn_w2, v_even_w_in, v_even_w_out, v_na_rpb, v_pool_w, v_pool_scale, v_conv_w_in, v_conv_w, v_conv_w_out, v_final_g):
    me = 4 * lax.axis_index("x") + 2 * lax.axis_index("y") + lax.axis_index("c")
    ff = ffn_w2.shape[2] * N_DEV
    w13c = ffn_w13.shape[3]
    w2r = ffn_w2.shape[2]
    mcols = mod_w.shape[2]
    gcols = norm_g.shape[2]

    big = {"w13": ffn_w13.reshape(4 * D, w13c), "w2": ffn_w2.reshape(4 * w2r, D), "ewi": even_w_in[0],
           "ewo": even_w_out[0], "cwi": conv_w_in[0], "cwo": conv_w_out[0]}
    names = list(big)
    shards = {"ewi": even_w_in[0].astype(BF16), "ewo": even_w_out[0].astype(BF16),
              "cwi": conv_w_in[0].astype(BF16), "cwo": conv_w_out[0].astype(BF16)}
    for l in range(2):
        for f in range(2):
            shards["w13_%d%d" % (l, f)] = ffn_w13[l, f].astype(BF16)
            shards["w2_%d%d" % (l, f)] = ffn_w2[l, f].astype(BF16)
    wts = {}

    c_all = small_allgather(jnp.pad(c, ((0, 7), (0, 0))), "cond_allgather")[:, 0, :]
    cm = jnp.concatenate([c_all, c_ctx[None, :], jnp.zeros((7, D), F32)], axis=0)
    mod_b_cols = lax.dynamic_slice(mod_b, (0, me * mcols), (2, mcols))[:, None, :]
    m_cols = adaln_fwd(cm, mod_w, mod_b_cols, "adaln_fwd")
    m_all = small_allgather(m_cols.reshape(32, mcols), "mod_allgather")
    m_full = m_all.reshape(N_DEV, 2, 16, mcols).transpose(1, 2, 0, 3).reshape(2, 16, N_MOD * D)
    ml = lax.dynamic_slice(m_full, (0, me, 0), (2, 1, N_MOD * D)).reshape(2, N_MOD, D)
    mc = m_full[:, 8].reshape(2, N_MOD, D)

    full_norm_g = small_allgather(_rows_of(norm_g, 8), "norm_g_allgather")[:, 0, :2 * 3 * gcols]
    full_norm_g = full_norm_g.reshape(N_DEV, 2, 3, gcols).transpose(1, 2, 0, 3).reshape(2, 3, D)
    full_conv_w = small_allgather(_rows_of(conv_w, 8), "conv_w_allgather")[:, 0, :3 * gcols]
    full_conv_w = full_conv_w.reshape(N_DEV, 3, gcols).transpose(1, 0, 2).reshape(3, D)
    wts.update(norm_g=full_norm_g, conv_w=full_conv_w, rpb=na_rpb[0], pool_w=pool_w[0], pool_scale=pool_scale,
               final_g=final_g)
    out = local_step(x[0], ctx[0], ml, mc, wts, loss_target[0], shards)

    dm_pack = jnp.concatenate([out["dml"].reshape(2, N_MOD * D), out["dmc"].reshape(2, N_MOD * D),
                               jnp.zeros((4, N_MOD * D), F32)], axis=0)
    dm_t = small_allgather(dm_pack, "dmod_allgather").transpose(1, 0, 2)[:4]
    dm_cols = lax.dynamic_slice(dm_t, (0, 0, me * mcols), (4, N_DEV, mcols))
    g_mod_w, pc = adaln_bwd(cm.T, mod_w, dm_cols, "adaln_bwd")
    g_mod_b = mod_b_grad(dm_t, "mod_b_grad")[:2]

    pack = jnp.concatenate([_rows_of(t, 8) for t in (
        out["norm_g"], out["conv_w"], out["final_g"], pc[0, :1] + pc[1, :1], out["pool_scale"], out["loss"],
        out["rpb"])] + [_rows_of(out["pool_w"], 64)], axis=0)
    small = sum_devices(small_allgather(pack, "small_grads_allgather"), "small_grads_sum")
    g_norm_g = lax.dynamic_slice(small[0:6].reshape(2, 3, D), (0, 0, me * gcols), (2, 3, gcols))
    g_conv_w = lax.dynamic_slice(small[8:11], (0, me * gcols), (3, gcols))[None]
    g_final_g = small[16]
    sg = _sigmoid(c_ctx)
    g_c_ctx = small[24] * (sg * (1.0 + c_ctx * (1.0 - sg)))
    g_pool_scale = small[32:33, :POOL_W]
    loss = small[40, 0]
    g_rpb = small[48:52].reshape(-1)[:na_rpb.size].reshape(na_rpb.shape)
    g_pool_w = small[56:120].reshape(pool_w.shape)

    pieces = out["recv"]
    lf = ("00", "01", "10", "11")
    recv = {"w13": jnp.concatenate([pieces["w13_" + t] for t in lf], axis=1),
            "w2": jnp.concatenate([pieces["w2_" + t] for t in lf], axis=1),
            "ewi": pieces["ewi"], "ewo": pieces["ewo"], "cwi": pieces["cwi"], "cwo": pieces["cwo"]}

    moments = {"w13": (m_ffn_w13, v_ffn_w13), "w2": (m_ffn_w2, v_ffn_w2), "ewi": (m_even_w_in, v_even_w_in),
               "ewo": (m_even_w_out, v_even_w_out), "cwi": (m_conv_w_in, v_conv_w_in),
               "cwo": (m_conv_w_out, v_conv_w_out)}
    orig = {"w13": ffn_w13, "w2": ffn_w2, "ewi": even_w_in, "ewo": even_w_out, "cwi": conv_w_in, "cwo": conv_w_out}
    upd = {}
    for k in names:
        shp2 = big[k].shape
        res = adamw(big[k], moments[k][0].reshape(shp2), moments[k][1].reshape(shp2), "adamw_" + k, recv=recv[k])
        upd[k] = [r.reshape(orig[k].shape) for r in res]
    shp2 = (2 * D, mcols)
    upd["mod_w"] = [r.reshape(mod_w.shape) for r in adamw(mod_w.reshape(shp2), m_mod_w.reshape(shp2),
                                                          v_mod_w.reshape(shp2), "adamw_mod_w",
                                                          g=g_mod_w.reshape(shp2))]

    smalls = [("c_ctx", c_ctx, m_c_ctx, v_c_ctx, g_c_ctx, 8), ("mod_b", mod_b, m_mod_b, v_mod_b, g_mod_b, 24),
              ("norm_g", norm_g, m_norm_g, v_norm_g, g_norm_g, 8), ("rpb", na_rpb, m_na_rpb, v_na_rpb, g_rpb, 8),
              ("pool_w", pool_w, m_pool_w, v_pool_w, g_pool_w, 64),
              ("pool_scale", pool_scale, m_pool_scale, v_pool_scale, g_pool_scale, 8),
              ("conv_w", conv_w, m_conv_w, v_conv_w, g_conv_w, 8), ("final_g", final_g, m_final_g, v_final_g, g_final_g, 8)]
    packed = [jnp.concatenate([_rows_of(s[col], s[5]) for s in smalls], axis=0) for col in (1, 2, 3, 4)]
    res = adamw(packed[0], packed[1], packed[2], "adamw_small", g=packed[3])
    row = 0
    for name, w, _, _, _, nrows in smalls:
        upd[name] = [r[row:row + nrows].reshape(-1)[:w.size].reshape(w.shape) for r in res]
        row += nrows

    order = ["c_ctx", "mod_w", "mod_b", "norm_g", "w13", "w2", "ewi", "ewo", "rpb", "pool_w", "pool_scale", "cwi",
             "conv_w", "cwo", "final_g"]
    grad_x = out["grad_x"][None]
    return (loss, grad_x, *[upd[k][0] for k in order], *[upd[k][1] for k in order], *[upd[k][2] for k in order],
            *[upd[k][3] for k in order])
```

```python
import functools

import numpy as np
import jax
import jax.numpy as jnp
from jax import lax
from jax.experimental import pallas as pl
from jax.experimental.pallas import tpu as pltpu

D = 1024
FF = 2816
SEQ = 16384
CTX = 256
GRID_W = 64
N_MOD = 9
HEADS = 8
HEAD_DIM = 64
NA_W = 512
POOL_W = 512
POOL_G = 128
POOL_WINDOWS = (2, 4, 8, 16)
KH = 8
KW = 16
RMS_EPS = 1e-6
NEG_INF = -1e30
N_DEV = 8

ADAM_LR = 0.001
ADAM_B1 = 0.9
ADAM_B2 = 0.999
ADAM_EPS = 1e-08
ADAM_WD = 0.01
ADAM_STEP = 10

VMEM_LIMIT = 52 * 1024 * 1024
HALO = 16
QROWS = 8
WROWS = 24

BF16 = jnp.bfloat16
F32 = jnp.float32
MESH_ID = pl.DeviceIdType.MESH
HI = lax.Precision.HIGHEST

NT_DIMS = (((1,), (1,)), ((), ()))
TN_DIMS = (((0,), (0,)), ((), ()))


def _tile(n, cands):
    for c in cands:
        if n % c == 0:
            return c
    return n


def _params(sem):
    return pltpu.CompilerParams(dimension_semantics=sem, vmem_limit_bytes=VMEM_LIMIT)


def _dot(a, b):
    return jnp.dot(a, b, preferred_element_type=F32)


def _dot_nt(a, b):
    return lax.dot_general(a, b, NT_DIMS, preferred_element_type=F32)


def _dot_tn(a, b):
    return lax.dot_general(a, b, TN_DIMS, preferred_element_type=F32)


def _sigmoid(x):
    return 1.0 / (1.0 + jnp.exp(-x))


def normmod(h, g, mods, i_shift, i_scale, name, exchange=None):
    n = h.shape[0]
    te = _tile(n, (512, 256))
    nt = n // te
    nx = exchange.n if exchange else 0

    def body(*refs):
        h_ref, g_ref, m_ref = refs[:3]
        x_in = refs[3:3 + nx]
        o_ref = refs[3 + nx]
        x_out = refs[4 + nx:4 + 2 * nx]
        x_sems = refs[4 + 2 * nx:]
        if exchange:
            @pl.when(pl.program_id(0) == 0)
            def _():
                exchange.start(x_in, x_out, x_sems)

        x = h_ref[...]
        r = lax.rsqrt(jnp.mean(x * x, axis=-1, keepdims=True) + RMS_EPS)
        y = x * r * g_ref[...]
        o_ref[...] = (y * (1.0 + m_ref[i_scale:i_scale + 1, :]) + m_ref[i_shift:i_shift + 1, :]).astype(BF16)

        if exchange:
            @pl.when(pl.program_id(0) == nt - 1)
            def _():
                exchange.finish(x_in, x_out, x_sems)

    hbm = pl.BlockSpec(memory_space=pltpu.HBM)
    res = pl.pallas_call(
        body, name=name, grid=(nt,),
        in_specs=[pl.BlockSpec((te, D), lambda i: (i, 0)),
                  pl.BlockSpec((1, D), lambda i: (0, 0)),
                  pl.BlockSpec((N_MOD, D), lambda i: (0, 0))] + [hbm] * nx,
        out_specs=[pl.BlockSpec((te, D), lambda i: (i, 0))] + [hbm] * nx,
        out_shape=[jax.ShapeDtypeStruct((n, D), BF16)] + (exchange.out_shapes if exchange else []),
        scratch_shapes=exchange.scratch if exchange else [],
        compiler_params=_params(("arbitrary",) if exchange else ("parallel",)),
    )(h, g, mods, *(exchange.arrays if exchange else []))
    return (res[0], list(res[1:])) if exchange else res[0]


def loss_head(x, g, target, prev, name):
    n = x.shape[0]
    te = _tile(n, (256,))
    i_gate, coef = prev[2], prev[3]

    def body(x_ref, g_ref, t_ref, y_ref, m_ref, dx_ref, loss_ref, dg_ref, dy_ref, dgate_ref):
        @pl.when(pl.program_id(0) == 0)
        def _():
            loss_ref[...] = jnp.zeros_like(loss_ref)
            dg_ref[...] = jnp.zeros_like(dg_ref)
            dgate_ref[...] = jnp.zeros_like(dgate_ref)

        xv = x_ref[...]
        gv = g_ref[...]
        r = lax.rsqrt(jnp.mean(xv * xv, axis=-1, keepdims=True) + RMS_EPS)
        xhat = xv * r
        e = xhat * gv - t_ref[...]
        per_tok = jnp.mean(e * e, axis=-1, keepdims=True)
        loss_ref[...] += 0.5 * jnp.sum(per_tok, axis=0, keepdims=True)
        dy = e * (1.0 / D)
        dg_ref[...] += jnp.sum(dy * xhat, axis=0, keepdims=True)
        dxhat = dy * gv
        dx = r * (dxhat - xhat * jnp.mean(dxhat * xhat, axis=-1, keepdims=True))
        dx_ref[...] = dx
        dy_ref[...] = (dx * (coef * m_ref[i_gate:i_gate + 1, :])).astype(BF16)
        dgate_ref[...] += coef * jnp.sum(dx * y_ref[...].astype(F32), axis=0, keepdims=True)

    row = pl.BlockSpec((te, D), lambda i: (i, 0))
    vec = pl.BlockSpec((1, D), lambda i: (0, 0))
    return pl.pallas_call(
        body, name=name, grid=(n // te,),
        in_specs=[row, vec, row, row, pl.BlockSpec((N_MOD, D), lambda i: (0, 0))],
        out_specs=[row, pl.BlockSpec((1, 128), lambda i: (0, 0)), vec, row, vec],
        out_shape=[jax.ShapeDtypeStruct((n, D), F32), jax.ShapeDtypeStruct((1, 128), F32),
                   jax.ShapeDtypeStruct((1, D), F32), jax.ShapeDtypeStruct((n, D), BF16),
                   jax.ShapeDtypeStruct((1, D), F32)],
        compiler_params=_params(("arbitrary",)),
    )(x, g, target, prev[0], prev[1])


def ffn_up(hn, w13, name, exchange=None):
    n = hn.shape[0]
    ff = w13.shape[1] // 2
    tm = _tile(n, (512, 256))
    tn = _tile(ff, (1408, 512, 256, 128))
    nj = ff // tn
    ni = n // tm
    nx = exchange.n if exchange else 0

    def body(*refs):
        h_ref, wa_ref, wb_ref = refs[:3]
        x_in = refs[3:3 + nx]
        p_ref, u_ref, s_ref = refs[3 + nx:6 + nx]
        x_out = refs[6 + nx:6 + 2 * nx]
        x_sems = refs[6 + 2 * nx:]
        if exchange:
            @pl.when((pl.program_id(0) == 0) & (pl.program_id(1) == 0))
            def _():
                exchange.start(x_in, x_out, x_sems)

        hv = h_ref[...]
        a = _dot(hv, wa_ref[...])
        b = _dot(hv, wb_ref[...])
        sig = _sigmoid(a)
        p = a * sig
        p_ref[...] = p.astype(BF16)
        u_ref[...] = (b * (sig * (1.0 + a * (1.0 - sig)))).astype(BF16)
        s_ref[...] = (p * b).astype(BF16)

        if exchange:
            @pl.when((pl.program_id(0) == nj - 1) & (pl.program_id(1) == ni - 1))
            def _():
                exchange.finish(x_in, x_out, x_sems)

    out = pl.BlockSpec((tm, tn), lambda j, i: (i, j))
    hbm = pl.BlockSpec(memory_space=pltpu.HBM)
    sem = ("arbitrary", "arbitrary") if exchange else ("parallel", "parallel")
    res = pl.pallas_call(
        body, name=name, grid=(nj, ni),
        in_specs=[pl.BlockSpec((tm, D), lambda j, i: (i, 0)),
                  pl.BlockSpec((D, tn), lambda j, i: (0, j)),
                  pl.BlockSpec((D, tn), lambda j, i: (0, j + nj))] + [hbm] * nx,
        out_specs=[out, out, out] + [hbm] * nx,
        out_shape=[jax.ShapeDtypeStruct((n, ff), BF16)] * 3 + (exchange.out_shapes if exchange else []),
        scratch_shapes=exchange.scratch if exchange else [],
        compiler_params=_params(sem),
    )(hn, w13, w13, *(exchange.arrays if exchange else []))
    return res[:3], list(res[3:])


def mm_nn(a_list, w, row_offs, name, out_dtype=BF16, res=None, nxt=None):
    n = a_list[0].shape[0]
    nout = w.shape[1]
    ks = [a.shape[1] for a in a_list]
    tm = _tile(n, (512, 256) if res is not None else (1024, 512, 256))
    tn = _tile(nout, (1024, 512, 256, 128))
    na = len(a_list)
    assert nxt is None or (res is not None and tn == D)

    def body(*refs):
        a_refs = refs[:na]
        w_refs = refs[na:2 * na]
        acc = _dot(a_refs[0][...], w_refs[0][...])
        for k in range(1, na):
            acc += _dot(a_refs[k][...], w_refs[k][...])
        if res is None:
            refs[2 * na][...] = acc.astype(out_dtype)
        else:
            h_ref, m_ref = refs[2 * na:2 * na + 2]
            i_gate, coef = res[2], res[3]
            h_new = h_ref[...] + (coef * m_ref[i_gate:i_gate + 1, :]) * acc
            if nxt is None:
                hn_ref, y_ref = refs[2 * na + 2:]
            else:
                g2_ref, m2_ref, hn_ref, y_ref, nx_ref = refs[2 * na + 2:]
                r = lax.rsqrt(jnp.mean(h_new * h_new, axis=-1, keepdims=True) + RMS_EPS)
                nx_ref[...] = ((h_new * r * g2_ref[...]) * (1.0 + m2_ref[nxt[3]:nxt[3] + 1, :])
                               + m2_ref[nxt[2]:nxt[2] + 1, :]).astype(BF16)
            hn_ref[...] = h_new
            y_ref[...] = acc.astype(BF16)

    in_specs = [pl.BlockSpec((tm, k), lambda j, i: (i, 0)) for k in ks]
    for k, off in zip(ks, row_offs):
        in_specs.append(pl.BlockSpec((k, tn), functools.partial(lambda j, i, ob: (ob, j), ob=off // k)))
    args = list(a_list) + [w] * na
    out = pl.BlockSpec((tm, tn), lambda j, i: (i, j))
    if res is None:
        out_specs = out
        out_shape = jax.ShapeDtypeStruct((n, nout), out_dtype)
    else:
        in_specs += [out, pl.BlockSpec((N_MOD, tn), lambda j, i: (0, j))]
        args += [res[0], res[1]]
        out_specs = [out, out]
        out_shape = [jax.ShapeDtypeStruct((n, nout), F32), jax.ShapeDtypeStruct((n, nout), BF16)]
        if nxt is not None:
            in_specs += [pl.BlockSpec((1, D), lambda j, i: (0, 0)), pl.BlockSpec((N_MOD, D), lambda j, i: (0, 0))]
            args += [nxt[0], nxt[1]]
            out_specs.append(out)
            out_shape.append(jax.ShapeDtypeStruct((n, nout), BF16))
    return pl.pallas_call(
        body, name=name, grid=(nout // tn, n // tm),
        in_specs=in_specs, out_specs=out_specs, out_shape=out_shape,
        compiler_params=_params(("parallel", "parallel")),
    )(*args)


def mm_nt(g, w, name, dswiglu=None):
    n, kg = g.shape
    nout = w.shape[0]
    tm = _tile(n, (512, 256) if dswiglu is not None else (1024, 512, 256))
    tn = _tile(nout, (1408, 1024, 512, 256, 128))

    def body(*refs):
        r = _dot_nt(refs[0][...], refs[1][...])
        if dswiglu is None:
            refs[2][...] = r.astype(BF16)
        else:
            u_ref, p_ref, da_ref, db_ref = refs[2:]
            da_ref[...] = (r * u_ref[...].astype(F32)).astype(BF16)
            db_ref[...] = (r * p_ref[...].astype(F32)).astype(BF16)

    out = pl.BlockSpec((tm, tn), lambda j, i: (i, j))
    in_specs = [pl.BlockSpec((tm, kg), lambda j, i: (i, 0)), pl.BlockSpec((tn, kg), lambda j, i: (j, 0))]
    args = [g, w]
    if dswiglu is None:
        out_specs = out
        out_shape = jax.ShapeDtypeStruct((n, nout), BF16)
    else:
        in_specs += [out, out]
        args += list(dswiglu)
        out_specs = [out, out]
        out_shape = [jax.ShapeDtypeStruct((n, nout), BF16)] * 2
    return pl.pallas_call(
        body, name=name, grid=(nout // tn, n // tm),
        in_specs=in_specs, out_specs=out_specs, out_shape=out_shape,
        compiler_params=_params(("parallel", "parallel")),
    )(*args)


def mm_nt_norm(g_list, w, col_offs, h, g, mods, i_scale, dres, name, prev=None, exchange=None):
    n = h.shape[0]
    kg = g_list[0].shape[1]
    tm = _tile(n, (512, 256))
    tk = _tile(kg, (1408, 1024, 512, 256, 128))
    ng = len(g_list)
    nk = kg // tk
    ni = n // tm
    has_res = dres is not None
    nx = exchange.n if exchange else 0

    def body(*refs):
        g_refs = refs[:ng]
        w_refs = refs[ng:2 * ng]
        pos = 2 * ng
        h_ref, gv_ref, m_ref = refs[pos:pos + 3]
        pos += 3
        if has_res:
            dres_ref = refs[pos]
            pos += 1
        if prev is not None:
            y_ref, mp_ref = refs[pos:pos + 2]
            pos += 2
        x_in = refs[pos:pos + nx]
        pos += nx
        dh_ref, dshift_ref, dscale_ref, dg_ref = refs[pos:pos + 4]
        pos += 4
        if prev is not None:
            dy_ref, dgate_ref = refs[pos:pos + 2]
            pos += 2
        x_out = refs[pos:pos + nx]
        pos += nx
        acc_ref = refs[pos]
        x_sems = refs[pos + 1:]
        i = pl.program_id(0)
        k = pl.program_id(1)

        @pl.when((i == 0) & (k == 0))
        def _():
            if exchange:
                exchange.start(x_in, x_out, x_sems)
            dshift_ref[...] = jnp.zeros_like(dshift_ref)
            dscale_ref[...] = jnp.zeros_like(dscale_ref)
            dg_ref[...] = jnp.zeros_like(dg_ref)
            if prev is not None:
                dgate_ref[...] = jnp.zeros_like(dgate_ref)

        def dots():
            acc = _dot_nt(g_refs[0][...], w_refs[0][...])
            for q in range(1, ng):
                acc += _dot_nt(g_refs[q][...], w_refs[q][...])
            return acc

        if nk > 1:
            @pl.when(k == 0)
            def _():
                acc_ref[...] = dots()

        if nk > 2:
            @pl.when((k > 0) & (k < nk - 1))
            def _():
                acc_ref[...] += dots()

        @pl.when(k == nk - 1)
        def _():
            d = dots() + acc_ref[...] if nk > 1 else dots()
            x = h_ref[...]
            gv = gv_ref[...]
            r = lax.rsqrt(jnp.mean(x * x, axis=-1, keepdims=True) + RMS_EPS)
            xhat = x * r
            one_scale = 1.0 + m_ref[i_scale:i_scale + 1, :]
            t = d * xhat
            tsum = jnp.sum(t, axis=0, keepdims=True)
            dshift_ref[...] += jnp.sum(d, axis=0, keepdims=True)
            dscale_ref[...] += gv * tsum
            dg_ref[...] += one_scale * tsum
            cvec = one_scale * gv
            dh = r * (d * cvec - xhat * jnp.mean(t * cvec, axis=-1, keepdims=True))
            if has_res:
                dh = dh + dres_ref[...]
            dh_ref[...] = dh
            if prev is not None:
                i_gate, coef = prev[2], prev[3]
                dy_ref[...] = (dh * (coef * mp_ref[i_gate:i_gate + 1, :])).astype(BF16)
                dgate_ref[...] += coef * jnp.sum(dh * y_ref[...].astype(F32), axis=0, keepdims=True)

        if exchange:
            @pl.when((i == ni - 1) & (k == nk - 1))
            def _():
                exchange.finish(x_in, x_out, x_sems)

    row = pl.BlockSpec((tm, D), lambda i, k: (i, 0))
    vec = pl.BlockSpec((1, D), lambda i, k: (0, 0))
    modspec = pl.BlockSpec((N_MOD, D), lambda i, k: (0, 0))
    in_specs = [pl.BlockSpec((tm, tk), lambda i, k: (i, k)) for _ in g_list]
    for off in col_offs:
        in_specs.append(pl.BlockSpec((D, tk), functools.partial(lambda i, k, ob: (0, ob + k), ob=off // tk)))
    in_specs += [row, vec, modspec]
    args = list(g_list) + [w] * ng + [h, g, mods]
    out_specs = [row, vec, vec, vec]
    out_shape = [jax.ShapeDtypeStruct((n, D), F32)] + [jax.ShapeDtypeStruct((1, D), F32)] * 3
    if has_res:
        in_specs.append(row)
        args.append(dres)
    if prev is not None:
        in_specs += [row, modspec]
        args += [prev[0], prev[1]]
        out_specs += [row, vec]
        out_shape += [jax.ShapeDtypeStruct((n, D), BF16), jax.ShapeDtypeStruct((1, D), F32)]
    scratch = [pltpu.VMEM((tm, D), F32)]
    if exchange:
        hbm = pl.BlockSpec(memory_space=pltpu.HBM)
        in_specs += [hbm] * nx
        args += exchange.arrays
        out_specs += [hbm] * nx
        out_shape += exchange.out_shapes
        scratch += exchange.scratch
    return pl.pallas_call(
        body, name=name, grid=(ni, nk),
        in_specs=in_specs, out_specs=out_specs, out_shape=out_shape, scratch_shapes=scratch,
        compiler_params=_params(("arbitrary", "arbitrary")),
    )(*args)


def mm_tn(a, g, name, acc=None):
    n, ka = a.shape
    ngc = g.shape[1]
    tka = _tile(ka, (1408, 1024, 512, 256, 128))
    tng = _tile(ngc, (1408, 1024, 512, 256, 128))
    tr = _tile(n, (2048, 1024, 512, 256))
    has_acc = acc is not None

    def body(*refs):
        a_ref, g_ref = refs[0], refs[1]
        o_ref = refs[-1]
        r = pl.program_id(2)

        @pl.when(r == 0)
        def _():
            d = _dot_tn(a_ref[...], g_ref[...])
            o_ref[...] = d + refs[2][...] if has_acc else d

        @pl.when(r > 0)
        def _():
            o_ref[...] += _dot_tn(a_ref[...], g_ref[...])

    out = pl.BlockSpec((tka, tng), lambda p, q, r: (p, q))
    in_specs = [pl.BlockSpec((tr, tka), lambda p, q, r: (r, p)),
                pl.BlockSpec((tr, tng), lambda p, q, r: (r, q))]
    args = [a, g]
    if has_acc:
        in_specs.append(out)
        args.append(acc)
    return pl.pallas_call(
        body, name=name, grid=(ka // tka, ngc // tng, n // tr),
        in_specs=in_specs, out_specs=out,
        out_shape=jax.ShapeDtypeStruct((ka, ngc), F32),
        compiler_params=_params(("parallel", "parallel", "arbitrary")),
    )(*args)


def mm_tn_multi(a, g_list, name, acc=None):
    n, ka = a.shape
    ngc = g_list[0].shape[1]
    ng = len(g_list)
    tka = _tile(ka, (512, 256, 128))
    tr = _tile(n, (1024, 512, 256))
    has_acc = acc is not None

    def body(*refs):
        a_ref = refs[0]
        g_refs = refs[1:1 + ng]
        o_ref = refs[-1]
        r = pl.program_id(1)

        @pl.when(r == 0)
        def _():
            av = a_ref[...]
            for q in range(ng):
                cs = slice(q * ngc, (q + 1) * ngc)
                d = _dot_tn(av, g_refs[q][...])
                o_ref[:, cs] = d + refs[1 + ng][:, cs] if has_acc else d

        @pl.when(r > 0)
        def _():
            av = a_ref[...]
            for q in range(ng):
                cs = slice(q * ngc, (q + 1) * ngc)
                o_ref[:, cs] += _dot_tn(av, g_refs[q][...])

    out = pl.BlockSpec((tka, ng * ngc), lambda p, r: (p, 0))
    in_specs = [pl.BlockSpec((tr, tka), lambda p, r: (r, p))]
    in_specs += [pl.BlockSpec((tr, ngc), lambda p, r: (r, 0)) for _ in g_list]
    args = [a] + list(g_list)
    if has_acc:
        in_specs.append(out)
        args.append(acc)
    return pl.pallas_call(
        body, name=name, grid=(ka // tka, n // tr),
        in_specs=in_specs, out_specs=out,
        out_shape=jax.ShapeDtypeStruct((ka, ng * ngc), F32),
        compiler_params=_params(("parallel", "arbitrary")),
    )(*args)


def mm_small(a, b, name, trans_b=False):
    m = a.shape[0]
    nout = b.shape[0] if trans_b else b.shape[1]

    def body(a_ref, b_ref, o_ref):
        if trans_b:
            o_ref[...] = lax.dot_general(a_ref[...], b_ref[...], NT_DIMS, precision=HI, preferred_element_type=F32)
        else:
            o_ref[...] = jnp.dot(a_ref[...], b_ref[...], precision=HI, preferred_element_type=F32)

    return pl.pallas_call(
        body, name=name,
        out_shape=jax.ShapeDtypeStruct((m, nout), F32),
        compiler_params=pltpu.CompilerParams(vmem_limit_bytes=VMEM_LIMIT),
    )(a, b)


def _col_tables():
    col = np.arange(GRID_W)
    start = np.clip(col - KW // 2, 0, GRID_W - KW)
    ok = (col[None, :] >= start[:, None]) & (col[None, :] < start[:, None] + KW)
    ci = np.clip(col[None, :] - col[:, None] + (KW - 1), 0, 2 * KW - 2)
    e = np.zeros((2 * KW - 1, GRID_W, GRID_W), np.float32)
    for c in range(2 * KW - 1):
        e[c] = (ci == c) & ok
    return e.reshape(2 * KW - 1, GRID_W * GRID_W), ok


def bias_table(rpb):
    e, ok = _col_tables()
    e_pad = np.zeros((32, GRID_W * GRID_W), np.float32)
    e_pad[:31] = e
    rp = jnp.pad(rpb.reshape(HEADS * 15, 31), ((0, 0), (0, 1)))
    t = mm_small(rp, jnp.asarray(e_pad), "rpb_expand").reshape(HEADS, 15, GRID_W, GRID_W)
    t = jnp.where(jnp.asarray(ok)[None, None], t, NEG_INF)
    tab = jnp.stack([t[:, v:v + KH] for v in range(8)], axis=0)
    return tab.transpose(0, 1, 3, 2, 4).reshape(TAB_SHAPE)


def bias_table_bwd(dtab):
    e, _ = _col_tables()
    e_pad = np.zeros((128, GRID_W * GRID_W), np.float32)
    e_pad[:31] = e
    d = dtab.reshape(8, HEADS, GRID_W, KH, GRID_W).transpose(0, 1, 3, 2, 4).reshape(8 * HEADS * KH, GRID_W * GRID_W)
    gv = mm_small(d, jnp.asarray(e_pad), "rpb_reduce", trans_b=True)[:, :31]
    gv = gv.reshape(8, HEADS, KH, 31).transpose(0, 2, 1, 3).reshape(8 * KH, HEADS * 31)
    sel = np.zeros((16, 8 * KH), np.float32)
    for v in range(8):
        for j in range(KH):
            sel[v + j, v * KH + j] = 1.0
    gpad = jnp.pad(gv, ((0, 0), (0, 256 - HEADS * 31)))
    out = mm_small(jnp.asarray(sel), gpad, "rpb_fold")[:15, :HEADS * 31]
    return out.reshape(15, HEADS, 31).transpose(1, 0, 2)


def _attn_geometry(seq):
    rows = seq // GRID_W
    nb = rows // QROWS
    return rows, nb


def _stack_heads(t2):
    first = (lax.broadcasted_iota(jnp.int32, (1, 128), 1) // HEAD_DIM) == 0
    zero = jnp.zeros_like(t2)
    return jnp.concatenate([jnp.where(first, t2, zero), jnp.where(first, zero, t2)], axis=0)


def _unstack_heads(t):
    first = (lax.broadcasted_iota(jnp.int32, (1, 128), 1) // HEAD_DIM) == 0
    return jnp.where(first, t[0:GRID_W], t[GRID_W:2 * GRID_W])


TAB_SHAPE = (8, HEADS // 2, 2 * GRID_W, KH * GRID_W)


def attn_fwd(qkvu, qkvu_c, tab, name, exchange=None):
    seq = qkvu.shape[0]
    nctx = qkvu_c.shape[0]
    rows, nb = _attn_geometry(seq)
    qt = QROWS * GRID_W
    wt = WROWS * GRID_W
    scale = HEAD_DIM ** -0.5
    nx = exchange.n if exchange else 0

    def wb0(i):
        return jnp.clip(i - 1, 0, nb - 3)

    def body(*refs):
        q_ref, k0, k1, k2, v0, v1, v2, kc_ref, vc_ref, tab_hbm = refs[:10]
        x_in = refs[10:10 + nx]
        o_ref = refs[10 + nx]
        x_out = refs[11 + nx:11 + 2 * nx]
        kbuf, vbuf, tab_s, sem = refs[11 + 2 * nx:15 + 2 * nx]
        x_sems = refs[15 + 2 * nx:]
        i = pl.program_id(0)

        @pl.when(i == 0)
        def _():
            if exchange:
                exchange.start(x_in, x_out, x_sems)
            cp = pltpu.make_async_copy(tab_hbm, tab_s, sem)
            cp.start()
            cp.wait()

        for t, (kr, vr) in enumerate(((k0, v0), (k1, v1), (k2, v2))):
            kbuf[t * qt:(t + 1) * qt, :] = kr[...]
            vbuf[t * qt:(t + 1) * qt, :] = vr[...]
        base = wb0(i) * QROWS

        def row_body(rl, carry):
            r = i * QROWS + rl
            rs = jnp.clip(r - KH // 2, 0, rows - KH)
            vi = rs - r + (KH - 1)
            off = pl.multiple_of((rs - base) * GRID_W, GRID_W)
            qoff = pl.multiple_of(rl * GRID_W, GRID_W)
            for p in range(HEADS // 2):
                ls = slice(p * 128, (p + 1) * 128)
                qst = _stack_heads(q_ref[pl.ds(qoff, GRID_W), ls])
                k2v = kbuf[pl.ds(off, KH * GRID_W), ls]
                v2v = vbuf[pl.ds(off, KH * GRID_W), ls]
                s_w = _dot_nt(qst, k2v) * scale + tab_s[vi, p]
                s_c = _dot_nt(qst, kc_ref[:, ls]) * scale
                m = jnp.maximum(jnp.max(s_w, axis=-1, keepdims=True), jnp.max(s_c, axis=-1, keepdims=True))
                pw = jnp.exp(s_w - m)
                pc = jnp.exp(s_c - m)
                l = jnp.sum(pw, axis=-1, keepdims=True) + jnp.sum(pc, axis=-1, keepdims=True)
                o = _dot(pw.astype(BF16), v2v) + _dot(pc.astype(BF16), vc_ref[:, ls])
                o_ref[pl.ds(qoff, GRID_W), ls] = _unstack_heads(o * (1.0 / l)).astype(BF16)
            return carry

        lax.fori_loop(0, QROWS, row_body, 0, unroll=2)

        if exchange:
            @pl.when(i == nb - 1)
            def _():
                exchange.finish(x_in, x_out, x_sems)

    blk = lambda col: [pl.BlockSpec((qt, NA_W), functools.partial(lambda i, t, c: (wb0(i) + t, c), t=t, c=col))
                       for t in range(3)]
    hbm = pl.BlockSpec(memory_space=pltpu.HBM)
    res = pl.pallas_call(
        body, name=name, grid=(nb,),
        in_specs=[pl.BlockSpec((qt, NA_W), lambda i: (i, 0))] + blk(1) + blk(2)
                 + [pl.BlockSpec((nctx, NA_W), lambda i: (0, 1)), pl.BlockSpec((nctx, NA_W), lambda i: (0, 2)),
                    pl.BlockSpec(memory_space=pl.ANY)] + [hbm] * nx,
        out_specs=[pl.BlockSpec((qt, NA_W), lambda i: (i, 0))] + [hbm] * nx,
        out_shape=[jax.ShapeDtypeStruct((seq, NA_W), BF16)] + (exchange.out_shapes if exchange else []),
        scratch_shapes=[pltpu.VMEM((wt, NA_W), BF16), pltpu.VMEM((wt, NA_W), BF16),
                        pltpu.VMEM(TAB_SHAPE, F32), pltpu.SemaphoreType.DMA] + (exchange.scratch if exchange else []),
        compiler_params=_params(("arbitrary",)),
    )(qkvu, qkvu, qkvu, qkvu, qkvu, qkvu, qkvu, qkvu_c, qkvu_c, tab, *(exchange.arrays if exchange else []))
    return res[0], list(res[1:])


def attn_bwd(qkvu, qkvu_c, tab, dmix, name, exchange=None):
    seq = qkvu.shape[0]
    nctx = qkvu_c.shape[0]
    rows, nb = _attn_geometry(seq)
    qt = QROWS * GRID_W
    wt = WROWS * GRID_W
    scale = HEAD_DIM ** -0.5
    nx = exchange.n if exchange else 0

    def wb0(i):
        return jnp.clip(i - 1, 0, nb - 3)

    def body(*refs):
        q_ref, k0, k1, k2, v0, v1, v2, kc_ref, vc_ref, do_ref, tab_hbm = refs[:11]
        x_in = refs[11:11 + nx]
        dq_ref, dk_hbm, dv_hbm, dkc_ref, dvc_ref, dtab_hbm = refs[11 + nx:17 + nx]
        x_out = refs[17 + nx:17 + 2 * nx]
        kbuf, vbuf, dkacc, dvacc, tab_s, dtab_s, stage, sem = refs[17 + 2 * nx:25 + 2 * nx]
        x_sems = refs[25 + 2 * nx:]
        i = pl.program_id(0)

        if exchange:
            @pl.when(i == 0)
            def _():
                exchange.start(x_in, x_out, x_sems)

        def flush(src, dst, block, dst_row):
            stage[...] = src[block * qt:(block + 1) * qt, :].astype(BF16)
            cp = pltpu.make_async_copy(stage, dst.at[pl.ds(dst_row, qt)], sem)
            cp.start()
            cp.wait()

        @pl.when(i == 0)
        def _():
            cp = pltpu.make_async_copy(tab_hbm, tab_s, sem)
            cp.start()
            cp.wait()
            dtab_s[...] = jnp.zeros_like(dtab_s)
            dkacc[...] = jnp.zeros_like(dkacc)
            dvacc[...] = jnp.zeros_like(dvacc)
            dkc_ref[...] = jnp.zeros_like(dkc_ref)
            dvc_ref[...] = jnp.zeros_like(dvc_ref)

        @pl.when((i >= 2) & (i <= nb - 2))
        def _():
            dst_row = pl.multiple_of((i - 2) * qt, qt)
            for acc_ref, dst in ((dkacc, dk_hbm), (dvacc, dv_hbm)):
                flush(acc_ref, dst, 0, dst_row)
                acc_ref[0:qt, :] = acc_ref[qt:2 * qt, :]
                acc_ref[qt:2 * qt, :] = acc_ref[2 * qt:3 * qt, :]
                acc_ref[2 * qt:3 * qt, :] = jnp.zeros((qt, NA_W), F32)

        for t, (kr, vr) in enumerate(((k0, v0), (k1, v1), (k2, v2))):
            kbuf[t * qt:(t + 1) * qt, :] = kr[...]
            vbuf[t * qt:(t + 1) * qt, :] = vr[...]
        base = wb0(i) * QROWS

        def row_body(rl, carry):
            r = i * QROWS + rl
            rs = jnp.clip(r - KH // 2, 0, rows - KH)
            vi = rs - r + (KH - 1)
            off = pl.multiple_of((rs - base) * GRID_W, GRID_W)
            qoff = pl.multiple_of(rl * GRID_W, GRID_W)
            for p in range(HEADS // 2):
                ls = slice(p * 128, (p + 1) * 128)
                qst = _stack_heads(q_ref[pl.ds(qoff, GRID_W), ls])
                dost = _stack_heads(do_ref[pl.ds(qoff, GRID_W), ls])
                k2v = kbuf[pl.ds(off, KH * GRID_W), ls]
                v2v = vbuf[pl.ds(off, KH * GRID_W), ls]
                kc2 = kc_ref[:, ls]
                vc2 = vc_ref[:, ls]
                s_w = _dot_nt(qst, k2v) * scale + tab_s[vi, p]
                s_c = _dot_nt(qst, kc2) * scale
                m = jnp.maximum(jnp.max(s_w, axis=-1, keepdims=True), jnp.max(s_c, axis=-1, keepdims=True))
                pw = jnp.exp(s_w - m)
                pc = jnp.exp(s_c - m)
                inv = 1.0 / (jnp.sum(pw, axis=-1, keepdims=True) + jnp.sum(pc, axis=-1, keepdims=True))
                pw = pw * inv
                pc = pc * inv
                dpw = _dot_nt(dost, v2v)
                dpc = _dot_nt(dost, vc2)
                delta = jnp.sum(pw * dpw, axis=-1, keepdims=True) + jnp.sum(pc * dpc, axis=-1, keepdims=True)
                ds_w = pw * (dpw - delta)
                ds_c = pc * (dpc - delta)
                dtab_s[vi, p] += ds_w
                dsw16 = ds_w.astype(BF16)
                dsc16 = ds_c.astype(BF16)
                dq = (_dot(dsw16, k2v) + _dot(dsc16, kc2)) * scale
                dq_ref[pl.ds(qoff, GRID_W), ls] = _unstack_heads(dq).astype(BF16)
                dkacc[pl.ds(off, KH * GRID_W), ls] += _dot_tn(dsw16, qst) * scale
                dvacc[pl.ds(off, KH * GRID_W), ls] += _dot_tn(pw.astype(BF16), dost)
                dkc_ref[:, ls] += _dot_tn(dsc16, qst) * scale
                dvc_ref[:, ls] += _dot_tn(pc.astype(BF16), dost)
            return carry

        lax.fori_loop(0, QROWS, row_body, 0)

        @pl.when(i == nb - 1)
        def _():
            for t in range(3):
                dst_row = (nb - 3 + t) * qt
                flush(dkacc, dk_hbm, t, dst_row)
                flush(dvacc, dv_hbm, t, dst_row)
            cp = pltpu.make_async_copy(dtab_s, dtab_hbm, sem)
            cp.start()
            cp.wait()
            if exchange:
                exchange.finish(x_in, x_out, x_sems)

    blk = lambda col: [pl.BlockSpec((qt, NA_W), functools.partial(lambda i, t, c: (wb0(i) + t, c), t=t, c=col))
                       for t in range(3)]
    any_spec = pl.BlockSpec(memory_space=pl.ANY)
    hbm = pl.BlockSpec(memory_space=pltpu.HBM)
    res = pl.pallas_call(
        body, name=name, grid=(nb,),
        in_specs=[pl.BlockSpec((qt, NA_W), lambda i: (i, 0))] + blk(1) + blk(2)
                 + [pl.BlockSpec((nctx, NA_W), lambda i: (0, 1)), pl.BlockSpec((nctx, NA_W), lambda i: (0, 2)),
                    pl.BlockSpec((qt, NA_W), lambda i: (i, 0)), any_spec] + [hbm] * nx,
        out_specs=[pl.BlockSpec((qt, NA_W), lambda i: (i, 0)), any_spec, any_spec,
                   pl.BlockSpec((nctx, NA_W), lambda i: (0, 0)), pl.BlockSpec((nctx, NA_W), lambda i: (0, 0)),
                   any_spec] + [hbm] * nx,
        out_shape=[jax.ShapeDtypeStruct((seq, NA_W), BF16), jax.ShapeDtypeStruct((seq, NA_W), BF16),
                   jax.ShapeDtypeStruct((seq, NA_W), BF16), jax.ShapeDtypeStruct((nctx, NA_W), F32),
                   jax.ShapeDtypeStruct((nctx, NA_W), F32), jax.ShapeDtypeStruct(TAB_SHAPE, F32)]
                  + (exchange.out_shapes if exchange else []),
        scratch_shapes=[pltpu.VMEM((wt, NA_W), BF16), pltpu.VMEM((wt, NA_W), BF16),
                        pltpu.VMEM((wt, NA_W), F32), pltpu.VMEM((wt, NA_W), F32),
                        pltpu.VMEM(TAB_SHAPE, F32), pltpu.VMEM(TAB_SHAPE, F32), pltpu.VMEM((qt, NA_W), BF16),
                        pltpu.SemaphoreType.DMA]
                       + (exchange.scratch if exchange else []),
        compiler_params=_params(("arbitrary",)),
    )(qkvu, qkvu, qkvu, qkvu, qkvu, qkvu, qkvu, qkvu_c, qkvu_c, dmix, tab, *(exchange.arrays if exchange else []))
    return res[:6], list(res[6:])


def _halo_specs(te, seq, col, width):
    per = te // HALO
    last = seq // HALO - 1
    return [pl.BlockSpec((HALO, width), lambda i: (jnp.maximum(i * per - 1, 0), col)),
            pl.BlockSpec((te, width), lambda i: (i, col)),
            pl.BlockSpec((HALO, width), lambda i: (jnp.minimum((i + 1) * per, last), col))]


def _extended(prev_ref, cur_ref, next_ref, i, te, seq):
    xe = jnp.concatenate([prev_ref[...], cur_ref[...], next_ref[...]], axis=0).astype(F32)
    pos = i * te - HALO + lax.broadcasted_iota(jnp.int32, (te + 2 * HALO, 1), 0)
    return jnp.where((pos >= 0) & (pos < seq), xe, 0.0), pos


def _window_sum(x, levels, n, mirrored):
    first = (n - 1) if mirrored else 1
    acc = x + pltpu.roll(x, first, 0)
    step = 1
    for _ in range(levels - 1):
        acc = pltpu.roll(acc, step, 0) + pltpu.roll(acc, n - step, 0)
        step *= 2
    return acc


def _window_count(pos, w, seq):
    lo = jnp.clip(pos - w // 2, 0, seq)
    hi = jnp.clip(pos - w // 2 + w, 0, seq)
    return jnp.maximum(hi - lo, 1).astype(F32)


def pool_fwd(qkvu, pool_w, pool_scale, name):
    seq = qkvu.shape[0]
    te = _tile(seq, (512, 256))
    n = te + 2 * HALO

    def body(up_ref, uc_ref, un_ref, w_ref, sc_ref, o_ref):
        i = pl.program_id(0)
        xe, pos = _extended(up_ref, uc_ref, un_ref, i, te, seq)
        cnt = pos[HALO:HALO + te]
        for g, w in enumerate(POOL_WINDOWS):
            ls = slice(g * POOL_G, (g + 1) * POOL_G)
            xg = xe[:, ls]
            win = _window_sum(xg, g + 1, n, False)[HALO:HALO + te]
            dlt = win / _window_count(cnt, w, seq) - xg[HALO:HALO + te]
            z = _dot(dlt.astype(BF16), w_ref[g])
            o_ref[:, ls] = (z * sc_ref[:, ls]).astype(BF16)

    return pl.pallas_call(
        body, name=name, grid=(seq // te,),
        in_specs=_halo_specs(te, seq, 3, POOL_W)
                 + [pl.BlockSpec((4, POOL_G, POOL_G), lambda i: (0, 0, 0)), pl.BlockSpec((1, POOL_W), lambda i: (0, 0))],
        out_specs=pl.BlockSpec((te, POOL_W), lambda i: (i, 0)),
        out_shape=jax.ShapeDtypeStruct((seq, POOL_W), BF16),
        compiler_params=_params(("parallel",)),
    )(qkvu, qkvu, qkvu, pool_w, pool_scale)


def pool_bwd(qkvu, dmix, pool_w, pool_scale, name):
    seq = qkvu.shape[0]
    te = _tile(seq, (512, 256))
    n = te + 2 * HALO

    def body(up_ref, uc_ref, un_ref, dp_ref, dc_ref, dn_ref, w_ref, sc_ref, du_ref, dw_ref, dsc_ref):
        i = pl.program_id(0)

        @pl.when(i == 0)
        def _():
            dw_ref[...] = jnp.zeros_like(dw_ref)
            dsc_ref[...] = jnp.zeros_like(dsc_ref)

        xe, pos = _extended(up_ref, uc_ref, un_ref, i, te, seq)
        de, _ = _extended(dp_ref, dc_ref, dn_ref, i, te, seq)
        cpos = pos[HALO:HALO + te]
        for g, w in enumerate(POOL_WINDOWS):
            ls = slice(g * POOL_G, (g + 1) * POOL_G)
            xg = xe[:, ls]
            wg = w_ref[g]
            win = _window_sum(xg, g + 1, n, False)[HALO:HALO + te]
            dlt = (win / _window_count(cpos, w, seq) - xg[HALO:HALO + te]).astype(BF16)
            z = _dot(dlt, wg)
            dpg = de[:, ls]
            dsc_ref[:, ls] += jnp.sum(dpg[HALO:HALO + te] * z, axis=0, keepdims=True)
            dz = (dpg * sc_ref[:, ls]).astype(BF16)
            dw_ref[g] += _dot_tn(dlt, dz[HALO:HALO + te])
            dd = _dot_nt(dz, wg)
            back = _window_sum(dd / _window_count(pos, w, seq), g + 1, n, True)
            du_ref[:, ls] = (back[HALO:HALO + te] - dd[HALO:HALO + te]).astype(BF16)

    return pl.pallas_call(
        body, name=name, grid=(seq // te,),
        in_specs=_halo_specs(te, seq, 3, POOL_W) + _halo_specs(te, seq, 1, POOL_W)
                 + [pl.BlockSpec((4, POOL_G, POOL_G), lambda i: (0, 0, 0)), pl.BlockSpec((1, POOL_W), lambda i: (0, 0))],
        out_specs=[pl.BlockSpec((te, POOL_W), lambda i: (i, 0)),
                   pl.BlockSpec((4, POOL_G, POOL_G), lambda i: (0, 0, 0)), pl.BlockSpec((1, POOL_W), lambda i: (0, 0))],
        out_shape=[jax.ShapeDtypeStruct((seq, POOL_W), BF16), jax.ShapeDtypeStruct((4, POOL_G, POOL_G), F32),
                   jax.ShapeDtypeStruct((1, POOL_W), F32)],
        compiler_params=_params(("arbitrary",)),
    )(qkvu, qkvu, qkvu, dmix, dmix, dmix, pool_w, pool_scale)


def _shifted(z, zprev_row, znext_row, te):
    rows = lax.broadcasted_iota(jnp.int32, (te, 1), 0)
    zp = jnp.where(rows == 0, zprev_row, pltpu.roll(z, 1, 0))
    zn = jnp.where(rows == te - 1, znext_row, pltpu.roll(z, te - 1, 0))
    return zp, zn


def _edge_rows(prev_ref, next_ref, i, nt):
    p = prev_ref[HALO - 1:HALO, :].astype(F32)
    q = next_ref[0:1, :].astype(F32)
    return jnp.where(i == 0, 0.0, p), jnp.where(i == nt - 1, 0.0, q)


def conv_fwd(proj, conv_w, name):
    seq = proj.shape[0]
    te = _tile(seq, (512, 256))
    nt = seq // te

    def body(bg_ref, cp_ref, cc_ref, cn_ref, xp_ref, xc_ref, xn_ref, w_ref, o_ref):
        i = pl.program_id(0)
        z = cc_ref[...].astype(F32) * xc_ref[...].astype(F32)
        cpr, cnr = _edge_rows(cp_ref, cn_ref, i, nt)
        xpr, xnr = _edge_rows(xp_ref, xn_ref, i, nt)
        zp, zn = _shifted(z, cpr * xpr, cnr * xnr, te)
        y = zp * w_ref[0:1, :] + z * w_ref[1:2, :] + zn * w_ref[2:3, :]
        o_ref[...] = (bg_ref[...].astype(F32) * y).astype(BF16)

    return pl.pallas_call(
        body, name=name, grid=(nt,),
        in_specs=[pl.BlockSpec((te, D), lambda i: (i, 0))] + _halo_specs(te, seq, 1, D) + _halo_specs(te, seq, 2, D)
                 + [pl.BlockSpec((3, D), lambda i: (0, 0))],
        out_specs=pl.BlockSpec((te, D), lambda i: (i, 0)),
        out_shape=jax.ShapeDtypeStruct((seq, D), BF16),
        compiler_params=_params(("parallel",)),
    )(proj, proj, proj, proj, proj, proj, proj, conv_w)


def conv_bwd(proj, dgm, conv_w, name):
    seq = proj.shape[0]
    te = _tile(seq, (512, 256))
    nt = seq // te

    def body(bp_ref, bc_ref, bn_ref, cp_ref, cc_ref, cn_ref, xp_ref, xc_ref, xn_ref, gp_ref, gc_ref, gn_ref, w_ref,
             dbg_ref, dcg_ref, dxin_ref, dw_ref):
        i = pl.program_id(0)

        @pl.when(i == 0)
        def _():
            dw_ref[...] = jnp.zeros_like(dw_ref)

        bg = bc_ref[...].astype(F32)
        cg = cc_ref[...].astype(F32)
        xin = xc_ref[...].astype(F32)
        dg = gc_ref[...].astype(F32)
        z = cg * xin
        cpr, cnr = _edge_rows(cp_ref, cn_ref, i, nt)
        xpr, xnr = _edge_rows(xp_ref, xn_ref, i, nt)
        zp, zn = _shifted(z, cpr * xpr, cnr * xnr, te)
        w0, w1, w2 = w_ref[0:1, :], w_ref[1:2, :], w_ref[2:3, :]
        y = zp * w0 + z * w1 + zn * w2
        dbg_ref[...] = (dg * y).astype(BF16)
        dy = dg * bg
        dw_ref[0:1, :] += jnp.sum(dy * zp, axis=0, keepdims=True)
        dw_ref[1:2, :] += jnp.sum(dy * z, axis=0, keepdims=True)
        dw_ref[2:3, :] += jnp.sum(dy * zn, axis=0, keepdims=True)
        bpr, bnr = _edge_rows(bp_ref, bn_ref, i, nt)
        gpr, gnr = _edge_rows(gp_ref, gn_ref, i, nt)
        dyp, dyn = _shifted(dy, bpr * gpr, bnr * gnr, te)
        dz = dyn * w0 + dy * w1 + dyp * w2
        dcg_ref[...] = (dz * xin).astype(BF16)
        dxin_ref[...] = (dz * cg).astype(BF16)

    row = pl.BlockSpec((te, D), lambda i: (i, 0))
    return pl.pallas_call(
        body, name=name, grid=(nt,),
        in_specs=_halo_specs(te, seq, 0, D) + _halo_specs(te, seq, 1, D) + _halo_specs(te, seq, 2, D)
                 + _halo_specs(te, seq, 0, D) + [pl.BlockSpec((3, D), lambda i: (0, 0))],
        out_specs=[row, row, row, pl.BlockSpec((3, D), lambda i: (0, 0))],
        out_shape=[jax.ShapeDtypeStruct((seq, D), BF16)] * 3 + [jax.ShapeDtypeStruct((3, D), F32)],
        compiler_params=_params(("arbitrary",)),
    )(proj, proj, proj, proj, proj, proj, proj, proj, proj, dgm, dgm, dgm, conv_w)


def _position():
    x, y, c = lax.axis_index("x"), lax.axis_index("y"), lax.axis_index("c")
    return x, y, c, 4 * x + 2 * y + c


def _peer(x, y, c, j):
    px = 1 - x if j & 4 else x
    py = 1 - y if j & 2 else y
    pc = 1 - c if j & 1 else c
    return (px, py, pc), 4 * px + 2 * py + pc


def small_allgather(v, name):
    rows, cols = v.shape

    def body(v_ref, o_ref, send_sems, recv_sems, local_sem):
        x, y, c, me = _position()
        mine = pltpu.make_async_copy(v_ref, o_ref.at[me], local_sem)
        mine.start()
        sends = []
        for j in range(1, N_DEV):
            peer, _ = _peer(x, y, c, j)
            cp = pltpu.make_async_remote_copy(src_ref=v_ref, dst_ref=o_ref.at[me], send_sem=send_sems.at[j - 1],
                                              recv_sem=recv_sems.at[j - 1], device_id=peer, device_id_type=MESH_ID)
            cp.start()
            sends.append(cp)
        for j in range(1, N_DEV):
            peer, pid = _peer(x, y, c, j)
            pltpu.make_async_remote_copy(src_ref=v_ref, dst_ref=o_ref.at[pid], send_sem=send_sems.at[j - 1],
                                         recv_sem=recv_sems.at[j - 1], device_id=peer,
                                         device_id_type=MESH_ID).wait_recv()
        for cp in sends:
            cp.wait_send()
        mine.wait()

    return pl.pallas_call(
        body, name=name,
        out_shape=jax.ShapeDtypeStruct((N_DEV, rows, cols), v.dtype),
        in_specs=[pl.BlockSpec(memory_space=pltpu.VMEM)],
        out_specs=pl.BlockSpec(memory_space=pltpu.VMEM),
        scratch_shapes=[pltpu.SemaphoreType.DMA((N_DEV - 1,)), pltpu.SemaphoreType.DMA((N_DEV - 1,)),
                        pltpu.SemaphoreType.DMA],
        compiler_params=pltpu.CompilerParams(vmem_limit_bytes=VMEM_LIMIT),
    )(v)


class Exchange:
    def __init__(self, kind, arrays):
        self.kind, self.arrays, self.n = kind, list(arrays), len(arrays)
        n = self.n
        if kind == "gather":
            self.out_shapes = [jax.ShapeDtypeStruct((N_DEV,) + a.shape, a.dtype) for a in self.arrays]
        else:
            self.out_shapes = [jax.ShapeDtypeStruct(a.shape, a.dtype) for a in self.arrays]
        self.scratch = [pltpu.SemaphoreType.DMA((7 * n,)), pltpu.SemaphoreType.DMA((7 * n,)),
                        pltpu.SemaphoreType.DMA((n,))]

    def _gather_copies(self, ins, outs, sems):
        send_sems, recv_sems, local_sems = sems
        x, y, c, me = _position()
        chips = [(1 - x, y), (x, 1 - y), (1 - x, 1 - y)]

        def blk(k, px, py, pc):
            return outs[k].at[4 * px + 2 * py + pc]

        def copy(k, slot, block, to, src=None):
            return pltpu.make_async_remote_copy(
                src_ref=blk(k, *block) if src is None else src, dst_ref=blk(k, *block),
                send_sem=send_sems.at[k * 7 + slot], recv_sem=recv_sems.at[k * 7 + slot],
                device_id=to, device_id_type=MESH_ID)

        mine = [pltpu.make_async_copy(ins[k], blk(k, x, y, c), local_sems.at[k]) for k in range(self.n)]
        first = []
        for k in range(self.n):
            first.append(copy(k, 0, (x, y, c), (x, y, 1 - c), src=ins[k]))
            first += [copy(k, 1 + j, (x, y, c), (*chip, c), src=ins[k]) for j, chip in enumerate(chips)]
        return (x, y, c), chips, copy, mine, first

    def start(self, ins, outs, sems):
        if self.kind == "gather":
            _, _, _, mine, first = self._gather_copies(ins, outs, sems)
            for cp in mine + first:
                cp.start()
        else:
            for cp in self._scatter_copies(ins, outs, sems, False):
                cp.start()

    def finish(self, ins, outs, sems):
        if self.kind == "gather":
            (x, y, c), chips, copy, mine, first = self._gather_copies(ins, outs, sems)
            passed = []
            for j, chip in enumerate(chips):
                for k in range(self.n):
                    copy(k, 1 + j, (*chip, c), (x, y, c)).wait_recv()
                    cp = copy(k, 4 + j, (*chip, c), (x, y, 1 - c))
                    cp.start()
                    passed.append(cp)
            for k in range(self.n):
                copy(k, 0, (x, y, 1 - c), (x, y, c)).wait_recv()
                for j, chip in enumerate(chips):
                    copy(k, 4 + j, (*chip, 1 - c), (x, y, c)).wait_recv()
            for cp in first + passed:
                cp.wait_send()
            for cp in mine:
                cp.wait()
        else:
            for cp in self._scatter_copies(ins, outs, sems, True):
                cp.wait_recv()
            copies = self._scatter_copies(ins, outs, sems, False)
            for cp in copies[self.n:]:
                cp.wait_send()
            for cp in copies[:self.n]:
                cp.wait()

    def _scatter_copies(self, ins, outs, sems, arrivals):
        send_sems, recv_sems, local_sems = sems
        x, y, c, me = _position()
        out = []
        if not arrivals:
            out = [pltpu.make_async_copy(ins[k].at[me], outs[k].at[me], local_sems.at[k]) for k in range(self.n)]
        for j in range(1, N_DEV):
            peer, pid = _peer(x, y, c, j)
            for k in range(self.n):
                out.append(pltpu.make_async_remote_copy(
                    src_ref=ins[k].at[pid], dst_ref=outs[k].at[pid if arrivals else me],
                    send_sem=send_sems.at[k * 7 + j - 1], recv_sem=recv_sems.at[k * 7 + j - 1],
                    device_id=peer, device_id_type=MESH_ID))
        return out


def sum_devices(v, name):
    _, rows, cols = v.shape

    def body(v_ref, o_ref):
        acc = v_ref[0]
        for p in range(1, N_DEV):
            acc = acc + v_ref[p]
        o_ref[...] = acc

    return pl.pallas_call(
        body, name=name, out_shape=jax.ShapeDtypeStruct((rows, cols), F32),
        compiler_params=pltpu.CompilerParams(vmem_limit_bytes=VMEM_LIMIT),
    )(v)


def _silu(x):
    return x * _sigmoid(x)


def adaln_fwd(cm, mod_w, mod_b_cols, name):
    cols = mod_w.shape[2]

    def body(c_ref, w_ref, b_ref, o_ref):
        o_ref[0] = jnp.dot(_silu(c_ref[...]), w_ref[0], precision=HI, preferred_element_type=F32) + b_ref[0]

    return pl.pallas_call(
        body, name=name, grid=(2,),
        in_specs=[pl.BlockSpec((16, D), lambda l: (0, 0)), pl.BlockSpec((1, D, cols), lambda l: (l, 0, 0)),
                  pl.BlockSpec((1, 1, cols), lambda l: (l, 0, 0))],
        out_specs=pl.BlockSpec((1, 16, cols), lambda l: (l, 0, 0)),
        out_shape=jax.ShapeDtypeStruct((2, 16, cols), F32),
        compiler_params=_params(("parallel",)),
    )(cm, mod_w, mod_b_cols)


def adaln_bwd(cm_t, mod_w, dm_t, name):
    cols = mod_w.shape[2]

    def body(c_ref, w_ref, lat_ref, ctx_ref, gw_ref, pc_ref):
        ctot = jnp.sum(ctx_ref[0], axis=0, keepdims=True)
        rows = lax.broadcasted_iota(jnp.int32, (8, 1), 0)
        g_hi = jnp.where(rows == 0, ctot, 0.0)
        g = jnp.concatenate([lat_ref[0], g_hi], axis=0)
        gw_ref[0] = jnp.dot(_silu(c_ref[...]), g, precision=HI, preferred_element_type=F32)
        pc_ref[0] = lax.dot_general(g_hi, w_ref[0], NT_DIMS, precision=HI, preferred_element_type=F32)

    return pl.pallas_call(
        body, name=name, grid=(2,),
        in_specs=[pl.BlockSpec((D, 16), lambda l: (0, 0)), pl.BlockSpec((1, D, cols), lambda l: (l, 0, 0)),
                  pl.BlockSpec((1, 8, cols), lambda l: (l, 0, 0)), pl.BlockSpec((1, 8, cols), lambda l: (l + 2, 0, 0))],
        out_specs=[pl.BlockSpec((1, D, cols), lambda l: (l, 0, 0)), pl.BlockSpec((1, 8, D), lambda l: (l, 0, 0))],
        out_shape=[jax.ShapeDtypeStruct((2, D, cols), F32), jax.ShapeDtypeStruct((2, 8, D), F32)],
        compiler_params=_params(("parallel",)),
    )(cm_t, mod_w, dm_t, dm_t)


def mod_b_grad(dm_t, name):
    width = dm_t.shape[2]
    tn = width // 8

    def body(d_ref, o_ref):
        s = jnp.concatenate([jnp.sum(d_ref[k], axis=0, keepdims=True) for k in range(4)]
                            + [jnp.zeros((4, tn), F32)], axis=0)
        o_ref[...] = s + pltpu.roll(s, 6, 0)

    return pl.pallas_call(
        body, name=name, grid=(8,),
        in_specs=[pl.BlockSpec((4, 8, tn), lambda j: (0, 0, j))],
        out_specs=pl.BlockSpec((8, tn), lambda j: (0, j)),
        out_shape=jax.ShapeDtypeStruct((8, width), F32),
        compiler_params=_params(("parallel",)),
    )(dm_t)


def adamw(w, m, v, name, g=None, recv=None):
    rows, cols = w.shape
    tr = _tile(rows, (256, 128, 64, 32, 16, 8))
    summed = recv is not None

    def body(w_ref, m_ref, v_ref, g_ref, go_ref, d_ref, mo_ref, vo_ref):
        if summed:
            gv = g_ref[0].astype(F32)
            for p in range(1, N_DEV):
                gv = gv + g_ref[p].astype(F32)
        else:
            gv = g_ref[...]
        mn = ADAM_B1 * m_ref[...] + (1.0 - ADAM_B1) * gv
        vn = ADAM_B2 * v_ref[...] + (1.0 - ADAM_B2) * (gv * gv)
        m_hat = mn / (1.0 - ADAM_B1 ** ADAM_STEP)
        v_hat = vn / (1.0 - ADAM_B2 ** ADAM_STEP)
        go_ref[...] = gv
        d_ref[...] = -ADAM_LR * (m_hat / (jnp.sqrt(v_hat) + ADAM_EPS) + ADAM_WD * w_ref[...])
        mo_ref[...] = mn
        vo_ref[...] = vn

    row = pl.BlockSpec((tr, cols), lambda i: (i, 0))
    gspec = pl.BlockSpec((N_DEV, tr, cols), lambda i: (0, i, 0)) if summed else row
    return pl.pallas_call(
        body, name=name, grid=(rows // tr,),
        in_specs=[row, row, row, gspec], out_specs=[row] * 4,
        out_shape=[jax.ShapeDtypeStruct((rows, cols), F32)] * 4,
        compiler_params=_params(("parallel",)),
    )(w, m, v, recv if summed else g)


def _ffn_fwd(h, hn, mods, w13, w2, base, tag, nxt=None, exchange=None):
    (p, u, s), exchanged = ffn_up(hn, w13, tag + "_up", exchange)
    if callable(w2):
        w2 = w2(exchanged)
    outs = mm_nn([s], w2, [0], tag + "_down", res=(h, mods, base + 2, 0.5), nxt=nxt)
    h_new, y = outs[0], outs[1]
    return h_new, (outs[2] if nxt else None), (h, hn, p, u, s, y), exchanged


COLUMN_CUT = ("w13", "ewi", "cwi")
GATHER_FIRST = ("w13_00",)
GATHER_IN_FFN = ("w2_00", "ewi", "ewo", "w13_01", "w2_01")
GATHER_IN_ATTN = ("w13_10", "w2_10", "cwi", "cwo", "w13_11", "w2_11")
SCATTER_IN_ATTN = ("w13_11", "w2_11", "cwi", "cwo", "w13_10", "w2_10", "w13_01", "w2_01", "ewo")
SCATTER_LAST = ("w13_00", "w2_00", "ewi")


def unpack_piece(p, g):
    if p.split("_")[0] in COLUMN_CUT:
        return g.transpose(1, 0, 2).reshape(g.shape[1], -1)
    return g.reshape(-1, g.shape[2])


def block_piece(p, full):
    if p.split("_")[0] in COLUMN_CUT:
        return full.reshape(full.shape[0], N_DEV, -1).transpose(1, 0, 2).astype(BF16)
    return full.reshape(N_DEV, -1, full.shape[1]).astype(BF16)


def _ffn_bwd(dy, dres, saved, mods, g, w13, w2, base, tag, prev=None, acc=None, exchange_of=None):
    h, hn, p, u, s, _ = saved
    ff = w2.shape[0]
    acc = acc or (None, None, None)
    da, db = mm_nt(dy, w2, tag + "_ds", dswiglu=(u, p))
    dw2 = mm_tn(s, dy, tag + "_dw2", acc=acc[2])
    dwa = mm_tn(hn, da, tag + "_dw13a", acc=acc[0])
    dwb = mm_tn(hn, db, tag + "_dw13b", acc=acc[1])
    exchange = exchange_of(dwa, dwb, dw2) if exchange_of else None
    outs = mm_nt_norm([da, db], w13, [0, ff], h, g, mods, base + 1, dres, tag + "_dhn", prev=prev, exchange=exchange)
    dh, dshift, dscale, dg = outs[:4]
    nprev = 2 if prev else 0
    return (dh, (dwa, dwb, dw2), dg, {base: dshift, base + 1: dscale}, tuple(outs[4:4 + nprev]),
            list(outs[4 + nprev:]))


def _mod_rows(parts):
    zero = jnp.zeros((1, D), F32)
    return jnp.concatenate([parts.get(k, zero) for k in range(N_MOD)], axis=0)


def local_step(x, ctx, ml, mc, wts, target, shards=None):
    wts = dict(wts)
    ng = wts["norm_g"]
    gvec = lambda l, k: ng[l, k][None, :]
    pool_w16 = wts["pool_w"].astype(BF16)
    grads = {}

    def gather(pieces):
        return Exchange("gather", [shards[p] for p in pieces]) if shards else None

    def arrived(pieces, results):
        for p, g in zip(pieces, results):
            wts[p] = unpack_piece(p, g)

    def ffn_grads(lf, f):
        grads["w13_" + lf] = jnp.concatenate([f[0], f[1]], axis=1)
        grads["w2_" + lf] = f[2]

    if shards:
        xh, got = normmod(x, gvec(0, 0), ml[0], 0, 1, "l0f1_norm", gather(GATHER_FIRST))
        arrived(GATHER_FIRST, got)
    else:
        xh = normmod(x, gvec(0, 0), ml[0], 0, 1, "l0f1_norm")
    ch = normmod(ctx, gvec(0, 0), mc[0], 0, 1, "l0f1c_norm")
    def w2_after_up(got):
        arrived(GATHER_IN_FFN, got)
        return wts["w2_00"]

    x1, xn, sv1, _ = _ffn_fwd(x, xh, ml[0], wts["w13_00"], w2_after_up, 0, "l0f1",
                              nxt=(gvec(0, 1), ml[0], 3, 4), exchange=gather(GATHER_IN_FFN))
    c1, cn, sv1c, _ = _ffn_fwd(ctx, ch, mc[0], wts["w13_00"], wts["w2_00"], 0, "l0f1c", nxt=(gvec(0, 1), mc[0], 3, 4))
    qkvu = mm_nn([xn], wts["ewi"], [0], "l0mix_in")
    qkvu_c = mm_nn([cn], wts["ewi"], [0], "l0mix_in_c")
    tab = bias_table(wts["rpb"])
    att, got = attn_fwd(qkvu, qkvu_c, tab, "l0_attn", gather(GATHER_IN_ATTN))
    arrived(GATHER_IN_ATTN, got)
    pool = pool_fwd(qkvu, pool_w16, wts["pool_scale"], "l0_pool")
    x2, ymix, xh = mm_nn([att, pool], wts["ewo"], [0, NA_W], "l0mix_out", res=(x1, ml[0], 5, 1.0),
                         nxt=(gvec(0, 2), ml[0], 6, 7))
    x3, xh, sv2, _ = _ffn_fwd(x2, xh, ml[0], wts["w13_01"], wts["w2_01"], 6, "l0f2", nxt=(gvec(1, 0), ml[1], 0, 1))

    x4, xn1, sv3, _ = _ffn_fwd(x3, xh, ml[1], wts["w13_10"], wts["w2_10"], 0, "l1f1", nxt=(gvec(1, 1), ml[1], 3, 4))
    proj = mm_nn([xn1], wts["cwi"], [0], "l1mix_in")
    gm = conv_fwd(proj, wts["conv_w"], "l1_conv")
    x5, ycv, xh = mm_nn([gm], wts["cwo"], [0], "l1mix_out", res=(x4, ml[1], 5, 1.0), nxt=(gvec(1, 2), ml[1], 6, 7))
    x6, _, sv4, _ = _ffn_fwd(x5, xh, ml[1], wts["w13_11"], wts["w2_11"], 6, "l1f2")

    dx6, loss, dgf, dy, dgate = loss_head(x6, wts["final_g"][None, :], target, (sv4[5], ml[1], 8, 0.5), "loss_head")
    dm1 = {8: dgate}
    dx5, dwf4, dg12, dm_f4, (dy, dgate), _ = _ffn_bwd(dy, dx6, sv4, ml[1], gvec(1, 2), wts["w13_11"], wts["w2_11"], 6,
                                                      "l1f2", prev=(ycv, ml[1], 5, 1.0))
    ffn_grads("11", dwf4)
    dm1.update({5: dgate, **dm_f4})
    dgm = mm_nt(dy, wts["cwo"], "l1mix_dgm")
    grads["cwo"] = mm_tn(gm, dy, "l1mix_dwo")
    dbg, dcg, dxin, dconv_w = conv_bwd(proj, dgm, wts["conv_w"], "l1_conv_bwd")
    grads["cwi"] = mm_tn_multi(xn1, [dbg, dcg, dxin], "l1mix_dwi")
    dx4, dsh, dsc, dg11, dy, dgate = mm_nt_norm([dbg, dcg, dxin], wts["cwi"], [0, D, 2 * D], x4, gvec(1, 1), ml[1], 4,
                                                dx5, "l1mix_dxn", prev=(sv3[5], ml[1], 2, 0.5))
    dm1.update({3: dsh, 4: dsc, 2: dgate})
    dx3, dwf3, dg10, dm_f3, (dy, dgate), _ = _ffn_bwd(dy, dx4, sv3, ml[1], gvec(1, 0), wts["w13_10"], wts["w2_10"], 0,
                                                      "l1f1", prev=(sv2[5], ml[0], 8, 0.5))
    ffn_grads("10", dwf3)
    dm1.update(dm_f3)
    dm0 = {8: dgate}

    dx2, dwf2, dg02, dm_f2, (dy, dgate), _ = _ffn_bwd(dy, dx3, sv2, ml[0], gvec(0, 2), wts["w13_01"], wts["w2_01"], 6,
                                                      "l0f2", prev=(ymix, ml[0], 5, 1.0))
    ffn_grads("01", dwf2)
    dm0.update({5: dgate, **dm_f2})
    dmix = mm_nt(dy, wts["ewo"], "l0mix_dmix")
    grads["ewo"] = jnp.concatenate([mm_tn(att, dy, "l0mix_dwo_att"), mm_tn(pool, dy, "l0mix_dwo_pool")], axis=0)
    scatter = Exchange("scatter", [block_piece(p, grads.pop(p)) for p in SCATTER_IN_ATTN]) if shards else None
    (dq, dk, dv, dkc, dvc, dtab), got = attn_bwd(qkvu, qkvu_c, tab, dmix, "l0_attn_bwd", scatter)
    recv = dict(zip(SCATTER_IN_ATTN, got))
    du, dpool_w, dpool_scale = pool_bwd(qkvu, dmix, pool_w16, wts["pool_scale"], "l0_pool_bwd")
    drpb = bias_table_bwd(dtab)
    dk16, dv16, dkc16, dvc16 = dk, dv, dkc.astype(BF16), dvc.astype(BF16)
    no_grad = jnp.zeros_like(dkc16)
    dewi_c = mm_tn_multi(cn, [no_grad, dkc16, dvc16, no_grad], "l0mix_dwi_c")
    grads["ewi"] = mm_tn_multi(xn, [dq, dk16, dv16, du], "l0mix_dwi", acc=dewi_c)
    dx1, dsh, dsc, dg01, dy, dgate = mm_nt_norm([dq, dk16, dv16, du], wts["ewi"], [0, NA_W, 2 * NA_W, 3 * NA_W], x1,
                                                gvec(0, 1), ml[0], 4, dx2, "l0mix_dxn", prev=(sv1[5], ml[0], 2, 0.5))
    dm0.update({3: dsh, 4: dsc, 2: dgate})
    dc1, dsh_c, dsc_c, dg01c, dy_c, dgate_c = mm_nt_norm([dkc16, dvc16], wts["ewi"], [NA_W, 2 * NA_W], c1, gvec(0, 1),
                                                         mc[0], 4, None, "l0mix_dxn_c", prev=(sv1c[5], mc[0], 2, 0.5))
    _, dwf1c, dg00c, dm_f1c, _, _ = _ffn_bwd(dy_c, dc1, sv1c, mc[0], gvec(0, 0), wts["w13_00"], wts["w2_00"], 0, "l0f1c")
    dmc0 = {3: dsh_c, 4: dsc_c, 2: dgate_c, **dm_f1c}

    def last_scatter(dwa, dwb, dw2):
        ffn_grads("00", (dwa, dwb, dw2))
        return Exchange("scatter", [block_piece(p, grads.pop(p)) for p in SCATTER_LAST]) if shards else None

    dx0, _, dg00, dm_f1, _, got = _ffn_bwd(dy, dx1, sv1, ml[0], gvec(0, 0), wts["w13_00"], wts["w2_00"], 0, "l0f1",
                                           acc=dwf1c, exchange_of=last_scatter)
    recv.update(zip(SCATTER_LAST, got))
    dm0.update(dm_f1)

    return {
        "loss": loss, "grad_x": dx0,
        "dml": jnp.stack([_mod_rows(dm0), _mod_rows(dm1)]),
        "dmc": jnp.stack([_mod_rows(dmc0), jnp.zeros((N_MOD, D), F32)]),
        "norm_g": jnp.concatenate([dg00 + dg00c, dg01 + dg01c, dg02, dg10, dg11, dg12], axis=0),
        "grads": grads, "recv": recv,
        "rpb": drpb, "pool_w": dpool_w, "pool_scale": dpool_scale, "conv_w": dconv_w, "final_g": dgf,
    }


def _rows_of(v, nrows):
    flat = v.reshape(-1)
    return jnp.pad(flat, (0, nrows * D - flat.shape[0])).reshape(nrows, D)


def kernel(x, c, ctx, c_ctx, mod_w, mod_b, norm_g, ffn_w13, ffn_w2, even_w_in, even_w_out, na_rpb, pool_w, pool_scale, conv_w_in, conv_w, conv_w_out, final_g, loss_target, m_c_ctx, m_mod_w, m_mod_b, m_norm_g, m_ffn_w13, m_ffn_w2, m_even_w_in, m_even_w_out, m_na_rpb, m_pool_w, m_pool_scale, m_conv_w_in, m_conv_w, m_conv_w_out, m_final_g, v_c_ctx, v_mod_w, v_mod_b, v_norm_g, v_ffn_w13, v_ffn_w2, v_even_w_in, v_even_w_out, v_na_rpb, v_pool_w, v_pool_scale, v_conv_w_in, v_conv_w, v_conv_w_out, v_final_g):
    me = 4 * lax.axis_index("x") + 2 * lax.axis_index("y") + lax.axis_index("c")
    ff = ffn_w2.shape[2] * N_DEV
    w13c = ffn_w13.shape[3]
    w2r = ffn_w2.shape[2]
    mcols = mod_w.shape[2]
    gcols = norm_g.shape[2]

    big = {"w13": ffn_w13.reshape(4 * D, w13c), "w2": ffn_w2.reshape(4 * w2r, D), "ewi": even_w_in[0],
           "ewo": even_w_out[0], "cwi": conv_w_in[0], "cwo": conv_w_out[0]}
    names = list(big)
    shards = {"ewi": even_w_in[0].astype(BF16), "ewo": even_w_out[0].astype(BF16),
              "cwi": conv_w_in[0].astype(BF16), "cwo": conv_w_out[0].astype(BF16)}
    for l in range(2):
        for f in range(2):
            shards["w13_%d%d" % (l, f)] = ffn_w13[l, f].astype(BF16)
            shards["w2_%d%d" % (l, f)] = ffn_w2[l, f].astype(BF16)
    wts = {}

    c_all = small_allgather(jnp.pad(c, ((0, 7), (0, 0))), "cond_allgather")[:, 0, :]
    cm = jnp.concatenate([c_all, c_ctx[None, :], jnp.zeros((7, D), F32)], axis=0)
    mod_b_cols = lax.dynamic_slice(mod_b, (0, me * mcols), (2, mcols))[:, None, :]
    m_cols = adaln_fwd(cm, mod_w, mod_b_cols, "adaln_fwd")
    m_all = small_allgather(m_cols.reshape(32, mcols), "mod_allgather")
    m_full = m_all.reshape(N_DEV, 2, 16, mcols).transpose(1, 2, 0, 3).reshape(2, 16, N_MOD * D)
    ml = lax.dynamic_slice(m_full, (0, me, 0), (2, 1, N_MOD * D)).reshape(2, N_MOD, D)
    mc = m_full[:, 8].reshape(2, N_MOD, D)

    full_norm_g = small_allgather(_rows_of(norm_g, 8), "norm_g_allgather")[:, 0, :2 * 3 * gcols]
    full_norm_g = full_norm_g.reshape(N_DEV, 2, 3, gcols).transpose(1, 2, 0, 3).reshape(2, 3, D)
    full_conv_w = small_allgather(_rows_of(conv_w, 8), "conv_w_allgather")[:, 0, :3 * gcols]
    full_conv_w = full_conv_w.reshape(N_DEV, 3, gcols).transpose(1, 0, 2).reshape(3, D)
    wts.update(norm_g=full_norm_g, conv_w=full_conv_w, rpb=na_rpb[0], pool_w=pool_w[0], pool_scale=pool_scale,
               final_g=final_g)
    out = local_step(x[0], ctx[0], ml, mc, wts, loss_target[0], shards)

    dm_pack = jnp.concatenate([out["dml"].reshape(2, N_MOD * D), out["dmc"].reshape(2, N_MOD * D),
                               jnp.zeros((4, N_MOD * D), F32)], axis=0)
    dm_t = small_allgather(dm_pack, "dmod_allgather").transpose(1, 0, 2)[:4]
    dm_cols = lax.dynamic_slice(dm_t, (0, 0, me * mcols), (4, N_DEV, mcols))
    g_mod_w, pc = adaln_bwd(cm.T, mod_w, dm_cols, "adaln_bwd")
    g_mod_b = mod_b_grad(dm_t, "mod_b_grad")[:2]

    pack = jnp.concatenate([_rows_of(t, 8) for t in (
        out["norm_g"], out["conv_w"], out["final_g"], pc[0, :1] + pc[1, :1], out["pool_scale"], out["loss"],
        out["rpb"])] + [_rows_of(out["pool_w"], 64)], axis=0)
    small = sum_devices(small_allgather(pack, "small_grads_allgather"), "small_grads_sum")
    g_norm_g = lax.dynamic_slice(small[0:6].reshape(2, 3, D), (0, 0, me * gcols), (2, 3, gcols))
    g_conv_w = lax.dynamic_slice(small[8:11], (0, me * gcols), (3, gcols))[None]
    g_final_g = small[16]
    sg = _sigmoid(c_ctx)
    g_c_ctx = small[24] * (sg * (1.0 + c_ctx * (1.0 - sg)))
    g_pool_scale = small[32:33, :POOL_W]
    loss = small[40, 0]
    g_rpb = small[48:52].reshape(-1)[:na_rpb.size].reshape(na_rpb.shape)
    g_pool_w = small[56:120].reshape(pool_w.shape)

    pieces = out["recv"]
    lf = ("00", "01", "10", "11")
    recv = {"w13": jnp.concatenate([pieces["w13_" + t] for t in lf], axis=1),
            "w2": jnp.concatenate([pieces["w2_" + t] for t in lf], axis=1),
            "ewi": pieces["ewi"], "ewo": pieces["ewo"], "cwi": pieces["cwi"], "cwo": pieces["cwo"]}

    moments = {"w13": (m_ffn_w13, v_ffn_w13), "w2": (m_ffn_w2, v_ffn_w2), "ewi": (m_even_w_in, v_even_w_in),
               "ewo": (m_even_w_out, v_even_w_out), "cwi": (m_conv_w_in, v_conv_w_in),
               "cwo": (m_conv_w_out, v_conv_w_out)}
    orig = {"w13": ffn_w13, "w2": ffn_w2, "ewi": even_w_in, "ewo": even_w_out, "cwi": conv_w_in, "cwo": conv_w_out}
    upd = {}
    for k in names:
        shp2 = big[k].shape
        res = adamw(big[k], moments[k][0].reshape(shp2), moments[k][1].reshape(shp2), "adamw_" + k, recv=recv[k])
        upd[k] = [r.reshape(orig[k].shape) for r in res]
    shp2 = (2 * D, mcols)
    upd["mod_w"] = [r.reshape(mod_w.shape) for r in adamw(mod_w.reshape(shp2), m_mod_w.reshape(shp2),
                                                          v_mod_w.reshape(shp2), "adamw_mod_w",
                                                          g=g_mod_w.reshape(shp2))]

    smalls = [("c_ctx", c_ctx, m_c_ctx, v_c_ctx, g_c_ctx, 8), ("mod_b", mod_b, m_mod_b, v_mod_b, g_mod_b, 24),
              ("norm_g", norm_g, m_norm_g, v_norm_g, g_norm_g, 8), ("rpb", na_rpb, m_na_rpb, v_na_rpb, g_rpb, 8),
              ("pool_w", pool_w, m_pool_w, v_pool_w, g_pool_w, 64),
              ("pool_scale", pool_scale, m_pool_scale, v_pool_scale, g_pool_scale, 8),
              ("conv_w", conv_w, m_conv_w, v_conv_w, g_conv_w, 8), ("final_g", final_g, m_final_g, v_final_g, g_final_g, 8)]
    packed = [jnp.concatenate([_rows_of(s[col], s[5]) for s in smalls], axis=0) for col in (1, 2, 3, 4)]
    res = adamw(packed[0], packed[1], packed[2], "adamw_small", g=packed[3])
    row = 0
    for name, w, _, _, _, nrows in smalls:
        upd[name] = [r[row:row + nrows].reshape(-1)[:w.size].reshape(w.shape) for r in res]
        row += nrows

    order = ["c_ctx", "mod_w", "mod_b", "norm_g", "w13", "w2", "ewi", "ewo", "rpb", "pool_w", "pool_scale", "cwi",
             "conv_w", "cwo", "final_g"]
    grad_x = out["grad_x"][None]
    return (loss, grad_x, *[upd[k][0] for k in order], *[upd[k][1] for k in order], *[upd[k][2] for k in order],
            *[upd[k][3] for k in order])
```
